```python
import math
import jax, jax.numpy as jnp
from jax import lax
import numpy as np

D_MODEL = 1024
BATCH = 8
SEQ = 4096
DEPTH = 1

A_HEADS = 8
A_HEAD_DIM = 64
IDX_HEADS = 8
IDX_DIM = 32
TOPK_MAX = 256
Q_BLOCK = 128
REL_BUCKETS = 32
REL_MAX_DIST = 128
B_HEADS = 4
B_KEY_DIM = 128
B_VAL_DIM = 128
CHUNK = 64
N_EXPERTS = 32
TOP_K = 4
D_FF = 1024
SWIGLU_LIMIT = 7.0
SWIGLU_ALPHA = 1.702
EXPERT_BLOCK = 256
EPS = 1e-6

A_WIDTH = A_HEADS * A_HEAD_DIM
B_WIDTH = B_HEADS * B_VAL_DIM
IN_SPLITS = (A_WIDTH, A_WIDTH, A_WIDTH,
             IDX_HEADS * IDX_DIM, IDX_HEADS, IDX_DIM,
             B_HEADS * B_KEY_DIM, B_HEADS * B_KEY_DIM,
             B_WIDTH, B_WIDTH,
             D_MODEL, D_MODEL)
IN_WIDTH = sum(IN_SPLITS)

kernel_name = "hybrid_dsa_hgrn2_moe_block"


def rmsnorm(x, gain):
    xf = x.astype(jnp.float32)
    y = xf * lax.rsqrt(jnp.mean(xf * xf, axis=-1, keepdims=True) + EPS)
    return (y * gain.astype(jnp.float32)).astype(x.dtype)


def t5_bucket(dist):
    n = jnp.maximum(dist, 0)
    max_exact = REL_BUCKETS // 2
    nf = jnp.maximum(n, 1).astype(jnp.float32)
    large = max_exact + (jnp.log(nf / max_exact) / math.log(REL_MAX_DIST / max_exact)
                         * (REL_BUCKETS - max_exact)).astype(jnp.int32)
    large = jnp.minimum(large, REL_BUCKETS - 1)
    return jnp.where(n < max_exact, n, large)


def dsa_sparse_attention(q, k, v, q_idx, w_idx, k_idx, rel_bias):
    B, S, H, Dh = q.shape
    topk = min(TOPK_MAX, S // 4)
    n_blocks = S // Q_BLOCK
    scale = A_HEAD_DIM ** -0.5
    idx_scale = IDX_DIM ** -0.5
    w_scaled = w_idx.astype(jnp.float32) * (IDX_HEADS ** -0.5)
    key_pos = jnp.arange(S, dtype=jnp.int32)
    bidx = jnp.arange(B)[:, None, None]

    def block(i):
        start = i * Q_BLOCK
        qb = lax.dynamic_slice_in_dim(q, start, Q_BLOCK, axis=1)
        qib = lax.dynamic_slice_in_dim(q_idx, start, Q_BLOCK, axis=1)
        wb = lax.dynamic_slice_in_dim(w_scaled, start, Q_BLOCK, axis=1)
        qpos = start + jnp.arange(Q_BLOCK, dtype=jnp.int32)
        s_h = jnp.einsum('bqhd,bsd->bqhs', qib, k_idx,
                         preferred_element_type=jnp.float32) * idx_scale
        score = jnp.einsum('bqhs,bqh->bqs', jax.nn.relu(s_h), wb)
        causal = key_pos[None, :] <= qpos[:, None]
        score = jnp.where(causal[None], score, -jnp.inf)
        _, sel = lax.top_k(score, topk)
        kg = k[bidx, sel]
        vg = v[bidx, sel]
        logits = jnp.einsum('bqhd,bqkhd->bqhk', qb, kg,
                            preferred_element_type=jnp.float32) * scale
        dist = qpos[None, :, None] - sel
        bias = rel_bias[t5_bucket(dist)].astype(jnp.float32)
        logits = logits + jnp.transpose(bias, (0, 1, 3, 2))
        logits = jnp.where((dist >= 0)[:, :, None, :], logits, -jnp.inf)
        p = jax.nn.softmax(logits, axis=-1).astype(v.dtype)
        return jnp.einsum('bqhk,bqkhd->bqhd', p, vg)

    out = lax.map(block, jnp.arange(n_blocks))
    return jnp.transpose(out, (1, 0, 2, 3, 4)).reshape(B, S, H * Dh)


def hgrn2_chunkwise(q_raw, f_raw, i_in, lb):
    B, S, H, Dk = q_raw.shape
    Dv = i_in.shape[-1]
    n_chunks = S // CHUNK
    f = lb + (1.0 - lb) * jax.nn.sigmoid(f_raw.astype(jnp.float32))
    g = jnp.log(f)
    kk = 1.0 - f
    qq = jax.nn.silu(q_raw.astype(jnp.float32)) * (Dk ** -0.5)
    vv = i_in.astype(jnp.float32)

    def to_chunks(t):
        return t.reshape(B, n_chunks, CHUNK, H, t.shape[-1]).transpose(1, 0, 3, 2, 4)

    mask = jnp.tril(jnp.ones((CHUNK, CHUNK), dtype=bool))

    def step(state, xs):
        qc, kc, vc, gc = xs
        b = jnp.cumsum(gc, axis=2)
        o_inter = jnp.einsum('bhcd,bhde->bhce', qc * jnp.exp(b), state)
        rel = jnp.where(mask[None, None, :, :, None],
                        b[:, :, :, None, :] - b[:, :, None, :, :], -jnp.inf)
        attn = jnp.einsum('bhtd,bhsd,bhtsd->bhts', qc, kc, jnp.exp(rel))
        o_intra = jnp.einsum('bhts,bhse->bhte', attn, vc)
        b_last = b[:, :, -1:, :]
        new_state = (jnp.exp(b_last[:, :, 0, :])[..., None] * state
                     + jnp.einsum('bhsd,bhse->bhde', kc * jnp.exp(b_last - b), vc))
        return new_state, o_inter + o_intra

    s0 = jnp.zeros((B, H, Dk, Dv), jnp.float32)
    _, o = lax.scan(step, s0, (to_chunks(qq), to_chunks(kk), to_chunks(vv), to_chunks(g)))
    return o.transpose(1, 0, 3, 2, 4).reshape(B, S, H, Dv)


def moe_ffn(h, w_router, b_router, w_gu, b_gu, w_down, b_down):
    B, S, D = h.shape
    T = B * S
    xt = h.reshape(T, D)
    logits = (xt @ w_router).astype(jnp.float32) + b_router.astype(jnp.float32)
    top_val, top_idx = lax.top_k(logits, TOP_K)
    gates = jax.nn.softmax(top_val, axis=-1)
    A = T * TOP_K
    e_flat = top_idx.reshape(A)
    order = jnp.argsort(e_flat)
    e_sorted = e_flat[order]
    tok_sorted = (order // TOP_K).astype(jnp.int32)
    counts = jnp.bincount(e_flat, length=N_EXPERTS)
    starts = jnp.cumsum(counts) - counts
    padded = (counts + EXPERT_BLOCK - 1) // EXPERT_BLOCK * EXPERT_BLOCK
    pad_ends = jnp.cumsum(padded)
    pad_starts = pad_ends - padded
    dest = pad_starts[e_sorted] + (jnp.arange(A) - starts[e_sorted])
    n_blocks = -(-A // EXPERT_BLOCK) + N_EXPERTS
    P = n_blocks * EXPERT_BLOCK
    buf_tok = jnp.full((P,), T, dtype=jnp.int32).at[dest].set(tok_sorted)
    block_expert = jnp.minimum(
        jnp.searchsorted(pad_ends, jnp.arange(n_blocks) * EXPERT_BLOCK, side='right'),
        N_EXPERTS - 1)
    x_pad = jnp.concatenate([xt, jnp.zeros((1, D), xt.dtype)], axis=0)

    def expert_block(args):
        toks, e = args
        xb = x_pad[toks]
        gu = xb @ w_gu[e] + b_gu[e]
        gate, lin = gu[:, :D_FF], gu[:, D_FF:]
        gate = jnp.minimum(gate, SWIGLU_LIMIT)
        lin = jnp.clip(lin, -SWIGLU_LIMIT, SWIGLU_LIMIT)
        act = (lin + 1.0) * gate * jax.nn.sigmoid(SWIGLU_ALPHA * gate)
        return act @ w_down[e] + b_down[e]

    y_buf = lax.map(expert_block, (buf_tok.reshape(n_blocks, EXPERT_BLOCK), block_expert))
    y_buf = y_buf.reshape(P, D)
    y_assign = jnp.zeros((A, D), y_buf.dtype).at[order].set(y_buf[dest])
    y = jnp.einsum('tkd,tk->td', y_assign.reshape(T, TOP_K, D), gates.astype(y_buf.dtype))
    return y.reshape(B, S, D)


def setup_inputs(seed: int = 0) -> dict:
    key = jax.random.key(seed)
    ks = jax.random.split(key, 18)
    nrm = jax.random.normal
    f32 = jnp.float32
    return {
        "x": nrm(ks[0], (BATCH, SEQ, D_MODEL), f32),
        "w_in": nrm(ks[1], (DEPTH, D_MODEL, IN_WIDTH), f32) * D_MODEL ** -0.5,
        "w_up_a": nrm(ks[2], (DEPTH, A_WIDTH, D_MODEL), f32) * A_WIDTH ** -0.5,
        "w_up_b": nrm(ks[3], (DEPTH, B_WIDTH, D_MODEL), f32) * B_WIDTH ** -0.5,
        "w_out": nrm(ks[4], (DEPTH, D_MODEL, D_MODEL), f32) * D_MODEL ** -0.5,
        "norm_mix": 1.0 + 0.1 * nrm(ks[5], (DEPTH, D_MODEL), f32),
        "norm_ffn": 1.0 + 0.1 * nrm(ks[6], (DEPTH, D_MODEL), f32),
        "norm_final": 1.0 + 0.1 * nrm(ks[7], (D_MODEL,), f32),
        "hgrn_norm": 1.0 + 0.1 * nrm(ks[8], (DEPTH, B_HEADS, B_VAL_DIM), f32),
        "lb_logits": 0.5 * nrm(ks[9], (DEPTH + 1, B_HEADS * B_KEY_DIM), f32),
        "rel_bias": 0.5 * nrm(ks[10], (REL_BUCKETS, A_HEADS), f32),
        "w_router": nrm(ks[11], (DEPTH, D_MODEL, N_EXPERTS), f32) * D_MODEL ** -0.5,
        "b_router": 0.01 * nrm(ks[12], (DEPTH, N_EXPERTS), f32),
        "w_gu": nrm(ks[13], (DEPTH, N_EXPERTS, D_MODEL, 2 * D_FF), f32) * D_MODEL ** -0.5,
        "b_gu": 0.01 * nrm(ks[14], (DEPTH, N_EXPERTS, 2 * D_FF), f32),
        "w_down": nrm(ks[15], (DEPTH, N_EXPERTS, D_FF, D_MODEL), f32) * D_FF ** -0.5,
        "b_down": 0.01 * nrm(ks[16], (DEPTH, N_EXPERTS, D_MODEL), f32),
    }


def reference(x, w_in, w_up_a, w_up_b, w_out, norm_mix, norm_ffn, norm_final, hgrn_norm,
              lb_logits, rel_bias, w_router, b_router, w_gu, b_gu, w_down, b_down):
    B, S, _ = x.shape
    split_points = [int(p) for p in np.cumsum(IN_SPLITS)[:-1]]
    lb_all = jnp.cumsum(jax.nn.softmax(lb_logits.astype(jnp.float32), axis=0), axis=0)
    for l in range(DEPTH):
        h = rmsnorm(x, norm_mix[l])
        proj = h @ w_in[l]
        (aq, ak, av, iq, iw, ik, bq, bf, bi, bg, ga, gb) = jnp.split(proj, split_points, axis=-1)
        y_a = dsa_sparse_attention(
            aq.reshape(B, S, A_HEADS, A_HEAD_DIM),
            ak.reshape(B, S, A_HEADS, A_HEAD_DIM),
            av.reshape(B, S, A_HEADS, A_HEAD_DIM),
            iq.reshape(B, S, IDX_HEADS, IDX_DIM), iw, ik, rel_bias)
        lb = lb_all[l].reshape(B_HEADS, B_KEY_DIM)
        o_b = hgrn2_chunkwise(bq.reshape(B, S, B_HEADS, B_KEY_DIM),
                              bf.reshape(B, S, B_HEADS, B_KEY_DIM),
                              bi.reshape(B, S, B_HEADS, B_VAL_DIM), lb)
        o_b = rmsnorm(o_b, hgrn_norm[l]) * jax.nn.silu(
            bg.reshape(B, S, B_HEADS, B_VAL_DIM).astype(jnp.float32))
        y_b = o_b.reshape(B, S, B_WIDTH).astype(x.dtype)
        merged = (jax.nn.sigmoid(ga) * (y_a @ w_up_a[l])
                  + jax.nn.sigmoid(gb) * (y_b @ w_up_b[l]))
        x = x + merged @ w_out[l]
        x = x + moe_ffn(rmsnorm(x, norm_ffn[l]), w_router[l], b_router[l],
                        w_gu[l], b_gu[l], w_down[l], b_down[l])
    return rmsnorm(x, norm_final)
```

```python
import functools
import math

import numpy as np
import jax
import jax.numpy as jnp
from jax import lax
from jax.experimental import pallas as pl
from jax.experimental.pallas import tpu as pltpu

A_HEADS = 8
A_HEAD_DIM = 64
IDX_HEADS = 8
IDX_DIM = 32
TOPK_MAX = 256
REL_BUCKETS = 32
REL_MAX_DIST = 128
B_HEADS = 4
B_KEY_DIM = 128
B_VAL_DIM = 128
N_EXPERTS = 32
TOP_K = 4
SWIGLU_LIMIT = 7.0
SWIGLU_ALPHA = 1.702
EPS = 1e-6

A_WIDTH = A_HEADS * A_HEAD_DIM
B_WIDTH = B_HEADS * B_VAL_DIM
IDX_WIDTH = IDX_HEADS * IDX_DIM

V7X_LANES = 128
V7X_VMEM_LIMIT_BYTES = 56 * 1024 * 1024

PROJ_ROWS = 512
ATT_Q = 256
HGRN_ROWS = 512
HGRN_CHUNK = 32
HGRN_SAFE_DECAY = 60.0
ROUTE_ROWS = 512
EXPERT_ROWS = 256
DISPATCH_ROWS = 512
DMA_WINDOW = 64
MASK_NEG = -1e30
BISECT_FAST_ITERS = 26


def _cparams(dims):
    return pltpu.CompilerParams(dimension_semantics=dims, vmem_limit_bytes=V7X_VMEM_LIMIT_BYTES)


def _rms(x, gain):
    return x * lax.rsqrt(jnp.mean(x * x, axis=-1, keepdims=True) + EPS) * gain


def _sigmoid(x):
    return 1.0 / (1.0 + jnp.exp(-x))


def _inproj_kernel(x_ref, g_ref, wq_ref, wv_ref, wiq_ref, wiw_ref, wkT_ref, wikT_ref,
                   wb_ref, wg_ref,
                   q_ref, v_ref, iq_ref, iw_ref, kT_ref, ikT_ref, bq_ref, bf_ref, bi_ref, bg_ref,
                   ga_ref, gb_ref):
    x = x_ref[...]
    hn = _rms(x, g_ref[...]).astype(jnp.bfloat16)

    def mm(w_ref):
        return jnp.dot(hn, w_ref[...], preferred_element_type=jnp.float32)

    def mm_t(w_ref):
        return lax.dot_general(w_ref[...], hn, (((1,), (1,)), ((), ())),
                               preferred_element_type=jnp.float32)

    q_ref[...] = (mm(wq_ref) * (A_HEAD_DIM ** -0.5)).astype(jnp.bfloat16)
    v_ref[...] = mm(wv_ref).astype(jnp.bfloat16)
    iq_ref[...] = mm(wiq_ref).astype(jnp.bfloat16)
    iw_ref[...] = mm(wiw_ref) * ((IDX_HEADS * IDX_DIM) ** -0.5)
    kT_ref[0] = mm_t(wkT_ref).astype(jnp.bfloat16)
    ikT_ref[0] = mm_t(wikT_ref).astype(jnp.bfloat16)
    hb = mm(wb_ref)
    bq_ref[...] = hb[:, 0 * B_WIDTH:1 * B_WIDTH]
    bf_ref[...] = hb[:, 1 * B_WIDTH:2 * B_WIDTH]
    bi_ref[...] = hb[:, 2 * B_WIDTH:3 * B_WIDTH]
    bg_ref[...] = hb[:, 3 * B_WIDTH:4 * B_WIDTH]
    d = ga_ref.shape[-1]
    hg = mm(wg_ref)
    ga_ref[...] = _sigmoid(hg[:, :d]).astype(jnp.bfloat16)
    gb_ref[...] = _sigmoid(hg[:, d:]).astype(jnp.bfloat16)


def _inproj(x2, gain, w_in, B, S):
    T, D = x2.shape
    R = min(PROJ_ROWS, S)
    nS = S // R
    o = np.cumsum((0, A_WIDTH, A_WIDTH, A_WIDTH, IDX_WIDTH, IDX_HEADS, IDX_DIM,
                   B_WIDTH, B_WIDTH, B_WIDTH, B_WIDTH, D, D))
    bf = jnp.bfloat16
    wq = w_in[:, o[0]:o[1]].astype(bf)
    wk = w_in[:, o[1]:o[2]]
    wv = w_in[:, o[2]:o[3]].astype(bf)
    wiq = w_in[:, o[3]:o[4]].astype(bf)
    wiw = w_in[:, o[4]:o[5]].astype(bf)
    wik = w_in[:, o[5]:o[6]]
    wb = w_in[:, o[6]:o[10]].astype(bf)
    wg = w_in[:, o[10]:o[12]].astype(bf)
    wkT = wk.T.astype(bf)
    wikT = jnp.tile(wik.T, (IDX_HEADS, 1)).astype(bf)

    def full(a):
        return pl.BlockSpec(a.shape, lambda b, i: (0,) * a.ndim)

    row = lambda n: pl.BlockSpec((R, n), lambda b, i: (b * nS + i, 0))
    colT = lambda n: pl.BlockSpec((1, n, R), lambda b, i: (b, 0, i))
    f32 = jnp.float32
    outs = [
        (jax.ShapeDtypeStruct((T, A_WIDTH), bf), row(A_WIDTH)),
        (jax.ShapeDtypeStruct((T, A_WIDTH), bf), row(A_WIDTH)),
        (jax.ShapeDtypeStruct((T, IDX_WIDTH), bf), row(IDX_WIDTH)),
        (jax.ShapeDtypeStruct((T, IDX_HEADS), f32), row(IDX_HEADS)),
        (jax.ShapeDtypeStruct((B, A_WIDTH, S), bf), colT(A_WIDTH)),
        (jax.ShapeDtypeStruct((B, IDX_WIDTH, S), bf), colT(IDX_WIDTH)),
        (jax.ShapeDtypeStruct((T, B_WIDTH), f32), row(B_WIDTH)),
        (jax.ShapeDtypeStruct((T, B_WIDTH), f32), row(B_WIDTH)),
        (jax.ShapeDtypeStruct((T, B_WIDTH), f32), row(B_WIDTH)),
        (jax.ShapeDtypeStruct((T, B_WIDTH), f32), row(B_WIDTH)),
        (jax.ShapeDtypeStruct((T, D), bf), row(D)),
        (jax.ShapeDtypeStruct((T, D), bf), row(D)),
    ]
    ins = [x2, gain.reshape(1, D), wq, wv, wiq, wiw, wkT, wikT, wb, wg]
    in_specs = [row(D)] + [full(a) for a in ins[1:]]
    return pl.pallas_call(
        _inproj_kernel,
        grid=(B, nS),
        in_specs=in_specs,
        out_specs=[s for _, s in outs],
        out_shape=[s for s, _ in outs],
        compiler_params=_cparams(("parallel", "parallel")),
        name="inproj",
    )(*ins)


def _t5_bucket_table(n):
    d = np.arange(n)
    max_exact = REL_BUCKETS // 2
    nf = np.maximum(d, 1).astype(np.float64)
    large = max_exact + (np.log(nf / max_exact) / math.log(REL_MAX_DIST / max_exact)
                         * (REL_BUCKETS - max_exact)).astype(np.int32)
    large = np.minimum(large, REL_BUCKETS - 1)
    return np.where(d < max_exact, d, large)


def _dsa_kernel(efar_ref, q_ref, kT_ref, v_ref, iq_ref, iw_ref, ikT_ref, enear_ref,
                o_ref, sc_ref, *, topk):
    TQ = q_ref.shape[0]
    KC = TQ
    i = pl.program_id(1)
    nch = i + 1
    q0 = i * TQ
    f32 = jnp.float32
    row_id = lax.broadcasted_iota(jnp.int32, (TQ, KC), 0)
    col_id = lax.broadcasted_iota(jnp.int32, (TQ, KC), 1)

    iq = iq_ref[...]
    iw = iw_ref[...]
    head_of_lane = lax.broadcasted_iota(jnp.int32, iq.shape, 1) // IDX_DIM
    iq_heads = [jnp.where(head_of_lane == h, iq, jnp.zeros_like(iq)) for h in range(IDX_HEADS)]

    def score_chunk(c, carry):
        rmin, rmax = carry
        k0 = pl.multiple_of(c * KC, KC)
        ik = ikT_ref[0, :, pl.ds(k0, KC)]
        acc = jnp.zeros((TQ, KC), f32)
        for h in range(IDX_HEADS):
            sh = jnp.dot(iq_heads[h], ik, preferred_element_type=f32)
            acc = acc + jnp.maximum(sh, 0.0) * iw[:, h:h + 1]
        valid = (k0 + col_id) <= (q0 + row_id)
        sc_ref[:, pl.ds(k0, KC)] = jnp.where(valid, acc, MASK_NEG)
        rmin = jnp.minimum(rmin, jnp.min(jnp.where(valid, acc, -MASK_NEG), axis=1, keepdims=True))
        rmax = jnp.maximum(rmax, jnp.max(jnp.where(valid, acc, MASK_NEG), axis=1, keepdims=True))
        return rmin, rmax

    rmin, rmax = lax.fori_loop(
        0, nch, score_chunk,
        (jnp.full((TQ, 1), -MASK_NEG, f32), jnp.full((TQ, 1), MASK_NEG, f32)))

    def fold_lanes(x):
        out = x[:, :V7X_LANES]
        for j in range(1, KC // V7X_LANES):
            out = out + x[:, j * V7X_LANES:(j + 1) * V7X_LANES]
        return out

    def count_where(pred_fn):
        def body(c, acc):
            k0 = pl.multiple_of(c * KC, KC)
            blk = sc_ref[:, pl.ds(k0, KC)]
            return acc + fold_lanes(jnp.where(pred_fn(blk), 1.0, 0.0))
        acc = lax.fori_loop(0, nch, body, jnp.zeros((TQ, V7X_LANES), f32))
        return jnp.sum(acc, axis=1, keepdims=True)

    def band_min_max(lo, hi):
        def body(c, carry):
            bmin, bmax = carry
            k0 = pl.multiple_of(c * KC, KC)
            blk = sc_ref[:, pl.ds(k0, KC)]
            bmin = jnp.minimum(bmin, jnp.min(jnp.where(blk >= lo, blk, -MASK_NEG), axis=1, keepdims=True))
            bmax = jnp.maximum(bmax, jnp.max(jnp.where(blk < hi, blk, MASK_NEG), axis=1, keepdims=True))
            return bmin, bmax
        return lax.fori_loop(0, nch, body,
                             (jnp.full((TQ, 1), -MASK_NEG, f32), jnp.full((TQ, 1), MASK_NEG, f32)))

    kf = float(topk)
    n_valid = (q0 + 1 + lax.broadcasted_iota(jnp.int32, (TQ, 1), 0)).astype(f32)
    c_top = count_where(lambda blk: blk >= rmax)
    top_tie = c_top >= kf
    lo0 = jnp.where(top_tie, rmax, rmin)
    cnt0 = jnp.where(top_tie, c_top, n_valid)
    done0 = jnp.where(top_tie | (cnt0 <= kf), 1.0, 0.0)

    def cond(st):
        it, lo, hi, cnt, done = st
        return jnp.min(done) < 0.5

    def body(st):
        it, lo, hi, cnt, done = st

        def plain(_):
            mid = lo + 0.5 * (hi - lo)
            stuck = (mid <= lo) | (mid >= hi)
            return lo, mid, jnp.where(stuck, 1.0, 0.0)

        def snapped(_):
            bmin, bmax = band_min_max(lo, hi)
            mid = bmin + 0.5 * (bmax - bmin)
            mid = jnp.where(mid <= bmin, bmax, mid)
            return bmin, mid, jnp.where(bmax <= bmin, 1.0, 0.0)

        lo_s, mid, tie_f = lax.cond(it < BISECT_FAST_ITERS, plain, snapped, None)
        tie = tie_f > 0.5
        active = done < 0.5
        lo_s = jnp.where(active, lo_s, lo)
        c = count_where(lambda blk: blk >= mid)
        feas = c >= kf
        move = active & jnp.logical_not(tie)
        lo_n = jnp.where(move & feas, mid, lo_s)
        cnt_n = jnp.where(move & feas, c, cnt)
        hi_n = jnp.where(move & jnp.logical_not(feas), mid, hi)
        done_n = jnp.where((active & tie) | (cnt_n <= kf), 1.0, done)
        return it + 1, lo_n, hi_n, cnt_n, done_n

    _, thr, _, _, _ = lax.while_loop(cond, body, (jnp.int32(0), lo0, rmax, cnt0, done0))

    need = kf - count_where(lambda blk: blk > thr)
    tri = jnp.where(lax.broadcasted_iota(jnp.int32, (KC, KC), 0)
                    <= lax.broadcasted_iota(jnp.int32, (KC, KC), 1), 1.0, 0.0).astype(jnp.bfloat16)

    def mask_chunk(c, run):
        k0 = pl.multiple_of(c * KC, KC)
        blk = sc_ref[:, pl.ds(k0, KC)]
        eq = jnp.where(blk == thr, 1.0, 0.0)
        pref = jnp.dot(eq.astype(jnp.bfloat16), tri, preferred_element_type=f32)
        sel = (blk > thr) | ((eq > 0.5) & (run + pref <= need))
        sc_ref[:, pl.ds(k0, KC)] = jnp.where(sel, 0.0, MASK_NEG)
        return run + pref[:, KC - 1:KC]

    lax.fori_loop(0, nch, mask_chunk, jnp.zeros((TQ, 1), f32))

    lane = lax.broadcasted_iota(jnp.int32, (TQ, V7X_LANES), 1)

    def head_pair(p, _):
        l0 = pl.multiple_of(p * V7X_LANES, V7X_LANES)
        q_pair = q_ref[:, pl.ds(l0, V7X_LANES)]
        outs = []
        for sub in range(2):
            h = 2 * p + sub
            in_head = (lane // A_HEAD_DIM) == sub
            qh = jnp.where(in_head, q_pair, jnp.zeros_like(q_pair))

            def step(c, bias, carry):
                m, l, acc = carry
                k0 = pl.multiple_of(c * KC, KC)
                kT = kT_ref[0, pl.ds(l0, V7X_LANES), pl.ds(k0, KC)]
                s = jnp.dot(qh, kT, preferred_element_type=f32) + bias + sc_ref[:, pl.ds(k0, KC)]
                m_new = jnp.maximum(m, jnp.max(s, axis=1, keepdims=True))
                alpha = jnp.exp(m - m_new)
                pr = jnp.exp(s - m_new)
                l_new = alpha * l + jnp.sum(pr, axis=1, keepdims=True)
                vv = v_ref[pl.ds(k0, KC), pl.ds(l0, V7X_LANES)]
                acc_new = alpha * acc + jnp.dot(pr.astype(jnp.bfloat16), vv, preferred_element_type=f32)
                return m_new, l_new, acc_new

            carry = (jnp.full((TQ, 1), MASK_NEG, f32), jnp.zeros((TQ, 1), f32),
                     jnp.zeros((TQ, V7X_LANES), f32))
            far_bias = efar_ref[h]
            carry = lax.fori_loop(0, jnp.maximum(i - 1, 0),
                                  lambda c, cr: step(c, far_bias, cr), carry)
            carry = lax.cond(i >= 1,
                             lambda cr: step(i - 1, enear_ref[h, :, 0:KC], cr),
                             lambda cr: cr, carry)
            m, l, acc = step(i, enear_ref[h, :, KC:2 * KC], carry)
            outs.append(acc / l)
        o_pair = jnp.where((lane // A_HEAD_DIM) == 0, outs[0], outs[1])
        o_ref[:, pl.ds(l0, V7X_LANES)] = o_pair.astype(o_ref.dtype)
        return 0

    lax.fori_loop(0, A_HEADS // 2, head_pair, 0)


def _dsa(q, v, iq, iw, kT, ikT, rel_bias, B, S):
    T = q.shape[0]
    TQ = min(ATT_Q, S)
    nQ = S // TQ
    topk = min(TOPK_MAX, S // 4)
    buckets = _t5_bucket_table(2 * TQ + 1)
    assert np.all(_t5_bucket_table(S + 1)[TQ + 1:] == REL_BUCKETS - 1)
    r = np.arange(TQ)[:, None]
    j = np.arange(2 * TQ)[None, :]
    dist = np.maximum(r + TQ - j, 0)
    enear = jnp.transpose(rel_bias[buckets[dist]], (2, 0, 1)).astype(jnp.float32)
    efar = rel_bias[REL_BUCKETS - 1].astype(jnp.float32)

    return pl.pallas_call(
        functools.partial(_dsa_kernel, topk=topk),
        grid=(B, nQ),
        in_specs=[
            pl.BlockSpec(memory_space=pltpu.SMEM),
            pl.BlockSpec((TQ, A_WIDTH), lambda b, i: (b * nQ + i, 0)),
            pl.BlockSpec((1, A_WIDTH, S), lambda b, i: (b, 0, 0)),
            pl.BlockSpec((S, A_WIDTH), lambda b, i: (b, 0)),
            pl.BlockSpec((TQ, IDX_WIDTH), lambda b, i: (b * nQ + i, 0)),
            pl.BlockSpec((TQ, IDX_HEADS), lambda b, i: (b * nQ + i, 0)),
            pl.BlockSpec((1, IDX_WIDTH, S), lambda b, i: (b, 0, 0)),
            pl.BlockSpec((A_HEADS, TQ, 2 * TQ), lambda b, i: (0, 0, 0)),
        ],
        out_specs=pl.BlockSpec((TQ, A_WIDTH), lambda b, i: (b * nQ + i, 0)),
        scratch_shapes=[pltpu.VMEM((TQ, S), jnp.float32)],
        out_shape=jax.ShapeDtypeStruct((T, A_WIDTH), jnp.bfloat16),
        compiler_params=_cparams(("parallel", "arbitrary")),
        name="dsa",
    )(efar, q, kT, v, iq, iw, ikT, enear)


def _hgrn_kernel(bq_ref, bf_ref, bi_ref, bg_ref, lb_ref, gain_ref, o_ref,
                 st_ref, b_scr, q_scr, k_scr, v_scr, oi_scr):
    R = bq_ref.shape[0]
    C = HGRN_CHUNK
    f32 = jnp.float32
    bf16 = jnp.bfloat16
    h = pl.program_id(1)

    @pl.when(pl.program_id(2) == 0)
    def _():
        st_ref[...] = jnp.zeros_like(st_ref)

    lb = lb_ref[pl.ds(h, 1), :]
    gain = gain_ref[pl.ds(h, 1), :]
    tril_incl = jnp.where(lax.broadcasted_iota(jnp.int32, (C, C), 1)
                          <= lax.broadcasted_iota(jnp.int32, (C, C), 0), 1.0, 0.0)
    srow = lax.broadcasted_iota(jnp.int32, (C, B_KEY_DIM), 0)

    def chunk(ci, _):
        r0 = pl.multiple_of(ci * C, C)
        f = lb + (1.0 - lb) * _sigmoid(bf_ref[pl.ds(r0, C), :])
        g = jnp.log(f)
        kk = 1.0 - f
        qr = bq_ref[pl.ds(r0, C), :]
        qq = qr * _sigmoid(qr) * (B_KEY_DIM ** -0.5)
        vv = bi_ref[pl.ds(r0, C), :]
        b = jnp.dot(tril_incl, g, preferred_element_type=f32, precision=lax.Precision.HIGHEST)
        b_last = b[C - 1:C, :]
        st = st_ref[...]
        o_inter = lax.dot_general((qq * jnp.exp(b)).astype(bf16), st.astype(bf16),
                                  (((1,), (1,)), ((), ())), preferred_element_type=f32)
        b_scr[...] = b
        q_scr[...] = qq
        k_scr[...] = kk
        v_scr[...] = vv

        safe = jnp.min(b_last) >= -HGRN_SAFE_DECAY

        @pl.when(safe)
        def _():
            qd = (qq * jnp.exp(b)).astype(bf16)
            kd = (kk * jnp.exp(-b)).astype(bf16)
            att = lax.dot_general(qd, kd, (((1,), (1,)), ((), ())), preferred_element_type=f32)
            att = att * tril_incl
            oi_scr[...] = jnp.dot(att.astype(bf16), vv.astype(bf16), preferred_element_type=f32)

        @pl.when(jnp.logical_not(safe))
        def _():
            def row(t, _):
                bt = b_scr[pl.ds(t, 1), :]
                qt = q_scr[pl.ds(t, 1), :]
                ex = jnp.where(srow <= t, bt - b_scr[...], -jnp.inf)
                a = jnp.sum(qt * k_scr[...] * jnp.exp(ex), axis=1, keepdims=True)
                oi_scr[pl.ds(t, 1), :] = jnp.sum(a * v_scr[...], axis=0, keepdims=True)
                return 0
            lax.fori_loop(0, C, row, 0)

        o = o_inter + oi_scr[...]
        kd_last = (kk * jnp.exp(b_last - b)).astype(bf16)
        upd = lax.dot_general(vv.astype(bf16), kd_last, (((0,), (0,)), ((), ())),
                              preferred_element_type=f32)
        st_ref[...] = st * jnp.exp(b_last) + upd
        og = bg_ref[pl.ds(r0, C), :]
        y = _rms(o, gain) * (og * _sigmoid(og))
        o_ref[pl.ds(r0, C), :] = y.astype(o_ref.dtype)
        return 0

    lax.fori_loop(0, R // C, chunk, 0)


def _hgrn(bq, bf, bi, bg, lb, gain, B, S):
    T = bq.shape[0]
    R = min(HGRN_ROWS, S)
    nR = S // R
    C = HGRN_CHUNK
    blk = pl.BlockSpec((R, B_KEY_DIM), lambda b, h, c: (b * nR + c, h))
    small = pl.BlockSpec((B_HEADS, B_KEY_DIM), lambda b, h, c: (0, 0))
    f32 = jnp.float32
    return pl.pallas_call(
        _hgrn_kernel,
        grid=(B, B_HEADS, nR),
        in_specs=[blk, blk, blk, blk, small, small],
        out_specs=blk,
        out_shape=jax.ShapeDtypeStruct((T, B_WIDTH), jnp.bfloat16),
        scratch_shapes=[pltpu.VMEM((B_VAL_DIM, B_KEY_DIM), f32)] +
                       [pltpu.VMEM((C, B_KEY_DIM), f32) for _ in range(5)],
        compiler_params=_cparams(("parallel", "parallel", "arbitrary")),
        name="hgrn",
    )(bq, bf, bi, bg, lb, gain)


def _merge_kernel(x_ref, ya_ref, yb_ref, ga_ref, gb_ref, wa_ref, wb_ref, wo_ref, g_ref, wr_ref, br_ref,
                  x1_ref, xn_ref, lg_ref):
    f32 = jnp.float32
    ma = jnp.dot(ya_ref[...], wa_ref[...], preferred_element_type=f32)
    mb = jnp.dot(yb_ref[...], wb_ref[...], preferred_element_type=f32)
    merged = ga_ref[...].astype(f32) * ma + gb_ref[...].astype(f32) * mb
    x1 = x_ref[...] + jnp.dot(merged.astype(jnp.bfloat16), wo_ref[...], preferred_element_type=f32)
    x1_ref[...] = x1
    hn = _rms(x1, g_ref[...])
    xn_ref[...] = hn.astype(jnp.bfloat16)
    lg_ref[...] = jnp.dot(hn, wr_ref[...], preferred_element_type=f32,
                          precision=lax.Precision.HIGHEST) + br_ref[...]


def _merge(x2, ya, yb, ga, gb, w_up_a, w_up_b, w_out, gain, w_router, b_router):
    T, D = x2.shape
    R = min(PROJ_ROWS, T)
    bf = jnp.bfloat16
    ins = [x2, ya, yb, ga, gb, w_up_a.astype(bf), w_up_b.astype(bf), w_out.astype(bf),
           gain.reshape(1, D), w_router, b_router.reshape(1, N_EXPERTS)]
    row = lambda n: pl.BlockSpec((R, n), lambda i: (i, 0))
    full = lambda a: pl.BlockSpec(a.shape, lambda i: (0,) * a.ndim)
    in_specs = [row(D), row(A_WIDTH), row(B_WIDTH), row(D), row(D)] + [full(a) for a in ins[5:]]
    return pl.pallas_call(
        _merge_kernel,
        grid=(T // R,),
        in_specs=in_specs,
        out_specs=[row(D), row(D), row(N_EXPERTS)],
        out_shape=[jax.ShapeDtypeStruct((T, D), jnp.float32), jax.ShapeDtypeStruct((T, D), bf),
                   jax.ShapeDtypeStruct((T, N_EXPERTS), jnp.float32)],
        compiler_params=_cparams(("parallel",)),
        name="merge",
    )(*ins)


def _route_kernel(lg_ref, eidx_ref, gate_ref, rank_ref, cnt_ref, run_ref):
    R = lg_ref.shape[0]
    f32 = jnp.float32

    @pl.when(pl.program_id(0) == 0)
    def _():
        run_ref[...] = jnp.zeros_like(run_ref)

    lg = lg_ref[...]
    lane = lax.broadcasted_iota(jnp.int32, (R, N_EXPERTS), 1)
    work = lg
    onehots, vals, idxs = [], [], []
    for _ in range(TOP_K):
        m = jnp.max(work, axis=1, keepdims=True)
        idx = jnp.min(jnp.where(work == m, lane, N_EXPERTS), axis=1, keepdims=True)
        oh = lane == idx
        onehots.append(oh)
        vals.append(m)
        idxs.append(idx)
        work = jnp.where(oh, -jnp.inf, work)
    ex = [jnp.exp(v - vals[0]) for v in vals]
    den = ex[0] + ex[1] + ex[2] + ex[3]
    chosen = jnp.where(onehots[0] | onehots[1] | onehots[2] | onehots[3], 1.0, 0.0)
    strict = jnp.where(lax.broadcasted_iota(jnp.int32, (R, R), 1)
                       < lax.broadcasted_iota(jnp.int32, (R, R), 0), 1.0, 0.0).astype(jnp.bfloat16)
    before = jnp.dot(strict, chosen.astype(jnp.bfloat16), preferred_element_type=f32) + run_ref[...]
    lane4 = lax.broadcasted_iota(jnp.int32, (R, TOP_K), 1)
    eidx = jnp.zeros((R, TOP_K), jnp.int32)
    gate = jnp.zeros((R, TOP_K), f32)
    rank = jnp.zeros((R, TOP_K), f32)
    for k in range(TOP_K):
        eidx = jnp.where(lane4 == k, idxs[k], eidx)
        gate = jnp.where(lane4 == k, ex[k] / den, gate)
        rk = jnp.sum(jnp.where(onehots[k], before, 0.0), axis=1, keepdims=True)
        rank = jnp.where(lane4 == k, rk, rank)
    eidx_ref[...] = eidx
    gate_ref[...] = gate
    rank_ref[...] = rank.astype(jnp.int32)
    run_ref[...] = run_ref[...] + jnp.sum(chosen, axis=0, keepdims=True)
    cnt_ref[...] = run_ref[...].astype(jnp.int32)


def _route(logits):
    T = logits.shape[0]
    R = min(ROUTE_ROWS, T)
    row = lambda n: pl.BlockSpec((R, n), lambda i: (i, 0))
    return pl.pallas_call(
        _route_kernel,
        grid=(T // R,),
        in_specs=[row(N_EXPERTS)],
        out_specs=[row(TOP_K), row(TOP_K), row(TOP_K), pl.BlockSpec((1, N_EXPERTS), lambda i: (0, 0))],
        out_shape=[jax.ShapeDtypeStruct((T, TOP_K), jnp.int32), jax.ShapeDtypeStruct((T, TOP_K), jnp.float32),
                   jax.ShapeDtypeStruct((T, TOP_K), jnp.int32), jax.ShapeDtypeStruct((1, N_EXPERTS), jnp.int32)],
        scratch_shapes=[pltpu.VMEM((1, N_EXPERTS), jnp.float32)],
        compiler_params=_cparams(("arbitrary",)),
        name="route",
    )(logits)


def _dispatch_kernel(dest_ref, x_ref, zsrc_ref, o_ref, sem):
    R = dest_ref.shape[-1] // TOP_K
    step = pl.program_id(0)
    n_blocks = o_ref.shape[0] // EXPERT_ROWS

    def zcopy(e):
        return pltpu.make_async_copy(zsrc_ref, o_ref.at[pl.ds(e * EXPERT_ROWS, EXPERT_ROWS)], sem)

    @pl.when(step == 0)
    def _():
        def start(e, _):
            zcopy(e).start()
            return 0
        lax.fori_loop(0, n_blocks, start, 0)

        def wait(e, _):
            zcopy(e).wait()
            return 0
        lax.fori_loop(0, n_blocks, wait, 0)

    base = step * R

    def rcopy(j):
        t = base + j // TOP_K
        return pltpu.make_async_copy(x_ref.at[pl.ds(t, 1)], o_ref.at[pl.ds(dest_ref[0, 0, j], 1)], sem)

    _windowed_copies(rcopy, R * TOP_K)


def _windowed_copies(copy_of, n):
    def issue(j, _):
        copy_of(j).start()

        @pl.when(j >= DMA_WINDOW)
        def _():
            copy_of(j - DMA_WINDOW).wait()
        return 0
    lax.fori_loop(0, n, issue, 0)

    def drain(j, _):
        copy_of(j).wait()
        return 0
    lax.fori_loop(max(n - DMA_WINDOW, 0), n, drain, 0)


def _dispatch(xw, dest, P):
    T, W = xw.shape
    R = min(DISPATCH_ROWS, T)
    n = T // R
    dest3 = dest.reshape(n, 1, R * TOP_K)
    zsrc = jnp.zeros((EXPERT_ROWS, W), xw.dtype)
    return pl.pallas_call(
        _dispatch_kernel,
        grid=(n,),
        in_specs=[pl.BlockSpec((1, 1, R * TOP_K), lambda i: (i, 0, 0), memory_space=pltpu.SMEM),
                  pl.BlockSpec(memory_space=pl.ANY),
                  pl.BlockSpec(memory_space=pl.ANY)],
        out_specs=pl.BlockSpec(memory_space=pl.ANY),
        out_shape=jax.ShapeDtypeStruct((P, W), xw.dtype),
        scratch_shapes=[pltpu.SemaphoreType.DMA],
        compiler_params=pltpu.CompilerParams(dimension_semantics=("arbitrary",),
                                             has_side_effects=True),
        name="dispatch",
    )(dest3, xw, zsrc)


def _experts_kernel(be_ref, nb_ref, x_ref, wgu_ref, bgu_ref, wd_ref, bd_ref, o_ref):
    f32 = jnp.float32
    d_ff = wd_ref.shape[1]

    @pl.when(pl.program_id(0) < nb_ref[0])
    def _():
        gu = jnp.dot(x_ref[...], wgu_ref[0], preferred_element_type=f32) + bgu_ref[0]
        gate = jnp.minimum(gu[:, :d_ff], SWIGLU_LIMIT)
        lin = jnp.clip(gu[:, d_ff:], -SWIGLU_LIMIT, SWIGLU_LIMIT)
        act = (lin + 1.0) * gate * _sigmoid(SWIGLU_ALPHA * gate)
        y = jnp.dot(act.astype(jnp.bfloat16), wd_ref[0], preferred_element_type=f32) + bd_ref[0]
        o_ref[...] = y

    @pl.when(pl.program_id(0) >= nb_ref[0])
    def _():
        o_ref[...] = jnp.zeros_like(o_ref)


def _experts(xs, block_expert, n_used, w_gu, b_gu, w_down, b_down):
    P, D = xs.shape
    E, _, F2 = w_gu.shape
    nb = P // EXPERT_ROWS
    grid_spec = pltpu.PrefetchScalarGridSpec(
        num_scalar_prefetch=2,
        grid=(nb,),
        in_specs=[
            pl.BlockSpec((EXPERT_ROWS, D), lambda i, be, nu: (i, 0)),
            pl.BlockSpec((1, D, F2), lambda i, be, nu: (be[i], 0, 0)),
            pl.BlockSpec((1, 1, F2), lambda i, be, nu: (be[i], 0, 0)),
            pl.BlockSpec((1, F2 // 2, D), lambda i, be, nu: (be[i], 0, 0)),
            pl.BlockSpec((1, 1, D), lambda i, be, nu: (be[i], 0, 0)),
        ],
        out_specs=pl.BlockSpec((EXPERT_ROWS, D), lambda i, be, nu: (i, 0)),
    )
    return pl.pallas_call(
        _experts_kernel,
        grid_spec=grid_spec,
        out_shape=jax.ShapeDtypeStruct((P, D), jnp.float32),
        compiler_params=_cparams(("arbitrary",)),
        name="experts",
    )(block_expert, n_used, xs, w_gu.astype(jnp.bfloat16), b_gu.reshape(E, 1, F2),
      w_down.astype(jnp.bfloat16), b_down.reshape(E, 1, D))


def _combine_kernel(dest_ref, y_ref, x1_ref, gate_ref, g_ref, o_ref, buf, sem):
    R = x1_ref.shape[0]

    def rcopy(j):
        return pltpu.make_async_copy(y_ref.at[pl.ds(dest_ref[0, 0, j], 1)],
                                     buf.at[j % TOP_K, pl.ds(j // TOP_K, 1)], sem)

    _windowed_copies(rcopy, R * TOP_K)

    gate = gate_ref[...]
    y = x1_ref[...]
    for k in range(TOP_K):
        y = y + gate[:, k:k + 1] * buf[k]
    o_ref[...] = _rms(y, g_ref[...])


def _combine(y_buf, dest, x1, gates, gain):
    T, D = x1.shape
    R = min(DISPATCH_ROWS, T)
    n = T // R
    dest3 = dest.reshape(n, 1, R * TOP_K)
    row = lambda w: pl.BlockSpec((R, w), lambda i: (i, 0))
    return pl.pallas_call(
        _combine_kernel,
        grid=(n,),
        in_specs=[pl.BlockSpec((1, 1, R * TOP_K), lambda i: (i, 0, 0), memory_space=pltpu.SMEM),
                  pl.BlockSpec(memory_space=pl.ANY),
                  row(D), row(TOP_K),
                  pl.BlockSpec((1, D), lambda i: (0, 0))],
        out_specs=row(D),
        out_shape=jax.ShapeDtypeStruct((T, D), jnp.float32),
        scratch_shapes=[pltpu.VMEM((TOP_K, R, D), jnp.float32), pltpu.SemaphoreType.DMA],
        compiler_params=_cparams(("arbitrary",)),
        name="combine",
    )(dest3, y_buf, x1, gates, gain.reshape(1, D))


def _moe_plan(eidx, rank, counts, A):
    counts = counts.reshape(N_EXPERTS)
    padded = (counts + EXPERT_ROWS - 1) // EXPERT_ROWS * EXPERT_ROWS
    pad_ends = jnp.cumsum(padded)
    pad_starts = pad_ends - padded
    n_blocks = -(-A // EXPERT_ROWS) + N_EXPERTS
    dest = pad_starts[eidx] + rank
    block_expert = jnp.minimum(
        jnp.searchsorted(pad_ends, jnp.arange(n_blocks) * EXPERT_ROWS, side='right'),
        N_EXPERTS - 1).astype(jnp.int32)
    n_used = (pad_ends[-1] // EXPERT_ROWS).astype(jnp.int32).reshape(1)
    return dest.astype(jnp.int32), block_expert, n_used, n_blocks


def kernel(x, w_in, w_up_a, w_up_b, w_out, norm_mix, norm_ffn, norm_final, hgrn_norm,
           lb_logits, rel_bias, w_router, b_router, w_gu, b_gu, w_down, b_down):
    B, S, D = x.shape
    T = B * S
    assert w_in.shape[0] == 1, "the final rmsnorm is fused into the single layer's combine stage"
    lb_all = jnp.cumsum(jax.nn.softmax(lb_logits.astype(jnp.float32), axis=0), axis=0)
    x2 = x.reshape(T, D)
    (q, v, iq, iw, kT, ikT, bq, bf, bi, bg, ga, gb) = _inproj(x2, norm_mix[0], w_in[0], B, S)
    ya = _dsa(q, v, iq, iw, kT, ikT, rel_bias, B, S)
    yb = _hgrn(bq, bf, bi, bg, lb_all[0].reshape(B_HEADS, B_KEY_DIM), hgrn_norm[0], B, S)
    x1, xn, logits = _merge(x2, ya, yb, ga, gb, w_up_a[0], w_up_b[0], w_out[0], norm_ffn[0],
                            w_router[0], b_router[0])
    eidx, gates, rank, counts = _route(logits)
    dest, block_expert, n_used, n_blocks = _moe_plan(eidx, rank, counts, T * TOP_K)
    P = n_blocks * EXPERT_ROWS
    xw = lax.bitcast_convert_type(xn.reshape(T, D // 2, 2), jnp.uint32)
    xs = _dispatch(xw, dest, P)
    xs = lax.bitcast_convert_type(xs, jnp.bfloat16).reshape(P, D)
    y_buf = _experts(xs, block_expert, n_used, w_gu[0], b_gu[0], w_down[0], b_down[0])
    out = _combine(y_buf, dest, x1, gates, norm_final)
    return out.reshape(B, S, D)
```

```python
import functools
import math

import numpy as np
import jax
import jax.numpy as jnp
from jax import lax
from jax.experimental import pallas as pl
from jax.experimental.pallas import tpu as pltpu

A_HEADS = 8
A_HEAD_DIM = 64
IDX_HEADS = 8
IDX_DIM = 32
TOPK_MAX = 256
REL_BUCKETS = 32
REL_MAX_DIST = 128
B_HEADS = 4
B_KEY_DIM = 128
B_VAL_DIM = 128
N_EXPERTS = 32
TOP_K = 4
SWIGLU_LIMIT = 7.0
SWIGLU_ALPHA = 1.702
EPS = 1e-6

A_WIDTH = A_HEADS * A_HEAD_DIM
B_WIDTH = B_HEADS * B_VAL_DIM
IDX_WIDTH = IDX_HEADS * IDX_DIM

V7X_LANES = 128
V7X_VMEM_LIMIT_BYTES = 56 * 1024 * 1024

PROJ_ROWS = 512
ATT_Q = 256
HGRN_ROWS = 512
HGRN_CHUNK = 32
HGRN_SAFE_DECAY = 60.0
ROUTE_ROWS = 512
EXPERT_ROWS = 256
DISPATCH_ROWS = 512
DMA_WINDOW = 64
MASK_NEG = -1e30
BISECT_FAST_ITERS = 26


def _cparams(dims):
    return pltpu.CompilerParams(dimension_semantics=dims, vmem_limit_bytes=V7X_VMEM_LIMIT_BYTES)


def _rms(x, gain):
    return x * lax.rsqrt(jnp.mean(x * x, axis=-1, keepdims=True) + EPS) * gain


def _sigmoid(x):
    return 1.0 / (1.0 + jnp.exp(-x))


def _pack_bf16_pairs(x):
    n = x.shape[1] // 2
    as_bits = lambda v: lax.bitcast_convert_type(v.astype(jnp.bfloat16).astype(jnp.float32), jnp.uint32)
    return (as_bits(x[:, :n]) & jnp.uint32(0xFFFF0000)) | (as_bits(x[:, n:]) >> 16)


def _unpack_bf16_pairs(w):
    hi = lax.bitcast_convert_type(w & jnp.uint32(0xFFFF0000), jnp.float32).astype(jnp.bfloat16)
    lo = lax.bitcast_convert_type(w << 16, jnp.float32).astype(jnp.bfloat16)
    return hi, lo


def _inproj_kernel(x_ref, g_ref, wq_ref, wv_ref, wiq_ref, wiw_ref, wkT_ref, wikT_ref,
                   wb_ref, wg_ref,
                   q_ref, v_ref, iq_ref, iw_ref, kT_ref, ikT_ref, bq_ref, bf_ref, bi_ref, bg_ref,
                   ga_ref, gb_ref):
    x = x_ref[...]
    hn = _rms(x, g_ref[...]).astype(jnp.bfloat16)

    def mm(w_ref):
        return jnp.dot(hn, w_ref[...], preferred_element_type=jnp.float32)

    def mm_t(w_ref):
        return lax.dot_general(w_ref[...], hn, (((1,), (1,)), ((), ())),
                               preferred_element_type=jnp.float32)

    q_ref[...] = (mm(wq_ref) * (A_HEAD_DIM ** -0.5)).astype(jnp.bfloat16)
    v_ref[...] = mm(wv_ref).astype(jnp.bfloat16)
    iq_ref[...] = mm(wiq_ref).astype(jnp.bfloat16)
    iw_ref[...] = mm(wiw_ref) * ((IDX_HEADS * IDX_DIM) ** -0.5)
    kT_ref[0] = mm_t(wkT_ref).astype(jnp.bfloat16)
    ikT_ref[0] = mm_t(wikT_ref).astype(jnp.bfloat16)
    hb = mm(wb_ref)
    bq_ref[...] = hb[:, 0 * B_WIDTH:1 * B_WIDTH]
    bf_ref[...] = hb[:, 1 * B_WIDTH:2 * B_WIDTH]
    bi_ref[...] = hb[:, 2 * B_WIDTH:3 * B_WIDTH]
    bg_ref[...] = hb[:, 3 * B_WIDTH:4 * B_WIDTH]
    d = ga_ref.shape[-1]
    hg = mm(wg_ref)
    ga_ref[...] = _sigmoid(hg[:, :d]).astype(jnp.bfloat16)
    gb_ref[...] = _sigmoid(hg[:, d:]).astype(jnp.bfloat16)


def _inproj(x2, gain, w_in, B, S):
    T, D = x2.shape
    R = min(PROJ_ROWS, S)
    nS = S // R
    o = np.cumsum((0, A_WIDTH, A_WIDTH, A_WIDTH, IDX_WIDTH, IDX_HEADS, IDX_DIM,
                   B_WIDTH, B_WIDTH, B_WIDTH, B_WIDTH, D, D))
    bf = jnp.bfloat16
    wq = w_in[:, o[0]:o[1]].astype(bf)
    wk = w_in[:, o[1]:o[2]]
    wv = w_in[:, o[2]:o[3]].astype(bf)
    wiq = w_in[:, o[3]:o[4]].astype(bf)
    wiw = w_in[:, o[4]:o[5]].astype(bf)
    wik = w_in[:, o[5]:o[6]]
    wb = w_in[:, o[6]:o[10]].astype(bf)
    wg = w_in[:, o[10]:o[12]].astype(bf)
    wkT = wk.T.astype(bf)
    wikT = jnp.tile(wik.T, (IDX_HEADS, 1)).astype(bf)

    def full(a):
        return pl.BlockSpec(a.shape, lambda b, i: (0,) * a.ndim)

    row = lambda n: pl.BlockSpec((R, n), lambda b, i: (b * nS + i, 0))
    colT = lambda n: pl.BlockSpec((1, n, R), lambda b, i: (b, 0, i))
    f32 = jnp.float32
    outs = [
        (jax.ShapeDtypeStruct((T, A_WIDTH), bf), row(A_WIDTH)),
        (jax.ShapeDtypeStruct((T, A_WIDTH), bf), row(A_WIDTH)),
        (jax.ShapeDtypeStruct((T, IDX_WIDTH), bf), row(IDX_WIDTH)),
        (jax.ShapeDtypeStruct((T, IDX_HEADS), f32), row(IDX_HEADS)),
        (jax.ShapeDtypeStruct((B, A_WIDTH, S), bf), colT(A_WIDTH)),
        (jax.ShapeDtypeStruct((B, IDX_WIDTH, S), bf), colT(IDX_WIDTH)),
        (jax.ShapeDtypeStruct((T, B_WIDTH), f32), row(B_WIDTH)),
        (jax.ShapeDtypeStruct((T, B_WIDTH), f32), row(B_WIDTH)),
        (jax.ShapeDtypeStruct((T, B_WIDTH), f32), row(B_WIDTH)),
        (jax.ShapeDtypeStruct((T, B_WIDTH), f32), row(B_WIDTH)),
        (jax.ShapeDtypeStruct((T, D), bf), row(D)),
        (jax.ShapeDtypeStruct((T, D), bf), row(D)),
    ]
    ins = [x2, gain.reshape(1, D), wq, wv, wiq, wiw, wkT, wikT, wb, wg]
    in_specs = [row(D)] + [full(a) for a in ins[1:]]
    return pl.pallas_call(
        _inproj_kernel,
        grid=(B, nS),
        in_specs=in_specs,
        out_specs=[s for _, s in outs],
        out_shape=[s for s, _ in outs],
        compiler_params=_cparams(("parallel", "parallel")),
        name="inproj",
    )(*ins)


def _t5_bucket_table(n):
    d = np.arange(n)
    max_exact = REL_BUCKETS // 2
    nf = np.maximum(d, 1).astype(np.float64)
    large = max_exact + (np.log(nf / max_exact) / math.log(REL_MAX_DIST / max_exact)
                         * (REL_BUCKETS - max_exact)).astype(np.int32)
    large = np.minimum(large, REL_BUCKETS - 1)
    return np.where(d < max_exact, d, large)


def _dsa_kernel(efar_ref, q_ref, kT_ref, v_ref, iq_ref, iw_ref, ikT_ref, enear_ref,
                o_ref, sc_ref, *, topk):
    TQ = q_ref.shape[0]
    KC = TQ
    i = pl.program_id(1)
    nch = i + 1
    q0 = i * TQ
    f32 = jnp.float32
    row_id = lax.broadcasted_iota(jnp.int32, (TQ, KC), 0)
    col_id = lax.broadcasted_iota(jnp.int32, (TQ, KC), 1)

    iq = iq_ref[...]
    iw = iw_ref[...]
    head_of_lane = lax.broadcasted_iota(jnp.int32, iq.shape, 1) // IDX_DIM
    iq_heads = [jnp.where(head_of_lane == h, iq, jnp.zeros_like(iq)) for h in range(IDX_HEADS)]

    def score_chunk(c, carry):
        rmin, rmax = carry
        k0 = pl.multiple_of(c * KC, KC)
        ik = ikT_ref[0, :, pl.ds(k0, KC)]
        acc = jnp.zeros((TQ, KC), f32)
        for h in range(IDX_HEADS):
            sh = jnp.dot(iq_heads[h], ik, preferred_element_type=f32)
            acc = acc + jnp.maximum(sh, 0.0) * iw[:, h:h + 1]
        valid = (k0 + col_id) <= (q0 + row_id)
        sc_ref[:, pl.ds(k0, KC)] = jnp.where(valid, acc, MASK_NEG)
        rmin = jnp.minimum(rmin, jnp.min(jnp.where(valid, acc, -MASK_NEG), axis=1, keepdims=True))
        rmax = jnp.maximum(rmax, jnp.max(jnp.where(valid, acc, MASK_NEG), axis=1, keepdims=True))
        return rmin, rmax

    rmin, rmax = lax.fori_loop(
        0, nch, score_chunk,
        (jnp.full((TQ, 1), -MASK_NEG, f32), jnp.full((TQ, 1), MASK_NEG, f32)))

    def fold_lanes(x):
        out = x[:, :V7X_LANES]
        for j in range(1, KC // V7X_LANES):
            out = out + x[:, j * V7X_LANES:(j + 1) * V7X_LANES]
        return out

    def count_where(pred_fn):
        def body(c, acc):
            k0 = pl.multiple_of(c * KC, KC)
            blk = sc_ref[:, pl.ds(k0, KC)]
            return acc + fold_lanes(jnp.where(pred_fn(blk), 1.0, 0.0))
        acc = lax.fori_loop(0, nch, body, jnp.zeros((TQ, V7X_LANES), f32))
        return jnp.sum(acc, axis=1, keepdims=True)

    def band_min_max(lo, hi):
        def body(c, carry):
            bmin, bmax = carry
            k0 = pl.multiple_of(c * KC, KC)
            blk = sc_ref[:, pl.ds(k0, KC)]
            bmin = jnp.minimum(bmin, jnp.min(jnp.where(blk >= lo, blk, -MASK_NEG), axis=1, keepdims=True))
            bmax = jnp.maximum(bmax, jnp.max(jnp.where(blk < hi, blk, MASK_NEG), axis=1, keepdims=True))
            return bmin, bmax
        return lax.fori_loop(0, nch, body,
                             (jnp.full((TQ, 1), -MASK_NEG, f32), jnp.full((TQ, 1), MASK_NEG, f32)))

    kf = float(topk)
    n_valid = (q0 + 1 + lax.broadcasted_iota(jnp.int32, (TQ, 1), 0)).astype(f32)
    c_top = count_where(lambda blk: blk >= rmax)
    top_tie = c_top >= kf
    lo0 = jnp.where(top_tie, rmax, rmin)
    cnt0 = jnp.where(top_tie, c_top, n_valid)
    done0 = jnp.where(top_tie | (cnt0 <= kf), 1.0, 0.0)

    def cond(st):
        it, lo, hi, cnt, done = st
        return jnp.min(done) < 0.5

    def body(st):
        it, lo, hi, cnt, done = st

        def plain(_):
            mid = lo + 0.5 * (hi - lo)
            stuck = (mid <= lo) | (mid >= hi)
            return lo, mid, jnp.where(stuck, 1.0, 0.0)

        def snapped(_):
            bmin, bmax = band_min_max(lo, hi)
            mid = bmin + 0.5 * (bmax - bmin)
            mid = jnp.where(mid <= bmin, bmax, mid)
            return bmin, mid, jnp.where(bmax <= bmin, 1.0, 0.0)

        lo_s, mid, tie_f = lax.cond(it < BISECT_FAST_ITERS, plain, snapped, None)
        tie = tie_f > 0.5
        active = done < 0.5
        lo_s = jnp.where(active, lo_s, lo)
        c = count_where(lambda blk: blk >= mid)
        feas = c >= kf
        move = active & jnp.logical_not(tie)
        lo_n = jnp.where(move & feas, mid, lo_s)
        cnt_n = jnp.where(move & feas, c, cnt)
        hi_n = jnp.where(move & jnp.logical_not(feas), mid, hi)
        done_n = jnp.where((active & tie) | (cnt_n <= kf), 1.0, done)
        return it + 1, lo_n, hi_n, cnt_n, done_n

    _, thr, _, _, _ = lax.while_loop(cond, body, (jnp.int32(0), lo0, rmax, cnt0, done0))

    need = kf - count_where(lambda blk: blk > thr)
    tri = jnp.where(lax.broadcasted_iota(jnp.int32, (KC, KC), 0)
                    <= lax.broadcasted_iota(jnp.int32, (KC, KC), 1), 1.0, 0.0).astype(jnp.bfloat16)

    def mask_chunk(c, run):
        k0 = pl.multiple_of(c * KC, KC)
        blk = sc_ref[:, pl.ds(k0, KC)]
        eq = jnp.where(blk == thr, 1.0, 0.0)
        pref = jnp.dot(eq.astype(jnp.bfloat16), tri, preferred_element_type=f32)
        sel = (blk > thr) | ((eq > 0.5) & (run + pref <= need))
        sc_ref[:, pl.ds(k0, KC)] = jnp.where(sel, 0.0, MASK_NEG)
        return run + pref[:, KC - 1:KC]

    lax.fori_loop(0, nch, mask_chunk, jnp.zeros((TQ, 1), f32))

    lane = lax.broadcasted_iota(jnp.int32, (TQ, V7X_LANES), 1)

    def head_pair(p, _):
        l0 = pl.multiple_of(p * V7X_LANES, V7X_LANES)
        q_pair = q_ref[:, pl.ds(l0, V7X_LANES)]
        outs = []
        for sub in range(2):
            h = 2 * p + sub
            in_head = (lane // A_HEAD_DIM) == sub
            qh = jnp.where(in_head, q_pair, jnp.zeros_like(q_pair))

            def step(c, bias, carry):
                m, l, acc = carry
                k0 = pl.multiple_of(c * KC, KC)
                kT = kT_ref[0, pl.ds(l0, V7X_LANES), pl.ds(k0, KC)]
                s = jnp.dot(qh, kT, preferred_element_type=f32) + bias + sc_ref[:, pl.ds(k0, KC)]
                m_new = jnp.maximum(m, jnp.max(s, axis=1, keepdims=True))
                alpha = jnp.exp(m - m_new)
                pr = jnp.exp(s - m_new)
                l_new = alpha * l + jnp.sum(pr, axis=1, keepdims=True)
                vv = v_ref[pl.ds(k0, KC), pl.ds(l0, V7X_LANES)]
                acc_new = alpha * acc + jnp.dot(pr.astype(jnp.bfloat16), vv, preferred_element_type=f32)
                return m_new, l_new, acc_new

            carry = (jnp.full((TQ, 1), MASK_NEG, f32), jnp.zeros((TQ, 1), f32),
                     jnp.zeros((TQ, V7X_LANES), f32))
            far_bias = efar_ref[h]
            carry = lax.fori_loop(0, jnp.maximum(i - 1, 0),
                                  lambda c, cr: step(c, far_bias, cr), carry)
            carry = lax.cond(i >= 1,
                             lambda cr: step(i - 1, enear_ref[h, :, 0:KC], cr),
                             lambda cr: cr, carry)
            m, l, acc = step(i, enear_ref[h, :, KC:2 * KC], carry)
            outs.append(acc / l)
        o_pair = jnp.where((lane // A_HEAD_DIM) == 0, outs[0], outs[1])
        o_ref[:, pl.ds(l0, V7X_LANES)] = o_pair.astype(o_ref.dtype)
        return 0

    lax.fori_loop(0, A_HEADS // 2, head_pair, 0)


def _dsa(q, v, iq, iw, kT, ikT, rel_bias, B, S):
    T = q.shape[0]
    TQ = min(ATT_Q, S)
    nQ = S // TQ
    topk = min(TOPK_MAX, S // 4)
    buckets = _t5_bucket_table(2 * TQ + 1)
    assert np.all(_t5_bucket_table(S + 1)[TQ + 1:] == REL_BUCKETS - 1)
    r = np.arange(TQ)[:, None]
    j = np.arange(2 * TQ)[None, :]
    dist = np.maximum(r + TQ - j, 0)
    onehot = (jnp.asarray(buckets[dist], jnp.int32)[None]
              == jnp.arange(REL_BUCKETS, dtype=jnp.int32)[:, None, None]).astype(jnp.float32)
    enear = jnp.einsum('nh,nrj->hrj', rel_bias.astype(jnp.float32), onehot,
                       precision=lax.Precision.HIGHEST)
    efar = rel_bias[REL_BUCKETS - 1].astype(jnp.float32)

    return pl.pallas_call(
        functools.partial(_dsa_kernel, topk=topk),
        grid=(B, nQ),
        in_specs=[
            pl.BlockSpec(memory_space=pltpu.SMEM),
            pl.BlockSpec((TQ, A_WIDTH), lambda b, i: (b * nQ + i, 0)),
            pl.BlockSpec((1, A_WIDTH, S), lambda b, i: (b, 0, 0)),
            pl.BlockSpec((S, A_WIDTH), lambda b, i: (b, 0)),
            pl.BlockSpec((TQ, IDX_WIDTH), lambda b, i: (b * nQ + i, 0)),
            pl.BlockSpec((TQ, IDX_HEADS), lambda b, i: (b * nQ + i, 0)),
            pl.BlockSpec((1, IDX_WIDTH, S), lambda b, i: (b, 0, 0)),
            pl.BlockSpec((A_HEADS, TQ, 2 * TQ), lambda b, i: (0, 0, 0)),
        ],
        out_specs=pl.BlockSpec((TQ, A_WIDTH), lambda b, i: (b * nQ + i, 0)),
        scratch_shapes=[pltpu.VMEM((TQ, S), jnp.float32)],
        out_shape=jax.ShapeDtypeStruct((T, A_WIDTH), jnp.bfloat16),
        compiler_params=_cparams(("parallel", "arbitrary")),
        name="dsa",
    )(efar, q, kT, v, iq, iw, ikT, enear)


def _hgrn_kernel(bq_ref, bf_ref, bi_ref, bg_ref, lb_ref, gain_ref, o_ref,
                 st_ref, b_scr, q_scr, k_scr, v_scr, oi_scr):
    R = bq_ref.shape[0]
    C = HGRN_CHUNK
    f32 = jnp.float32
    bf16 = jnp.bfloat16
    h = pl.program_id(1)

    @pl.when(pl.program_id(2) == 0)
    def _():
        st_ref[...] = jnp.zeros_like(st_ref)

    lb = lb_ref[pl.ds(h, 1), :]
    gain = gain_ref[pl.ds(h, 1), :]
    tril_incl = jnp.where(lax.broadcasted_iota(jnp.int32, (C, C), 1)
                          <= lax.broadcasted_iota(jnp.int32, (C, C), 0), 1.0, 0.0)
    srow = lax.broadcasted_iota(jnp.int32, (C, B_KEY_DIM), 0)

    def chunk(ci, _):
        r0 = pl.multiple_of(ci * C, C)
        f = lb + (1.0 - lb) * _sigmoid(bf_ref[pl.ds(r0, C), :])
        g = jnp.log(f)
        kk = 1.0 - f
        qr = bq_ref[pl.ds(r0, C), :]
        qq = qr * _sigmoid(qr) * (B_KEY_DIM ** -0.5)
        vv = bi_ref[pl.ds(r0, C), :]
        b = jnp.dot(tril_incl, g, preferred_element_type=f32, precision=lax.Precision.HIGHEST)
        b_last = b[C - 1:C, :]
        st = st_ref[...]
        o_inter = lax.dot_general((qq * jnp.exp(b)).astype(bf16), st.astype(bf16),
                                  (((1,), (1,)), ((), ())), preferred_element_type=f32)
        b_scr[...] = b
        q_scr[...] = qq
        k_scr[...] = kk
        v_scr[...] = vv

        safe = jnp.min(b_last) >= -HGRN_SAFE_DECAY

        @pl.when(safe)
        def _():
            qd = (qq * jnp.exp(b)).astype(bf16)
            kd = (kk * jnp.exp(-b)).astype(bf16)
            att = lax.dot_general(qd, kd, (((1,), (1,)), ((), ())), preferred_element_type=f32)
            att = att * tril_incl
            oi_scr[...] = jnp.dot(att.astype(bf16), vv.astype(bf16), preferred_element_type=f32)

        @pl.when(jnp.logical_not(safe))
        def _():
            def row(t, _):
                bt = b_scr[pl.ds(t, 1), :]
                qt = q_scr[pl.ds(t, 1), :]
                ex = jnp.where(srow <= t, bt - b_scr[...], -jnp.inf)
                a = jnp.sum(qt * k_scr[...] * jnp.exp(ex), axis=1, keepdims=True)
                oi_scr[pl.ds(t, 1), :] = jnp.sum(a * v_scr[...], axis=0, keepdims=True)
                return 0
            lax.fori_loop(0, C, row, 0)

        o = o_inter + oi_scr[...]
        kd_last = (kk * jnp.exp(b_last - b)).astype(bf16)
        upd = lax.dot_general(vv.astype(bf16), kd_last, (((0,), (0,)), ((), ())),
                              preferred_element_type=f32)
        st_ref[...] = st * jnp.exp(b_last) + upd
        og = bg_ref[pl.ds(r0, C), :]
        y = _rms(o, gain) * (og * _sigmoid(og))
        o_ref[pl.ds(r0, C), :] = y.astype(o_ref.dtype)
        return 0

    lax.fori_loop(0, R // C, chunk, 0)


def _hgrn(bq, bf, bi, bg, lb, gain, B, S):
    T = bq.shape[0]
    R = min(HGRN_ROWS, S)
    nR = S // R
    C = HGRN_CHUNK
    blk = pl.BlockSpec((R, B_KEY_DIM), lambda b, h, c: (b * nR + c, h))
    small = pl.BlockSpec((B_HEADS, B_KEY_DIM), lambda b, h, c: (0, 0))
    f32 = jnp.float32
    return pl.pallas_call(
        _hgrn_kernel,
        grid=(B, B_HEADS, nR),
        in_specs=[blk, blk, blk, blk, small, small],
        out_specs=blk,
        out_shape=jax.ShapeDtypeStruct((T, B_WIDTH), jnp.bfloat16),
        scratch_shapes=[pltpu.VMEM((B_VAL_DIM, B_KEY_DIM), f32)] +
                       [pltpu.VMEM((C, B_KEY_DIM), f32) for _ in range(5)],
        compiler_params=_cparams(("parallel", "parallel", "arbitrary")),
        name="hgrn",
    )(bq, bf, bi, bg, lb, gain)


def _merge_kernel(x_ref, ya_ref, yb_ref, ga_ref, gb_ref, wa_ref, wb_ref, wo_ref, g_ref, wr_ref, br_ref,
                  x1_ref, xn_ref, lg_ref):
    f32 = jnp.float32
    ma = jnp.dot(ya_ref[...], wa_ref[...], preferred_element_type=f32)
    mb = jnp.dot(yb_ref[...], wb_ref[...], preferred_element_type=f32)
    merged = ga_ref[...].astype(f32) * ma + gb_ref[...].astype(f32) * mb
    x1 = x_ref[...] + jnp.dot(merged.astype(jnp.bfloat16), wo_ref[...], preferred_element_type=f32)
    x1_ref[...] = x1
    hn = _rms(x1, g_ref[...])
    xn_ref[...] = _pack_bf16_pairs(hn)
    lg_ref[...] = jnp.dot(hn, wr_ref[...], preferred_element_type=f32,
                          precision=lax.Precision.HIGHEST) + br_ref[...]


def _merge(x2, ya, yb, ga, gb, w_up_a, w_up_b, w_out, gain, w_router, b_router):
    T, D = x2.shape
    R = min(PROJ_ROWS, T)
    bf = jnp.bfloat16
    ins = [x2, ya, yb, ga, gb, w_up_a.astype(bf), w_up_b.astype(bf), w_out.astype(bf),
           gain.reshape(1, D), w_router, b_router.reshape(1, N_EXPERTS)]
    row = lambda n: pl.BlockSpec((R, n), lambda i: (i, 0))
    full = lambda a: pl.BlockSpec(a.shape, lambda i: (0,) * a.ndim)
    in_specs = [row(D), row(A_WIDTH), row(B_WIDTH), row(D), row(D)] + [full(a) for a in ins[5:]]
    return pl.pallas_call(
        _merge_kernel,
        grid=(T // R,),
        in_specs=in_specs,
        out_specs=[row(D), row(D // 2), row(N_EXPERTS)],
        out_shape=[jax.ShapeDtypeStruct((T, D), jnp.float32), jax.ShapeDtypeStruct((T, D // 2), jnp.uint32),
                   jax.ShapeDtypeStruct((T, N_EXPERTS), jnp.float32)],
        compiler_params=_cparams(("parallel",)),
        name="merge",
    )(*ins)


def _route_kernel(lg_ref, eidx_ref, gate_ref, rank_ref, cnt_ref, run_ref):
    R = lg_ref.shape[0]
    f32 = jnp.float32

    @pl.when(pl.program_id(0) == 0)
    def _():
        run_ref[...] = jnp.zeros_like(run_ref)

    lg = lg_ref[...]
    lane = lax.broadcasted_iota(jnp.int32, (R, N_EXPERTS), 1)
    work = lg
    onehots, vals, idxs = [], [], []
    for _ in range(TOP_K):
        m = jnp.max(work, axis=1, keepdims=True)
        idx = jnp.min(jnp.where(work == m, lane, N_EXPERTS), axis=1, keepdims=True)
        oh = lane == idx
        onehots.append(oh)
        vals.append(m)
        idxs.append(idx)
        work = jnp.where(oh, -jnp.inf, work)
    ex = [jnp.exp(v - vals[0]) for v in vals]
    den = ex[0] + ex[1] + ex[2] + ex[3]
    chosen = jnp.where(onehots[0] | onehots[1] | onehots[2] | onehots[3], 1.0, 0.0)
    strict = jnp.where(lax.broadcasted_iota(jnp.int32, (R, R), 1)
                       < lax.broadcasted_iota(jnp.int32, (R, R), 0), 1.0, 0.0).astype(jnp.bfloat16)
    before = jnp.dot(strict, chosen.astype(jnp.bfloat16), preferred_element_type=f32) + run_ref[...]
    lane4 = lax.broadcasted_iota(jnp.int32, (R, TOP_K), 1)
    eidx = jnp.zeros((R, TOP_K), jnp.int32)
    gate = jnp.zeros((R, TOP_K), f32)
    rank = jnp.zeros((R, TOP_K), f32)
    for k in range(TOP_K):
        eidx = jnp.where(lane4 == k, idxs[k], eidx)
        gate = jnp.where(lane4 == k, ex[k] / den, gate)
        rk = jnp.sum(jnp.where(onehots[k], before, 0.0), axis=1, keepdims=True)
        rank = jnp.where(lane4 == k, rk, rank)
    eidx_ref[...] = eidx
    gate_ref[...] = gate
    rank_ref[...] = rank.astype(jnp.int32)
    run_ref[...] = run_ref[...] + jnp.sum(chosen, axis=0, keepdims=True)
    cnt_ref[...] = run_ref[...].astype(jnp.int32)


def _route(logits):
    T = logits.shape[0]
    R = min(ROUTE_ROWS, T)
    row = lambda n: pl.BlockSpec((R, n), lambda i: (i, 0))
    return pl.pallas_call(
        _route_kernel,
        grid=(T // R,),
        in_specs=[row(N_EXPERTS)],
        out_specs=[row(TOP_K), row(TOP_K), row(TOP_K), pl.BlockSpec((1, N_EXPERTS), lambda i: (0, 0))],
        out_shape=[jax.ShapeDtypeStruct((T, TOP_K), jnp.int32), jax.ShapeDtypeStruct((T, TOP_K), jnp.float32),
                   jax.ShapeDtypeStruct((T, TOP_K), jnp.int32), jax.ShapeDtypeStruct((1, N_EXPERTS), jnp.int32)],
        scratch_shapes=[pltpu.VMEM((1, N_EXPERTS), jnp.float32)],
        compiler_params=_cparams(("arbitrary",)),
        name="route",
    )(logits)


def _dispatch_kernel(dest_ref, x_ref, zsrc_ref, o_ref, sem):
    R = x_ref.shape[0]
    n_blocks = o_ref.shape[0] // EXPERT_ROWS

    def zcopy(e):
        return pltpu.make_async_copy(zsrc_ref, o_ref.at[pl.ds(e * EXPERT_ROWS, EXPERT_ROWS)], sem)

    @pl.when(pl.program_id(0) == 0)
    def _():
        def start(e, _):
            zcopy(e).start()
            return 0
        lax.fori_loop(0, n_blocks, start, 0)

        def wait(e, _):
            zcopy(e).wait()
            return 0
        lax.fori_loop(0, n_blocks, wait, 0)

    def rcopy(j):
        return pltpu.make_async_copy(x_ref.at[pl.ds(j // TOP_K, 1)],
                                     o_ref.at[pl.ds(dest_ref[0, 0, j], 1)], sem)

    _windowed_copies(rcopy, R * TOP_K)


def _windowed_copies(copy_of, n):
    def issue(j, _):
        copy_of(j).start()

        @pl.when(j >= DMA_WINDOW)
        def _():
            copy_of(j - DMA_WINDOW).wait()
        return 0
    lax.fori_loop(0, n, issue, 0)

    def drain(j, _):
        copy_of(j).wait()
        return 0
    lax.fori_loop(max(n - DMA_WINDOW, 0), n, drain, 0)


def _dispatch(xw, dest, P):
    T, W = xw.shape
    R = min(DISPATCH_ROWS, T)
    n = T // R
    dest3 = dest.reshape(n, 1, R * TOP_K)
    zsrc = jnp.zeros((EXPERT_ROWS, W), xw.dtype)
    return pl.pallas_call(
        _dispatch_kernel,
        grid=(n,),
        in_specs=[pl.BlockSpec((1, 1, R * TOP_K), lambda i: (i, 0, 0), memory_space=pltpu.SMEM),
                  pl.BlockSpec((R, W), lambda i: (i, 0)),
                  pl.BlockSpec((EXPERT_ROWS, W), lambda i: (0, 0))],
        out_specs=pl.BlockSpec(memory_space=pl.ANY),
        out_shape=jax.ShapeDtypeStruct((P, W), xw.dtype),
        scratch_shapes=[pltpu.SemaphoreType.DMA],
        compiler_params=_cparams(("arbitrary",)),
        name="dispatch",
    )(dest3, xw, zsrc)


def _experts_kernel(be_ref, nb_ref, x_ref, wgu_ref, bgu_ref, wd_ref, bd_ref, o_ref):
    f32 = jnp.float32
    d_ff = wd_ref.shape[1]

    @pl.when(pl.program_id(0) < nb_ref[0])
    def _():
        x_hi, x_lo = _unpack_bf16_pairs(x_ref[...])
        half = x_hi.shape[1]
        gu = (jnp.dot(x_hi, wgu_ref[0, :half, :], preferred_element_type=f32)
              + jnp.dot(x_lo, wgu_ref[0, half:, :], preferred_element_type=f32) + bgu_ref[0])
        gate = jnp.minimum(gu[:, :d_ff], SWIGLU_LIMIT)
        lin = jnp.clip(gu[:, d_ff:], -SWIGLU_LIMIT, SWIGLU_LIMIT)
        act = (lin + 1.0) * gate * _sigmoid(SWIGLU_ALPHA * gate)
        y = jnp.dot(act.astype(jnp.bfloat16), wd_ref[0], preferred_element_type=f32) + bd_ref[0]
        o_ref[...] = y

    @pl.when(pl.program_id(0) >= nb_ref[0])
    def _():
        o_ref[...] = jnp.zeros_like(o_ref)


def _experts(xs, block_expert, n_used, w_gu, b_gu, w_down, b_down):
    P, W = xs.shape
    E, D, F2 = w_gu.shape
    nb = P // EXPERT_ROWS
    grid_spec = pltpu.PrefetchScalarGridSpec(
        num_scalar_prefetch=2,
        grid=(nb,),
        in_specs=[
            pl.BlockSpec((EXPERT_ROWS, W), lambda i, be, nu: (i, 0)),
            pl.BlockSpec((1, D, F2), lambda i, be, nu: (be[i], 0, 0)),
            pl.BlockSpec((1, 1, F2), lambda i, be, nu: (be[i], 0, 0)),
            pl.BlockSpec((1, F2 // 2, D), lambda i, be, nu: (be[i], 0, 0)),
            pl.BlockSpec((1, 1, D), lambda i, be, nu: (be[i], 0, 0)),
        ],
        out_specs=pl.BlockSpec((EXPERT_ROWS, D), lambda i, be, nu: (i, 0)),
    )
    return pl.pallas_call(
        _experts_kernel,
        grid_spec=grid_spec,
        out_shape=jax.ShapeDtypeStruct((P, D), jnp.float32),
        compiler_params=_cparams(("arbitrary",)),
        name="experts",
    )(block_expert, n_used, xs, w_gu.astype(jnp.bfloat16), b_gu.reshape(E, 1, F2),
      w_down.astype(jnp.bfloat16), b_down.reshape(E, 1, D))


def _combine_kernel(dest_ref, y_ref, x1_ref, gate_ref, g_ref, o_ref, buf, sem):
    R = x1_ref.shape[0]

    def rcopy(j):
        return pltpu.make_async_copy(y_ref.at[pl.ds(dest_ref[0, 0, j], 1)],
                                     buf.at[j % TOP_K, pl.ds(j // TOP_K, 1)], sem)

    _windowed_copies(rcopy, R * TOP_K)

    gate = gate_ref[...]
    y = x1_ref[...]
    for k in range(TOP_K):
        y = y + gate[:, k:k + 1] * buf[k]
    o_ref[...] = _rms(y, g_ref[...])


def _combine(y_buf, dest, x1, gates, gain):
    T, D = x1.shape
    R = min(DISPATCH_ROWS, T)
    n = T // R
    dest3 = dest.reshape(n, 1, R * TOP_K)
    row = lambda w: pl.BlockSpec((R, w), lambda i: (i, 0))
    return pl.pallas_call(
        _combine_kernel,
        grid=(n,),
        in_specs=[pl.BlockSpec((1, 1, R * TOP_K), lambda i: (i, 0, 0), memory_space=pltpu.SMEM),
                  pl.BlockSpec(memory_space=pl.ANY),
                  row(D), row(TOP_K),
                  pl.BlockSpec((1, D), lambda i: (0, 0))],
        out_specs=row(D),
        out_shape=jax.ShapeDtypeStruct((T, D), jnp.float32),
        scratch_shapes=[pltpu.VMEM((TOP_K, R, D), jnp.float32), pltpu.SemaphoreType.DMA],
        compiler_params=_cparams(("arbitrary",)),
        name="combine",
    )(dest3, y_buf, x1, gates, gain.reshape(1, D))


def _moe_plan(eidx, rank, counts, A):
    counts = counts.reshape(N_EXPERTS)
    padded = (counts + EXPERT_ROWS - 1) // EXPERT_ROWS * EXPERT_ROWS
    pad_ends = jnp.cumsum(padded)
    pad_starts = pad_ends - padded
    n_blocks = -(-A // EXPERT_ROWS) + N_EXPERTS
    dest = pad_starts[eidx] + rank
    block_start = jnp.arange(n_blocks, dtype=pad_ends.dtype) * EXPERT_ROWS
    block_expert = jnp.minimum(jnp.sum(pad_ends[None, :] <= block_start[:, None], axis=1),
                               N_EXPERTS - 1).astype(jnp.int32)
    n_used = (pad_ends[-1] // EXPERT_ROWS).astype(jnp.int32).reshape(1)
    return dest.astype(jnp.int32), block_expert, n_used, n_blocks


def kernel(x, w_in, w_up_a, w_up_b, w_out, norm_mix, norm_ffn, norm_final, hgrn_norm,
           lb_logits, rel_bias, w_router, b_router, w_gu, b_gu, w_down, b_down):
    B, S, D = x.shape
    T = B * S
    assert w_in.shape[0] == 1, "the final rmsnorm is fused into the single layer's combine stage"
    lb_all = jnp.cumsum(jax.nn.softmax(lb_logits.astype(jnp.float32), axis=0), axis=0)
    x2 = x.reshape(T, D)
    (q, v, iq, iw, kT, ikT, bq, bf, bi, bg, ga, gb) = _inproj(x2, norm_mix[0], w_in[0], B, S)
    ya = _dsa(q, v, iq, iw, kT, ikT, rel_bias, B, S)
    yb = _hgrn(bq, bf, bi, bg, lb_all[0].reshape(B_HEADS, B_KEY_DIM), hgrn_norm[0], B, S)
    x1, xn, logits = _merge(x2, ya, yb, ga, gb, w_up_a[0], w_up_b[0], w_out[0], norm_ffn[0],
                            w_router[0], b_router[0])
    eidx, gates, rank, counts = _route(logits)
    dest, block_expert, n_used, n_blocks = _moe_plan(eidx, rank, counts, T * TOP_K)
    P = n_blocks * EXPERT_ROWS
    xs = _dispatch(xn, dest, P)
    y_buf = _experts(xs, block_expert, n_used, w_gu[0], b_gu[0], w_down[0], b_down[0])
    out = _combine(y_buf, dest, x1, gates, norm_final)
    return out.reshape(B, S, D)
```

```python
import functools
import math

import numpy as np
import jax
import jax.numpy as jnp
from jax import lax
from jax.experimental import pallas as pl
from jax.experimental.pallas import tpu as pltpu

A_HEADS = 8
A_HEAD_DIM = 64
IDX_HEADS = 8
IDX_DIM = 32
TOPK_MAX = 256
REL_BUCKETS = 32
REL_MAX_DIST = 128
B_HEADS = 4
B_KEY_DIM = 128
B_VAL_DIM = 128
N_EXPERTS = 32
TOP_K = 4
SWIGLU_LIMIT = 7.0
SWIGLU_ALPHA = 1.702
EPS = 1e-6

A_WIDTH = A_HEADS * A_HEAD_DIM
B_WIDTH = B_HEADS * B_VAL_DIM
IDX_WIDTH = IDX_HEADS * IDX_DIM

V7X_LANES = 128
V7X_SUBLANES = 8
V7X_VMEM_LIMIT_BYTES = 56 * 1024 * 1024

PROJ_ROWS = 512
ATT_Q = 256
ATT_KC = 128
HGRN_ROWS = 512
HGRN_CHUNK = 64
HGRN_SAFE_DECAY = 70.0
ROUTE_ROWS = 512
EXPERT_ROWS = 256
DISPATCH_ROWS = 512
DMA_WINDOW = 64
MASK_NEG = -1e30
BISECT_FAST_ITERS = 26


def _cparams(dims):
    return pltpu.CompilerParams(dimension_semantics=dims, vmem_limit_bytes=V7X_VMEM_LIMIT_BYTES)


def _rms(x, gain):
    return x * lax.rsqrt(jnp.mean(x * x, axis=-1, keepdims=True) + EPS) * gain


def _sigmoid(x):
    return 1.0 / (1.0 + jnp.exp(-x))


def _pack_bf16_pairs(x):
    n = x.shape[1] // 2
    as_bits = lambda v: lax.bitcast_convert_type(v.astype(jnp.bfloat16).astype(jnp.float32), jnp.uint32)
    return (as_bits(x[:, :n]) & jnp.uint32(0xFFFF0000)) | (as_bits(x[:, n:]) >> 16)


def _unpack_bf16_pairs(w):
    hi = lax.bitcast_convert_type(w & jnp.uint32(0xFFFF0000), jnp.float32).astype(jnp.bfloat16)
    lo = lax.bitcast_convert_type(w << 16, jnp.float32).astype(jnp.bfloat16)
    return hi, lo


def _fold_rows(x, op):
    return op(x.reshape(x.shape[0] // V7X_SUBLANES, V7X_SUBLANES, x.shape[1]), axis=0)


def _inproj_kernel(x_ref, g_ref, wk_ref, wik_ref, wqT_ref, wvT_ref, wiqT_ref, wiwT_ref, wb_ref, wg_ref,
                   k_ref, ik_ref, qT_ref, vT_ref, iqT_ref, iwT_ref, bq_ref, bf_ref, bi_ref, bg_ref,
                   ga_ref, gb_ref):
    x = x_ref[...]
    hn = _rms(x, g_ref[...]).astype(jnp.bfloat16)

    def mm(w_ref):
        return jnp.dot(hn, w_ref[...], preferred_element_type=jnp.float32)

    def mm_t(w_ref):
        return lax.dot_general(w_ref[...], hn, (((1,), (1,)), ((), ())),
                               preferred_element_type=jnp.float32)

    k_ref[...] = mm(wk_ref).astype(jnp.bfloat16)
    ik_ref[...] = mm(wik_ref).astype(jnp.bfloat16)
    qT_ref[0] = (mm_t(wqT_ref) * (A_HEAD_DIM ** -0.5)).astype(jnp.bfloat16)
    vT_ref[0] = mm_t(wvT_ref).astype(jnp.bfloat16)
    iqT_ref[0] = mm_t(wiqT_ref).astype(jnp.bfloat16)
    iwT_ref[0] = mm_t(wiwT_ref) * ((IDX_HEADS * IDX_DIM) ** -0.5)
    hb = mm(wb_ref)
    bq_ref[...] = hb[:, 0 * B_WIDTH:1 * B_WIDTH]
    bf_ref[...] = hb[:, 1 * B_WIDTH:2 * B_WIDTH]
    bi_ref[...] = hb[:, 2 * B_WIDTH:3 * B_WIDTH]
    bg_ref[...] = hb[:, 3 * B_WIDTH:4 * B_WIDTH]
    d = ga_ref.shape[-1]
    hg = mm(wg_ref)
    ga_ref[...] = _sigmoid(hg[:, :d]).astype(jnp.bfloat16)
    gb_ref[...] = _sigmoid(hg[:, d:]).astype(jnp.bfloat16)


def _inproj(x2, gain, w_in, B, S):
    T, D = x2.shape
    R = min(PROJ_ROWS, S)
    nS = S // R
    o = np.cumsum((0, A_WIDTH, A_WIDTH, A_WIDTH, IDX_WIDTH, IDX_HEADS, IDX_DIM,
                   B_WIDTH, B_WIDTH, B_WIDTH, B_WIDTH, D, D))
    bf = jnp.bfloat16
    wqT = w_in[:, o[0]:o[1]].T.astype(bf)
    wk = w_in[:, o[1]:o[2]].astype(bf)
    wvT = w_in[:, o[2]:o[3]].T.astype(bf)
    wiqT = w_in[:, o[3]:o[4]].T.astype(bf)
    wiwT = w_in[:, o[4]:o[5]].T.astype(bf)
    wik = w_in[:, o[5]:o[6]].astype(bf)
    wb = w_in[:, o[6]:o[10]].astype(bf)
    wg = w_in[:, o[10]:o[12]].astype(bf)

    def full(a):
        return pl.BlockSpec(a.shape, lambda b, i: (0,) * a.ndim)

    row = lambda n: pl.BlockSpec((R, n), lambda b, i: (b * nS + i, 0))
    colT = lambda n: pl.BlockSpec((1, n, R), lambda b, i: (b, 0, i))
    f32 = jnp.float32
    outs = [
        (jax.ShapeDtypeStruct((T, A_WIDTH), bf), row(A_WIDTH)),
        (jax.ShapeDtypeStruct((T, IDX_DIM), bf), row(IDX_DIM)),
        (jax.ShapeDtypeStruct((B, A_WIDTH, S), bf), colT(A_WIDTH)),
        (jax.ShapeDtypeStruct((B, A_WIDTH, S), bf), colT(A_WIDTH)),
        (jax.ShapeDtypeStruct((B, IDX_WIDTH, S), bf), colT(IDX_WIDTH)),
        (jax.ShapeDtypeStruct((B, IDX_HEADS, S), f32), colT(IDX_HEADS)),
        (jax.ShapeDtypeStruct((T, B_WIDTH), f32), row(B_WIDTH)),
        (jax.ShapeDtypeStruct((T, B_WIDTH), f32), row(B_WIDTH)),
        (jax.ShapeDtypeStruct((T, B_WIDTH), f32), row(B_WIDTH)),
        (jax.ShapeDtypeStruct((T, B_WIDTH), f32), row(B_WIDTH)),
        (jax.ShapeDtypeStruct((T, D), bf), row(D)),
        (jax.ShapeDtypeStruct((T, D), bf), row(D)),
    ]
    ins = [x2, gain.reshape(1, D), wk, wik, wqT, wvT, wiqT, wiwT, wb, wg]
    in_specs = [row(D)] + [full(a) for a in ins[1:]]
    return pl.pallas_call(
        _inproj_kernel,
        grid=(B, nS),
        in_specs=in_specs,
        out_specs=[s for _, s in outs],
        out_shape=[s for s, _ in outs],
        compiler_params=_cparams(("parallel", "parallel")),
        name="inproj",
    )(*ins)


def _t5_bucket_table(n):
    d = np.arange(n)
    max_exact = REL_BUCKETS // 2
    nf = np.maximum(d, 1).astype(np.float64)
    large = max_exact + (np.log(nf / max_exact) / math.log(REL_MAX_DIST / max_exact)
                         * (REL_BUCKETS - max_exact)).astype(np.int32)
    large = np.minimum(large, REL_BUCKETS - 1)
    return np.where(d < max_exact, d, large)


def _dsa_kernel(qT_ref, k_ref, vT_ref, iqT_ref, iwT_ref, ik_ref, enear_ref,
                o_ref, sc_ref, *, topk):
    TQ = qT_ref.shape[2]
    KC = TQ
    i = pl.program_id(1)
    nch = i + 1
    q0 = i * TQ
    f32 = jnp.float32
    bf16 = jnp.bfloat16
    key_id = lax.broadcasted_iota(jnp.int32, (KC, TQ), 0)
    qry_id = lax.broadcasted_iota(jnp.int32, (KC, TQ), 1)

    def col_reduce(x, op):
        return op(_fold_rows(x, op), axis=0, keepdims=True)

    iw = iwT_ref[0]

    def score_chunk(c, carry):
        rmin, rmax = carry
        k0 = pl.multiple_of(c * KC, KC)
        ik = ik_ref[pl.ds(k0, KC), :]
        acc = jnp.zeros((KC, TQ), f32)
        for h in range(IDX_HEADS):
            sh = jnp.dot(ik, iqT_ref[0, h * IDX_DIM:(h + 1) * IDX_DIM, :], preferred_element_type=f32)
            acc = acc + jnp.maximum(sh, 0.0) * iw[h:h + 1, :]
        valid = (k0 + key_id) <= (q0 + qry_id)
        sc_ref[pl.ds(k0, KC), :] = jnp.where(valid, acc, MASK_NEG)
        rmin = jnp.minimum(rmin, _fold_rows(jnp.where(valid, acc, -MASK_NEG), jnp.min))
        rmax = jnp.maximum(rmax, _fold_rows(jnp.where(valid, acc, MASK_NEG), jnp.max))
        return rmin, rmax

    rmin8, rmax8 = lax.fori_loop(
        0, nch, score_chunk,
        (jnp.full((V7X_SUBLANES, TQ), -MASK_NEG, f32), jnp.full((V7X_SUBLANES, TQ), MASK_NEG, f32)))
    rmin = jnp.min(rmin8, axis=0, keepdims=True)
    rmax = jnp.max(rmax8, axis=0, keepdims=True)

    def count_where(pred_fn):
        def body(c, acc):
            k0 = pl.multiple_of(c * KC, KC)
            blk = sc_ref[pl.ds(k0, KC), :]
            return acc + _fold_rows(jnp.where(pred_fn(blk), 1.0, 0.0), jnp.sum)
        acc = lax.fori_loop(0, nch, body, jnp.zeros((V7X_SUBLANES, TQ), f32))
        return jnp.sum(acc, axis=0, keepdims=True)

    def band_min_max(lo, hi):
        def body(c, carry):
            bmin, bmax = carry
            k0 = pl.multiple_of(c * KC, KC)
            blk = sc_ref[pl.ds(k0, KC), :]
            bmin = jnp.minimum(bmin, _fold_rows(jnp.where(blk >= lo, blk, -MASK_NEG), jnp.min))
            bmax = jnp.maximum(bmax, _fold_rows(jnp.where(blk < hi, blk, MASK_NEG), jnp.max))
            return bmin, bmax
        bmin8, bmax8 = lax.fori_loop(
            0, nch, body,
            (jnp.full((V7X_SUBLANES, TQ), -MASK_NEG, f32), jnp.full((V7X_SUBLANES, TQ), MASK_NEG, f32)))
        return jnp.min(bmin8, axis=0, keepdims=True), jnp.max(bmax8, axis=0, keepdims=True)

    kf = float(topk)
    n_valid = (q0 + 1 + lax.broadcasted_iota(jnp.int32, (1, TQ), 1)).astype(f32)
    c_top = count_where(lambda blk: blk >= rmax)
    top_tie = c_top >= kf
    lo0 = jnp.where(top_tie, rmax, rmin)
    cnt0 = jnp.where(top_tie, c_top, n_valid)
    done0 = jnp.where(top_tie | (cnt0 <= kf), 1.0, 0.0)

    def cond(st):
        return jnp.min(st[-1]) < 0.5

    def body(st):
        it, lo, hi, cnt, chi, done = st

        def plain(_):
            half = lo + 0.5 * (hi - lo)
            stuck = (half <= lo) | (half >= hi)
            guess = lo + (hi - lo) * ((cnt - kf + 0.5) / (cnt - chi))
            use_guess = (it % 2 == 0) & (guess > lo) & (guess < hi)
            return lo, jnp.where(use_guess, guess, half), jnp.where(stuck, 1.0, 0.0)

        def snapped(_):
            bmin, bmax = band_min_max(lo, hi)
            mid = bmin + 0.5 * (bmax - bmin)
            mid = jnp.where(mid <= bmin, bmax, mid)
            return bmin, mid, jnp.where(bmax <= bmin, 1.0, 0.0)

        lo_s, mid, tie_f = lax.cond(it < BISECT_FAST_ITERS, plain, snapped, None)
        tie = tie_f > 0.5
        active = done < 0.5
        lo_s = jnp.where(active, lo_s, lo)
        c = count_where(lambda blk: blk >= mid)
        feas = c >= kf
        move = active & jnp.logical_not(tie)
        lo_n = jnp.where(move & feas, mid, lo_s)
        cnt_n = jnp.where(move & feas, c, cnt)
        hi_n = jnp.where(move & jnp.logical_not(feas), mid, hi)
        chi_n = jnp.where(move & jnp.logical_not(feas), c, chi)
        done_n = jnp.where((active & tie) | (cnt_n <= kf), 1.0, done)
        return it + 1, lo_n, hi_n, cnt_n, chi_n, done_n

    thr = lax.while_loop(cond, body, (jnp.int32(0), lo0, rmax, cnt0, c_top, done0))[1]

    need = kf - count_where(lambda blk: blk > thr)
    tril = jnp.where(lax.broadcasted_iota(jnp.int32, (KC, KC), 1)
                     <= lax.broadcasted_iota(jnp.int32, (KC, KC), 0), 1.0, 0.0).astype(bf16)

    def mask_chunk(c, run):
        k0 = pl.multiple_of(c * KC, KC)
        blk = sc_ref[pl.ds(k0, KC), :]
        eq = jnp.where(blk == thr, 1.0, 0.0)
        pref = jnp.dot(tril, eq.astype(bf16), preferred_element_type=f32)
        sel = (blk > thr) | ((eq > 0.5) & (run + pref <= need))
        sc_ref[pl.ds(k0, KC), :] = jnp.where(sel, 0.0, MASK_NEG)
        return run + pref[KC - 1:KC, :]

    lax.fori_loop(0, nch, mask_chunk, jnp.zeros((1, TQ), f32))

    AK = min(ATT_KC, TQ)
    per = TQ // AK
    head0_q = (lax.broadcasted_iota(jnp.int32, (V7X_LANES, TQ), 0) // A_HEAD_DIM) == 0

    def head_pair(p, _):
        l0 = pl.multiple_of(p * V7X_LANES, V7X_LANES)
        q_pair = qT_ref[0, pl.ds(l0, V7X_LANES), :]
        zq = jnp.zeros_like(q_pair)
        q_heads = (jnp.where(head0_q, q_pair, zq), jnp.where(head0_q, zq, q_pair))

        def step(c, bias_of, carry):
            k0 = pl.multiple_of(c * AK, AK)
            kp = k_ref[pl.ds(k0, AK), pl.ds(l0, V7X_LANES)]
            vp = vT_ref[0, pl.ds(l0, V7X_LANES), pl.ds(k0, AK)]
            msk = sc_ref[pl.ds(k0, AK), :]
            new = []
            for sub in range(2):
                m, l, acc = carry[sub]
                s = jnp.dot(kp, q_heads[sub], preferred_element_type=f32) + msk
                if bias_of is not None:
                    s = s + bias_of(sub)
                m_new = jnp.maximum(m, col_reduce(s, jnp.max))
                alpha = jnp.exp(m - m_new)
                pr = jnp.exp(s - m_new)
                l_new = alpha * l + col_reduce(pr, jnp.sum)
                acc_new = alpha * acc + jnp.dot(vp, pr.astype(bf16), preferred_element_type=f32)
                new.append((m_new, l_new, acc_new))
            return tuple(new)

        def near(block, first_chunk, carry):
            for jj in range(per):
                rows = slice(block * TQ + jj * AK, block * TQ + (jj + 1) * AK)
                carry = step(first_chunk + jj, lambda sub: enear_ref[2 * p + sub, rows, :], carry)
            return carry

        def far(blk, carry):
            for jj in range(per):
                carry = step(blk * per + jj, None, carry)
            return carry

        init = tuple((jnp.full((1, TQ), MASK_NEG, f32), jnp.zeros((1, TQ), f32),
                      jnp.zeros((V7X_LANES, TQ), f32)) for _ in range(2))
        carry = lax.fori_loop(0, jnp.maximum(i - 1, 0), far, init)
        carry = lax.cond(i >= 1, lambda cr: near(0, (i - 1) * per, cr), lambda cr: cr, carry)
        (_, l_a, acc_a), (_, l_b, acc_b) = near(1, i * per, carry)
        o_pair = jnp.where(head0_q, acc_a / l_a, acc_b / l_b)
        o_ref[:, pl.ds(l0, V7X_LANES)] = o_pair.T.astype(o_ref.dtype)
        return 0

    lax.fori_loop(0, A_HEADS // 2, head_pair, 0)


def _dsa(k, ik, qT, vT, iqT, iwT, rel_bias, B, S):
    T = k.shape[0]
    TQ = min(ATT_Q, S)
    nQ = S // TQ
    topk = min(TOPK_MAX, S // 4)
    buckets = _t5_bucket_table(2 * TQ + 1)
    assert np.all(_t5_bucket_table(S + 1)[TQ + 1:] == REL_BUCKETS - 1)
    j = np.arange(2 * TQ)[:, None]
    r = np.arange(TQ)[None, :]
    dist = np.maximum(r + TQ - j, 0)
    onehot = (jnp.asarray(buckets[dist], jnp.int32)[None]
              == jnp.arange(REL_BUCKETS, dtype=jnp.int32)[:, None, None]).astype(jnp.float32)
    rel = rel_bias.astype(jnp.float32) - rel_bias[REL_BUCKETS - 1].astype(jnp.float32)[None, :]
    enear = jnp.einsum('nh,njr->hjr', rel, onehot, precision=lax.Precision.HIGHEST)

    return pl.pallas_call(
        functools.partial(_dsa_kernel, topk=topk),
        grid=(B, nQ),
        in_specs=[
            pl.BlockSpec((1, A_WIDTH, TQ), lambda b, i: (b, 0, i)),
            pl.BlockSpec((S, A_WIDTH), lambda b, i: (b, 0)),
            pl.BlockSpec((1, A_WIDTH, S), lambda b, i: (b, 0, 0)),
            pl.BlockSpec((1, IDX_WIDTH, TQ), lambda b, i: (b, 0, i)),
            pl.BlockSpec((1, IDX_HEADS, TQ), lambda b, i: (b, 0, i)),
            pl.BlockSpec((S, IDX_DIM), lambda b, i: (b, 0)),
            pl.BlockSpec((A_HEADS, 2 * TQ, TQ), lambda b, i: (0, 0, 0)),
        ],
        out_specs=pl.BlockSpec((TQ, A_WIDTH), lambda b, i: (b * nQ + i, 0)),
        scratch_shapes=[pltpu.VMEM((S, TQ), jnp.float32)],
        out_shape=jax.ShapeDtypeStruct((T, A_WIDTH), jnp.bfloat16),
        compiler_params=_cparams(("parallel", "arbitrary")),
        name="dsa",
    )(qT, k, vT, iqT, iwT, ik, enear)


def _hgrn_kernel(bq_ref, bf_ref, bi_ref, bg_ref, lb_ref, gain_ref, o_ref,
                 st_ref, b_scr, q_scr, k_scr, v_scr, oi_scr):
    R = bq_ref.shape[0]
    C = HGRN_CHUNK
    nC = R // C
    f32 = jnp.float32
    bf16 = jnp.bfloat16
    h = pl.program_id(1)

    @pl.when(pl.program_id(2) == 0)
    def _():
        st_ref[...] = jnp.zeros_like(st_ref)

    lb = lb_ref[pl.ds(h, 1), :]
    gain = gain_ref[pl.ds(h, 1), :]
    tril_incl = jnp.where(lax.broadcasted_iota(jnp.int32, (C, C), 1)
                          <= lax.broadcasted_iota(jnp.int32, (C, C), 0), 1.0, 0.0)
    srow = lax.broadcasted_iota(jnp.int32, (C, B_KEY_DIM), 0)

    def gates(r0):
        f = lb + (1.0 - lb) * _sigmoid(bf_ref[pl.ds(r0, C), :])
        qr = bq_ref[pl.ds(r0, C), :]
        return jnp.log(f), 1.0 - f, qr * _sigmoid(qr) * (B_KEY_DIM ** -0.5), bi_ref[pl.ds(r0, C), :]

    def chunk(r0, st, factorised):
        g, kk, qq, vv = gates(r0)
        b = jnp.dot(tril_incl, g, preferred_element_type=f32, precision=lax.Precision.HIGHEST)
        b_last = b[C - 1:C, :]
        qd = (qq * jnp.exp(b)).astype(bf16)
        o_inter = lax.dot_general(qd, st.astype(bf16), (((1,), (1,)), ((), ())), preferred_element_type=f32)
        if factorised:
            kd = (kk * jnp.exp(-b)).astype(bf16)
            att = lax.dot_general(qd, kd, (((1,), (1,)), ((), ())), preferred_element_type=f32) * tril_incl
            o_intra = jnp.dot(att.astype(bf16), vv.astype(bf16), preferred_element_type=f32)
        else:
            b_scr[...] = b
            q_scr[...] = qq
            k_scr[...] = kk
            v_scr[...] = vv

            def row(t, _):
                bt = b_scr[pl.ds(t, 1), :]
                qt = q_scr[pl.ds(t, 1), :]
                ex = jnp.where(srow <= t, bt - b_scr[...], -jnp.inf)
                a = jnp.sum(qt * k_scr[...] * jnp.exp(ex), axis=1, keepdims=True)
                oi_scr[pl.ds(t, 1), :] = jnp.sum(a * v_scr[...], axis=0, keepdims=True)
                return 0
            lax.fori_loop(0, C, row, 0)
            o_intra = oi_scr[...]
        o = o_inter + o_intra
        kd_last = (kk * jnp.exp(b_last - b)).astype(bf16)
        upd = lax.dot_general(vv.astype(bf16), kd_last, (((0,), (0,)), ((), ())),
                              preferred_element_type=f32)
        og = bg_ref[pl.ds(r0, C), :]
        y = _rms(o, gain) * (og * _sigmoid(og))
        o_ref[pl.ds(r0, C), :] = y.astype(o_ref.dtype)
        return st * jnp.exp(b_last) + upd

    f_all = lb + (1.0 - lb) * _sigmoid(bf_ref[...])
    decay = jnp.sum(jnp.log(f_all).reshape(nC, C, B_KEY_DIM), axis=1)
    safe = jnp.min(decay) >= -HGRN_SAFE_DECAY

    @pl.when(safe)
    def _():
        st = st_ref[...]
        for c in range(nC):
            st = chunk(c * C, st, True)
        st_ref[...] = st

    @pl.when(jnp.logical_not(safe))
    def _():
        st_ref[...] = lax.fori_loop(
            0, nC, lambda c, st: chunk(pl.multiple_of(c * C, C), st, False), st_ref[...])


def _hgrn(bq, bf, bi, bg, lb, gain, B, S):
    T = bq.shape[0]
    R = min(HGRN_ROWS, S)
    nR = S // R
    C = HGRN_CHUNK
    blk = pl.BlockSpec((R, B_KEY_DIM), lambda b, h, c: (b * nR + c, h))
    small = pl.BlockSpec((B_HEADS, B_KEY_DIM), lambda b, h, c: (0, 0))
    f32 = jnp.float32
    return pl.pallas_call(
        _hgrn_kernel,
        grid=(B, B_HEADS, nR),
        in_specs=[blk, blk, blk, blk, small, small],
        out_specs=blk,
        out_shape=jax.ShapeDtypeStruct((T, B_WIDTH), jnp.bfloat16),
        scratch_shapes=[pltpu.VMEM((B_VAL_DIM, B_KEY_DIM), f32)] +
                       [pltpu.VMEM((C, B_KEY_DIM), f32) for _ in range(5)],
        compiler_params=_cparams(("parallel", "parallel", "arbitrary")),
        name="hgrn",
    )(bq, bf, bi, bg, lb, gain)


def _merge_kernel(x_ref, ya_ref, yb_ref, ga_ref, gb_ref, wa_ref, wb_ref, wo_ref, g_ref, wr_ref, br_ref,
                  x1_ref, xn_ref, lg_ref):
    f32 = jnp.float32
    ma = jnp.dot(ya_ref[...], wa_ref[...], preferred_element_type=f32)
    mb = jnp.dot(yb_ref[...], wb_ref[...], preferred_element_type=f32)
    merged = ga_ref[...].astype(f32) * ma + gb_ref[...].astype(f32) * mb
    x1 = x_ref[...] + jnp.dot(merged.astype(jnp.bfloat16), wo_ref[...], preferred_element_type=f32)
    x1_ref[...] = x1
    hn = _rms(x1, g_ref[...])
    xn_ref[...] = _pack_bf16_pairs(hn)
    lg_ref[...] = jnp.dot(hn, wr_ref[...], preferred_element_type=f32,
                          precision=lax.Precision.HIGHEST) + br_ref[...]


def _merge(x2, ya, yb, ga, gb, w_up_a, w_up_b, w_out, gain, w_router, b_router):
    T, D = x2.shape
    R = min(PROJ_ROWS, T)
    bf = jnp.bfloat16
    ins = [x2, ya, yb, ga, gb, w_up_a.astype(bf), w_up_b.astype(bf), w_out.astype(bf),
           gain.reshape(1, D), w_router, b_router.reshape(1, N_EXPERTS)]
    row = lambda n: pl.BlockSpec((R, n), lambda i: (i, 0))
    full = lambda a: pl.BlockSpec(a.shape, lambda i: (0,) * a.ndim)
    in_specs = [row(D), row(A_WIDTH), row(B_WIDTH), row(D), row(D)] + [full(a) for a in ins[5:]]
    return pl.pallas_call(
        _merge_kernel,
        grid=(T // R,),
        in_specs=in_specs,
        out_specs=[row(D), row(D // 2), row(N_EXPERTS)],
        out_shape=[jax.ShapeDtypeStruct((T, D), jnp.float32), jax.ShapeDtypeStruct((T, D // 2), jnp.uint32),
                   jax.ShapeDtypeStruct((T, N_EXPERTS), jnp.float32)],
        compiler_params=_cparams(("parallel",)),
        name="merge",
    )(*ins)


def _route_kernel(lg_ref, eidx_ref, gate_ref, rank_ref, cnt_ref, run_ref):
    R = lg_ref.shape[0]
    f32 = jnp.float32

    @pl.when(pl.program_id(0) == 0)
    def _():
        run_ref[...] = jnp.zeros_like(run_ref)

    lg = lg_ref[...]
    lane = lax.broadcasted_iota(jnp.int32, (R, N_EXPERTS), 1)
    work = lg
    onehots, vals, idxs = [], [], []
    for _ in range(TOP_K):
        m = jnp.max(work, axis=1, keepdims=True)
        idx = jnp.min(jnp.where(work == m, lane, N_EXPERTS), axis=1, keepdims=True)
        oh = lane == idx
        onehots.append(oh)
        vals.append(m)
        idxs.append(idx)
        work = jnp.where(oh, -jnp.inf, work)
    ex = [jnp.exp(v - vals[0]) for v in vals]
    den = ex[0] + ex[1] + ex[2] + ex[3]
    chosen = jnp.where(onehots[0] | onehots[1] | onehots[2] | onehots[3], 1.0, 0.0)
    strict = jnp.where(lax.broadcasted_iota(jnp.int32, (R, R), 1)
                       < lax.broadcasted_iota(jnp.int32, (R, R), 0), 1.0, 0.0).astype(jnp.bfloat16)
    before = jnp.dot(strict, chosen.astype(jnp.bfloat16), preferred_element_type=f32) + run_ref[...]
    lane4 = lax.broadcasted_iota(jnp.int32, (R, TOP_K), 1)
    eidx = jnp.zeros((R, TOP_K), jnp.int32)
    gate = jnp.zeros((R, TOP_K), f32)
    rank = jnp.zeros((R, TOP_K), f32)
    for k in range(TOP_K):
        eidx = jnp.where(lane4 == k, idxs[k], eidx)
        gate = jnp.where(lane4 == k, ex[k] / den, gate)
        rk = jnp.sum(jnp.where(onehots[k], before, 0.0), axis=1, keepdims=True)
        rank = jnp.where(lane4 == k, rk, rank)
    eidx_ref[...] = eidx
    gate_ref[...] = gate
    rank_ref[...] = rank.astype(jnp.int32)
    run_ref[...] = run_ref[...] + jnp.sum(chosen, axis=0, keepdims=True)
    cnt_ref[...] = run_ref[...].astype(jnp.int32)


def _route(logits):
    T = logits.shape[0]
    R = min(ROUTE_ROWS, T)
    row = lambda n: pl.BlockSpec((R, n), lambda i: (i, 0))
    return pl.pallas_call(
        _route_kernel,
        grid=(T // R,),
        in_specs=[row(N_EXPERTS)],
        out_specs=[row(TOP_K), row(TOP_K), row(TOP_K), pl.BlockSpec((1, N_EXPERTS), lambda i: (0, 0))],
        out_shape=[jax.ShapeDtypeStruct((T, TOP_K), jnp.int32), jax.ShapeDtypeStruct((T, TOP_K), jnp.float32),
                   jax.ShapeDtypeStruct((T, TOP_K), jnp.int32), jax.ShapeDtypeStruct((1, N_EXPERTS), jnp.int32)],
        scratch_shapes=[pltpu.VMEM((1, N_EXPERTS), jnp.float32)],
        compiler_params=_cparams(("arbitrary",)),
        name="route",
    )(logits)


def _windowed_copies(copy_of, n):
    def issue(j, _):
        copy_of(j).start()

        @pl.when(j >= DMA_WINDOW)
        def _():
            copy_of(j - DMA_WINDOW).wait()
        return 0
    lax.fori_loop(0, n, issue, 0)

    def drain(j, _):
        copy_of(j).wait()
        return 0
    lax.fori_loop(max(n - DMA_WINDOW, 0), n, drain, 0)


def _dispatch_kernel(dest_ref, x_ref, zsrc_ref, o_ref, sem):
    R = x_ref.shape[0]
    n_blocks = o_ref.shape[0] // EXPERT_ROWS

    def zcopy(e):
        return pltpu.make_async_copy(zsrc_ref, o_ref.at[pl.ds(e * EXPERT_ROWS, EXPERT_ROWS)], sem)

    @pl.when(pl.program_id(0) == 0)
    def _():
        def start(e, _):
            zcopy(e).start()
            return 0
        lax.fori_loop(0, n_blocks, start, 0)

        def wait(e, _):
            zcopy(e).wait()
            return 0
        lax.fori_loop(0, n_blocks, wait, 0)

    def rcopy(j):
        return pltpu.make_async_copy(x_ref.at[pl.ds(j // TOP_K, 1)],
                                     o_ref.at[pl.ds(dest_ref[0, 0, j], 1)], sem)

    _windowed_copies(rcopy, R * TOP_K)


def _dispatch(xw, dest, P):
    T, W = xw.shape
    R = min(DISPATCH_ROWS, T)
    n = T // R
    dest3 = dest.reshape(n, 1, R * TOP_K)
    zsrc = jnp.zeros((EXPERT_ROWS, W), xw.dtype)
    return pl.pallas_call(
        _dispatch_kernel,
        grid=(n,),
        in_specs=[pl.BlockSpec((1, 1, R * TOP_K), lambda i: (i, 0, 0), memory_space=pltpu.SMEM),
                  pl.BlockSpec((R, W), lambda i: (i, 0)),
                  pl.BlockSpec((EXPERT_ROWS, W), lambda i: (0, 0))],
        out_specs=pl.BlockSpec(memory_space=pl.ANY),
        out_shape=jax.ShapeDtypeStruct((P, W), xw.dtype),
        scratch_shapes=[pltpu.SemaphoreType.DMA],
        compiler_params=_cparams(("arbitrary",)),
        name="dispatch",
    )(dest3, xw, zsrc)


def _experts_kernel(be_ref, nb_ref, x_ref, wgu_ref, bgu_ref, wd_ref, bd_ref, o_ref):
    f32 = jnp.float32
    d_ff = wd_ref.shape[1]

    @pl.when(pl.program_id(0) < nb_ref[0])
    def _():
        x_hi, x_lo = _unpack_bf16_pairs(x_ref[...])
        half = x_hi.shape[1]
        gu = (jnp.dot(x_hi, wgu_ref[0, :half, :], preferred_element_type=f32)
              + jnp.dot(x_lo, wgu_ref[0, half:, :], preferred_element_type=f32) + bgu_ref[0])
        gate = jnp.minimum(gu[:, :d_ff], SWIGLU_LIMIT)
        lin = jnp.clip(gu[:, d_ff:], -SWIGLU_LIMIT, SWIGLU_LIMIT)
        act = (lin + 1.0) * gate * _sigmoid(SWIGLU_ALPHA * gate)
        y = jnp.dot(act.astype(jnp.bfloat16), wd_ref[0], preferred_element_type=f32) + bd_ref[0]
        o_ref[...] = y

    @pl.when(pl.program_id(0) >= nb_ref[0])
    def _():
        o_ref[...] = jnp.zeros_like(o_ref)


def _experts(xs, block_expert, n_used, w_gu, b_gu, w_down, b_down):
    P, W = xs.shape
    E, D, F2 = w_gu.shape
    nb = P // EXPERT_ROWS
    grid_spec = pltpu.PrefetchScalarGridSpec(
        num_scalar_prefetch=2,
        grid=(nb,),
        in_specs=[
            pl.BlockSpec((EXPERT_ROWS, W), lambda i, be, nu: (i, 0)),
            pl.BlockSpec((1, D, F2), lambda i, be, nu: (be[i], 0, 0)),
            pl.BlockSpec((1, 1, F2), lambda i, be, nu: (be[i], 0, 0)),
            pl.BlockSpec((1, F2 // 2, D), lambda i, be, nu: (be[i], 0, 0)),
            pl.BlockSpec((1, 1, D), lambda i, be, nu: (be[i], 0, 0)),
        ],
        out_specs=pl.BlockSpec((EXPERT_ROWS, D), lambda i, be, nu: (i, 0)),
    )
    return pl.pallas_call(
        _experts_kernel,
        grid_spec=grid_spec,
        out_shape=jax.ShapeDtypeStruct((P, D), jnp.float32),
        compiler_params=_cparams(("arbitrary",)),
        name="experts",
    )(block_expert, n_used, xs, w_gu.astype(jnp.bfloat16), b_gu.reshape(E, 1, F2),
      w_down.astype(jnp.bfloat16), b_down.reshape(E, 1, D))


def _combine_kernel(dest_ref, y_ref, x1_ref, gate_ref, g_ref, o_ref, buf, sem):
    R = x1_ref.shape[0]

    def rcopy(j):
        return pltpu.make_async_copy(y_ref.at[pl.ds(dest_ref[0, 0, j], 1)],
                                     buf.at[j % TOP_K, pl.ds(j // TOP_K, 1)], sem)

    _windowed_copies(rcopy, R * TOP_K)

    gate = gate_ref[...]
    y = x1_ref[...]
    for k in range(TOP_K):
        y = y + gate[:, k:k + 1] * buf[k]
    o_ref[...] = _rms(y, g_ref[...])


def _combine(y_buf, dest, x1, gates, gain):
    T, D = x1.shape
    R = min(DISPATCH_ROWS, T)
    n = T // R
    dest3 = dest.reshape(n, 1, R * TOP_K)
    row = lambda w: pl.BlockSpec((R, w), lambda i: (i, 0))
    return pl.pallas_call(
        _combine_kernel,
        grid=(n,),
        in_specs=[pl.BlockSpec((1, 1, R * TOP_K), lambda i: (i, 0, 0), memory_space=pltpu.SMEM),
                  pl.BlockSpec(memory_space=pl.ANY),
                  row(D), row(TOP_K),
                  pl.BlockSpec((1, D), lambda i: (0, 0))],
        out_specs=row(D),
        out_shape=jax.ShapeDtypeStruct((T, D), jnp.float32),
        scratch_shapes=[pltpu.VMEM((TOP_K, R, D), jnp.float32), pltpu.SemaphoreType.DMA],
        compiler_params=_cparams(("arbitrary",)),
        name="combine",
    )(dest3, y_buf, x1, gates, gain.reshape(1, D))


def _moe_plan(eidx, rank, counts, A):
    counts = counts.reshape(N_EXPERTS)
    padded = (counts + EXPERT_ROWS - 1) // EXPERT_ROWS * EXPERT_ROWS
    pad_ends = jnp.cumsum(padded)
    pad_starts = pad_ends - padded
    n_blocks = -(-A // EXPERT_ROWS) + N_EXPERTS
    dest = pad_starts[eidx] + rank
    block_start = jnp.arange(n_blocks, dtype=pad_ends.dtype) * EXPERT_ROWS
    block_expert = jnp.minimum(jnp.sum(pad_ends[None, :] <= block_start[:, None], axis=1),
                               N_EXPERTS - 1).astype(jnp.int32)
    n_used = (pad_ends[-1] // EXPERT_ROWS).astype(jnp.int32).reshape(1)
    return dest.astype(jnp.int32), block_expert, n_used, n_blocks


def kernel(x, w_in, w_up_a, w_up_b, w_out, norm_mix, norm_ffn, norm_final, hgrn_norm,
           lb_logits, rel_bias, w_router, b_router, w_gu, b_gu, w_down, b_down):
    B, S, D = x.shape
    T = B * S
    assert w_in.shape[0] == 1, "the final rmsnorm is fused into the single layer's combine stage"
    lb_all = jnp.cumsum(jax.nn.softmax(lb_logits.astype(jnp.float32), axis=0), axis=0)
    x2 = x.reshape(T, D)
    (k, ik, qT, vT, iqT, iwT, bq, bf, bi, bg, ga, gb) = _inproj(x2, norm_mix[0], w_in[0], B, S)
    ya = _dsa(k, ik, qT, vT, iqT, iwT, rel_bias, B, S)
    yb = _hgrn(bq, bf, bi, bg, lb_all[0].reshape(B_HEADS, B_KEY_DIM), hgrn_norm[0], B, S)
    x1, xn, logits = _merge(x2, ya, yb, ga, gb, w_up_a[0], w_up_b[0], w_out[0], norm_ffn[0],
                            w_router[0], b_router[0])
    eidx, gates, rank, counts = _route(logits)
    dest, block_expert, n_used, n_blocks = _moe_plan(eidx, rank, counts, T * TOP_K)
    P = n_blocks * EXPERT_ROWS
    xs = _dispatch(xn, dest, P)
    y_buf = _experts(xs, block_expert, n_used, w_gu[0], b_gu[0], w_down[0], b_down[0])
    out = _combine(y_buf, dest, x1, gates, norm_final)
    return out.reshape(B, S, D)
```

```python
import functools
import math

import numpy as np
import jax
import jax.numpy as jnp
from jax import lax
from jax.experimental import pallas as pl
from jax.experimental.pallas import tpu as pltpu
from jax.experimental.pallas import tpu_sc as plsc

A_HEADS = 8
A_HEAD_DIM = 64
IDX_HEADS = 8
IDX_DIM = 32
TOPK_MAX = 256
REL_BUCKETS = 32
REL_MAX_DIST = 128
B_HEADS = 4
B_KEY_DIM = 128
B_VAL_DIM = 128
N_EXPERTS = 32
TOP_K = 4
SWIGLU_LIMIT = 7.0
SWIGLU_ALPHA = 1.702
EPS = 1e-6

A_WIDTH = A_HEADS * A_HEAD_DIM
B_WIDTH = B_HEADS * B_VAL_DIM
IDX_WIDTH = IDX_HEADS * IDX_DIM

V7X_LANES = 128
V7X_SUBLANES = 8
V7X_VMEM_LIMIT_BYTES = 56 * 1024 * 1024
V7X_SC_CORES = 2
V7X_SC_SUBCORES = 16

PROJ_ROWS = 512
ATT_Q = 256
ATT_KC = 128
HGRN_ROWS = 512
HGRN_CHUNK = 64
HGRN_SAFE_DECAY = 70.0
ROUTE_ROWS = 512
EXPERT_ROWS = 256
COMBINE_ROWS = 512
SC_GATHER_ROWS = 64
MASK_NEG = -1e30
BISECT_FAST_ITERS = 26


def _cparams(dims):
    return pltpu.CompilerParams(dimension_semantics=dims, vmem_limit_bytes=V7X_VMEM_LIMIT_BYTES)


def _rms(x, gain):
    return x * lax.rsqrt(jnp.mean(x * x, axis=-1, keepdims=True) + EPS) * gain


def _sigmoid(x):
    return 1.0 / (1.0 + jnp.exp(-x))


def _pack_bf16_pairs(x):
    n = x.shape[1] // 2
    as_bits = lambda v: lax.bitcast_convert_type(v.astype(jnp.bfloat16).astype(jnp.float32), jnp.uint32)
    return (as_bits(x[:, :n]) & jnp.uint32(0xFFFF0000)) | (as_bits(x[:, n:]) >> 16)


def _unpack_bf16_pairs(w):
    hi = lax.bitcast_convert_type(w & jnp.uint32(0xFFFF0000), jnp.float32).astype(jnp.bfloat16)
    lo = lax.bitcast_convert_type(w << 16, jnp.float32).astype(jnp.bfloat16)
    return hi, lo


def _fold_rows(x, op):
    return op(x.reshape(x.shape[0] // V7X_SUBLANES, V7X_SUBLANES, x.shape[1]), axis=0)


def _inproj_kernel(x_ref, g_ref, wk_ref, wik_ref, wqT_ref, wvT_ref, wiqT_ref, wiwT_ref, wb_ref, wg_ref,
                   k_ref, ik_ref, qT_ref, vT_ref, iqT_ref, iwT_ref, bq_ref, bf_ref, bi_ref, bg_ref,
                   ga_ref, gb_ref):
    x = x_ref[...]
    hn = _rms(x, g_ref[...]).astype(jnp.bfloat16)

    def mm(w_ref):
        return jnp.dot(hn, w_ref[...], preferred_element_type=jnp.float32)

    def mm_t(w_ref):
        return lax.dot_general(w_ref[...], hn, (((1,), (1,)), ((), ())),
                               preferred_element_type=jnp.float32)

    k_ref[...] = mm(wk_ref).astype(jnp.bfloat16)
    ik_ref[...] = mm(wik_ref).astype(jnp.bfloat16)
    qT_ref[0] = (mm_t(wqT_ref) * (A_HEAD_DIM ** -0.5)).astype(jnp.bfloat16)
    vT_ref[0] = mm_t(wvT_ref).astype(jnp.bfloat16)
    iqT_ref[0] = mm_t(wiqT_ref).astype(jnp.bfloat16)
    iwT_ref[0] = mm_t(wiwT_ref) * ((IDX_HEADS * IDX_DIM) ** -0.5)
    hb = mm(wb_ref)
    bq_ref[...] = hb[:, 0 * B_WIDTH:1 * B_WIDTH]
    bf_ref[...] = hb[:, 1 * B_WIDTH:2 * B_WIDTH]
    bi_ref[...] = hb[:, 2 * B_WIDTH:3 * B_WIDTH]
    bg_ref[...] = hb[:, 3 * B_WIDTH:4 * B_WIDTH]
    d = ga_ref.shape[-1]
    hg = mm(wg_ref)
    ga_ref[...] = _sigmoid(hg[:, :d]).astype(jnp.bfloat16)
    gb_ref[...] = _sigmoid(hg[:, d:]).astype(jnp.bfloat16)


def _inproj(x2, gain, w_in, B, S):
    T, D = x2.shape
    R = min(PROJ_ROWS, S)
    nS = S // R
    o = np.cumsum((0, A_WIDTH, A_WIDTH, A_WIDTH, IDX_WIDTH, IDX_HEADS, IDX_DIM,
                   B_WIDTH, B_WIDTH, B_WIDTH, B_WIDTH, D, D))
    bf = jnp.bfloat16
    wqT = w_in[:, o[0]:o[1]].T.astype(bf)
    wk = w_in[:, o[1]:o[2]].astype(bf)
    wvT = w_in[:, o[2]:o[3]].T.astype(bf)
    wiqT = w_in[:, o[3]:o[4]].T.astype(bf)
    wiwT = w_in[:, o[4]:o[5]].T.astype(bf)
    wik = w_in[:, o[5]:o[6]].astype(bf)
    wb = w_in[:, o[6]:o[10]].astype(bf)
    wg = w_in[:, o[10]:o[12]].astype(bf)

    def full(a):
        return pl.BlockSpec(a.shape, lambda b, i: (0,) * a.ndim)

    row = lambda n: pl.BlockSpec((R, n), lambda b, i: (b * nS + i, 0))
    colT = lambda n: pl.BlockSpec((1, n, R), lambda b, i: (b, 0, i))
    f32 = jnp.float32
    outs = [
        (jax.ShapeDtypeStruct((T, A_WIDTH), bf), row(A_WIDTH)),
        (jax.ShapeDtypeStruct((T, IDX_DIM), bf), row(IDX_DIM)),
        (jax.ShapeDtypeStruct((B, A_WIDTH, S), bf), colT(A_WIDTH)),
        (jax.ShapeDtypeStruct((B, A_WIDTH, S), bf), colT(A_WIDTH)),
        (jax.ShapeDtypeStruct((B, IDX_WIDTH, S), bf), colT(IDX_WIDTH)),
        (jax.ShapeDtypeStruct((B, IDX_HEADS, S), f32), colT(IDX_HEADS)),
        (jax.ShapeDtypeStruct((T, B_WIDTH), f32), row(B_WIDTH)),
        (jax.ShapeDtypeStruct((T, B_WIDTH), f32), row(B_WIDTH)),
        (jax.ShapeDtypeStruct((T, B_WIDTH), f32), row(B_WIDTH)),
        (jax.ShapeDtypeStruct((T, B_WIDTH), f32), row(B_WIDTH)),
        (jax.ShapeDtypeStruct((T, D), bf), row(D)),
        (jax.ShapeDtypeStruct((T, D), bf), row(D)),
    ]
    ins = [x2, gain.reshape(1, D), wk, wik, wqT, wvT, wiqT, wiwT, wb, wg]
    in_specs = [row(D)] + [full(a) for a in ins[1:]]
    return pl.pallas_call(
        _inproj_kernel,
        grid=(B, nS),
        in_specs=in_specs,
        out_specs=[s for _, s in outs],
        out_shape=[s for s, _ in outs],
        compiler_params=_cparams(("parallel", "parallel")),
        name="inproj",
    )(*ins)


def _t5_bucket_table(n):
    d = np.arange(n)
    max_exact = REL_BUCKETS // 2
    nf = np.maximum(d, 1).astype(np.float64)
    large = max_exact + (np.log(nf / max_exact) / math.log(REL_MAX_DIST / max_exact)
                         * (REL_BUCKETS - max_exact)).astype(np.int32)
    large = np.minimum(large, REL_BUCKETS - 1)
    return np.where(d < max_exact, d, large)


def _dsa_kernel(qT_ref, k_ref, vT_ref, iqT_ref, iwT_ref, ik_ref, enear_ref,
                o_ref, sc_ref, *, topk):
    TQ = qT_ref.shape[2]
    KC = TQ
    i = pl.program_id(1)
    nch = i + 1
    q0 = i * TQ
    f32 = jnp.float32
    bf16 = jnp.bfloat16
    key_id = lax.broadcasted_iota(jnp.int32, (KC, TQ), 0)
    qry_id = lax.broadcasted_iota(jnp.int32, (KC, TQ), 1)

    def col_reduce(x, op):
        return op(_fold_rows(x, op), axis=0, keepdims=True)

    iw = iwT_ref[0]

    def score_chunk(c, carry):
        rmin, rmax = carry
        k0 = pl.multiple_of(c * KC, KC)
        ik = ik_ref[pl.ds(k0, KC), :]
        acc = jnp.zeros((KC, TQ), f32)
        for h in range(IDX_HEADS):
            sh = jnp.dot(ik, iqT_ref[0, h * IDX_DIM:(h + 1) * IDX_DIM, :], preferred_element_type=f32)
            acc = acc + jnp.maximum(sh, 0.0) * iw[h:h + 1, :]
        valid = (k0 + key_id) <= (q0 + qry_id)
        sc_ref[pl.ds(k0, KC), :] = jnp.where(valid, acc, MASK_NEG)
        rmin = jnp.minimum(rmin, _fold_rows(jnp.where(valid, acc, -MASK_NEG), jnp.min))
        rmax = jnp.maximum(rmax, _fold_rows(jnp.where(valid, acc, MASK_NEG), jnp.max))
        return rmin, rmax

    rmin8, rmax8 = lax.fori_loop(
        0, nch, score_chunk,
        (jnp.full((V7X_SUBLANES, TQ), -MASK_NEG, f32), jnp.full((V7X_SUBLANES, TQ), MASK_NEG, f32)))
    rmin = jnp.min(rmin8, axis=0, keepdims=True)
    rmax = jnp.max(rmax8, axis=0, keepdims=True)

    def count_where(pred_fn):
        def body(c, acc):
            k0 = pl.multiple_of(c * KC, KC)
            blk = sc_ref[pl.ds(k0, KC), :]
            return acc + _fold_rows(jnp.where(pred_fn(blk), 1.0, 0.0), jnp.sum)
        acc = lax.fori_loop(0, nch, body, jnp.zeros((V7X_SUBLANES, TQ), f32))
        return jnp.sum(acc, axis=0, keepdims=True)

    def band_min_max(lo, hi):
        def body(c, carry):
            bmin, bmax = carry
            k0 = pl.multiple_of(c * KC, KC)
            blk = sc_ref[pl.ds(k0, KC), :]
            bmin = jnp.minimum(bmin, _fold_rows(jnp.where(blk >= lo, blk, -MASK_NEG), jnp.min))
            bmax = jnp.maximum(bmax, _fold_rows(jnp.where(blk < hi, blk, MASK_NEG), jnp.max))
            return bmin, bmax
        bmin8, bmax8 = lax.fori_loop(
            0, nch, body,
            (jnp.full((V7X_SUBLANES, TQ), -MASK_NEG, f32), jnp.full((V7X_SUBLANES, TQ), MASK_NEG, f32)))
        return jnp.min(bmin8, axis=0, keepdims=True), jnp.max(bmax8, axis=0, keepdims=True)

    kf = float(topk)
    n_valid = (q0 + 1 + lax.broadcasted_iota(jnp.int32, (1, TQ), 1)).astype(f32)
    c_top = count_where(lambda blk: blk >= rmax)
    top_tie = c_top >= kf
    lo0 = jnp.where(top_tie, rmax, rmin)
    cnt0 = jnp.where(top_tie, c_top, n_valid)
    done0 = jnp.where(top_tie | (cnt0 <= kf), 1.0, 0.0)

    def cond(st):
        return jnp.min(st[-1]) < 0.5

    def body(st):
        it, lo, hi, cnt, chi, done = st

        def plain(_):
            half = lo + 0.5 * (hi - lo)
            stuck = (half <= lo) | (half >= hi)
            guess = lo + (hi - lo) * ((cnt - kf + 0.5) / (cnt - chi))
            use_guess = (it % 2 == 0) & (guess > lo) & (guess < hi)
            return lo, jnp.where(use_guess, guess, half), jnp.where(stuck, 1.0, 0.0)

        def snapped(_):
            bmin, bmax = band_min_max(lo, hi)
            mid = bmin + 0.5 * (bmax - bmin)
            mid = jnp.where(mid <= bmin, bmax, mid)
            return bmin, mid, jnp.where(bmax <= bmin, 1.0, 0.0)

        lo_s, mid, tie_f = lax.cond(it < BISECT_FAST_ITERS, plain, snapped, None)
        tie = tie_f > 0.5
        active = done < 0.5
        lo_s = jnp.where(active, lo_s, lo)
        c = count_where(lambda blk: blk >= mid)
        feas = c >= kf
        move = active & jnp.logical_not(tie)
        lo_n = jnp.where(move & feas, mid, lo_s)
        cnt_n = jnp.where(move & feas, c, cnt)
        hi_n = jnp.where(move & jnp.logical_not(feas), mid, hi)
        chi_n = jnp.where(move & jnp.logical_not(feas), c, chi)
        done_n = jnp.where((active & tie) | (cnt_n <= kf), 1.0, done)
        return it + 1, lo_n, hi_n, cnt_n, chi_n, done_n

    thr = lax.while_loop(cond, body, (jnp.int32(0), lo0, rmax, cnt0, c_top, done0))[1]

    need = kf - count_where(lambda blk: blk > thr)
    tril = jnp.where(lax.broadcasted_iota(jnp.int32, (KC, KC), 1)
                     <= lax.broadcasted_iota(jnp.int32, (KC, KC), 0), 1.0, 0.0).astype(bf16)

    def mask_chunk(c, run):
        k0 = pl.multiple_of(c * KC, KC)
        blk = sc_ref[pl.ds(k0, KC), :]
        eq = jnp.where(blk == thr, 1.0, 0.0)
        pref = jnp.dot(tril, eq.astype(bf16), preferred_element_type=f32)
        sel = (blk > thr) | ((eq > 0.5) & (run + pref <= need))
        sc_ref[pl.ds(k0, KC), :] = jnp.where(sel, 0.0, MASK_NEG)
        return run + pref[KC - 1:KC, :]

    lax.fori_loop(0, nch, mask_chunk, jnp.zeros((1, TQ), f32))

    AK = min(ATT_KC, TQ)
    per = TQ // AK
    head0_q = (lax.broadcasted_iota(jnp.int32, (V7X_LANES, TQ), 0) // A_HEAD_DIM) == 0

    def head_pair(p, _):
        l0 = pl.multiple_of(p * V7X_LANES, V7X_LANES)
        q_pair = qT_ref[0, pl.ds(l0, V7X_LANES), :]
        zq = jnp.zeros_like(q_pair)
        q_heads = (jnp.where(head0_q, q_pair, zq), jnp.where(head0_q, zq, q_pair))

        def step(c, bias_of, carry):
            k0 = pl.multiple_of(c * AK, AK)
            kp = k_ref[pl.ds(k0, AK), pl.ds(l0, V7X_LANES)]
            vp = vT_ref[0, pl.ds(l0, V7X_LANES), pl.ds(k0, AK)]
            msk = sc_ref[pl.ds(k0, AK), :]
            new = []
            for sub in range(2):
                m, l, acc = carry[sub]
                s = jnp.dot(kp, q_heads[sub], preferred_element_type=f32) + msk
                if bias_of is not None:
                    s = s + bias_of(sub)
                m_new = jnp.maximum(m, col_reduce(s, jnp.max))
                alpha = jnp.exp(m - m_new)
                pr = jnp.exp(s - m_new)
                l_new = alpha * l + col_reduce(pr, jnp.sum)
                acc_new = alpha * acc + jnp.dot(vp, pr.astype(bf16), preferred_element_type=f32)
                new.append((m_new, l_new, acc_new))
            return tuple(new)

        def near(block, first_chunk, carry):
            for jj in range(per):
                rows = slice(block * TQ + jj * AK, block * TQ + (jj + 1) * AK)
                carry = step(first_chunk + jj, lambda sub: enear_ref[2 * p + sub, rows, :], carry)
            return carry

        def far(blk, carry):
            for jj in range(per):
                carry = step(blk * per + jj, None, carry)
            return carry

        init = tuple((jnp.full((1, TQ), MASK_NEG, f32), jnp.zeros((1, TQ), f32),
                      jnp.zeros((V7X_LANES, TQ), f32)) for _ in range(2))
        carry = lax.fori_loop(0, jnp.maximum(i - 1, 0), far, init)
        carry = lax.cond(i >= 1, lambda cr: near(0, (i - 1) * per, cr), lambda cr: cr, carry)
        (_, l_a, acc_a), (_, l_b, acc_b) = near(1, i * per, carry)
        o_pair = jnp.where(head0_q, acc_a / l_a, acc_b / l_b)
        o_ref[:, pl.ds(l0, V7X_LANES)] = o_pair.T.astype(o_ref.dtype)
        return 0

    lax.fori_loop(0, A_HEADS // 2, head_pair, 0)


def _dsa(k, ik, qT, vT, iqT, iwT, rel_bias, B, S):
    T = k.shape[0]
    TQ = min(ATT_Q, S)
    nQ = S // TQ
    topk = min(TOPK_MAX, S // 4)
    buckets = _t5_bucket_table(2 * TQ + 1)
    assert np.all(_t5_bucket_table(S + 1)[TQ + 1:] == REL_BUCKETS - 1)
    j = np.arange(2 * TQ)[:, None]
    r = np.arange(TQ)[None, :]
    dist = np.maximum(r + TQ - j, 0)
    onehot = (jnp.asarray(buckets[dist], jnp.int32)[None]
              == jnp.arange(REL_BUCKETS, dtype=jnp.int32)[:, None, None]).astype(jnp.float32)
    rel = rel_bias.astype(jnp.float32) - rel_bias[REL_BUCKETS - 1].astype(jnp.float32)[None, :]
    enear = jnp.einsum('nh,njr->hjr', rel, onehot, precision=lax.Precision.HIGHEST)

    return pl.pallas_call(
        functools.partial(_dsa_kernel, topk=topk),
        grid=(B, nQ),
        in_specs=[
            pl.BlockSpec((1, A_WIDTH, TQ), lambda b, i: (b, 0, i)),
            pl.BlockSpec((S, A_WIDTH), lambda b, i: (b, 0)),
            pl.BlockSpec((1, A_WIDTH, S), lambda b, i: (b, 0, 0)),
            pl.BlockSpec((1, IDX_WIDTH, TQ), lambda b, i: (b, 0, i)),
            pl.BlockSpec((1, IDX_HEADS, TQ), lambda b, i: (b, 0, i)),
            pl.BlockSpec((S, IDX_DIM), lambda b, i: (b, 0)),
            pl.BlockSpec((A_HEADS, 2 * TQ, TQ), lambda b, i: (0, 0, 0)),
        ],
        out_specs=pl.BlockSpec((TQ, A_WIDTH), lambda b, i: (b * nQ + i, 0)),
        scratch_shapes=[pltpu.VMEM((S, TQ), jnp.float32)],
        out_shape=jax.ShapeDtypeStruct((T, A_WIDTH), jnp.bfloat16),
        compiler_params=_cparams(("parallel", "arbitrary")),
        name="dsa",
    )(qT, k, vT, iqT, iwT, ik, enear)


def _hgrn_kernel(bq_ref, bf_ref, bi_ref, bg_ref, lb_ref, gain_ref, o_ref,
                 st_ref, b_scr, q_scr, k_scr, v_scr, oi_scr):
    R = bq_ref.shape[0]
    C = HGRN_CHUNK
    nC = R // C
    f32 = jnp.float32
    bf16 = jnp.bfloat16
    h = pl.program_id(1)

    @pl.when(pl.program_id(2) == 0)
    def _():
        st_ref[...] = jnp.zeros_like(st_ref)

    lb = lb_ref[pl.ds(h, 1), :]
    gain = gain_ref[pl.ds(h, 1), :]
    tril_incl = jnp.where(lax.broadcasted_iota(jnp.int32, (C, C), 1)
                          <= lax.broadcasted_iota(jnp.int32, (C, C), 0), 1.0, 0.0)
    srow = lax.broadcasted_iota(jnp.int32, (C, B_KEY_DIM), 0)

    def gates(r0):
        f = lb + (1.0 - lb) * _sigmoid(bf_ref[pl.ds(r0, C), :])
        qr = bq_ref[pl.ds(r0, C), :]
        return jnp.log(f), 1.0 - f, qr * _sigmoid(qr) * (B_KEY_DIM ** -0.5), bi_ref[pl.ds(r0, C), :]

    def chunk(r0, st, factorised):
        g, kk, qq, vv = gates(r0)
        b = jnp.dot(tril_incl, g, preferred_element_type=f32, precision=lax.Precision.HIGHEST)
        b_last = b[C - 1:C, :]
        qd = (qq * jnp.exp(b)).astype(bf16)
        o_inter = lax.dot_general(qd, st.astype(bf16), (((1,), (1,)), ((), ())), preferred_element_type=f32)
        if factorised:
            kd = (kk * jnp.exp(-b)).astype(bf16)
            att = lax.dot_general(qd, kd, (((1,), (1,)), ((), ())), preferred_element_type=f32) * tril_incl
            o_intra = jnp.dot(att.astype(bf16), vv.astype(bf16), preferred_element_type=f32)
        else:
            b_scr[...] = b
            q_scr[...] = qq
            k_scr[...] = kk
            v_scr[...] = vv

            def row(t, _):
                bt = b_scr[pl.ds(t, 1), :]
                qt = q_scr[pl.ds(t, 1), :]
                ex = jnp.where(srow <= t, bt - b_scr[...], -jnp.inf)
                a = jnp.sum(qt * k_scr[...] * jnp.exp(ex), axis=1, keepdims=True)
                oi_scr[pl.ds(t, 1), :] = jnp.sum(a * v_scr[...], axis=0, keepdims=True)
                return 0
            lax.fori_loop(0, C, row, 0)
            o_intra = oi_scr[...]
        o = o_inter + o_intra
        kd_last = (kk * jnp.exp(b_last - b)).astype(bf16)
        upd = lax.dot_general(vv.astype(bf16), kd_last, (((0,), (0,)), ((), ())),
                              preferred_element_type=f32)
        og = bg_ref[pl.ds(r0, C), :]
        y = _rms(o, gain) * (og * _sigmoid(og))
        o_ref[pl.ds(r0, C), :] = y.astype(o_ref.dtype)
        return st * jnp.exp(b_last) + upd

    f_all = lb + (1.0 - lb) * _sigmoid(bf_ref[...])
    decay = jnp.sum(jnp.log(f_all).reshape(nC, C, B_KEY_DIM), axis=1)
    safe = jnp.min(decay) >= -HGRN_SAFE_DECAY

    @pl.when(safe)
    def _():
        st = st_ref[...]
        for c in range(nC):
            st = chunk(c * C, st, True)
        st_ref[...] = st

    @pl.when(jnp.logical_not(safe))
    def _():
        st_ref[...] = lax.fori_loop(
            0, nC, lambda c, st: chunk(pl.multiple_of(c * C, C), st, False), st_ref[...])


def _hgrn(bq, bf, bi, bg, lb, gain, B, S):
    T = bq.shape[0]
    R = min(HGRN_ROWS, S)
    nR = S // R
    C = HGRN_CHUNK
    blk = pl.BlockSpec((R, B_KEY_DIM), lambda b, h, c: (b * nR + c, h))
    small = pl.BlockSpec((B_HEADS, B_KEY_DIM), lambda b, h, c: (0, 0))
    f32 = jnp.float32
    return pl.pallas_call(
        _hgrn_kernel,
        grid=(B, B_HEADS, nR),
        in_specs=[blk, blk, blk, blk, small, small],
        out_specs=blk,
        out_shape=jax.ShapeDtypeStruct((T, B_WIDTH), jnp.bfloat16),
        scratch_shapes=[pltpu.VMEM((B_VAL_DIM, B_KEY_DIM), f32)] +
                       [pltpu.VMEM((C, B_KEY_DIM), f32) for _ in range(5)],
        compiler_params=_cparams(("parallel", "parallel", "arbitrary")),
        name="hgrn",
    )(bq, bf, bi, bg, lb, gain)


def _merge_kernel(x_ref, ya_ref, yb_ref, ga_ref, gb_ref, wa_ref, wb_ref, wo_ref, g_ref, wr_ref, br_ref,
                  x1_ref, xn_ref, lg_ref):
    f32 = jnp.float32
    ma = jnp.dot(ya_ref[...], wa_ref[...], preferred_element_type=f32)
    mb = jnp.dot(yb_ref[...], wb_ref[...], preferred_element_type=f32)
    merged = ga_ref[...].astype(f32) * ma + gb_ref[...].astype(f32) * mb
    x1 = x_ref[...] + jnp.dot(merged.astype(jnp.bfloat16), wo_ref[...], preferred_element_type=f32)
    x1_ref[...] = x1
    hn = _rms(x1, g_ref[...])
    xn_ref[...] = _pack_bf16_pairs(hn)
    lg_ref[...] = jnp.dot(hn, wr_ref[...], preferred_element_type=f32,
                          precision=lax.Precision.HIGHEST) + br_ref[...]


def _merge(x2, ya, yb, ga, gb, w_up_a, w_up_b, w_out, gain, w_router, b_router):
    T, D = x2.shape
    R = min(PROJ_ROWS, T)
    bf = jnp.bfloat16
    ins = [x2, ya, yb, ga, gb, w_up_a.astype(bf), w_up_b.astype(bf), w_out.astype(bf),
           gain.reshape(1, D), w_router, b_router.reshape(1, N_EXPERTS)]
    row = lambda n: pl.BlockSpec((R, n), lambda i: (i, 0))
    full = lambda a: pl.BlockSpec(a.shape, lambda i: (0,) * a.ndim)
    in_specs = [row(D), row(A_WIDTH), row(B_WIDTH), row(D), row(D)] + [full(a) for a in ins[5:]]
    return pl.pallas_call(
        _merge_kernel,
        grid=(T // R,),
        in_specs=in_specs,
        out_specs=[row(D), row(D // 2), row(N_EXPERTS)],
        out_shape=[jax.ShapeDtypeStruct((T, D), jnp.float32), jax.ShapeDtypeStruct((T, D // 2), jnp.uint32),
                   jax.ShapeDtypeStruct((T, N_EXPERTS), jnp.float32)],
        compiler_params=_cparams(("parallel",)),
        name="merge",
    )(*ins)


def _route_kernel(lg_ref, eidx_ref, gate_ref, rank_ref, cnt_ref, run_ref):
    R = lg_ref.shape[0]
    f32 = jnp.float32

    @pl.when(pl.program_id(0) == 0)
    def _():
        run_ref[...] = jnp.zeros_like(run_ref)

    lg = lg_ref[...]
    lane = lax.broadcasted_iota(jnp.int32, (R, N_EXPERTS), 1)
    work = lg
    onehots, vals, idxs = [], [], []
    for _ in range(TOP_K):
        m = jnp.max(work, axis=1, keepdims=True)
        idx = jnp.min(jnp.where(work == m, lane, N_EXPERTS), axis=1, keepdims=True)
        oh = lane == idx
        onehots.append(oh)
        vals.append(m)
        idxs.append(idx)
        work = jnp.where(oh, -jnp.inf, work)
    ex = [jnp.exp(v - vals[0]) for v in vals]
    den = ex[0] + ex[1] + ex[2] + ex[3]
    chosen = jnp.where(onehots[0] | onehots[1] | onehots[2] | onehots[3], 1.0, 0.0)
    strict = jnp.where(lax.broadcasted_iota(jnp.int32, (R, R), 1)
                       < lax.broadcasted_iota(jnp.int32, (R, R), 0), 1.0, 0.0).astype(jnp.bfloat16)
    before = jnp.dot(strict, chosen.astype(jnp.bfloat16), preferred_element_type=f32) + run_ref[...]
    lane4 = lax.broadcasted_iota(jnp.int32, (R, TOP_K), 1)
    eidx = jnp.zeros((R, TOP_K), jnp.int32)
    gate = jnp.zeros((R, TOP_K), f32)
    rank = jnp.zeros((R, TOP_K), f32)
    for k in range(TOP_K):
        eidx = jnp.where(lane4 == k, idxs[k], eidx)
        gate = jnp.where(lane4 == k, ex[k] / den, gate)
        rk = jnp.sum(jnp.where(onehots[k], before, 0.0), axis=1, keepdims=True)
        rank = jnp.where(lane4 == k, rk, rank)
    eidx_ref[...] = eidx
    gate_ref[...] = gate
    rank_ref[...] = rank.astype(jnp.int32)
    run_ref[...] = run_ref[...] + jnp.sum(chosen, axis=0, keepdims=True)
    cnt_ref[...] = run_ref[...].astype(jnp.int32)


def _route(logits):
    T = logits.shape[0]
    R = min(ROUTE_ROWS, T)
    row = lambda n: pl.BlockSpec((R, n), lambda i: (i, 0))
    return pl.pallas_call(
        _route_kernel,
        grid=(T // R,),
        in_specs=[row(N_EXPERTS)],
        out_specs=[row(TOP_K), row(TOP_K), row(TOP_K), pl.BlockSpec((1, N_EXPERTS), lambda i: (0, 0))],
        out_shape=[jax.ShapeDtypeStruct((T, TOP_K), jnp.int32), jax.ShapeDtypeStruct((T, TOP_K), jnp.float32),
                   jax.ShapeDtypeStruct((T, TOP_K), jnp.int32), jax.ShapeDtypeStruct((1, N_EXPERTS), jnp.int32)],
        scratch_shapes=[pltpu.VMEM((1, N_EXPERTS), jnp.float32)],
        compiler_params=_cparams(("arbitrary",)),
        name="route",
    )(logits)


def _sc_gather_rows(table, idx):
    N, W = table.shape
    M = idx.shape[0]
    workers = V7X_SC_CORES * V7X_SC_SUBCORES
    per_worker = M // workers
    assert per_worker * workers == M and per_worker % SC_GATHER_ROWS == 0
    mesh = plsc.VectorSubcoreMesh(core_axis_name="core", subcore_axis_name="subcore",
                                  num_cores=V7X_SC_CORES, num_subcores=V7X_SC_SUBCORES)

    @functools.partial(
        pl.kernel, mesh=mesh,
        out_type=jax.ShapeDtypeStruct((M, W), table.dtype),
        scratch_types=[pltpu.VMEM((per_worker,), jnp.int32),
                       pltpu.VMEM((SC_GATHER_ROWS, W), table.dtype),
                       pltpu.SemaphoreType.DMA],
    )
    def gather(table_hbm, idx_hbm, out_hbm, idx_v, rows_v, sem):
        worker = lax.axis_index("subcore") * V7X_SC_CORES + lax.axis_index("core")
        base = pl.multiple_of(worker * per_worker, SC_GATHER_ROWS)
        pltpu.sync_copy(idx_hbm.at[pl.ds(base, per_worker)], idx_v)

        @pl.loop(0, per_worker // SC_GATHER_ROWS)
        def _(g):
            off = pl.multiple_of(g * SC_GATHER_ROWS, SC_GATHER_ROWS)
            pltpu.async_copy(table_hbm.at[idx_v.at[pl.ds(off, SC_GATHER_ROWS)]], rows_v, sem).wait()
            pltpu.sync_copy(rows_v, out_hbm.at[pl.ds(base + off, SC_GATHER_ROWS)])

    return gather(table, idx)


def _experts_kernel(be_ref, nb_ref, x_ref, wgu_ref, bgu_ref, wd_ref, bd_ref, o_ref):
    f32 = jnp.float32
    d_ff = wd_ref.shape[1]

    @pl.when(pl.program_id(0) < nb_ref[0])
    def _():
        x_hi, x_lo = _unpack_bf16_pairs(x_ref[...])
        half = x_hi.shape[1]
        gu = (jnp.dot(x_hi, wgu_ref[0, :half, :], preferred_element_type=f32)
              + jnp.dot(x_lo, wgu_ref[0, half:, :], preferred_element_type=f32) + bgu_ref[0])
        gate = jnp.minimum(gu[:, :d_ff], SWIGLU_LIMIT)
        lin = jnp.clip(gu[:, d_ff:], -SWIGLU_LIMIT, SWIGLU_LIMIT)
        act = (lin + 1.0) * gate * _sigmoid(SWIGLU_ALPHA * gate)
        y = jnp.dot(act.astype(jnp.bfloat16), wd_ref[0], preferred_element_type=f32) + bd_ref[0]
        o_ref[...] = _pack_bf16_pairs(y)

    @pl.when(pl.program_id(0) >= nb_ref[0])
    def _():
        o_ref[...] = jnp.zeros_like(o_ref)


def _experts(xs, block_expert, n_used, w_gu, b_gu, w_down, b_down):
    P, W = xs.shape
    E, D, F2 = w_gu.shape
    nb = P // EXPERT_ROWS
    grid_spec = pltpu.PrefetchScalarGridSpec(
        num_scalar_prefetch=2,
        grid=(nb,),
        in_specs=[
            pl.BlockSpec((EXPERT_ROWS, W), lambda i, be, nu: (i, 0)),
            pl.BlockSpec((1, D, F2), lambda i, be, nu: (be[i], 0, 0)),
            pl.BlockSpec((1, 1, F2), lambda i, be, nu: (be[i], 0, 0)),
            pl.BlockSpec((1, F2 // 2, D), lambda i, be, nu: (be[i], 0, 0)),
            pl.BlockSpec((1, 1, D), lambda i, be, nu: (be[i], 0, 0)),
        ],
        out_specs=pl.BlockSpec((EXPERT_ROWS, W), lambda i, be, nu: (i, 0)),
    )
    return pl.pallas_call(
        _experts_kernel,
        grid_spec=grid_spec,
        out_shape=jax.ShapeDtypeStruct((P, W), jnp.uint32),
        compiler_params=_cparams(("arbitrary",)),
        name="experts",
    )(block_expert, n_used, xs, w_gu.astype(jnp.bfloat16), b_gu.reshape(E, 1, F2),
      w_down.astype(jnp.bfloat16), b_down.reshape(E, 1, D))


def _combine_kernel(ya_ref, x1_ref, gate_ref, g_ref, o_ref):
    half = x1_ref.shape[1] // 2
    f32 = jnp.float32
    gate = gate_ref[...]
    x1 = x1_ref[...]
    y_hi = x1[:, :half]
    y_lo = x1[:, half:]
    for k in range(TOP_K):
        hi, lo = _unpack_bf16_pairs(ya_ref[k])
        y_hi = y_hi + gate[:, k:k + 1] * hi.astype(f32)
        y_lo = y_lo + gate[:, k:k + 1] * lo.astype(f32)
    o_ref[...] = _rms(jnp.concatenate([y_hi, y_lo], axis=1), g_ref[...])


def _combine(ya, x1, gates, gain):
    T, D = x1.shape
    R = min(COMBINE_ROWS, T)
    row = lambda w: pl.BlockSpec((R, w), lambda i: (i, 0))
    return pl.pallas_call(
        _combine_kernel,
        grid=(T // R,),
        in_specs=[pl.BlockSpec((TOP_K, R, D // 2), lambda i: (0, i, 0)), row(D), row(TOP_K),
                  pl.BlockSpec((1, D), lambda i: (0, 0))],
        out_specs=row(D),
        out_shape=jax.ShapeDtypeStruct((T, D), jnp.float32),
        compiler_params=_cparams(("parallel",)),
        name="combine",
    )(ya, x1, gates, gain.reshape(1, D))


def _moe_plan(eidx, rank, counts, A):
    counts = counts.reshape(N_EXPERTS)
    padded = (counts + EXPERT_ROWS - 1) // EXPERT_ROWS * EXPERT_ROWS
    pad_ends = jnp.cumsum(padded)
    pad_starts = pad_ends - padded
    n_blocks = -(-A // EXPERT_ROWS) + N_EXPERTS
    dest = pad_starts[eidx] + rank
    block_start = jnp.arange(n_blocks, dtype=pad_ends.dtype) * EXPERT_ROWS
    block_expert = jnp.minimum(jnp.sum(pad_ends[None, :] <= block_start[:, None], axis=1),
                               N_EXPERTS - 1).astype(jnp.int32)
    n_used = (pad_ends[-1] // EXPERT_ROWS).astype(jnp.int32).reshape(1)
    return dest.astype(jnp.int32), block_expert, n_used, n_blocks


def kernel(x, w_in, w_up_a, w_up_b, w_out, norm_mix, norm_ffn, norm_final, hgrn_norm,
           lb_logits, rel_bias, w_router, b_router, w_gu, b_gu, w_down, b_down):
    B, S, D = x.shape
    T = B * S
    assert w_in.shape[0] == 1, "the final rmsnorm is fused into the single layer's combine stage"
    lb_all = jnp.cumsum(jax.nn.softmax(lb_logits.astype(jnp.float32), axis=0), axis=0)
    x2 = x.reshape(T, D)
    (k, ik, qT, vT, iqT, iwT, bq, bf, bi, bg, ga, gb) = _inproj(x2, norm_mix[0], w_in[0], B, S)
    ya = _dsa(k, ik, qT, vT, iqT, iwT, rel_bias, B, S)
    yb = _hgrn(bq, bf, bi, bg, lb_all[0].reshape(B_HEADS, B_KEY_DIM), hgrn_norm[0], B, S)
    x1, xn, logits = _merge(x2, ya, yb, ga, gb, w_up_a[0], w_up_b[0], w_out[0], norm_ffn[0],
                            w_router[0], b_router[0])
    eidx, gates, rank, counts = _route(logits)
    dest, block_expert, n_used, n_blocks = _moe_plan(eidx, rank, counts, T * TOP_K)
    P = n_blocks * EXPERT_ROWS
    A = T * TOP_K
    dest_flat = dest.reshape(A)
    buf_tok = jnp.zeros((P,), jnp.int32).at[dest_flat].set(
        jnp.arange(A, dtype=jnp.int32) // TOP_K, unique_indices=True)
    xs = _sc_gather_rows(xn, buf_tok)
    y_buf = _experts(xs, block_expert, n_used, w_gu[0], b_gu[0], w_down[0], b_down[0])
    ya = _sc_gather_rows(y_buf, dest.T.reshape(A)).reshape(TOP_K, T, D // 2)
    out = _combine(ya, x1, gates, norm_final)
    return out.reshape(B, S, D)
```

```python
import functools
import math

import numpy as np
import jax
import jax.numpy as jnp
from jax import lax
from jax.experimental import pallas as pl
from jax.experimental.pallas import tpu as pltpu
from jax.experimental.pallas import tpu_sc as plsc

A_HEADS = 8
A_HEAD_DIM = 64
IDX_HEADS = 8
IDX_DIM = 32
TOPK_MAX = 256
REL_BUCKETS = 32
REL_MAX_DIST = 128
B_HEADS = 4
B_KEY_DIM = 128
B_VAL_DIM = 128
N_EXPERTS = 32
TOP_K = 4
SWIGLU_LIMIT = 7.0
SWIGLU_ALPHA = 1.702
EPS = 1e-6

A_WIDTH = A_HEADS * A_HEAD_DIM
B_WIDTH = B_HEADS * B_VAL_DIM
IDX_WIDTH = IDX_HEADS * IDX_DIM

V7X_LANES = 128
V7X_SUBLANES = 8
V7X_VMEM_LIMIT_BYTES = 56 * 1024 * 1024
V7X_SC_CORES = 2
V7X_SC_SUBCORES = 16

PROJ_ROWS = 512
ATT_Q = 256
ATT_KC = 128
HGRN_ROWS = 512
HGRN_CHUNK = 64
HGRN_SAFE_DECAY = 70.0
ROUTE_ROWS = 512
EXPERT_ROWS = 256
COMBINE_ROWS = 512
SC_GATHER_ROWS = 64
MASK_NEG = -1e30
BISECT_FAST_ITERS = 26


def _cparams(dims):
    return pltpu.CompilerParams(dimension_semantics=dims, vmem_limit_bytes=V7X_VMEM_LIMIT_BYTES)


def _rms(x, gain):
    return x * lax.rsqrt(jnp.mean(x * x, axis=-1, keepdims=True) + EPS) * gain


def _sigmoid(x):
    return 1.0 / (1.0 + jnp.exp(-x))


def _pack_bf16_pairs(x):
    n = x.shape[1] // 2
    as_bits = lambda v: lax.bitcast_convert_type(v.astype(jnp.bfloat16).astype(jnp.float32), jnp.uint32)
    return (as_bits(x[:, :n]) & jnp.uint32(0xFFFF0000)) | (as_bits(x[:, n:]) >> 16)


def _unpack_bf16_pairs(w):
    hi = lax.bitcast_convert_type(w & jnp.uint32(0xFFFF0000), jnp.float32).astype(jnp.bfloat16)
    lo = lax.bitcast_convert_type(w << 16, jnp.float32).astype(jnp.bfloat16)
    return hi, lo


def _fold_rows(x, op):
    return op(x.reshape(x.shape[0] // V7X_SUBLANES, V7X_SUBLANES, x.shape[1]), axis=0)


def _inproj_kernel(x_ref, g_ref, wk_ref, wik_ref, wqT_ref, wvT_ref, wiqT_ref, wiwT_ref, wb_ref, wg_ref,
                   k_ref, ik_ref, qT_ref, vT_ref, iqT_ref, iwT_ref, bq_ref, bf_ref, bi_ref, bg_ref,
                   ga_ref, gb_ref):
    x = x_ref[...]
    hn = _rms(x, g_ref[...]).astype(jnp.bfloat16)

    def mm(w_ref):
        return jnp.dot(hn, w_ref[...], preferred_element_type=jnp.float32)

    def mm_t(w_ref):
        return lax.dot_general(w_ref[...], hn, (((1,), (1,)), ((), ())),
                               preferred_element_type=jnp.float32)

    k_ref[...] = mm(wk_ref).astype(jnp.bfloat16)
    ik_ref[...] = mm(wik_ref).astype(jnp.bfloat16)
    qT_ref[0] = (mm_t(wqT_ref) * (A_HEAD_DIM ** -0.5)).astype(jnp.bfloat16)
    vT_ref[0] = mm_t(wvT_ref).astype(jnp.bfloat16)
    iqT_ref[0] = mm_t(wiqT_ref).astype(jnp.bfloat16)
    iwT_ref[0] = mm_t(wiwT_ref) * ((IDX_HEADS * IDX_DIM) ** -0.5)
    hb = mm(wb_ref)
    bq_ref[...] = hb[:, 0 * B_WIDTH:1 * B_WIDTH]
    bf_ref[...] = hb[:, 1 * B_WIDTH:2 * B_WIDTH]
    bi_ref[...] = hb[:, 2 * B_WIDTH:3 * B_WIDTH]
    bg_ref[...] = hb[:, 3 * B_WIDTH:4 * B_WIDTH]
    d = ga_ref.shape[-1]
    hg = mm(wg_ref)
    ga_ref[...] = _sigmoid(hg[:, :d]).astype(jnp.bfloat16)
    gb_ref[...] = _sigmoid(hg[:, d:]).astype(jnp.bfloat16)


def _inproj(x2, gain, w_in, B, S):
    T, D = x2.shape
    R = min(PROJ_ROWS, S)
    nS = S // R
    o = np.cumsum((0, A_WIDTH, A_WIDTH, A_WIDTH, IDX_WIDTH, IDX_HEADS, IDX_DIM,
                   B_WIDTH, B_WIDTH, B_WIDTH, B_WIDTH, D, D))
    bf = jnp.bfloat16
    wqT = w_in[:, o[0]:o[1]].T.astype(bf)
    wk = w_in[:, o[1]:o[2]].astype(bf)
    wvT = w_in[:, o[2]:o[3]].T.astype(bf)
    wiqT = w_in[:, o[3]:o[4]].T.astype(bf)
    wiwT = w_in[:, o[4]:o[5]].T.astype(bf)
    wik = w_in[:, o[5]:o[6]].astype(bf)
    wb = w_in[:, o[6]:o[10]].astype(bf)
    wg = w_in[:, o[10]:o[12]].astype(bf)

    def full(a):
        return pl.BlockSpec(a.shape, lambda b, i: (0,) * a.ndim)

    row = lambda n: pl.BlockSpec((R, n), lambda b, i: (b * nS + i, 0))
    colT = lambda n: pl.BlockSpec((1, n, R), lambda b, i: (b, 0, i))
    f32 = jnp.float32
    outs = [
        (jax.ShapeDtypeStruct((T, A_WIDTH), bf), row(A_WIDTH)),
        (jax.ShapeDtypeStruct((T, IDX_DIM), bf), row(IDX_DIM)),
        (jax.ShapeDtypeStruct((B, A_WIDTH, S), bf), colT(A_WIDTH)),
        (jax.ShapeDtypeStruct((B, A_WIDTH, S), bf), colT(A_WIDTH)),
        (jax.ShapeDtypeStruct((B, IDX_WIDTH, S), bf), colT(IDX_WIDTH)),
        (jax.ShapeDtypeStruct((B, IDX_HEADS, S), f32), colT(IDX_HEADS)),
        (jax.ShapeDtypeStruct((T, B_WIDTH), f32), row(B_WIDTH)),
        (jax.ShapeDtypeStruct((T, B_WIDTH), f32), row(B_WIDTH)),
        (jax.ShapeDtypeStruct((T, B_WIDTH), f32), row(B_WIDTH)),
        (jax.ShapeDtypeStruct((T, B_WIDTH), f32), row(B_WIDTH)),
        (jax.ShapeDtypeStruct((T, D), bf), row(D)),
        (jax.ShapeDtypeStruct((T, D), bf), row(D)),
    ]
    ins = [x2, gain.reshape(1, D), wk, wik, wqT, wvT, wiqT, wiwT, wb, wg]
    in_specs = [row(D)] + [full(a) for a in ins[1:]]
    return pl.pallas_call(
        _inproj_kernel,
        grid=(B, nS),
        in_specs=in_specs,
        out_specs=[s for _, s in outs],
        out_shape=[s for s, _ in outs],
        compiler_params=_cparams(("parallel", "parallel")),
        name="inproj",
    )(*ins)


def _t5_bucket_table(n):
    d = np.arange(n)
    max_exact = REL_BUCKETS // 2
    nf = np.maximum(d, 1).astype(np.float64)
    large = max_exact + (np.log(nf / max_exact) / math.log(REL_MAX_DIST / max_exact)
                         * (REL_BUCKETS - max_exact)).astype(np.int32)
    large = np.minimum(large, REL_BUCKETS - 1)
    return np.where(d < max_exact, d, large)


def _dsa_kernel(qT_ref, k_ref, vT_ref, iqT_ref, iwT_ref, ik_ref, enear_ref,
                o_ref, sc_ref, qh_scr, m_scr, l_scr, acc_scr, *, topk):
    TQ = qT_ref.shape[2]
    KC = TQ
    i = pl.program_id(1)
    nch = i + 1
    q0 = i * TQ
    f32 = jnp.float32
    bf16 = jnp.bfloat16
    key_id = lax.broadcasted_iota(jnp.int32, (KC, TQ), 0)
    qry_id = lax.broadcasted_iota(jnp.int32, (KC, TQ), 1)

    def col_reduce(x, op):
        return op(_fold_rows(x, op), axis=0, keepdims=True)

    iw = iwT_ref[0]

    def score_chunk(c, carry):
        rmin, rmax = carry
        k0 = pl.multiple_of(c * KC, KC)
        ik = ik_ref[pl.ds(k0, KC), :]
        acc = jnp.zeros((KC, TQ), f32)
        for h in range(IDX_HEADS):
            sh = jnp.dot(ik, iqT_ref[0, h * IDX_DIM:(h + 1) * IDX_DIM, :], preferred_element_type=f32)
            acc = acc + jnp.maximum(sh, 0.0) * iw[h:h + 1, :]
        valid = (k0 + key_id) <= (q0 + qry_id)
        sc_ref[pl.ds(k0, KC), :] = jnp.where(valid, acc, MASK_NEG)
        rmin = jnp.minimum(rmin, _fold_rows(jnp.where(valid, acc, -MASK_NEG), jnp.min))
        rmax = jnp.maximum(rmax, _fold_rows(jnp.where(valid, acc, MASK_NEG), jnp.max))
        return rmin, rmax

    rmin8, rmax8 = lax.fori_loop(
        0, nch, score_chunk,
        (jnp.full((V7X_SUBLANES, TQ), -MASK_NEG, f32), jnp.full((V7X_SUBLANES, TQ), MASK_NEG, f32)))
    rmin = jnp.min(rmin8, axis=0, keepdims=True)
    rmax = jnp.max(rmax8, axis=0, keepdims=True)

    def count_where(pred_fn):
        def body(c, acc):
            k0 = pl.multiple_of(c * KC, KC)
            blk = sc_ref[pl.ds(k0, KC), :]
            return acc + _fold_rows(jnp.where(pred_fn(blk), 1.0, 0.0), jnp.sum)
        acc = lax.fori_loop(0, nch, body, jnp.zeros((V7X_SUBLANES, TQ), f32))
        return jnp.sum(acc, axis=0, keepdims=True)

    def band_min_max(lo, hi):
        def body(c, carry):
            bmin, bmax = carry
            k0 = pl.multiple_of(c * KC, KC)
            blk = sc_ref[pl.ds(k0, KC), :]
            bmin = jnp.minimum(bmin, _fold_rows(jnp.where(blk >= lo, blk, -MASK_NEG), jnp.min))
            bmax = jnp.maximum(bmax, _fold_rows(jnp.where(blk < hi, blk, MASK_NEG), jnp.max))
            return bmin, bmax
        bmin8, bmax8 = lax.fori_loop(
            0, nch, body,
            (jnp.full((V7X_SUBLANES, TQ), -MASK_NEG, f32), jnp.full((V7X_SUBLANES, TQ), MASK_NEG, f32)))
        return jnp.min(bmin8, axis=0, keepdims=True), jnp.max(bmax8, axis=0, keepdims=True)

    kf = float(topk)
    n_valid = (q0 + 1 + lax.broadcasted_iota(jnp.int32, (1, TQ), 1)).astype(f32)
    c_top = count_where(lambda blk: blk >= rmax)
    top_tie = c_top >= kf
    lo0 = jnp.where(top_tie, rmax, rmin)
    cnt0 = jnp.where(top_tie, c_top, n_valid)
    done0 = jnp.where(top_tie | (cnt0 <= kf), 1.0, 0.0)

    def cond(st):
        return jnp.min(st[-1]) < 0.5

    def body(st):
        it, lo, hi, cnt, chi, done = st

        def plain(_):
            half = lo + 0.5 * (hi - lo)
            stuck = (half <= lo) | (half >= hi)
            guess = lo + (hi - lo) * ((cnt - kf + 0.5) / (cnt - chi))
            use_guess = (it % 2 == 0) & (guess > lo) & (guess < hi)
            return lo, jnp.where(use_guess, guess, half), jnp.where(stuck, 1.0, 0.0)

        def snapped(_):
            bmin, bmax = band_min_max(lo, hi)
            mid = bmin + 0.5 * (bmax - bmin)
            mid = jnp.where(mid <= bmin, bmax, mid)
            return bmin, mid, jnp.where(bmax <= bmin, 1.0, 0.0)

        lo_s, mid, tie_f = lax.cond(it < BISECT_FAST_ITERS, plain, snapped, None)
        tie = tie_f > 0.5
        active = done < 0.5
        lo_s = jnp.where(active, lo_s, lo)
        c = count_where(lambda blk: blk >= mid)
        feas = c >= kf
        move = active & jnp.logical_not(tie)
        lo_n = jnp.where(move & feas, mid, lo_s)
        cnt_n = jnp.where(move & feas, c, cnt)
        hi_n = jnp.where(move & jnp.logical_not(feas), mid, hi)
        chi_n = jnp.where(move & jnp.logical_not(feas), c, chi)
        done_n = jnp.where((active & tie) | (cnt_n <= kf), 1.0, done)
        return it + 1, lo_n, hi_n, cnt_n, chi_n, done_n

    thr = lax.while_loop(cond, body, (jnp.int32(0), lo0, rmax, cnt0, c_top, done0))[1]

    need = kf - count_where(lambda blk: blk > thr)
    tril = jnp.where(lax.broadcasted_iota(jnp.int32, (KC, KC), 1)
                     <= lax.broadcasted_iota(jnp.int32, (KC, KC), 0), 1.0, 0.0).astype(bf16)

    def mask_chunk(c, run):
        k0 = pl.multiple_of(c * KC, KC)
        blk = sc_ref[pl.ds(k0, KC), :]
        eq = jnp.where(blk == thr, 1.0, 0.0)
        pref = jnp.dot(tril, eq.astype(bf16), preferred_element_type=f32)
        sel = (blk > thr) | ((eq > 0.5) & (run + pref <= need))
        sc_ref[pl.ds(k0, KC), :] = jnp.where(sel, 0.0, MASK_NEG)
        return run + pref[KC - 1:KC, :]

    lax.fori_loop(0, nch, mask_chunk, jnp.zeros((1, TQ), f32))

    AK = min(ATT_KC, TQ)
    per = TQ // AK
    head0_q = (lax.broadcasted_iota(jnp.int32, (V7X_LANES, TQ), 0) // A_HEAD_DIM) == 0
    n_pairs = A_HEADS // 2

    m_scr[...] = jnp.full(m_scr.shape, MASK_NEG, f32)
    l_scr[...] = jnp.zeros(l_scr.shape, f32)
    acc_scr[...] = jnp.zeros(acc_scr.shape, f32)
    for p in range(n_pairs):
        q_pair = qT_ref[0, p * V7X_LANES:(p + 1) * V7X_LANES, :]
        zq = jnp.zeros_like(q_pair)
        qh_scr[2 * p] = jnp.where(head0_q, q_pair, zq)
        qh_scr[2 * p + 1] = jnp.where(head0_q, zq, q_pair)

    def step(c, bias_rows):
        k0 = pl.multiple_of(c * AK, AK)
        msk = sc_ref[pl.ds(k0, AK), :]
        for p in range(n_pairs):
            kp = k_ref[pl.ds(k0, AK), p * V7X_LANES:(p + 1) * V7X_LANES]
            vp = vT_ref[0, p * V7X_LANES:(p + 1) * V7X_LANES, pl.ds(k0, AK)]
            for sub in range(2):
                h = 2 * p + sub
                s = jnp.dot(kp, qh_scr[h], preferred_element_type=f32) + msk
                if bias_rows is not None:
                    s = s + enear_ref[h, bias_rows, :]
                m = m_scr[h:h + 1, :]
                m_new = jnp.maximum(m, col_reduce(s, jnp.max))
                alpha = jnp.exp(m - m_new)
                pr = jnp.exp(s - m_new)
                m_scr[h:h + 1, :] = m_new
                l_scr[h:h + 1, :] = alpha * l_scr[h:h + 1, :] + col_reduce(pr, jnp.sum)
                acc_scr[h] = alpha * acc_scr[h] + jnp.dot(vp, pr.astype(bf16), preferred_element_type=f32)

    def far(blk, _):
        for jj in range(per):
            step(blk * per + jj, None)
        return 0

    def near(block, first_chunk):
        for jj in range(per):
            step(first_chunk + jj, slice(block * TQ + jj * AK, block * TQ + (jj + 1) * AK))

    lax.fori_loop(0, jnp.maximum(i - 1, 0), far, 0)

    @pl.when(i >= 1)
    def _():
        near(0, (i - 1) * per)

    near(1, i * per)
    for p in range(n_pairs):
        o_pair = jnp.where(head0_q, acc_scr[2 * p] / l_scr[2 * p:2 * p + 1, :],
                           acc_scr[2 * p + 1] / l_scr[2 * p + 1:2 * p + 2, :])
        o_ref[:, p * V7X_LANES:(p + 1) * V7X_LANES] = o_pair.T.astype(o_ref.dtype)


def _dsa(k, ik, qT, vT, iqT, iwT, rel_bias, B, S):
    T = k.shape[0]
    TQ = min(ATT_Q, S)
    nQ = S // TQ
    topk = min(TOPK_MAX, S // 4)
    buckets = _t5_bucket_table(2 * TQ + 1)
    assert np.all(_t5_bucket_table(S + 1)[TQ + 1:] == REL_BUCKETS - 1)
    j = np.arange(2 * TQ)[:, None]
    r = np.arange(TQ)[None, :]
    dist = np.maximum(r + TQ - j, 0)
    onehot = (jnp.asarray(buckets[dist], jnp.int32)[None]
              == jnp.arange(REL_BUCKETS, dtype=jnp.int32)[:, None, None]).astype(jnp.float32)
    rel = rel_bias.astype(jnp.float32) - rel_bias[REL_BUCKETS - 1].astype(jnp.float32)[None, :]
    enear = jnp.einsum('nh,njr->hjr', rel, onehot, precision=lax.Precision.HIGHEST)

    return pl.pallas_call(
        functools.partial(_dsa_kernel, topk=topk),
        grid=(B, nQ),
        in_specs=[
            pl.BlockSpec((1, A_WIDTH, TQ), lambda b, i: (b, 0, i)),
            pl.BlockSpec((S, A_WIDTH), lambda b, i: (b, 0)),
            pl.BlockSpec((1, A_WIDTH, S), lambda b, i: (b, 0, 0)),
            pl.BlockSpec((1, IDX_WIDTH, TQ), lambda b, i: (b, 0, i)),
            pl.BlockSpec((1, IDX_HEADS, TQ), lambda b, i: (b, 0, i)),
            pl.BlockSpec((S, IDX_DIM), lambda b, i: (b, 0)),
            pl.BlockSpec((A_HEADS, 2 * TQ, TQ), lambda b, i: (0, 0, 0)),
        ],
        out_specs=pl.BlockSpec((TQ, A_WIDTH), lambda b, i: (b * nQ + i, 0)),
        scratch_shapes=[pltpu.VMEM((S, TQ), jnp.float32),
                        pltpu.VMEM((A_HEADS, V7X_LANES, TQ), jnp.bfloat16),
                        pltpu.VMEM((A_HEADS, TQ), jnp.float32),
                        pltpu.VMEM((A_HEADS, TQ), jnp.float32),
                        pltpu.VMEM((A_HEADS, V7X_LANES, TQ), jnp.float32)],
        out_shape=jax.ShapeDtypeStruct((T, A_WIDTH), jnp.bfloat16),
        compiler_params=_cparams(("parallel", "arbitrary")),
        name="dsa",
    )(qT, k, vT, iqT, iwT, ik, enear)


def _hgrn_kernel(bq_ref, bf_ref, bi_ref, bg_ref, lb_ref, gain_ref, o_ref,
                 st_ref, b_scr, q_scr, k_scr, v_scr, oi_scr):
    R = bq_ref.shape[0]
    C = HGRN_CHUNK
    nC = R // C
    f32 = jnp.float32
    bf16 = jnp.bfloat16
    h = pl.program_id(1)

    @pl.when(pl.program_id(2) == 0)
    def _():
        st_ref[...] = jnp.zeros_like(st_ref)

    lb = lb_ref[pl.ds(h, 1), :]
    gain = gain_ref[pl.ds(h, 1), :]
    tril_incl = jnp.where(lax.broadcasted_iota(jnp.int32, (C, C), 1)
                          <= lax.broadcasted_iota(jnp.int32, (C, C), 0), 1.0, 0.0)
    srow = lax.broadcasted_iota(jnp.int32, (C, B_KEY_DIM), 0)

    def gates(r0):
        f = lb + (1.0 - lb) * _sigmoid(bf_ref[pl.ds(r0, C), :])
        qr = bq_ref[pl.ds(r0, C), :]
        return jnp.log(f), 1.0 - f, qr * _sigmoid(qr) * (B_KEY_DIM ** -0.5), bi_ref[pl.ds(r0, C), :]

    def chunk(r0, st, factorised):
        g, kk, qq, vv = gates(r0)
        b = jnp.dot(tril_incl, g, preferred_element_type=f32, precision=lax.Precision.HIGHEST)
        b_last = b[C - 1:C, :]
        qd = (qq * jnp.exp(b)).astype(bf16)
        o_inter = lax.dot_general(qd, st.astype(bf16), (((1,), (1,)), ((), ())), preferred_element_type=f32)
        if factorised:
            kd = (kk * jnp.exp(-b)).astype(bf16)
            att = lax.dot_general(qd, kd, (((1,), (1,)), ((), ())), preferred_element_type=f32) * tril_incl
            o_intra = jnp.dot(att.astype(bf16), vv.astype(bf16), preferred_element_type=f32)
        else:
            b_scr[...] = b
            q_scr[...] = qq
            k_scr[...] = kk
            v_scr[...] = vv

            def row(t, _):
                bt = b_scr[pl.ds(t, 1), :]
                qt = q_scr[pl.ds(t, 1), :]
                ex = jnp.where(srow <= t, bt - b_scr[...], -jnp.inf)
                a = jnp.sum(qt * k_scr[...] * jnp.exp(ex), axis=1, keepdims=True)
                oi_scr[pl.ds(t, 1), :] = jnp.sum(a * v_scr[...], axis=0, keepdims=True)
                return 0
            lax.fori_loop(0, C, row, 0)
            o_intra = oi_scr[...]
        o = o_inter + o_intra
        kd_last = (kk * jnp.exp(b_last - b)).astype(bf16)
        upd = lax.dot_general(vv.astype(bf16), kd_last, (((0,), (0,)), ((), ())),
                              preferred_element_type=f32)
        og = bg_ref[pl.ds(r0, C), :]
        y = _rms(o, gain) * (og * _sigmoid(og))
        o_ref[pl.ds(r0, C), :] = y.astype(o_ref.dtype)
        return st * jnp.exp(b_last) + upd

    f_all = lb + (1.0 - lb) * _sigmoid(bf_ref[...])
    decay = jnp.sum(jnp.log(f_all).reshape(nC, C, B_KEY_DIM), axis=1)
    safe = jnp.min(decay) >= -HGRN_SAFE_DECAY

    @pl.when(safe)
    def _():
        st = st_ref[...]
        for c in range(nC):
            st = chunk(c * C, st, True)
        st_ref[...] = st

    @pl.when(jnp.logical_not(safe))
    def _():
        st_ref[...] = lax.fori_loop(
            0, nC, lambda c, st: chunk(pl.multiple_of(c * C, C), st, False), st_ref[...])


def _hgrn(bq, bf, bi, bg, lb, gain, B, S):
    T = bq.shape[0]
    R = min(HGRN_ROWS, S)
    nR = S // R
    C = HGRN_CHUNK
    blk = pl.BlockSpec((R, B_KEY_DIM), lambda b, h, c: (b * nR + c, h))
    small = pl.BlockSpec((B_HEADS, B_KEY_DIM), lambda b, h, c: (0, 0))
    f32 = jnp.float32
    return pl.pallas_call(
        _hgrn_kernel,
        grid=(B, B_HEADS, nR),
        in_specs=[blk, blk, blk, blk, small, small],
        out_specs=blk,
        out_shape=jax.ShapeDtypeStruct((T, B_WIDTH), jnp.bfloat16),
        scratch_shapes=[pltpu.VMEM((B_VAL_DIM, B_KEY_DIM), f32)] +
                       [pltpu.VMEM((C, B_KEY_DIM), f32) for _ in range(5)],
        compiler_params=_cparams(("parallel", "parallel", "arbitrary")),
        name="hgrn",
    )(bq, bf, bi, bg, lb, gain)


def _merge_kernel(x_ref, ya_ref, yb_ref, ga_ref, gb_ref, wa_ref, wb_ref, wo_ref, g_ref, wr_ref, br_ref,
                  x1_ref, xn_ref, lg_ref):
    f32 = jnp.float32
    ma = jnp.dot(ya_ref[...], wa_ref[...], preferred_element_type=f32)
    mb = jnp.dot(yb_ref[...], wb_ref[...], preferred_element_type=f32)
    merged = ga_ref[...].astype(f32) * ma + gb_ref[...].astype(f32) * mb
    x1 = x_ref[...] + jnp.dot(merged.astype(jnp.bfloat16), wo_ref[...], preferred_element_type=f32)
    x1_ref[...] = x1
    hn = _rms(x1, g_ref[...])
    xn_ref[...] = _pack_bf16_pairs(hn)
    lg_ref[...] = jnp.dot(hn, wr_ref[...], preferred_element_type=f32,
                          precision=lax.Precision.HIGHEST) + br_ref[...]


def _merge(x2, ya, yb, ga, gb, w_up_a, w_up_b, w_out, gain, w_router, b_router):
    T, D = x2.shape
    R = min(PROJ_ROWS, T)
    bf = jnp.bfloat16
    ins = [x2, ya, yb, ga, gb, w_up_a.astype(bf), w_up_b.astype(bf), w_out.astype(bf),
           gain.reshape(1, D), w_router, b_router.reshape(1, N_EXPERTS)]
    row = lambda n: pl.BlockSpec((R, n), lambda i: (i, 0))
    full = lambda a: pl.BlockSpec(a.shape, lambda i: (0,) * a.ndim)
    in_specs = [row(D), row(A_WIDTH), row(B_WIDTH), row(D), row(D)] + [full(a) for a in ins[5:]]
    return pl.pallas_call(
        _merge_kernel,
        grid=(T // R,),
        in_specs=in_specs,
        out_specs=[row(D), row(D // 2), row(N_EXPERTS)],
        out_shape=[jax.ShapeDtypeStruct((T, D), jnp.float32), jax.ShapeDtypeStruct((T, D // 2), jnp.uint32),
                   jax.ShapeDtypeStruct((T, N_EXPERTS), jnp.float32)],
        compiler_params=_cparams(("parallel",)),
        name="merge",
    )(*ins)


def _route_kernel(lg_ref, eidx_ref, gate_ref, rank_ref, cnt_ref, run_ref):
    R = lg_ref.shape[0]
    f32 = jnp.float32

    @pl.when(pl.program_id(0) == 0)
    def _():
        run_ref[...] = jnp.zeros_like(run_ref)

    lg = lg_ref[...]
    lane = lax.broadcasted_iota(jnp.int32, (R, N_EXPERTS), 1)
    work = lg
    onehots, vals, idxs = [], [], []
    for _ in range(TOP_K):
        m = jnp.max(work, axis=1, keepdims=True)
        idx = jnp.min(jnp.where(work == m, lane, N_EXPERTS), axis=1, keepdims=True)
        oh = lane == idx
        onehots.append(oh)
        vals.append(m)
        idxs.append(idx)
        work = jnp.where(oh, -jnp.inf, work)
    ex = [jnp.exp(v - vals[0]) for v in vals]
    den = ex[0] + ex[1] + ex[2] + ex[3]
    chosen = jnp.where(onehots[0] | onehots[1] | onehots[2] | onehots[3], 1.0, 0.0)
    strict = jnp.where(lax.broadcasted_iota(jnp.int32, (R, R), 1)
                       < lax.broadcasted_iota(jnp.int32, (R, R), 0), 1.0, 0.0).astype(jnp.bfloat16)
    before = jnp.dot(strict, chosen.astype(jnp.bfloat16), preferred_element_type=f32) + run_ref[...]
    lane4 = lax.broadcasted_iota(jnp.int32, (R, TOP_K), 1)
    eidx = jnp.zeros((R, TOP_K), jnp.int32)
    gate = jnp.zeros((R, TOP_K), f32)
    rank = jnp.zeros((R, TOP_K), f32)
    for k in range(TOP_K):
        eidx = jnp.where(lane4 == k, idxs[k], eidx)
        gate = jnp.where(lane4 == k, ex[k] / den, gate)
        rk = jnp.sum(jnp.where(onehots[k], before, 0.0), axis=1, keepdims=True)
        rank = jnp.where(lane4 == k, rk, rank)
    eidx_ref[...] = eidx
    gate_ref[...] = gate
    rank_ref[...] = rank.astype(jnp.int32)
    run_ref[...] = run_ref[...] + jnp.sum(chosen, axis=0, keepdims=True)
    cnt_ref[...] = run_ref[...].astype(jnp.int32)


def _route(logits):
    T = logits.shape[0]
    R = min(ROUTE_ROWS, T)
    row = lambda n: pl.BlockSpec((R, n), lambda i: (i, 0))
    return pl.pallas_call(
        _route_kernel,
        grid=(T // R,),
        in_specs=[row(N_EXPERTS)],
        out_specs=[row(TOP_K), row(TOP_K), row(TOP_K), pl.BlockSpec((1, N_EXPERTS), lambda i: (0, 0))],
        out_shape=[jax.ShapeDtypeStruct((T, TOP_K), jnp.int32), jax.ShapeDtypeStruct((T, TOP_K), jnp.float32),
                   jax.ShapeDtypeStruct((T, TOP_K), jnp.int32), jax.ShapeDtypeStruct((1, N_EXPERTS), jnp.int32)],
        scratch_shapes=[pltpu.VMEM((1, N_EXPERTS), jnp.float32)],
        compiler_params=_cparams(("arbitrary",)),
        name="route",
    )(logits)


def _sc_gather_rows(table, idx):
    N, W = table.shape
    M = idx.shape[0]
    workers = V7X_SC_CORES * V7X_SC_SUBCORES
    per_worker = M // workers
    assert per_worker * workers == M and per_worker % SC_GATHER_ROWS == 0
    mesh = plsc.VectorSubcoreMesh(core_axis_name="core", subcore_axis_name="subcore",
                                  num_cores=V7X_SC_CORES, num_subcores=V7X_SC_SUBCORES)

    @functools.partial(
        pl.kernel, mesh=mesh,
        out_type=jax.ShapeDtypeStruct((M, W), table.dtype),
        scratch_types=[pltpu.VMEM((per_worker,), jnp.int32),
                       pltpu.VMEM((SC_GATHER_ROWS, W), table.dtype),
                       pltpu.SemaphoreType.DMA],
    )
    def gather(table_hbm, idx_hbm, out_hbm, idx_v, rows_v, sem):
        worker = lax.axis_index("subcore") * V7X_SC_CORES + lax.axis_index("core")
        base = pl.multiple_of(worker * per_worker, SC_GATHER_ROWS)
        pltpu.sync_copy(idx_hbm.at[pl.ds(base, per_worker)], idx_v)

        @pl.loop(0, per_worker // SC_GATHER_ROWS)
        def _(g):
            off = pl.multiple_of(g * SC_GATHER_ROWS, SC_GATHER_ROWS)
            pltpu.async_copy(table_hbm.at[idx_v.at[pl.ds(off, SC_GATHER_ROWS)]], rows_v, sem).wait()
            pltpu.sync_copy(rows_v, out_hbm.at[pl.ds(base + off, SC_GATHER_ROWS)])

    return gather(table, idx)


def _experts_kernel(be_ref, nb_ref, x_ref, wgu_ref, bgu_ref, wd_ref, bd_ref, o_ref):
    f32 = jnp.float32
    d_ff = wd_ref.shape[1]

    @pl.when(pl.program_id(0) < nb_ref[0])
    def _():
        x_hi, x_lo = _unpack_bf16_pairs(x_ref[...])
        half = x_hi.shape[1]
        gu = (jnp.dot(x_hi, wgu_ref[0, :half, :], preferred_element_type=f32)
              + jnp.dot(x_lo, wgu_ref[0, half:, :], preferred_element_type=f32) + bgu_ref[0])
        gate = jnp.minimum(gu[:, :d_ff], SWIGLU_LIMIT)
        lin = jnp.clip(gu[:, d_ff:], -SWIGLU_LIMIT, SWIGLU_LIMIT)
        act = (lin + 1.0) * gate * _sigmoid(SWIGLU_ALPHA * gate)
        y = jnp.dot(act.astype(jnp.bfloat16), wd_ref[0], preferred_element_type=f32) + bd_ref[0]
        o_ref[...] = _pack_bf16_pairs(y)

    @pl.when(pl.program_id(0) >= nb_ref[0])
    def _():
        o_ref[...] = jnp.zeros_like(o_ref)


def _experts(xs, block_expert, n_used, w_gu, b_gu, w_down, b_down):
    P, W = xs.shape
    E, D, F2 = w_gu.shape
    nb = P // EXPERT_ROWS
    grid_spec = pltpu.PrefetchScalarGridSpec(
        num_scalar_prefetch=2,
        grid=(nb,),
        in_specs=[
            pl.BlockSpec((EXPERT_ROWS, W), lambda i, be, nu: (i, 0)),
            pl.BlockSpec((1, D, F2), lambda i, be, nu: (be[i], 0, 0)),
            pl.BlockSpec((1, 1, F2), lambda i, be, nu: (be[i], 0, 0)),
            pl.BlockSpec((1, F2 // 2, D), lambda i, be, nu: (be[i], 0, 0)),
            pl.BlockSpec((1, 1, D), lambda i, be, nu: (be[i], 0, 0)),
        ],
        out_specs=pl.BlockSpec((EXPERT_ROWS, W), lambda i, be, nu: (i, 0)),
    )
    return pl.pallas_call(
        _experts_kernel,
        grid_spec=grid_spec,
        out_shape=jax.ShapeDtypeStruct((P, W), jnp.uint32),
        compiler_params=_cparams(("arbitrary",)),
        name="experts",
    )(block_expert, n_used, xs, w_gu.astype(jnp.bfloat16), b_gu.reshape(E, 1, F2),
      w_down.astype(jnp.bfloat16), b_down.reshape(E, 1, D))


def _combine_kernel(ya_ref, x1_ref, gate_ref, g_ref, o_ref):
    half = x1_ref.shape[1] // 2
    f32 = jnp.float32
    gate = gate_ref[...]
    x1 = x1_ref[...]
    y_hi = x1[:, :half]
    y_lo = x1[:, half:]
    for k in range(TOP_K):
        hi, lo = _unpack_bf16_pairs(ya_ref[k])
        y_hi = y_hi + gate[:, k:k + 1] * hi.astype(f32)
        y_lo = y_lo + gate[:, k:k + 1] * lo.astype(f32)
    o_ref[...] = _rms(jnp.concatenate([y_hi, y_lo], axis=1), g_ref[...])


def _combine(ya, x1, gates, gain):
    T, D = x1.shape
    R = min(COMBINE_ROWS, T)
    row = lambda w: pl.BlockSpec((R, w), lambda i: (i, 0))
    return pl.pallas_call(
        _combine_kernel,
        grid=(T // R,),
        in_specs=[pl.BlockSpec((TOP_K, R, D // 2), lambda i: (0, i, 0)), row(D), row(TOP_K),
                  pl.BlockSpec((1, D), lambda i: (0, 0))],
        out_specs=row(D),
        out_shape=jax.ShapeDtypeStruct((T, D), jnp.float32),
        compiler_params=_cparams(("parallel",)),
        name="combine",
    )(ya, x1, gates, gain.reshape(1, D))


def _moe_plan(eidx, rank, counts, A):
    counts = counts.reshape(N_EXPERTS)
    padded = (counts + EXPERT_ROWS - 1) // EXPERT_ROWS * EXPERT_ROWS
    pad_ends = jnp.cumsum(padded)
    pad_starts = pad_ends - padded
    n_blocks = -(-A // EXPERT_ROWS) + N_EXPERTS
    dest = pad_starts[eidx] + rank
    block_start = jnp.arange(n_blocks, dtype=pad_ends.dtype) * EXPERT_ROWS
    block_expert = jnp.minimum(jnp.sum(pad_ends[None, :] <= block_start[:, None], axis=1),
                               N_EXPERTS - 1).astype(jnp.int32)
    n_used = (pad_ends[-1] // EXPERT_ROWS).astype(jnp.int32).reshape(1)
    return dest.astype(jnp.int32), block_expert, n_used, n_blocks


def kernel(x, w_in, w_up_a, w_up_b, w_out, norm_mix, norm_ffn, norm_final, hgrn_norm,
           lb_logits, rel_bias, w_router, b_router, w_gu, b_gu, w_down, b_down):
    B, S, D = x.shape
    T = B * S
    assert w_in.shape[0] == 1, "the final rmsnorm is fused into the single layer's combine stage"
    lb_all = jnp.cumsum(jax.nn.softmax(lb_logits.astype(jnp.float32), axis=0), axis=0)
    x2 = x.reshape(T, D)
    (k, ik, qT, vT, iqT, iwT, bq, bf, bi, bg, ga, gb) = _inproj(x2, norm_mix[0], w_in[0], B, S)
    ya = _dsa(k, ik, qT, vT, iqT, iwT, rel_bias, B, S)
    yb = _hgrn(bq, bf, bi, bg, lb_all[0].reshape(B_HEADS, B_KEY_DIM), hgrn_norm[0], B, S)
    x1, xn, logits = _merge(x2, ya, yb, ga, gb, w_up_a[0], w_up_b[0], w_out[0], norm_ffn[0],
                            w_router[0], b_router[0])
    eidx, gates, rank, counts = _route(logits)
    dest, block_expert, n_used, n_blocks = _moe_plan(eidx, rank, counts, T * TOP_K)
    P = n_blocks * EXPERT_ROWS
    A = T * TOP_K
    dest_flat = dest.reshape(A)
    buf_tok = jnp.zeros((P,), jnp.int32).at[dest_flat].set(
        jnp.arange(A, dtype=jnp.int32) // TOP_K, unique_indices=True)
    xs = _sc_gather_rows(xn, buf_tok)
    y_buf = _experts(xs, block_expert, n_used, w_gu[0], b_gu[0], w_down[0], b_down[0])
    ya = _sc_gather_rows(y_buf, dest.T.reshape(A)).reshape(TOP_K, T, D // 2)
    out = _combine(ya, x1, gates, norm_final)
    return out.reshape(B, S, D)
```

```python
import functools
import math

import numpy as np
import jax
import jax.numpy as jnp
from jax import lax
from jax.experimental import pallas as pl
from jax.experimental.pallas import tpu as pltpu
from jax.experimental.pallas import tpu_sc as plsc

A_HEADS = 8
A_HEAD_DIM = 64
IDX_HEADS = 8
IDX_DIM = 32
TOPK_MAX = 256
REL_BUCKETS = 32
REL_MAX_DIST = 128
B_HEADS = 4
B_KEY_DIM = 128
B_VAL_DIM = 128
N_EXPERTS = 32
TOP_K = 4
SWIGLU_LIMIT = 7.0
SWIGLU_ALPHA = 1.702
EPS = 1e-6

A_WIDTH = A_HEADS * A_HEAD_DIM
B_WIDTH = B_HEADS * B_VAL_DIM
IDX_WIDTH = IDX_HEADS * IDX_DIM

V7X_LANES = 128
V7X_SUBLANES = 8
V7X_VMEM_LIMIT_BYTES = 56 * 1024 * 1024
V7X_SC_CORES = 2
V7X_SC_SUBCORES = 16

PROJ_ROWS = 512
ATT_Q = 256
ATT_KC = 128
HGRN_ROWS = 512
HGRN_CHUNK = 64
HGRN_SAFE_DECAY = 70.0
ROUTE_ROWS = 512
EXPERT_ROWS = 256
COMBINE_ROWS = 512
SC_GATHER_ROWS = 64
MASK_NEG = -1e30
BISECT_FAST_ITERS = 26


def _cparams(dims):
    return pltpu.CompilerParams(dimension_semantics=dims, vmem_limit_bytes=V7X_VMEM_LIMIT_BYTES)


def _rms(x, gain):
    return x * lax.rsqrt(jnp.mean(x * x, axis=-1, keepdims=True) + EPS) * gain


def _sigmoid(x):
    return 1.0 / (1.0 + jnp.exp(-x))


def _pack_bf16_pairs(x):
    n = x.shape[1] // 2
    as_bits = lambda v: lax.bitcast_convert_type(v.astype(jnp.bfloat16).astype(jnp.float32), jnp.uint32)
    return (as_bits(x[:, :n]) & jnp.uint32(0xFFFF0000)) | (as_bits(x[:, n:]) >> 16)


def _unpack_bf16_pairs(w):
    hi = lax.bitcast_convert_type(w & jnp.uint32(0xFFFF0000), jnp.float32).astype(jnp.bfloat16)
    lo = lax.bitcast_convert_type(w << 16, jnp.float32).astype(jnp.bfloat16)
    return hi, lo


def _fold_rows(x, op):
    return op(x.reshape(x.shape[0] // V7X_SUBLANES, V7X_SUBLANES, x.shape[1]), axis=0)


def _inproj_kernel(x_ref, g_ref, wk_ref, wik_ref, wqT_ref, wvT_ref, wiqT_ref, wiwT_ref, wb_ref, wg_ref,
                   k_ref, ik_ref, qT_ref, vT_ref, iqT_ref, iwT_ref, bq_ref, bf_ref, bi_ref, bg_ref,
                   ga_ref, gb_ref):
    x = x_ref[...]
    hn = _rms(x, g_ref[...]).astype(jnp.bfloat16)

    def mm(w_ref):
        return jnp.dot(hn, w_ref[...], preferred_element_type=jnp.float32)

    def mm_t(w_ref):
        return lax.dot_general(w_ref[...], hn, (((1,), (1,)), ((), ())),
                               preferred_element_type=jnp.float32)

    k_ref[...] = mm(wk_ref).astype(jnp.bfloat16)
    ik_ref[...] = mm(wik_ref).astype(jnp.bfloat16)
    qT_ref[0] = (mm_t(wqT_ref) * (A_HEAD_DIM ** -0.5)).astype(jnp.bfloat16)
    vT_ref[0] = mm_t(wvT_ref).astype(jnp.bfloat16)
    iqT_ref[0] = mm_t(wiqT_ref).astype(jnp.bfloat16)
    iwT_ref[0] = mm_t(wiwT_ref) * ((IDX_HEADS * IDX_DIM) ** -0.5)
    hb = mm(wb_ref)
    bq_ref[...] = hb[:, 0 * B_WIDTH:1 * B_WIDTH]
    bf_ref[...] = hb[:, 1 * B_WIDTH:2 * B_WIDTH]
    bi_ref[...] = hb[:, 2 * B_WIDTH:3 * B_WIDTH]
    bg_ref[...] = hb[:, 3 * B_WIDTH:4 * B_WIDTH]
    d = ga_ref.shape[-1]
    hg = mm(wg_ref)
    ga_ref[...] = _sigmoid(hg[:, :d]).astype(jnp.bfloat16)
    gb_ref[...] = _sigmoid(hg[:, d:]).astype(jnp.bfloat16)


def _inproj(x2, gain, w_in, B, S):
    T, D = x2.shape
    R = min(PROJ_ROWS, S)
    nS = S // R
    o = np.cumsum((0, A_WIDTH, A_WIDTH, A_WIDTH, IDX_WIDTH, IDX_HEADS, IDX_DIM,
                   B_WIDTH, B_WIDTH, B_WIDTH, B_WIDTH, D, D))
    bf = jnp.bfloat16
    wqT = w_in[:, o[0]:o[1]].T.astype(bf)
    wk = w_in[:, o[1]:o[2]].astype(bf)
    wvT = w_in[:, o[2]:o[3]].T.astype(bf)
    wiqT = w_in[:, o[3]:o[4]].T.astype(bf)
    wiwT = w_in[:, o[4]:o[5]].T.astype(bf)
    wik = w_in[:, o[5]:o[6]].astype(bf)
    wb = w_in[:, o[6]:o[10]].astype(bf)
    wg = w_in[:, o[10]:o[12]].astype(bf)

    def full(a):
        return pl.BlockSpec(a.shape, lambda b, i: (0,) * a.ndim)

    row = lambda n: pl.BlockSpec((R, n), lambda b, i: (b * nS + i, 0))
    colT = lambda n: pl.BlockSpec((1, n, R), lambda b, i: (b, 0, i))
    f32 = jnp.float32
    outs = [
        (jax.ShapeDtypeStruct((T, A_WIDTH), bf), row(A_WIDTH)),
        (jax.ShapeDtypeStruct((T, IDX_DIM), bf), row(IDX_DIM)),
        (jax.ShapeDtypeStruct((B, A_WIDTH, S), bf), colT(A_WIDTH)),
        (jax.ShapeDtypeStruct((B, A_WIDTH, S), bf), colT(A_WIDTH)),
        (jax.ShapeDtypeStruct((B, IDX_WIDTH, S), bf), colT(IDX_WIDTH)),
        (jax.ShapeDtypeStruct((B, IDX_HEADS, S), f32), colT(IDX_HEADS)),
        (jax.ShapeDtypeStruct((T, B_WIDTH), f32), row(B_WIDTH)),
        (jax.ShapeDtypeStruct((T, B_WIDTH), f32), row(B_WIDTH)),
        (jax.ShapeDtypeStruct((T, B_WIDTH), f32), row(B_WIDTH)),
        (jax.ShapeDtypeStruct((T, B_WIDTH), f32), row(B_WIDTH)),
        (jax.ShapeDtypeStruct((T, D), bf), row(D)),
        (jax.ShapeDtypeStruct((T, D), bf), row(D)),
    ]
    ins = [x2, gain.reshape(1, D), wk, wik, wqT, wvT, wiqT, wiwT, wb, wg]
    in_specs = [row(D)] + [full(a) for a in ins[1:]]
    return pl.pallas_call(
        _inproj_kernel,
        grid=(B, nS),
        in_specs=in_specs,
        out_specs=[s for _, s in outs],
        out_shape=[s for s, _ in outs],
        compiler_params=_cparams(("parallel", "parallel")),
        name="inproj",
    )(*ins)


def _t5_bucket_table(n):
    d = np.arange(n)
    max_exact = REL_BUCKETS // 2
    nf = np.maximum(d, 1).astype(np.float64)
    large = max_exact + (np.log(nf / max_exact) / math.log(REL_MAX_DIST / max_exact)
                         * (REL_BUCKETS - max_exact)).astype(np.int32)
    large = np.minimum(large, REL_BUCKETS - 1)
    return np.where(d < max_exact, d, large)


def _dsa_kernel(qT_ref, k_ref, vT_ref, iqT_ref, iwT_ref, ik_ref, enear_ref,
                o_ref, sc_ref, qh_scr, m_scr, l_scr, acc_scr, *, topk):
    TQ = qT_ref.shape[2]
    KC = TQ
    i = pl.program_id(1)
    nch = i + 1
    q0 = i * TQ
    f32 = jnp.float32
    bf16 = jnp.bfloat16
    key_id = lax.broadcasted_iota(jnp.int32, (KC, TQ), 0)
    qry_id = lax.broadcasted_iota(jnp.int32, (KC, TQ), 1)

    def col_reduce(x, op):
        return op(_fold_rows(x, op), axis=0, keepdims=True)

    iw = iwT_ref[0]

    def score_chunk(c, carry):
        rmin, rmax = carry
        k0 = pl.multiple_of(c * KC, KC)
        ik = ik_ref[pl.ds(k0, KC), :]
        acc = jnp.zeros((KC, TQ), f32)
        for h in range(IDX_HEADS):
            sh = jnp.dot(ik, iqT_ref[0, h * IDX_DIM:(h + 1) * IDX_DIM, :], preferred_element_type=f32)
            acc = acc + jnp.maximum(sh, 0.0) * iw[h:h + 1, :]
        valid = (k0 + key_id) <= (q0 + qry_id)
        sc_ref[pl.ds(k0, KC), :] = jnp.where(valid, acc, MASK_NEG)
        rmin = jnp.minimum(rmin, _fold_rows(jnp.where(valid, acc, -MASK_NEG), jnp.min))
        rmax = jnp.maximum(rmax, _fold_rows(jnp.where(valid, acc, MASK_NEG), jnp.max))
        return rmin, rmax

    rmin8, rmax8 = lax.fori_loop(
        0, nch, score_chunk,
        (jnp.full((V7X_SUBLANES, TQ), -MASK_NEG, f32), jnp.full((V7X_SUBLANES, TQ), MASK_NEG, f32)))
    rmin = jnp.min(rmin8, axis=0, keepdims=True)
    rmax = jnp.max(rmax8, axis=0, keepdims=True)

    def count_where(pred_fn):
        def body(c, acc):
            k0 = pl.multiple_of(c * KC, KC)
            blk = sc_ref[pl.ds(k0, KC), :]
            return acc + _fold_rows(jnp.where(pred_fn(blk), 1.0, 0.0), jnp.sum)
        acc = lax.fori_loop(0, nch, body, jnp.zeros((V7X_SUBLANES, TQ), f32))
        return jnp.sum(acc, axis=0, keepdims=True)

    def band_min_max(lo, hi):
        def body(c, carry):
            bmin, bmax = carry
            k0 = pl.multiple_of(c * KC, KC)
            blk = sc_ref[pl.ds(k0, KC), :]
            bmin = jnp.minimum(bmin, _fold_rows(jnp.where(blk >= lo, blk, -MASK_NEG), jnp.min))
            bmax = jnp.maximum(bmax, _fold_rows(jnp.where(blk < hi, blk, MASK_NEG), jnp.max))
            return bmin, bmax
        bmin8, bmax8 = lax.fori_loop(
            0, nch, body,
            (jnp.full((V7X_SUBLANES, TQ), -MASK_NEG, f32), jnp.full((V7X_SUBLANES, TQ), MASK_NEG, f32)))
        return jnp.min(bmin8, axis=0, keepdims=True), jnp.max(bmax8, axis=0, keepdims=True)

    kf = float(topk)
    n_valid = (q0 + 1 + lax.broadcasted_iota(jnp.int32, (1, TQ), 1)).astype(f32)
    c_top = count_where(lambda blk: blk >= rmax)
    top_tie = c_top >= kf
    lo0 = jnp.where(top_tie, rmax, rmin)
    cnt0 = jnp.where(top_tie, c_top, n_valid)
    done0 = jnp.where(top_tie | (cnt0 <= kf), 1.0, 0.0)

    def cond(st):
        return jnp.min(st[-1]) < 0.5

    def body(st):
        it, lo, hi, cnt, done = st

        def plain(_):
            half = lo + 0.5 * (hi - lo)
            stuck = (half <= lo) | (half >= hi)
            return lo, half, jnp.where(stuck, 1.0, 0.0)

        def snapped(_):
            bmin, bmax = band_min_max(lo, hi)
            mid = bmin + 0.5 * (bmax - bmin)
            mid = jnp.where(mid <= bmin, bmax, mid)
            return bmin, mid, jnp.where(bmax <= bmin, 1.0, 0.0)

        lo_s, mid, tie_f = lax.cond(it < BISECT_FAST_ITERS, plain, snapped, None)
        tie = tie_f > 0.5
        active = done < 0.5
        lo_s = jnp.where(active, lo_s, lo)
        c = count_where(lambda blk: blk >= mid)
        feas = c >= kf
        move = active & jnp.logical_not(tie)
        lo_n = jnp.where(move & feas, mid, lo_s)
        cnt_n = jnp.where(move & feas, c, cnt)
        hi_n = jnp.where(move & jnp.logical_not(feas), mid, hi)
        done_n = jnp.where((active & tie) | (cnt_n <= kf), 1.0, done)
        return it + 1, lo_n, hi_n, cnt_n, done_n

    thr = lax.while_loop(cond, body, (jnp.int32(0), lo0, rmax, cnt0, done0))[1]

    need = kf - count_where(lambda blk: blk > thr)
    tril = jnp.where(lax.broadcasted_iota(jnp.int32, (KC, KC), 1)
                     <= lax.broadcasted_iota(jnp.int32, (KC, KC), 0), 1.0, 0.0).astype(bf16)

    def mask_chunk(c, run):
        k0 = pl.multiple_of(c * KC, KC)
        blk = sc_ref[pl.ds(k0, KC), :]
        eq = jnp.where(blk == thr, 1.0, 0.0)
        pref = jnp.dot(tril, eq.astype(bf16), preferred_element_type=f32)
        sel = (blk > thr) | ((eq > 0.5) & (run + pref <= need))
        sc_ref[pl.ds(k0, KC), :] = jnp.where(sel, 0.0, MASK_NEG)
        return run + pref[KC - 1:KC, :]

    lax.fori_loop(0, nch, mask_chunk, jnp.zeros((1, TQ), f32))

    AK = min(ATT_KC, TQ)
    per = TQ // AK
    head0_q = (lax.broadcasted_iota(jnp.int32, (V7X_LANES, TQ), 0) // A_HEAD_DIM) == 0
    n_pairs = A_HEADS // 2

    m_scr[...] = jnp.full(m_scr.shape, MASK_NEG, f32)
    l_scr[...] = jnp.zeros(l_scr.shape, f32)
    acc_scr[...] = jnp.zeros(acc_scr.shape, f32)
    for p in range(n_pairs):
        q_pair = qT_ref[0, p * V7X_LANES:(p + 1) * V7X_LANES, :]
        zq = jnp.zeros_like(q_pair)
        qh_scr[2 * p] = jnp.where(head0_q, q_pair, zq)
        qh_scr[2 * p + 1] = jnp.where(head0_q, zq, q_pair)

    def step(c, bias_rows):
        k0 = pl.multiple_of(c * AK, AK)
        msk = sc_ref[pl.ds(k0, AK), :]
        for p in range(n_pairs):
            kp = k_ref[pl.ds(k0, AK), p * V7X_LANES:(p + 1) * V7X_LANES]
            vp = vT_ref[0, p * V7X_LANES:(p + 1) * V7X_LANES, pl.ds(k0, AK)]
            for sub in range(2):
                h = 2 * p + sub
                s = jnp.dot(kp, qh_scr[h], preferred_element_type=f32) + msk
                if bias_rows is not None:
                    s = s + enear_ref[h, bias_rows, :]
                m = m_scr[h:h + 1, :]
                m_new = jnp.maximum(m, col_reduce(s, jnp.max))
                alpha = jnp.exp(m - m_new)
                pr = jnp.exp(s - m_new)
                m_scr[h:h + 1, :] = m_new
                l_scr[h:h + 1, :] = alpha * l_scr[h:h + 1, :] + col_reduce(pr, jnp.sum)
                acc_scr[h] = alpha * acc_scr[h] + jnp.dot(vp, pr.astype(bf16), preferred_element_type=f32)

    def far(blk, _):
        for jj in range(per):
            step(blk * per + jj, None)
        return 0

    def near(block, first_chunk):
        for jj in range(per):
            step(first_chunk + jj, slice(block * TQ + jj * AK, block * TQ + (jj + 1) * AK))

    lax.fori_loop(0, jnp.maximum(i - 1, 0), far, 0)

    @pl.when(i >= 1)
    def _():
        near(0, (i - 1) * per)

    near(1, i * per)
    for p in range(n_pairs):
        o_pair = jnp.where(head0_q, acc_scr[2 * p] / l_scr[2 * p:2 * p + 1, :],
                           acc_scr[2 * p + 1] / l_scr[2 * p + 1:2 * p + 2, :])
        o_ref[:, p * V7X_LANES:(p + 1) * V7X_LANES] = o_pair.T.astype(o_ref.dtype)


def _dsa(k, ik, qT, vT, iqT, iwT, rel_bias, B, S):
    T = k.shape[0]
    TQ = min(ATT_Q, S)
    nQ = S // TQ
    topk = min(TOPK_MAX, S // 4)
    buckets = _t5_bucket_table(2 * TQ + 1)
    assert np.all(_t5_bucket_table(S + 1)[TQ + 1:] == REL_BUCKETS - 1)
    j = np.arange(2 * TQ)[:, None]
    r = np.arange(TQ)[None, :]
    dist = np.maximum(r + TQ - j, 0)
    onehot = (jnp.asarray(buckets[dist], jnp.int32)[None]
              == jnp.arange(REL_BUCKETS, dtype=jnp.int32)[:, None, None]).astype(jnp.float32)
    rel = rel_bias.astype(jnp.float32) - rel_bias[REL_BUCKETS - 1].astype(jnp.float32)[None, :]
    enear = jnp.einsum('nh,njr->hjr', rel, onehot, precision=lax.Precision.HIGHEST)

    return pl.pallas_call(
        functools.partial(_dsa_kernel, topk=topk),
        grid=(B, nQ),
        in_specs=[
            pl.BlockSpec((1, A_WIDTH, TQ), lambda b, i: (b, 0, i)),
            pl.BlockSpec((S, A_WIDTH), lambda b, i: (b, 0)),
            pl.BlockSpec((1, A_WIDTH, S), lambda b, i: (b, 0, 0)),
            pl.BlockSpec((1, IDX_WIDTH, TQ), lambda b, i: (b, 0, i)),
            pl.BlockSpec((1, IDX_HEADS, TQ), lambda b, i: (b, 0, i)),
            pl.BlockSpec((S, IDX_DIM), lambda b, i: (b, 0)),
            pl.BlockSpec((A_HEADS, 2 * TQ, TQ), lambda b, i: (0, 0, 0)),
        ],
        out_specs=pl.BlockSpec((TQ, A_WIDTH), lambda b, i: (b * nQ + i, 0)),
        scratch_shapes=[pltpu.VMEM((S, TQ), jnp.float32),
                        pltpu.VMEM((A_HEADS, V7X_LANES, TQ), jnp.bfloat16),
                        pltpu.VMEM((A_HEADS, TQ), jnp.float32),
                        pltpu.VMEM((A_HEADS, TQ), jnp.float32),
                        pltpu.VMEM((A_HEADS, V7X_LANES, TQ), jnp.float32)],
        out_shape=jax.ShapeDtypeStruct((T, A_WIDTH), jnp.bfloat16),
        compiler_params=_cparams(("parallel", "arbitrary")),
        name="dsa",
    )(qT, k, vT, iqT, iwT, ik, enear)


def _hgrn_kernel(bq_ref, bf_ref, bi_ref, bg_ref, lb_ref, gain_ref, o_ref,
                 st_ref, b_scr, q_scr, k_scr, v_scr, oi_scr):
    R = bq_ref.shape[0]
    C = HGRN_CHUNK
    nC = R // C
    f32 = jnp.float32
    bf16 = jnp.bfloat16
    h = pl.program_id(1)

    @pl.when(pl.program_id(2) == 0)
    def _():
        st_ref[...] = jnp.zeros_like(st_ref)

    lb = lb_ref[pl.ds(h, 1), :]
    gain = gain_ref[pl.ds(h, 1), :]
    tril_incl = jnp.where(lax.broadcasted_iota(jnp.int32, (C, C), 1)
                          <= lax.broadcasted_iota(jnp.int32, (C, C), 0), 1.0, 0.0)
    srow = lax.broadcasted_iota(jnp.int32, (C, B_KEY_DIM), 0)

    def gates(r0):
        f = lb + (1.0 - lb) * _sigmoid(bf_ref[pl.ds(r0, C), :])
        qr = bq_ref[pl.ds(r0, C), :]
        return jnp.log(f), 1.0 - f, qr * _sigmoid(qr) * (B_KEY_DIM ** -0.5), bi_ref[pl.ds(r0, C), :]

    def chunk(r0, st, factorised):
        g, kk, qq, vv = gates(r0)
        b = jnp.dot(tril_incl, g, preferred_element_type=f32, precision=lax.Precision.HIGHEST)
        b_last = b[C - 1:C, :]
        qd = (qq * jnp.exp(b)).astype(bf16)
        o_inter = lax.dot_general(qd, st.astype(bf16), (((1,), (1,)), ((), ())), preferred_element_type=f32)
        if factorised:
            kd = (kk * jnp.exp(-b)).astype(bf16)
            att = lax.dot_general(qd, kd, (((1,), (1,)), ((), ())), preferred_element_type=f32) * tril_incl
            o_intra = jnp.dot(att.astype(bf16), vv.astype(bf16), preferred_element_type=f32)
        else:
            b_scr[...] = b
            q_scr[...] = qq
            k_scr[...] = kk
            v_scr[...] = vv

            def row(t, _):
                bt = b_scr[pl.ds(t, 1), :]
                qt = q_scr[pl.ds(t, 1), :]
                ex = jnp.where(srow <= t, bt - b_scr[...], -jnp.inf)
                a = jnp.sum(qt * k_scr[...] * jnp.exp(ex), axis=1, keepdims=True)
                oi_scr[pl.ds(t, 1), :] = jnp.sum(a * v_scr[...], axis=0, keepdims=True)
                return 0
            lax.fori_loop(0, C, row, 0)
            o_intra = oi_scr[...]
        o = o_inter + o_intra
        kd_last = (kk * jnp.exp(b_last - b)).astype(bf16)
        upd = lax.dot_general(vv.astype(bf16), kd_last, (((0,), (0,)), ((), ())),
                              preferred_element_type=f32)
        og = bg_ref[pl.ds(r0, C), :]
        y = _rms(o, gain) * (og * _sigmoid(og))
        o_ref[pl.ds(r0, C), :] = y.astype(o_ref.dtype)
        return st * jnp.exp(b_last) + upd

    f_all = lb + (1.0 - lb) * _sigmoid(bf_ref[...])
    decay = jnp.sum(jnp.log(f_all).reshape(nC, C, B_KEY_DIM), axis=1)
    safe = jnp.min(decay) >= -HGRN_SAFE_DECAY

    @pl.when(safe)
    def _():
        st = st_ref[...]
        for c in range(nC):
            st = chunk(c * C, st, True)
        st_ref[...] = st

    @pl.when(jnp.logical_not(safe))
    def _():
        st_ref[...] = lax.fori_loop(
            0, nC, lambda c, st: chunk(pl.multiple_of(c * C, C), st, False), st_ref[...])


def _hgrn(bq, bf, bi, bg, lb, gain, B, S):
    T = bq.shape[0]
    R = min(HGRN_ROWS, S)
    nR = S // R
    C = HGRN_CHUNK
    blk = pl.BlockSpec((R, B_KEY_DIM), lambda b, h, c: (b * nR + c, h))
    small = pl.BlockSpec((B_HEADS, B_KEY_DIM), lambda b, h, c: (0, 0))
    f32 = jnp.float32
    return pl.pallas_call(
        _hgrn_kernel,
        grid=(B, B_HEADS, nR),
        in_specs=[blk, blk, blk, blk, small, small],
        out_specs=blk,
        out_shape=jax.ShapeDtypeStruct((T, B_WIDTH), jnp.bfloat16),
        scratch_shapes=[pltpu.VMEM((B_VAL_DIM, B_KEY_DIM), f32)] +
                       [pltpu.VMEM((C, B_KEY_DIM), f32) for _ in range(5)],
        compiler_params=_cparams(("parallel", "parallel", "arbitrary")),
        name="hgrn",
    )(bq, bf, bi, bg, lb, gain)


def _merge_kernel(x_ref, ya_ref, yb_ref, ga_ref, gb_ref, wa_ref, wb_ref, wo_ref, g_ref, wr_ref, br_ref,
                  x1_ref, xn_ref, lg_ref):
    f32 = jnp.float32
    ma = jnp.dot(ya_ref[...], wa_ref[...], preferred_element_type=f32)
    mb = jnp.dot(yb_ref[...], wb_ref[...], preferred_element_type=f32)
    merged = ga_ref[...].astype(f32) * ma + gb_ref[...].astype(f32) * mb
    x1 = x_ref[...] + jnp.dot(merged.astype(jnp.bfloat16), wo_ref[...], preferred_element_type=f32)
    x1_ref[...] = x1
    hn = _rms(x1, g_ref[...])
    xn_ref[...] = _pack_bf16_pairs(hn)
    lg_ref[...] = jnp.dot(hn, wr_ref[...], preferred_element_type=f32,
                          precision=lax.Precision.HIGHEST) + br_ref[...]


def _merge(x2, ya, yb, ga, gb, w_up_a, w_up_b, w_out, gain, w_router, b_router):
    T, D = x2.shape
    R = min(PROJ_ROWS, T)
    bf = jnp.bfloat16
    ins = [x2, ya, yb, ga, gb, w_up_a.astype(bf), w_up_b.astype(bf), w_out.astype(bf),
           gain.reshape(1, D), w_router, b_router.reshape(1, N_EXPERTS)]
    row = lambda n: pl.BlockSpec((R, n), lambda i: (i, 0))
    full = lambda a: pl.BlockSpec(a.shape, lambda i: (0,) * a.ndim)
    in_specs = [row(D), row(A_WIDTH), row(B_WIDTH), row(D), row(D)] + [full(a) for a in ins[5:]]
    return pl.pallas_call(
        _merge_kernel,
        grid=(T // R,),
        in_specs=in_specs,
        out_specs=[row(D), row(D // 2), row(N_EXPERTS)],
        out_shape=[jax.ShapeDtypeStruct((T, D), jnp.float32), jax.ShapeDtypeStruct((T, D // 2), jnp.uint32),
                   jax.ShapeDtypeStruct((T, N_EXPERTS), jnp.float32)],
        compiler_params=_cparams(("parallel",)),
        name="merge",
    )(*ins)


def _route_kernel(lg_ref, eidx_ref, gate_ref, rank_ref, cnt_ref, run_ref):
    R = lg_ref.shape[0]
    f32 = jnp.float32

    @pl.when(pl.program_id(0) == 0)
    def _():
        run_ref[...] = jnp.zeros_like(run_ref)

    lg = lg_ref[...]
    lane = lax.broadcasted_iota(jnp.int32, (R, N_EXPERTS), 1)
    work = lg
    onehots, vals, idxs = [], [], []
    for _ in range(TOP_K):
        m = jnp.max(work, axis=1, keepdims=True)
        idx = jnp.min(jnp.where(work == m, lane, N_EXPERTS), axis=1, keepdims=True)
        oh = lane == idx
        onehots.append(oh)
        vals.append(m)
        idxs.append(idx)
        work = jnp.where(oh, -jnp.inf, work)
    ex = [jnp.exp(v - vals[0]) for v in vals]
    den = ex[0] + ex[1] + ex[2] + ex[3]
    chosen = jnp.where(onehots[0] | onehots[1] | onehots[2] | onehots[3], 1.0, 0.0)
    strict = jnp.where(lax.broadcasted_iota(jnp.int32, (R, R), 1)
                       < lax.broadcasted_iota(jnp.int32, (R, R), 0), 1.0, 0.0).astype(jnp.bfloat16)
    before = jnp.dot(strict, chosen.astype(jnp.bfloat16), preferred_element_type=f32) + run_ref[...]
    lane4 = lax.broadcasted_iota(jnp.int32, (R, TOP_K), 1)
    eidx = jnp.zeros((R, TOP_K), jnp.int32)
    gate = jnp.zeros((R, TOP_K), f32)
    rank = jnp.zeros((R, TOP_K), f32)
    for k in range(TOP_K):
        eidx = jnp.where(lane4 == k, idxs[k], eidx)
        gate = jnp.where(lane4 == k, ex[k] / den, gate)
        rk = jnp.sum(jnp.where(onehots[k], before, 0.0), axis=1, keepdims=True)
        rank = jnp.where(lane4 == k, rk, rank)
    eidx_ref[...] = eidx
    gate_ref[...] = gate
    rank_ref[...] = rank.astype(jnp.int32)
    run_ref[...] = run_ref[...] + jnp.sum(chosen, axis=0, keepdims=True)
    cnt_ref[...] = run_ref[...].astype(jnp.int32)


def _route(logits):
    T = logits.shape[0]
    R = min(ROUTE_ROWS, T)
    row = lambda n: pl.BlockSpec((R, n), lambda i: (i, 0))
    return pl.pallas_call(
        _route_kernel,
        grid=(T // R,),
        in_specs=[row(N_EXPERTS)],
        out_specs=[row(TOP_K), row(TOP_K), row(TOP_K), pl.BlockSpec((1, N_EXPERTS), lambda i: (0, 0))],
        out_shape=[jax.ShapeDtypeStruct((T, TOP_K), jnp.int32), jax.ShapeDtypeStruct((T, TOP_K), jnp.float32),
                   jax.ShapeDtypeStruct((T, TOP_K), jnp.int32), jax.ShapeDtypeStruct((1, N_EXPERTS), jnp.int32)],
        scratch_shapes=[pltpu.VMEM((1, N_EXPERTS), jnp.float32)],
        compiler_params=_cparams(("arbitrary",)),
        name="route",
    )(logits)


def _sc_gather_rows(table, idx):
    N, W = table.shape
    M = idx.shape[0]
    workers = V7X_SC_CORES * V7X_SC_SUBCORES
    per_worker = M // workers
    assert per_worker * workers == M and per_worker % SC_GATHER_ROWS == 0
    mesh = plsc.VectorSubcoreMesh(core_axis_name="core", subcore_axis_name="subcore",
                                  num_cores=V7X_SC_CORES, num_subcores=V7X_SC_SUBCORES)

    @functools.partial(
        pl.kernel, mesh=mesh,
        out_type=jax.ShapeDtypeStruct((M, W), table.dtype),
        scratch_types=[pltpu.VMEM((per_worker,), jnp.int32),
                       pltpu.VMEM((SC_GATHER_ROWS, W), table.dtype),
                       pltpu.SemaphoreType.DMA],
    )
    def gather(table_hbm, idx_hbm, out_hbm, idx_v, rows_v, sem):
        worker = lax.axis_index("subcore") * V7X_SC_CORES + lax.axis_index("core")
        base = pl.multiple_of(worker * per_worker, SC_GATHER_ROWS)
        pltpu.sync_copy(idx_hbm.at[pl.ds(base, per_worker)], idx_v)

        @pl.loop(0, per_worker // SC_GATHER_ROWS)
        def _(g):
            off = pl.multiple_of(g * SC_GATHER_ROWS, SC_GATHER_ROWS)
            pltpu.async_copy(table_hbm.at[idx_v.at[pl.ds(off, SC_GATHER_ROWS)]], rows_v, sem).wait()
            pltpu.sync_copy(rows_v, out_hbm.at[pl.ds(base + off, SC_GATHER_ROWS)])

    return gather(table, idx)


def _sc_scatter_rows(rows, dest, n_out):
    T, W = rows.shape
    slots = dest.shape[1]
    workers = V7X_SC_CORES * V7X_SC_SUBCORES
    per_worker = T // workers
    pieces = per_worker // SC_GATHER_ROWS
    assert per_worker * workers == T and pieces * SC_GATHER_ROWS == per_worker
    idx = dest.reshape(workers, pieces, SC_GATHER_ROWS, slots).transpose(0, 1, 3, 2)
    mesh = plsc.VectorSubcoreMesh(core_axis_name="core", subcore_axis_name="subcore",
                                  num_cores=V7X_SC_CORES, num_subcores=V7X_SC_SUBCORES)

    @functools.partial(
        pl.kernel, mesh=mesh,
        out_type=jax.ShapeDtypeStruct((n_out, W), rows.dtype),
        scratch_types=[pltpu.VMEM((pieces, slots, SC_GATHER_ROWS), jnp.int32),
                       pltpu.VMEM((SC_GATHER_ROWS, W), rows.dtype)],
    )
    def scatter(rows_hbm, idx_hbm, out_hbm, idx_v, rows_v):
        worker = lax.axis_index("subcore") * V7X_SC_CORES + lax.axis_index("core")
        base = pl.multiple_of(worker * per_worker, SC_GATHER_ROWS)
        pltpu.sync_copy(idx_hbm.at[worker], idx_v)

        @pl.loop(0, pieces)
        def _(g):
            off = pl.multiple_of(g * SC_GATHER_ROWS, SC_GATHER_ROWS)
            pltpu.sync_copy(rows_hbm.at[pl.ds(base + off, SC_GATHER_ROWS)], rows_v)
            for k in range(slots):
                pltpu.sync_copy(rows_v, out_hbm.at[idx_v.at[g, k]])

    return scatter(rows, idx)


def _experts_kernel(be_ref, nb_ref, x_ref, wgu_ref, bgu_ref, wd_ref, bd_ref, o_ref, wgu_bf, wd_bf):
    f32 = jnp.float32
    d_ff = wd_ref.shape[1]
    i = pl.program_id(0)
    live = i < nb_ref[0]

    @pl.when(live & ((i == 0) | (be_ref[i] != be_ref[jnp.maximum(i - 1, 0)])))
    def _():
        wgu_bf[...] = wgu_ref[0].astype(jnp.bfloat16)
        wd_bf[...] = wd_ref[0].astype(jnp.bfloat16)

    @pl.when(live)
    def _():
        x_hi, x_lo = _unpack_bf16_pairs(x_ref[...])
        half = x_hi.shape[1]
        gu = (jnp.dot(x_hi, wgu_bf[:half, :], preferred_element_type=f32)
              + jnp.dot(x_lo, wgu_bf[half:, :], preferred_element_type=f32) + bgu_ref[0])
        gate = jnp.minimum(gu[:, :d_ff], SWIGLU_LIMIT)
        lin = jnp.clip(gu[:, d_ff:], -SWIGLU_LIMIT, SWIGLU_LIMIT)
        act = (lin + 1.0) * gate * _sigmoid(SWIGLU_ALPHA * gate)
        y = jnp.dot(act.astype(jnp.bfloat16), wd_bf[...], preferred_element_type=f32) + bd_ref[0]
        o_ref[...] = _pack_bf16_pairs(y)

    @pl.when(pl.program_id(0) >= nb_ref[0])
    def _():
        o_ref[...] = jnp.zeros_like(o_ref)


def _experts(xs, block_expert, n_used, w_gu, b_gu, w_down, b_down):
    P, W = xs.shape
    E, D, F2 = w_gu.shape
    nb = P // EXPERT_ROWS
    grid_spec = pltpu.PrefetchScalarGridSpec(
        num_scalar_prefetch=2,
        grid=(nb,),
        in_specs=[
            pl.BlockSpec((EXPERT_ROWS, W), lambda i, be, nu: (i, 0)),
            pl.BlockSpec((1, D, F2), lambda i, be, nu: (be[i], 0, 0)),
            pl.BlockSpec((1, 1, F2), lambda i, be, nu: (be[i], 0, 0)),
            pl.BlockSpec((1, F2 // 2, D), lambda i, be, nu: (be[i], 0, 0)),
            pl.BlockSpec((1, 1, D), lambda i, be, nu: (be[i], 0, 0)),
        ],
        out_specs=pl.BlockSpec((EXPERT_ROWS, W), lambda i, be, nu: (i, 0)),
        scratch_shapes=[pltpu.VMEM((D, F2), jnp.bfloat16), pltpu.VMEM((F2 // 2, D), jnp.bfloat16)],
    )
    return pl.pallas_call(
        _experts_kernel,
        grid_spec=grid_spec,
        out_shape=jax.ShapeDtypeStruct((P, W), jnp.uint32),
        compiler_params=_cparams(("arbitrary",)),
        name="experts",
    )(block_expert, n_used, xs, w_gu, b_gu.reshape(E, 1, F2), w_down, b_down.reshape(E, 1, D))


def _combine_kernel(ya_ref, x1_ref, gate_ref, g_ref, o_ref):
    half = x1_ref.shape[1] // 2
    f32 = jnp.float32
    gate = gate_ref[...]
    x1 = x1_ref[...]
    y_hi = x1[:, :half]
    y_lo = x1[:, half:]
    for k in range(TOP_K):
        hi, lo = _unpack_bf16_pairs(ya_ref[k])
        y_hi = y_hi + gate[:, k:k + 1] * hi.astype(f32)
        y_lo = y_lo + gate[:, k:k + 1] * lo.astype(f32)
    o_ref[...] = _rms(jnp.concatenate([y_hi, y_lo], axis=1), g_ref[...])


def _combine(ya, x1, gates, gain):
    T, D = x1.shape
    R = min(COMBINE_ROWS, T)
    row = lambda w: pl.BlockSpec((R, w), lambda i: (i, 0))
    return pl.pallas_call(
        _combine_kernel,
        grid=(T // R,),
        in_specs=[pl.BlockSpec((TOP_K, R, D // 2), lambda i: (0, i, 0)), row(D), row(TOP_K),
                  pl.BlockSpec((1, D), lambda i: (0, 0))],
        out_specs=row(D),
        out_shape=jax.ShapeDtypeStruct((T, D), jnp.float32),
        compiler_params=_cparams(("parallel",)),
        name="combine",
    )(ya, x1, gates, gain.reshape(1, D))


def _moe_plan(eidx, rank, counts, A):
    counts = counts.reshape(N_EXPERTS)
    padded = (counts + EXPERT_ROWS - 1) // EXPERT_ROWS * EXPERT_ROWS
    pad_ends = jnp.cumsum(padded)
    pad_starts = pad_ends - padded
    n_blocks = -(-A // EXPERT_ROWS) + N_EXPERTS
    dest = pad_starts[eidx] + rank
    block_start = jnp.arange(n_blocks, dtype=pad_ends.dtype) * EXPERT_ROWS
    block_expert = jnp.minimum(jnp.sum(pad_ends[None, :] <= block_start[:, None], axis=1),
                               N_EXPERTS - 1).astype(jnp.int32)
    n_used = (pad_ends[-1] // EXPERT_ROWS).astype(jnp.int32).reshape(1)
    return dest.astype(jnp.int32), block_expert, n_used, n_blocks


def kernel(x, w_in, w_up_a, w_up_b, w_out, norm_mix, norm_ffn, norm_final, hgrn_norm,
           lb_logits, rel_bias, w_router, b_router, w_gu, b_gu, w_down, b_down):
    B, S, D = x.shape
    T = B * S
    assert w_in.shape[0] == 1, "the final rmsnorm is fused into the single layer's combine stage"
    lb_all = jnp.cumsum(jax.nn.softmax(lb_logits.astype(jnp.float32), axis=0), axis=0)
    x2 = x.reshape(T, D)
    (k, ik, qT, vT, iqT, iwT, bq, bf, bi, bg, ga, gb) = _inproj(x2, norm_mix[0], w_in[0], B, S)
    ya = _dsa(k, ik, qT, vT, iqT, iwT, rel_bias, B, S)
    yb = _hgrn(bq, bf, bi, bg, lb_all[0].reshape(B_HEADS, B_KEY_DIM), hgrn_norm[0], B, S)
    x1, xn, logits = _merge(x2, ya, yb, ga, gb, w_up_a[0], w_up_b[0], w_out[0], norm_ffn[0],
                            w_router[0], b_router[0])
    eidx, gates, rank, counts = _route(logits)
    dest, block_expert, n_used, n_blocks = _moe_plan(eidx, rank, counts, T * TOP_K)
    P = n_blocks * EXPERT_ROWS
    A = T * TOP_K
    xs = _sc_scatter_rows(xn, dest, P)
    y_buf = _experts(xs, block_expert, n_used, w_gu[0], b_gu[0], w_down[0], b_down[0])
    ya = _sc_gather_rows(y_buf, dest.T.reshape(A)).reshape(TOP_K, T, D // 2)
    out = _combine(ya, x1, gates, norm_final)
    return out.reshape(B, S, D)
```

```python
import functools
import math

import numpy as np
import jax
import jax.numpy as jnp
from jax import lax
from jax.experimental import pallas as pl
from jax.experimental.pallas import tpu as pltpu
from jax.experimental.pallas import tpu_sc as plsc

A_HEADS = 8
A_HEAD_DIM = 64
IDX_HEADS = 8
IDX_DIM = 32
TOPK_MAX = 256
REL_BUCKETS = 32
REL_MAX_DIST = 128
B_HEADS = 4
B_KEY_DIM = 128
B_VAL_DIM = 128
N_EXPERTS = 32
TOP_K = 4
SWIGLU_LIMIT = 7.0
SWIGLU_ALPHA = 1.702
EPS = 1e-6
LOG2_E = math.log2(math.e)

A_WIDTH = A_HEADS * A_HEAD_DIM
B_WIDTH = B_HEADS * B_VAL_DIM
IDX_WIDTH = IDX_HEADS * IDX_DIM

V7X_LANES = 128
V7X_SUBLANES = 8
V7X_VMEM_LIMIT_BYTES = 56 * 1024 * 1024
V7X_SC_CORES = 2
V7X_SC_SUBCORES = 16

PROJ_ROWS = 512
ATT_Q = 256
ATT_KC = 128
HGRN_ROWS = 512
HGRN_CHUNK = 64
HGRN_SAFE_DECAY = 70.0
ROUTE_ROWS = 512
EXPERT_ROWS = 256
COMBINE_ROWS = 512
SC_GATHER_ROWS = 64
MASK_NEG = -1e30
BISECT_FAST_ITERS = 26


def _cparams(dims):
    return pltpu.CompilerParams(dimension_semantics=dims, vmem_limit_bytes=V7X_VMEM_LIMIT_BYTES)


def _rms(x, gain):
    return x * lax.rsqrt(jnp.mean(x * x, axis=-1, keepdims=True) + EPS) * gain


def _sigmoid(x):
    return 1.0 / (1.0 + jnp.exp(-x))


def _pack_bf16_pairs(x):
    n = x.shape[1] // 2
    as_bits = lambda v: lax.bitcast_convert_type(v.astype(jnp.bfloat16).astype(jnp.float32), jnp.uint32)
    return (as_bits(x[:, :n]) & jnp.uint32(0xFFFF0000)) | (as_bits(x[:, n:]) >> 16)


def _unpack_bf16_pairs(w):
    hi = lax.bitcast_convert_type(w & jnp.uint32(0xFFFF0000), jnp.float32).astype(jnp.bfloat16)
    lo = lax.bitcast_convert_type(w << 16, jnp.float32).astype(jnp.bfloat16)
    return hi, lo


def _fold_rows(x, op):
    return op(x.reshape(x.shape[0] // V7X_SUBLANES, V7X_SUBLANES, x.shape[1]), axis=0)


def _inproj_kernel(x_ref, g_ref, wk_ref, wik_ref, wqT_ref, wvT_ref, wiqT_ref, wiwT_ref, wb_ref, wg_ref,
                   k_ref, ik_ref, qT_ref, vT_ref, iqT_ref, iwT_ref, bq_ref, bf_ref, bi_ref, bg_ref,
                   ga_ref, gb_ref):
    x = x_ref[...]
    hn = _rms(x, g_ref[...]).astype(jnp.bfloat16)

    def mm(w_ref):
        return jnp.dot(hn, w_ref[...], preferred_element_type=jnp.float32)

    def mm_t(w_ref):
        return lax.dot_general(w_ref[...], hn, (((1,), (1,)), ((), ())),
                               preferred_element_type=jnp.float32)

    k_ref[...] = mm(wk_ref).astype(jnp.bfloat16)
    ik_ref[...] = mm(wik_ref).astype(jnp.bfloat16)
    qT_ref[0] = (mm_t(wqT_ref) * (A_HEAD_DIM ** -0.5 * LOG2_E)).astype(jnp.bfloat16)
    vT_ref[0] = mm_t(wvT_ref).astype(jnp.bfloat16)
    iqT_ref[0] = mm_t(wiqT_ref).astype(jnp.bfloat16)
    iwT_ref[0] = mm_t(wiwT_ref) * ((IDX_HEADS * IDX_DIM) ** -0.5)
    hb = mm(wb_ref)
    bq_ref[...] = hb[:, 0 * B_WIDTH:1 * B_WIDTH]
    bf_ref[...] = hb[:, 1 * B_WIDTH:2 * B_WIDTH]
    bi_ref[...] = hb[:, 2 * B_WIDTH:3 * B_WIDTH]
    bg_ref[...] = hb[:, 3 * B_WIDTH:4 * B_WIDTH]
    d = ga_ref.shape[-1]
    hg = mm(wg_ref)
    ga_ref[...] = _sigmoid(hg[:, :d]).astype(jnp.bfloat16)
    gb_ref[...] = _sigmoid(hg[:, d:]).astype(jnp.bfloat16)


def _inproj(x2, gain, w_in, B, S):
    T, D = x2.shape
    R = min(PROJ_ROWS, S)
    nS = S // R
    o = np.cumsum((0, A_WIDTH, A_WIDTH, A_WIDTH, IDX_WIDTH, IDX_HEADS, IDX_DIM,
                   B_WIDTH, B_WIDTH, B_WIDTH, B_WIDTH, D, D))
    bf = jnp.bfloat16
    wqT = w_in[:, o[0]:o[1]].T.astype(bf)
    wk = w_in[:, o[1]:o[2]].astype(bf)
    wvT = w_in[:, o[2]:o[3]].T.astype(bf)
    wiqT = w_in[:, o[3]:o[4]].T.astype(bf)
    wiwT = w_in[:, o[4]:o[5]].T.astype(bf)
    wik = w_in[:, o[5]:o[6]].astype(bf)
    wb = w_in[:, o[6]:o[10]].astype(bf)
    wg = w_in[:, o[10]:o[12]].astype(bf)

    def full(a):
        return pl.BlockSpec(a.shape, lambda b, i: (0,) * a.ndim)

    row = lambda n: pl.BlockSpec((R, n), lambda b, i: (b * nS + i, 0))
    colT = lambda n: pl.BlockSpec((1, n, R), lambda b, i: (b, 0, i))
    f32 = jnp.float32
    outs = [
        (jax.ShapeDtypeStruct((T, A_WIDTH), bf), row(A_WIDTH)),
        (jax.ShapeDtypeStruct((T, IDX_DIM), bf), row(IDX_DIM)),
        (jax.ShapeDtypeStruct((B, A_WIDTH, S), bf), colT(A_WIDTH)),
        (jax.ShapeDtypeStruct((B, A_WIDTH, S), bf), colT(A_WIDTH)),
        (jax.ShapeDtypeStruct((B, IDX_WIDTH, S), bf), colT(IDX_WIDTH)),
        (jax.ShapeDtypeStruct((B, IDX_HEADS, S), f32), colT(IDX_HEADS)),
        (jax.ShapeDtypeStruct((T, B_WIDTH), f32), row(B_WIDTH)),
        (jax.ShapeDtypeStruct((T, B_WIDTH), f32), row(B_WIDTH)),
        (jax.ShapeDtypeStruct((T, B_WIDTH), f32), row(B_WIDTH)),
        (jax.ShapeDtypeStruct((T, B_WIDTH), f32), row(B_WIDTH)),
        (jax.ShapeDtypeStruct((T, D), bf), row(D)),
        (jax.ShapeDtypeStruct((T, D), bf), row(D)),
    ]
    ins = [x2, gain.reshape(1, D), wk, wik, wqT, wvT, wiqT, wiwT, wb, wg]
    in_specs = [row(D)] + [full(a) for a in ins[1:]]
    return pl.pallas_call(
        _inproj_kernel,
        grid=(B, nS),
        in_specs=in_specs,
        out_specs=[s for _, s in outs],
        out_shape=[s for s, _ in outs],
        compiler_params=_cparams(("parallel", "parallel")),
        name="inproj",
    )(*ins)


def _t5_bucket_table(n):
    d = np.arange(n)
    max_exact = REL_BUCKETS // 2
    nf = np.maximum(d, 1).astype(np.float64)
    large = max_exact + (np.log(nf / max_exact) / math.log(REL_MAX_DIST / max_exact)
                         * (REL_BUCKETS - max_exact)).astype(np.int32)
    large = np.minimum(large, REL_BUCKETS - 1)
    return np.where(d < max_exact, d, large)


def _dsa_kernel(qT_ref, k_ref, vT_ref, iqT_ref, iwT_ref, ik_ref, enear_ref,
                o_ref, sc_ref, qh_scr, m_scr, acc_scr, *, topk):
    TQ = qT_ref.shape[2]
    KC = TQ
    i = pl.program_id(1)
    nch = i + 1
    q0 = i * TQ
    f32 = jnp.float32
    bf16 = jnp.bfloat16
    key_id = lax.broadcasted_iota(jnp.int32, (KC, TQ), 0)
    qry_id = lax.broadcasted_iota(jnp.int32, (KC, TQ), 1)

    def col_reduce(x, op):
        return op(_fold_rows(x, op), axis=0, keepdims=True)

    iw = iwT_ref[0]

    def score_chunk(c, carry):
        rmin, rmax = carry
        k0 = pl.multiple_of(c * KC, KC)
        ik = ik_ref[pl.ds(k0, KC), :]
        acc = jnp.zeros((KC, TQ), f32)
        for h in range(IDX_HEADS):
            sh = jnp.dot(ik, iqT_ref[0, h * IDX_DIM:(h + 1) * IDX_DIM, :], preferred_element_type=f32)
            acc = acc + jnp.maximum(sh, 0.0) * iw[h:h + 1, :]
        valid = (k0 + key_id) <= (q0 + qry_id)
        sc_ref[pl.ds(k0, KC), :] = jnp.where(valid, acc, MASK_NEG)
        rmin = jnp.minimum(rmin, _fold_rows(jnp.where(valid, acc, -MASK_NEG), jnp.min))
        rmax = jnp.maximum(rmax, _fold_rows(jnp.where(valid, acc, MASK_NEG), jnp.max))
        return rmin, rmax

    rmin8, rmax8 = lax.fori_loop(
        0, nch, score_chunk,
        (jnp.full((V7X_SUBLANES, TQ), -MASK_NEG, f32), jnp.full((V7X_SUBLANES, TQ), MASK_NEG, f32)))
    rmin = jnp.min(rmin8, axis=0, keepdims=True)
    rmax = jnp.max(rmax8, axis=0, keepdims=True)

    def count_where(pred_fn):
        def body(c, acc):
            k0 = pl.multiple_of(c * KC, KC)
            blk = sc_ref[pl.ds(k0, KC), :]
            return acc + _fold_rows(jnp.where(pred_fn(blk), 1.0, 0.0), jnp.sum)
        acc = lax.fori_loop(0, nch, body, jnp.zeros((V7X_SUBLANES, TQ), f32))
        return jnp.sum(acc, axis=0, keepdims=True)

    def band_min_max(lo, hi):
        def body(c, carry):
            bmin, bmax = carry
            k0 = pl.multiple_of(c * KC, KC)
            blk = sc_ref[pl.ds(k0, KC), :]
            bmin = jnp.minimum(bmin, _fold_rows(jnp.where(blk >= lo, blk, -MASK_NEG), jnp.min))
            bmax = jnp.maximum(bmax, _fold_rows(jnp.where(blk < hi, blk, MASK_NEG), jnp.max))
            return bmin, bmax
        bmin8, bmax8 = lax.fori_loop(
            0, nch, body,
            (jnp.full((V7X_SUBLANES, TQ), -MASK_NEG, f32), jnp.full((V7X_SUBLANES, TQ), MASK_NEG, f32)))
        return jnp.min(bmin8, axis=0, keepdims=True), jnp.max(bmax8, axis=0, keepdims=True)

    kf = float(topk)
    n_valid = (q0 + 1 + lax.broadcasted_iota(jnp.int32, (1, TQ), 1)).astype(f32)
    c_top = count_where(lambda blk: blk >= rmax)
    top_tie = c_top >= kf
    lo0 = jnp.where(top_tie, rmax, rmin)
    cnt0 = jnp.where(top_tie, c_top, n_valid)
    done0 = jnp.where(top_tie | (cnt0 <= kf), 1.0, 0.0)

    def cond(st):
        return jnp.min(st[-1]) < 0.5

    def body(st):
        it, lo, hi, cnt, done = st

        def plain(_):
            half = lo + 0.5 * (hi - lo)
            stuck = (half <= lo) | (half >= hi)
            return lo, half, jnp.where(stuck, 1.0, 0.0)

        def snapped(_):
            bmin, bmax = band_min_max(lo, hi)
            mid = bmin + 0.5 * (bmax - bmin)
            mid = jnp.where(mid <= bmin, bmax, mid)
            return bmin, mid, jnp.where(bmax <= bmin, 1.0, 0.0)

        lo_s, mid, tie_f = lax.cond(it < BISECT_FAST_ITERS, plain, snapped, None)
        tie = tie_f > 0.5
        active = done < 0.5
        lo_s = jnp.where(active, lo_s, lo)
        c = count_where(lambda blk: blk >= mid)
        feas = c >= kf
        move = active & jnp.logical_not(tie)
        lo_n = jnp.where(move & feas, mid, lo_s)
        cnt_n = jnp.where(move & feas, c, cnt)
        hi_n = jnp.where(move & jnp.logical_not(feas), mid, hi)
        done_n = jnp.where((active & tie) | (cnt_n <= kf), 1.0, done)
        return it + 1, lo_n, hi_n, cnt_n, done_n

    thr = lax.while_loop(cond, body, (jnp.int32(0), lo0, rmax, cnt0, done0))[1]

    need = kf - count_where(lambda blk: blk > thr)
    tril = jnp.where(lax.broadcasted_iota(jnp.int32, (KC, KC), 1)
                     <= lax.broadcasted_iota(jnp.int32, (KC, KC), 0), 1.0, 0.0).astype(bf16)

    def mask_chunk(c, run):
        k0 = pl.multiple_of(c * KC, KC)
        blk = sc_ref[pl.ds(k0, KC), :]
        eq = jnp.where(blk == thr, 1.0, 0.0)
        pref = jnp.dot(tril, eq.astype(bf16), preferred_element_type=f32)
        sel = (blk > thr) | ((eq > 0.5) & (run + pref <= need))
        sc_ref[pl.ds(k0, KC), :] = jnp.where(sel, 0.0, MASK_NEG)
        return run + pref[KC - 1:KC, :]

    lax.fori_loop(0, nch, mask_chunk, jnp.zeros((1, TQ), f32))

    AK = min(ATT_KC, TQ)
    per = TQ // AK
    head0_q = (lax.broadcasted_iota(jnp.int32, (V7X_LANES, TQ), 0) // A_HEAD_DIM) == 0
    n_pairs = A_HEADS // 2

    m_scr[...] = jnp.full(m_scr.shape, MASK_NEG, f32)
    acc_scr[...] = jnp.zeros(acc_scr.shape, f32)
    v_row = lax.broadcasted_iota(jnp.int32, (V7X_LANES, AK), 0)
    denom_row = [A_HEAD_DIM * (1 - sub) for sub in range(2)]
    for p in range(n_pairs):
        q_pair = qT_ref[0, p * V7X_LANES:(p + 1) * V7X_LANES, :]
        zq = jnp.zeros_like(q_pair)
        qh_scr[2 * p] = jnp.where(head0_q, q_pair, zq)
        qh_scr[2 * p + 1] = jnp.where(head0_q, zq, q_pair)

    def step(c, bias_rows):
        k0 = pl.multiple_of(c * AK, AK)
        msk = sc_ref[pl.ds(k0, AK), :]
        for p in range(n_pairs):
            kp = k_ref[pl.ds(k0, AK), p * V7X_LANES:(p + 1) * V7X_LANES]
            vp = vT_ref[0, p * V7X_LANES:(p + 1) * V7X_LANES, pl.ds(k0, AK)]
            for sub in range(2):
                h = 2 * p + sub
                s = jnp.dot(kp, qh_scr[h], preferred_element_type=f32) + msk
                if bias_rows is not None:
                    s = s + enear_ref[h, bias_rows, :]
                m = m_scr[h:h + 1, :]
                m_new = jnp.maximum(m, col_reduce(s, jnp.max))
                alpha = jnp.exp2(m - m_new)
                pr = jnp.exp2(s - m_new)
                m_scr[h:h + 1, :] = m_new
                v_aug = jnp.where(v_row == denom_row[sub], jnp.ones_like(vp), vp)
                acc_scr[h] = alpha * acc_scr[h] + jnp.dot(v_aug, pr.astype(bf16), preferred_element_type=f32)

    def far(blk, _):
        for jj in range(per):
            step(blk * per + jj, None)
        return 0

    def near(block, first_chunk):
        for jj in range(per):
            step(first_chunk + jj, slice(block * TQ + jj * AK, block * TQ + (jj + 1) * AK))

    lax.fori_loop(0, jnp.maximum(i - 1, 0), far, 0)

    @pl.when(i >= 1)
    def _():
        near(0, (i - 1) * per)

    near(1, i * per)
    for p in range(n_pairs):
        outs = [acc_scr[2 * p + sub] / acc_scr[2 * p + sub, denom_row[sub]:denom_row[sub] + 1, :]
                for sub in range(2)]
        o_pair = jnp.where(head0_q, outs[0], outs[1])
        o_ref[:, p * V7X_LANES:(p + 1) * V7X_LANES] = o_pair.T.astype(o_ref.dtype)


def _dsa(k, ik, qT, vT, iqT, iwT, rel_bias, B, S):
    T = k.shape[0]
    TQ = min(ATT_Q, S)
    nQ = S // TQ
    topk = min(TOPK_MAX, S // 4)
    buckets = _t5_bucket_table(2 * TQ + 1)
    assert np.all(_t5_bucket_table(S + 1)[TQ + 1:] == REL_BUCKETS - 1)
    j = np.arange(2 * TQ)[:, None]
    r = np.arange(TQ)[None, :]
    dist = np.maximum(r + TQ - j, 0)
    onehot = (jnp.asarray(buckets[dist], jnp.int32)[None]
              == jnp.arange(REL_BUCKETS, dtype=jnp.int32)[:, None, None]).astype(jnp.float32)
    rel = (rel_bias.astype(jnp.float32) - rel_bias[REL_BUCKETS - 1].astype(jnp.float32)[None, :]) * LOG2_E
    enear = jnp.einsum('nh,njr->hjr', rel, onehot, precision=lax.Precision.HIGHEST)

    return pl.pallas_call(
        functools.partial(_dsa_kernel, topk=topk),
        grid=(B, nQ),
        in_specs=[
            pl.BlockSpec((1, A_WIDTH, TQ), lambda b, i: (b, 0, i)),
            pl.BlockSpec((S, A_WIDTH), lambda b, i: (b, 0)),
            pl.BlockSpec((1, A_WIDTH, S), lambda b, i: (b, 0, 0)),
            pl.BlockSpec((1, IDX_WIDTH, TQ), lambda b, i: (b, 0, i)),
            pl.BlockSpec((1, IDX_HEADS, TQ), lambda b, i: (b, 0, i)),
            pl.BlockSpec((S, IDX_DIM), lambda b, i: (b, 0)),
            pl.BlockSpec((A_HEADS, 2 * TQ, TQ), lambda b, i: (0, 0, 0)),
        ],
        out_specs=pl.BlockSpec((TQ, A_WIDTH), lambda b, i: (b * nQ + i, 0)),
        scratch_shapes=[pltpu.VMEM((S, TQ), jnp.float32),
                        pltpu.VMEM((A_HEADS, V7X_LANES, TQ), jnp.bfloat16),
                        pltpu.VMEM((A_HEADS, TQ), jnp.float32),
                        pltpu.VMEM((A_HEADS, V7X_LANES, TQ), jnp.float32)],
        out_shape=jax.ShapeDtypeStruct((T, A_WIDTH), jnp.bfloat16),
        compiler_params=_cparams(("parallel", "arbitrary")),
        name="dsa",
    )(qT, k, vT, iqT, iwT, ik, enear)


def _hgrn_kernel(bq_ref, bf_ref, bi_ref, bg_ref, lb_ref, gain_ref, o_ref,
                 st_ref, b_scr, q_scr, k_scr, v_scr, oi_scr, qd_s, kd_s, kl_s, vv_s, dec_s, oi_s, upd_s):
    R = bq_ref.shape[0]
    C = HGRN_CHUNK
    nC = R // C
    f32 = jnp.float32
    bf16 = jnp.bfloat16
    h = pl.program_id(1)

    @pl.when(pl.program_id(2) == 0)
    def _():
        st_ref[...] = jnp.zeros_like(st_ref)

    lb = lb_ref[pl.ds(h, 1), :]
    gain = gain_ref[pl.ds(h, 1), :]
    tril_incl = jnp.where(lax.broadcasted_iota(jnp.int32, (C, C), 1)
                          <= lax.broadcasted_iota(jnp.int32, (C, C), 0), 1.0, 0.0)
    srow = lax.broadcasted_iota(jnp.int32, (C, B_KEY_DIM), 0)

    def gates(r0):
        f = lb + (1.0 - lb) * _sigmoid(bf_ref[pl.ds(r0, C), :])
        qr = bq_ref[pl.ds(r0, C), :]
        return jnp.log(f), 1.0 - f, qr * _sigmoid(qr) * (B_KEY_DIM ** -0.5), bi_ref[pl.ds(r0, C), :]

    def cumdecay(g):
        return jnp.dot(tril_incl, g, preferred_element_type=f32, precision=lax.Precision.HIGHEST)

    def advance(r0, st, qd, o_intra, upd, decay_row):
        o_inter = lax.dot_general(qd, st.astype(bf16), (((1,), (1,)), ((), ())), preferred_element_type=f32)
        og = bg_ref[pl.ds(r0, C), :]
        y = _rms(o_inter + o_intra, gain) * (og * _sigmoid(og))
        o_ref[pl.ds(r0, C), :] = y.astype(o_ref.dtype)
        return st * decay_row + upd

    f_all = lb + (1.0 - lb) * _sigmoid(bf_ref[...])
    g_all = jnp.log(f_all)
    decay = jnp.sum(g_all.reshape(nC, C, B_KEY_DIM), axis=1)
    safe = jnp.min(decay) >= -HGRN_SAFE_DECAY

    @pl.when(safe)
    def _():
        qr = bq_ref[...]
        qq = qr * _sigmoid(qr) * (B_KEY_DIM ** -0.5)
        kk = 1.0 - f_all
        b = jnp.concatenate([cumdecay(g_all[c * C:(c + 1) * C]) for c in range(nC)], axis=0)
        b_end = jnp.concatenate([jnp.broadcast_to(b[(c + 1) * C - 1:(c + 1) * C], (C, B_KEY_DIM))
                                 for c in range(nC)], axis=0)
        qd_s[...] = (qq * jnp.exp(b)).astype(bf16)
        kd_s[...] = (kk * jnp.exp(-b)).astype(bf16)
        kl_s[...] = (kk * jnp.exp(b_end - b)).astype(bf16)
        vv_s[...] = bi_ref[...].astype(bf16)
        dec_s[...] = jnp.exp(b_end)
        for c in range(nC):
            rows = slice(c * C, (c + 1) * C)
            att = lax.dot_general(qd_s[rows], kd_s[rows], (((1,), (1,)), ((), ())),
                                  preferred_element_type=f32) * tril_incl
            oi_s[rows] = jnp.dot(att.astype(bf16), vv_s[rows], preferred_element_type=f32)
            upd_s[c] = lax.dot_general(vv_s[rows], kl_s[rows], (((0,), (0,)), ((), ())),
                                       preferred_element_type=f32)
        st = st_ref[...]
        for c in range(nC):
            rows = slice(c * C, (c + 1) * C)
            st = advance(c * C, st, qd_s[rows], oi_s[rows], upd_s[c], dec_s[c * C:c * C + 1])
        st_ref[...] = st

    @pl.when(jnp.logical_not(safe))
    def _():
        def body(c, st):
            r0 = pl.multiple_of(c * C, C)
            g, kk, qq, vv = gates(r0)
            b = cumdecay(g)
            b_last = b[C - 1:C, :]
            b_scr[...] = b
            q_scr[...] = qq
            k_scr[...] = kk
            v_scr[...] = vv

            def row(t, _):
                bt = b_scr[pl.ds(t, 1), :]
                qt = q_scr[pl.ds(t, 1), :]
                ex = jnp.where(srow <= t, bt - b_scr[...], -jnp.inf)
                a = jnp.sum(qt * k_scr[...] * jnp.exp(ex), axis=1, keepdims=True)
                oi_scr[pl.ds(t, 1), :] = jnp.sum(a * v_scr[...], axis=0, keepdims=True)
                return 0
            lax.fori_loop(0, C, row, 0)
            kd_last = (kk * jnp.exp(b_last - b)).astype(bf16)
            upd = lax.dot_general(vv.astype(bf16), kd_last, (((0,), (0,)), ((), ())),
                                  preferred_element_type=f32)
            return advance(r0, st, (qq * jnp.exp(b)).astype(bf16), oi_scr[...], upd, jnp.exp(b_last))
        st_ref[...] = lax.fori_loop(0, nC, body, st_ref[...])


def _hgrn(bq, bf, bi, bg, lb, gain, B, S):
    T = bq.shape[0]
    R = min(HGRN_ROWS, S)
    nR = S // R
    C = HGRN_CHUNK
    blk = pl.BlockSpec((R, B_KEY_DIM), lambda b, h, c: (b * nR + c, h))
    small = pl.BlockSpec((B_HEADS, B_KEY_DIM), lambda b, h, c: (0, 0))
    f32 = jnp.float32
    return pl.pallas_call(
        _hgrn_kernel,
        grid=(B, B_HEADS, nR),
        in_specs=[blk, blk, blk, blk, small, small],
        out_specs=blk,
        out_shape=jax.ShapeDtypeStruct((T, B_WIDTH), jnp.bfloat16),
        scratch_shapes=[pltpu.VMEM((B_VAL_DIM, B_KEY_DIM), f32)] +
                       [pltpu.VMEM((C, B_KEY_DIM), f32) for _ in range(5)] +
                       [pltpu.VMEM((R, B_KEY_DIM), jnp.bfloat16) for _ in range(4)] +
                       [pltpu.VMEM((R, B_KEY_DIM), f32) for _ in range(2)] +
                       [pltpu.VMEM((R // C, B_VAL_DIM, B_KEY_DIM), f32)],
        compiler_params=_cparams(("parallel", "parallel", "arbitrary")),
        name="hgrn",
    )(bq, bf, bi, bg, lb, gain)


def _merge_kernel(x_ref, ya_ref, yb_ref, ga_ref, gb_ref, wa_ref, wb_ref, wo_ref, g_ref, wr_ref, br_ref,
                  x1_ref, xn_ref, lg_ref):
    f32 = jnp.float32
    ma = jnp.dot(ya_ref[...], wa_ref[...], preferred_element_type=f32)
    mb = jnp.dot(yb_ref[...], wb_ref[...], preferred_element_type=f32)
    merged = ga_ref[...].astype(f32) * ma + gb_ref[...].astype(f32) * mb
    x1 = x_ref[...] + jnp.dot(merged.astype(jnp.bfloat16), wo_ref[...], preferred_element_type=f32)
    x1_ref[...] = x1
    hn = _rms(x1, g_ref[...])
    xn_ref[...] = _pack_bf16_pairs(hn)
    lg_ref[...] = jnp.dot(hn, wr_ref[...], preferred_element_type=f32,
                          precision=lax.Precision.HIGHEST) + br_ref[...]


def _merge(x2, ya, yb, ga, gb, w_up_a, w_up_b, w_out, gain, w_router, b_router):
    T, D = x2.shape
    R = min(PROJ_ROWS, T)
    bf = jnp.bfloat16
    ins = [x2, ya, yb, ga, gb, w_up_a.astype(bf), w_up_b.astype(bf), w_out.astype(bf),
           gain.reshape(1, D), w_router, b_router.reshape(1, N_EXPERTS)]
    row = lambda n: pl.BlockSpec((R, n), lambda i: (i, 0))
    full = lambda a: pl.BlockSpec(a.shape, lambda i: (0,) * a.ndim)
    in_specs = [row(D), row(A_WIDTH), row(B_WIDTH), row(D), row(D)] + [full(a) for a in ins[5:]]
    return pl.pallas_call(
        _merge_kernel,
        grid=(T // R,),
        in_specs=in_specs,
        out_specs=[row(D), row(D // 2), row(N_EXPERTS)],
        out_shape=[jax.ShapeDtypeStruct((T, D), jnp.float32), jax.ShapeDtypeStruct((T, D // 2), jnp.uint32),
                   jax.ShapeDtypeStruct((T, N_EXPERTS), jnp.float32)],
        compiler_params=_cparams(("parallel",)),
        name="merge",
    )(*ins)


def _route_kernel(lg_ref, eidx_ref, gate_ref, rank_ref, cnt_ref, run_ref):
    R = lg_ref.shape[0]
    f32 = jnp.float32

    @pl.when(pl.program_id(0) == 0)
    def _():
        run_ref[...] = jnp.zeros_like(run_ref)

    lg = lg_ref[...]
    lane = lax.broadcasted_iota(jnp.int32, (R, N_EXPERTS), 1)
    work = lg
    onehots, vals, idxs = [], [], []
    for _ in range(TOP_K):
        m = jnp.max(work, axis=1, keepdims=True)
        idx = jnp.min(jnp.where(work == m, lane, N_EXPERTS), axis=1, keepdims=True)
        oh = lane == idx
        onehots.append(oh)
        vals.append(m)
        idxs.append(idx)
        work = jnp.where(oh, -jnp.inf, work)
    ex = [jnp.exp(v - vals[0]) for v in vals]
    den = ex[0] + ex[1] + ex[2] + ex[3]
    chosen = jnp.where(onehots[0] | onehots[1] | onehots[2] | onehots[3], 1.0, 0.0)
    strict = jnp.where(lax.broadcasted_iota(jnp.int32, (R, R), 1)
                       < lax.broadcasted_iota(jnp.int32, (R, R), 0), 1.0, 0.0).astype(jnp.bfloat16)
    before = jnp.dot(strict, chosen.astype(jnp.bfloat16), preferred_element_type=f32) + run_ref[...]
    lane4 = lax.broadcasted_iota(jnp.int32, (R, TOP_K), 1)
    eidx = jnp.zeros((R, TOP_K), jnp.int32)
    gate = jnp.zeros((R, TOP_K), f32)
    rank = jnp.zeros((R, TOP_K), f32)
    for k in range(TOP_K):
        eidx = jnp.where(lane4 == k, idxs[k], eidx)
        gate = jnp.where(lane4 == k, ex[k] / den, gate)
        rk = jnp.sum(jnp.where(onehots[k], before, 0.0), axis=1, keepdims=True)
        rank = jnp.where(lane4 == k, rk, rank)
    eidx_ref[...] = eidx
    gate_ref[...] = gate
    rank_ref[...] = rank.astype(jnp.int32)
    run_ref[...] = run_ref[...] + jnp.sum(chosen, axis=0, keepdims=True)
    cnt_ref[...] = run_ref[...].astype(jnp.int32)


def _route(logits):
    T = logits.shape[0]
    R = min(ROUTE_ROWS, T)
    row = lambda n: pl.BlockSpec((R, n), lambda i: (i, 0))
    return pl.pallas_call(
        _route_kernel,
        grid=(T // R,),
        in_specs=[row(N_EXPERTS)],
        out_specs=[row(TOP_K), row(TOP_K), row(TOP_K), pl.BlockSpec((1, N_EXPERTS), lambda i: (0, 0))],
        out_shape=[jax.ShapeDtypeStruct((T, TOP_K), jnp.int32), jax.ShapeDtypeStruct((T, TOP_K), jnp.float32),
                   jax.ShapeDtypeStruct((T, TOP_K), jnp.int32), jax.ShapeDtypeStruct((1, N_EXPERTS), jnp.int32)],
        scratch_shapes=[pltpu.VMEM((1, N_EXPERTS), jnp.float32)],
        compiler_params=_cparams(("arbitrary",)),
        name="route",
    )(logits)


def _sc_gather_rows(table, idx):
    N, W = table.shape
    M = idx.shape[0]
    workers = V7X_SC_CORES * V7X_SC_SUBCORES
    per_worker = M // workers
    assert per_worker * workers == M and per_worker % SC_GATHER_ROWS == 0
    mesh = plsc.VectorSubcoreMesh(core_axis_name="core", subcore_axis_name="subcore",
                                  num_cores=V7X_SC_CORES, num_subcores=V7X_SC_SUBCORES)

    @functools.partial(
        pl.kernel, mesh=mesh,
        out_type=jax.ShapeDtypeStruct((M, W), table.dtype),
        scratch_types=[pltpu.VMEM((per_worker,), jnp.int32),
                       pltpu.VMEM((SC_GATHER_ROWS, W), table.dtype),
                       pltpu.SemaphoreType.DMA],
    )
    def gather(table_hbm, idx_hbm, out_hbm, idx_v, rows_v, sem):
        worker = lax.axis_index("subcore") * V7X_SC_CORES + lax.axis_index("core")
        base = pl.multiple_of(worker * per_worker, SC_GATHER_ROWS)
        pltpu.sync_copy(idx_hbm.at[pl.ds(base, per_worker)], idx_v)

        @pl.loop(0, per_worker // SC_GATHER_ROWS)
        def _(g):
            off = pl.multiple_of(g * SC_GATHER_ROWS, SC_GATHER_ROWS)
            pltpu.async_copy(table_hbm.at[idx_v.at[pl.ds(off, SC_GATHER_ROWS)]], rows_v, sem).wait()
            pltpu.sync_copy(rows_v, out_hbm.at[pl.ds(base + off, SC_GATHER_ROWS)])

    return gather(table, idx)


def _sc_scatter_rows(rows, dest, n_out):
    T, W = rows.shape
    slots = dest.shape[1]
    workers = V7X_SC_CORES * V7X_SC_SUBCORES
    per_worker = T // workers
    pieces = per_worker // SC_GATHER_ROWS
    assert per_worker * workers == T and pieces * SC_GATHER_ROWS == per_worker
    idx = dest.reshape(workers, pieces, SC_GATHER_ROWS, slots).transpose(0, 1, 3, 2)
    mesh = plsc.VectorSubcoreMesh(core_axis_name="core", subcore_axis_name="subcore",
                                  num_cores=V7X_SC_CORES, num_subcores=V7X_SC_SUBCORES)

    @functools.partial(
        pl.kernel, mesh=mesh,
        out_type=jax.ShapeDtypeStruct((n_out, W), rows.dtype),
        scratch_types=[pltpu.VMEM((pieces, slots, SC_GATHER_ROWS), jnp.int32),
                       pltpu.VMEM((SC_GATHER_ROWS, W), rows.dtype)],
    )
    def scatter(rows_hbm, idx_hbm, out_hbm, idx_v, rows_v):
        worker = lax.axis_index("subcore") * V7X_SC_CORES + lax.axis_index("core")
        base = pl.multiple_of(worker * per_worker, SC_GATHER_ROWS)
        pltpu.sync_copy(idx_hbm.at[worker], idx_v)

        @pl.loop(0, pieces)
        def _(g):
            off = pl.multiple_of(g * SC_GATHER_ROWS, SC_GATHER_ROWS)
            pltpu.sync_copy(rows_hbm.at[pl.ds(base + off, SC_GATHER_ROWS)], rows_v)
            for k in range(slots):
                pltpu.sync_copy(rows_v, out_hbm.at[idx_v.at[g, k]])

    return scatter(rows, idx)


def _experts_kernel(be_ref, nb_ref, x_ref, wgu_ref, bgu_ref, wd_ref, bd_ref, o_ref, wgu_bf, wd_bf):
    f32 = jnp.float32
    d_ff = wd_ref.shape[1]
    i = pl.program_id(0)
    live = i < nb_ref[0]

    @pl.when(live & ((i == 0) | (be_ref[i] != be_ref[jnp.maximum(i - 1, 0)])))
    def _():
        wgu_bf[...] = wgu_ref[0].astype(jnp.bfloat16)
        wd_bf[...] = wd_ref[0].astype(jnp.bfloat16)

    @pl.when(live)
    def _():
        x_hi, x_lo = _unpack_bf16_pairs(x_ref[...])
        half = x_hi.shape[1]
        gu = (jnp.dot(x_hi, wgu_bf[:half, :], preferred_element_type=f32)
              + jnp.dot(x_lo, wgu_bf[half:, :], preferred_element_type=f32) + bgu_ref[0])
        gate = jnp.minimum(gu[:, :d_ff], SWIGLU_LIMIT)
        lin = jnp.clip(gu[:, d_ff:], -SWIGLU_LIMIT, SWIGLU_LIMIT)
        act = (lin + 1.0) * gate * _sigmoid(SWIGLU_ALPHA * gate)
        y = jnp.dot(act.astype(jnp.bfloat16), wd_bf[...], preferred_element_type=f32) + bd_ref[0]
        o_ref[...] = _pack_bf16_pairs(y)

    @pl.when(pl.program_id(0) >= nb_ref[0])
    def _():
        o_ref[...] = jnp.zeros_like(o_ref)


def _experts(xs, block_expert, n_used, w_gu, b_gu, w_down, b_down):
    P, W = xs.shape
    E, D, F2 = w_gu.shape
    nb = P // EXPERT_ROWS
    grid_spec = pltpu.PrefetchScalarGridSpec(
        num_scalar_prefetch=2,
        grid=(nb,),
        in_specs=[
            pl.BlockSpec((EXPERT_ROWS, W), lambda i, be, nu: (i, 0)),
            pl.BlockSpec((1, D, F2), lambda i, be, nu: (be[i], 0, 0)),
            pl.BlockSpec((1, 1, F2), lambda i, be, nu: (be[i], 0, 0)),
            pl.BlockSpec((1, F2 // 2, D), lambda i, be, nu: (be[i], 0, 0)),
            pl.BlockSpec((1, 1, D), lambda i, be, nu: (be[i], 0, 0)),
        ],
        out_specs=pl.BlockSpec((EXPERT_ROWS, W), lambda i, be, nu: (i, 0)),
        scratch_shapes=[pltpu.VMEM((D, F2), jnp.bfloat16), pltpu.VMEM((F2 // 2, D), jnp.bfloat16)],
    )
    return pl.pallas_call(
        _experts_kernel,
        grid_spec=grid_spec,
        out_shape=jax.ShapeDtypeStruct((P, W), jnp.uint32),
        compiler_params=_cparams(("arbitrary",)),
        name="experts",
    )(block_expert, n_used, xs, w_gu, b_gu.reshape(E, 1, F2), w_down, b_down.reshape(E, 1, D))


def _combine_kernel(ya_ref, x1_ref, gate_ref, g_ref, o_ref):
    half = x1_ref.shape[1] // 2
    f32 = jnp.float32
    gate = gate_ref[...]
    x1 = x1_ref[...]
    y_hi = x1[:, :half]
    y_lo = x1[:, half:]
    for k in range(TOP_K):
        hi, lo = _unpack_bf16_pairs(ya_ref[k])
        y_hi = y_hi + gate[:, k:k + 1] * hi.astype(f32)
        y_lo = y_lo + gate[:, k:k + 1] * lo.astype(f32)
    o_ref[...] = _rms(jnp.concatenate([y_hi, y_lo], axis=1), g_ref[...])


def _combine(ya, x1, gates, gain):
    T, D = x1.shape
    R = min(COMBINE_ROWS, T)
    row = lambda w: pl.BlockSpec((R, w), lambda i: (i, 0))
    return pl.pallas_call(
        _combine_kernel,
        grid=(T // R,),
        in_specs=[pl.BlockSpec((TOP_K, R, D // 2), lambda i: (0, i, 0)), row(D), row(TOP_K),
                  pl.BlockSpec((1, D), lambda i: (0, 0))],
        out_specs=row(D),
        out_shape=jax.ShapeDtypeStruct((T, D), jnp.float32),
        compiler_params=_cparams(("parallel",)),
        name="combine",
    )(ya, x1, gates, gain.reshape(1, D))


def _moe_plan(eidx, rank, counts, A):
    counts = counts.reshape(N_EXPERTS)
    padded = (counts + EXPERT_ROWS - 1) // EXPERT_ROWS * EXPERT_ROWS
    pad_ends = jnp.cumsum(padded)
    pad_starts = pad_ends - padded
    n_blocks = -(-A // EXPERT_ROWS) + N_EXPERTS
    dest = pad_starts[eidx] + rank
    block_start = jnp.arange(n_blocks, dtype=pad_ends.dtype) * EXPERT_ROWS
    block_expert = jnp.minimum(jnp.sum(pad_ends[None, :] <= block_start[:, None], axis=1),
                               N_EXPERTS - 1).astype(jnp.int32)
    n_used = (pad_ends[-1] // EXPERT_ROWS).astype(jnp.int32).reshape(1)
    return dest.astype(jnp.int32), block_expert, n_used, n_blocks


def kernel(x, w_in, w_up_a, w_up_b, w_out, norm_mix, norm_ffn, norm_final, hgrn_norm,
           lb_logits, rel_bias, w_router, b_router, w_gu, b_gu, w_down, b_down):
    B, S, D = x.shape
    T = B * S
    assert w_in.shape[0] == 1, "the final rmsnorm is fused into the single layer's combine stage"
    lb_all = jnp.cumsum(jax.nn.softmax(lb_logits.astype(jnp.float32), axis=0), axis=0)
    x2 = x.reshape(T, D)
    (k, ik, qT, vT, iqT, iwT, bq, bf, bi, bg, ga, gb) = _inproj(x2, norm_mix[0], w_in[0], B, S)
    ya = _dsa(k, ik, qT, vT, iqT, iwT, rel_bias, B, S)
    yb = _hgrn(bq, bf, bi, bg, lb_all[0].reshape(B_HEADS, B_KEY_DIM), hgrn_norm[0], B, S)
    x1, xn, logits = _merge(x2, ya, yb, ga, gb, w_up_a[0], w_up_b[0], w_out[0], norm_ffn[0],
                            w_router[0], b_router[0])
    eidx, gates, rank, counts = _route(logits)
    dest, block_expert, n_used, n_blocks = _moe_plan(eidx, rank, counts, T * TOP_K)
    P = n_blocks * EXPERT_ROWS
    A = T * TOP_K
    xs = _sc_scatter_rows(xn, dest, P)
    y_buf = _experts(xs, block_expert, n_used, w_gu[0], b_gu[0], w_down[0], b_down[0])
    ya = _sc_gather_rows(y_buf, dest.T.reshape(A)).reshape(TOP_K, T, D // 2)
    out = _combine(ya, x1, gates, norm_final)
    return out.reshape(B, S, D)
```

```python
import functools
import math

import numpy as np
import jax
import jax.numpy as jnp
from jax import lax
from jax.experimental import pallas as pl
from jax.experimental.pallas import tpu as pltpu
from jax.experimental.pallas import tpu_sc as plsc

A_HEADS = 8
A_HEAD_DIM = 64
IDX_HEADS = 8
IDX_DIM = 32
TOPK_MAX = 256
REL_BUCKETS = 32
REL_MAX_DIST = 128
B_HEADS = 4
B_KEY_DIM = 128
B_VAL_DIM = 128
N_EXPERTS = 32
TOP_K = 4
SWIGLU_LIMIT = 7.0
SWIGLU_ALPHA = 1.702
EPS = 1e-6
LOG2_E = math.log2(math.e)

A_WIDTH = A_HEADS * A_HEAD_DIM
B_WIDTH = B_HEADS * B_VAL_DIM
IDX_WIDTH = IDX_HEADS * IDX_DIM

V7X_LANES = 128
V7X_SUBLANES = 8
V7X_VMEM_LIMIT_BYTES = 56 * 1024 * 1024
V7X_SC_CORES = 2
V7X_SC_SUBCORES = 16

PROJ_ROWS = 512
ATT_Q = 256
ATT_KC = 128
HGRN_ROWS = 512
HGRN_CHUNK = 64
HGRN_SAFE_DECAY = 70.0
ROUTE_ROWS = 512
EXPERT_ROWS = 256
COMBINE_ROWS = 512
SC_GATHER_ROWS = 64
MASK_NEG = -1e30
BISECT_FAST_ITERS = 26


def _cparams(dims):
    return pltpu.CompilerParams(dimension_semantics=dims, vmem_limit_bytes=V7X_VMEM_LIMIT_BYTES)


def _rms(x, gain):
    return x * lax.rsqrt(jnp.mean(x * x, axis=-1, keepdims=True) + EPS) * gain


def _sigmoid(x):
    return 1.0 / (1.0 + jnp.exp(-x))


def _pack_bf16_pairs(x):
    n = x.shape[1] // 2
    as_bits = lambda v: lax.bitcast_convert_type(v.astype(jnp.bfloat16).astype(jnp.float32), jnp.uint32)
    return (as_bits(x[:, :n]) & jnp.uint32(0xFFFF0000)) | (as_bits(x[:, n:]) >> 16)


def _unpack_bf16_pairs(w):
    hi = lax.bitcast_convert_type(w & jnp.uint32(0xFFFF0000), jnp.float32).astype(jnp.bfloat16)
    lo = lax.bitcast_convert_type(w << 16, jnp.float32).astype(jnp.bfloat16)
    return hi, lo


def _fold_rows(x, op):
    return op(x.reshape(x.shape[0] // V7X_SUBLANES, V7X_SUBLANES, x.shape[1]), axis=0)


def _inproj_kernel(x_ref, g_ref, wk_ref, wik_ref, wqT_ref, wvT_ref, wiqT_ref, wiwT_ref, wb_ref, wg_ref,
                   k_ref, ik_ref, qT_ref, vT_ref, iqT_ref, iwT_ref, bq_ref, bf_ref, bi_ref, bg_ref,
                   ga_ref, gb_ref):
    x = x_ref[...]
    hn = _rms(x, g_ref[...]).astype(jnp.bfloat16)

    def mm(w_ref):
        return jnp.dot(hn, w_ref[...], preferred_element_type=jnp.float32)

    def mm_t(w_ref):
        return lax.dot_general(w_ref[...], hn, (((1,), (1,)), ((), ())),
                               preferred_element_type=jnp.float32)

    k_ref[...] = mm(wk_ref).astype(jnp.bfloat16)
    ik_ref[...] = mm(wik_ref).astype(jnp.bfloat16)
    qT_ref[0] = (mm_t(wqT_ref) * (A_HEAD_DIM ** -0.5 * LOG2_E)).astype(jnp.bfloat16)
    vT_ref[0] = mm_t(wvT_ref).astype(jnp.bfloat16)
    iqT_ref[0] = mm_t(wiqT_ref).astype(jnp.bfloat16)
    iwT_ref[0] = mm_t(wiwT_ref) * ((IDX_HEADS * IDX_DIM) ** -0.5)
    hb = mm(wb_ref)
    bq_ref[...] = hb[:, 0 * B_WIDTH:1 * B_WIDTH]
    bf_ref[...] = hb[:, 1 * B_WIDTH:2 * B_WIDTH]
    bi_ref[...] = hb[:, 2 * B_WIDTH:3 * B_WIDTH]
    bg_ref[...] = hb[:, 3 * B_WIDTH:4 * B_WIDTH]
    d = ga_ref.shape[-1]
    hg = mm(wg_ref)
    ga_ref[...] = _sigmoid(hg[:, :d]).astype(jnp.bfloat16)
    gb_ref[...] = _sigmoid(hg[:, d:]).astype(jnp.bfloat16)


def _inproj(x2, gain, w_in, B, S):
    T, D = x2.shape
    R = min(PROJ_ROWS, S)
    nS = S // R
    o = np.cumsum((0, A_WIDTH, A_WIDTH, A_WIDTH, IDX_WIDTH, IDX_HEADS, IDX_DIM,
                   B_WIDTH, B_WIDTH, B_WIDTH, B_WIDTH, D, D))
    bf = jnp.bfloat16
    wqT = w_in[:, o[0]:o[1]].T.astype(bf)
    wk = w_in[:, o[1]:o[2]].astype(bf)
    wvT = w_in[:, o[2]:o[3]].T.astype(bf)
    wiqT = w_in[:, o[3]:o[4]].T.astype(bf)
    wiwT = w_in[:, o[4]:o[5]].T.astype(bf)
    wik = w_in[:, o[5]:o[6]].astype(bf)
    wb = w_in[:, o[6]:o[10]].astype(bf)
    wg = w_in[:, o[10]:o[12]].astype(bf)

    def full(a):
        return pl.BlockSpec(a.shape, lambda b, i: (0,) * a.ndim)

    row = lambda n: pl.BlockSpec((R, n), lambda b, i: (b * nS + i, 0))
    colT = lambda n: pl.BlockSpec((1, n, R), lambda b, i: (b, 0, i))
    f32 = jnp.float32
    outs = [
        (jax.ShapeDtypeStruct((T, A_WIDTH), bf), row(A_WIDTH)),
        (jax.ShapeDtypeStruct((T, IDX_DIM), bf), row(IDX_DIM)),
        (jax.ShapeDtypeStruct((B, A_WIDTH, S), bf), colT(A_WIDTH)),
        (jax.ShapeDtypeStruct((B, A_WIDTH, S), bf), colT(A_WIDTH)),
        (jax.ShapeDtypeStruct((B, IDX_WIDTH, S), bf), colT(IDX_WIDTH)),
        (jax.ShapeDtypeStruct((B, IDX_HEADS, S), f32), colT(IDX_HEADS)),
        (jax.ShapeDtypeStruct((T, B_WIDTH), f32), row(B_WIDTH)),
        (jax.ShapeDtypeStruct((T, B_WIDTH), f32), row(B_WIDTH)),
        (jax.ShapeDtypeStruct((T, B_WIDTH), f32), row(B_WIDTH)),
        (jax.ShapeDtypeStruct((T, B_WIDTH), f32), row(B_WIDTH)),
        (jax.ShapeDtypeStruct((T, D), bf), row(D)),
        (jax.ShapeDtypeStruct((T, D), bf), row(D)),
    ]
    ins = [x2, gain.reshape(1, D), wk, wik, wqT, wvT, wiqT, wiwT, wb, wg]
    in_specs = [row(D)] + [full(a) for a in ins[1:]]
    return pl.pallas_call(
        _inproj_kernel,
        grid=(B, nS),
        in_specs=in_specs,
        out_specs=[s for _, s in outs],
        out_shape=[s for s, _ in outs],
        compiler_params=_cparams(("parallel", "parallel")),
        name="inproj",
    )(*ins)


def _t5_bucket_table(n):
    d = np.arange(n)
    max_exact = REL_BUCKETS // 2
    nf = np.maximum(d, 1).astype(np.float64)
    large = max_exact + (np.log(nf / max_exact) / math.log(REL_MAX_DIST / max_exact)
                         * (REL_BUCKETS - max_exact)).astype(np.int32)
    large = np.minimum(large, REL_BUCKETS - 1)
    return np.where(d < max_exact, d, large)


def _dsa_kernel(qT_ref, k_ref, vT_ref, iqT_ref, iwT_ref, ik_ref, enear_ref,
                o_ref, sc_ref, qh_scr, m_scr, acc_scr, *, topk):
    TQ = qT_ref.shape[2]
    KC = TQ
    i = pl.program_id(1)
    nch = i + 1
    q0 = i * TQ
    f32 = jnp.float32
    bf16 = jnp.bfloat16
    key_id = lax.broadcasted_iota(jnp.int32, (KC, TQ), 0)
    qry_id = lax.broadcasted_iota(jnp.int32, (KC, TQ), 1)

    def col_reduce(x, op):
        return op(_fold_rows(x, op), axis=0, keepdims=True)

    iw = iwT_ref[0]

    def score_chunk(c, carry):
        rmin, rmax = carry
        k0 = pl.multiple_of(c * KC, KC)
        ik = ik_ref[pl.ds(k0, KC), :]
        acc = jnp.zeros((KC, TQ), f32)
        for h in range(IDX_HEADS):
            sh = jnp.dot(ik, iqT_ref[0, h * IDX_DIM:(h + 1) * IDX_DIM, :], preferred_element_type=f32)
            acc = acc + jnp.maximum(sh, 0.0) * iw[h:h + 1, :]
        valid = (k0 + key_id) <= (q0 + qry_id)
        sc_ref[pl.ds(k0, KC), :] = jnp.where(valid, acc, MASK_NEG)
        rmin = jnp.minimum(rmin, _fold_rows(jnp.where(valid, acc, -MASK_NEG), jnp.min))
        rmax = jnp.maximum(rmax, _fold_rows(jnp.where(valid, acc, MASK_NEG), jnp.max))
        return rmin, rmax

    rmin8, rmax8 = lax.fori_loop(
        0, nch, score_chunk,
        (jnp.full((V7X_SUBLANES, TQ), -MASK_NEG, f32), jnp.full((V7X_SUBLANES, TQ), MASK_NEG, f32)))
    rmin = jnp.min(rmin8, axis=0, keepdims=True)
    rmax = jnp.max(rmax8, axis=0, keepdims=True)

    def count_where(pred_fn):
        def body(c, acc):
            k0 = pl.multiple_of(c * KC, KC)
            blk = sc_ref[pl.ds(k0, KC), :]
            return acc + _fold_rows(jnp.where(pred_fn(blk), 1.0, 0.0), jnp.sum)
        acc = lax.fori_loop(0, nch, body, jnp.zeros((V7X_SUBLANES, TQ), f32))
        return jnp.sum(acc, axis=0, keepdims=True)

    def band_min_max(lo, hi):
        def body(c, carry):
            bmin, bmax = carry
            k0 = pl.multiple_of(c * KC, KC)
            blk = sc_ref[pl.ds(k0, KC), :]
            bmin = jnp.minimum(bmin, _fold_rows(jnp.where(blk >= lo, blk, -MASK_NEG), jnp.min))
            bmax = jnp.maximum(bmax, _fold_rows(jnp.where(blk < hi, blk, MASK_NEG), jnp.max))
            return bmin, bmax
        bmin8, bmax8 = lax.fori_loop(
            0, nch, body,
            (jnp.full((V7X_SUBLANES, TQ), -MASK_NEG, f32), jnp.full((V7X_SUBLANES, TQ), MASK_NEG, f32)))
        return jnp.min(bmin8, axis=0, keepdims=True), jnp.max(bmax8, axis=0, keepdims=True)

    kf = float(topk)
    n_valid = (q0 + 1 + lax.broadcasted_iota(jnp.int32, (1, TQ), 1)).astype(f32)
    c_top = count_where(lambda blk: blk >= rmax)
    top_tie = c_top >= kf
    lo0 = jnp.where(top_tie, rmax, rmin)
    cnt0 = jnp.where(top_tie, c_top, n_valid)
    done0 = jnp.where(top_tie | (cnt0 <= kf), 1.0, 0.0)

    def probe(st, lo_s, mid, tie):
        it, lo, hi, cnt, done = st
        active = done < 0.5
        lo_s = jnp.where(active, lo_s, lo)
        c = count_where(lambda blk: blk >= mid)
        feas = c >= kf
        move = active & jnp.logical_not(tie)
        lo_n = jnp.where(move & feas, mid, lo_s)
        cnt_n = jnp.where(move & feas, c, cnt)
        hi_n = jnp.where(move & jnp.logical_not(feas), mid, hi)
        done_n = jnp.where((active & tie) | (cnt_n <= kf), 1.0, done)
        return it + 1, lo_n, hi_n, cnt_n, done_n

    def halve(st):
        _, lo, hi, _, _ = st
        half = lo + 0.5 * (hi - lo)
        stuck = (half <= lo) | (half >= hi)
        return probe(st, lo, half, stuck)

    def snap(st):
        _, lo, hi, _, _ = st
        bmin, bmax = band_min_max(lo, hi)
        mid = bmin + 0.5 * (bmax - bmin)
        return probe(st, bmin, jnp.where(mid <= bmin, bmax, mid), bmax <= bmin)

    st = lax.while_loop(lambda st: (jnp.min(st[-1]) < 0.5) & (st[0] < BISECT_FAST_ITERS), halve,
                        (jnp.int32(0), lo0, rmax, cnt0, done0))
    _, thr, _, cnt_thr, _ = lax.while_loop(lambda st: jnp.min(st[-1]) < 0.5, snap, st)

    tie_overflow = jnp.max(cnt_thr) > kf

    @pl.when(jnp.logical_not(tie_overflow))
    def _():
        def mask_chunk(c, _):
            k0 = pl.multiple_of(c * KC, KC)
            sc_ref[pl.ds(k0, KC), :] = jnp.where(sc_ref[pl.ds(k0, KC), :] >= thr, 0.0, MASK_NEG)
            return 0
        lax.fori_loop(0, nch, mask_chunk, 0)

    @pl.when(tie_overflow)
    def _():
        need = kf - count_where(lambda blk: blk > thr)
        tril = jnp.where(lax.broadcasted_iota(jnp.int32, (KC, KC), 1)
                         <= lax.broadcasted_iota(jnp.int32, (KC, KC), 0), 1.0, 0.0).astype(bf16)

        def mask_chunk(c, run):
            k0 = pl.multiple_of(c * KC, KC)
            blk = sc_ref[pl.ds(k0, KC), :]
            eq = jnp.where(blk == thr, 1.0, 0.0)
            pref = jnp.dot(tril, eq.astype(bf16), preferred_element_type=f32)
            sel = (blk > thr) | ((eq > 0.5) & (run + pref <= need))
            sc_ref[pl.ds(k0, KC), :] = jnp.where(sel, 0.0, MASK_NEG)
            return run + pref[KC - 1:KC, :]

        lax.fori_loop(0, nch, mask_chunk, jnp.zeros((1, TQ), f32))

    AK = min(ATT_KC, TQ)
    per = TQ // AK
    head0_q = (lax.broadcasted_iota(jnp.int32, (V7X_LANES, TQ), 0) // A_HEAD_DIM) == 0
    n_pairs = A_HEADS // 2

    m_scr[...] = jnp.full(m_scr.shape, MASK_NEG, f32)
    acc_scr[...] = jnp.zeros(acc_scr.shape, f32)
    v_row = lax.broadcasted_iota(jnp.int32, (V7X_LANES, AK), 0)
    denom_row = [A_HEAD_DIM * (1 - sub) for sub in range(2)]
    for p in range(n_pairs):
        q_pair = qT_ref[0, p * V7X_LANES:(p + 1) * V7X_LANES, :]
        zq = jnp.zeros_like(q_pair)
        qh_scr[2 * p] = jnp.where(head0_q, q_pair, zq)
        qh_scr[2 * p + 1] = jnp.where(head0_q, zq, q_pair)

    def step(c, bias_rows):
        k0 = pl.multiple_of(c * AK, AK)
        msk = sc_ref[pl.ds(k0, AK), :]
        for p in range(n_pairs):
            kp = k_ref[pl.ds(k0, AK), p * V7X_LANES:(p + 1) * V7X_LANES]
            vp = vT_ref[0, p * V7X_LANES:(p + 1) * V7X_LANES, pl.ds(k0, AK)]
            for sub in range(2):
                h = 2 * p + sub
                s = jnp.dot(kp, qh_scr[h], preferred_element_type=f32) + msk
                if bias_rows is not None:
                    s = s + enear_ref[h, bias_rows, :]
                m = m_scr[h:h + 1, :]
                m_new = jnp.maximum(m, col_reduce(s, jnp.max))
                alpha = jnp.exp2(m - m_new)
                pr = jnp.exp2(s - m_new)
                m_scr[h:h + 1, :] = m_new
                v_aug = jnp.where(v_row == denom_row[sub], jnp.ones_like(vp), vp)
                acc_scr[h] = alpha * acc_scr[h] + jnp.dot(v_aug, pr.astype(bf16), preferred_element_type=f32)

    def far(blk, _):
        for jj in range(per):
            step(blk * per + jj, None)
        return 0

    def near(block, first_chunk):
        for jj in range(per):
            step(first_chunk + jj, slice(block * TQ + jj * AK, block * TQ + (jj + 1) * AK))

    lax.fori_loop(0, jnp.maximum(i - 1, 0), far, 0)

    @pl.when(i >= 1)
    def _():
        near(0, (i - 1) * per)

    near(1, i * per)
    for p in range(n_pairs):
        outs = [acc_scr[2 * p + sub] / acc_scr[2 * p + sub, denom_row[sub]:denom_row[sub] + 1, :]
                for sub in range(2)]
        o_pair = jnp.where(head0_q, outs[0], outs[1])
        o_ref[:, p * V7X_LANES:(p + 1) * V7X_LANES] = o_pair.T.astype(o_ref.dtype)


def _dsa(k, ik, qT, vT, iqT, iwT, rel_bias, B, S):
    T = k.shape[0]
    TQ = min(ATT_Q, S)
    nQ = S // TQ
    topk = min(TOPK_MAX, S // 4)
    buckets = _t5_bucket_table(2 * TQ + 1)
    assert np.all(_t5_bucket_table(S + 1)[TQ + 1:] == REL_BUCKETS - 1)
    j = np.arange(2 * TQ)[:, None]
    r = np.arange(TQ)[None, :]
    dist = np.maximum(r + TQ - j, 0)
    onehot = (jnp.asarray(buckets[dist], jnp.int32)[None]
              == jnp.arange(REL_BUCKETS, dtype=jnp.int32)[:, None, None]).astype(jnp.float32)
    rel = (rel_bias.astype(jnp.float32) - rel_bias[REL_BUCKETS - 1].astype(jnp.float32)[None, :]) * LOG2_E
    enear = jnp.einsum('nh,njr->hjr', rel, onehot, precision=lax.Precision.HIGHEST)

    return pl.pallas_call(
        functools.partial(_dsa_kernel, topk=topk),
        grid=(B, nQ),
        in_specs=[
            pl.BlockSpec((1, A_WIDTH, TQ), lambda b, i: (b, 0, i)),
            pl.BlockSpec((S, A_WIDTH), lambda b, i: (b, 0)),
            pl.BlockSpec((1, A_WIDTH, S), lambda b, i: (b, 0, 0)),
            pl.BlockSpec((1, IDX_WIDTH, TQ), lambda b, i: (b, 0, i)),
            pl.BlockSpec((1, IDX_HEADS, TQ), lambda b, i: (b, 0, i)),
            pl.BlockSpec((S, IDX_DIM), lambda b, i: (b, 0)),
            pl.BlockSpec((A_HEADS, 2 * TQ, TQ), lambda b, i: (0, 0, 0)),
        ],
        out_specs=pl.BlockSpec((TQ, A_WIDTH), lambda b, i: (b * nQ + i, 0)),
        scratch_shapes=[pltpu.VMEM((S, TQ), jnp.float32),
                        pltpu.VMEM((A_HEADS, V7X_LANES, TQ), jnp.bfloat16),
                        pltpu.VMEM((A_HEADS, TQ), jnp.float32),
                        pltpu.VMEM((A_HEADS, V7X_LANES, TQ), jnp.float32)],
        out_shape=jax.ShapeDtypeStruct((T, A_WIDTH), jnp.bfloat16),
        compiler_params=_cparams(("parallel", "arbitrary")),
        name="dsa",
    )(qT, k, vT, iqT, iwT, ik, enear)


def _hgrn_kernel(bq_ref, bf_ref, bi_ref, bg_ref, lb_ref, gain_ref, o_ref,
                 st_ref, b_scr, q_scr, k_scr, v_scr, oi_scr, qd_s, kd_s, kl_s, vv_s, dec_s, oi_s, upd_s):
    R = bq_ref.shape[0]
    C = HGRN_CHUNK
    nC = R // C
    f32 = jnp.float32
    bf16 = jnp.bfloat16
    h = pl.program_id(1)

    @pl.when(pl.program_id(2) == 0)
    def _():
        st_ref[...] = jnp.zeros_like(st_ref)

    lb = lb_ref[pl.ds(h, 1), :]
    gain = gain_ref[pl.ds(h, 1), :]
    tril_incl = jnp.where(lax.broadcasted_iota(jnp.int32, (C, C), 1)
                          <= lax.broadcasted_iota(jnp.int32, (C, C), 0), 1.0, 0.0)
    srow = lax.broadcasted_iota(jnp.int32, (C, B_KEY_DIM), 0)

    def gates(r0):
        f = lb + (1.0 - lb) * _sigmoid(bf_ref[pl.ds(r0, C), :])
        qr = bq_ref[pl.ds(r0, C), :]
        return jnp.log(f), 1.0 - f, qr * _sigmoid(qr) * (B_KEY_DIM ** -0.5), bi_ref[pl.ds(r0, C), :]

    def cumdecay(g):
        tri = tril_incl.astype(bf16)
        g_hi = g.astype(bf16)
        rest = g - g_hi.astype(f32)
        g_mid = rest.astype(bf16)
        g_lo = (rest - g_mid.astype(f32)).astype(bf16)
        return (jnp.dot(tri, g_hi, preferred_element_type=f32) + jnp.dot(tri, g_mid, preferred_element_type=f32)
                + jnp.dot(tri, g_lo, preferred_element_type=f32))

    def advance(r0, st, qd, o_intra, upd, decay_row):
        o_inter = lax.dot_general(qd, st.astype(bf16), (((1,), (1,)), ((), ())), preferred_element_type=f32)
        og = bg_ref[pl.ds(r0, C), :]
        y = _rms(o_inter + o_intra, gain) * (og * _sigmoid(og))
        o_ref[pl.ds(r0, C), :] = y.astype(o_ref.dtype)
        return st * decay_row + upd

    f_all = lb + (1.0 - lb) * _sigmoid(bf_ref[...])
    g_all = jnp.log(f_all)
    decay = jnp.sum(g_all.reshape(nC, C, B_KEY_DIM), axis=1)
    safe = jnp.min(decay) >= -HGRN_SAFE_DECAY

    @pl.when(safe)
    def _():
        qr = bq_ref[...]
        qq = qr * _sigmoid(qr) * (B_KEY_DIM ** -0.5)
        kk = 1.0 - f_all
        b = jnp.concatenate([cumdecay(g_all[c * C:(c + 1) * C]) for c in range(nC)], axis=0)
        b_end = jnp.concatenate([jnp.broadcast_to(b[(c + 1) * C - 1:(c + 1) * C], (C, B_KEY_DIM))
                                 for c in range(nC)], axis=0)
        qd_s[...] = (qq * jnp.exp(b)).astype(bf16)
        kd_s[...] = (kk * jnp.exp(-b)).astype(bf16)
        kl_s[...] = (kk * jnp.exp(b_end - b)).astype(bf16)
        vv_s[...] = bi_ref[...].astype(bf16)
        dec_s[...] = jnp.exp(b_end)
        for c in range(nC):
            rows = slice(c * C, (c + 1) * C)
            att = lax.dot_general(qd_s[rows], kd_s[rows], (((1,), (1,)), ((), ())),
                                  preferred_element_type=f32) * tril_incl
            oi_s[rows] = jnp.dot(att.astype(bf16), vv_s[rows], preferred_element_type=f32)
            upd_s[c] = lax.dot_general(vv_s[rows], kl_s[rows], (((0,), (0,)), ((), ())),
                                       preferred_element_type=f32)
        st = st_ref[...]
        for c in range(nC):
            rows = slice(c * C, (c + 1) * C)
            st = advance(c * C, st, qd_s[rows], oi_s[rows], upd_s[c], dec_s[c * C:c * C + 1])
        st_ref[...] = st

    @pl.when(jnp.logical_not(safe))
    def _():
        def body(c, st):
            r0 = pl.multiple_of(c * C, C)
            g, kk, qq, vv = gates(r0)
            b = cumdecay(g)
            b_last = b[C - 1:C, :]
            b_scr[...] = b
            q_scr[...] = qq
            k_scr[...] = kk
            v_scr[...] = vv

            def row(t, _):
                bt = b_scr[pl.ds(t, 1), :]
                qt = q_scr[pl.ds(t, 1), :]
                ex = jnp.where(srow <= t, bt - b_scr[...], -jnp.inf)
                a = jnp.sum(qt * k_scr[...] * jnp.exp(ex), axis=1, keepdims=True)
                oi_scr[pl.ds(t, 1), :] = jnp.sum(a * v_scr[...], axis=0, keepdims=True)
                return 0
            lax.fori_loop(0, C, row, 0)
            kd_last = (kk * jnp.exp(b_last - b)).astype(bf16)
            upd = lax.dot_general(vv.astype(bf16), kd_last, (((0,), (0,)), ((), ())),
                                  preferred_element_type=f32)
            return advance(r0, st, (qq * jnp.exp(b)).astype(bf16), oi_scr[...], upd, jnp.exp(b_last))
        st_ref[...] = lax.fori_loop(0, nC, body, st_ref[...])


def _hgrn(bq, bf, bi, bg, lb, gain, B, S):
    T = bq.shape[0]
    R = min(HGRN_ROWS, S)
    nR = S // R
    C = HGRN_CHUNK
    blk = pl.BlockSpec((R, B_KEY_DIM), lambda b, h, c: (b * nR + c, h))
    small = pl.BlockSpec((B_HEADS, B_KEY_DIM), lambda b, h, c: (0, 0))
    f32 = jnp.float32
    return pl.pallas_call(
        _hgrn_kernel,
        grid=(B, B_HEADS, nR),
        in_specs=[blk, blk, blk, blk, small, small],
        out_specs=blk,
        out_shape=jax.ShapeDtypeStruct((T, B_WIDTH), jnp.bfloat16),
        scratch_shapes=[pltpu.VMEM((B_VAL_DIM, B_KEY_DIM), f32)] +
                       [pltpu.VMEM((C, B_KEY_DIM), f32) for _ in range(5)] +
                       [pltpu.VMEM((R, B_KEY_DIM), jnp.bfloat16) for _ in range(4)] +
                       [pltpu.VMEM((R, B_KEY_DIM), f32) for _ in range(2)] +
                       [pltpu.VMEM((R // C, B_VAL_DIM, B_KEY_DIM), f32)],
        compiler_params=_cparams(("parallel", "parallel", "arbitrary")),
        name="hgrn",
    )(bq, bf, bi, bg, lb, gain)


def _merge_kernel(x_ref, ya_ref, yb_ref, ga_ref, gb_ref, wa_ref, wb_ref, wo_ref, g_ref, wrh_ref, wrl_ref, br_ref,
                  x1_ref, xn_ref, lg_ref):
    f32 = jnp.float32
    ma = jnp.dot(ya_ref[...], wa_ref[...], preferred_element_type=f32)
    mb = jnp.dot(yb_ref[...], wb_ref[...], preferred_element_type=f32)
    merged = ga_ref[...].astype(f32) * ma + gb_ref[...].astype(f32) * mb
    x1 = x_ref[...] + jnp.dot(merged.astype(jnp.bfloat16), wo_ref[...], preferred_element_type=f32)
    x1_ref[...] = x1
    hn = _rms(x1, g_ref[...])
    xn_ref[...] = _pack_bf16_pairs(hn)
    hn_hi = hn.astype(jnp.bfloat16)
    hn_lo = (hn - hn_hi.astype(f32)).astype(jnp.bfloat16)
    lg_ref[...] = (jnp.dot(hn_hi, wrh_ref[...], preferred_element_type=f32)
                   + jnp.dot(hn_lo, wrh_ref[...], preferred_element_type=f32)
                   + jnp.dot(hn_hi, wrl_ref[...], preferred_element_type=f32) + br_ref[...])


def _merge(x2, ya, yb, ga, gb, w_up_a, w_up_b, w_out, gain, w_router, b_router):
    T, D = x2.shape
    R = min(PROJ_ROWS, T)
    bf = jnp.bfloat16
    wr_hi = w_router.astype(bf)
    wr_lo = (w_router - wr_hi.astype(jnp.float32)).astype(bf)
    ins = [x2, ya, yb, ga, gb, w_up_a.astype(bf), w_up_b.astype(bf), w_out.astype(bf),
           gain.reshape(1, D), wr_hi, wr_lo, b_router.reshape(1, N_EXPERTS)]
    row = lambda n: pl.BlockSpec((R, n), lambda i: (i, 0))
    full = lambda a: pl.BlockSpec(a.shape, lambda i: (0,) * a.ndim)
    in_specs = [row(D), row(A_WIDTH), row(B_WIDTH), row(D), row(D)] + [full(a) for a in ins[5:]]
    return pl.pallas_call(
        _merge_kernel,
        grid=(T // R,),
        in_specs=in_specs,
        out_specs=[row(D), row(D // 2), row(N_EXPERTS)],
        out_shape=[jax.ShapeDtypeStruct((T, D), jnp.float32), jax.ShapeDtypeStruct((T, D // 2), jnp.uint32),
                   jax.ShapeDtypeStruct((T, N_EXPERTS), jnp.float32)],
        compiler_params=_cparams(("parallel",)),
        name="merge",
    )(*ins)


def _route_kernel(lg_ref, eidx_ref, gate_ref, rank_ref, cnt_ref, run_ref):
    R = lg_ref.shape[0]
    f32 = jnp.float32

    @pl.when(pl.program_id(0) == 0)
    def _():
        run_ref[...] = jnp.zeros_like(run_ref)

    lg = lg_ref[...]
    lane = lax.broadcasted_iota(jnp.int32, (R, N_EXPERTS), 1)
    work = lg
    onehots, vals, idxs = [], [], []
    for _ in range(TOP_K):
        m = jnp.max(work, axis=1, keepdims=True)
        idx = jnp.min(jnp.where(work == m, lane, N_EXPERTS), axis=1, keepdims=True)
        oh = lane == idx
        onehots.append(oh)
        vals.append(m)
        idxs.append(idx)
        work = jnp.where(oh, -jnp.inf, work)
    ex = [jnp.exp(v - vals[0]) for v in vals]
    den = ex[0] + ex[1] + ex[2] + ex[3]
    chosen = jnp.where(onehots[0] | onehots[1] | onehots[2] | onehots[3], 1.0, 0.0)
    strict = jnp.where(lax.broadcasted_iota(jnp.int32, (R, R), 1)
                       < lax.broadcasted_iota(jnp.int32, (R, R), 0), 1.0, 0.0).astype(jnp.bfloat16)
    before = jnp.dot(strict, chosen.astype(jnp.bfloat16), preferred_element_type=f32) + run_ref[...]
    lane4 = lax.broadcasted_iota(jnp.int32, (R, TOP_K), 1)
    eidx = jnp.zeros((R, TOP_K), jnp.int32)
    gate = jnp.zeros((R, TOP_K), f32)
    rank = jnp.zeros((R, TOP_K), f32)
    for k in range(TOP_K):
        eidx = jnp.where(lane4 == k, idxs[k], eidx)
        gate = jnp.where(lane4 == k, ex[k] / den, gate)
        rk = jnp.sum(jnp.where(onehots[k], before, 0.0), axis=1, keepdims=True)
        rank = jnp.where(lane4 == k, rk, rank)
    eidx_ref[...] = eidx
    gate_ref[...] = gate
    rank_ref[...] = rank.astype(jnp.int32)
    run_ref[...] = run_ref[...] + jnp.sum(chosen, axis=0, keepdims=True)
    cnt_ref[...] = run_ref[...].astype(jnp.int32)


def _route(logits):
    T = logits.shape[0]
    R = min(ROUTE_ROWS, T)
    row = lambda n: pl.BlockSpec((R, n), lambda i: (i, 0))
    return pl.pallas_call(
        _route_kernel,
        grid=(T // R,),
        in_specs=[row(N_EXPERTS)],
        out_specs=[row(TOP_K), row(TOP_K), row(TOP_K), pl.BlockSpec((1, N_EXPERTS), lambda i: (0, 0))],
        out_shape=[jax.ShapeDtypeStruct((T, TOP_K), jnp.int32), jax.ShapeDtypeStruct((T, TOP_K), jnp.float32),
                   jax.ShapeDtypeStruct((T, TOP_K), jnp.int32), jax.ShapeDtypeStruct((1, N_EXPERTS), jnp.int32)],
        scratch_shapes=[pltpu.VMEM((1, N_EXPERTS), jnp.float32)],
        compiler_params=_cparams(("arbitrary",)),
        name="route",
    )(logits)


def _sc_gather_rows(table, idx):
    N, W = table.shape
    M = idx.shape[0]
    workers = V7X_SC_CORES * V7X_SC_SUBCORES
    per_worker = M // workers
    assert per_worker * workers == M and per_worker % SC_GATHER_ROWS == 0
    mesh = plsc.VectorSubcoreMesh(core_axis_name="core", subcore_axis_name="subcore",
                                  num_cores=V7X_SC_CORES, num_subcores=V7X_SC_SUBCORES)

    @functools.partial(
        pl.kernel, mesh=mesh,
        out_type=jax.ShapeDtypeStruct((M, W), table.dtype),
        scratch_types=[pltpu.VMEM((per_worker,), jnp.int32),
                       pltpu.VMEM((SC_GATHER_ROWS, W), table.dtype),
                       pltpu.SemaphoreType.DMA],
    )
    def gather(table_hbm, idx_hbm, out_hbm, idx_v, rows_v, sem):
        worker = lax.axis_index("subcore") * V7X_SC_CORES + lax.axis_index("core")
        base = pl.multiple_of(worker * per_worker, SC_GATHER_ROWS)
        pltpu.sync_copy(idx_hbm.at[pl.ds(base, per_worker)], idx_v)

        @pl.loop(0, per_worker // SC_GATHER_ROWS)
        def _(g):
            off = pl.multiple_of(g * SC_GATHER_ROWS, SC_GATHER_ROWS)
            pltpu.async_copy(table_hbm.at[idx_v.at[pl.ds(off, SC_GATHER_ROWS)]], rows_v, sem).wait()
            pltpu.sync_copy(rows_v, out_hbm.at[pl.ds(base + off, SC_GATHER_ROWS)])

    return gather(table, idx)


def _sc_scatter_rows(rows, dest, n_out):
    T, W = rows.shape
    slots = dest.shape[1]
    workers = V7X_SC_CORES * V7X_SC_SUBCORES
    per_worker = T // workers
    pieces = per_worker // SC_GATHER_ROWS
    assert per_worker * workers == T and pieces * SC_GATHER_ROWS == per_worker
    idx = dest.reshape(workers, pieces, SC_GATHER_ROWS, slots).transpose(0, 1, 3, 2)
    mesh = plsc.VectorSubcoreMesh(core_axis_name="core", subcore_axis_name="subcore",
                                  num_cores=V7X_SC_CORES, num_subcores=V7X_SC_SUBCORES)

    @functools.partial(
        pl.kernel, mesh=mesh,
        out_type=jax.ShapeDtypeStruct((n_out, W), rows.dtype),
        scratch_types=[pltpu.VMEM((pieces, slots, SC_GATHER_ROWS), jnp.int32),
                       pltpu.VMEM((SC_GATHER_ROWS, W), rows.dtype)],
    )
    def scatter(rows_hbm, idx_hbm, out_hbm, idx_v, rows_v):
        worker = lax.axis_index("subcore") * V7X_SC_CORES + lax.axis_index("core")
        base = pl.multiple_of(worker * per_worker, SC_GATHER_ROWS)
        pltpu.sync_copy(idx_hbm.at[worker], idx_v)

        @pl.loop(0, pieces)
        def _(g):
            off = pl.multiple_of(g * SC_GATHER_ROWS, SC_GATHER_ROWS)
            pltpu.sync_copy(rows_hbm.at[pl.ds(base + off, SC_GATHER_ROWS)], rows_v)
            for k in range(slots):
                pltpu.sync_copy(rows_v, out_hbm.at[idx_v.at[g, k]])

    return scatter(rows, idx)


def _experts_kernel(be_ref, nb_ref, x_ref, wgu_ref, bgu_ref, wd_ref, bd_ref, o_ref, wgu_bf, wd_bf):
    f32 = jnp.float32
    d_ff = wd_ref.shape[1]
    i = pl.program_id(0)
    live = i < nb_ref[0]

    @pl.when(live & ((i == 0) | (be_ref[i] != be_ref[jnp.maximum(i - 1, 0)])))
    def _():
        wgu_bf[...] = wgu_ref[0].astype(jnp.bfloat16)
        wd_bf[...] = wd_ref[0].astype(jnp.bfloat16)

    @pl.when(live)
    def _():
        x_hi, x_lo = _unpack_bf16_pairs(x_ref[...])
        half = x_hi.shape[1]
        gu = (jnp.dot(x_hi, wgu_bf[:half, :], preferred_element_type=f32)
              + jnp.dot(x_lo, wgu_bf[half:, :], preferred_element_type=f32) + bgu_ref[0])
        gate = jnp.minimum(gu[:, :d_ff], SWIGLU_LIMIT)
        lin = jnp.clip(gu[:, d_ff:], -SWIGLU_LIMIT, SWIGLU_LIMIT)
        act = (lin + 1.0) * gate * _sigmoid(SWIGLU_ALPHA * gate)
        y = jnp.dot(act.astype(jnp.bfloat16), wd_bf[...], preferred_element_type=f32) + bd_ref[0]
        o_ref[...] = _pack_bf16_pairs(y)

    @pl.when(pl.program_id(0) >= nb_ref[0])
    def _():
        o_ref[...] = jnp.zeros_like(o_ref)


def _experts(xs, block_expert, n_used, w_gu, b_gu, w_down, b_down):
    P, W = xs.shape
    E, D, F2 = w_gu.shape
    nb = P // EXPERT_ROWS
    grid_spec = pltpu.PrefetchScalarGridSpec(
        num_scalar_prefetch=2,
        grid=(nb,),
        in_specs=[
            pl.BlockSpec((EXPERT_ROWS, W), lambda i, be, nu: (i, 0)),
            pl.BlockSpec((1, D, F2), lambda i, be, nu: (be[i], 0, 0)),
            pl.BlockSpec((1, 1, F2), lambda i, be, nu: (be[i], 0, 0)),
            pl.BlockSpec((1, F2 // 2, D), lambda i, be, nu: (be[i], 0, 0)),
            pl.BlockSpec((1, 1, D), lambda i, be, nu: (be[i], 0, 0)),
        ],
        out_specs=pl.BlockSpec((EXPERT_ROWS, W), lambda i, be, nu: (i, 0)),
        scratch_shapes=[pltpu.VMEM((D, F2), jnp.bfloat16), pltpu.VMEM((F2 // 2, D), jnp.bfloat16)],
    )
    return pl.pallas_call(
        _experts_kernel,
        grid_spec=grid_spec,
        out_shape=jax.ShapeDtypeStruct((P, W), jnp.uint32),
        compiler_params=_cparams(("arbitrary",)),
        name="experts",
    )(block_expert, n_used, xs, w_gu, b_gu.reshape(E, 1, F2), w_down, b_down.reshape(E, 1, D))


def _combine_kernel(ya_ref, x1_ref, gate_ref, g_ref, o_ref):
    half = x1_ref.shape[1] // 2
    f32 = jnp.float32
    gate = gate_ref[...]
    x1 = x1_ref[...]
    y_hi = x1[:, :half]
    y_lo = x1[:, half:]
    for k in range(TOP_K):
        hi, lo = _unpack_bf16_pairs(ya_ref[k])
        y_hi = y_hi + gate[:, k:k + 1] * hi.astype(f32)
        y_lo = y_lo + gate[:, k:k + 1] * lo.astype(f32)
    o_ref[...] = _rms(jnp.concatenate([y_hi, y_lo], axis=1), g_ref[...])


def _combine(ya, x1, gates, gain):
    T, D = x1.shape
    R = min(COMBINE_ROWS, T)
    row = lambda w: pl.BlockSpec((R, w), lambda i: (i, 0))
    return pl.pallas_call(
        _combine_kernel,
        grid=(T // R,),
        in_specs=[pl.BlockSpec((TOP_K, R, D // 2), lambda i: (0, i, 0)), row(D), row(TOP_K),
                  pl.BlockSpec((1, D), lambda i: (0, 0))],
        out_specs=row(D),
        out_shape=jax.ShapeDtypeStruct((T, D), jnp.float32),
        compiler_params=_cparams(("parallel",)),
        name="combine",
    )(ya, x1, gates, gain.reshape(1, D))


def _moe_plan(eidx, rank, counts, A):
    counts = counts.reshape(N_EXPERTS)
    padded = (counts + EXPERT_ROWS - 1) // EXPERT_ROWS * EXPERT_ROWS
    pad_ends = jnp.cumsum(padded)
    pad_starts = pad_ends - padded
    n_blocks = -(-A // EXPERT_ROWS) + N_EXPERTS
    dest = pad_starts[eidx] + rank
    block_start = jnp.arange(n_blocks, dtype=pad_ends.dtype) * EXPERT_ROWS
    block_expert = jnp.minimum(jnp.sum(pad_ends[None, :] <= block_start[:, None], axis=1),
                               N_EXPERTS - 1).astype(jnp.int32)
    n_used = (pad_ends[-1] // EXPERT_ROWS).astype(jnp.int32).reshape(1)
    return dest.astype(jnp.int32), block_expert, n_used, n_blocks


def kernel(x, w_in, w_up_a, w_up_b, w_out, norm_mix, norm_ffn, norm_final, hgrn_norm,
           lb_logits, rel_bias, w_router, b_router, w_gu, b_gu, w_down, b_down):
    B, S, D = x.shape
    T = B * S
    assert w_in.shape[0] == 1, "the final rmsnorm is fused into the single layer's combine stage"
    lb_all = jnp.cumsum(jax.nn.softmax(lb_logits.astype(jnp.float32), axis=0), axis=0)
    x2 = x.reshape(T, D)
    (k, ik, qT, vT, iqT, iwT, bq, bf, bi, bg, ga, gb) = _inproj(x2, norm_mix[0], w_in[0], B, S)
    ya = _dsa(k, ik, qT, vT, iqT, iwT, rel_bias, B, S)
    yb = _hgrn(bq, bf, bi, bg, lb_all[0].reshape(B_HEADS, B_KEY_DIM), hgrn_norm[0], B, S)
    x1, xn, logits = _merge(x2, ya, yb, ga, gb, w_up_a[0], w_up_b[0], w_out[0], norm_ffn[0],
                            w_router[0], b_router[0])
    eidx, gates, rank, counts = _route(logits)
    dest, block_expert, n_used, n_blocks = _moe_plan(eidx, rank, counts, T * TOP_K)
    P = n_blocks * EXPERT_ROWS
    A = T * TOP_K
    xs = _sc_scatter_rows(xn, dest, P)
    y_buf = _experts(xs, block_expert, n_used, w_gu[0], b_gu[0], w_down[0], b_down[0])
    ya = _sc_gather_rows(y_buf, dest.T.reshape(A)).reshape(TOP_K, T, D // 2)
    out = _combine(ya, x1, gates, norm_final)
    return out.reshape(B, S, D)
```

```python
import functools
import math

import numpy as np
import jax
import jax.numpy as jnp
from jax import lax
from jax.experimental import pallas as pl
from jax.experimental.pallas import tpu as pltpu
from jax.experimental.pallas import tpu_sc as plsc

A_HEADS = 8
A_HEAD_DIM = 64
IDX_HEADS = 8
IDX_DIM = 32
TOPK_MAX = 256
REL_BUCKETS = 32
REL_MAX_DIST = 128
B_HEADS = 4
B_KEY_DIM = 128
B_VAL_DIM = 128
N_EXPERTS = 32
TOP_K = 4
SWIGLU_LIMIT = 7.0
SWIGLU_ALPHA = 1.702
EPS = 1e-6
LOG2_E = math.log2(math.e)

A_WIDTH = A_HEADS * A_HEAD_DIM
B_WIDTH = B_HEADS * B_VAL_DIM
IDX_WIDTH = IDX_HEADS * IDX_DIM

V7X_LANES = 128
V7X_SUBLANES = 8
V7X_VMEM_LIMIT_BYTES = 56 * 1024 * 1024
V7X_SC_CORES = 2
V7X_SC_SUBCORES = 16

PROJ_ROWS = 512
ATT_Q = 256
ATT_KC = 128
HGRN_ROWS = 512
HGRN_CHUNK = 64
HGRN_SAFE_DECAY = 70.0
ROUTE_ROWS = 512
EXPERT_ROWS = 256
COMBINE_ROWS = 512
SC_GATHER_ROWS = 64
MASK_NEG = -1e30
BISECT_FAST_ITERS = 26


def _cparams(dims):
    return pltpu.CompilerParams(dimension_semantics=dims, vmem_limit_bytes=V7X_VMEM_LIMIT_BYTES)


def _rms(x, gain):
    return x * lax.rsqrt(jnp.mean(x * x, axis=-1, keepdims=True) + EPS) * gain


def _sigmoid(x):
    return 1.0 / (1.0 + jnp.exp(-x))


def _pack_bf16_pairs(x):
    n = x.shape[1] // 2
    as_bits = lambda v: lax.bitcast_convert_type(v.astype(jnp.bfloat16).astype(jnp.float32), jnp.uint32)
    return (as_bits(x[:, :n]) & jnp.uint32(0xFFFF0000)) | (as_bits(x[:, n:]) >> 16)


def _unpack_bf16_pairs(w):
    hi = lax.bitcast_convert_type(w & jnp.uint32(0xFFFF0000), jnp.float32).astype(jnp.bfloat16)
    lo = lax.bitcast_convert_type(w << 16, jnp.float32).astype(jnp.bfloat16)
    return hi, lo


def _fold_rows(x, op):
    return op(x.reshape(x.shape[0] // V7X_SUBLANES, V7X_SUBLANES, x.shape[1]), axis=0)


def _inproj_kernel(x_ref, g_ref, wk_ref, wik_ref, wqT_ref, wvT_ref, wiqT_ref, wiwT_ref, wb_ref, wg_ref,
                   k_ref, ik_ref, qT_ref, vT_ref, iqT_ref, iwT_ref, bq_ref, bf_ref, bi_ref, bg_ref,
                   ga_ref, gb_ref):
    x = x_ref[...]
    hn = _rms(x, g_ref[...]).astype(jnp.bfloat16)

    def mm(w_ref):
        return jnp.dot(hn, w_ref[...], preferred_element_type=jnp.float32)

    def mm_t(w_ref):
        return lax.dot_general(w_ref[...], hn, (((1,), (1,)), ((), ())),
                               preferred_element_type=jnp.float32)

    k_ref[...] = mm(wk_ref).astype(jnp.bfloat16)
    ik_ref[...] = mm(wik_ref).astype(jnp.bfloat16)
    qT_ref[0] = (mm_t(wqT_ref) * (A_HEAD_DIM ** -0.5 * LOG2_E)).astype(jnp.bfloat16)
    vT_ref[0] = mm_t(wvT_ref).astype(jnp.bfloat16)
    iqT_ref[0] = mm_t(wiqT_ref).astype(jnp.bfloat16)
    iwT_ref[0] = mm_t(wiwT_ref) * ((IDX_HEADS * IDX_DIM) ** -0.5)
    hb = mm(wb_ref)
    bq_ref[...] = hb[:, 0 * B_WIDTH:1 * B_WIDTH]
    bf_ref[...] = hb[:, 1 * B_WIDTH:2 * B_WIDTH]
    bi_ref[...] = hb[:, 2 * B_WIDTH:3 * B_WIDTH]
    bg_ref[...] = hb[:, 3 * B_WIDTH:4 * B_WIDTH]
    d = ga_ref.shape[-1]
    hg = mm(wg_ref)
    ga_ref[...] = _sigmoid(hg[:, :d]).astype(jnp.bfloat16)
    gb_ref[...] = _sigmoid(hg[:, d:]).astype(jnp.bfloat16)


def _inproj(x2, gain, w_in, B, S):
    T, D = x2.shape
    R = min(PROJ_ROWS, S)
    nS = S // R
    o = np.cumsum((0, A_WIDTH, A_WIDTH, A_WIDTH, IDX_WIDTH, IDX_HEADS, IDX_DIM,
                   B_WIDTH, B_WIDTH, B_WIDTH, B_WIDTH, D, D))
    bf = jnp.bfloat16
    wqT = w_in[:, o[0]:o[1]].T.astype(bf)
    wk = w_in[:, o[1]:o[2]].astype(bf)
    wvT = w_in[:, o[2]:o[3]].T.astype(bf)
    wiqT = w_in[:, o[3]:o[4]].T.astype(bf)
    wiwT = w_in[:, o[4]:o[5]].T.astype(bf)
    wik = w_in[:, o[5]:o[6]].astype(bf)
    wb = w_in[:, o[6]:o[10]].astype(bf)
    wg = w_in[:, o[10]:o[12]].astype(bf)

    def full(a):
        return pl.BlockSpec(a.shape, lambda b, i: (0,) * a.ndim)

    row = lambda n: pl.BlockSpec((R, n), lambda b, i: (b * nS + i, 0))
    colT = lambda n: pl.BlockSpec((1, n, R), lambda b, i: (b, 0, i))
    f32 = jnp.float32
    outs = [
        (jax.ShapeDtypeStruct((T, A_WIDTH), bf), row(A_WIDTH)),
        (jax.ShapeDtypeStruct((T, IDX_DIM), bf), row(IDX_DIM)),
        (jax.ShapeDtypeStruct((B, A_WIDTH, S), bf), colT(A_WIDTH)),
        (jax.ShapeDtypeStruct((B, A_WIDTH, S), bf), colT(A_WIDTH)),
        (jax.ShapeDtypeStruct((B, IDX_WIDTH, S), bf), colT(IDX_WIDTH)),
        (jax.ShapeDtypeStruct((B, IDX_HEADS, S), f32), colT(IDX_HEADS)),
        (jax.ShapeDtypeStruct((T, B_WIDTH), f32), row(B_WIDTH)),
        (jax.ShapeDtypeStruct((T, B_WIDTH), f32), row(B_WIDTH)),
        (jax.ShapeDtypeStruct((T, B_WIDTH), f32), row(B_WIDTH)),
        (jax.ShapeDtypeStruct((T, B_WIDTH), f32), row(B_WIDTH)),
        (jax.ShapeDtypeStruct((T, D), bf), row(D)),
        (jax.ShapeDtypeStruct((T, D), bf), row(D)),
    ]
    ins = [x2, gain.reshape(1, D), wk, wik, wqT, wvT, wiqT, wiwT, wb, wg]
    in_specs = [row(D)] + [full(a) for a in ins[1:]]
    return pl.pallas_call(
        _inproj_kernel,
        grid=(B, nS),
        in_specs=in_specs,
        out_specs=[s for _, s in outs],
        out_shape=[s for s, _ in outs],
        compiler_params=_cparams(("parallel", "parallel")),
        name="inproj",
    )(*ins)


def _t5_bucket_table(n):
    d = np.arange(n)
    max_exact = REL_BUCKETS // 2
    nf = np.maximum(d, 1).astype(np.float64)
    large = max_exact + (np.log(nf / max_exact) / math.log(REL_MAX_DIST / max_exact)
                         * (REL_BUCKETS - max_exact)).astype(np.int32)
    large = np.minimum(large, REL_BUCKETS - 1)
    return np.where(d < max_exact, d, large)


def _dsa_kernel(qT_ref, k_ref, vT_ref, iqT_ref, iwT_ref, ik_ref, enear_ref,
                o_ref, sc_ref, qh_scr, m_scr, acc_scr, *, topk):
    TQ = qT_ref.shape[2]
    KC = TQ
    i = pl.program_id(1)
    nch = i + 1
    q0 = i * TQ
    f32 = jnp.float32
    bf16 = jnp.bfloat16
    key_id = lax.broadcasted_iota(jnp.int32, (KC, TQ), 0)
    qry_id = lax.broadcasted_iota(jnp.int32, (KC, TQ), 1)

    def col_reduce(x, op):
        return op(_fold_rows(x, op), axis=0, keepdims=True)

    iw = iwT_ref[0]

    def score_chunk(c, carry):
        rmin, rmax = carry
        k0 = pl.multiple_of(c * KC, KC)
        ik = ik_ref[pl.ds(k0, KC), :]
        acc = jnp.zeros((KC, TQ), f32)
        for h in range(IDX_HEADS):
            sh = jnp.dot(ik, iqT_ref[0, h * IDX_DIM:(h + 1) * IDX_DIM, :], preferred_element_type=f32)
            acc = acc + jnp.maximum(sh, 0.0) * iw[h:h + 1, :]
        valid = (k0 + key_id) <= (q0 + qry_id)
        sc_ref[pl.ds(k0, KC), :] = jnp.where(valid, acc, MASK_NEG)
        rmin = jnp.minimum(rmin, _fold_rows(jnp.where(valid, acc, -MASK_NEG), jnp.min))
        rmax = jnp.maximum(rmax, _fold_rows(jnp.where(valid, acc, MASK_NEG), jnp.max))
        return rmin, rmax

    rmin8, rmax8 = lax.fori_loop(
        0, nch, score_chunk,
        (jnp.full((V7X_SUBLANES, TQ), -MASK_NEG, f32), jnp.full((V7X_SUBLANES, TQ), MASK_NEG, f32)))
    rmin = jnp.min(rmin8, axis=0, keepdims=True)
    rmax = jnp.max(rmax8, axis=0, keepdims=True)

    def count_where(pred_fn):
        def body(c, acc):
            k0 = pl.multiple_of(c * KC, KC)
            blk = sc_ref[pl.ds(k0, KC), :]
            return acc + _fold_rows(jnp.where(pred_fn(blk), 1.0, 0.0), jnp.sum)
        acc = lax.fori_loop(0, nch, body, jnp.zeros((V7X_SUBLANES, TQ), f32))
        return jnp.sum(acc, axis=0, keepdims=True)

    def band_min_max(lo, hi):
        def body(c, carry):
            bmin, bmax = carry
            k0 = pl.multiple_of(c * KC, KC)
            blk = sc_ref[pl.ds(k0, KC), :]
            bmin = jnp.minimum(bmin, _fold_rows(jnp.where(blk >= lo, blk, -MASK_NEG), jnp.min))
            bmax = jnp.maximum(bmax, _fold_rows(jnp.where(blk < hi, blk, MASK_NEG), jnp.max))
            return bmin, bmax
        bmin8, bmax8 = lax.fori_loop(
            0, nch, body,
            (jnp.full((V7X_SUBLANES, TQ), -MASK_NEG, f32), jnp.full((V7X_SUBLANES, TQ), MASK_NEG, f32)))
        return jnp.min(bmin8, axis=0, keepdims=True), jnp.max(bmax8, axis=0, keepdims=True)

    kf = float(topk)
    n_valid = (q0 + 1 + lax.broadcasted_iota(jnp.int32, (1, TQ), 1)).astype(f32)
    c_top = count_where(lambda blk: blk >= rmax)
    top_tie = c_top >= kf
    lo0 = jnp.where(top_tie, rmax, rmin)
    cnt0 = jnp.where(top_tie, c_top, n_valid)
    done0 = jnp.where(top_tie | (cnt0 <= kf), 1.0, 0.0)

    def probe(st, lo_s, mid, tie):
        it, lo, hi, cnt, done = st
        active = done < 0.5
        lo_s = jnp.where(active, lo_s, lo)
        c = count_where(lambda blk: blk >= mid)
        feas = c >= kf
        move = active & jnp.logical_not(tie)
        lo_n = jnp.where(move & feas, mid, lo_s)
        cnt_n = jnp.where(move & feas, c, cnt)
        hi_n = jnp.where(move & jnp.logical_not(feas), mid, hi)
        done_n = jnp.where((active & tie) | (cnt_n <= kf), 1.0, done)
        return it + 1, lo_n, hi_n, cnt_n, done_n

    def halve(st):
        _, lo, hi, _, _ = st
        half = lo + 0.5 * (hi - lo)
        stuck = (half <= lo) | (half >= hi)
        return probe(st, lo, half, stuck)

    def snap(st):
        _, lo, hi, _, _ = st
        bmin, bmax = band_min_max(lo, hi)
        mid = bmin + 0.5 * (bmax - bmin)
        return probe(st, bmin, jnp.where(mid <= bmin, bmax, mid), bmax <= bmin)

    st = lax.while_loop(lambda st: (jnp.min(st[-1]) < 0.5) & (st[0] < BISECT_FAST_ITERS), halve,
                        (jnp.int32(0), lo0, rmax, cnt0, done0))
    _, thr, _, cnt_thr, _ = lax.while_loop(lambda st: jnp.min(st[-1]) < 0.5, snap, st)

    tie_overflow = jnp.max(cnt_thr) > kf

    @pl.when(jnp.logical_not(tie_overflow))
    def _():
        def mask_chunk(c, _):
            k0 = pl.multiple_of(c * KC, KC)
            sc_ref[pl.ds(k0, KC), :] = jnp.where(sc_ref[pl.ds(k0, KC), :] >= thr, 0.0, MASK_NEG)
            return 0
        lax.fori_loop(0, nch, mask_chunk, 0)

    @pl.when(tie_overflow)
    def _():
        need = kf - count_where(lambda blk: blk > thr)
        tril = jnp.where(lax.broadcasted_iota(jnp.int32, (KC, KC), 1)
                         <= lax.broadcasted_iota(jnp.int32, (KC, KC), 0), 1.0, 0.0).astype(bf16)

        def mask_chunk(c, run):
            k0 = pl.multiple_of(c * KC, KC)
            blk = sc_ref[pl.ds(k0, KC), :]
            eq = jnp.where(blk == thr, 1.0, 0.0)
            pref = jnp.dot(tril, eq.astype(bf16), preferred_element_type=f32)
            sel = (blk > thr) | ((eq > 0.5) & (run + pref <= need))
            sc_ref[pl.ds(k0, KC), :] = jnp.where(sel, 0.0, MASK_NEG)
            return run + pref[KC - 1:KC, :]

        lax.fori_loop(0, nch, mask_chunk, jnp.zeros((1, TQ), f32))

    AK = min(ATT_KC, TQ)
    per = TQ // AK
    head0_q = (lax.broadcasted_iota(jnp.int32, (V7X_LANES, TQ), 0) // A_HEAD_DIM) == 0
    n_pairs = A_HEADS // 2

    m_scr[...] = jnp.full(m_scr.shape, MASK_NEG, f32)
    acc_scr[...] = jnp.zeros(acc_scr.shape, f32)
    v_row = lax.broadcasted_iota(jnp.int32, (V7X_LANES, AK), 0)
    denom_row = [A_HEAD_DIM * (1 - sub) for sub in range(2)]
    for p in range(n_pairs):
        q_pair = qT_ref[0, p * V7X_LANES:(p + 1) * V7X_LANES, :]
        zq = jnp.zeros_like(q_pair)
        qh_scr[2 * p] = jnp.where(head0_q, q_pair, zq)
        qh_scr[2 * p + 1] = jnp.where(head0_q, zq, q_pair)

    def step(c, bias_rows):
        k0 = pl.multiple_of(c * AK, AK)
        msk = sc_ref[pl.ds(k0, AK), :]
        for p in range(n_pairs):
            kp = k_ref[pl.ds(k0, AK), p * V7X_LANES:(p + 1) * V7X_LANES]
            vp = vT_ref[0, p * V7X_LANES:(p + 1) * V7X_LANES, pl.ds(k0, AK)]
            for sub in range(2):
                h = 2 * p + sub
                s = jnp.dot(kp, qh_scr[h], preferred_element_type=f32) + msk
                if bias_rows is not None:
                    s = s + enear_ref[h, bias_rows, :]
                m = m_scr[h:h + 1, :]
                m_new = jnp.maximum(m, col_reduce(s, jnp.max))
                alpha = jnp.exp2(m - m_new)
                pr = jnp.exp2(s - m_new)
                m_scr[h:h + 1, :] = m_new
                v_aug = jnp.where(v_row == denom_row[sub], jnp.ones_like(vp), vp)
                acc_scr[h] = alpha * acc_scr[h] + jnp.dot(v_aug, pr.astype(bf16), preferred_element_type=f32)

    def far(blk, _):
        for jj in range(per):
            step(blk * per + jj, None)
        return 0

    def near(block, first_chunk):
        for jj in range(per):
            step(first_chunk + jj, slice(block * TQ + jj * AK, block * TQ + (jj + 1) * AK))

    lax.fori_loop(0, jnp.maximum(i - 1, 0), far, 0)

    @pl.when(i >= 1)
    def _():
        near(0, (i - 1) * per)

    near(1, i * per)
    for p in range(n_pairs):
        outs = [acc_scr[2 * p + sub] / acc_scr[2 * p + sub, denom_row[sub]:denom_row[sub] + 1, :]
                for sub in range(2)]
        o_pair = jnp.where(head0_q, outs[0], outs[1])
        o_ref[:, p * V7X_LANES:(p + 1) * V7X_LANES] = o_pair.T.astype(o_ref.dtype)


def _dsa(k, ik, qT, vT, iqT, iwT, rel_bias, B, S):
    T = k.shape[0]
    TQ = min(ATT_Q, S)
    nQ = S // TQ
    topk = min(TOPK_MAX, S // 4)
    buckets = _t5_bucket_table(2 * TQ + 1)
    assert np.all(_t5_bucket_table(S + 1)[TQ + 1:] == REL_BUCKETS - 1)
    j = np.arange(2 * TQ)[:, None]
    r = np.arange(TQ)[None, :]
    dist = np.maximum(r + TQ - j, 0)
    onehot = (jnp.asarray(buckets[dist], jnp.int32)[None]
              == jnp.arange(REL_BUCKETS, dtype=jnp.int32)[:, None, None]).astype(jnp.float32)
    rel = (rel_bias.astype(jnp.float32) - rel_bias[REL_BUCKETS - 1].astype(jnp.float32)[None, :]) * LOG2_E
    enear = jnp.einsum('nh,njr->hjr', rel, onehot, precision=lax.Precision.HIGHEST)

    return pl.pallas_call(
        functools.partial(_dsa_kernel, topk=topk),
        grid=(B, nQ),
        in_specs=[
            pl.BlockSpec((1, A_WIDTH, TQ), lambda b, i: (b, 0, i)),
            pl.BlockSpec((S, A_WIDTH), lambda b, i: (b, 0)),
            pl.BlockSpec((1, A_WIDTH, S), lambda b, i: (b, 0, 0)),
            pl.BlockSpec((1, IDX_WIDTH, TQ), lambda b, i: (b, 0, i)),
            pl.BlockSpec((1, IDX_HEADS, TQ), lambda b, i: (b, 0, i)),
            pl.BlockSpec((S, IDX_DIM), lambda b, i: (b, 0)),
            pl.BlockSpec((A_HEADS, 2 * TQ, TQ), lambda b, i: (0, 0, 0)),
        ],
        out_specs=pl.BlockSpec((TQ, A_WIDTH), lambda b, i: (b * nQ + i, 0)),
        scratch_shapes=[pltpu.VMEM((S, TQ), jnp.float32),
                        pltpu.VMEM((A_HEADS, V7X_LANES, TQ), jnp.bfloat16),
                        pltpu.VMEM((A_HEADS, TQ), jnp.float32),
                        pltpu.VMEM((A_HEADS, V7X_LANES, TQ), jnp.float32)],
        out_shape=jax.ShapeDtypeStruct((T, A_WIDTH), jnp.bfloat16),
        compiler_params=_cparams(("parallel", "arbitrary")),
        name="dsa",
    )(qT, k, vT, iqT, iwT, ik, enear)


def _hgrn_kernel(bq_ref, bf_ref, bi_ref, bg_ref, lb_ref, gain_ref, o_ref,
                 st_ref, b_scr, q_scr, k_scr, v_scr, oi_scr, qd_s, kd_s, kl_s, vv_s, dec_s, oi_s, upd_s, st_s):
    R = bq_ref.shape[0]
    C = HGRN_CHUNK
    nC = R // C
    f32 = jnp.float32
    bf16 = jnp.bfloat16
    h = pl.program_id(1)

    @pl.when(pl.program_id(2) == 0)
    def _():
        st_ref[...] = jnp.zeros_like(st_ref)

    lb = lb_ref[pl.ds(h, 1), :]
    gain = gain_ref[pl.ds(h, 1), :]
    tril_incl = jnp.where(lax.broadcasted_iota(jnp.int32, (C, C), 1)
                          <= lax.broadcasted_iota(jnp.int32, (C, C), 0), 1.0, 0.0)
    srow = lax.broadcasted_iota(jnp.int32, (C, B_KEY_DIM), 0)

    def gates(r0):
        f = lb + (1.0 - lb) * _sigmoid(bf_ref[pl.ds(r0, C), :])
        qr = bq_ref[pl.ds(r0, C), :]
        return jnp.log(f), 1.0 - f, qr * _sigmoid(qr) * (B_KEY_DIM ** -0.5), bi_ref[pl.ds(r0, C), :]

    def cumdecay(g):
        tri = tril_incl.astype(bf16)
        g_hi = g.astype(bf16)
        rest = g - g_hi.astype(f32)
        g_mid = rest.astype(bf16)
        g_lo = (rest - g_mid.astype(f32)).astype(bf16)
        return (jnp.dot(tri, g_hi, preferred_element_type=f32) + jnp.dot(tri, g_mid, preferred_element_type=f32)
                + jnp.dot(tri, g_lo, preferred_element_type=f32))

    def advance(r0, st, qd, o_intra, upd, decay_row):
        o_inter = lax.dot_general(qd, st.astype(bf16), (((1,), (1,)), ((), ())), preferred_element_type=f32)
        og = bg_ref[pl.ds(r0, C), :]
        y = _rms(o_inter + o_intra, gain) * (og * _sigmoid(og))
        o_ref[pl.ds(r0, C), :] = y.astype(o_ref.dtype)
        return st * decay_row + upd

    f_all = lb + (1.0 - lb) * _sigmoid(bf_ref[...])
    g_all = jnp.log(f_all)
    decay = jnp.sum(g_all.reshape(nC, C, B_KEY_DIM), axis=1)
    safe = jnp.min(decay) >= -HGRN_SAFE_DECAY

    @pl.when(safe)
    def _():
        qr = bq_ref[...]
        qq = qr * _sigmoid(qr) * (B_KEY_DIM ** -0.5)
        kk = 1.0 - f_all
        b = jnp.concatenate([cumdecay(g_all[c * C:(c + 1) * C]) for c in range(nC)], axis=0)
        b_end = jnp.concatenate([jnp.broadcast_to(b[(c + 1) * C - 1:(c + 1) * C], (C, B_KEY_DIM))
                                 for c in range(nC)], axis=0)
        qd_s[...] = (qq * jnp.exp(b)).astype(bf16)
        kd_s[...] = (kk * jnp.exp(-b)).astype(bf16)
        kl_s[...] = (kk * jnp.exp(b_end - b)).astype(bf16)
        vv_s[...] = bi_ref[...].astype(bf16)
        dec_s[...] = jnp.exp(b_end)
        for c in range(nC):
            rows = slice(c * C, (c + 1) * C)
            att = lax.dot_general(qd_s[rows], kd_s[rows], (((1,), (1,)), ((), ())),
                                  preferred_element_type=f32) * tril_incl
            oi_s[rows] = jnp.dot(att.astype(bf16), vv_s[rows], preferred_element_type=f32)
            upd_s[c] = lax.dot_general(vv_s[rows], kl_s[rows], (((0,), (0,)), ((), ())),
                                       preferred_element_type=f32)
        st = st_ref[...]
        for c in range(nC):
            st_s[c] = st.astype(bf16)
            st = st * dec_s[c * C:c * C + 1] + upd_s[c]
        st_ref[...] = st
        for c in range(nC):
            rows = slice(c * C, (c + 1) * C)
            oi_s[rows] = oi_s[rows] + lax.dot_general(qd_s[rows], st_s[c], (((1,), (1,)), ((), ())),
                                                      preferred_element_type=f32)
        og = bg_ref[...]
        o_ref[...] = (_rms(oi_s[...], gain) * (og * _sigmoid(og))).astype(o_ref.dtype)

    @pl.when(jnp.logical_not(safe))
    def _():
        def body(c, st):
            r0 = pl.multiple_of(c * C, C)
            g, kk, qq, vv = gates(r0)
            b = cumdecay(g)
            b_last = b[C - 1:C, :]
            b_scr[...] = b
            q_scr[...] = qq
            k_scr[...] = kk
            v_scr[...] = vv

            def row(t, _):
                bt = b_scr[pl.ds(t, 1), :]
                qt = q_scr[pl.ds(t, 1), :]
                ex = jnp.where(srow <= t, bt - b_scr[...], -jnp.inf)
                a = jnp.sum(qt * k_scr[...] * jnp.exp(ex), axis=1, keepdims=True)
                oi_scr[pl.ds(t, 1), :] = jnp.sum(a * v_scr[...], axis=0, keepdims=True)
                return 0
            lax.fori_loop(0, C, row, 0)
            kd_last = (kk * jnp.exp(b_last - b)).astype(bf16)
            upd = lax.dot_general(vv.astype(bf16), kd_last, (((0,), (0,)), ((), ())),
                                  preferred_element_type=f32)
            return advance(r0, st, (qq * jnp.exp(b)).astype(bf16), oi_scr[...], upd, jnp.exp(b_last))
        st_ref[...] = lax.fori_loop(0, nC, body, st_ref[...])


def _hgrn(bq, bf, bi, bg, lb, gain, B, S):
    T = bq.shape[0]
    R = min(HGRN_ROWS, S)
    nR = S // R
    C = HGRN_CHUNK
    blk = pl.BlockSpec((R, B_KEY_DIM), lambda b, h, c: (b * nR + c, h))
    small = pl.BlockSpec((B_HEADS, B_KEY_DIM), lambda b, h, c: (0, 0))
    f32 = jnp.float32
    return pl.pallas_call(
        _hgrn_kernel,
        grid=(B, B_HEADS, nR),
        in_specs=[blk, blk, blk, blk, small, small],
        out_specs=blk,
        out_shape=jax.ShapeDtypeStruct((T, B_WIDTH), jnp.bfloat16),
        scratch_shapes=[pltpu.VMEM((B_VAL_DIM, B_KEY_DIM), f32)] +
                       [pltpu.VMEM((C, B_KEY_DIM), f32) for _ in range(5)] +
                       [pltpu.VMEM((R, B_KEY_DIM), jnp.bfloat16) for _ in range(4)] +
                       [pltpu.VMEM((R, B_KEY_DIM), f32) for _ in range(2)] +
                       [pltpu.VMEM((R // C, B_VAL_DIM, B_KEY_DIM), f32),
                        pltpu.VMEM((R // C, B_VAL_DIM, B_KEY_DIM), jnp.bfloat16)],
        compiler_params=_cparams(("parallel", "parallel", "arbitrary")),
        name="hgrn",
    )(bq, bf, bi, bg, lb, gain)


def _merge_kernel(x_ref, ya_ref, yb_ref, ga_ref, gb_ref, wa_ref, wb_ref, wo_ref, g_ref, wrh_ref, wrl_ref, br_ref,
                  x1_ref, xn_ref, lg_ref):
    f32 = jnp.float32
    ma = jnp.dot(ya_ref[...], wa_ref[...], preferred_element_type=f32)
    mb = jnp.dot(yb_ref[...], wb_ref[...], preferred_element_type=f32)
    merged = ga_ref[...].astype(f32) * ma + gb_ref[...].astype(f32) * mb
    x1 = x_ref[...] + jnp.dot(merged.astype(jnp.bfloat16), wo_ref[...], preferred_element_type=f32)
    x1_ref[...] = x1
    hn = _rms(x1, g_ref[...])
    xn_ref[...] = _pack_bf16_pairs(hn)
    hn_hi = hn.astype(jnp.bfloat16)
    hn_lo = (hn - hn_hi.astype(f32)).astype(jnp.bfloat16)
    lg_ref[...] = (jnp.dot(hn_hi, wrh_ref[...], preferred_element_type=f32)
                   + jnp.dot(hn_lo, wrh_ref[...], preferred_element_type=f32)
                   + jnp.dot(hn_hi, wrl_ref[...], preferred_element_type=f32) + br_ref[...])


def _merge(x2, ya, yb, ga, gb, w_up_a, w_up_b, w_out, gain, w_router, b_router):
    T, D = x2.shape
    R = min(PROJ_ROWS, T)
    bf = jnp.bfloat16
    wr_hi = w_router.astype(bf)
    wr_lo = (w_router - wr_hi.astype(jnp.float32)).astype(bf)
    ins = [x2, ya, yb, ga, gb, w_up_a.astype(bf), w_up_b.astype(bf), w_out.astype(bf),
           gain.reshape(1, D), wr_hi, wr_lo, b_router.reshape(1, N_EXPERTS)]
    row = lambda n: pl.BlockSpec((R, n), lambda i: (i, 0))
    full = lambda a: pl.BlockSpec(a.shape, lambda i: (0,) * a.ndim)
    in_specs = [row(D), row(A_WIDTH), row(B_WIDTH), row(D), row(D)] + [full(a) for a in ins[5:]]
    return pl.pallas_call(
        _merge_kernel,
        grid=(T // R,),
        in_specs=in_specs,
        out_specs=[row(D), row(D // 2), row(N_EXPERTS)],
        out_shape=[jax.ShapeDtypeStruct((T, D), jnp.float32), jax.ShapeDtypeStruct((T, D // 2), jnp.uint32),
                   jax.ShapeDtypeStruct((T, N_EXPERTS), jnp.float32)],
        compiler_params=_cparams(("parallel",)),
        name="merge",
    )(*ins)


def _route_kernel(lg_ref, eidx_ref, gate_ref, rank_ref, cnt_ref, run_ref):
    R = lg_ref.shape[0]
    f32 = jnp.float32

    @pl.when(pl.program_id(0) == 0)
    def _():
        run_ref[...] = jnp.zeros_like(run_ref)

    lg = lg_ref[...]
    lane = lax.broadcasted_iota(jnp.int32, (R, N_EXPERTS), 1)
    work = lg
    onehots, vals, idxs = [], [], []
    for _ in range(TOP_K):
        m = jnp.max(work, axis=1, keepdims=True)
        idx = jnp.min(jnp.where(work == m, lane, N_EXPERTS), axis=1, keepdims=True)
        oh = lane == idx
        onehots.append(oh)
        vals.append(m)
        idxs.append(idx)
        work = jnp.where(oh, -jnp.inf, work)
    ex = [jnp.exp(v - vals[0]) for v in vals]
    den = ex[0] + ex[1] + ex[2] + ex[3]
    chosen = jnp.where(onehots[0] | onehots[1] | onehots[2] | onehots[3], 1.0, 0.0)
    strict = jnp.where(lax.broadcasted_iota(jnp.int32, (R, R), 1)
                       < lax.broadcasted_iota(jnp.int32, (R, R), 0), 1.0, 0.0).astype(jnp.bfloat16)
    before = jnp.dot(strict, chosen.astype(jnp.bfloat16), preferred_element_type=f32) + run_ref[...]
    lane4 = lax.broadcasted_iota(jnp.int32, (R, TOP_K), 1)
    eidx = jnp.zeros((R, TOP_K), jnp.int32)
    gate = jnp.zeros((R, TOP_K), f32)
    rank = jnp.zeros((R, TOP_K), f32)
    for k in range(TOP_K):
        eidx = jnp.where(lane4 == k, idxs[k], eidx)
        gate = jnp.where(lane4 == k, ex[k] / den, gate)
        rk = jnp.sum(jnp.where(onehots[k], before, 0.0), axis=1, keepdims=True)
        rank = jnp.where(lane4 == k, rk, rank)
    eidx_ref[...] = eidx
    gate_ref[...] = gate
    rank_ref[...] = rank.astype(jnp.int32)
    run_ref[...] = run_ref[...] + jnp.sum(chosen, axis=0, keepdims=True)
    cnt_ref[...] = run_ref[...].astype(jnp.int32)


def _route(logits):
    T = logits.shape[0]
    R = min(ROUTE_ROWS, T)
    row = lambda n: pl.BlockSpec((R, n), lambda i: (i, 0))
    return pl.pallas_call(
        _route_kernel,
        grid=(T // R,),
        in_specs=[row(N_EXPERTS)],
        out_specs=[row(TOP_K), row(TOP_K), row(TOP_K), pl.BlockSpec((1, N_EXPERTS), lambda i: (0, 0))],
        out_shape=[jax.ShapeDtypeStruct((T, TOP_K), jnp.int32), jax.ShapeDtypeStruct((T, TOP_K), jnp.float32),
                   jax.ShapeDtypeStruct((T, TOP_K), jnp.int32), jax.ShapeDtypeStruct((1, N_EXPERTS), jnp.int32)],
        scratch_shapes=[pltpu.VMEM((1, N_EXPERTS), jnp.float32)],
        compiler_params=_cparams(("arbitrary",)),
        name="route",
    )(logits)


def _sc_gather_rows(table, idx):
    N, W = table.shape
    M = idx.shape[0]
    workers = V7X_SC_CORES * V7X_SC_SUBCORES
    per_worker = M // workers
    pieces = per_worker // SC_GATHER_ROWS
    assert per_worker * workers == M and pieces * SC_GATHER_ROWS == per_worker and pieces % 2 == 0
    mesh = plsc.VectorSubcoreMesh(core_axis_name="core", subcore_axis_name="subcore",
                                  num_cores=V7X_SC_CORES, num_subcores=V7X_SC_SUBCORES)

    @functools.partial(
        pl.kernel, mesh=mesh,
        out_type=jax.ShapeDtypeStruct((M, W), table.dtype),
        scratch_types=[pltpu.VMEM((per_worker,), jnp.int32),
                       pltpu.VMEM((SC_GATHER_ROWS, W), table.dtype),
                       pltpu.VMEM((SC_GATHER_ROWS, W), table.dtype),
                       pltpu.SemaphoreType.DMA, pltpu.SemaphoreType.DMA],
    )
    def gather(table_hbm, idx_hbm, out_hbm, idx_v, rows_a, rows_b, sem_a, sem_b):
        worker = lax.axis_index("subcore") * V7X_SC_CORES + lax.axis_index("core")
        base = pl.multiple_of(worker * per_worker, SC_GATHER_ROWS)
        pltpu.sync_copy(idx_hbm.at[pl.ds(base, per_worker)], idx_v)
        bufs = ((rows_a, sem_a), (rows_b, sem_b))

        def fetch(g, buf, sem):
            off = pl.multiple_of(g * SC_GATHER_ROWS, SC_GATHER_ROWS)
            return pltpu.make_async_copy(table_hbm.at[idx_v.at[pl.ds(off, SC_GATHER_ROWS)]], buf, sem)

        fetch(0, *bufs[0]).start()

        @pl.loop(0, pieces, step=2)
        def _(g0):
            for half in range(2):
                g = g0 + half
                buf, sem = bufs[half]
                fetch(g, buf, sem).wait()

                @pl.when(g + 1 < pieces)
                def _():
                    fetch(g + 1, *bufs[1 - half]).start()

                off = pl.multiple_of(g * SC_GATHER_ROWS, SC_GATHER_ROWS)
                pltpu.sync_copy(buf, out_hbm.at[pl.ds(base + off, SC_GATHER_ROWS)])

    return gather(table, idx)


def _sc_scatter_rows(rows, dest, n_out):
    T, W = rows.shape
    slots = dest.shape[1]
    workers = V7X_SC_CORES * V7X_SC_SUBCORES
    per_worker = T // workers
    pieces = per_worker // SC_GATHER_ROWS
    assert per_worker * workers == T and pieces * SC_GATHER_ROWS == per_worker and pieces % 2 == 0
    idx = dest.reshape(workers, pieces, SC_GATHER_ROWS, slots).transpose(0, 1, 3, 2)
    mesh = plsc.VectorSubcoreMesh(core_axis_name="core", subcore_axis_name="subcore",
                                  num_cores=V7X_SC_CORES, num_subcores=V7X_SC_SUBCORES)

    @functools.partial(
        pl.kernel, mesh=mesh,
        out_type=jax.ShapeDtypeStruct((n_out, W), rows.dtype),
        scratch_types=[pltpu.VMEM((pieces, slots, SC_GATHER_ROWS), jnp.int32),
                       pltpu.VMEM((SC_GATHER_ROWS, W), rows.dtype),
                       pltpu.VMEM((SC_GATHER_ROWS, W), rows.dtype),
                       pltpu.SemaphoreType.DMA, pltpu.SemaphoreType.DMA, pltpu.SemaphoreType.DMA],
    )
    def scatter(rows_hbm, idx_hbm, out_hbm, idx_v, rows_a, rows_b, sem_a, sem_b, sem_out):
        worker = lax.axis_index("subcore") * V7X_SC_CORES + lax.axis_index("core")
        base = pl.multiple_of(worker * per_worker, SC_GATHER_ROWS)
        pltpu.sync_copy(idx_hbm.at[worker], idx_v)
        bufs = ((rows_a, sem_a), (rows_b, sem_b))

        def fetch(g, buf, sem):
            off = pl.multiple_of(g * SC_GATHER_ROWS, SC_GATHER_ROWS)
            return pltpu.make_async_copy(rows_hbm.at[pl.ds(base + off, SC_GATHER_ROWS)], buf, sem)

        fetch(0, *bufs[0]).start()

        @pl.loop(0, pieces, step=2)
        def _(g0):
            for half in range(2):
                g = g0 + half
                buf, sem = bufs[half]
                fetch(g, buf, sem).wait()

                @pl.when(g + 1 < pieces)
                def _():
                    fetch(g + 1, *bufs[1 - half]).start()

                puts = [pltpu.make_async_copy(buf, out_hbm.at[idx_v.at[g, k]], sem_out) for k in range(slots)]
                for put in puts:
                    put.start()
                for put in puts:
                    put.wait()

    return scatter(rows, idx)


def _experts_kernel(be_ref, nb_ref, x_ref, wgu_ref, bgu_ref, wd_ref, bd_ref, o_ref, wgu_bf, wd_bf):
    f32 = jnp.float32
    d_ff = wd_ref.shape[1]
    i = pl.program_id(0)
    live = i < nb_ref[0]

    @pl.when(live & ((i == 0) | (be_ref[i] != be_ref[jnp.maximum(i - 1, 0)])))
    def _():
        wgu_bf[...] = wgu_ref[0].astype(jnp.bfloat16)
        wd_bf[...] = wd_ref[0].astype(jnp.bfloat16)

    @pl.when(live)
    def _():
        x_hi, x_lo = _unpack_bf16_pairs(x_ref[...])
        half = x_hi.shape[1]
        gu = (jnp.dot(x_hi, wgu_bf[:half, :], preferred_element_type=f32)
              + jnp.dot(x_lo, wgu_bf[half:, :], preferred_element_type=f32) + bgu_ref[0])
        gate = jnp.minimum(gu[:, :d_ff], SWIGLU_LIMIT)
        lin = jnp.clip(gu[:, d_ff:], -SWIGLU_LIMIT, SWIGLU_LIMIT)
        act = (lin + 1.0) * gate * _sigmoid(SWIGLU_ALPHA * gate)
        y = jnp.dot(act.astype(jnp.bfloat16), wd_bf[...], preferred_element_type=f32) + bd_ref[0]
        o_ref[...] = _pack_bf16_pairs(y)

    @pl.when(pl.program_id(0) >= nb_ref[0])
    def _():
        o_ref[...] = jnp.zeros_like(o_ref)


def _experts(xs, block_expert, n_used, w_gu, b_gu, w_down, b_down):
    P, W = xs.shape
    E, D, F2 = w_gu.shape
    nb = P // EXPERT_ROWS
    grid_spec = pltpu.PrefetchScalarGridSpec(
        num_scalar_prefetch=2,
        grid=(nb,),
        in_specs=[
            pl.BlockSpec((EXPERT_ROWS, W), lambda i, be, nu: (i, 0)),
            pl.BlockSpec((1, D, F2), lambda i, be, nu: (be[i], 0, 0)),
            pl.BlockSpec((1, 1, F2), lambda i, be, nu: (be[i], 0, 0)),
            pl.BlockSpec((1, F2 // 2, D), lambda i, be, nu: (be[i], 0, 0)),
            pl.BlockSpec((1, 1, D), lambda i, be, nu: (be[i], 0, 0)),
        ],
        out_specs=pl.BlockSpec((EXPERT_ROWS, W), lambda i, be, nu: (i, 0)),
        scratch_shapes=[pltpu.VMEM((D, F2), jnp.bfloat16), pltpu.VMEM((F2 // 2, D), jnp.bfloat16)],
    )
    return pl.pallas_call(
        _experts_kernel,
        grid_spec=grid_spec,
        out_shape=jax.ShapeDtypeStruct((P, W), jnp.uint32),
        compiler_params=_cparams(("arbitrary",)),
        name="experts",
    )(block_expert, n_used, xs, w_gu, b_gu.reshape(E, 1, F2), w_down, b_down.reshape(E, 1, D))


def _combine_kernel(ya_ref, x1_ref, gate_ref, g_ref, o_ref):
    half = x1_ref.shape[1] // 2
    f32 = jnp.float32
    gate = gate_ref[...]
    x1 = x1_ref[...]
    y_hi = x1[:, :half]
    y_lo = x1[:, half:]
    for k in range(TOP_K):
        hi, lo = _unpack_bf16_pairs(ya_ref[k])
        y_hi = y_hi + gate[:, k:k + 1] * hi.astype(f32)
        y_lo = y_lo + gate[:, k:k + 1] * lo.astype(f32)
    o_ref[...] = _rms(jnp.concatenate([y_hi, y_lo], axis=1), g_ref[...])


def _combine(ya, x1, gates, gain):
    T, D = x1.shape
    R = min(COMBINE_ROWS, T)
    row = lambda w: pl.BlockSpec((R, w), lambda i: (i, 0))
    return pl.pallas_call(
        _combine_kernel,
        grid=(T // R,),
        in_specs=[pl.BlockSpec((TOP_K, R, D // 2), lambda i: (0, i, 0)), row(D), row(TOP_K),
                  pl.BlockSpec((1, D), lambda i: (0, 0))],
        out_specs=row(D),
        out_shape=jax.ShapeDtypeStruct((T, D), jnp.float32),
        compiler_params=_cparams(("parallel",)),
        name="combine",
    )(ya, x1, gates, gain.reshape(1, D))


def _moe_plan(eidx, rank, counts, A):
    counts = counts.reshape(N_EXPERTS)
    padded = (counts + EXPERT_ROWS - 1) // EXPERT_ROWS * EXPERT_ROWS
    pad_ends = jnp.cumsum(padded)
    pad_starts = pad_ends - padded
    n_blocks = -(-A // EXPERT_ROWS) + N_EXPERTS
    dest = pad_starts[eidx] + rank
    block_start = jnp.arange(n_blocks, dtype=pad_ends.dtype) * EXPERT_ROWS
    block_expert = jnp.minimum(jnp.sum(pad_ends[None, :] <= block_start[:, None], axis=1),
                               N_EXPERTS - 1).astype(jnp.int32)
    n_used = (pad_ends[-1] // EXPERT_ROWS).astype(jnp.int32).reshape(1)
    return dest.astype(jnp.int32), block_expert, n_used, n_blocks


def kernel(x, w_in, w_up_a, w_up_b, w_out, norm_mix, norm_ffn, norm_final, hgrn_norm,
           lb_logits, rel_bias, w_router, b_router, w_gu, b_gu, w_down, b_down):
    B, S, D = x.shape
    T = B * S
    assert w_in.shape[0] == 1, "the final rmsnorm is fused into the single layer's combine stage"
    lb_all = jnp.cumsum(jax.nn.softmax(lb_logits.astype(jnp.float32), axis=0), axis=0)
    x2 = x.reshape(T, D)
    (k, ik, qT, vT, iqT, iwT, bq, bf, bi, bg, ga, gb) = _inproj(x2, norm_mix[0], w_in[0], B, S)
    ya = _dsa(k, ik, qT, vT, iqT, iwT, rel_bias, B, S)
    yb = _hgrn(bq, bf, bi, bg, lb_all[0].reshape(B_HEADS, B_KEY_DIM), hgrn_norm[0], B, S)
    x1, xn, logits = _merge(x2, ya, yb, ga, gb, w_up_a[0], w_up_b[0], w_out[0], norm_ffn[0],
                            w_router[0], b_router[0])
    eidx, gates, rank, counts = _route(logits)
    dest, block_expert, n_used, n_blocks = _moe_plan(eidx, rank, counts, T * TOP_K)
    P = n_blocks * EXPERT_ROWS
    A = T * TOP_K
    xs = _sc_scatter_rows(xn, dest, P)
    y_buf = _experts(xs, block_expert, n_used, w_gu[0], b_gu[0], w_down[0], b_down[0])
    ya = _sc_gather_rows(y_buf, dest.T.reshape(A)).reshape(TOP_K, T, D // 2)
    out = _combine(ya, x1, gates, norm_final)
    return out.reshape(B, S, D)
```

```python
import functools
import math

import numpy as np
import jax
import jax.numpy as jnp
from jax import lax
from jax.experimental import pallas as pl
from jax.experimental.pallas import tpu as pltpu
from jax.experimental.pallas import tpu_sc as plsc

A_HEADS = 8
A_HEAD_DIM = 64
IDX_HEADS = 8
IDX_DIM = 32
TOPK_MAX = 256
REL_BUCKETS = 32
REL_MAX_DIST = 128
B_HEADS = 4
B_KEY_DIM = 128
B_VAL_DIM = 128
N_EXPERTS = 32
TOP_K = 4
SWIGLU_LIMIT = 7.0
SWIGLU_ALPHA = 1.702
EPS = 1e-6
LOG2_E = math.log2(math.e)

A_WIDTH = A_HEADS * A_HEAD_DIM
B_WIDTH = B_HEADS * B_VAL_DIM
IDX_WIDTH = IDX_HEADS * IDX_DIM

V7X_LANES = 128
V7X_SUBLANES = 8
V7X_VMEM_LIMIT_BYTES = 56 * 1024 * 1024
V7X_SC_CORES = 2
V7X_SC_SUBCORES = 16

PROJ_ROWS = 512
ATT_Q = 256
ATT_KC = 128
HGRN_ROWS = 512
HGRN_CHUNK = 64
HGRN_SAFE_DECAY = 70.0
ROUTE_ROWS = 512
EXPERT_ROWS = 256
COMBINE_ROWS = 512
SC_GATHER_ROWS = 64
MASK_NEG = -1e30
BISECT_FAST_ITERS = 26


def _cparams(dims):
    return pltpu.CompilerParams(dimension_semantics=dims, vmem_limit_bytes=V7X_VMEM_LIMIT_BYTES)


def _rms(x, gain):
    return x * lax.rsqrt(jnp.mean(x * x, axis=-1, keepdims=True) + EPS) * gain


def _sigmoid(x):
    return 1.0 / (1.0 + jnp.exp(-x))


def _pack_bf16_pairs(x):
    n = x.shape[1] // 2
    as_bits = lambda v: lax.bitcast_convert_type(v.astype(jnp.bfloat16).astype(jnp.float32), jnp.uint32)
    return (as_bits(x[:, :n]) & jnp.uint32(0xFFFF0000)) | (as_bits(x[:, n:]) >> 16)


def _unpack_bf16_pairs(w):
    hi = lax.bitcast_convert_type(w & jnp.uint32(0xFFFF0000), jnp.float32).astype(jnp.bfloat16)
    lo = lax.bitcast_convert_type(w << 16, jnp.float32).astype(jnp.bfloat16)
    return hi, lo


def _fold_rows(x, op):
    return op(x.reshape(x.shape[0] // V7X_SUBLANES, V7X_SUBLANES, x.shape[1]), axis=0)


def _inproj_kernel(x_ref, g_ref, wk_ref, wik_ref, wqT_ref, wvT_ref, wiqT_ref, wiwT_ref, wb_ref, wg_ref,
                   k_ref, ik_ref, qT_ref, vT_ref, iqT_ref, iwT_ref, bq_ref, bf_ref, bi_ref, bg_ref,
                   ga_ref, gb_ref):
    x = x_ref[...]
    hn = _rms(x, g_ref[...]).astype(jnp.bfloat16)

    def mm(w_ref):
        return jnp.dot(hn, w_ref[...], preferred_element_type=jnp.float32)

    def mm_t(w_ref):
        return lax.dot_general(w_ref[...], hn, (((1,), (1,)), ((), ())),
                               preferred_element_type=jnp.float32)

    k_ref[...] = mm(wk_ref).astype(jnp.bfloat16)
    ik_ref[...] = mm(wik_ref).astype(jnp.bfloat16)
    qT_ref[0] = (mm_t(wqT_ref) * (A_HEAD_DIM ** -0.5 * LOG2_E)).astype(jnp.bfloat16)
    vT_ref[0] = mm_t(wvT_ref).astype(jnp.bfloat16)
    iqT_ref[0] = mm_t(wiqT_ref).astype(jnp.bfloat16)
    iwT_ref[0] = mm_t(wiwT_ref) * ((IDX_HEADS * IDX_DIM) ** -0.5)
    hb = mm(wb_ref)
    bq_ref[...] = hb[:, 0 * B_WIDTH:1 * B_WIDTH]
    bf_ref[...] = hb[:, 1 * B_WIDTH:2 * B_WIDTH]
    bi_ref[...] = hb[:, 2 * B_WIDTH:3 * B_WIDTH]
    bg_ref[...] = hb[:, 3 * B_WIDTH:4 * B_WIDTH]
    d = ga_ref.shape[-1]
    hg = mm(wg_ref)
    ga_ref[...] = _sigmoid(hg[:, :d]).astype(jnp.bfloat16)
    gb_ref[...] = _sigmoid(hg[:, d:]).astype(jnp.bfloat16)


def _inproj(x2, gain, w_in, B, S):
    T, D = x2.shape
    R = min(PROJ_ROWS, S)
    nS = S // R
    o = np.cumsum((0, A_WIDTH, A_WIDTH, A_WIDTH, IDX_WIDTH, IDX_HEADS, IDX_DIM,
                   B_WIDTH, B_WIDTH, B_WIDTH, B_WIDTH, D, D))
    bf = jnp.bfloat16
    wqT = w_in[:, o[0]:o[1]].T.astype(bf)
    wk = w_in[:, o[1]:o[2]].astype(bf)
    wvT = w_in[:, o[2]:o[3]].T.astype(bf)
    wiqT = w_in[:, o[3]:o[4]].T.astype(bf)
    wiwT = w_in[:, o[4]:o[5]].T.astype(bf)
    wik = w_in[:, o[5]:o[6]].astype(bf)
    wb = w_in[:, o[6]:o[10]].astype(bf)
    wg = w_in[:, o[10]:o[12]].astype(bf)

    def full(a):
        return pl.BlockSpec(a.shape, lambda b, i: (0,) * a.ndim)

    row = lambda n: pl.BlockSpec((R, n), lambda b, i: (b * nS + i, 0))
    colT = lambda n: pl.BlockSpec((1, n, R), lambda b, i: (b, 0, i))
    f32 = jnp.float32
    outs = [
        (jax.ShapeDtypeStruct((T, A_WIDTH), bf), row(A_WIDTH)),
        (jax.ShapeDtypeStruct((T, IDX_DIM), bf), row(IDX_DIM)),
        (jax.ShapeDtypeStruct((B, A_WIDTH, S), bf), colT(A_WIDTH)),
        (jax.ShapeDtypeStruct((B, A_WIDTH, S), bf), colT(A_WIDTH)),
        (jax.ShapeDtypeStruct((B, IDX_WIDTH, S), bf), colT(IDX_WIDTH)),
        (jax.ShapeDtypeStruct((B, IDX_HEADS, S), f32), colT(IDX_HEADS)),
        (jax.ShapeDtypeStruct((T, B_WIDTH), f32), row(B_WIDTH)),
        (jax.ShapeDtypeStruct((T, B_WIDTH), f32), row(B_WIDTH)),
        (jax.ShapeDtypeStruct((T, B_WIDTH), f32), row(B_WIDTH)),
        (jax.ShapeDtypeStruct((T, B_WIDTH), f32), row(B_WIDTH)),
        (jax.ShapeDtypeStruct((T, D), bf), row(D)),
        (jax.ShapeDtypeStruct((T, D), bf), row(D)),
    ]
    ins = [x2, gain.reshape(1, D), wk, wik, wqT, wvT, wiqT, wiwT, wb, wg]
    in_specs = [row(D)] + [full(a) for a in ins[1:]]
    return pl.pallas_call(
        _inproj_kernel,
        grid=(B, nS),
        in_specs=in_specs,
        out_specs=[s for _, s in outs],
        out_shape=[s for s, _ in outs],
        compiler_params=_cparams(("parallel", "parallel")),
        name="inproj",
    )(*ins)


def _t5_bucket_table(n):
    d = np.arange(n)
    max_exact = REL_BUCKETS // 2
    nf = np.maximum(d, 1).astype(np.float64)
    large = max_exact + (np.log(nf / max_exact) / math.log(REL_MAX_DIST / max_exact)
                         * (REL_BUCKETS - max_exact)).astype(np.int32)
    large = np.minimum(large, REL_BUCKETS - 1)
    return np.where(d < max_exact, d, large)


def _dsa_kernel(qT_ref, k_ref, vT_ref, iqT_ref, iwT_ref, ik_ref, enear_ref,
                o_ref, sc_ref, qh_scr, m_scr, acc_scr, *, topk):
    TQ = qT_ref.shape[2]
    KC = TQ
    i = pl.program_id(1)
    nch = i + 1
    q0 = i * TQ
    f32 = jnp.float32
    bf16 = jnp.bfloat16
    key_id = lax.broadcasted_iota(jnp.int32, (KC, TQ), 0)
    qry_id = lax.broadcasted_iota(jnp.int32, (KC, TQ), 1)

    def col_reduce(x, op):
        return op(_fold_rows(x, op), axis=0, keepdims=True)

    iw = iwT_ref[0]

    def score_chunk(c, carry):
        rmin, rmax = carry
        k0 = pl.multiple_of(c * KC, KC)
        ik = ik_ref[pl.ds(k0, KC), :]
        acc = jnp.zeros((KC, TQ), f32)
        for h in range(IDX_HEADS):
            sh = jnp.dot(ik, iqT_ref[0, h * IDX_DIM:(h + 1) * IDX_DIM, :], preferred_element_type=f32)
            acc = acc + jnp.maximum(sh, 0.0) * iw[h:h + 1, :]
        valid = (k0 + key_id) <= (q0 + qry_id)
        sc_ref[pl.ds(k0, KC), :] = jnp.where(valid, acc, MASK_NEG)
        rmin = jnp.minimum(rmin, _fold_rows(jnp.where(valid, acc, -MASK_NEG), jnp.min))
        rmax = jnp.maximum(rmax, _fold_rows(jnp.where(valid, acc, MASK_NEG), jnp.max))
        return rmin, rmax

    rmin8, rmax8 = lax.fori_loop(
        0, nch, score_chunk,
        (jnp.full((V7X_SUBLANES, TQ), -MASK_NEG, f32), jnp.full((V7X_SUBLANES, TQ), MASK_NEG, f32)))
    rmin = jnp.min(rmin8, axis=0, keepdims=True)
    rmax = jnp.max(rmax8, axis=0, keepdims=True)

    def count_where(pred_fn):
        def body(c, acc):
            k0 = pl.multiple_of(c * KC, KC)
            blk = sc_ref[pl.ds(k0, KC), :]
            return acc + _fold_rows(jnp.where(pred_fn(blk), 1.0, 0.0), jnp.sum)
        acc = lax.fori_loop(0, nch, body, jnp.zeros((V7X_SUBLANES, TQ), f32))
        return jnp.sum(acc, axis=0, keepdims=True)

    def band_min_max(lo, hi):
        def body(c, carry):
            bmin, bmax = carry
            k0 = pl.multiple_of(c * KC, KC)
            blk = sc_ref[pl.ds(k0, KC), :]
            bmin = jnp.minimum(bmin, _fold_rows(jnp.where(blk >= lo, blk, -MASK_NEG), jnp.min))
            bmax = jnp.maximum(bmax, _fold_rows(jnp.where(blk < hi, blk, MASK_NEG), jnp.max))
            return bmin, bmax
        bmin8, bmax8 = lax.fori_loop(
            0, nch, body,
            (jnp.full((V7X_SUBLANES, TQ), -MASK_NEG, f32), jnp.full((V7X_SUBLANES, TQ), MASK_NEG, f32)))
        return jnp.min(bmin8, axis=0, keepdims=True), jnp.max(bmax8, axis=0, keepdims=True)

    kf = float(topk)
    n_valid = (q0 + 1 + lax.broadcasted_iota(jnp.int32, (1, TQ), 1)).astype(f32)
    lo0 = rmin
    cnt0 = n_valid
    hi0 = rmax + jnp.maximum(jnp.abs(rmax) * 2.0 ** -20, 1e-30)
    done0 = jnp.where(cnt0 <= kf, 1.0, 0.0)

    def probe(st, lo_s, mid, tie):
        it, lo, hi, cnt, done = st
        active = done < 0.5
        lo_s = jnp.where(active, lo_s, lo)
        c = count_where(lambda blk: blk >= mid)
        feas = c >= kf
        move = active & jnp.logical_not(tie)
        lo_n = jnp.where(move & feas, mid, lo_s)
        cnt_n = jnp.where(move & feas, c, cnt)
        hi_n = jnp.where(move & jnp.logical_not(feas), mid, hi)
        done_n = jnp.where((active & tie) | (cnt_n <= kf), 1.0, done)
        return it + 1, lo_n, hi_n, cnt_n, done_n

    def halve(st):
        _, lo, hi, _, _ = st
        half = lo + 0.5 * (hi - lo)
        stuck = (half <= lo) | (half >= hi)
        return probe(st, lo, half, stuck)

    def snap(st):
        _, lo, hi, _, _ = st
        bmin, bmax = band_min_max(lo, hi)
        mid = bmin + 0.5 * (bmax - bmin)
        return probe(st, bmin, jnp.where(mid <= bmin, bmax, mid), bmax <= bmin)

    st = lax.while_loop(lambda st: (jnp.min(st[-1]) < 0.5) & (st[0] < BISECT_FAST_ITERS), halve,
                        (jnp.int32(0), lo0, hi0, cnt0, done0))
    _, thr, _, cnt_thr, _ = lax.while_loop(lambda st: jnp.min(st[-1]) < 0.5, snap, st)

    tie_overflow = jnp.max(cnt_thr) > kf

    @pl.when(jnp.logical_not(tie_overflow))
    def _():
        def mask_chunk(c, _):
            k0 = pl.multiple_of(c * KC, KC)
            sc_ref[pl.ds(k0, KC), :] = jnp.where(sc_ref[pl.ds(k0, KC), :] >= thr, 0.0, MASK_NEG)
            return 0
        lax.fori_loop(0, nch, mask_chunk, 0)

    @pl.when(tie_overflow)
    def _():
        need = kf - count_where(lambda blk: blk > thr)
        tril = jnp.where(lax.broadcasted_iota(jnp.int32, (KC, KC), 1)
                         <= lax.broadcasted_iota(jnp.int32, (KC, KC), 0), 1.0, 0.0).astype(bf16)

        def mask_chunk(c, run):
            k0 = pl.multiple_of(c * KC, KC)
            blk = sc_ref[pl.ds(k0, KC), :]
            eq = jnp.where(blk == thr, 1.0, 0.0)
            pref = jnp.dot(tril, eq.astype(bf16), preferred_element_type=f32)
            sel = (blk > thr) | ((eq > 0.5) & (run + pref <= need))
            sc_ref[pl.ds(k0, KC), :] = jnp.where(sel, 0.0, MASK_NEG)
            return run + pref[KC - 1:KC, :]

        lax.fori_loop(0, nch, mask_chunk, jnp.zeros((1, TQ), f32))

    AK = min(ATT_KC, TQ)
    per = TQ // AK
    head0_q = (lax.broadcasted_iota(jnp.int32, (V7X_LANES, TQ), 0) // A_HEAD_DIM) == 0
    n_pairs = A_HEADS // 2

    m_scr[...] = jnp.full(m_scr.shape, MASK_NEG, f32)
    acc_scr[...] = jnp.zeros(acc_scr.shape, f32)
    v_row = lax.broadcasted_iota(jnp.int32, (V7X_LANES, AK), 0)
    denom_row = [A_HEAD_DIM * (1 - sub) for sub in range(2)]
    for p in range(n_pairs):
        q_pair = qT_ref[0, p * V7X_LANES:(p + 1) * V7X_LANES, :]
        zq = jnp.zeros_like(q_pair)
        qh_scr[2 * p] = jnp.where(head0_q, q_pair, zq)
        qh_scr[2 * p + 1] = jnp.where(head0_q, zq, q_pair)

    def step(c, bias_rows):
        k0 = pl.multiple_of(c * AK, AK)
        msk = sc_ref[pl.ds(k0, AK), :]
        for p in range(n_pairs):
            kp = k_ref[pl.ds(k0, AK), p * V7X_LANES:(p + 1) * V7X_LANES]
            vp = vT_ref[0, p * V7X_LANES:(p + 1) * V7X_LANES, pl.ds(k0, AK)]
            for sub in range(2):
                h = 2 * p + sub
                s = jnp.dot(kp, qh_scr[h], preferred_element_type=f32) + msk
                if bias_rows is not None:
                    s = s + enear_ref[h, bias_rows, :]
                m = m_scr[h:h + 1, :]
                m_new = jnp.maximum(m, col_reduce(s, jnp.max))
                alpha = jnp.exp2(m - m_new)
                pr = jnp.exp2(s - m_new)
                m_scr[h:h + 1, :] = m_new
                v_aug = jnp.where(v_row == denom_row[sub], jnp.ones_like(vp), vp)
                acc_scr[h] = alpha * acc_scr[h] + jnp.dot(v_aug, pr.astype(bf16), preferred_element_type=f32)

    def far(blk, _):
        for jj in range(per):
            step(blk * per + jj, None)
        return 0

    def near(block, first_chunk):
        for jj in range(per):
            step(first_chunk + jj, slice(block * TQ + jj * AK, block * TQ + (jj + 1) * AK))

    lax.fori_loop(0, jnp.maximum(i - 1, 0), far, 0)

    @pl.when(i >= 1)
    def _():
        near(0, (i - 1) * per)

    near(1, i * per)
    for p in range(n_pairs):
        outs = [acc_scr[2 * p + sub] / acc_scr[2 * p + sub, denom_row[sub]:denom_row[sub] + 1, :]
                for sub in range(2)]
        o_pair = jnp.where(head0_q, outs[0], outs[1])
        o_ref[:, p * V7X_LANES:(p + 1) * V7X_LANES] = o_pair.T.astype(o_ref.dtype)


def _dsa(k, ik, qT, vT, iqT, iwT, rel_bias, B, S):
    T = k.shape[0]
    TQ = min(ATT_Q, S)
    nQ = S // TQ
    topk = min(TOPK_MAX, S // 4)
    buckets = _t5_bucket_table(2 * TQ + 1)
    assert np.all(_t5_bucket_table(S + 1)[TQ + 1:] == REL_BUCKETS - 1)
    j = np.arange(2 * TQ)[:, None]
    r = np.arange(TQ)[None, :]
    dist = np.maximum(r + TQ - j, 0)
    onehot = (jnp.asarray(buckets[dist], jnp.int32)[None]
              == jnp.arange(REL_BUCKETS, dtype=jnp.int32)[:, None, None]).astype(jnp.float32)
    rel = (rel_bias.astype(jnp.float32) - rel_bias[REL_BUCKETS - 1].astype(jnp.float32)[None, :]) * LOG2_E
    enear = jnp.einsum('nh,njr->hjr', rel, onehot, precision=lax.Precision.HIGHEST)

    return pl.pallas_call(
        functools.partial(_dsa_kernel, topk=topk),
        grid=(B, nQ),
        in_specs=[
            pl.BlockSpec((1, A_WIDTH, TQ), lambda b, i: (b, 0, i)),
            pl.BlockSpec((S, A_WIDTH), lambda b, i: (b, 0)),
            pl.BlockSpec((1, A_WIDTH, S), lambda b, i: (b, 0, 0)),
            pl.BlockSpec((1, IDX_WIDTH, TQ), lambda b, i: (b, 0, i)),
            pl.BlockSpec((1, IDX_HEADS, TQ), lambda b, i: (b, 0, i)),
            pl.BlockSpec((S, IDX_DIM), lambda b, i: (b, 0)),
            pl.BlockSpec((A_HEADS, 2 * TQ, TQ), lambda b, i: (0, 0, 0)),
        ],
        out_specs=pl.BlockSpec((TQ, A_WIDTH), lambda b, i: (b * nQ + i, 0)),
        scratch_shapes=[pltpu.VMEM((S, TQ), jnp.float32),
                        pltpu.VMEM((A_HEADS, V7X_LANES, TQ), jnp.bfloat16),
                        pltpu.VMEM((A_HEADS, TQ), jnp.float32),
                        pltpu.VMEM((A_HEADS, V7X_LANES, TQ), jnp.float32)],
        out_shape=jax.ShapeDtypeStruct((T, A_WIDTH), jnp.bfloat16),
        compiler_params=_cparams(("parallel", "arbitrary")),
        name="dsa",
    )(qT, k, vT, iqT, iwT, ik, enear)


def _hgrn_kernel(bq_ref, bf_ref, bi_ref, bg_ref, lb_ref, gain_ref, o_ref,
                 st_ref, b_scr, q_scr, k_scr, v_scr, oi_scr, qd_s, kd_s, kl_s, vv_s, dec_s, oi_s, upd_s, st_s):
    R = bq_ref.shape[0]
    C = HGRN_CHUNK
    nC = R // C
    f32 = jnp.float32
    bf16 = jnp.bfloat16
    h = pl.program_id(1)

    @pl.when(pl.program_id(2) == 0)
    def _():
        st_ref[...] = jnp.zeros_like(st_ref)

    lb = lb_ref[pl.ds(h, 1), :]
    gain = gain_ref[pl.ds(h, 1), :]
    tril_incl = jnp.where(lax.broadcasted_iota(jnp.int32, (C, C), 1)
                          <= lax.broadcasted_iota(jnp.int32, (C, C), 0), 1.0, 0.0)
    srow = lax.broadcasted_iota(jnp.int32, (C, B_KEY_DIM), 0)

    def gates(r0):
        f = lb + (1.0 - lb) * _sigmoid(bf_ref[pl.ds(r0, C), :])
        qr = bq_ref[pl.ds(r0, C), :]
        return jnp.log(f), 1.0 - f, qr * _sigmoid(qr) * (B_KEY_DIM ** -0.5), bi_ref[pl.ds(r0, C), :]

    def cumdecay(g):
        tri = tril_incl.astype(bf16)
        g_hi = g.astype(bf16)
        rest = g - g_hi.astype(f32)
        g_mid = rest.astype(bf16)
        g_lo = (rest - g_mid.astype(f32)).astype(bf16)
        return (jnp.dot(tri, g_hi, preferred_element_type=f32) + jnp.dot(tri, g_mid, preferred_element_type=f32)
                + jnp.dot(tri, g_lo, preferred_element_type=f32))

    def advance(r0, st, qd, o_intra, upd, decay_row):
        o_inter = lax.dot_general(qd, st.astype(bf16), (((1,), (1,)), ((), ())), preferred_element_type=f32)
        og = bg_ref[pl.ds(r0, C), :]
        y = _rms(o_inter + o_intra, gain) * (og * _sigmoid(og))
        o_ref[pl.ds(r0, C), :] = y.astype(o_ref.dtype)
        return st * decay_row + upd

    f_all = lb + (1.0 - lb) * _sigmoid(bf_ref[...])
    g_all = jnp.log(f_all)
    decay = jnp.sum(g_all.reshape(nC, C, B_KEY_DIM), axis=1)
    safe = jnp.min(decay) >= -HGRN_SAFE_DECAY

    @pl.when(safe)
    def _():
        qr = bq_ref[...]
        qq = qr * _sigmoid(qr) * (B_KEY_DIM ** -0.5)
        kk = 1.0 - f_all
        b = jnp.concatenate([cumdecay(g_all[c * C:(c + 1) * C]) for c in range(nC)], axis=0)
        b_end = jnp.concatenate([jnp.broadcast_to(b[(c + 1) * C - 1:(c + 1) * C], (C, B_KEY_DIM))
                                 for c in range(nC)], axis=0)
        qd_s[...] = (qq * jnp.exp(b)).astype(bf16)
        kd_s[...] = (kk * jnp.exp(-b)).astype(bf16)
        kl_s[...] = (kk * jnp.exp(b_end - b)).astype(bf16)
        vv_s[...] = bi_ref[...].astype(bf16)
        dec_s[...] = jnp.exp(b_end)
        for c in range(nC):
            rows = slice(c * C, (c + 1) * C)
            att = lax.dot_general(qd_s[rows], kd_s[rows], (((1,), (1,)), ((), ())),
                                  preferred_element_type=f32) * tril_incl
            oi_s[rows] = jnp.dot(att.astype(bf16), vv_s[rows], preferred_element_type=f32)
            upd_s[c] = lax.dot_general(vv_s[rows], kl_s[rows], (((0,), (0,)), ((), ())),
                                       preferred_element_type=f32)
        st = st_ref[...]
        for c in range(nC):
            st_s[c] = st.astype(bf16)
            st = st * dec_s[c * C:c * C + 1] + upd_s[c]
        st_ref[...] = st
        for c in range(nC):
            rows = slice(c * C, (c + 1) * C)
            oi_s[rows] = oi_s[rows] + lax.dot_general(qd_s[rows], st_s[c], (((1,), (1,)), ((), ())),
                                                      preferred_element_type=f32)
        og = bg_ref[...]
        o_ref[...] = (_rms(oi_s[...], gain) * (og * _sigmoid(og))).astype(o_ref.dtype)

    @pl.when(jnp.logical_not(safe))
    def _():
        def body(c, st):
            r0 = pl.multiple_of(c * C, C)
            g, kk, qq, vv = gates(r0)
            b = cumdecay(g)
            b_last = b[C - 1:C, :]
            b_scr[...] = b
            q_scr[...] = qq
            k_scr[...] = kk
            v_scr[...] = vv

            def row(t, _):
                bt = b_scr[pl.ds(t, 1), :]
                qt = q_scr[pl.ds(t, 1), :]
                ex = jnp.where(srow <= t, bt - b_scr[...], -jnp.inf)
                a = jnp.sum(qt * k_scr[...] * jnp.exp(ex), axis=1, keepdims=True)
                oi_scr[pl.ds(t, 1), :] = jnp.sum(a * v_scr[...], axis=0, keepdims=True)
                return 0
            lax.fori_loop(0, C, row, 0)
            kd_last = (kk * jnp.exp(b_last - b)).astype(bf16)
            upd = lax.dot_general(vv.astype(bf16), kd_last, (((0,), (0,)), ((), ())),
                                  preferred_element_type=f32)
            return advance(r0, st, (qq * jnp.exp(b)).astype(bf16), oi_scr[...], upd, jnp.exp(b_last))
        st_ref[...] = lax.fori_loop(0, nC, body, st_ref[...])


def _hgrn(bq, bf, bi, bg, lb, gain, B, S):
    T = bq.shape[0]
    R = min(HGRN_ROWS, S)
    nR = S // R
    C = HGRN_CHUNK
    blk = pl.BlockSpec((R, B_KEY_DIM), lambda b, h, c: (b * nR + c, h))
    small = pl.BlockSpec((B_HEADS, B_KEY_DIM), lambda b, h, c: (0, 0))
    f32 = jnp.float32
    return pl.pallas_call(
        _hgrn_kernel,
        grid=(B, B_HEADS, nR),
        in_specs=[blk, blk, blk, blk, small, small],
        out_specs=blk,
        out_shape=jax.ShapeDtypeStruct((T, B_WIDTH), jnp.bfloat16),
        scratch_shapes=[pltpu.VMEM((B_VAL_DIM, B_KEY_DIM), f32)] +
                       [pltpu.VMEM((C, B_KEY_DIM), f32) for _ in range(5)] +
                       [pltpu.VMEM((R, B_KEY_DIM), jnp.bfloat16) for _ in range(4)] +
                       [pltpu.VMEM((R, B_KEY_DIM), f32) for _ in range(2)] +
                       [pltpu.VMEM((R // C, B_VAL_DIM, B_KEY_DIM), f32),
                        pltpu.VMEM((R // C, B_VAL_DIM, B_KEY_DIM), jnp.bfloat16)],
        compiler_params=_cparams(("parallel", "parallel", "arbitrary")),
        name="hgrn",
    )(bq, bf, bi, bg, lb, gain)


def _merge_kernel(x_ref, ya_ref, yb_ref, ga_ref, gb_ref, wa_ref, wb_ref, wo_ref, g_ref, wrh_ref, wrl_ref, br_ref,
                  x1_ref, xn_ref, lg_ref):
    f32 = jnp.float32
    ma = jnp.dot(ya_ref[...], wa_ref[...], preferred_element_type=f32)
    mb = jnp.dot(yb_ref[...], wb_ref[...], preferred_element_type=f32)
    merged = ga_ref[...].astype(f32) * ma + gb_ref[...].astype(f32) * mb
    x1 = x_ref[...] + jnp.dot(merged.astype(jnp.bfloat16), wo_ref[...], preferred_element_type=f32)
    x1_ref[...] = x1
    hn = _rms(x1, g_ref[...])
    xn_ref[...] = _pack_bf16_pairs(hn)
    hn_hi = hn.astype(jnp.bfloat16)
    hn_lo = (hn - hn_hi.astype(f32)).astype(jnp.bfloat16)
    lg_ref[...] = (jnp.dot(hn_hi, wrh_ref[...], preferred_element_type=f32)
                   + jnp.dot(hn_lo, wrh_ref[...], preferred_element_type=f32)
                   + jnp.dot(hn_hi, wrl_ref[...], preferred_element_type=f32) + br_ref[...])


def _merge(x2, ya, yb, ga, gb, w_up_a, w_up_b, w_out, gain, w_router, b_router):
    T, D = x2.shape
    R = min(PROJ_ROWS, T)
    bf = jnp.bfloat16
    wr_hi = w_router.astype(bf)
    wr_lo = (w_router - wr_hi.astype(jnp.float32)).astype(bf)
    ins = [x2, ya, yb, ga, gb, w_up_a.astype(bf), w_up_b.astype(bf), w_out.astype(bf),
           gain.reshape(1, D), wr_hi, wr_lo, b_router.reshape(1, N_EXPERTS)]
    row = lambda n: pl.BlockSpec((R, n), lambda i: (i, 0))
    full = lambda a: pl.BlockSpec(a.shape, lambda i: (0,) * a.ndim)
    in_specs = [row(D), row(A_WIDTH), row(B_WIDTH), row(D), row(D)] + [full(a) for a in ins[5:]]
    return pl.pallas_call(
        _merge_kernel,
        grid=(T // R,),
        in_specs=in_specs,
        out_specs=[row(D), row(D // 2), row(N_EXPERTS)],
        out_shape=[jax.ShapeDtypeStruct((T, D), jnp.float32), jax.ShapeDtypeStruct((T, D // 2), jnp.uint32),
                   jax.ShapeDtypeStruct((T, N_EXPERTS), jnp.float32)],
        compiler_params=_cparams(("parallel",)),
        name="merge",
    )(*ins)


def _route_kernel(lg_ref, eidx_ref, gate_ref, rank_ref, cnt_ref, run_ref):
    R = lg_ref.shape[0]
    f32 = jnp.float32

    @pl.when(pl.program_id(0) == 0)
    def _():
        run_ref[...] = jnp.zeros_like(run_ref)

    lg = lg_ref[...]
    lane = lax.broadcasted_iota(jnp.int32, (R, N_EXPERTS), 1)
    work = lg
    onehots, vals, idxs = [], [], []
    for _ in range(TOP_K):
        m = jnp.max(work, axis=1, keepdims=True)
        idx = jnp.min(jnp.where(work == m, lane, N_EXPERTS), axis=1, keepdims=True)
        oh = lane == idx
        onehots.append(oh)
        vals.append(m)
        idxs.append(idx)
        work = jnp.where(oh, -jnp.inf, work)
    ex = [jnp.exp(v - vals[0]) for v in vals]
    den = ex[0] + ex[1] + ex[2] + ex[3]
    chosen = jnp.where(onehots[0] | onehots[1] | onehots[2] | onehots[3], 1.0, 0.0)
    strict = jnp.where(lax.broadcasted_iota(jnp.int32, (R, R), 1)
                       < lax.broadcasted_iota(jnp.int32, (R, R), 0), 1.0, 0.0).astype(jnp.bfloat16)
    before = jnp.dot(strict, chosen.astype(jnp.bfloat16), preferred_element_type=f32) + run_ref[...]
    lane4 = lax.broadcasted_iota(jnp.int32, (R, TOP_K), 1)
    eidx = jnp.zeros((R, TOP_K), jnp.int32)
    gate = jnp.zeros((R, TOP_K), f32)
    rank = jnp.zeros((R, TOP_K), f32)
    for k in range(TOP_K):
        eidx = jnp.where(lane4 == k, idxs[k], eidx)
        gate = jnp.where(lane4 == k, ex[k] / den, gate)
        rk = jnp.sum(jnp.where(onehots[k], before, 0.0), axis=1, keepdims=True)
        rank = jnp.where(lane4 == k, rk, rank)
    eidx_ref[...] = eidx
    gate_ref[...] = gate
    rank_ref[...] = rank.astype(jnp.int32)
    run_ref[...] = run_ref[...] + jnp.sum(chosen, axis=0, keepdims=True)
    cnt_ref[...] = run_ref[...].astype(jnp.int32)


def _route(logits):
    T = logits.shape[0]
    R = min(ROUTE_ROWS, T)
    row = lambda n: pl.BlockSpec((R, n), lambda i: (i, 0))
    return pl.pallas_call(
        _route_kernel,
        grid=(T // R,),
        in_specs=[row(N_EXPERTS)],
        out_specs=[row(TOP_K), row(TOP_K), row(TOP_K), pl.BlockSpec((1, N_EXPERTS), lambda i: (0, 0))],
        out_shape=[jax.ShapeDtypeStruct((T, TOP_K), jnp.int32), jax.ShapeDtypeStruct((T, TOP_K), jnp.float32),
                   jax.ShapeDtypeStruct((T, TOP_K), jnp.int32), jax.ShapeDtypeStruct((1, N_EXPERTS), jnp.int32)],
        scratch_shapes=[pltpu.VMEM((1, N_EXPERTS), jnp.float32)],
        compiler_params=_cparams(("arbitrary",)),
        name="route",
    )(logits)


def _sc_gather_rows(table, idx):
    N, W = table.shape
    M = idx.shape[0]
    workers = V7X_SC_CORES * V7X_SC_SUBCORES
    per_worker = M // workers
    pieces = per_worker // SC_GATHER_ROWS
    assert per_worker * workers == M and pieces * SC_GATHER_ROWS == per_worker and pieces % 2 == 0
    mesh = plsc.VectorSubcoreMesh(core_axis_name="core", subcore_axis_name="subcore",
                                  num_cores=V7X_SC_CORES, num_subcores=V7X_SC_SUBCORES)

    @functools.partial(
        pl.kernel, mesh=mesh,
        out_type=jax.ShapeDtypeStruct((M, W), table.dtype),
        scratch_types=[pltpu.VMEM((per_worker,), jnp.int32),
                       pltpu.VMEM((SC_GATHER_ROWS, W), table.dtype),
                       pltpu.VMEM((SC_GATHER_ROWS, W), table.dtype),
                       pltpu.SemaphoreType.DMA, pltpu.SemaphoreType.DMA],
    )
    def gather(table_hbm, idx_hbm, out_hbm, idx_v, rows_a, rows_b, sem_a, sem_b):
        worker = lax.axis_index("subcore") * V7X_SC_CORES + lax.axis_index("core")
        base = pl.multiple_of(worker * per_worker, SC_GATHER_ROWS)
        pltpu.sync_copy(idx_hbm.at[pl.ds(base, per_worker)], idx_v)
        bufs = ((rows_a, sem_a), (rows_b, sem_b))

        def fetch(g, buf, sem):
            off = pl.multiple_of(g * SC_GATHER_ROWS, SC_GATHER_ROWS)
            return pltpu.make_async_copy(table_hbm.at[idx_v.at[pl.ds(off, SC_GATHER_ROWS)]], buf, sem)

        fetch(0, *bufs[0]).start()

        @pl.loop(0, pieces, step=2)
        def _(g0):
            for half in range(2):
                g = g0 + half
                buf, sem = bufs[half]
                fetch(g, buf, sem).wait()

                @pl.when(g + 1 < pieces)
                def _():
                    fetch(g + 1, *bufs[1 - half]).start()

                off = pl.multiple_of(g * SC_GATHER_ROWS, SC_GATHER_ROWS)
                pltpu.sync_copy(buf, out_hbm.at[pl.ds(base + off, SC_GATHER_ROWS)])

    return gather(table, idx)


def _sc_scatter_rows(rows, dest, n_out):
    T, W = rows.shape
    slots = dest.shape[1]
    workers = V7X_SC_CORES * V7X_SC_SUBCORES
    per_worker = T // workers
    pieces = per_worker // SC_GATHER_ROWS
    assert per_worker * workers == T and pieces * SC_GATHER_ROWS == per_worker and pieces % 2 == 0
    idx = dest.reshape(workers, pieces, SC_GATHER_ROWS, slots).transpose(0, 1, 3, 2)
    mesh = plsc.VectorSubcoreMesh(core_axis_name="core", subcore_axis_name="subcore",
                                  num_cores=V7X_SC_CORES, num_subcores=V7X_SC_SUBCORES)

    @functools.partial(
        pl.kernel, mesh=mesh,
        out_type=jax.ShapeDtypeStruct((n_out, W), rows.dtype),
        scratch_types=[pltpu.VMEM((pieces, slots, SC_GATHER_ROWS), jnp.int32),
                       pltpu.VMEM((SC_GATHER_ROWS, W), rows.dtype),
                       pltpu.VMEM((SC_GATHER_ROWS, W), rows.dtype),
                       pltpu.SemaphoreType.DMA, pltpu.SemaphoreType.DMA, pltpu.SemaphoreType.DMA],
    )
    def scatter(rows_hbm, idx_hbm, out_hbm, idx_v, rows_a, rows_b, sem_a, sem_b, sem_out):
        worker = lax.axis_index("subcore") * V7X_SC_CORES + lax.axis_index("core")
        base = pl.multiple_of(worker * per_worker, SC_GATHER_ROWS)
        pltpu.sync_copy(idx_hbm.at[worker], idx_v)
        bufs = ((rows_a, sem_a), (rows_b, sem_b))

        def fetch(g, buf, sem):
            off = pl.multiple_of(g * SC_GATHER_ROWS, SC_GATHER_ROWS)
            return pltpu.make_async_copy(rows_hbm.at[pl.ds(base + off, SC_GATHER_ROWS)], buf, sem)

        fetch(0, *bufs[0]).start()

        @pl.loop(0, pieces, step=2)
        def _(g0):
            for half in range(2):
                g = g0 + half
                buf, sem = bufs[half]
                fetch(g, buf, sem).wait()

                @pl.when(g + 1 < pieces)
                def _():
                    fetch(g + 1, *bufs[1 - half]).start()

                puts = [pltpu.make_async_copy(buf, out_hbm.at[idx_v.at[g, k]], sem_out) for k in range(slots)]
                for put in puts:
                    put.start()
                for put in puts:
                    put.wait()

    return scatter(rows, idx)


def _experts_kernel(be_ref, nb_ref, first_ref, slot_ref, next_ref,
                    x_ref, wgu_hbm, bgu_ref, wd_hbm, bd_ref, o_ref,
                    wgu_f32, wd_f32, wgu_bf, wd_bf, sems):
    f32 = jnp.float32
    d_ff = wd_bf.shape[0]
    i = pl.program_id(0)
    live = i < nb_ref[0]

    def weight_copies(e, s):
        return (pltpu.make_async_copy(wgu_hbm.at[e], wgu_f32.at[s], sems.at[s, 0]),
                pltpu.make_async_copy(wd_hbm.at[e], wd_f32.at[s], sems.at[s, 1]))

    @pl.when(live & (first_ref[i] == 1))
    def _():
        e = be_ref[i]
        s = slot_ref[i]

        @pl.when(i == 0)
        def _():
            for cp in weight_copies(e, s):
                cp.start()

        for cp in weight_copies(e, s):
            cp.wait()

        @pl.when(next_ref[i] >= 0)
        def _():
            for cp in weight_copies(next_ref[i], 1 - s):
                cp.start()

        wgu_bf[...] = wgu_f32[s].astype(jnp.bfloat16)
        wd_bf[...] = wd_f32[s].astype(jnp.bfloat16)

    @pl.when(live)
    def _():
        x_hi, x_lo = _unpack_bf16_pairs(x_ref[...])
        half = x_hi.shape[1]
        gu = (jnp.dot(x_hi, wgu_bf[:half, :], preferred_element_type=f32)
              + jnp.dot(x_lo, wgu_bf[half:, :], preferred_element_type=f32) + bgu_ref[0])
        gate = jnp.minimum(gu[:, :d_ff], SWIGLU_LIMIT)
        lin = jnp.clip(gu[:, d_ff:], -SWIGLU_LIMIT, SWIGLU_LIMIT)
        act = (lin + 1.0) * gate * _sigmoid(SWIGLU_ALPHA * gate)
        y = jnp.dot(act.astype(jnp.bfloat16), wd_bf[...], preferred_element_type=f32) + bd_ref[0]
        o_ref[...] = _pack_bf16_pairs(y)

    @pl.when(pl.program_id(0) >= nb_ref[0])
    def _():
        o_ref[...] = jnp.zeros_like(o_ref)


def _experts(xs, plan, w_gu, b_gu, w_down, b_down):
    P, W = xs.shape
    E, D, F2 = w_gu.shape
    nb = P // EXPERT_ROWS
    by_expert = lambda i, be, *_: (be[i], 0, 0)
    grid_spec = pltpu.PrefetchScalarGridSpec(
        num_scalar_prefetch=5,
        grid=(nb,),
        in_specs=[
            pl.BlockSpec((EXPERT_ROWS, W), lambda i, *_: (i, 0)),
            pl.BlockSpec(memory_space=pl.ANY),
            pl.BlockSpec((1, 1, F2), by_expert),
            pl.BlockSpec(memory_space=pl.ANY),
            pl.BlockSpec((1, 1, D), by_expert),
        ],
        out_specs=pl.BlockSpec((EXPERT_ROWS, W), lambda i, *_: (i, 0)),
        scratch_shapes=[pltpu.VMEM((2, D, F2), jnp.float32), pltpu.VMEM((2, F2 // 2, D), jnp.float32),
                        pltpu.VMEM((D, F2), jnp.bfloat16), pltpu.VMEM((F2 // 2, D), jnp.bfloat16),
                        pltpu.SemaphoreType.DMA((2, 2))],
    )
    return pl.pallas_call(
        _experts_kernel,
        grid_spec=grid_spec,
        out_shape=jax.ShapeDtypeStruct((P, W), jnp.uint32),
        compiler_params=_cparams(("arbitrary",)),
        name="experts",
    )(*plan, xs, w_gu, b_gu.reshape(E, 1, F2), w_down, b_down.reshape(E, 1, D))


def _combine_kernel(ya_ref, x1_ref, gate_ref, g_ref, o_ref):
    half = x1_ref.shape[1] // 2
    f32 = jnp.float32
    gate = gate_ref[...]
    x1 = x1_ref[...]
    y_hi = x1[:, :half]
    y_lo = x1[:, half:]
    for k in range(TOP_K):
        hi, lo = _unpack_bf16_pairs(ya_ref[k])
        y_hi = y_hi + gate[:, k:k + 1] * hi.astype(f32)
        y_lo = y_lo + gate[:, k:k + 1] * lo.astype(f32)
    o_ref[...] = _rms(jnp.concatenate([y_hi, y_lo], axis=1), g_ref[...])


def _combine(ya, x1, gates, gain):
    T, D = x1.shape
    R = min(COMBINE_ROWS, T)
    row = lambda w: pl.BlockSpec((R, w), lambda i: (i, 0))
    return pl.pallas_call(
        _combine_kernel,
        grid=(T // R,),
        in_specs=[pl.BlockSpec((TOP_K, R, D // 2), lambda i: (0, i, 0)), row(D), row(TOP_K),
                  pl.BlockSpec((1, D), lambda i: (0, 0))],
        out_specs=row(D),
        out_shape=jax.ShapeDtypeStruct((T, D), jnp.float32),
        compiler_params=_cparams(("parallel",)),
        name="combine",
    )(ya, x1, gates, gain.reshape(1, D))


def _moe_plan(eidx, rank, counts, A):
    counts = counts.reshape(N_EXPERTS)
    padded = (counts + EXPERT_ROWS - 1) // EXPERT_ROWS * EXPERT_ROWS
    pad_ends = jnp.cumsum(padded)
    pad_starts = pad_ends - padded
    n_blocks = -(-A // EXPERT_ROWS) + N_EXPERTS
    dest = pad_starts[eidx] + rank
    block_start = jnp.arange(n_blocks, dtype=pad_ends.dtype) * EXPERT_ROWS
    block_expert = jnp.minimum(jnp.sum(pad_ends[None, :] <= block_start[:, None], axis=1),
                               N_EXPERTS - 1).astype(jnp.int32)
    n_used = (pad_ends[-1] // EXPERT_ROWS).astype(jnp.int32).reshape(1)
    has_rows = counts > 0
    ordinal = jnp.cumsum(has_rows.astype(jnp.int32)) - 1
    ids = jnp.arange(N_EXPERTS, dtype=jnp.int32)
    later = has_rows[None, :] & (ids[None, :] > ids[:, None])
    next_expert = jnp.where(jnp.any(later, axis=1), jnp.argmax(later, axis=1), -1).astype(jnp.int32)
    block_first = ((block_start == pad_starts[block_expert]) & (block_start < pad_ends[-1])).astype(jnp.int32)
    block_slot = (ordinal[block_expert] % 2).astype(jnp.int32)
    block_next = next_expert[block_expert]
    plan = (block_expert, n_used, block_first, block_slot, block_next)
    return dest.astype(jnp.int32), plan, n_blocks


def kernel(x, w_in, w_up_a, w_up_b, w_out, norm_mix, norm_ffn, norm_final, hgrn_norm,
           lb_logits, rel_bias, w_router, b_router, w_gu, b_gu, w_down, b_down):
    B, S, D = x.shape
    T = B * S
    assert w_in.shape[0] == 1, "the final rmsnorm is fused into the single layer's combine stage"
    lb_all = jnp.cumsum(jax.nn.softmax(lb_logits.astype(jnp.float32), axis=0), axis=0)
    x2 = x.reshape(T, D)
    (k, ik, qT, vT, iqT, iwT, bq, bf, bi, bg, ga, gb) = _inproj(x2, norm_mix[0], w_in[0], B, S)
    ya = _dsa(k, ik, qT, vT, iqT, iwT, rel_bias, B, S)
    yb = _hgrn(bq, bf, bi, bg, lb_all[0].reshape(B_HEADS, B_KEY_DIM), hgrn_norm[0], B, S)
    x1, xn, logits = _merge(x2, ya, yb, ga, gb, w_up_a[0], w_up_b[0], w_out[0], norm_ffn[0],
                            w_router[0], b_router[0])
    eidx, gates, rank, counts = _route(logits)
    dest, plan, n_blocks = _moe_plan(eidx, rank, counts, T * TOP_K)
    P = n_blocks * EXPERT_ROWS
    A = T * TOP_K
    xs = _sc_scatter_rows(xn, dest, P)
    y_buf = _experts(xs, plan, w_gu[0], b_gu[0], w_down[0], b_down[0])
    ya = _sc_gather_rows(y_buf, dest.T.reshape(A)).reshape(TOP_K, T, D // 2)
    out = _combine(ya, x1, gates, norm_final)
    return out.reshape(B, S, D)
```

```python
import functools
import math

import numpy as np
import jax
import jax.numpy as jnp
from jax import lax
from jax.experimental import pallas as pl
from jax.experimental.pallas import tpu as pltpu
from jax.experimental.pallas import tpu_sc as plsc

A_HEADS = 8
A_HEAD_DIM = 64
IDX_HEADS = 8
IDX_DIM = 32
TOPK_MAX = 256
REL_BUCKETS = 32
REL_MAX_DIST = 128
B_HEADS = 4
B_KEY_DIM = 128
B_VAL_DIM = 128
N_EXPERTS = 32
TOP_K = 4
SWIGLU_LIMIT = 7.0
SWIGLU_ALPHA = 1.702
EPS = 1e-6
LOG2_E = math.log2(math.e)

A_WIDTH = A_HEADS * A_HEAD_DIM
B_WIDTH = B_HEADS * B_VAL_DIM
IDX_WIDTH = IDX_HEADS * IDX_DIM

V7X_LANES = 128
V7X_SUBLANES = 8
V7X_VMEM_LIMIT_BYTES = 56 * 1024 * 1024
V7X_SC_CORES = 2
V7X_SC_SUBCORES = 16

PROJ_ROWS = 512
ATT_Q = 256
ATT_KC = 128
HGRN_ROWS = 1024
HGRN_CHUNK = 64
HGRN_SAFE_DECAY = 70.0
ROUTE_ROWS = 512
EXPERT_ROWS = 256
COMBINE_ROWS = 512
SC_GATHER_ROWS = 64
MASK_NEG = -1e30
BISECT_FAST_ITERS = 26


def _cparams(dims):
    return pltpu.CompilerParams(dimension_semantics=dims, vmem_limit_bytes=V7X_VMEM_LIMIT_BYTES)


def _rms(x, gain):
    return x * lax.rsqrt(jnp.mean(x * x, axis=-1, keepdims=True) + EPS) * gain


def _sigmoid(x):
    return 1.0 / (1.0 + jnp.exp(-x))


def _pack_bf16_pairs(x):
    n = x.shape[1] // 2
    as_bits = lambda v: lax.bitcast_convert_type(v.astype(jnp.bfloat16).astype(jnp.float32), jnp.uint32)
    return (as_bits(x[:, :n]) & jnp.uint32(0xFFFF0000)) | (as_bits(x[:, n:]) >> 16)


def _unpack_bf16_pairs(w):
    hi = lax.bitcast_convert_type(w & jnp.uint32(0xFFFF0000), jnp.float32).astype(jnp.bfloat16)
    lo = lax.bitcast_convert_type(w << 16, jnp.float32).astype(jnp.bfloat16)
    return hi, lo


def _fold_rows(x, op):
    return op(x.reshape(x.shape[0] // V7X_SUBLANES, V7X_SUBLANES, x.shape[1]), axis=0)


def _inproj_kernel(x_ref, g_ref, wk_ref, wik_ref, wqT_ref, wvT_ref, wiqT_ref, wiwT_ref, wb_ref, wg_ref,
                   k_ref, ik_ref, qT_ref, vT_ref, iqT_ref, iwT_ref, bq_ref, bf_ref, bi_ref, bg_ref,
                   ga_ref, gb_ref):
    x = x_ref[...]
    hn = _rms(x, g_ref[...]).astype(jnp.bfloat16)

    def mm(w_ref):
        return jnp.dot(hn, w_ref[...], preferred_element_type=jnp.float32)

    def mm_t(w_ref):
        return lax.dot_general(w_ref[...], hn, (((1,), (1,)), ((), ())),
                               preferred_element_type=jnp.float32)

    k_ref[...] = mm(wk_ref).astype(jnp.bfloat16)
    ik_ref[...] = mm(wik_ref).astype(jnp.bfloat16)
    qT_ref[0] = (mm_t(wqT_ref) * (A_HEAD_DIM ** -0.5 * LOG2_E)).astype(jnp.bfloat16)
    vT_ref[0] = mm_t(wvT_ref).astype(jnp.bfloat16)
    iqT_ref[0] = mm_t(wiqT_ref).astype(jnp.bfloat16)
    iwT_ref[0] = mm_t(wiwT_ref) * ((IDX_HEADS * IDX_DIM) ** -0.5)
    hb = mm(wb_ref)
    bq_ref[...] = hb[:, 0 * B_WIDTH:1 * B_WIDTH]
    bf_ref[...] = hb[:, 1 * B_WIDTH:2 * B_WIDTH]
    bi_ref[...] = hb[:, 2 * B_WIDTH:3 * B_WIDTH]
    bg_ref[...] = hb[:, 3 * B_WIDTH:4 * B_WIDTH]
    d = ga_ref.shape[-1]
    hg = mm(wg_ref)
    ga_ref[...] = _sigmoid(hg[:, :d]).astype(jnp.bfloat16)
    gb_ref[...] = _sigmoid(hg[:, d:]).astype(jnp.bfloat16)


def _inproj(x2, gain, w_in, B, S):
    T, D = x2.shape
    R = min(PROJ_ROWS, S)
    nS = S // R
    o = np.cumsum((0, A_WIDTH, A_WIDTH, A_WIDTH, IDX_WIDTH, IDX_HEADS, IDX_DIM,
                   B_WIDTH, B_WIDTH, B_WIDTH, B_WIDTH, D, D))
    bf = jnp.bfloat16
    wqT = w_in[:, o[0]:o[1]].T.astype(bf)
    wk = w_in[:, o[1]:o[2]].astype(bf)
    wvT = w_in[:, o[2]:o[3]].T.astype(bf)
    wiqT = w_in[:, o[3]:o[4]].T.astype(bf)
    wiwT = w_in[:, o[4]:o[5]].T.astype(bf)
    wik = w_in[:, o[5]:o[6]].astype(bf)
    wb = w_in[:, o[6]:o[10]].astype(bf)
    wg = w_in[:, o[10]:o[12]].astype(bf)

    def full(a):
        return pl.BlockSpec(a.shape, lambda b, i: (0,) * a.ndim)

    row = lambda n: pl.BlockSpec((R, n), lambda b, i: (b * nS + i, 0))
    colT = lambda n: pl.BlockSpec((1, n, R), lambda b, i: (b, 0, i))
    f32 = jnp.float32
    outs = [
        (jax.ShapeDtypeStruct((T, A_WIDTH), bf), row(A_WIDTH)),
        (jax.ShapeDtypeStruct((T, IDX_DIM), bf), row(IDX_DIM)),
        (jax.ShapeDtypeStruct((B, A_WIDTH, S), bf), colT(A_WIDTH)),
        (jax.ShapeDtypeStruct((B, A_WIDTH, S), bf), colT(A_WIDTH)),
        (jax.ShapeDtypeStruct((B, IDX_WIDTH, S), bf), colT(IDX_WIDTH)),
        (jax.ShapeDtypeStruct((B, IDX_HEADS, S), f32), colT(IDX_HEADS)),
        (jax.ShapeDtypeStruct((T, B_WIDTH), f32), row(B_WIDTH)),
        (jax.ShapeDtypeStruct((T, B_WIDTH), f32), row(B_WIDTH)),
        (jax.ShapeDtypeStruct((T, B_WIDTH), f32), row(B_WIDTH)),
        (jax.ShapeDtypeStruct((T, B_WIDTH), f32), row(B_WIDTH)),
        (jax.ShapeDtypeStruct((T, D), bf), row(D)),
        (jax.ShapeDtypeStruct((T, D), bf), row(D)),
    ]
    ins = [x2, gain.reshape(1, D), wk, wik, wqT, wvT, wiqT, wiwT, wb, wg]
    in_specs = [row(D)] + [full(a) for a in ins[1:]]
    return pl.pallas_call(
        _inproj_kernel,
        grid=(B, nS),
        in_specs=in_specs,
        out_specs=[s for _, s in outs],
        out_shape=[s for s, _ in outs],
        compiler_params=_cparams(("parallel", "parallel")),
        name="inproj",
    )(*ins)


def _t5_bucket_table(n):
    d = np.arange(n)
    max_exact = REL_BUCKETS // 2
    nf = np.maximum(d, 1).astype(np.float64)
    large = max_exact + (np.log(nf / max_exact) / math.log(REL_MAX_DIST / max_exact)
                         * (REL_BUCKETS - max_exact)).astype(np.int32)
    large = np.minimum(large, REL_BUCKETS - 1)
    return np.where(d < max_exact, d, large)


def _dsa_kernel(qT_ref, k_ref, vT_ref, iqT_ref, iwT_ref, ik_ref, enear_ref,
                o_ref, sc_ref, qh_scr, m_scr, acc_scr, *, topk):
    TQ = qT_ref.shape[2]
    KC = TQ
    i = pl.program_id(1)
    nch = i + 1
    q0 = i * TQ
    f32 = jnp.float32
    bf16 = jnp.bfloat16
    key_id = lax.broadcasted_iota(jnp.int32, (KC, TQ), 0)
    qry_id = lax.broadcasted_iota(jnp.int32, (KC, TQ), 1)

    def col_reduce(x, op):
        return op(_fold_rows(x, op), axis=0, keepdims=True)

    iw = iwT_ref[0]

    def score_chunk(c, carry):
        rmin, rmax = carry
        k0 = pl.multiple_of(c * KC, KC)
        ik = ik_ref[pl.ds(k0, KC), :]
        acc = jnp.zeros((KC, TQ), f32)
        for h in range(IDX_HEADS):
            sh = jnp.dot(ik, iqT_ref[0, h * IDX_DIM:(h + 1) * IDX_DIM, :], preferred_element_type=f32)
            acc = acc + jnp.maximum(sh, 0.0) * iw[h:h + 1, :]
        valid = (k0 + key_id) <= (q0 + qry_id)
        sc_ref[pl.ds(k0, KC), :] = jnp.where(valid, acc, MASK_NEG)
        rmin = jnp.minimum(rmin, _fold_rows(jnp.where(valid, acc, -MASK_NEG), jnp.min))
        rmax = jnp.maximum(rmax, _fold_rows(jnp.where(valid, acc, MASK_NEG), jnp.max))
        return rmin, rmax

    rmin8, rmax8 = lax.fori_loop(
        0, nch, score_chunk,
        (jnp.full((V7X_SUBLANES, TQ), -MASK_NEG, f32), jnp.full((V7X_SUBLANES, TQ), MASK_NEG, f32)))
    rmin = jnp.min(rmin8, axis=0, keepdims=True)
    rmax = jnp.max(rmax8, axis=0, keepdims=True)

    def count_where(pred_fn):
        def body(c, acc):
            k0 = pl.multiple_of(c * KC, KC)
            blk = sc_ref[pl.ds(k0, KC), :]
            return acc + _fold_rows(jnp.where(pred_fn(blk), 1.0, 0.0), jnp.sum)
        acc = lax.fori_loop(0, nch, body, jnp.zeros((V7X_SUBLANES, TQ), f32))
        return jnp.sum(acc, axis=0, keepdims=True)

    def band_min_max(lo, hi):
        def body(c, carry):
            bmin, bmax = carry
            k0 = pl.multiple_of(c * KC, KC)
            blk = sc_ref[pl.ds(k0, KC), :]
            bmin = jnp.minimum(bmin, _fold_rows(jnp.where(blk >= lo, blk, -MASK_NEG), jnp.min))
            bmax = jnp.maximum(bmax, _fold_rows(jnp.where(blk < hi, blk, MASK_NEG), jnp.max))
            return bmin, bmax
        bmin8, bmax8 = lax.fori_loop(
            0, nch, body,
            (jnp.full((V7X_SUBLANES, TQ), -MASK_NEG, f32), jnp.full((V7X_SUBLANES, TQ), MASK_NEG, f32)))
        return jnp.min(bmin8, axis=0, keepdims=True), jnp.max(bmax8, axis=0, keepdims=True)

    kf = float(topk)
    n_valid = (q0 + 1 + lax.broadcasted_iota(jnp.int32, (1, TQ), 1)).astype(f32)
    lo0 = rmin
    cnt0 = n_valid
    hi0 = rmax + jnp.maximum(jnp.abs(rmax) * 2.0 ** -20, 1e-30)
    done0 = jnp.where(cnt0 <= kf, 1.0, 0.0)

    def probe(st, lo_s, mid, tie):
        it, lo, hi, cnt, done = st
        active = done < 0.5
        lo_s = jnp.where(active, lo_s, lo)
        c = count_where(lambda blk: blk >= mid)
        feas = c >= kf
        move = active & jnp.logical_not(tie)
        lo_n = jnp.where(move & feas, mid, lo_s)
        cnt_n = jnp.where(move & feas, c, cnt)
        hi_n = jnp.where(move & jnp.logical_not(feas), mid, hi)
        done_n = jnp.where((active & tie) | (cnt_n <= kf), 1.0, done)
        return it + 1, lo_n, hi_n, cnt_n, done_n

    def halve(st):
        _, lo, hi, _, _ = st
        half = lo + 0.5 * (hi - lo)
        stuck = (half <= lo) | (half >= hi)
        return probe(st, lo, half, stuck)

    def snap(st):
        _, lo, hi, _, _ = st
        bmin, bmax = band_min_max(lo, hi)
        mid = bmin + 0.5 * (bmax - bmin)
        return probe(st, bmin, jnp.where(mid <= bmin, bmax, mid), bmax <= bmin)

    st = lax.while_loop(lambda st: (jnp.min(st[-1]) < 0.5) & (st[0] < BISECT_FAST_ITERS), halve,
                        (jnp.int32(0), lo0, hi0, cnt0, done0))
    _, thr, _, cnt_thr, _ = lax.while_loop(lambda st: jnp.min(st[-1]) < 0.5, snap, st)

    tie_overflow = jnp.max(cnt_thr) > kf

    @pl.when(jnp.logical_not(tie_overflow))
    def _():
        def mask_chunk(c, _):
            k0 = pl.multiple_of(c * KC, KC)
            sc_ref[pl.ds(k0, KC), :] = jnp.where(sc_ref[pl.ds(k0, KC), :] >= thr, 0.0, MASK_NEG)
            return 0
        lax.fori_loop(0, nch, mask_chunk, 0)

    @pl.when(tie_overflow)
    def _():
        need = kf - count_where(lambda blk: blk > thr)
        tril = jnp.where(lax.broadcasted_iota(jnp.int32, (KC, KC), 1)
                         <= lax.broadcasted_iota(jnp.int32, (KC, KC), 0), 1.0, 0.0).astype(bf16)

        def mask_chunk(c, run):
            k0 = pl.multiple_of(c * KC, KC)
            blk = sc_ref[pl.ds(k0, KC), :]
            eq = jnp.where(blk == thr, 1.0, 0.0)
            pref = jnp.dot(tril, eq.astype(bf16), preferred_element_type=f32)
            sel = (blk > thr) | ((eq > 0.5) & (run + pref <= need))
            sc_ref[pl.ds(k0, KC), :] = jnp.where(sel, 0.0, MASK_NEG)
            return run + pref[KC - 1:KC, :]

        lax.fori_loop(0, nch, mask_chunk, jnp.zeros((1, TQ), f32))

    AK = min(ATT_KC, TQ)
    per = TQ // AK
    head0_q = (lax.broadcasted_iota(jnp.int32, (V7X_LANES, TQ), 0) // A_HEAD_DIM) == 0
    n_pairs = A_HEADS // 2

    m_scr[...] = jnp.full(m_scr.shape, MASK_NEG, f32)
    acc_scr[...] = jnp.zeros(acc_scr.shape, f32)
    v_row = lax.broadcasted_iota(jnp.int32, (V7X_LANES, AK), 0)
    denom_row = [A_HEAD_DIM * (1 - sub) for sub in range(2)]
    for p in range(n_pairs):
        q_pair = qT_ref[0, p * V7X_LANES:(p + 1) * V7X_LANES, :]
        zq = jnp.zeros_like(q_pair)
        qh_scr[2 * p] = jnp.where(head0_q, q_pair, zq)
        qh_scr[2 * p + 1] = jnp.where(head0_q, zq, q_pair)

    def step(c, bias_rows):
        k0 = pl.multiple_of(c * AK, AK)
        msk = sc_ref[pl.ds(k0, AK), :]
        for p in range(n_pairs):
            kp = k_ref[pl.ds(k0, AK), p * V7X_LANES:(p + 1) * V7X_LANES]
            vp = vT_ref[0, p * V7X_LANES:(p + 1) * V7X_LANES, pl.ds(k0, AK)]
            for sub in range(2):
                h = 2 * p + sub
                s = jnp.dot(kp, qh_scr[h], preferred_element_type=f32) + msk
                if bias_rows is not None:
                    s = s + enear_ref[h, bias_rows, :]
                m = m_scr[h:h + 1, :]
                m_new = jnp.maximum(m, col_reduce(s, jnp.max))
                alpha = jnp.exp2(m - m_new)
                pr = jnp.exp2(s - m_new)
                m_scr[h:h + 1, :] = m_new
                v_aug = jnp.where(v_row == denom_row[sub], jnp.ones_like(vp), vp)
                acc_scr[h] = alpha * acc_scr[h] + jnp.dot(v_aug, pr.astype(bf16), preferred_element_type=f32)

    def far(blk, _):
        for jj in range(per):
            step(blk * per + jj, None)
        return 0

    def near(block, first_chunk):
        for jj in range(per):
            step(first_chunk + jj, slice(block * TQ + jj * AK, block * TQ + (jj + 1) * AK))

    lax.fori_loop(0, jnp.maximum(i - 1, 0), far, 0)

    @pl.when(i >= 1)
    def _():
        near(0, (i - 1) * per)

    near(1, i * per)
    for p in range(n_pairs):
        outs = [acc_scr[2 * p + sub] / acc_scr[2 * p + sub, denom_row[sub]:denom_row[sub] + 1, :]
                for sub in range(2)]
        o_pair = jnp.where(head0_q, outs[0], outs[1])
        o_ref[:, p * V7X_LANES:(p + 1) * V7X_LANES] = o_pair.T.astype(o_ref.dtype)


def _dsa(k, ik, qT, vT, iqT, iwT, rel_bias, B, S):
    T = k.shape[0]
    TQ = min(ATT_Q, S)
    nQ = S // TQ
    topk = min(TOPK_MAX, S // 4)
    buckets = _t5_bucket_table(2 * TQ + 1)
    assert np.all(_t5_bucket_table(S + 1)[TQ + 1:] == REL_BUCKETS - 1)
    j = np.arange(2 * TQ)[:, None]
    r = np.arange(TQ)[None, :]
    dist = np.maximum(r + TQ - j, 0)
    onehot = (jnp.asarray(buckets[dist], jnp.int32)[None]
              == jnp.arange(REL_BUCKETS, dtype=jnp.int32)[:, None, None]).astype(jnp.float32)
    rel = (rel_bias.astype(jnp.float32) - rel_bias[REL_BUCKETS - 1].astype(jnp.float32)[None, :]) * LOG2_E
    enear = jnp.einsum('nh,njr->hjr', rel, onehot, precision=lax.Precision.HIGHEST)

    return pl.pallas_call(
        functools.partial(_dsa_kernel, topk=topk),
        grid=(B, nQ),
        in_specs=[
            pl.BlockSpec((1, A_WIDTH, TQ), lambda b, i: (b, 0, i)),
            pl.BlockSpec((S, A_WIDTH), lambda b, i: (b, 0)),
            pl.BlockSpec((1, A_WIDTH, S), lambda b, i: (b, 0, 0)),
            pl.BlockSpec((1, IDX_WIDTH, TQ), lambda b, i: (b, 0, i)),
            pl.BlockSpec((1, IDX_HEADS, TQ), lambda b, i: (b, 0, i)),
            pl.BlockSpec((S, IDX_DIM), lambda b, i: (b, 0)),
            pl.BlockSpec((A_HEADS, 2 * TQ, TQ), lambda b, i: (0, 0, 0)),
        ],
        out_specs=pl.BlockSpec((TQ, A_WIDTH), lambda b, i: (b * nQ + i, 0)),
        scratch_shapes=[pltpu.VMEM((S, TQ), jnp.float32),
                        pltpu.VMEM((A_HEADS, V7X_LANES, TQ), jnp.bfloat16),
                        pltpu.VMEM((A_HEADS, TQ), jnp.float32),
                        pltpu.VMEM((A_HEADS, V7X_LANES, TQ), jnp.float32)],
        out_shape=jax.ShapeDtypeStruct((T, A_WIDTH), jnp.bfloat16),
        compiler_params=_cparams(("parallel", "arbitrary")),
        name="dsa",
    )(qT, k, vT, iqT, iwT, ik, enear)


def _hgrn_kernel(bq_ref, bf_ref, bi_ref, bg_ref, lb_ref, gain_ref, o_ref,
                 st_ref, b_scr, q_scr, k_scr, v_scr, oi_scr, qd_s, kd_s, kl_s, vv_s, dec_s, oi_s, upd_s, st_s):
    R = bq_ref.shape[0]
    C = HGRN_CHUNK
    nC = R // C
    f32 = jnp.float32
    bf16 = jnp.bfloat16
    h = pl.program_id(1)

    @pl.when(pl.program_id(2) == 0)
    def _():
        st_ref[...] = jnp.zeros_like(st_ref)

    lb = lb_ref[pl.ds(h, 1), :]
    gain = gain_ref[pl.ds(h, 1), :]
    tril_incl = jnp.where(lax.broadcasted_iota(jnp.int32, (C, C), 1)
                          <= lax.broadcasted_iota(jnp.int32, (C, C), 0), 1.0, 0.0)
    srow = lax.broadcasted_iota(jnp.int32, (C, B_KEY_DIM), 0)

    def gates(r0):
        f = lb + (1.0 - lb) * _sigmoid(bf_ref[pl.ds(r0, C), :])
        qr = bq_ref[pl.ds(r0, C), :]
        return jnp.log(f), 1.0 - f, qr * _sigmoid(qr) * (B_KEY_DIM ** -0.5), bi_ref[pl.ds(r0, C), :]

    def cumdecay(g):
        tri = tril_incl.astype(bf16)
        g_hi = g.astype(bf16)
        rest = g - g_hi.astype(f32)
        g_mid = rest.astype(bf16)
        g_lo = (rest - g_mid.astype(f32)).astype(bf16)
        return (jnp.dot(tri, g_hi, preferred_element_type=f32) + jnp.dot(tri, g_mid, preferred_element_type=f32)
                + jnp.dot(tri, g_lo, preferred_element_type=f32))

    def advance(r0, st, qd, o_intra, upd, decay_row):
        o_inter = lax.dot_general(qd, st.astype(bf16), (((1,), (1,)), ((), ())), preferred_element_type=f32)
        og = bg_ref[pl.ds(r0, C), :]
        y = _rms(o_inter + o_intra, gain) * (og * _sigmoid(og))
        o_ref[pl.ds(r0, C), :] = y.astype(o_ref.dtype)
        return st * decay_row + upd

    f_all = lb + (1.0 - lb) * _sigmoid(bf_ref[...])
    g_all = jnp.log(f_all)
    decay = jnp.sum(g_all.reshape(nC, C, B_KEY_DIM), axis=1)
    safe = jnp.min(decay) >= -HGRN_SAFE_DECAY

    @pl.when(safe)
    def _():
        qr = bq_ref[...]
        qq = qr * _sigmoid(qr) * (B_KEY_DIM ** -0.5)
        kk = 1.0 - f_all
        b = jnp.concatenate([cumdecay(g_all[c * C:(c + 1) * C]) for c in range(nC)], axis=0)
        b_end = jnp.concatenate([jnp.broadcast_to(b[(c + 1) * C - 1:(c + 1) * C], (C, B_KEY_DIM))
                                 for c in range(nC)], axis=0)
        qd_s[...] = (qq * jnp.exp(b)).astype(bf16)
        kd_s[...] = (kk * jnp.exp(-b)).astype(bf16)
        kl_s[...] = (kk * jnp.exp(b_end - b)).astype(bf16)
        vv_s[...] = bi_ref[...].astype(bf16)
        dec_s[...] = jnp.exp(b_end)
        for c in range(nC):
            rows = slice(c * C, (c + 1) * C)
            att = lax.dot_general(qd_s[rows], kd_s[rows], (((1,), (1,)), ((), ())),
                                  preferred_element_type=f32) * tril_incl
            oi_s[rows] = jnp.dot(att.astype(bf16), vv_s[rows], preferred_element_type=f32)
            upd_s[c] = lax.dot_general(vv_s[rows], kl_s[rows], (((0,), (0,)), ((), ())),
                                       preferred_element_type=f32)
        st = st_ref[...]
        for c in range(nC):
            st_s[c] = st.astype(bf16)
            st = st * dec_s[c * C:c * C + 1] + upd_s[c]
        st_ref[...] = st
        for c in range(nC):
            rows = slice(c * C, (c + 1) * C)
            oi_s[rows] = oi_s[rows] + lax.dot_general(qd_s[rows], st_s[c], (((1,), (1,)), ((), ())),
                                                      preferred_element_type=f32)
        og = bg_ref[...]
        o_ref[...] = (_rms(oi_s[...], gain) * (og * _sigmoid(og))).astype(o_ref.dtype)

    @pl.when(jnp.logical_not(safe))
    def _():
        def body(c, st):
            r0 = pl.multiple_of(c * C, C)
            g, kk, qq, vv = gates(r0)
            b = cumdecay(g)
            b_last = b[C - 1:C, :]
            b_scr[...] = b
            q_scr[...] = qq
            k_scr[...] = kk
            v_scr[...] = vv

            def row(t, _):
                bt = b_scr[pl.ds(t, 1), :]
                qt = q_scr[pl.ds(t, 1), :]
                ex = jnp.where(srow <= t, bt - b_scr[...], -jnp.inf)
                a = jnp.sum(qt * k_scr[...] * jnp.exp(ex), axis=1, keepdims=True)
                oi_scr[pl.ds(t, 1), :] = jnp.sum(a * v_scr[...], axis=0, keepdims=True)
                return 0
            lax.fori_loop(0, C, row, 0)
            kd_last = (kk * jnp.exp(b_last - b)).astype(bf16)
            upd = lax.dot_general(vv.astype(bf16), kd_last, (((0,), (0,)), ((), ())),
                                  preferred_element_type=f32)
            return advance(r0, st, (qq * jnp.exp(b)).astype(bf16), oi_scr[...], upd, jnp.exp(b_last))
        st_ref[...] = lax.fori_loop(0, nC, body, st_ref[...])


def _hgrn(bq, bf, bi, bg, lb, gain, B, S):
    T = bq.shape[0]
    R = min(HGRN_ROWS, S)
    nR = S // R
    C = HGRN_CHUNK
    blk = pl.BlockSpec((R, B_KEY_DIM), lambda b, h, c: (b * nR + c, h))
    small = pl.BlockSpec((B_HEADS, B_KEY_DIM), lambda b, h, c: (0, 0))
    f32 = jnp.float32
    return pl.pallas_call(
        _hgrn_kernel,
        grid=(B, B_HEADS, nR),
        in_specs=[blk, blk, blk, blk, small, small],
        out_specs=blk,
        out_shape=jax.ShapeDtypeStruct((T, B_WIDTH), jnp.bfloat16),
        scratch_shapes=[pltpu.VMEM((B_VAL_DIM, B_KEY_DIM), f32)] +
                       [pltpu.VMEM((C, B_KEY_DIM), f32) for _ in range(5)] +
                       [pltpu.VMEM((R, B_KEY_DIM), jnp.bfloat16) for _ in range(4)] +
                       [pltpu.VMEM((R, B_KEY_DIM), f32) for _ in range(2)] +
                       [pltpu.VMEM((R // C, B_VAL_DIM, B_KEY_DIM), f32),
                        pltpu.VMEM((R // C, B_VAL_DIM, B_KEY_DIM), jnp.bfloat16)],
        compiler_params=_cparams(("parallel", "parallel", "arbitrary")),
        name="hgrn",
    )(bq, bf, bi, bg, lb, gain)


def _merge_kernel(x_ref, ya_ref, yb_ref, ga_ref, gb_ref, wa_ref, wb_ref, wo_ref, g_ref, wrh_ref, wrl_ref, br_ref,
                  x1_ref, xn_ref, lg_ref):
    f32 = jnp.float32
    ma = jnp.dot(ya_ref[...], wa_ref[...], preferred_element_type=f32)
    mb = jnp.dot(yb_ref[...], wb_ref[...], preferred_element_type=f32)
    merged = ga_ref[...].astype(f32) * ma + gb_ref[...].astype(f32) * mb
    x1 = x_ref[...] + jnp.dot(merged.astype(jnp.bfloat16), wo_ref[...], preferred_element_type=f32)
    x1_ref[...] = x1
    hn = _rms(x1, g_ref[...])
    xn_ref[...] = _pack_bf16_pairs(hn)
    hn_hi = hn.astype(jnp.bfloat16)
    hn_lo = (hn - hn_hi.astype(f32)).astype(jnp.bfloat16)
    lg_ref[...] = (jnp.dot(hn_hi, wrh_ref[...], preferred_element_type=f32)
                   + jnp.dot(hn_lo, wrh_ref[...], preferred_element_type=f32)
                   + jnp.dot(hn_hi, wrl_ref[...], preferred_element_type=f32) + br_ref[...])


def _merge(x2, ya, yb, ga, gb, w_up_a, w_up_b, w_out, gain, w_router, b_router):
    T, D = x2.shape
    R = min(PROJ_ROWS, T)
    bf = jnp.bfloat16
    wr_hi = w_router.astype(bf)
    wr_lo = (w_router - wr_hi.astype(jnp.float32)).astype(bf)
    ins = [x2, ya, yb, ga, gb, w_up_a.astype(bf), w_up_b.astype(bf), w_out.astype(bf),
           gain.reshape(1, D), wr_hi, wr_lo, b_router.reshape(1, N_EXPERTS)]
    row = lambda n: pl.BlockSpec((R, n), lambda i: (i, 0))
    full = lambda a: pl.BlockSpec(a.shape, lambda i: (0,) * a.ndim)
    in_specs = [row(D), row(A_WIDTH), row(B_WIDTH), row(D), row(D)] + [full(a) for a in ins[5:]]
    return pl.pallas_call(
        _merge_kernel,
        grid=(T // R,),
        in_specs=in_specs,
        out_specs=[row(D), row(D // 2), row(N_EXPERTS)],
        out_shape=[jax.ShapeDtypeStruct((T, D), jnp.float32), jax.ShapeDtypeStruct((T, D // 2), jnp.uint32),
                   jax.ShapeDtypeStruct((T, N_EXPERTS), jnp.float32)],
        compiler_params=_cparams(("parallel",)),
        name="merge",
    )(*ins)


def _route_kernel(lg_ref, eidx_ref, gate_ref, rank_ref, cnt_ref, run_ref):
    R = lg_ref.shape[0]
    f32 = jnp.float32

    @pl.when(pl.program_id(0) == 0)
    def _():
        run_ref[...] = jnp.zeros_like(run_ref)

    lg = lg_ref[...].T
    expert = lax.broadcasted_iota(jnp.int32, (N_EXPERTS, R), 0)
    work = lg
    onehots, vals, idxs = [], [], []
    for _ in range(TOP_K):
        m = jnp.max(work, axis=0, keepdims=True)
        idx = jnp.min(jnp.where(work == m, expert, N_EXPERTS), axis=0, keepdims=True)
        oh = expert == idx
        onehots.append(oh)
        vals.append(m)
        idxs.append(idx)
        work = jnp.where(oh, -jnp.inf, work)
    ex = [jnp.exp(v - vals[0]) for v in vals]
    den = ex[0] + ex[1] + ex[2] + ex[3]
    chosen = jnp.where(onehots[0] | onehots[1] | onehots[2] | onehots[3], 1.0, 0.0)
    earlier = jnp.where(lax.broadcasted_iota(jnp.int32, (R, R), 0)
                        < lax.broadcasted_iota(jnp.int32, (R, R), 1), 1.0, 0.0).astype(jnp.bfloat16)
    before = jnp.dot(chosen.astype(jnp.bfloat16), earlier, preferred_element_type=f32) + run_ref[...]
    slot = lax.broadcasted_iota(jnp.int32, (TOP_K, R), 0)
    eidx = jnp.zeros((TOP_K, R), jnp.int32)
    gate = jnp.zeros((TOP_K, R), f32)
    rank = jnp.zeros((TOP_K, R), f32)
    for k in range(TOP_K):
        eidx = jnp.where(slot == k, idxs[k], eidx)
        gate = jnp.where(slot == k, ex[k] / den, gate)
        rk = jnp.sum(jnp.where(onehots[k], before, 0.0), axis=0, keepdims=True)
        rank = jnp.where(slot == k, rk, rank)
    eidx_ref[...] = eidx
    gate_ref[...] = gate
    rank_ref[...] = rank.astype(jnp.int32)
    run_ref[...] = run_ref[...] + jnp.sum(chosen, axis=1, keepdims=True)
    cnt_ref[...] = run_ref[...].astype(jnp.int32)


def _route(logits):
    T = logits.shape[0]
    R = min(ROUTE_ROWS, T)
    col = pl.BlockSpec((TOP_K, R), lambda i: (0, i))
    return pl.pallas_call(
        _route_kernel,
        grid=(T // R,),
        in_specs=[pl.BlockSpec((R, N_EXPERTS), lambda i: (i, 0))],
        out_specs=[col, col, col, pl.BlockSpec((N_EXPERTS, 1), lambda i: (0, 0))],
        out_shape=[jax.ShapeDtypeStruct((TOP_K, T), jnp.int32), jax.ShapeDtypeStruct((TOP_K, T), jnp.float32),
                   jax.ShapeDtypeStruct((TOP_K, T), jnp.int32), jax.ShapeDtypeStruct((N_EXPERTS, 1), jnp.int32)],
        scratch_shapes=[pltpu.VMEM((N_EXPERTS, 1), jnp.float32)],
        compiler_params=_cparams(("arbitrary",)),
        name="route",
    )(logits)


def _sc_gather_rows(table, idx):
    N, W = table.shape
    M = idx.shape[0]
    workers = V7X_SC_CORES * V7X_SC_SUBCORES
    per_worker = M // workers
    pieces = per_worker // SC_GATHER_ROWS
    assert per_worker * workers == M and pieces * SC_GATHER_ROWS == per_worker and pieces % 2 == 0
    mesh = plsc.VectorSubcoreMesh(core_axis_name="core", subcore_axis_name="subcore",
                                  num_cores=V7X_SC_CORES, num_subcores=V7X_SC_SUBCORES)

    @functools.partial(
        pl.kernel, mesh=mesh,
        out_type=jax.ShapeDtypeStruct((M, W), table.dtype),
        scratch_types=[pltpu.VMEM((per_worker,), jnp.int32),
                       pltpu.VMEM((SC_GATHER_ROWS, W), table.dtype),
                       pltpu.VMEM((SC_GATHER_ROWS, W), table.dtype),
                       pltpu.SemaphoreType.DMA, pltpu.SemaphoreType.DMA],
    )
    def gather(table_hbm, idx_hbm, out_hbm, idx_v, rows_a, rows_b, sem_a, sem_b):
        worker = lax.axis_index("subcore") * V7X_SC_CORES + lax.axis_index("core")
        base = pl.multiple_of(worker * per_worker, SC_GATHER_ROWS)
        pltpu.sync_copy(idx_hbm.at[pl.ds(base, per_worker)], idx_v)
        bufs = ((rows_a, sem_a), (rows_b, sem_b))

        def fetch(g, buf, sem):
            off = pl.multiple_of(g * SC_GATHER_ROWS, SC_GATHER_ROWS)
            return pltpu.make_async_copy(table_hbm.at[idx_v.at[pl.ds(off, SC_GATHER_ROWS)]], buf, sem)

        fetch(0, *bufs[0]).start()

        @pl.loop(0, pieces, step=2)
        def _(g0):
            for half in range(2):
                g = g0 + half
                buf, sem = bufs[half]
                fetch(g, buf, sem).wait()

                @pl.when(g + 1 < pieces)
                def _():
                    fetch(g + 1, *bufs[1 - half]).start()

                off = pl.multiple_of(g * SC_GATHER_ROWS, SC_GATHER_ROWS)
                pltpu.sync_copy(buf, out_hbm.at[pl.ds(base + off, SC_GATHER_ROWS)])

    return gather(table, idx)


def _sc_scatter_rows(rows, dest, n_out):
    T, W = rows.shape
    slots = dest.shape[0]
    workers = V7X_SC_CORES * V7X_SC_SUBCORES
    per_worker = T // workers
    pieces = per_worker // SC_GATHER_ROWS
    assert per_worker * workers == T and pieces * SC_GATHER_ROWS == per_worker and pieces % 2 == 0
    idx = dest.reshape(slots, workers, pieces, SC_GATHER_ROWS).transpose(1, 2, 0, 3)
    mesh = plsc.VectorSubcoreMesh(core_axis_name="core", subcore_axis_name="subcore",
                                  num_cores=V7X_SC_CORES, num_subcores=V7X_SC_SUBCORES)

    @functools.partial(
        pl.kernel, mesh=mesh,
        out_type=jax.ShapeDtypeStruct((n_out, W), rows.dtype),
        scratch_types=[pltpu.VMEM((pieces, slots, SC_GATHER_ROWS), jnp.int32),
                       pltpu.VMEM((SC_GATHER_ROWS, W), rows.dtype),
                       pltpu.VMEM((SC_GATHER_ROWS, W), rows.dtype),
                       pltpu.SemaphoreType.DMA, pltpu.SemaphoreType.DMA, pltpu.SemaphoreType.DMA],
    )
    def scatter(rows_hbm, idx_hbm, out_hbm, idx_v, rows_a, rows_b, sem_a, sem_b, sem_out):
        worker = lax.axis_index("subcore") * V7X_SC_CORES + lax.axis_index("core")
        base = pl.multiple_of(worker * per_worker, SC_GATHER_ROWS)
        pltpu.sync_copy(idx_hbm.at[worker], idx_v)
        bufs = ((rows_a, sem_a), (rows_b, sem_b))

        def fetch(g, buf, sem):
            off = pl.multiple_of(g * SC_GATHER_ROWS, SC_GATHER_ROWS)
            return pltpu.make_async_copy(rows_hbm.at[pl.ds(base + off, SC_GATHER_ROWS)], buf, sem)

        fetch(0, *bufs[0]).start()

        @pl.loop(0, pieces, step=2)
        def _(g0):
            for half in range(2):
                g = g0 + half
                buf, sem = bufs[half]
                fetch(g, buf, sem).wait()

                @pl.when(g + 1 < pieces)
                def _():
                    fetch(g + 1, *bufs[1 - half]).start()

                puts = [pltpu.make_async_copy(buf, out_hbm.at[idx_v.at[g, k]], sem_out) for k in range(slots)]
                for put in puts:
                    put.start()
                for put in puts:
                    put.wait()

    return scatter(rows, idx)


def _experts_kernel(be_ref, nb_ref, first_ref, slot_ref, next_ref,
                    x_ref, wgu_hbm, bgu_ref, wd_hbm, bd_ref, o_ref,
                    wgu_f32, wd_f32, wgu_bf, wd_bf, sems):
    f32 = jnp.float32
    d_ff = wd_bf.shape[0]
    i = pl.program_id(0)
    live = i < nb_ref[0]

    def weight_copies(e, s):
        return (pltpu.make_async_copy(wgu_hbm.at[e], wgu_f32.at[s], sems.at[s, 0]),
                pltpu.make_async_copy(wd_hbm.at[e], wd_f32.at[s], sems.at[s, 1]))

    @pl.when(live & (first_ref[i] == 1))
    def _():
        e = be_ref[i]
        s = slot_ref[i]

        @pl.when(i == 0)
        def _():
            for cp in weight_copies(e, s):
                cp.start()

        for cp in weight_copies(e, s):
            cp.wait()

        @pl.when(next_ref[i] >= 0)
        def _():
            for cp in weight_copies(next_ref[i], 1 - s):
                cp.start()

        wgu_bf[...] = wgu_f32[s].astype(jnp.bfloat16)
        wd_bf[...] = wd_f32[s].astype(jnp.bfloat16)

    @pl.when(live)
    def _():
        x_hi, x_lo = _unpack_bf16_pairs(x_ref[...])
        half = x_hi.shape[1]
        gu = (jnp.dot(x_hi, wgu_bf[:half, :], preferred_element_type=f32)
              + jnp.dot(x_lo, wgu_bf[half:, :], preferred_element_type=f32) + bgu_ref[0])
        gate = jnp.minimum(gu[:, :d_ff], SWIGLU_LIMIT)
        lin = jnp.clip(gu[:, d_ff:], -SWIGLU_LIMIT, SWIGLU_LIMIT)
        act = (lin + 1.0) * gate * _sigmoid(SWIGLU_ALPHA * gate)
        y = jnp.dot(act.astype(jnp.bfloat16), wd_bf[...], preferred_element_type=f32) + bd_ref[0]
        o_ref[...] = _pack_bf16_pairs(y)

    @pl.when(pl.program_id(0) >= nb_ref[0])
    def _():
        o_ref[...] = jnp.zeros_like(o_ref)


def _experts(xs, plan, w_gu, b_gu, w_down, b_down):
    P, W = xs.shape
    E, D, F2 = w_gu.shape
    nb = P // EXPERT_ROWS
    by_expert = lambda i, be, *_: (be[i], 0, 0)
    grid_spec = pltpu.PrefetchScalarGridSpec(
        num_scalar_prefetch=5,
        grid=(nb,),
        in_specs=[
            pl.BlockSpec((EXPERT_ROWS, W), lambda i, *_: (i, 0)),
            pl.BlockSpec(memory_space=pl.ANY),
            pl.BlockSpec((1, 1, F2), by_expert),
            pl.BlockSpec(memory_space=pl.ANY),
            pl.BlockSpec((1, 1, D), by_expert),
        ],
        out_specs=pl.BlockSpec((EXPERT_ROWS, W), lambda i, *_: (i, 0)),
        scratch_shapes=[pltpu.VMEM((2, D, F2), jnp.float32), pltpu.VMEM((2, F2 // 2, D), jnp.float32),
                        pltpu.VMEM((D, F2), jnp.bfloat16), pltpu.VMEM((F2 // 2, D), jnp.bfloat16),
                        pltpu.SemaphoreType.DMA((2, 2))],
    )
    return pl.pallas_call(
        _experts_kernel,
        grid_spec=grid_spec,
        out_shape=jax.ShapeDtypeStruct((P, W), jnp.uint32),
        compiler_params=_cparams(("arbitrary",)),
        name="experts",
    )(*plan, xs, w_gu, b_gu.reshape(E, 1, F2), w_down, b_down.reshape(E, 1, D))


def _combine_kernel(ya_ref, x1_ref, gate_ref, g_ref, o_ref):
    half = x1_ref.shape[1] // 2
    f32 = jnp.float32
    gate = gate_ref[...].T
    x1 = x1_ref[...]
    y_hi = x1[:, :half]
    y_lo = x1[:, half:]
    for k in range(TOP_K):
        hi, lo = _unpack_bf16_pairs(ya_ref[k])
        y_hi = y_hi + gate[:, k:k + 1] * hi.astype(f32)
        y_lo = y_lo + gate[:, k:k + 1] * lo.astype(f32)
    o_ref[...] = _rms(jnp.concatenate([y_hi, y_lo], axis=1), g_ref[...])


def _combine(ya, x1, gates, gain):
    T, D = x1.shape
    R = min(COMBINE_ROWS, T)
    row = lambda w: pl.BlockSpec((R, w), lambda i: (i, 0))
    return pl.pallas_call(
        _combine_kernel,
        grid=(T // R,),
        in_specs=[pl.BlockSpec((TOP_K, R, D // 2), lambda i: (0, i, 0)), row(D),
                  pl.BlockSpec((TOP_K, R), lambda i: (0, i)), pl.BlockSpec((1, D), lambda i: (0, 0))],
        out_specs=row(D),
        out_shape=jax.ShapeDtypeStruct((T, D), jnp.float32),
        compiler_params=_cparams(("parallel",)),
        name="combine",
    )(ya, x1, gates, gain.reshape(1, D))


def _moe_plan(eidx, rank, counts, A):
    counts = counts.reshape(N_EXPERTS)
    padded = (counts + EXPERT_ROWS - 1) // EXPERT_ROWS * EXPERT_ROWS
    pad_ends = jnp.cumsum(padded)
    pad_starts = pad_ends - padded
    n_blocks = -(-A // EXPERT_ROWS) + N_EXPERTS
    dest = pad_starts[eidx] + rank
    block_start = jnp.arange(n_blocks, dtype=pad_ends.dtype) * EXPERT_ROWS
    block_expert = jnp.minimum(jnp.sum(pad_ends[None, :] <= block_start[:, None], axis=1),
                               N_EXPERTS - 1).astype(jnp.int32)
    n_used = (pad_ends[-1] // EXPERT_ROWS).astype(jnp.int32).reshape(1)
    has_rows = counts > 0
    ordinal = jnp.cumsum(has_rows.astype(jnp.int32)) - 1
    ids = jnp.arange(N_EXPERTS, dtype=jnp.int32)
    later = has_rows[None, :] & (ids[None, :] > ids[:, None])
    next_expert = jnp.where(jnp.any(later, axis=1), jnp.argmax(later, axis=1), -1).astype(jnp.int32)
    block_first = ((block_start == pad_starts[block_expert]) & (block_start < pad_ends[-1])).astype(jnp.int32)
    block_slot = (ordinal[block_expert] % 2).astype(jnp.int32)
    block_next = next_expert[block_expert]
    plan = (block_expert, n_used, block_first, block_slot, block_next)
    return dest.astype(jnp.int32), plan, n_blocks


def kernel(x, w_in, w_up_a, w_up_b, w_out, norm_mix, norm_ffn, norm_final, hgrn_norm,
           lb_logits, rel_bias, w_router, b_router, w_gu, b_gu, w_down, b_down):
    B, S, D = x.shape
    T = B * S
    assert w_in.shape[0] == 1, "the final rmsnorm is fused into the single layer's combine stage"
    lb_all = jnp.cumsum(jax.nn.softmax(lb_logits.astype(jnp.float32), axis=0), axis=0)
    x2 = x.reshape(T, D)
    (k, ik, qT, vT, iqT, iwT, bq, bf, bi, bg, ga, gb) = _inproj(x2, norm_mix[0], w_in[0], B, S)
    ya = _dsa(k, ik, qT, vT, iqT, iwT, rel_bias, B, S)
    yb = _hgrn(bq, bf, bi, bg, lb_all[0].reshape(B_HEADS, B_KEY_DIM), hgrn_norm[0], B, S)
    x1, xn, logits = _merge(x2, ya, yb, ga, gb, w_up_a[0], w_up_b[0], w_out[0], norm_ffn[0],
                            w_router[0], b_router[0])
    eidx, gates, rank, counts = _route(logits)
    dest, plan, n_blocks = _moe_plan(eidx, rank, counts, T * TOP_K)
    P = n_blocks * EXPERT_ROWS
    A = T * TOP_K
    xs = _sc_scatter_rows(xn, dest, P)
    y_buf = _experts(xs, plan, w_gu[0], b_gu[0], w_down[0], b_down[0])
    ya = _sc_gather_rows(y_buf, dest.reshape(A)).reshape(TOP_K, T, D // 2)
    out = _combine(ya, x1, gates, norm_final)
    return out.reshape(B, S, D)
```

```python
import functools
import math

import numpy as np
import jax
import jax.numpy as jnp
from jax import lax
from jax.experimental import pallas as pl
from jax.experimental.pallas import tpu as pltpu
from jax.experimental.pallas import tpu_sc as plsc

A_HEADS = 8
A_HEAD_DIM = 64
IDX_HEADS = 8
IDX_DIM = 32
TOPK_MAX = 256
REL_BUCKETS = 32
REL_MAX_DIST = 128
B_HEADS = 4
B_KEY_DIM = 128
B_VAL_DIM = 128
N_EXPERTS = 32
TOP_K = 4
SWIGLU_LIMIT = 7.0
SWIGLU_ALPHA = 1.702
EPS = 1e-6
LOG2_E = math.log2(math.e)

A_WIDTH = A_HEADS * A_HEAD_DIM
B_WIDTH = B_HEADS * B_VAL_DIM
IDX_WIDTH = IDX_HEADS * IDX_DIM

V7X_LANES = 128
V7X_SUBLANES = 8
V7X_VMEM_LIMIT_BYTES = 56 * 1024 * 1024
V7X_SC_CORES = 2
V7X_SC_SUBCORES = 16

PROJ_ROWS = 512
ATT_Q = 256
ATT_KC = 128
HGRN_ROWS = 1024
HGRN_CHUNK = 64
HGRN_SAFE_DECAY = 70.0
ROUTE_ROWS = 512
EXPERT_ROWS = 256
COMBINE_ROWS = 512
SC_GATHER_ROWS = 64
MASK_NEG = -1e30
BISECT_FAST_ITERS = 26


def _cparams(dims):
    return pltpu.CompilerParams(dimension_semantics=dims, vmem_limit_bytes=V7X_VMEM_LIMIT_BYTES)


def _rms(x, gain):
    return x * lax.rsqrt(jnp.mean(x * x, axis=-1, keepdims=True) + EPS) * gain


def _sigmoid(x):
    return 1.0 / (1.0 + jnp.exp(-x))


def _pack_bf16_pairs(x):
    n = x.shape[1] // 2
    as_bits = lambda v: lax.bitcast_convert_type(v.astype(jnp.bfloat16).astype(jnp.float32), jnp.uint32)
    return (as_bits(x[:, :n]) & jnp.uint32(0xFFFF0000)) | (as_bits(x[:, n:]) >> 16)


def _unpack_bf16_pairs(w):
    hi = lax.bitcast_convert_type(w & jnp.uint32(0xFFFF0000), jnp.float32).astype(jnp.bfloat16)
    lo = lax.bitcast_convert_type(w << 16, jnp.float32).astype(jnp.bfloat16)
    return hi, lo


def _fold_rows(x, op):
    return op(x.reshape(x.shape[0] // V7X_SUBLANES, V7X_SUBLANES, x.shape[1]), axis=0)


def _inproj_kernel(x_ref, g_ref, wk_ref, wik_ref, wqT_ref, wvT_ref, wiqT_ref, wiwT_ref, wb_ref, wg_ref,
                   k_ref, ik_ref, qT_ref, vT_ref, iqT_ref, iwT_ref, bq_ref, bf_ref, bi_ref, bg_ref,
                   ga_ref, gb_ref):
    x = x_ref[...]
    hn = _rms(x, g_ref[...]).astype(jnp.bfloat16)

    def mm(w_ref):
        return jnp.dot(hn, w_ref[...], preferred_element_type=jnp.float32)

    def mm_t(w_ref):
        return lax.dot_general(w_ref[...], hn, (((1,), (1,)), ((), ())),
                               preferred_element_type=jnp.float32)

    k_ref[...] = mm(wk_ref).astype(jnp.bfloat16)
    ik_ref[...] = mm(wik_ref).astype(jnp.bfloat16)
    qT_ref[0] = (mm_t(wqT_ref) * (A_HEAD_DIM ** -0.5 * LOG2_E)).astype(jnp.bfloat16)
    vT_ref[0] = mm_t(wvT_ref).astype(jnp.bfloat16)
    iqT_ref[0] = mm_t(wiqT_ref).astype(jnp.bfloat16)
    iwT_ref[0] = mm_t(wiwT_ref) * ((IDX_HEADS * IDX_DIM) ** -0.5)
    hb = mm(wb_ref)
    bq_ref[...] = hb[:, 0 * B_WIDTH:1 * B_WIDTH]
    bf_ref[...] = hb[:, 1 * B_WIDTH:2 * B_WIDTH]
    bi_ref[...] = hb[:, 2 * B_WIDTH:3 * B_WIDTH]
    bg_ref[...] = hb[:, 3 * B_WIDTH:4 * B_WIDTH]
    d = ga_ref.shape[-1]
    hg = mm(wg_ref)
    ga_ref[...] = _sigmoid(hg[:, :d]).astype(jnp.bfloat16)
    gb_ref[...] = _sigmoid(hg[:, d:]).astype(jnp.bfloat16)


def _inproj(x2, gain, w_in, B, S):
    T, D = x2.shape
    R = min(PROJ_ROWS, S)
    nS = S // R
    o = np.cumsum((0, A_WIDTH, A_WIDTH, A_WIDTH, IDX_WIDTH, IDX_HEADS, IDX_DIM,
                   B_WIDTH, B_WIDTH, B_WIDTH, B_WIDTH, D, D))
    bf = jnp.bfloat16
    wqT = w_in[:, o[0]:o[1]].T.astype(bf)
    wk = w_in[:, o[1]:o[2]].astype(bf)
    wvT = w_in[:, o[2]:o[3]].T.astype(bf)
    wiqT = w_in[:, o[3]:o[4]].T.astype(bf)
    wiwT = w_in[:, o[4]:o[5]].T.astype(bf)
    wik = w_in[:, o[5]:o[6]].astype(bf)
    wb = w_in[:, o[6]:o[10]].astype(bf)
    wg = w_in[:, o[10]:o[12]].astype(bf)

    def full(a):
        return pl.BlockSpec(a.shape, lambda b, i: (0,) * a.ndim)

    row = lambda n: pl.BlockSpec((R, n), lambda b, i: (b * nS + i, 0))
    colT = lambda n: pl.BlockSpec((1, n, R), lambda b, i: (b, 0, i))
    f32 = jnp.float32
    outs = [
        (jax.ShapeDtypeStruct((T, A_WIDTH), bf), row(A_WIDTH)),
        (jax.ShapeDtypeStruct((T, IDX_DIM), bf), row(IDX_DIM)),
        (jax.ShapeDtypeStruct((B, A_WIDTH, S), bf), colT(A_WIDTH)),
        (jax.ShapeDtypeStruct((B, A_WIDTH, S), bf), colT(A_WIDTH)),
        (jax.ShapeDtypeStruct((B, IDX_WIDTH, S), bf), colT(IDX_WIDTH)),
        (jax.ShapeDtypeStruct((B, IDX_HEADS, S), f32), colT(IDX_HEADS)),
        (jax.ShapeDtypeStruct((T, B_WIDTH), f32), row(B_WIDTH)),
        (jax.ShapeDtypeStruct((T, B_WIDTH), f32), row(B_WIDTH)),
        (jax.ShapeDtypeStruct((T, B_WIDTH), f32), row(B_WIDTH)),
        (jax.ShapeDtypeStruct((T, B_WIDTH), f32), row(B_WIDTH)),
        (jax.ShapeDtypeStruct((T, D), bf), row(D)),
        (jax.ShapeDtypeStruct((T, D), bf), row(D)),
    ]
    ins = [x2, gain.reshape(1, D), wk, wik, wqT, wvT, wiqT, wiwT, wb, wg]
    in_specs = [row(D)] + [full(a) for a in ins[1:]]
    return pl.pallas_call(
        _inproj_kernel,
        grid=(B, nS),
        in_specs=in_specs,
        out_specs=[s for _, s in outs],
        out_shape=[s for s, _ in outs],
        compiler_params=_cparams(("parallel", "parallel")),
        name="inproj",
    )(*ins)


def _t5_bucket_table(n):
    d = np.arange(n)
    max_exact = REL_BUCKETS // 2
    nf = np.maximum(d, 1).astype(np.float64)
    large = max_exact + (np.log(nf / max_exact) / math.log(REL_MAX_DIST / max_exact)
                         * (REL_BUCKETS - max_exact)).astype(np.int32)
    large = np.minimum(large, REL_BUCKETS - 1)
    return np.where(d < max_exact, d, large)


def _dsa_kernel(qT_ref, k_ref, vT_ref, iqT_ref, iwT_ref, ik_ref, enear_ref,
                o_ref, sc_ref, qh_scr, m_scr, acc_scr, *, topk):
    TQ = qT_ref.shape[2]
    KC = TQ
    i = pl.program_id(1)
    nch = i + 1
    q0 = i * TQ
    f32 = jnp.float32
    bf16 = jnp.bfloat16
    key_id = lax.broadcasted_iota(jnp.int32, (KC, TQ), 0)
    qry_id = lax.broadcasted_iota(jnp.int32, (KC, TQ), 1)

    def col_reduce(x, op):
        return op(_fold_rows(x, op), axis=0, keepdims=True)

    iw = iwT_ref[0]

    def score_chunk(c, carry):
        rmin, rmax = carry
        k0 = pl.multiple_of(c * KC, KC)
        ik = ik_ref[pl.ds(k0, KC), :]
        acc = jnp.zeros((KC, TQ), f32)
        for h in range(IDX_HEADS):
            sh = jnp.dot(ik, iqT_ref[0, h * IDX_DIM:(h + 1) * IDX_DIM, :], preferred_element_type=f32)
            acc = acc + jnp.maximum(sh, 0.0) * iw[h:h + 1, :]
        valid = (k0 + key_id) <= (q0 + qry_id)
        sc_ref[pl.ds(k0, KC), :] = jnp.where(valid, acc, MASK_NEG)
        rmin = jnp.minimum(rmin, _fold_rows(jnp.where(valid, acc, -MASK_NEG), jnp.min))
        rmax = jnp.maximum(rmax, _fold_rows(jnp.where(valid, acc, MASK_NEG), jnp.max))
        return rmin, rmax

    rmin8, rmax8 = lax.fori_loop(
        0, nch, score_chunk,
        (jnp.full((V7X_SUBLANES, TQ), -MASK_NEG, f32), jnp.full((V7X_SUBLANES, TQ), MASK_NEG, f32)))
    rmin = jnp.min(rmin8, axis=0, keepdims=True)
    rmax = jnp.max(rmax8, axis=0, keepdims=True)

    def count_where(pred_fn):
        def body(c, acc):
            k0 = pl.multiple_of(c * KC, KC)
            blk = sc_ref[pl.ds(k0, KC), :]
            return acc + _fold_rows(jnp.where(pred_fn(blk), 1.0, 0.0), jnp.sum)
        acc = lax.fori_loop(0, nch, body, jnp.zeros((V7X_SUBLANES, TQ), f32))
        return jnp.sum(acc, axis=0, keepdims=True)

    def band_min_max(lo, hi):
        def body(c, carry):
            bmin, bmax = carry
            k0 = pl.multiple_of(c * KC, KC)
            blk = sc_ref[pl.ds(k0, KC), :]
            bmin = jnp.minimum(bmin, _fold_rows(jnp.where(blk >= lo, blk, -MASK_NEG), jnp.min))
            bmax = jnp.maximum(bmax, _fold_rows(jnp.where(blk < hi, blk, MASK_NEG), jnp.max))
            return bmin, bmax
        bmin8, bmax8 = lax.fori_loop(
            0, nch, body,
            (jnp.full((V7X_SUBLANES, TQ), -MASK_NEG, f32), jnp.full((V7X_SUBLANES, TQ), MASK_NEG, f32)))
        return jnp.min(bmin8, axis=0, keepdims=True), jnp.max(bmax8, axis=0, keepdims=True)

    kf = float(topk)
    n_valid = (q0 + 1 + lax.broadcasted_iota(jnp.int32, (1, TQ), 1)).astype(f32)
    lo0 = rmin
    cnt0 = n_valid
    hi0 = rmax + jnp.maximum(jnp.abs(rmax) * 2.0 ** -20, 1e-30)
    done0 = jnp.where(cnt0 <= kf, 1.0, 0.0)

    def probe(st, lo_s, mid, tie):
        it, lo, hi, cnt, done = st
        active = done < 0.5
        lo_s = jnp.where(active, lo_s, lo)
        c = count_where(lambda blk: blk >= mid)
        feas = c >= kf
        move = active & jnp.logical_not(tie)
        lo_n = jnp.where(move & feas, mid, lo_s)
        cnt_n = jnp.where(move & feas, c, cnt)
        hi_n = jnp.where(move & jnp.logical_not(feas), mid, hi)
        done_n = jnp.where((active & tie) | (cnt_n <= kf), 1.0, done)
        return it + 1, lo_n, hi_n, cnt_n, done_n

    def halve(st):
        _, lo, hi, _, _ = st
        half = lo + 0.5 * (hi - lo)
        stuck = (half <= lo) | (half >= hi)
        return probe(st, lo, half, stuck)

    def snap(st):
        _, lo, hi, _, _ = st
        bmin, bmax = band_min_max(lo, hi)
        mid = bmin + 0.5 * (bmax - bmin)
        return probe(st, bmin, jnp.where(mid <= bmin, bmax, mid), bmax <= bmin)

    st = lax.while_loop(lambda st: (jnp.min(st[-1]) < 0.5) & (st[0] < BISECT_FAST_ITERS), halve,
                        (jnp.int32(0), lo0, hi0, cnt0, done0))
    _, thr, _, cnt_thr, _ = lax.while_loop(lambda st: jnp.min(st[-1]) < 0.5, snap, st)

    tie_overflow = jnp.max(cnt_thr) > kf

    @pl.when(jnp.logical_not(tie_overflow))
    def _():
        def mask_chunk(c, _):
            k0 = pl.multiple_of(c * KC, KC)
            sc_ref[pl.ds(k0, KC), :] = jnp.where(sc_ref[pl.ds(k0, KC), :] >= thr, 0.0, MASK_NEG)
            return 0
        lax.fori_loop(0, nch, mask_chunk, 0)

    @pl.when(tie_overflow)
    def _():
        need = kf - count_where(lambda blk: blk > thr)
        tril = jnp.where(lax.broadcasted_iota(jnp.int32, (KC, KC), 1)
                         <= lax.broadcasted_iota(jnp.int32, (KC, KC), 0), 1.0, 0.0).astype(bf16)

        def mask_chunk(c, run):
            k0 = pl.multiple_of(c * KC, KC)
            blk = sc_ref[pl.ds(k0, KC), :]
            eq = jnp.where(blk == thr, 1.0, 0.0)
            pref = jnp.dot(tril, eq.astype(bf16), preferred_element_type=f32)
            sel = (blk > thr) | ((eq > 0.5) & (run + pref <= need))
            sc_ref[pl.ds(k0, KC), :] = jnp.where(sel, 0.0, MASK_NEG)
            return run + pref[KC - 1:KC, :]

        lax.fori_loop(0, nch, mask_chunk, jnp.zeros((1, TQ), f32))

    AK = min(ATT_KC, TQ)
    per = TQ // AK
    head0_q = (lax.broadcasted_iota(jnp.int32, (V7X_LANES, TQ), 0) // A_HEAD_DIM) == 0
    n_pairs = A_HEADS // 2

    m_scr[...] = jnp.full(m_scr.shape, MASK_NEG, f32)
    acc_scr[...] = jnp.zeros(acc_scr.shape, f32)
    v_row = lax.broadcasted_iota(jnp.int32, (V7X_LANES, AK), 0)
    denom_row = [A_HEAD_DIM * (1 - sub) for sub in range(2)]
    for p in range(n_pairs):
        q_pair = qT_ref[0, p * V7X_LANES:(p + 1) * V7X_LANES, :]
        zq = jnp.zeros_like(q_pair)
        qh_scr[2 * p] = jnp.where(head0_q, q_pair, zq)
        qh_scr[2 * p + 1] = jnp.where(head0_q, zq, q_pair)

    def step(c, bias_rows):
        k0 = pl.multiple_of(c * AK, AK)
        msk = sc_ref[pl.ds(k0, AK), :]
        for p in range(n_pairs):
            kp = k_ref[pl.ds(k0, AK), p * V7X_LANES:(p + 1) * V7X_LANES]
            vp = vT_ref[0, p * V7X_LANES:(p + 1) * V7X_LANES, pl.ds(k0, AK)]
            for sub in range(2):
                h = 2 * p + sub
                s = jnp.dot(kp, qh_scr[h], preferred_element_type=f32) + msk
                if bias_rows is not None:
                    s = s + enear_ref[h, bias_rows, :]
                m = m_scr[h:h + 1, :]
                m_new = jnp.maximum(m, col_reduce(s, jnp.max))
                alpha = jnp.exp2(m - m_new)
                pr = jnp.exp2(s - m_new)
                m_scr[h:h + 1, :] = m_new
                v_aug = jnp.where(v_row == denom_row[sub], jnp.ones_like(vp), vp)
                acc_scr[h] = alpha * acc_scr[h] + jnp.dot(v_aug, pr.astype(bf16), preferred_element_type=f32)

    def far(blk, _):
        for jj in range(per):
            step(blk * per + jj, None)
        return 0

    def near(block, first_chunk):
        for jj in range(per):
            step(first_chunk + jj, slice(block * TQ + jj * AK, block * TQ + (jj + 1) * AK))

    lax.fori_loop(0, jnp.maximum(i - 1, 0), far, 0)

    @pl.when(i >= 1)
    def _():
        near(0, (i - 1) * per)

    near(1, i * per)
    for p in range(n_pairs):
        outs = [acc_scr[2 * p + sub] / acc_scr[2 * p + sub, denom_row[sub]:denom_row[sub] + 1, :]
                for sub in range(2)]
        o_pair = jnp.where(head0_q, outs[0], outs[1])
        o_ref[:, p * V7X_LANES:(p + 1) * V7X_LANES] = o_pair.T.astype(o_ref.dtype)


def _dsa(k, ik, qT, vT, iqT, iwT, rel_bias, B, S):
    T = k.shape[0]
    TQ = min(ATT_Q, S)
    nQ = S // TQ
    topk = min(TOPK_MAX, S // 4)
    buckets = _t5_bucket_table(2 * TQ + 1)
    assert np.all(_t5_bucket_table(S + 1)[TQ + 1:] == REL_BUCKETS - 1)
    j = np.arange(2 * TQ)[:, None]
    r = np.arange(TQ)[None, :]
    dist = np.maximum(r + TQ - j, 0)
    onehot = (jnp.asarray(buckets[dist], jnp.int32)[None]
              == jnp.arange(REL_BUCKETS, dtype=jnp.int32)[:, None, None]).astype(jnp.float32)
    rel = (rel_bias.astype(jnp.float32) - rel_bias[REL_BUCKETS - 1].astype(jnp.float32)[None, :]) * LOG2_E
    enear = jnp.einsum('nh,njr->hjr', rel, onehot, precision=lax.Precision.HIGHEST)

    return pl.pallas_call(
        functools.partial(_dsa_kernel, topk=topk),
        grid=(B, nQ),
        in_specs=[
            pl.BlockSpec((1, A_WIDTH, TQ), lambda b, i: (b, 0, i)),
            pl.BlockSpec((S, A_WIDTH), lambda b, i: (b, 0)),
            pl.BlockSpec((1, A_WIDTH, S), lambda b, i: (b, 0, 0)),
            pl.BlockSpec((1, IDX_WIDTH, TQ), lambda b, i: (b, 0, i)),
            pl.BlockSpec((1, IDX_HEADS, TQ), lambda b, i: (b, 0, i)),
            pl.BlockSpec((S, IDX_DIM), lambda b, i: (b, 0)),
            pl.BlockSpec((A_HEADS, 2 * TQ, TQ), lambda b, i: (0, 0, 0)),
        ],
        out_specs=pl.BlockSpec((TQ, A_WIDTH), lambda b, i: (b * nQ + i, 0)),
        scratch_shapes=[pltpu.VMEM((S, TQ), jnp.float32),
                        pltpu.VMEM((A_HEADS, V7X_LANES, TQ), jnp.bfloat16),
                        pltpu.VMEM((A_HEADS, TQ), jnp.float32),
                        pltpu.VMEM((A_HEADS, V7X_LANES, TQ), jnp.float32)],
        out_shape=jax.ShapeDtypeStruct((T, A_WIDTH), jnp.bfloat16),
        compiler_params=_cparams(("parallel", "arbitrary")),
        name="dsa",
    )(qT, k, vT, iqT, iwT, ik, enear)


def _hgrn_kernel(bq_ref, bf_ref, bi_ref, bg_ref, lb_ref, gain_ref, o_ref,
                 st_ref, b_scr, q_scr, k_scr, v_scr, oi_scr, qd_s, kd_s, kl_s, vv_s, dec_s, oi_s, upd_s, st_s):
    R = bq_ref.shape[0]
    C = HGRN_CHUNK
    nC = R // C
    f32 = jnp.float32
    bf16 = jnp.bfloat16
    h = pl.program_id(1)

    @pl.when(pl.program_id(2) == 0)
    def _():
        st_ref[...] = jnp.zeros_like(st_ref)

    lb = lb_ref[pl.ds(h, 1), :]
    gain = gain_ref[pl.ds(h, 1), :]
    tril_incl = jnp.where(lax.broadcasted_iota(jnp.int32, (C, C), 1)
                          <= lax.broadcasted_iota(jnp.int32, (C, C), 0), 1.0, 0.0)
    srow = lax.broadcasted_iota(jnp.int32, (C, B_KEY_DIM), 0)

    def gates(r0):
        f = lb + (1.0 - lb) * _sigmoid(bf_ref[pl.ds(r0, C), :])
        qr = bq_ref[pl.ds(r0, C), :]
        return jnp.log(f), 1.0 - f, qr * _sigmoid(qr) * (B_KEY_DIM ** -0.5), bi_ref[pl.ds(r0, C), :]

    def cumdecay(g):
        tri = tril_incl.astype(bf16)
        g_hi = g.astype(bf16)
        rest = g - g_hi.astype(f32)
        g_mid = rest.astype(bf16)
        g_lo = (rest - g_mid.astype(f32)).astype(bf16)
        return (jnp.dot(tri, g_hi, preferred_element_type=f32) + jnp.dot(tri, g_mid, preferred_element_type=f32)
                + jnp.dot(tri, g_lo, preferred_element_type=f32))

    def advance(r0, st, qd, o_intra, upd, decay_row):
        o_inter = lax.dot_general(qd, st.astype(bf16), (((1,), (1,)), ((), ())), preferred_element_type=f32)
        og = bg_ref[pl.ds(r0, C), :]
        y = _rms(o_inter + o_intra, gain) * (og * _sigmoid(og))
        o_ref[pl.ds(r0, C), :] = y.astype(o_ref.dtype)
        return st * decay_row + upd

    f_all = lb + (1.0 - lb) * _sigmoid(bf_ref[...])
    g_all = jnp.log(f_all)
    decay = jnp.sum(g_all.reshape(nC, C, B_KEY_DIM), axis=1)
    safe = jnp.min(decay) >= -HGRN_SAFE_DECAY

    @pl.when(safe)
    def _():
        qr = bq_ref[...]
        qq = qr * _sigmoid(qr) * (B_KEY_DIM ** -0.5)
        kk = 1.0 - f_all
        b = jnp.concatenate([cumdecay(g_all[c * C:(c + 1) * C]) for c in range(nC)], axis=0)
        b_end = jnp.concatenate([jnp.broadcast_to(b[(c + 1) * C - 1:(c + 1) * C], (C, B_KEY_DIM))
                                 for c in range(nC)], axis=0)
        qd_s[...] = (qq * jnp.exp(b)).astype(bf16)
        kd_s[...] = (kk * jnp.exp(-b)).astype(bf16)
        kl_s[...] = (kk * jnp.exp(b_end - b)).astype(bf16)
        vv_s[...] = bi_ref[...].astype(bf16)
        dec_s[...] = jnp.exp(b_end)
        for c in range(nC):
            rows = slice(c * C, (c + 1) * C)
            att = lax.dot_general(qd_s[rows], kd_s[rows], (((1,), (1,)), ((), ())),
                                  preferred_element_type=f32) * tril_incl
            oi_s[rows] = jnp.dot(att.astype(bf16), vv_s[rows], preferred_element_type=f32)
            upd_s[c] = lax.dot_general(vv_s[rows], kl_s[rows], (((0,), (0,)), ((), ())),
                                       preferred_element_type=f32)
        st = st_ref[...]
        for c in range(nC):
            st_s[c] = st.astype(bf16)
            st = st * dec_s[c * C:c * C + 1] + upd_s[c]
        st_ref[...] = st
        for c in range(nC):
            rows = slice(c * C, (c + 1) * C)
            oi_s[rows] = oi_s[rows] + lax.dot_general(qd_s[rows], st_s[c], (((1,), (1,)), ((), ())),
                                                      preferred_element_type=f32)
        og = bg_ref[...]
        o_ref[...] = (_rms(oi_s[...], gain) * (og * _sigmoid(og))).astype(o_ref.dtype)

    @pl.when(jnp.logical_not(safe))
    def _():
        def body(c, st):
            r0 = pl.multiple_of(c * C, C)
            g, kk, qq, vv = gates(r0)
            b = cumdecay(g)
            b_last = b[C - 1:C, :]
            b_scr[...] = b
            q_scr[...] = qq
            k_scr[...] = kk
            v_scr[...] = vv

            def row(t, _):
                bt = b_scr[pl.ds(t, 1), :]
                qt = q_scr[pl.ds(t, 1), :]
                ex = jnp.where(srow <= t, bt - b_scr[...], -jnp.inf)
                a = jnp.sum(qt * k_scr[...] * jnp.exp(ex), axis=1, keepdims=True)
                oi_scr[pl.ds(t, 1), :] = jnp.sum(a * v_scr[...], axis=0, keepdims=True)
                return 0
            lax.fori_loop(0, C, row, 0)
            kd_last = (kk * jnp.exp(b_last - b)).astype(bf16)
            upd = lax.dot_general(vv.astype(bf16), kd_last, (((0,), (0,)), ((), ())),
                                  preferred_element_type=f32)
            return advance(r0, st, (qq * jnp.exp(b)).astype(bf16), oi_scr[...], upd, jnp.exp(b_last))
        st_ref[...] = lax.fori_loop(0, nC, body, st_ref[...])


def _hgrn(bq, bf, bi, bg, lb, gain, B, S):
    T = bq.shape[0]
    R = min(HGRN_ROWS, S)
    nR = S // R
    C = HGRN_CHUNK
    blk = pl.BlockSpec((R, B_KEY_DIM), lambda b, h, c: (b * nR + c, h))
    small = pl.BlockSpec((B_HEADS, B_KEY_DIM), lambda b, h, c: (0, 0))
    f32 = jnp.float32
    return pl.pallas_call(
        _hgrn_kernel,
        grid=(B, B_HEADS, nR),
        in_specs=[blk, blk, blk, blk, small, small],
        out_specs=blk,
        out_shape=jax.ShapeDtypeStruct((T, B_WIDTH), jnp.bfloat16),
        scratch_shapes=[pltpu.VMEM((B_VAL_DIM, B_KEY_DIM), f32)] +
                       [pltpu.VMEM((C, B_KEY_DIM), f32) for _ in range(5)] +
                       [pltpu.VMEM((R, B_KEY_DIM), jnp.bfloat16) for _ in range(4)] +
                       [pltpu.VMEM((R, B_KEY_DIM), f32) for _ in range(2)] +
                       [pltpu.VMEM((R // C, B_VAL_DIM, B_KEY_DIM), f32),
                        pltpu.VMEM((R // C, B_VAL_DIM, B_KEY_DIM), jnp.bfloat16)],
        compiler_params=_cparams(("parallel", "parallel", "arbitrary")),
        name="hgrn",
    )(bq, bf, bi, bg, lb, gain)


def _merge_kernel(x_ref, ya_ref, yb_ref, ga_ref, gb_ref, wa_ref, wb_ref, wo_ref, g_ref, wrh_ref, wrl_ref, br_ref,
                  x1_ref, xn_ref, lg_ref):
    f32 = jnp.float32
    ma = jnp.dot(ya_ref[...], wa_ref[...], preferred_element_type=f32)
    mb = jnp.dot(yb_ref[...], wb_ref[...], preferred_element_type=f32)
    merged = ga_ref[...].astype(f32) * ma + gb_ref[...].astype(f32) * mb
    x1 = x_ref[...] + jnp.dot(merged.astype(jnp.bfloat16), wo_ref[...], preferred_element_type=f32)
    x1_ref[...] = x1
    hn = _rms(x1, g_ref[...])
    xn_ref[...] = _pack_bf16_pairs(hn)
    hn_hi = hn.astype(jnp.bfloat16)
    hn_lo = (hn - hn_hi.astype(f32)).astype(jnp.bfloat16)
    lg_ref[...] = (jnp.dot(hn_hi, wrh_ref[...], preferred_element_type=f32)
                   + jnp.dot(hn_lo, wrh_ref[...], preferred_element_type=f32)
                   + jnp.dot(hn_hi, wrl_ref[...], preferred_element_type=f32) + br_ref[...])


def _merge(x2, ya, yb, ga, gb, w_up_a, w_up_b, w_out, gain, w_router, b_router):
    T, D = x2.shape
    R = min(PROJ_ROWS, T)
    bf = jnp.bfloat16
    wr_hi = w_router.astype(bf)
    wr_lo = (w_router - wr_hi.astype(jnp.float32)).astype(bf)
    ins = [x2, ya, yb, ga, gb, w_up_a.astype(bf), w_up_b.astype(bf), w_out.astype(bf),
           gain.reshape(1, D), wr_hi, wr_lo, b_router.reshape(1, N_EXPERTS)]
    row = lambda n: pl.BlockSpec((R, n), lambda i: (i, 0))
    full = lambda a: pl.BlockSpec(a.shape, lambda i: (0,) * a.ndim)
    in_specs = [row(D), row(A_WIDTH), row(B_WIDTH), row(D), row(D)] + [full(a) for a in ins[5:]]
    return pl.pallas_call(
        _merge_kernel,
        grid=(T // R,),
        in_specs=in_specs,
        out_specs=[row(D), row(D // 2), row(N_EXPERTS)],
        out_shape=[jax.ShapeDtypeStruct((T, D), jnp.float32), jax.ShapeDtypeStruct((T, D // 2), jnp.uint32),
                   jax.ShapeDtypeStruct((T, N_EXPERTS), jnp.float32)],
        compiler_params=_cparams(("parallel",)),
        name="merge",
    )(*ins)


def _route_kernel(lg_ref, eidx_ref, gate_ref, rank_ref, cnt_ref, run_ref):
    R = lg_ref.shape[0]
    f32 = jnp.float32

    @pl.when(pl.program_id(0) == 0)
    def _():
        run_ref[...] = jnp.zeros_like(run_ref)

    lg = lg_ref[...].T
    expert = lax.broadcasted_iota(jnp.int32, (N_EXPERTS, R), 0)
    work = lg
    onehots, vals, idxs = [], [], []
    for _ in range(TOP_K):
        m = jnp.max(work, axis=0, keepdims=True)
        idx = jnp.min(jnp.where(work == m, expert, N_EXPERTS), axis=0, keepdims=True)
        oh = expert == idx
        onehots.append(oh)
        vals.append(m)
        idxs.append(idx)
        work = jnp.where(oh, -jnp.inf, work)
    ex = [jnp.exp(v - vals[0]) for v in vals]
    den = ex[0] + ex[1] + ex[2] + ex[3]
    chosen = jnp.where(onehots[0] | onehots[1] | onehots[2] | onehots[3], 1.0, 0.0)
    earlier = jnp.where(lax.broadcasted_iota(jnp.int32, (R, R), 0)
                        < lax.broadcasted_iota(jnp.int32, (R, R), 1), 1.0, 0.0).astype(jnp.bfloat16)
    before = jnp.dot(chosen.astype(jnp.bfloat16), earlier, preferred_element_type=f32) + run_ref[...]
    slot = lax.broadcasted_iota(jnp.int32, (TOP_K, R), 0)
    eidx = jnp.zeros((TOP_K, R), jnp.int32)
    gate = jnp.zeros((TOP_K, R), f32)
    rank = jnp.zeros((TOP_K, R), f32)
    for k in range(TOP_K):
        eidx = jnp.where(slot == k, idxs[k], eidx)
        gate = jnp.where(slot == k, ex[k] / den, gate)
        rk = jnp.sum(jnp.where(onehots[k], before, 0.0), axis=0, keepdims=True)
        rank = jnp.where(slot == k, rk, rank)
    eidx_ref[...] = eidx
    gate_ref[...] = gate
    rank_ref[...] = rank.astype(jnp.int32)
    run_ref[...] = run_ref[...] + jnp.sum(chosen, axis=1, keepdims=True)
    cnt_ref[...] = run_ref[...].astype(jnp.int32)


def _route(logits):
    T = logits.shape[0]
    R = min(ROUTE_ROWS, T)
    col = pl.BlockSpec((TOP_K, R), lambda i: (0, i))
    return pl.pallas_call(
        _route_kernel,
        grid=(T // R,),
        in_specs=[pl.BlockSpec((R, N_EXPERTS), lambda i: (i, 0))],
        out_specs=[col, col, col, pl.BlockSpec((N_EXPERTS, 1), lambda i: (0, 0))],
        out_shape=[jax.ShapeDtypeStruct((TOP_K, T), jnp.int32), jax.ShapeDtypeStruct((TOP_K, T), jnp.float32),
                   jax.ShapeDtypeStruct((TOP_K, T), jnp.int32), jax.ShapeDtypeStruct((N_EXPERTS, 1), jnp.int32)],
        scratch_shapes=[pltpu.VMEM((N_EXPERTS, 1), jnp.float32)],
        compiler_params=_cparams(("arbitrary",)),
        name="route",
    )(logits)


def _sc_gather_rows(table, idx):
    N, W = table.shape
    M = idx.shape[0]
    workers = V7X_SC_CORES * V7X_SC_SUBCORES
    per_worker = M // workers
    pieces = per_worker // SC_GATHER_ROWS
    assert per_worker * workers == M and pieces * SC_GATHER_ROWS == per_worker and pieces % 2 == 0
    mesh = plsc.VectorSubcoreMesh(core_axis_name="core", subcore_axis_name="subcore",
                                  num_cores=V7X_SC_CORES, num_subcores=V7X_SC_SUBCORES)

    @functools.partial(
        pl.kernel, mesh=mesh,
        out_type=jax.ShapeDtypeStruct((M, W), table.dtype),
        scratch_types=[pltpu.VMEM((per_worker,), jnp.int32),
                       pltpu.VMEM((SC_GATHER_ROWS, W), table.dtype),
                       pltpu.VMEM((SC_GATHER_ROWS, W), table.dtype),
                       pltpu.SemaphoreType.DMA, pltpu.SemaphoreType.DMA],
    )
    def gather(table_hbm, idx_hbm, out_hbm, idx_v, rows_a, rows_b, sem_a, sem_b):
        worker = lax.axis_index("subcore") * V7X_SC_CORES + lax.axis_index("core")
        base = pl.multiple_of(worker * per_worker, SC_GATHER_ROWS)
        pltpu.sync_copy(idx_hbm.at[pl.ds(base, per_worker)], idx_v)
        bufs = ((rows_a, sem_a), (rows_b, sem_b))

        def fetch(g, buf, sem):
            off = pl.multiple_of(g * SC_GATHER_ROWS, SC_GATHER_ROWS)
            return pltpu.make_async_copy(table_hbm.at[idx_v.at[pl.ds(off, SC_GATHER_ROWS)]], buf, sem)

        fetch(0, *bufs[0]).start()

        @pl.loop(0, pieces, step=2)
        def _(g0):
            for half in range(2):
                g = g0 + half
                buf, sem = bufs[half]
                fetch(g, buf, sem).wait()

                @pl.when(g + 1 < pieces)
                def _():
                    fetch(g + 1, *bufs[1 - half]).start()

                off = pl.multiple_of(g * SC_GATHER_ROWS, SC_GATHER_ROWS)
                pltpu.sync_copy(buf, out_hbm.at[pl.ds(base + off, SC_GATHER_ROWS)])

    return gather(table, idx)


def _sc_scatter_rows(rows, dest, n_out):
    T, W = rows.shape
    slots = dest.shape[0]
    workers = V7X_SC_CORES * V7X_SC_SUBCORES
    per_worker = T // workers
    pieces = per_worker // SC_GATHER_ROWS
    assert per_worker * workers == T and pieces * SC_GATHER_ROWS == per_worker and pieces % 2 == 0
    idx = dest.reshape(slots, workers, pieces, SC_GATHER_ROWS).transpose(1, 2, 0, 3)
    mesh = plsc.VectorSubcoreMesh(core_axis_name="core", subcore_axis_name="subcore",
                                  num_cores=V7X_SC_CORES, num_subcores=V7X_SC_SUBCORES)

    @functools.partial(
        pl.kernel, mesh=mesh,
        out_type=jax.ShapeDtypeStruct((n_out, W), rows.dtype),
        scratch_types=[pltpu.VMEM((pieces, slots, SC_GATHER_ROWS), jnp.int32),
                       pltpu.VMEM((SC_GATHER_ROWS, W), rows.dtype),
                       pltpu.VMEM((SC_GATHER_ROWS, W), rows.dtype),
                       pltpu.SemaphoreType.DMA, pltpu.SemaphoreType.DMA, pltpu.SemaphoreType.DMA],
    )
    def scatter(rows_hbm, idx_hbm, out_hbm, idx_v, rows_a, rows_b, sem_a, sem_b, sem_out):
        worker = lax.axis_index("subcore") * V7X_SC_CORES + lax.axis_index("core")
        base = pl.multiple_of(worker * per_worker, SC_GATHER_ROWS)
        pltpu.sync_copy(idx_hbm.at[worker], idx_v)
        bufs = ((rows_a, sem_a), (rows_b, sem_b))

        def fetch(g, buf, sem):
            off = pl.multiple_of(g * SC_GATHER_ROWS, SC_GATHER_ROWS)
            return pltpu.make_async_copy(rows_hbm.at[pl.ds(base + off, SC_GATHER_ROWS)], buf, sem)

        fetch(0, *bufs[0]).start()

        @pl.loop(0, pieces, step=2)
        def _(g0):
            for half in range(2):
                g = g0 + half
                buf, sem = bufs[half]
                fetch(g, buf, sem).wait()

                @pl.when(g + 1 < pieces)
                def _():
                    fetch(g + 1, *bufs[1 - half]).start()

                puts = [pltpu.make_async_copy(buf, out_hbm.at[idx_v.at[g, k]], sem_out) for k in range(slots)]
                for put in puts:
                    put.start()
                for put in puts:
                    put.wait()

    return scatter(rows, idx)


def _experts_kernel(be_ref, nb_ref, first_ref, slot_ref, next_ref,
                    x_ref, wgu_hbm, bgu_ref, wd_hbm, bd_ref, o_ref,
                    wgu_f32, wd_f32, wgu_bf, wd_bf, sems):
    f32 = jnp.float32
    d_ff = wd_bf.shape[0]
    i = pl.program_id(0)
    live = i < nb_ref[0]

    def weight_copies(e, s):
        return (pltpu.make_async_copy(wgu_hbm.at[e], wgu_f32.at[s], sems.at[s, 0]),
                pltpu.make_async_copy(wd_hbm.at[e], wd_f32.at[s], sems.at[s, 1]))

    @pl.when(live & (first_ref[i] == 1))
    def _():
        e = be_ref[i]
        s = slot_ref[i]

        @pl.when(i == 0)
        def _():
            for cp in weight_copies(e, s):
                cp.start()

        for cp in weight_copies(e, s):
            cp.wait()

        @pl.when(next_ref[i] >= 0)
        def _():
            for cp in weight_copies(next_ref[i], 1 - s):
                cp.start()

        wgu_bf[...] = wgu_f32[s].astype(jnp.bfloat16)
        wd_bf[...] = wd_f32[s].astype(jnp.bfloat16)

    @pl.when(live)
    def _():
        x_hi, x_lo = _unpack_bf16_pairs(x_ref[...])
        half = x_hi.shape[1]
        gu = (jnp.dot(x_hi, wgu_bf[:half, :], preferred_element_type=f32)
              + jnp.dot(x_lo, wgu_bf[half:, :], preferred_element_type=f32) + bgu_ref[0])
        gate = jnp.minimum(gu[:, :d_ff], SWIGLU_LIMIT)
        lin = jnp.clip(gu[:, d_ff:], -SWIGLU_LIMIT, SWIGLU_LIMIT)
        act = (lin + 1.0) * gate * _sigmoid(SWIGLU_ALPHA * gate)
        y = jnp.dot(act.astype(jnp.bfloat16), wd_bf[...], preferred_element_type=f32) + bd_ref[0]
        o_ref[...] = _pack_bf16_pairs(y)

    @pl.when(pl.program_id(0) >= nb_ref[0])
    def _():
        o_ref[...] = jnp.zeros_like(o_ref)


def _experts(xs, plan, w_gu, b_gu, w_down, b_down):
    P, W = xs.shape
    E, D, F2 = w_gu.shape
    nb = P // EXPERT_ROWS
    by_expert = lambda i, be, *_: (be[i], 0, 0)
    grid_spec = pltpu.PrefetchScalarGridSpec(
        num_scalar_prefetch=5,
        grid=(nb,),
        in_specs=[
            pl.BlockSpec((EXPERT_ROWS, W), lambda i, *_: (i, 0)),
            pl.BlockSpec(memory_space=pl.ANY),
            pl.BlockSpec((1, 1, F2), by_expert),
            pl.BlockSpec(memory_space=pl.ANY),
            pl.BlockSpec((1, 1, D), by_expert),
        ],
        out_specs=pl.BlockSpec((EXPERT_ROWS, W), lambda i, *_: (i, 0)),
        scratch_shapes=[pltpu.VMEM((2, D, F2), jnp.float32), pltpu.VMEM((2, F2 // 2, D), jnp.float32),
                        pltpu.VMEM((D, F2), jnp.bfloat16), pltpu.VMEM((F2 // 2, D), jnp.bfloat16),
                        pltpu.SemaphoreType.DMA((2, 2))],
    )
    return pl.pallas_call(
        _experts_kernel,
        grid_spec=grid_spec,
        out_shape=jax.ShapeDtypeStruct((P, W), jnp.uint32),
        compiler_params=_cparams(("arbitrary",)),
        name="experts",
    )(*plan, xs, w_gu, b_gu.reshape(E, 1, F2), w_down, b_down.reshape(E, 1, D))


def _combine_kernel(ya_ref, x1_ref, gate_ref, g_ref, o_ref):
    half = x1_ref.shape[1] // 2
    f32 = jnp.float32
    gate = gate_ref[...].T
    x1 = x1_ref[...]
    y_hi = x1[:, :half]
    y_lo = x1[:, half:]
    for k in range(TOP_K):
        hi, lo = _unpack_bf16_pairs(ya_ref[k])
        y_hi = y_hi + gate[:, k:k + 1] * hi.astype(f32)
        y_lo = y_lo + gate[:, k:k + 1] * lo.astype(f32)
    o_ref[...] = _rms(jnp.concatenate([y_hi, y_lo], axis=1), g_ref[...])


def _combine(ya, x1, gates, gain):
    T, D = x1.shape
    R = min(COMBINE_ROWS, T)
    row = lambda w: pl.BlockSpec((R, w), lambda i: (i, 0))
    return pl.pallas_call(
        _combine_kernel,
        grid=(T // R,),
        in_specs=[pl.BlockSpec((TOP_K, R, D // 2), lambda i: (0, i, 0)), row(D),
                  pl.BlockSpec((TOP_K, R), lambda i: (0, i)), pl.BlockSpec((1, D), lambda i: (0, 0))],
        out_specs=row(D),
        out_shape=jax.ShapeDtypeStruct((T, D), jnp.float32),
        compiler_params=_cparams(("parallel",)),
        name="combine",
    )(ya, x1, gates, gain.reshape(1, D))


def _moe_plan(eidx, rank, counts, A):
    counts = counts.reshape(N_EXPERTS)
    padded = (counts + EXPERT_ROWS - 1) // EXPERT_ROWS * EXPERT_ROWS
    pad_ends = jnp.cumsum(padded)
    pad_starts = pad_ends - padded
    n_blocks = -(-A // EXPERT_ROWS) + N_EXPERTS
    ids = jnp.arange(N_EXPERTS, dtype=jnp.int32)
    dest = rank + jnp.sum(jnp.where(eidx[None] == ids[:, None, None], pad_starts[:, None, None], 0), axis=0)
    block_start = jnp.arange(n_blocks, dtype=pad_ends.dtype) * EXPERT_ROWS
    block_expert = jnp.minimum(jnp.sum(pad_ends[None, :] <= block_start[:, None], axis=1),
                               N_EXPERTS - 1).astype(jnp.int32)
    n_used = (pad_ends[-1] // EXPERT_ROWS).astype(jnp.int32).reshape(1)
    has_rows = counts > 0
    ordinal = jnp.cumsum(has_rows.astype(jnp.int32)) - 1
    later = has_rows[None, :] & (ids[None, :] > ids[:, None])
    next_expert = jnp.where(jnp.any(later, axis=1), jnp.argmax(later, axis=1), -1).astype(jnp.int32)
    block_first = ((block_start == pad_starts[block_expert]) & (block_start < pad_ends[-1])).astype(jnp.int32)
    block_slot = (ordinal[block_expert] % 2).astype(jnp.int32)
    block_next = next_expert[block_expert]
    plan = (block_expert, n_used, block_first, block_slot, block_next)
    return dest.astype(jnp.int32), plan, n_blocks


def kernel(x, w_in, w_up_a, w_up_b, w_out, norm_mix, norm_ffn, norm_final, hgrn_norm,
           lb_logits, rel_bias, w_router, b_router, w_gu, b_gu, w_down, b_down):
    B, S, D = x.shape
    T = B * S
    assert w_in.shape[0] == 1, "the final rmsnorm is fused into the single layer's combine stage"
    lb_all = jnp.cumsum(jax.nn.softmax(lb_logits.astype(jnp.float32), axis=0), axis=0)
    x2 = x.reshape(T, D)
    (k, ik, qT, vT, iqT, iwT, bq, bf, bi, bg, ga, gb) = _inproj(x2, norm_mix[0], w_in[0], B, S)
    ya = _dsa(k, ik, qT, vT, iqT, iwT, rel_bias, B, S)
    yb = _hgrn(bq, bf, bi, bg, lb_all[0].reshape(B_HEADS, B_KEY_DIM), hgrn_norm[0], B, S)
    x1, xn, logits = _merge(x2, ya, yb, ga, gb, w_up_a[0], w_up_b[0], w_out[0], norm_ffn[0],
                            w_router[0], b_router[0])
    eidx, gates, rank, counts = _route(logits)
    dest, plan, n_blocks = _moe_plan(eidx, rank, counts, T * TOP_K)
    P = n_blocks * EXPERT_ROWS
    A = T * TOP_K
    xs = _sc_scatter_rows(xn, dest, P)
    y_buf = _experts(xs, plan, w_gu[0], b_gu[0], w_down[0], b_down[0])
    ya = _sc_gather_rows(y_buf, dest.reshape(A)).reshape(TOP_K, T, D // 2)
    out = _combine(ya, x1, gates, norm_final)
    return out.reshape(B, S, D)
```

```python
import functools
import math

import numpy as np
import jax
import jax.numpy as jnp
from jax import lax
from jax.experimental import pallas as pl
from jax.experimental.pallas import tpu as pltpu
from jax.experimental.pallas import tpu_sc as plsc

A_HEADS = 8
A_HEAD_DIM = 64
IDX_HEADS = 8
IDX_DIM = 32
TOPK_MAX = 256
REL_BUCKETS = 32
REL_MAX_DIST = 128
B_HEADS = 4
B_KEY_DIM = 128
B_VAL_DIM = 128
N_EXPERTS = 32
TOP_K = 4
SWIGLU_LIMIT = 7.0
SWIGLU_ALPHA = 1.702
EPS = 1e-6
LOG2_E = math.log2(math.e)

A_WIDTH = A_HEADS * A_HEAD_DIM
B_WIDTH = B_HEADS * B_VAL_DIM
IDX_WIDTH = IDX_HEADS * IDX_DIM

V7X_LANES = 128
V7X_SUBLANES = 8
V7X_VMEM_LIMIT_BYTES = 56 * 1024 * 1024
V7X_SC_CORES = 2
V7X_SC_SUBCORES = 16

PROJ_ROWS = 512
ATT_Q = 256
ATT_KC = 128
HGRN_ROWS = 1024
HGRN_CHUNK = 64
HGRN_SAFE_DECAY = 70.0
ROUTE_ROWS = 512
EXPERT_ROWS = 256
COMBINE_ROWS = 512
SC_GATHER_ROWS = 64
MASK_NEG = -1e30
BISECT_FAST_ITERS = 26


def _cparams(dims):
    return pltpu.CompilerParams(dimension_semantics=dims, vmem_limit_bytes=V7X_VMEM_LIMIT_BYTES)


def _rms(x, gain):
    return x * lax.rsqrt(jnp.mean(x * x, axis=-1, keepdims=True) + EPS) * gain


def _sigmoid(x):
    return 1.0 / (1.0 + jnp.exp(-x))


def _pack_bf16_pairs(x):
    n = x.shape[1] // 2
    as_bits = lambda v: lax.bitcast_convert_type(v.astype(jnp.bfloat16).astype(jnp.float32), jnp.uint32)
    return (as_bits(x[:, :n]) & jnp.uint32(0xFFFF0000)) | (as_bits(x[:, n:]) >> 16)


def _unpack_bf16_pairs(w):
    hi = lax.bitcast_convert_type(w & jnp.uint32(0xFFFF0000), jnp.float32).astype(jnp.bfloat16)
    lo = lax.bitcast_convert_type(w << 16, jnp.float32).astype(jnp.bfloat16)
    return hi, lo


def _fold_rows(x, op):
    return op(x.reshape(x.shape[0] // V7X_SUBLANES, V7X_SUBLANES, x.shape[1]), axis=0)


def _inproj_kernel(x_ref, g_ref, wk_ref, wik_ref, wqT_ref, wvT_ref, wiqT_ref, wiwT_ref, wb_ref, wg_ref,
                   k_ref, ik_ref, qT_ref, vT_ref, iqT_ref, iwT_ref, bq_ref, bf_ref, bi_ref, bg_ref,
                   ga_ref, gb_ref):
    x = x_ref[...]
    hn = _rms(x, g_ref[...]).astype(jnp.bfloat16)

    def mm(w_ref):
        return jnp.dot(hn, w_ref[...], preferred_element_type=jnp.float32)

    def mm_t(w_ref):
        return lax.dot_general(w_ref[...], hn, (((1,), (1,)), ((), ())),
                               preferred_element_type=jnp.float32)

    k_ref[...] = mm(wk_ref).astype(jnp.bfloat16)
    ik_ref[...] = mm(wik_ref).astype(jnp.bfloat16)
    qT_ref[0] = (mm_t(wqT_ref) * (A_HEAD_DIM ** -0.5 * LOG2_E)).astype(jnp.bfloat16)
    vT_ref[0] = mm_t(wvT_ref).astype(jnp.bfloat16)
    iqT_ref[0] = mm_t(wiqT_ref).astype(jnp.bfloat16)
    iwT_ref[0] = mm_t(wiwT_ref) * ((IDX_HEADS * IDX_DIM) ** -0.5)
    hb = mm(wb_ref)
    bq_ref[...] = hb[:, 0 * B_WIDTH:1 * B_WIDTH]
    bf_ref[...] = hb[:, 1 * B_WIDTH:2 * B_WIDTH]
    bi_ref[...] = hb[:, 2 * B_WIDTH:3 * B_WIDTH]
    bg_ref[...] = hb[:, 3 * B_WIDTH:4 * B_WIDTH]
    d = ga_ref.shape[-1]
    hg = mm(wg_ref)
    ga_ref[...] = _sigmoid(hg[:, :d]).astype(jnp.bfloat16)
    gb_ref[...] = _sigmoid(hg[:, d:]).astype(jnp.bfloat16)


def _inproj(x2, gain, w_in, B, S):
    T, D = x2.shape
    R = min(PROJ_ROWS, S)
    nS = S // R
    o = np.cumsum((0, A_WIDTH, A_WIDTH, A_WIDTH, IDX_WIDTH, IDX_HEADS, IDX_DIM,
                   B_WIDTH, B_WIDTH, B_WIDTH, B_WIDTH, D, D))
    bf = jnp.bfloat16
    wqT = w_in[:, o[0]:o[1]].T.astype(bf)
    wk = w_in[:, o[1]:o[2]].astype(bf)
    wvT = w_in[:, o[2]:o[3]].T.astype(bf)
    wiqT = w_in[:, o[3]:o[4]].T.astype(bf)
    wiwT = w_in[:, o[4]:o[5]].T.astype(bf)
    wik = w_in[:, o[5]:o[6]].astype(bf)
    wb = w_in[:, o[6]:o[10]].astype(bf)
    wg = w_in[:, o[10]:o[12]].astype(bf)

    def full(a):
        return pl.BlockSpec(a.shape, lambda b, i: (0,) * a.ndim)

    row = lambda n: pl.BlockSpec((R, n), lambda b, i: (b * nS + i, 0))
    colT = lambda n: pl.BlockSpec((1, n, R), lambda b, i: (b, 0, i))
    f32 = jnp.float32
    outs = [
        (jax.ShapeDtypeStruct((T, A_WIDTH), bf), row(A_WIDTH)),
        (jax.ShapeDtypeStruct((T, IDX_DIM), bf), row(IDX_DIM)),
        (jax.ShapeDtypeStruct((B, A_WIDTH, S), bf), colT(A_WIDTH)),
        (jax.ShapeDtypeStruct((B, A_WIDTH, S), bf), colT(A_WIDTH)),
        (jax.ShapeDtypeStruct((B, IDX_WIDTH, S), bf), colT(IDX_WIDTH)),
        (jax.ShapeDtypeStruct((B, IDX_HEADS, S), f32), colT(IDX_HEADS)),
        (jax.ShapeDtypeStruct((T, B_WIDTH), f32), row(B_WIDTH)),
        (jax.ShapeDtypeStruct((T, B_WIDTH), f32), row(B_WIDTH)),
        (jax.ShapeDtypeStruct((T, B_WIDTH), f32), row(B_WIDTH)),
        (jax.ShapeDtypeStruct((T, B_WIDTH), f32), row(B_WIDTH)),
        (jax.ShapeDtypeStruct((T, D), bf), row(D)),
        (jax.ShapeDtypeStruct((T, D), bf), row(D)),
    ]
    ins = [x2, gain.reshape(1, D), wk, wik, wqT, wvT, wiqT, wiwT, wb, wg]
    in_specs = [row(D)] + [full(a) for a in ins[1:]]
    return pl.pallas_call(
        _inproj_kernel,
        grid=(B, nS),
        in_specs=in_specs,
        out_specs=[s for _, s in outs],
        out_shape=[s for s, _ in outs],
        compiler_params=_cparams(("parallel", "parallel")),
        name="inproj",
    )(*ins)


def _t5_bucket_table(n):
    d = np.arange(n)
    max_exact = REL_BUCKETS // 2
    nf = np.maximum(d, 1).astype(np.float64)
    large = max_exact + (np.log(nf / max_exact) / math.log(REL_MAX_DIST / max_exact)
                         * (REL_BUCKETS - max_exact)).astype(np.int32)
    large = np.minimum(large, REL_BUCKETS - 1)
    return np.where(d < max_exact, d, large)


def _dsa_kernel(qT_ref, k_ref, vT_ref, iqT_ref, iwT_ref, ik_ref, enear_ref,
                o_ref, sc_ref, qh_scr, m_scr, acc_scr, *, topk):
    TQ = qT_ref.shape[2]
    KC = TQ
    i = pl.program_id(1)
    nch = i + 1
    q0 = i * TQ
    f32 = jnp.float32
    bf16 = jnp.bfloat16
    key_id = lax.broadcasted_iota(jnp.int32, (KC, TQ), 0)
    qry_id = lax.broadcasted_iota(jnp.int32, (KC, TQ), 1)

    def col_reduce(x, op):
        return op(_fold_rows(x, op), axis=0, keepdims=True)

    iw = iwT_ref[0]

    def score_chunk(c, carry):
        rmin, rmax = carry
        k0 = pl.multiple_of(c * KC, KC)
        ik = ik_ref[pl.ds(k0, KC), :]
        acc = jnp.zeros((KC, TQ), f32)
        for h in range(IDX_HEADS):
            sh = jnp.dot(ik, iqT_ref[0, h * IDX_DIM:(h + 1) * IDX_DIM, :], preferred_element_type=f32)
            acc = acc + jnp.maximum(sh, 0.0) * iw[h:h + 1, :]
        valid = (k0 + key_id) <= (q0 + qry_id)
        sc_ref[pl.ds(k0, KC), :] = jnp.where(valid, acc, MASK_NEG)
        rmin = jnp.minimum(rmin, _fold_rows(jnp.where(valid, acc, -MASK_NEG), jnp.min))
        rmax = jnp.maximum(rmax, _fold_rows(jnp.where(valid, acc, MASK_NEG), jnp.max))
        return rmin, rmax

    rmin8, rmax8 = lax.fori_loop(
        0, nch, score_chunk,
        (jnp.full((V7X_SUBLANES, TQ), -MASK_NEG, f32), jnp.full((V7X_SUBLANES, TQ), MASK_NEG, f32)))
    rmin = jnp.min(rmin8, axis=0, keepdims=True)
    rmax = jnp.max(rmax8, axis=0, keepdims=True)

    def count_where(pred_fn):
        def one(c, acc):
            k0 = pl.multiple_of(c * KC, KC)
            blk = sc_ref[pl.ds(k0, KC), :]
            return acc + _fold_rows(jnp.where(pred_fn(blk), 1.0, 0.0), jnp.sum)
        acc = lax.fori_loop(0, nch // 2, lambda j, acc: one(2 * j + 1, one(2 * j, acc)),
                            jnp.zeros((V7X_SUBLANES, TQ), f32))
        acc = lax.cond(nch % 2 == 1, lambda a: one(nch - 1, a), lambda a: a, acc)
        return jnp.sum(acc, axis=0, keepdims=True)

    def band_min_max(lo, hi):
        def body(c, carry):
            bmin, bmax = carry
            k0 = pl.multiple_of(c * KC, KC)
            blk = sc_ref[pl.ds(k0, KC), :]
            bmin = jnp.minimum(bmin, _fold_rows(jnp.where(blk >= lo, blk, -MASK_NEG), jnp.min))
            bmax = jnp.maximum(bmax, _fold_rows(jnp.where(blk < hi, blk, MASK_NEG), jnp.max))
            return bmin, bmax
        bmin8, bmax8 = lax.fori_loop(
            0, nch, body,
            (jnp.full((V7X_SUBLANES, TQ), -MASK_NEG, f32), jnp.full((V7X_SUBLANES, TQ), MASK_NEG, f32)))
        return jnp.min(bmin8, axis=0, keepdims=True), jnp.max(bmax8, axis=0, keepdims=True)

    kf = float(topk)
    n_valid = (q0 + 1 + lax.broadcasted_iota(jnp.int32, (1, TQ), 1)).astype(f32)
    lo0 = rmin
    cnt0 = n_valid
    hi0 = rmax + jnp.maximum(jnp.abs(rmax) * 2.0 ** -20, 1e-30)
    done0 = jnp.where(cnt0 <= kf, 1.0, 0.0)

    def probe(st, lo_s, mid, tie):
        it, lo, hi, cnt, done = st
        active = done < 0.5
        lo_s = jnp.where(active, lo_s, lo)
        c = count_where(lambda blk: blk >= mid)
        feas = c >= kf
        move = active & jnp.logical_not(tie)
        lo_n = jnp.where(move & feas, mid, lo_s)
        cnt_n = jnp.where(move & feas, c, cnt)
        hi_n = jnp.where(move & jnp.logical_not(feas), mid, hi)
        done_n = jnp.where((active & tie) | (cnt_n <= kf), 1.0, done)
        return it + 1, lo_n, hi_n, cnt_n, done_n

    def halve(st):
        _, lo, hi, _, _ = st
        half = lo + 0.5 * (hi - lo)
        stuck = (half <= lo) | (half >= hi)
        return probe(st, lo, half, stuck)

    def snap(st):
        _, lo, hi, _, _ = st
        bmin, bmax = band_min_max(lo, hi)
        mid = bmin + 0.5 * (bmax - bmin)
        return probe(st, bmin, jnp.where(mid <= bmin, bmax, mid), bmax <= bmin)

    st = lax.while_loop(lambda st: (jnp.min(st[-1]) < 0.5) & (st[0] < BISECT_FAST_ITERS),
                        lambda st: halve(halve(st)), (jnp.int32(0), lo0, hi0, cnt0, done0))
    _, thr, _, cnt_thr, _ = lax.while_loop(lambda st: jnp.min(st[-1]) < 0.5, snap, st)

    tie_overflow = jnp.max(cnt_thr) > kf

    @pl.when(jnp.logical_not(tie_overflow))
    def _():
        def mask_chunk(c, _):
            k0 = pl.multiple_of(c * KC, KC)
            sc_ref[pl.ds(k0, KC), :] = jnp.where(sc_ref[pl.ds(k0, KC), :] >= thr, 0.0, MASK_NEG)
            return 0
        lax.fori_loop(0, nch, mask_chunk, 0)

    @pl.when(tie_overflow)
    def _():
        need = kf - count_where(lambda blk: blk > thr)
        tril = jnp.where(lax.broadcasted_iota(jnp.int32, (KC, KC), 1)
                         <= lax.broadcasted_iota(jnp.int32, (KC, KC), 0), 1.0, 0.0).astype(bf16)

        def mask_chunk(c, run):
            k0 = pl.multiple_of(c * KC, KC)
            blk = sc_ref[pl.ds(k0, KC), :]
            eq = jnp.where(blk == thr, 1.0, 0.0)
            pref = jnp.dot(tril, eq.astype(bf16), preferred_element_type=f32)
            sel = (blk > thr) | ((eq > 0.5) & (run + pref <= need))
            sc_ref[pl.ds(k0, KC), :] = jnp.where(sel, 0.0, MASK_NEG)
            return run + pref[KC - 1:KC, :]

        lax.fori_loop(0, nch, mask_chunk, jnp.zeros((1, TQ), f32))

    AK = min(ATT_KC, TQ)
    per = TQ // AK
    head0_q = (lax.broadcasted_iota(jnp.int32, (V7X_LANES, TQ), 0) // A_HEAD_DIM) == 0
    n_pairs = A_HEADS // 2

    m_scr[...] = jnp.full(m_scr.shape, MASK_NEG, f32)
    acc_scr[...] = jnp.zeros(acc_scr.shape, f32)
    v_row = lax.broadcasted_iota(jnp.int32, (V7X_LANES, AK), 0)
    denom_row = [A_HEAD_DIM * (1 - sub) for sub in range(2)]
    for p in range(n_pairs):
        q_pair = qT_ref[0, p * V7X_LANES:(p + 1) * V7X_LANES, :]
        zq = jnp.zeros_like(q_pair)
        qh_scr[2 * p] = jnp.where(head0_q, q_pair, zq)
        qh_scr[2 * p + 1] = jnp.where(head0_q, zq, q_pair)

    def step(c, bias_rows):
        k0 = pl.multiple_of(c * AK, AK)
        msk = sc_ref[pl.ds(k0, AK), :]
        for p in range(n_pairs):
            kp = k_ref[pl.ds(k0, AK), p * V7X_LANES:(p + 1) * V7X_LANES]
            vp = vT_ref[0, p * V7X_LANES:(p + 1) * V7X_LANES, pl.ds(k0, AK)]
            for sub in range(2):
                h = 2 * p + sub
                s = jnp.dot(kp, qh_scr[h], preferred_element_type=f32) + msk
                if bias_rows is not None:
                    s = s + enear_ref[h, bias_rows, :]
                m = m_scr[h:h + 1, :]
                m_new = jnp.maximum(m, col_reduce(s, jnp.max))
                alpha = jnp.exp2(m - m_new)
                pr = jnp.exp2(s - m_new)
                m_scr[h:h + 1, :] = m_new
                v_aug = jnp.where(v_row == denom_row[sub], jnp.ones_like(vp), vp)
                acc_scr[h] = alpha * acc_scr[h] + jnp.dot(v_aug, pr.astype(bf16), preferred_element_type=f32)

    def far(blk, _):
        for jj in range(per):
            step(blk * per + jj, None)
        return 0

    def near(block, first_chunk):
        for jj in range(per):
            step(first_chunk + jj, slice(block * TQ + jj * AK, block * TQ + (jj + 1) * AK))

    lax.fori_loop(0, jnp.maximum(i - 1, 0), far, 0)

    @pl.when(i >= 1)
    def _():
        near(0, (i - 1) * per)

    near(1, i * per)
    for p in range(n_pairs):
        outs = [acc_scr[2 * p + sub] / acc_scr[2 * p + sub, denom_row[sub]:denom_row[sub] + 1, :]
                for sub in range(2)]
        o_pair = jnp.where(head0_q, outs[0], outs[1])
        o_ref[:, p * V7X_LANES:(p + 1) * V7X_LANES] = o_pair.T.astype(o_ref.dtype)


def _dsa(k, ik, qT, vT, iqT, iwT, rel_bias, B, S):
    T = k.shape[0]
    TQ = min(ATT_Q, S)
    nQ = S // TQ
    topk = min(TOPK_MAX, S // 4)
    buckets = _t5_bucket_table(2 * TQ + 1)
    assert np.all(_t5_bucket_table(S + 1)[TQ + 1:] == REL_BUCKETS - 1)
    j = np.arange(2 * TQ)[:, None]
    r = np.arange(TQ)[None, :]
    dist = np.maximum(r + TQ - j, 0)
    onehot = (jnp.asarray(buckets[dist], jnp.int32)[None]
              == jnp.arange(REL_BUCKETS, dtype=jnp.int32)[:, None, None]).astype(jnp.float32)
    rel = (rel_bias.astype(jnp.float32) - rel_bias[REL_BUCKETS - 1].astype(jnp.float32)[None, :]) * LOG2_E
    enear = jnp.einsum('nh,njr->hjr', rel, onehot, precision=lax.Precision.HIGHEST)

    return pl.pallas_call(
        functools.partial(_dsa_kernel, topk=topk),
        grid=(B, nQ),
        in_specs=[
            pl.BlockSpec((1, A_WIDTH, TQ), lambda b, i: (b, 0, i)),
            pl.BlockSpec((S, A_WIDTH), lambda b, i: (b, 0)),
            pl.BlockSpec((1, A_WIDTH, S), lambda b, i: (b, 0, 0)),
            pl.BlockSpec((1, IDX_WIDTH, TQ), lambda b, i: (b, 0, i)),
            pl.BlockSpec((1, IDX_HEADS, TQ), lambda b, i: (b, 0, i)),
            pl.BlockSpec((S, IDX_DIM), lambda b, i: (b, 0)),
            pl.BlockSpec((A_HEADS, 2 * TQ, TQ), lambda b, i: (0, 0, 0)),
        ],
        out_specs=pl.BlockSpec((TQ, A_WIDTH), lambda b, i: (b * nQ + i, 0)),
        scratch_shapes=[pltpu.VMEM((S, TQ), jnp.float32),
                        pltpu.VMEM((A_HEADS, V7X_LANES, TQ), jnp.bfloat16),
                        pltpu.VMEM((A_HEADS, TQ), jnp.float32),
                        pltpu.VMEM((A_HEADS, V7X_LANES, TQ), jnp.float32)],
        out_shape=jax.ShapeDtypeStruct((T, A_WIDTH), jnp.bfloat16),
        compiler_params=_cparams(("parallel", "arbitrary")),
        name="dsa",
    )(qT, k, vT, iqT, iwT, ik, enear)


def _hgrn_kernel(bq_ref, bf_ref, bi_ref, bg_ref, lb_ref, gain_ref, o_ref,
                 st_ref, b_scr, q_scr, k_scr, v_scr, oi_scr, qd_s, kd_s, kl_s, vv_s, dec_s, oi_s, upd_s, st_s):
    R = bq_ref.shape[0]
    C = HGRN_CHUNK
    nC = R // C
    f32 = jnp.float32
    bf16 = jnp.bfloat16
    h = pl.program_id(1)

    @pl.when(pl.program_id(2) == 0)
    def _():
        st_ref[...] = jnp.zeros_like(st_ref)

    lb = lb_ref[pl.ds(h, 1), :]
    gain = gain_ref[pl.ds(h, 1), :]
    tril_incl = jnp.where(lax.broadcasted_iota(jnp.int32, (C, C), 1)
                          <= lax.broadcasted_iota(jnp.int32, (C, C), 0), 1.0, 0.0)
    srow = lax.broadcasted_iota(jnp.int32, (C, B_KEY_DIM), 0)

    def gates(r0):
        f = lb + (1.0 - lb) * _sigmoid(bf_ref[pl.ds(r0, C), :])
        qr = bq_ref[pl.ds(r0, C), :]
        return jnp.log(f), 1.0 - f, qr * _sigmoid(qr) * (B_KEY_DIM ** -0.5), bi_ref[pl.ds(r0, C), :]

    def cumdecay(g):
        tri = tril_incl.astype(bf16)
        g_hi = g.astype(bf16)
        rest = g - g_hi.astype(f32)
        g_mid = rest.astype(bf16)
        g_lo = (rest - g_mid.astype(f32)).astype(bf16)
        return (jnp.dot(tri, g_hi, preferred_element_type=f32) + jnp.dot(tri, g_mid, preferred_element_type=f32)
                + jnp.dot(tri, g_lo, preferred_element_type=f32))

    def advance(r0, st, qd, o_intra, upd, decay_row):
        o_inter = lax.dot_general(qd, st.astype(bf16), (((1,), (1,)), ((), ())), preferred_element_type=f32)
        og = bg_ref[pl.ds(r0, C), :]
        y = _rms(o_inter + o_intra, gain) * (og * _sigmoid(og))
        o_ref[pl.ds(r0, C), :] = y.astype(o_ref.dtype)
        return st * decay_row + upd

    f_all = lb + (1.0 - lb) * _sigmoid(bf_ref[...])
    g_all = jnp.log(f_all)
    decay = jnp.sum(g_all.reshape(nC, C, B_KEY_DIM), axis=1)
    safe = jnp.min(decay) >= -HGRN_SAFE_DECAY

    @pl.when(safe)
    def _():
        qr = bq_ref[...]
        qq = qr * _sigmoid(qr) * (B_KEY_DIM ** -0.5)
        kk = 1.0 - f_all
        b = jnp.concatenate([cumdecay(g_all[c * C:(c + 1) * C]) for c in range(nC)], axis=0)
        b_end = jnp.concatenate([jnp.broadcast_to(b[(c + 1) * C - 1:(c + 1) * C], (C, B_KEY_DIM))
                                 for c in range(nC)], axis=0)
        qd_s[...] = (qq * jnp.exp(b)).astype(bf16)
        kd_s[...] = (kk * jnp.exp(-b)).astype(bf16)
        kl_s[...] = (kk * jnp.exp(b_end - b)).astype(bf16)
        vv_s[...] = bi_ref[...].astype(bf16)
        dec_s[...] = jnp.exp(b_end)
        for c in range(nC):
            rows = slice(c * C, (c + 1) * C)
            att = lax.dot_general(qd_s[rows], kd_s[rows], (((1,), (1,)), ((), ())),
                                  preferred_element_type=f32) * tril_incl
            oi_s[rows] = jnp.dot(att.astype(bf16), vv_s[rows], preferred_element_type=f32)
            upd_s[c] = lax.dot_general(vv_s[rows], kl_s[rows], (((0,), (0,)), ((), ())),
                                       preferred_element_type=f32)
        st = st_ref[...]
        for c in range(nC):
            st_s[c] = st.astype(bf16)
            st = st * dec_s[c * C:c * C + 1] + upd_s[c]
        st_ref[...] = st
        for c in range(nC):
            rows = slice(c * C, (c + 1) * C)
            oi_s[rows] = oi_s[rows] + lax.dot_general(qd_s[rows], st_s[c], (((1,), (1,)), ((), ())),
                                                      preferred_element_type=f32)
        og = bg_ref[...]
        o_ref[...] = (_rms(oi_s[...], gain) * (og * _sigmoid(og))).astype(o_ref.dtype)

    @pl.when(jnp.logical_not(safe))
    def _():
        def body(c, st):
            r0 = pl.multiple_of(c * C, C)
            g, kk, qq, vv = gates(r0)
            b = cumdecay(g)
            b_last = b[C - 1:C, :]
            b_scr[...] = b
            q_scr[...] = qq
            k_scr[...] = kk
            v_scr[...] = vv

            def row(t, _):
                bt = b_scr[pl.ds(t, 1), :]
                qt = q_scr[pl.ds(t, 1), :]
                ex = jnp.where(srow <= t, bt - b_scr[...], -jnp.inf)
                a = jnp.sum(qt * k_scr[...] * jnp.exp(ex), axis=1, keepdims=True)
                oi_scr[pl.ds(t, 1), :] = jnp.sum(a * v_scr[...], axis=0, keepdims=True)
                return 0
            lax.fori_loop(0, C, row, 0)
            kd_last = (kk * jnp.exp(b_last - b)).astype(bf16)
            upd = lax.dot_general(vv.astype(bf16), kd_last, (((0,), (0,)), ((), ())),
                                  preferred_element_type=f32)
            return advance(r0, st, (qq * jnp.exp(b)).astype(bf16), oi_scr[...], upd, jnp.exp(b_last))
        st_ref[...] = lax.fori_loop(0, nC, body, st_ref[...])


def _hgrn(bq, bf, bi, bg, lb, gain, B, S):
    T = bq.shape[0]
    R = min(HGRN_ROWS, S)
    nR = S // R
    C = HGRN_CHUNK
    blk = pl.BlockSpec((R, B_KEY_DIM), lambda b, h, c: (b * nR + c, h))
    small = pl.BlockSpec((B_HEADS, B_KEY_DIM), lambda b, h, c: (0, 0))
    f32 = jnp.float32
    return pl.pallas_call(
        _hgrn_kernel,
        grid=(B, B_HEADS, nR),
        in_specs=[blk, blk, blk, blk, small, small],
        out_specs=blk,
        out_shape=jax.ShapeDtypeStruct((T, B_WIDTH), jnp.bfloat16),
        scratch_shapes=[pltpu.VMEM((B_VAL_DIM, B_KEY_DIM), f32)] +
                       [pltpu.VMEM((C, B_KEY_DIM), f32) for _ in range(5)] +
                       [pltpu.VMEM((R, B_KEY_DIM), jnp.bfloat16) for _ in range(4)] +
                       [pltpu.VMEM((R, B_KEY_DIM), f32) for _ in range(2)] +
                       [pltpu.VMEM((R // C, B_VAL_DIM, B_KEY_DIM), f32),
                        pltpu.VMEM((R // C, B_VAL_DIM, B_KEY_DIM), jnp.bfloat16)],
        compiler_params=_cparams(("parallel", "parallel", "arbitrary")),
        name="hgrn",
    )(bq, bf, bi, bg, lb, gain)


def _merge_kernel(x_ref, ya_ref, yb_ref, ga_ref, gb_ref, wa_ref, wb_ref, wo_ref, g_ref, wrh_ref, wrl_ref, br_ref,
                  x1_ref, xn_ref, lg_ref):
    f32 = jnp.float32
    ma = jnp.dot(ya_ref[...], wa_ref[...], preferred_element_type=f32)
    mb = jnp.dot(yb_ref[...], wb_ref[...], preferred_element_type=f32)
    merged = ga_ref[...].astype(f32) * ma + gb_ref[...].astype(f32) * mb
    x1 = x_ref[...] + jnp.dot(merged.astype(jnp.bfloat16), wo_ref[...], preferred_element_type=f32)
    x1_ref[...] = x1
    hn = _rms(x1, g_ref[...])
    xn_ref[...] = _pack_bf16_pairs(hn)
    hn_hi = hn.astype(jnp.bfloat16)
    hn_lo = (hn - hn_hi.astype(f32)).astype(jnp.bfloat16)
    lg_ref[...] = (jnp.dot(hn_hi, wrh_ref[...], preferred_element_type=f32)
                   + jnp.dot(hn_lo, wrh_ref[...], preferred_element_type=f32)
                   + jnp.dot(hn_hi, wrl_ref[...], preferred_element_type=f32) + br_ref[...])


def _merge(x2, ya, yb, ga, gb, w_up_a, w_up_b, w_out, gain, w_router, b_router):
    T, D = x2.shape
    R = min(PROJ_ROWS, T)
    bf = jnp.bfloat16
    wr_hi = w_router.astype(bf)
    wr_lo = (w_router - wr_hi.astype(jnp.float32)).astype(bf)
    ins = [x2, ya, yb, ga, gb, w_up_a.astype(bf), w_up_b.astype(bf), w_out.astype(bf),
           gain.reshape(1, D), wr_hi, wr_lo, b_router.reshape(1, N_EXPERTS)]
    row = lambda n: pl.BlockSpec((R, n), lambda i: (i, 0))
    full = lambda a: pl.BlockSpec(a.shape, lambda i: (0,) * a.ndim)
    in_specs = [row(D), row(A_WIDTH), row(B_WIDTH), row(D), row(D)] + [full(a) for a in ins[5:]]
    return pl.pallas_call(
        _merge_kernel,
        grid=(T // R,),
        in_specs=in_specs,
        out_specs=[row(D), row(D // 2), row(N_EXPERTS)],
        out_shape=[jax.ShapeDtypeStruct((T, D), jnp.float32), jax.ShapeDtypeStruct((T, D // 2), jnp.uint32),
                   jax.ShapeDtypeStruct((T, N_EXPERTS), jnp.float32)],
        compiler_params=_cparams(("parallel",)),
        name="merge",
    )(*ins)


def _route_kernel(lg_ref, eidx_ref, gate_ref, rank_ref, cnt_ref, run_ref):
    R = lg_ref.shape[0]
    f32 = jnp.float32

    @pl.when(pl.program_id(0) == 0)
    def _():
        run_ref[...] = jnp.zeros_like(run_ref)

    lg = lg_ref[...].T
    expert = lax.broadcasted_iota(jnp.int32, (N_EXPERTS, R), 0)
    work = lg
    onehots, vals, idxs = [], [], []
    for _ in range(TOP_K):
        m = jnp.max(work, axis=0, keepdims=True)
        idx = jnp.min(jnp.where(work == m, expert, N_EXPERTS), axis=0, keepdims=True)
        oh = expert == idx
        onehots.append(oh)
        vals.append(m)
        idxs.append(idx)
        work = jnp.where(oh, -jnp.inf, work)
    ex = [jnp.exp(v - vals[0]) for v in vals]
    den = ex[0] + ex[1] + ex[2] + ex[3]
    chosen = jnp.where(onehots[0] | onehots[1] | onehots[2] | onehots[3], 1.0, 0.0)
    earlier = jnp.where(lax.broadcasted_iota(jnp.int32, (R, R), 0)
                        < lax.broadcasted_iota(jnp.int32, (R, R), 1), 1.0, 0.0).astype(jnp.bfloat16)
    before = jnp.dot(chosen.astype(jnp.bfloat16), earlier, preferred_element_type=f32) + run_ref[...]
    slot = lax.broadcasted_iota(jnp.int32, (TOP_K, R), 0)
    eidx = jnp.zeros((TOP_K, R), jnp.int32)
    gate = jnp.zeros((TOP_K, R), f32)
    rank = jnp.zeros((TOP_K, R), f32)
    for k in range(TOP_K):
        eidx = jnp.where(slot == k, idxs[k], eidx)
        gate = jnp.where(slot == k, ex[k] / den, gate)
        rk = jnp.sum(jnp.where(onehots[k], before, 0.0), axis=0, keepdims=True)
        rank = jnp.where(slot == k, rk, rank)
    eidx_ref[...] = eidx
    gate_ref[...] = gate
    rank_ref[...] = rank.astype(jnp.int32)
    run_ref[...] = run_ref[...] + jnp.sum(chosen, axis=1, keepdims=True)
    cnt_ref[...] = run_ref[...].astype(jnp.int32)


def _route(logits):
    T = logits.shape[0]
    R = min(ROUTE_ROWS, T)
    col = pl.BlockSpec((TOP_K, R), lambda i: (0, i))
    return pl.pallas_call(
        _route_kernel,
        grid=(T // R,),
        in_specs=[pl.BlockSpec((R, N_EXPERTS), lambda i: (i, 0))],
        out_specs=[col, col, col, pl.BlockSpec((N_EXPERTS, 1), lambda i: (0, 0))],
        out_shape=[jax.ShapeDtypeStruct((TOP_K, T), jnp.int32), jax.ShapeDtypeStruct((TOP_K, T), jnp.float32),
                   jax.ShapeDtypeStruct((TOP_K, T), jnp.int32), jax.ShapeDtypeStruct((N_EXPERTS, 1), jnp.int32)],
        scratch_shapes=[pltpu.VMEM((N_EXPERTS, 1), jnp.float32)],
        compiler_params=_cparams(("arbitrary",)),
        name="route",
    )(logits)


def _sc_gather_rows(table, idx):
    N, W = table.shape
    M = idx.shape[0]
    workers = V7X_SC_CORES * V7X_SC_SUBCORES
    per_worker = M // workers
    pieces = per_worker // SC_GATHER_ROWS
    assert per_worker * workers == M and pieces * SC_GATHER_ROWS == per_worker and pieces % 2 == 0
    mesh = plsc.VectorSubcoreMesh(core_axis_name="core", subcore_axis_name="subcore",
                                  num_cores=V7X_SC_CORES, num_subcores=V7X_SC_SUBCORES)

    @functools.partial(
        pl.kernel, mesh=mesh,
        out_type=jax.ShapeDtypeStruct((M, W), table.dtype),
        scratch_types=[pltpu.VMEM((per_worker,), jnp.int32),
                       pltpu.VMEM((SC_GATHER_ROWS, W), table.dtype),
                       pltpu.VMEM((SC_GATHER_ROWS, W), table.dtype),
                       pltpu.SemaphoreType.DMA, pltpu.SemaphoreType.DMA],
    )
    def gather(table_hbm, idx_hbm, out_hbm, idx_v, rows_a, rows_b, sem_a, sem_b):
        worker = lax.axis_index("subcore") * V7X_SC_CORES + lax.axis_index("core")
        base = pl.multiple_of(worker * per_worker, SC_GATHER_ROWS)
        pltpu.sync_copy(idx_hbm.at[pl.ds(base, per_worker)], idx_v)
        bufs = ((rows_a, sem_a), (rows_b, sem_b))

        def fetch(g, buf, sem):
            off = pl.multiple_of(g * SC_GATHER_ROWS, SC_GATHER_ROWS)
            return pltpu.make_async_copy(table_hbm.at[idx_v.at[pl.ds(off, SC_GATHER_ROWS)]], buf, sem)

        fetch(0, *bufs[0]).start()

        @pl.loop(0, pieces, step=2)
        def _(g0):
            for half in range(2):
                g = g0 + half
                buf, sem = bufs[half]
                fetch(g, buf, sem).wait()

                @pl.when(g + 1 < pieces)
                def _():
                    fetch(g + 1, *bufs[1 - half]).start()

                off = pl.multiple_of(g * SC_GATHER_ROWS, SC_GATHER_ROWS)
                pltpu.sync_copy(buf, out_hbm.at[pl.ds(base + off, SC_GATHER_ROWS)])

    return gather(table, idx)


def _sc_scatter_rows(rows, dest, n_out):
    T, W = rows.shape
    slots = dest.shape[0]
    workers = V7X_SC_CORES * V7X_SC_SUBCORES
    per_worker = T // workers
    pieces = per_worker // SC_GATHER_ROWS
    assert per_worker * workers == T and pieces * SC_GATHER_ROWS == per_worker and pieces % 2 == 0
    idx = dest.reshape(slots, workers, pieces, SC_GATHER_ROWS).transpose(1, 2, 0, 3)
    mesh = plsc.VectorSubcoreMesh(core_axis_name="core", subcore_axis_name="subcore",
                                  num_cores=V7X_SC_CORES, num_subcores=V7X_SC_SUBCORES)

    @functools.partial(
        pl.kernel, mesh=mesh,
        out_type=jax.ShapeDtypeStruct((n_out, W), rows.dtype),
        scratch_types=[pltpu.VMEM((pieces, slots, SC_GATHER_ROWS), jnp.int32),
                       pltpu.VMEM((SC_GATHER_ROWS, W), rows.dtype),
                       pltpu.VMEM((SC_GATHER_ROWS, W), rows.dtype),
                       pltpu.SemaphoreType.DMA, pltpu.SemaphoreType.DMA, pltpu.SemaphoreType.DMA],
    )
    def scatter(rows_hbm, idx_hbm, out_hbm, idx_v, rows_a, rows_b, sem_a, sem_b, sem_out):
        worker = lax.axis_index("subcore") * V7X_SC_CORES + lax.axis_index("core")
        base = pl.multiple_of(worker * per_worker, SC_GATHER_ROWS)
        pltpu.sync_copy(idx_hbm.at[worker], idx_v)
        bufs = ((rows_a, sem_a), (rows_b, sem_b))

        def fetch(g, buf, sem):
            off = pl.multiple_of(g * SC_GATHER_ROWS, SC_GATHER_ROWS)
            return pltpu.make_async_copy(rows_hbm.at[pl.ds(base + off, SC_GATHER_ROWS)], buf, sem)

        fetch(0, *bufs[0]).start()

        @pl.loop(0, pieces, step=2)
        def _(g0):
            for half in range(2):
                g = g0 + half
                buf, sem = bufs[half]
                fetch(g, buf, sem).wait()

                @pl.when(g + 1 < pieces)
                def _():
                    fetch(g + 1, *bufs[1 - half]).start()

                puts = [pltpu.make_async_copy(buf, out_hbm.at[idx_v.at[g, k]], sem_out) for k in range(slots)]
                for put in puts:
                    put.start()
                for put in puts:
                    put.wait()

    return scatter(rows, idx)


def _experts_kernel(be_ref, nb_ref, first_ref, slot_ref, next_ref,
                    x_ref, wgu_hbm, bgu_ref, wd_hbm, bd_ref, o_ref,
                    wgu_f32, wd_f32, wgu_bf, wd_bf, sems):
    f32 = jnp.float32
    d_ff = wd_bf.shape[0]
    i = pl.program_id(0)
    live = i < nb_ref[0]

    def weight_copies(e, s):
        return (pltpu.make_async_copy(wgu_hbm.at[e], wgu_f32.at[s], sems.at[s, 0]),
                pltpu.make_async_copy(wd_hbm.at[e], wd_f32.at[s], sems.at[s, 1]))

    @pl.when(live & (first_ref[i] == 1))
    def _():
        e = be_ref[i]
        s = slot_ref[i]

        @pl.when(i == 0)
        def _():
            for cp in weight_copies(e, s):
                cp.start()

        for cp in weight_copies(e, s):
            cp.wait()

        @pl.when(next_ref[i] >= 0)
        def _():
            for cp in weight_copies(next_ref[i], 1 - s):
                cp.start()

        wgu_bf[...] = wgu_f32[s].astype(jnp.bfloat16)
        wd_bf[...] = wd_f32[s].astype(jnp.bfloat16)

    @pl.when(live)
    def _():
        x_hi, x_lo = _unpack_bf16_pairs(x_ref[...])
        half = x_hi.shape[1]
        gu = (jnp.dot(x_hi, wgu_bf[:half, :], preferred_element_type=f32)
              + jnp.dot(x_lo, wgu_bf[half:, :], preferred_element_type=f32) + bgu_ref[0])
        gate = jnp.minimum(gu[:, :d_ff], SWIGLU_LIMIT)
        lin = jnp.clip(gu[:, d_ff:], -SWIGLU_LIMIT, SWIGLU_LIMIT)
        act = (lin + 1.0) * gate * _sigmoid(SWIGLU_ALPHA * gate)
        y = jnp.dot(act.astype(jnp.bfloat16), wd_bf[...], preferred_element_type=f32) + bd_ref[0]
        o_ref[...] = _pack_bf16_pairs(y)

    @pl.when(pl.program_id(0) >= nb_ref[0])
    def _():
        o_ref[...] = jnp.zeros_like(o_ref)


def _experts(xs, plan, w_gu, b_gu, w_down, b_down):
    P, W = xs.shape
    E, D, F2 = w_gu.shape
    nb = P // EXPERT_ROWS
    by_expert = lambda i, be, *_: (be[i], 0, 0)
    grid_spec = pltpu.PrefetchScalarGridSpec(
        num_scalar_prefetch=5,
        grid=(nb,),
        in_specs=[
            pl.BlockSpec((EXPERT_ROWS, W), lambda i, *_: (i, 0)),
            pl.BlockSpec(memory_space=pl.ANY),
            pl.BlockSpec((1, 1, F2), by_expert),
            pl.BlockSpec(memory_space=pl.ANY),
            pl.BlockSpec((1, 1, D), by_expert),
        ],
        out_specs=pl.BlockSpec((EXPERT_ROWS, W), lambda i, *_: (i, 0)),
        scratch_shapes=[pltpu.VMEM((2, D, F2), jnp.float32), pltpu.VMEM((2, F2 // 2, D), jnp.float32),
                        pltpu.VMEM((D, F2), jnp.bfloat16), pltpu.VMEM((F2 // 2, D), jnp.bfloat16),
                        pltpu.SemaphoreType.DMA((2, 2))],
    )
    return pl.pallas_call(
        _experts_kernel,
        grid_spec=grid_spec,
        out_shape=jax.ShapeDtypeStruct((P, W), jnp.uint32),
        compiler_params=_cparams(("arbitrary",)),
        name="experts",
    )(*plan, xs, w_gu, b_gu.reshape(E, 1, F2), w_down, b_down.reshape(E, 1, D))


def _combine_kernel(ya_ref, x1_ref, gate_ref, g_ref, o_ref):
    half = x1_ref.shape[1] // 2
    f32 = jnp.float32
    gate = gate_ref[...].T
    x1 = x1_ref[...]
    y_hi = x1[:, :half]
    y_lo = x1[:, half:]
    for k in range(TOP_K):
        hi, lo = _unpack_bf16_pairs(ya_ref[k])
        y_hi = y_hi + gate[:, k:k + 1] * hi.astype(f32)
        y_lo = y_lo + gate[:, k:k + 1] * lo.astype(f32)
    o_ref[...] = _rms(jnp.concatenate([y_hi, y_lo], axis=1), g_ref[...])


def _combine(ya, x1, gates, gain):
    T, D = x1.shape
    R = min(COMBINE_ROWS, T)
    row = lambda w: pl.BlockSpec((R, w), lambda i: (i, 0))
    return pl.pallas_call(
        _combine_kernel,
        grid=(T // R,),
        in_specs=[pl.BlockSpec((TOP_K, R, D // 2), lambda i: (0, i, 0)), row(D),
                  pl.BlockSpec((TOP_K, R), lambda i: (0, i)), pl.BlockSpec((1, D), lambda i: (0, 0))],
        out_specs=row(D),
        out_shape=jax.ShapeDtypeStruct((T, D), jnp.float32),
        compiler_params=_cparams(("parallel",)),
        name="combine",
    )(ya, x1, gates, gain.reshape(1, D))


def _moe_plan(eidx, rank, counts, A):
    counts = counts.reshape(N_EXPERTS)
    padded = (counts + EXPERT_ROWS - 1) // EXPERT_ROWS * EXPERT_ROWS
    pad_ends = jnp.cumsum(padded)
    pad_starts = pad_ends - padded
    n_blocks = -(-A // EXPERT_ROWS) + N_EXPERTS
    ids = jnp.arange(N_EXPERTS, dtype=jnp.int32)
    dest = rank + jnp.sum(jnp.where(eidx[None] == ids[:, None, None], pad_starts[:, None, None], 0), axis=0)
    block_start = jnp.arange(n_blocks, dtype=pad_ends.dtype) * EXPERT_ROWS
    block_expert = jnp.minimum(jnp.sum(pad_ends[None, :] <= block_start[:, None], axis=1),
                               N_EXPERTS - 1).astype(jnp.int32)
    n_used = (pad_ends[-1] // EXPERT_ROWS).astype(jnp.int32).reshape(1)
    has_rows = counts > 0
    ordinal = jnp.cumsum(has_rows.astype(jnp.int32)) - 1
    later = has_rows[None, :] & (ids[None, :] > ids[:, None])
    next_expert = jnp.where(jnp.any(later, axis=1), jnp.argmax(later, axis=1), -1).astype(jnp.int32)
    block_first = ((block_start == pad_starts[block_expert]) & (block_start < pad_ends[-1])).astype(jnp.int32)
    block_slot = (ordinal[block_expert] % 2).astype(jnp.int32)
    block_next = next_expert[block_expert]
    plan = (block_expert, n_used, block_first, block_slot, block_next)
    return dest.astype(jnp.int32), plan, n_blocks


def kernel(x, w_in, w_up_a, w_up_b, w_out, norm_mix, norm_ffn, norm_final, hgrn_norm,
           lb_logits, rel_bias, w_router, b_router, w_gu, b_gu, w_down, b_down):
    B, S, D = x.shape
    T = B * S
    assert w_in.shape[0] == 1, "the final rmsnorm is fused into the single layer's combine stage"
    lb_all = jnp.cumsum(jax.nn.softmax(lb_logits.astype(jnp.float32), axis=0), axis=0)
    x2 = x.reshape(T, D)
    (k, ik, qT, vT, iqT, iwT, bq, bf, bi, bg, ga, gb) = _inproj(x2, norm_mix[0], w_in[0], B, S)
    ya = _dsa(k, ik, qT, vT, iqT, iwT, rel_bias, B, S)
    yb = _hgrn(bq, bf, bi, bg, lb_all[0].reshape(B_HEADS, B_KEY_DIM), hgrn_norm[0], B, S)
    x1, xn, logits = _merge(x2, ya, yb, ga, gb, w_up_a[0], w_up_b[0], w_out[0], norm_ffn[0],
                            w_router[0], b_router[0])
    eidx, gates, rank, counts = _route(logits)
    dest, plan, n_blocks = _moe_plan(eidx, rank, counts, T * TOP_K)
    P = n_blocks * EXPERT_ROWS
    A = T * TOP_K
    xs = _sc_scatter_rows(xn, dest, P)
    y_buf = _experts(xs, plan, w_gu[0], b_gu[0], w_down[0], b_down[0])
    ya = _sc_gather_rows(y_buf, dest.reshape(A)).reshape(TOP_K, T, D // 2)
    out = _combine(ya, x1, gates, norm_final)
    return out.reshape(B, S, D)
```

```python
import functools
import math

import numpy as np
import jax
import jax.numpy as jnp
from jax import lax
from jax.experimental import pallas as pl
from jax.experimental.pallas import tpu as pltpu
from jax.experimental.pallas import tpu_sc as plsc

A_HEADS = 8
A_HEAD_DIM = 64
IDX_HEADS = 8
IDX_DIM = 32
TOPK_MAX = 256
REL_BUCKETS = 32
REL_MAX_DIST = 128
B_HEADS = 4
B_KEY_DIM = 128
B_VAL_DIM = 128
N_EXPERTS = 32
TOP_K = 4
SWIGLU_LIMIT = 7.0
SWIGLU_ALPHA = 1.702
EPS = 1e-6
LOG2_E = math.log2(math.e)

A_WIDTH = A_HEADS * A_HEAD_DIM
B_WIDTH = B_HEADS * B_VAL_DIM
IDX_WIDTH = IDX_HEADS * IDX_DIM

V7X_LANES = 128
V7X_SUBLANES = 8
V7X_VMEM_LIMIT_BYTES = 56 * 1024 * 1024
V7X_SC_CORES = 2
V7X_SC_SUBCORES = 16

PROJ_ROWS = 512
ATT_Q = 256
ATT_KC = 128
HGRN_ROWS = 1024
HGRN_CHUNK = 64
HGRN_SAFE_DECAY = 70.0
ROUTE_ROWS = 512
EXPERT_ROWS = 512
COMBINE_ROWS = 512
SC_GATHER_ROWS = 64
MASK_NEG = -1e30
BISECT_FAST_ITERS = 26


def _cparams(dims):
    return pltpu.CompilerParams(dimension_semantics=dims, vmem_limit_bytes=V7X_VMEM_LIMIT_BYTES)


def _rms(x, gain):
    return x * lax.rsqrt(jnp.mean(x * x, axis=-1, keepdims=True) + EPS) * gain


def _sigmoid(x):
    return 1.0 / (1.0 + jnp.exp(-x))


def _pack_bf16_pairs(x):
    n = x.shape[1] // 2
    as_bits = lambda v: lax.bitcast_convert_type(v.astype(jnp.bfloat16).astype(jnp.float32), jnp.uint32)
    return (as_bits(x[:, :n]) & jnp.uint32(0xFFFF0000)) | (as_bits(x[:, n:]) >> 16)


def _unpack_bf16_pairs(w):
    hi = lax.bitcast_convert_type(w & jnp.uint32(0xFFFF0000), jnp.float32).astype(jnp.bfloat16)
    lo = lax.bitcast_convert_type(w << 16, jnp.float32).astype(jnp.bfloat16)
    return hi, lo


def _fold_rows(x, op):
    return op(x.reshape(x.shape[0] // V7X_SUBLANES, V7X_SUBLANES, x.shape[1]), axis=0)


def _inproj_kernel(x_ref, g_ref, wk_ref, wik_ref, wqT_ref, wvT_ref, wiqT_ref, wiwT_ref, wb_ref, wg_ref,
                   k_ref, ik_ref, qT_ref, vT_ref, iqT_ref, iwT_ref, bq_ref, bf_ref, bi_ref, bg_ref,
                   ga_ref, gb_ref):
    x = x_ref[...]
    hn = _rms(x, g_ref[...]).astype(jnp.bfloat16)

    def mm(w_ref):
        return jnp.dot(hn, w_ref[...], preferred_element_type=jnp.float32)

    def mm_t(w_ref):
        return lax.dot_general(w_ref[...], hn, (((1,), (1,)), ((), ())),
                               preferred_element_type=jnp.float32)

    k_ref[...] = mm(wk_ref).astype(jnp.bfloat16)
    ik_ref[...] = mm(wik_ref).astype(jnp.bfloat16)
    qT_ref[0] = (mm_t(wqT_ref) * (A_HEAD_DIM ** -0.5 * LOG2_E)).astype(jnp.bfloat16)
    vT_ref[0] = mm_t(wvT_ref).astype(jnp.bfloat16)
    iqT_ref[0] = mm_t(wiqT_ref).astype(jnp.bfloat16)
    iwT_ref[0] = mm_t(wiwT_ref) * ((IDX_HEADS * IDX_DIM) ** -0.5)
    hb = mm(wb_ref)
    bq_ref[...] = hb[:, 0 * B_WIDTH:1 * B_WIDTH]
    bf_ref[...] = hb[:, 1 * B_WIDTH:2 * B_WIDTH]
    bi_ref[...] = hb[:, 2 * B_WIDTH:3 * B_WIDTH]
    bg_ref[...] = hb[:, 3 * B_WIDTH:4 * B_WIDTH]
    d = ga_ref.shape[-1]
    hg = mm(wg_ref)
    ga_ref[...] = _sigmoid(hg[:, :d]).astype(jnp.bfloat16)
    gb_ref[...] = _sigmoid(hg[:, d:]).astype(jnp.bfloat16)


def _inproj(x2, gain, w_in, B, S):
    T, D = x2.shape
    R = min(PROJ_ROWS, S)
    nS = S // R
    o = np.cumsum((0, A_WIDTH, A_WIDTH, A_WIDTH, IDX_WIDTH, IDX_HEADS, IDX_DIM,
                   B_WIDTH, B_WIDTH, B_WIDTH, B_WIDTH, D, D))
    bf = jnp.bfloat16
    wqT = w_in[:, o[0]:o[1]].T.astype(bf)
    wk = w_in[:, o[1]:o[2]].astype(bf)
    wvT = w_in[:, o[2]:o[3]].T.astype(bf)
    wiqT = w_in[:, o[3]:o[4]].T.astype(bf)
    wiwT = w_in[:, o[4]:o[5]].T.astype(bf)
    wik = w_in[:, o[5]:o[6]].astype(bf)
    wb = w_in[:, o[6]:o[10]].astype(bf)
    wg = w_in[:, o[10]:o[12]].astype(bf)

    def full(a):
        return pl.BlockSpec(a.shape, lambda b, i: (0,) * a.ndim)

    row = lambda n: pl.BlockSpec((R, n), lambda b, i: (b * nS + i, 0))
    colT = lambda n: pl.BlockSpec((1, n, R), lambda b, i: (b, 0, i))
    f32 = jnp.float32
    outs = [
        (jax.ShapeDtypeStruct((T, A_WIDTH), bf), row(A_WIDTH)),
        (jax.ShapeDtypeStruct((T, IDX_DIM), bf), row(IDX_DIM)),
        (jax.ShapeDtypeStruct((B, A_WIDTH, S), bf), colT(A_WIDTH)),
        (jax.ShapeDtypeStruct((B, A_WIDTH, S), bf), colT(A_WIDTH)),
        (jax.ShapeDtypeStruct((B, IDX_WIDTH, S), bf), colT(IDX_WIDTH)),
        (jax.ShapeDtypeStruct((B, IDX_HEADS, S), f32), colT(IDX_HEADS)),
        (jax.ShapeDtypeStruct((T, B_WIDTH), f32), row(B_WIDTH)),
        (jax.ShapeDtypeStruct((T, B_WIDTH), f32), row(B_WIDTH)),
        (jax.ShapeDtypeStruct((T, B_WIDTH), f32), row(B_WIDTH)),
        (jax.ShapeDtypeStruct((T, B_WIDTH), f32), row(B_WIDTH)),
        (jax.ShapeDtypeStruct((T, D), bf), row(D)),
        (jax.ShapeDtypeStruct((T, D), bf), row(D)),
    ]
    ins = [x2, gain.reshape(1, D), wk, wik, wqT, wvT, wiqT, wiwT, wb, wg]
    in_specs = [row(D)] + [full(a) for a in ins[1:]]
    return pl.pallas_call(
        _inproj_kernel,
        grid=(B, nS),
        in_specs=in_specs,
        out_specs=[s for _, s in outs],
        out_shape=[s for s, _ in outs],
        compiler_params=_cparams(("parallel", "parallel")),
        name="inproj",
    )(*ins)


def _t5_bucket_table(n):
    d = np.arange(n)
    max_exact = REL_BUCKETS // 2
    nf = np.maximum(d, 1).astype(np.float64)
    large = max_exact + (np.log(nf / max_exact) / math.log(REL_MAX_DIST / max_exact)
                         * (REL_BUCKETS - max_exact)).astype(np.int32)
    large = np.minimum(large, REL_BUCKETS - 1)
    return np.where(d < max_exact, d, large)


def _dsa_kernel(qT_ref, k_ref, vT_ref, iqT_ref, iwT_ref, ik_ref, enear_ref,
                o_ref, sc_ref, qh_scr, m_scr, acc_scr, *, topk):
    TQ = qT_ref.shape[2]
    KC = TQ
    i = pl.program_id(1)
    nch = i + 1
    q0 = i * TQ
    f32 = jnp.float32
    bf16 = jnp.bfloat16
    key_id = lax.broadcasted_iota(jnp.int32, (KC, TQ), 0)
    qry_id = lax.broadcasted_iota(jnp.int32, (KC, TQ), 1)

    def col_reduce(x, op):
        return op(_fold_rows(x, op), axis=0, keepdims=True)

    iw = iwT_ref[0]

    def score_chunk(c, carry):
        rmin, rmax = carry
        k0 = pl.multiple_of(c * KC, KC)
        ik = ik_ref[pl.ds(k0, KC), :]
        acc = jnp.zeros((KC, TQ), f32)
        for h in range(IDX_HEADS):
            sh = jnp.dot(ik, iqT_ref[0, h * IDX_DIM:(h + 1) * IDX_DIM, :], preferred_element_type=f32)
            acc = acc + jnp.maximum(sh, 0.0) * iw[h:h + 1, :]
        valid = (k0 + key_id) <= (q0 + qry_id)
        sc_ref[pl.ds(k0, KC), :] = jnp.where(valid, acc, MASK_NEG)
        rmin = jnp.minimum(rmin, _fold_rows(jnp.where(valid, acc, -MASK_NEG), jnp.min))
        rmax = jnp.maximum(rmax, _fold_rows(jnp.where(valid, acc, MASK_NEG), jnp.max))
        return rmin, rmax

    rmin8, rmax8 = lax.fori_loop(
        0, nch, score_chunk,
        (jnp.full((V7X_SUBLANES, TQ), -MASK_NEG, f32), jnp.full((V7X_SUBLANES, TQ), MASK_NEG, f32)))
    rmin = jnp.min(rmin8, axis=0, keepdims=True)
    rmax = jnp.max(rmax8, axis=0, keepdims=True)

    def count_where(pred_fn):
        def one(c, acc):
            k0 = pl.multiple_of(c * KC, KC)
            blk = sc_ref[pl.ds(k0, KC), :]
            return acc + _fold_rows(jnp.where(pred_fn(blk), 1.0, 0.0), jnp.sum)
        acc = lax.fori_loop(0, nch // 2, lambda j, acc: one(2 * j + 1, one(2 * j, acc)),
                            jnp.zeros((V7X_SUBLANES, TQ), f32))
        acc = lax.cond(nch % 2 == 1, lambda a: one(nch - 1, a), lambda a: a, acc)
        return jnp.sum(acc, axis=0, keepdims=True)

    def band_min_max(lo, hi):
        def body(c, carry):
            bmin, bmax = carry
            k0 = pl.multiple_of(c * KC, KC)
            blk = sc_ref[pl.ds(k0, KC), :]
            bmin = jnp.minimum(bmin, _fold_rows(jnp.where(blk >= lo, blk, -MASK_NEG), jnp.min))
            bmax = jnp.maximum(bmax, _fold_rows(jnp.where(blk < hi, blk, MASK_NEG), jnp.max))
            return bmin, bmax
        bmin8, bmax8 = lax.fori_loop(
            0, nch, body,
            (jnp.full((V7X_SUBLANES, TQ), -MASK_NEG, f32), jnp.full((V7X_SUBLANES, TQ), MASK_NEG, f32)))
        return jnp.min(bmin8, axis=0, keepdims=True), jnp.max(bmax8, axis=0, keepdims=True)

    kf = float(topk)
    n_valid = (q0 + 1 + lax.broadcasted_iota(jnp.int32, (1, TQ), 1)).astype(f32)
    lo0 = rmin
    cnt0 = n_valid
    hi0 = rmax + jnp.maximum(jnp.abs(rmax) * 2.0 ** -20, 1e-30)
    done0 = jnp.where(cnt0 <= kf, 1.0, 0.0)

    def probe(st, lo_s, mid, tie):
        it, lo, hi, cnt, done = st
        active = done < 0.5
        lo_s = jnp.where(active, lo_s, lo)
        c = count_where(lambda blk: blk >= mid)
        feas = c >= kf
        move = active & jnp.logical_not(tie)
        lo_n = jnp.where(move & feas, mid, lo_s)
        cnt_n = jnp.where(move & feas, c, cnt)
        hi_n = jnp.where(move & jnp.logical_not(feas), mid, hi)
        done_n = jnp.where((active & tie) | (cnt_n <= kf), 1.0, done)
        return it + 1, lo_n, hi_n, cnt_n, done_n

    def halve(st):
        _, lo, hi, _, _ = st
        half = lo + 0.5 * (hi - lo)
        stuck = (half <= lo) | (half >= hi)
        return probe(st, lo, half, stuck)

    def snap(st):
        _, lo, hi, _, _ = st
        bmin, bmax = band_min_max(lo, hi)
        mid = bmin + 0.5 * (bmax - bmin)
        return probe(st, bmin, jnp.where(mid <= bmin, bmax, mid), bmax <= bmin)

    st = lax.while_loop(lambda st: (jnp.min(st[-1]) < 0.5) & (st[0] < BISECT_FAST_ITERS),
                        lambda st: halve(halve(st)), (jnp.int32(0), lo0, hi0, cnt0, done0))
    _, thr, _, cnt_thr, _ = lax.while_loop(lambda st: jnp.min(st[-1]) < 0.5, snap, st)

    tie_overflow = jnp.max(cnt_thr) > kf

    @pl.when(jnp.logical_not(tie_overflow))
    def _():
        def mask_chunk(c, _):
            k0 = pl.multiple_of(c * KC, KC)
            sc_ref[pl.ds(k0, KC), :] = jnp.where(sc_ref[pl.ds(k0, KC), :] >= thr, 0.0, MASK_NEG)
            return 0
        lax.fori_loop(0, nch, mask_chunk, 0)

    @pl.when(tie_overflow)
    def _():
        need = kf - count_where(lambda blk: blk > thr)
        tril = jnp.where(lax.broadcasted_iota(jnp.int32, (KC, KC), 1)
                         <= lax.broadcasted_iota(jnp.int32, (KC, KC), 0), 1.0, 0.0).astype(bf16)

        def mask_chunk(c, run):
            k0 = pl.multiple_of(c * KC, KC)
            blk = sc_ref[pl.ds(k0, KC), :]
            eq = jnp.where(blk == thr, 1.0, 0.0)
            pref = jnp.dot(tril, eq.astype(bf16), preferred_element_type=f32)
            sel = (blk > thr) | ((eq > 0.5) & (run + pref <= need))
            sc_ref[pl.ds(k0, KC), :] = jnp.where(sel, 0.0, MASK_NEG)
            return run + pref[KC - 1:KC, :]

        lax.fori_loop(0, nch, mask_chunk, jnp.zeros((1, TQ), f32))

    AK = min(ATT_KC, TQ)
    per = TQ // AK
    head0_q = (lax.broadcasted_iota(jnp.int32, (V7X_LANES, TQ), 0) // A_HEAD_DIM) == 0
    n_pairs = A_HEADS // 2

    m_scr[...] = jnp.full(m_scr.shape, MASK_NEG, f32)
    acc_scr[...] = jnp.zeros(acc_scr.shape, f32)
    v_row = lax.broadcasted_iota(jnp.int32, (V7X_LANES, AK), 0)
    denom_row = [A_HEAD_DIM * (1 - sub) for sub in range(2)]
    for p in range(n_pairs):
        q_pair = qT_ref[0, p * V7X_LANES:(p + 1) * V7X_LANES, :]
        zq = jnp.zeros_like(q_pair)
        qh_scr[2 * p] = jnp.where(head0_q, q_pair, zq)
        qh_scr[2 * p + 1] = jnp.where(head0_q, zq, q_pair)

    def step(c, bias_rows):
        k0 = pl.multiple_of(c * AK, AK)
        msk = sc_ref[pl.ds(k0, AK), :]
        for p in range(n_pairs):
            kp = k_ref[pl.ds(k0, AK), p * V7X_LANES:(p + 1) * V7X_LANES]
            vp = vT_ref[0, p * V7X_LANES:(p + 1) * V7X_LANES, pl.ds(k0, AK)]
            for sub in range(2):
                h = 2 * p + sub
                s = jnp.dot(kp, qh_scr[h], preferred_element_type=f32) + msk
                if bias_rows is not None:
                    s = s + enear_ref[h, bias_rows, :]
                m = m_scr[h:h + 1, :]
                m_new = jnp.maximum(m, col_reduce(s, jnp.max))
                alpha = jnp.exp2(m - m_new)
                pr = jnp.exp2(s - m_new)
                m_scr[h:h + 1, :] = m_new
                v_aug = jnp.where(v_row == denom_row[sub], jnp.ones_like(vp), vp)
                acc_scr[h] = alpha * acc_scr[h] + jnp.dot(v_aug, pr.astype(bf16), preferred_element_type=f32)

    def far(blk, _):
        for jj in range(per):
            step(blk * per + jj, None)
        return 0

    def near(block, first_chunk):
        for jj in range(per):
            step(first_chunk + jj, slice(block * TQ + jj * AK, block * TQ + (jj + 1) * AK))

    lax.fori_loop(0, jnp.maximum(i - 1, 0), far, 0)

    @pl.when(i >= 1)
    def _():
        near(0, (i - 1) * per)

    near(1, i * per)
    for p in range(n_pairs):
        outs = [acc_scr[2 * p + sub] / acc_scr[2 * p + sub, denom_row[sub]:denom_row[sub] + 1, :]
                for sub in range(2)]
        o_pair = jnp.where(head0_q, outs[0], outs[1])
        o_ref[:, p * V7X_LANES:(p + 1) * V7X_LANES] = o_pair.T.astype(o_ref.dtype)


def _dsa(k, ik, qT, vT, iqT, iwT, rel_bias, B, S):
    T = k.shape[0]
    TQ = min(ATT_Q, S)
    nQ = S // TQ
    topk = min(TOPK_MAX, S // 4)
    buckets = _t5_bucket_table(2 * TQ + 1)
    assert np.all(_t5_bucket_table(S + 1)[TQ + 1:] == REL_BUCKETS - 1)
    j = np.arange(2 * TQ)[:, None]
    r = np.arange(TQ)[None, :]
    dist = np.maximum(r + TQ - j, 0)
    onehot = (jnp.asarray(buckets[dist], jnp.int32)[None]
              == jnp.arange(REL_BUCKETS, dtype=jnp.int32)[:, None, None]).astype(jnp.float32)
    rel = (rel_bias.astype(jnp.float32) - rel_bias[REL_BUCKETS - 1].astype(jnp.float32)[None, :]) * LOG2_E
    enear = jnp.einsum('nh,njr->hjr', rel, onehot, precision=lax.Precision.HIGHEST)

    return pl.pallas_call(
        functools.partial(_dsa_kernel, topk=topk),
        grid=(B, nQ),
        in_specs=[
            pl.BlockSpec((1, A_WIDTH, TQ), lambda b, i: (b, 0, i)),
            pl.BlockSpec((S, A_WIDTH), lambda b, i: (b, 0)),
            pl.BlockSpec((1, A_WIDTH, S), lambda b, i: (b, 0, 0)),
            pl.BlockSpec((1, IDX_WIDTH, TQ), lambda b, i: (b, 0, i)),
            pl.BlockSpec((1, IDX_HEADS, TQ), lambda b, i: (b, 0, i)),
            pl.BlockSpec((S, IDX_DIM), lambda b, i: (b, 0)),
            pl.BlockSpec((A_HEADS, 2 * TQ, TQ), lambda b, i: (0, 0, 0)),
        ],
        out_specs=pl.BlockSpec((TQ, A_WIDTH), lambda b, i: (b * nQ + i, 0)),
        scratch_shapes=[pltpu.VMEM((S, TQ), jnp.float32),
                        pltpu.VMEM((A_HEADS, V7X_LANES, TQ), jnp.bfloat16),
                        pltpu.VMEM((A_HEADS, TQ), jnp.float32),
                        pltpu.VMEM((A_HEADS, V7X_LANES, TQ), jnp.float32)],
        out_shape=jax.ShapeDtypeStruct((T, A_WIDTH), jnp.bfloat16),
        compiler_params=_cparams(("parallel", "arbitrary")),
        name="dsa",
    )(qT, k, vT, iqT, iwT, ik, enear)


def _hgrn_kernel(bq_ref, bf_ref, bi_ref, bg_ref, lb_ref, gain_ref, o_ref,
                 st_ref, b_scr, q_scr, k_scr, v_scr, oi_scr, qd_s, kd_s, kl_s, vv_s, dec_s, oi_s, upd_s, st_s):
    R = bq_ref.shape[0]
    C = HGRN_CHUNK
    nC = R // C
    f32 = jnp.float32
    bf16 = jnp.bfloat16
    h = pl.program_id(1)

    @pl.when(pl.program_id(2) == 0)
    def _():
        st_ref[...] = jnp.zeros_like(st_ref)

    lb = lb_ref[pl.ds(h, 1), :]
    gain = gain_ref[pl.ds(h, 1), :]
    tril_incl = jnp.where(lax.broadcasted_iota(jnp.int32, (C, C), 1)
                          <= lax.broadcasted_iota(jnp.int32, (C, C), 0), 1.0, 0.0)
    srow = lax.broadcasted_iota(jnp.int32, (C, B_KEY_DIM), 0)

    def gates(r0):
        f = lb + (1.0 - lb) * _sigmoid(bf_ref[pl.ds(r0, C), :])
        qr = bq_ref[pl.ds(r0, C), :]
        return jnp.log(f), 1.0 - f, qr * _sigmoid(qr) * (B_KEY_DIM ** -0.5), bi_ref[pl.ds(r0, C), :]

    def cumdecay(g):
        tri = tril_incl.astype(bf16)
        g_hi = g.astype(bf16)
        rest = g - g_hi.astype(f32)
        g_mid = rest.astype(bf16)
        g_lo = (rest - g_mid.astype(f32)).astype(bf16)
        return (jnp.dot(tri, g_hi, preferred_element_type=f32) + jnp.dot(tri, g_mid, preferred_element_type=f32)
                + jnp.dot(tri, g_lo, preferred_element_type=f32))

    def advance(r0, st, qd, o_intra, upd, decay_row):
        o_inter = lax.dot_general(qd, st.astype(bf16), (((1,), (1,)), ((), ())), preferred_element_type=f32)
        og = bg_ref[pl.ds(r0, C), :]
        y = _rms(o_inter + o_intra, gain) * (og * _sigmoid(og))
        o_ref[pl.ds(r0, C), :] = y.astype(o_ref.dtype)
        return st * decay_row + upd

    f_all = lb + (1.0 - lb) * _sigmoid(bf_ref[...])
    g_all = jnp.log(f_all)
    decay = jnp.sum(g_all.reshape(nC, C, B_KEY_DIM), axis=1)
    safe = jnp.min(decay) >= -HGRN_SAFE_DECAY

    @pl.when(safe)
    def _():
        qr = bq_ref[...]
        qq = qr * _sigmoid(qr) * (B_KEY_DIM ** -0.5)
        kk = 1.0 - f_all
        b = jnp.concatenate([cumdecay(g_all[c * C:(c + 1) * C]) for c in range(nC)], axis=0)
        b_end = jnp.concatenate([jnp.broadcast_to(b[(c + 1) * C - 1:(c + 1) * C], (C, B_KEY_DIM))
                                 for c in range(nC)], axis=0)
        qd_s[...] = (qq * jnp.exp(b)).astype(bf16)
        kd_s[...] = (kk * jnp.exp(-b)).astype(bf16)
        kl_s[...] = (kk * jnp.exp(b_end - b)).astype(bf16)
        vv_s[...] = bi_ref[...].astype(bf16)
        dec_s[...] = jnp.exp(b_end)
        for c in range(nC):
            rows = slice(c * C, (c + 1) * C)
            att = lax.dot_general(qd_s[rows], kd_s[rows], (((1,), (1,)), ((), ())),
                                  preferred_element_type=f32) * tril_incl
            oi_s[rows] = jnp.dot(att.astype(bf16), vv_s[rows], preferred_element_type=f32)
            upd_s[c] = lax.dot_general(vv_s[rows], kl_s[rows], (((0,), (0,)), ((), ())),
                                       preferred_element_type=f32)
        st = st_ref[...]
        for c in range(nC):
            st_s[c] = st.astype(bf16)
            st = st * dec_s[c * C:c * C + 1] + upd_s[c]
        st_ref[...] = st
        for c in range(nC):
            rows = slice(c * C, (c + 1) * C)
            oi_s[rows] = oi_s[rows] + lax.dot_general(qd_s[rows], st_s[c], (((1,), (1,)), ((), ())),
                                                      preferred_element_type=f32)
        og = bg_ref[...]
        o_ref[...] = (_rms(oi_s[...], gain) * (og * _sigmoid(og))).astype(o_ref.dtype)

    @pl.when(jnp.logical_not(safe))
    def _():
        def body(c, st):
            r0 = pl.multiple_of(c * C, C)
            g, kk, qq, vv = gates(r0)
            b = cumdecay(g)
            b_last = b[C - 1:C, :]
            b_scr[...] = b
            q_scr[...] = qq
            k_scr[...] = kk
            v_scr[...] = vv

            def row(t, _):
                bt = b_scr[pl.ds(t, 1), :]
                qt = q_scr[pl.ds(t, 1), :]
                ex = jnp.where(srow <= t, bt - b_scr[...], -jnp.inf)
                a = jnp.sum(qt * k_scr[...] * jnp.exp(ex), axis=1, keepdims=True)
                oi_scr[pl.ds(t, 1), :] = jnp.sum(a * v_scr[...], axis=0, keepdims=True)
                return 0
            lax.fori_loop(0, C, row, 0)
            kd_last = (kk * jnp.exp(b_last - b)).astype(bf16)
            upd = lax.dot_general(vv.astype(bf16), kd_last, (((0,), (0,)), ((), ())),
                                  preferred_element_type=f32)
            return advance(r0, st, (qq * jnp.exp(b)).astype(bf16), oi_scr[...], upd, jnp.exp(b_last))
        st_ref[...] = lax.fori_loop(0, nC, body, st_ref[...])


def _hgrn(bq, bf, bi, bg, lb, gain, B, S):
    T = bq.shape[0]
    R = min(HGRN_ROWS, S)
    nR = S // R
    C = HGRN_CHUNK
    blk = pl.BlockSpec((R, B_KEY_DIM), lambda b, h, c: (b * nR + c, h))
    small = pl.BlockSpec((B_HEADS, B_KEY_DIM), lambda b, h, c: (0, 0))
    f32 = jnp.float32
    return pl.pallas_call(
        _hgrn_kernel,
        grid=(B, B_HEADS, nR),
        in_specs=[blk, blk, blk, blk, small, small],
        out_specs=blk,
        out_shape=jax.ShapeDtypeStruct((T, B_WIDTH), jnp.bfloat16),
        scratch_shapes=[pltpu.VMEM((B_VAL_DIM, B_KEY_DIM), f32)] +
                       [pltpu.VMEM((C, B_KEY_DIM), f32) for _ in range(5)] +
                       [pltpu.VMEM((R, B_KEY_DIM), jnp.bfloat16) for _ in range(4)] +
                       [pltpu.VMEM((R, B_KEY_DIM), f32) for _ in range(2)] +
                       [pltpu.VMEM((R // C, B_VAL_DIM, B_KEY_DIM), f32),
                        pltpu.VMEM((R // C, B_VAL_DIM, B_KEY_DIM), jnp.bfloat16)],
        compiler_params=_cparams(("parallel", "parallel", "arbitrary")),
        name="hgrn",
    )(bq, bf, bi, bg, lb, gain)


def _merge_kernel(x_ref, ya_ref, yb_ref, ga_ref, gb_ref, wa_ref, wb_ref, wo_ref, g_ref, wrh_ref, wrl_ref, br_ref,
                  x1_ref, xn_ref, lg_ref):
    f32 = jnp.float32
    ma = jnp.dot(ya_ref[...], wa_ref[...], preferred_element_type=f32)
    mb = jnp.dot(yb_ref[...], wb_ref[...], preferred_element_type=f32)
    merged = ga_ref[...].astype(f32) * ma + gb_ref[...].astype(f32) * mb
    x1 = x_ref[...] + jnp.dot(merged.astype(jnp.bfloat16), wo_ref[...], preferred_element_type=f32)
    x1_ref[...] = x1
    hn = _rms(x1, g_ref[...])
    xn_ref[...] = _pack_bf16_pairs(hn)
    hn_hi = hn.astype(jnp.bfloat16)
    hn_lo = (hn - hn_hi.astype(f32)).astype(jnp.bfloat16)
    lg_ref[...] = (jnp.dot(hn_hi, wrh_ref[...], preferred_element_type=f32)
                   + jnp.dot(hn_lo, wrh_ref[...], preferred_element_type=f32)
                   + jnp.dot(hn_hi, wrl_ref[...], preferred_element_type=f32) + br_ref[...])


def _merge(x2, ya, yb, ga, gb, w_up_a, w_up_b, w_out, gain, w_router, b_router):
    T, D = x2.shape
    R = min(PROJ_ROWS, T)
    bf = jnp.bfloat16
    wr_hi = w_router.astype(bf)
    wr_lo = (w_router - wr_hi.astype(jnp.float32)).astype(bf)
    ins = [x2, ya, yb, ga, gb, w_up_a.astype(bf), w_up_b.astype(bf), w_out.astype(bf),
           gain.reshape(1, D), wr_hi, wr_lo, b_router.reshape(1, N_EXPERTS)]
    row = lambda n: pl.BlockSpec((R, n), lambda i: (i, 0))
    full = lambda a: pl.BlockSpec(a.shape, lambda i: (0,) * a.ndim)
    in_specs = [row(D), row(A_WIDTH), row(B_WIDTH), row(D), row(D)] + [full(a) for a in ins[5:]]
    return pl.pallas_call(
        _merge_kernel,
        grid=(T // R,),
        in_specs=in_specs,
        out_specs=[row(D), row(D // 2), row(N_EXPERTS)],
        out_shape=[jax.ShapeDtypeStruct((T, D), jnp.float32), jax.ShapeDtypeStruct((T, D // 2), jnp.uint32),
                   jax.ShapeDtypeStruct((T, N_EXPERTS), jnp.float32)],
        compiler_params=_cparams(("parallel",)),
        name="merge",
    )(*ins)


def _route_kernel(lg_ref, eidx_ref, gate_ref, rank_ref, cnt_ref, run_ref):
    R = lg_ref.shape[0]
    f32 = jnp.float32

    @pl.when(pl.program_id(0) == 0)
    def _():
        run_ref[...] = jnp.zeros_like(run_ref)

    lg = lg_ref[...].T
    expert = lax.broadcasted_iota(jnp.int32, (N_EXPERTS, R), 0)
    work = lg
    onehots, vals, idxs = [], [], []
    for _ in range(TOP_K):
        m = jnp.max(work, axis=0, keepdims=True)
        idx = jnp.min(jnp.where(work == m, expert, N_EXPERTS), axis=0, keepdims=True)
        oh = expert == idx
        onehots.append(oh)
        vals.append(m)
        idxs.append(idx)
        work = jnp.where(oh, -jnp.inf, work)
    ex = [jnp.exp(v - vals[0]) for v in vals]
    den = ex[0] + ex[1] + ex[2] + ex[3]
    chosen = jnp.where(onehots[0] | onehots[1] | onehots[2] | onehots[3], 1.0, 0.0)
    earlier = jnp.where(lax.broadcasted_iota(jnp.int32, (R, R), 0)
                        < lax.broadcasted_iota(jnp.int32, (R, R), 1), 1.0, 0.0).astype(jnp.bfloat16)
    before = jnp.dot(chosen.astype(jnp.bfloat16), earlier, preferred_element_type=f32) + run_ref[...]
    slot = lax.broadcasted_iota(jnp.int32, (TOP_K, R), 0)
    eidx = jnp.zeros((TOP_K, R), jnp.int32)
    gate = jnp.zeros((TOP_K, R), f32)
    rank = jnp.zeros((TOP_K, R), f32)
    for k in range(TOP_K):
        eidx = jnp.where(slot == k, idxs[k], eidx)
        gate = jnp.where(slot == k, ex[k] / den, gate)
        rk = jnp.sum(jnp.where(onehots[k], before, 0.0), axis=0, keepdims=True)
        rank = jnp.where(slot == k, rk, rank)
    eidx_ref[...] = eidx
    gate_ref[...] = gate
    rank_ref[...] = rank.astype(jnp.int32)
    run_ref[...] = run_ref[...] + jnp.sum(chosen, axis=1, keepdims=True)
    cnt_ref[...] = run_ref[...].astype(jnp.int32)


def _route(logits):
    T = logits.shape[0]
    R = min(ROUTE_ROWS, T)
    col = pl.BlockSpec((TOP_K, R), lambda i: (0, i))
    return pl.pallas_call(
        _route_kernel,
        grid=(T // R,),
        in_specs=[pl.BlockSpec((R, N_EXPERTS), lambda i: (i, 0))],
        out_specs=[col, col, col, pl.BlockSpec((N_EXPERTS, 1), lambda i: (0, 0))],
        out_shape=[jax.ShapeDtypeStruct((TOP_K, T), jnp.int32), jax.ShapeDtypeStruct((TOP_K, T), jnp.float32),
                   jax.ShapeDtypeStruct((TOP_K, T), jnp.int32), jax.ShapeDtypeStruct((N_EXPERTS, 1), jnp.int32)],
        scratch_shapes=[pltpu.VMEM((N_EXPERTS, 1), jnp.float32)],
        compiler_params=_cparams(("arbitrary",)),
        name="route",
    )(logits)


def _sc_gather_rows(table, idx):
    N, W = table.shape
    M = idx.shape[0]
    workers = V7X_SC_CORES * V7X_SC_SUBCORES
    per_worker = M // workers
    pieces = per_worker // SC_GATHER_ROWS
    assert per_worker * workers == M and pieces * SC_GATHER_ROWS == per_worker and pieces % 2 == 0
    mesh = plsc.VectorSubcoreMesh(core_axis_name="core", subcore_axis_name="subcore",
                                  num_cores=V7X_SC_CORES, num_subcores=V7X_SC_SUBCORES)

    @functools.partial(
        pl.kernel, mesh=mesh,
        out_type=jax.ShapeDtypeStruct((M, W), table.dtype),
        scratch_types=[pltpu.VMEM((per_worker,), jnp.int32),
                       pltpu.VMEM((SC_GATHER_ROWS, W), table.dtype),
                       pltpu.VMEM((SC_GATHER_ROWS, W), table.dtype),
                       pltpu.SemaphoreType.DMA, pltpu.SemaphoreType.DMA],
    )
    def gather(table_hbm, idx_hbm, out_hbm, idx_v, rows_a, rows_b, sem_a, sem_b):
        worker = lax.axis_index("subcore") * V7X_SC_CORES + lax.axis_index("core")
        base = pl.multiple_of(worker * per_worker, SC_GATHER_ROWS)
        pltpu.sync_copy(idx_hbm.at[pl.ds(base, per_worker)], idx_v)
        bufs = ((rows_a, sem_a), (rows_b, sem_b))

        def fetch(g, buf, sem):
            off = pl.multiple_of(g * SC_GATHER_ROWS, SC_GATHER_ROWS)
            return pltpu.make_async_copy(table_hbm.at[idx_v.at[pl.ds(off, SC_GATHER_ROWS)]], buf, sem)

        fetch(0, *bufs[0]).start()

        @pl.loop(0, pieces, step=2)
        def _(g0):
            for half in range(2):
                g = g0 + half
                buf, sem = bufs[half]
                fetch(g, buf, sem).wait()

                @pl.when(g + 1 < pieces)
                def _():
                    fetch(g + 1, *bufs[1 - half]).start()

                off = pl.multiple_of(g * SC_GATHER_ROWS, SC_GATHER_ROWS)
                pltpu.sync_copy(buf, out_hbm.at[pl.ds(base + off, SC_GATHER_ROWS)])

    return gather(table, idx)


def _sc_scatter_rows(rows, dest, n_out):
    T, W = rows.shape
    slots = dest.shape[0]
    workers = V7X_SC_CORES * V7X_SC_SUBCORES
    per_worker = T // workers
    pieces = per_worker // SC_GATHER_ROWS
    assert per_worker * workers == T and pieces * SC_GATHER_ROWS == per_worker and pieces % 2 == 0
    idx = dest.reshape(slots, workers, pieces, SC_GATHER_ROWS).transpose(1, 2, 0, 3)
    mesh = plsc.VectorSubcoreMesh(core_axis_name="core", subcore_axis_name="subcore",
                                  num_cores=V7X_SC_CORES, num_subcores=V7X_SC_SUBCORES)

    @functools.partial(
        pl.kernel, mesh=mesh,
        out_type=jax.ShapeDtypeStruct((n_out, W), rows.dtype),
        scratch_types=[pltpu.VMEM((pieces, slots, SC_GATHER_ROWS), jnp.int32),
                       pltpu.VMEM((SC_GATHER_ROWS, W), rows.dtype),
                       pltpu.VMEM((SC_GATHER_ROWS, W), rows.dtype),
                       pltpu.SemaphoreType.DMA, pltpu.SemaphoreType.DMA, pltpu.SemaphoreType.DMA],
    )
    def scatter(rows_hbm, idx_hbm, out_hbm, idx_v, rows_a, rows_b, sem_a, sem_b, sem_out):
        worker = lax.axis_index("subcore") * V7X_SC_CORES + lax.axis_index("core")
        base = pl.multiple_of(worker * per_worker, SC_GATHER_ROWS)
        pltpu.sync_copy(idx_hbm.at[worker], idx_v)
        bufs = ((rows_a, sem_a), (rows_b, sem_b))

        def fetch(g, buf, sem):
            off = pl.multiple_of(g * SC_GATHER_ROWS, SC_GATHER_ROWS)
            return pltpu.make_async_copy(rows_hbm.at[pl.ds(base + off, SC_GATHER_ROWS)], buf, sem)

        fetch(0, *bufs[0]).start()

        @pl.loop(0, pieces, step=2)
        def _(g0):
            for half in range(2):
                g = g0 + half
                buf, sem = bufs[half]
                fetch(g, buf, sem).wait()

                @pl.when(g + 1 < pieces)
                def _():
                    fetch(g + 1, *bufs[1 - half]).start()

                puts = [pltpu.make_async_copy(buf, out_hbm.at[idx_v.at[g, k]], sem_out) for k in range(slots)]
                for put in puts:
                    put.start()
                for put in puts:
                    put.wait()

    return scatter(rows, idx)


def _experts_kernel(be_ref, nb_ref, first_ref, slot_ref, next_ref,
                    x_ref, wgu_hbm, bgu_ref, wd_hbm, bd_ref, o_ref,
                    wgu_f32, wd_f32, wgu_bf, wd_bf, sems):
    f32 = jnp.float32
    d_ff = wd_bf.shape[0]
    i = pl.program_id(0)
    live = i < nb_ref[0]

    def weight_copies(e, s):
        return (pltpu.make_async_copy(wgu_hbm.at[e], wgu_f32.at[s], sems.at[s, 0]),
                pltpu.make_async_copy(wd_hbm.at[e], wd_f32.at[s], sems.at[s, 1]))

    @pl.when(live & (first_ref[i] == 1))
    def _():
        e = be_ref[i]
        s = slot_ref[i]

        @pl.when(i == 0)
        def _():
            for cp in weight_copies(e, s):
                cp.start()

        for cp in weight_copies(e, s):
            cp.wait()

        @pl.when(next_ref[i] >= 0)
        def _():
            for cp in weight_copies(next_ref[i], 1 - s):
                cp.start()

        wgu_bf[...] = wgu_f32[s].astype(jnp.bfloat16)
        wd_bf[...] = wd_f32[s].astype(jnp.bfloat16)

    @pl.when(live)
    def _():
        x_hi, x_lo = _unpack_bf16_pairs(x_ref[...])
        half = x_hi.shape[1]
        gu = (jnp.dot(x_hi, wgu_bf[:half, :], preferred_element_type=f32)
              + jnp.dot(x_lo, wgu_bf[half:, :], preferred_element_type=f32) + bgu_ref[0])
        gate = jnp.minimum(gu[:, :d_ff], SWIGLU_LIMIT)
        lin = jnp.clip(gu[:, d_ff:], -SWIGLU_LIMIT, SWIGLU_LIMIT)
        act = (lin + 1.0) * gate * _sigmoid(SWIGLU_ALPHA * gate)
        y = jnp.dot(act.astype(jnp.bfloat16), wd_bf[...], preferred_element_type=f32) + bd_ref[0]
        o_ref[...] = _pack_bf16_pairs(y)

    @pl.when(pl.program_id(0) >= nb_ref[0])
    def _():
        o_ref[...] = jnp.zeros_like(o_ref)


def _experts(xs, plan, w_gu, b_gu, w_down, b_down):
    P, W = xs.shape
    E, D, F2 = w_gu.shape
    nb = P // EXPERT_ROWS
    by_expert = lambda i, be, *_: (be[i], 0, 0)
    grid_spec = pltpu.PrefetchScalarGridSpec(
        num_scalar_prefetch=5,
        grid=(nb,),
        in_specs=[
            pl.BlockSpec((EXPERT_ROWS, W), lambda i, *_: (i, 0)),
            pl.BlockSpec(memory_space=pl.ANY),
            pl.BlockSpec((1, 1, F2), by_expert),
            pl.BlockSpec(memory_space=pl.ANY),
            pl.BlockSpec((1, 1, D), by_expert),
        ],
        out_specs=pl.BlockSpec((EXPERT_ROWS, W), lambda i, *_: (i, 0)),
        scratch_shapes=[pltpu.VMEM((2, D, F2), jnp.float32), pltpu.VMEM((2, F2 // 2, D), jnp.float32),
                        pltpu.VMEM((D, F2), jnp.bfloat16), pltpu.VMEM((F2 // 2, D), jnp.bfloat16),
                        pltpu.SemaphoreType.DMA((2, 2))],
    )
    return pl.pallas_call(
        _experts_kernel,
        grid_spec=grid_spec,
        out_shape=jax.ShapeDtypeStruct((P, W), jnp.uint32),
        compiler_params=_cparams(("arbitrary",)),
        name="experts",
    )(*plan, xs, w_gu, b_gu.reshape(E, 1, F2), w_down, b_down.reshape(E, 1, D))


def _combine_kernel(ya_ref, x1_ref, gate_ref, g_ref, o_ref):
    half = x1_ref.shape[1] // 2
    f32 = jnp.float32
    gate = gate_ref[...].T
    x1 = x1_ref[...]
    y_hi = x1[:, :half]
    y_lo = x1[:, half:]
    for k in range(TOP_K):
        hi, lo = _unpack_bf16_pairs(ya_ref[k])
        y_hi = y_hi + gate[:, k:k + 1] * hi.astype(f32)
        y_lo = y_lo + gate[:, k:k + 1] * lo.astype(f32)
    o_ref[...] = _rms(jnp.concatenate([y_hi, y_lo], axis=1), g_ref[...])


def _combine(ya, x1, gates, gain):
    T, D = x1.shape
    R = min(COMBINE_ROWS, T)
    row = lambda w: pl.BlockSpec((R, w), lambda i: (i, 0))
    return pl.pallas_call(
        _combine_kernel,
        grid=(T // R,),
        in_specs=[pl.BlockSpec((TOP_K, R, D // 2), lambda i: (0, i, 0)), row(D),
                  pl.BlockSpec((TOP_K, R), lambda i: (0, i)), pl.BlockSpec((1, D), lambda i: (0, 0))],
        out_specs=row(D),
        out_shape=jax.ShapeDtypeStruct((T, D), jnp.float32),
        compiler_params=_cparams(("parallel",)),
        name="combine",
    )(ya, x1, gates, gain.reshape(1, D))


def _moe_plan(eidx, rank, counts, A):
    counts = counts.reshape(N_EXPERTS)
    padded = (counts + EXPERT_ROWS - 1) // EXPERT_ROWS * EXPERT_ROWS
    pad_ends = jnp.cumsum(padded)
    pad_starts = pad_ends - padded
    n_blocks = -(-A // EXPERT_ROWS) + N_EXPERTS
    ids = jnp.arange(N_EXPERTS, dtype=jnp.int32)
    dest = rank + jnp.sum(jnp.where(eidx[None] == ids[:, None, None], pad_starts[:, None, None], 0), axis=0)
    block_start = jnp.arange(n_blocks, dtype=pad_ends.dtype) * EXPERT_ROWS
    block_expert = jnp.minimum(jnp.sum(pad_ends[None, :] <= block_start[:, None], axis=1),
                               N_EXPERTS - 1).astype(jnp.int32)
    n_used = (pad_ends[-1] // EXPERT_ROWS).astype(jnp.int32).reshape(1)
    has_rows = counts > 0
    ordinal = jnp.cumsum(has_rows.astype(jnp.int32)) - 1
    later = has_rows[None, :] & (ids[None, :] > ids[:, None])
    next_expert = jnp.where(jnp.any(later, axis=1), jnp.argmax(later, axis=1), -1).astype(jnp.int32)
    block_first = ((block_start == pad_starts[block_expert]) & (block_start < pad_ends[-1])).astype(jnp.int32)
    block_slot = (ordinal[block_expert] % 2).astype(jnp.int32)
    block_next = next_expert[block_expert]
    plan = (block_expert, n_used, block_first, block_slot, block_next)
    return dest.astype(jnp.int32), plan, n_blocks


def kernel(x, w_in, w_up_a, w_up_b, w_out, norm_mix, norm_ffn, norm_final, hgrn_norm,
           lb_logits, rel_bias, w_router, b_router, w_gu, b_gu, w_down, b_down):
    B, S, D = x.shape
    T = B * S
    assert w_in.shape[0] == 1, "the final rmsnorm is fused into the single layer's combine stage"
    lb_all = jnp.cumsum(jax.nn.softmax(lb_logits.astype(jnp.float32), axis=0), axis=0)
    x2 = x.reshape(T, D)
    (k, ik, qT, vT, iqT, iwT, bq, bf, bi, bg, ga, gb) = _inproj(x2, norm_mix[0], w_in[0], B, S)
    ya = _dsa(k, ik, qT, vT, iqT, iwT, rel_bias, B, S)
    yb = _hgrn(bq, bf, bi, bg, lb_all[0].reshape(B_HEADS, B_KEY_DIM), hgrn_norm[0], B, S)
    x1, xn, logits = _merge(x2, ya, yb, ga, gb, w_up_a[0], w_up_b[0], w_out[0], norm_ffn[0],
                            w_router[0], b_router[0])
    eidx, gates, rank, counts = _route(logits)
    dest, plan, n_blocks = _moe_plan(eidx, rank, counts, T * TOP_K)
    P = n_blocks * EXPERT_ROWS
    A = T * TOP_K
    xs = _sc_scatter_rows(xn, dest, P)
    y_buf = _experts(xs, plan, w_gu[0], b_gu[0], w_down[0], b_down[0])
    ya = _sc_gather_rows(y_buf, dest.reshape(A)).reshape(TOP_K, T, D // 2)
    out = _combine(ya, x1, gates, norm_final)
    return out.reshape(B, S, D)
```

```python
import functools
import math

import numpy as np
import jax
import jax.numpy as jnp
from jax import lax
from jax.experimental import pallas as pl
from jax.experimental.pallas import tpu as pltpu
from jax.experimental.pallas import tpu_sc as plsc

A_HEADS = 8
A_HEAD_DIM = 64
IDX_HEADS = 8
IDX_DIM = 32
TOPK_MAX = 256
REL_BUCKETS = 32
REL_MAX_DIST = 128
B_HEADS = 4
B_KEY_DIM = 128
B_VAL_DIM = 128
N_EXPERTS = 32
TOP_K = 4
SWIGLU_LIMIT = 7.0
SWIGLU_ALPHA = 1.702
EPS = 1e-6
LOG2_E = math.log2(math.e)

A_WIDTH = A_HEADS * A_HEAD_DIM
B_WIDTH = B_HEADS * B_VAL_DIM
IDX_WIDTH = IDX_HEADS * IDX_DIM

V7X_LANES = 128
V7X_SUBLANES = 8
V7X_VMEM_LIMIT_BYTES = 56 * 1024 * 1024
V7X_SC_CORES = 2
V7X_SC_SUBCORES = 16

PROJ_ROWS = 512
ATT_Q = 256
ATT_KC = 128
HGRN_ROWS = 1024
HGRN_CHUNK = 64
HGRN_SAFE_DECAY = 70.0
ROUTE_ROWS = 512
EXPERT_ROWS = 512
COMBINE_ROWS = 512
SC_GATHER_ROWS = 64
MASK_NEG = -1e30
BISECT_FAST_ITERS = 26


def _cparams(dims):
    return pltpu.CompilerParams(dimension_semantics=dims, vmem_limit_bytes=V7X_VMEM_LIMIT_BYTES)


def _rms(x, gain):
    return x * lax.rsqrt(jnp.mean(x * x, axis=-1, keepdims=True) + EPS) * gain


def _sigmoid(x):
    return 1.0 / (1.0 + jnp.exp(-x))


def _pack_bf16_pairs(x):
    n = x.shape[1] // 2
    as_bits = lambda v: lax.bitcast_convert_type(v.astype(jnp.bfloat16).astype(jnp.float32), jnp.uint32)
    return (as_bits(x[:, :n]) & jnp.uint32(0xFFFF0000)) | (as_bits(x[:, n:]) >> 16)


def _unpack_bf16_pairs(w):
    hi = lax.bitcast_convert_type(w & jnp.uint32(0xFFFF0000), jnp.float32).astype(jnp.bfloat16)
    lo = lax.bitcast_convert_type(w << 16, jnp.float32).astype(jnp.bfloat16)
    return hi, lo


def _fold_rows(x, op):
    return op(x.reshape(x.shape[0] // V7X_SUBLANES, V7X_SUBLANES, x.shape[1]), axis=0)


def _inproj_kernel(x_ref, g_ref, wk_ref, wik_ref, wqT_ref, wvT_ref, wiqT_ref, wiwT_ref, wb_ref, wg_ref,
                   k_ref, ik_ref, qT_ref, vT_ref, iqT_ref, iwT_ref, bq_ref, bf_ref, bi_ref, bg_ref,
                   ga_ref, gb_ref):
    x = x_ref[...]
    hn = _rms(x, g_ref[...]).astype(jnp.bfloat16)

    def mm(w_ref):
        return jnp.dot(hn, w_ref[...], preferred_element_type=jnp.float32)

    def mm_t(w_ref):
        return lax.dot_general(w_ref[...], hn, (((1,), (1,)), ((), ())),
                               preferred_element_type=jnp.float32)

    k_ref[...] = mm(wk_ref).astype(jnp.bfloat16)
    ik_ref[...] = mm(wik_ref).astype(jnp.bfloat16)
    qT_ref[0] = (mm_t(wqT_ref) * (A_HEAD_DIM ** -0.5 * LOG2_E)).astype(jnp.bfloat16)
    vT_ref[0] = mm_t(wvT_ref).astype(jnp.bfloat16)
    iqT_ref[0] = mm_t(wiqT_ref).astype(jnp.bfloat16)
    iwT_ref[0] = mm_t(wiwT_ref) * ((IDX_HEADS * IDX_DIM) ** -0.5)
    hb = mm(wb_ref)
    bq_ref[...] = hb[:, 0 * B_WIDTH:1 * B_WIDTH]
    bf_ref[...] = hb[:, 1 * B_WIDTH:2 * B_WIDTH]
    bi_ref[...] = hb[:, 2 * B_WIDTH:3 * B_WIDTH]
    bg_ref[...] = hb[:, 3 * B_WIDTH:4 * B_WIDTH]
    d = ga_ref.shape[-1]
    hg = mm(wg_ref)
    ga_ref[...] = _sigmoid(hg[:, :d]).astype(jnp.bfloat16)
    gb_ref[...] = _sigmoid(hg[:, d:]).astype(jnp.bfloat16)


def _inproj(x2, gain, w_in, B, S):
    T, D = x2.shape
    R = min(PROJ_ROWS, S)
    nS = S // R
    o = np.cumsum((0, A_WIDTH, A_WIDTH, A_WIDTH, IDX_WIDTH, IDX_HEADS, IDX_DIM,
                   B_WIDTH, B_WIDTH, B_WIDTH, B_WIDTH, D, D))
    bf = jnp.bfloat16
    wqT = w_in[:, o[0]:o[1]].T.astype(bf)
    wk = w_in[:, o[1]:o[2]].astype(bf)
    wvT = w_in[:, o[2]:o[3]].T.astype(bf)
    wiqT = w_in[:, o[3]:o[4]].T.astype(bf)
    wiwT = w_in[:, o[4]:o[5]].T.astype(bf)
    wik = w_in[:, o[5]:o[6]].astype(bf)
    wb = w_in[:, o[6]:o[10]].astype(bf)
    wg = w_in[:, o[10]:o[12]].astype(bf)

    def full(a):
        return pl.BlockSpec(a.shape, lambda b, i: (0,) * a.ndim)

    row = lambda n: pl.BlockSpec((R, n), lambda b, i: (b * nS + i, 0))
    colT = lambda n: pl.BlockSpec((1, n, R), lambda b, i: (b, 0, i))
    f32 = jnp.float32
    outs = [
        (jax.ShapeDtypeStruct((T, A_WIDTH), bf), row(A_WIDTH)),
        (jax.ShapeDtypeStruct((T, IDX_DIM), bf), row(IDX_DIM)),
        (jax.ShapeDtypeStruct((B, A_WIDTH, S), bf), colT(A_WIDTH)),
        (jax.ShapeDtypeStruct((B, A_WIDTH, S), bf), colT(A_WIDTH)),
        (jax.ShapeDtypeStruct((B, IDX_WIDTH, S), bf), colT(IDX_WIDTH)),
        (jax.ShapeDtypeStruct((B, IDX_HEADS, S), f32), colT(IDX_HEADS)),
        (jax.ShapeDtypeStruct((T, B_WIDTH), f32), row(B_WIDTH)),
        (jax.ShapeDtypeStruct((T, B_WIDTH), f32), row(B_WIDTH)),
        (jax.ShapeDtypeStruct((T, B_WIDTH), f32), row(B_WIDTH)),
        (jax.ShapeDtypeStruct((T, B_WIDTH), f32), row(B_WIDTH)),
        (jax.ShapeDtypeStruct((T, D), bf), row(D)),
        (jax.ShapeDtypeStruct((T, D), bf), row(D)),
    ]
    ins = [x2, gain.reshape(1, D), wk, wik, wqT, wvT, wiqT, wiwT, wb, wg]
    in_specs = [row(D)] + [full(a) for a in ins[1:]]
    return pl.pallas_call(
        _inproj_kernel,
        grid=(B, nS),
        in_specs=in_specs,
        out_specs=[s for _, s in outs],
        out_shape=[s for s, _ in outs],
        compiler_params=_cparams(("parallel", "parallel")),
        name="inproj",
    )(*ins)


def _t5_bucket_table(n):
    d = np.arange(n)
    max_exact = REL_BUCKETS // 2
    nf = np.maximum(d, 1).astype(np.float64)
    large = max_exact + (np.log(nf / max_exact) / math.log(REL_MAX_DIST / max_exact)
                         * (REL_BUCKETS - max_exact)).astype(np.int32)
    large = np.minimum(large, REL_BUCKETS - 1)
    return np.where(d < max_exact, d, large)


def _dsa_kernel(qT_ref, k_ref, vT_ref, iqT_ref, iwT_ref, ik_ref, enear_ref,
                o_ref, sc_ref, qh_scr, m_scr, acc_scr, *, topk):
    TQ = qT_ref.shape[2]
    KC = TQ
    i = pl.program_id(1)
    nch = i + 1
    q0 = i * TQ
    f32 = jnp.float32
    bf16 = jnp.bfloat16
    key_id = lax.broadcasted_iota(jnp.int32, (KC, TQ), 0)
    qry_id = lax.broadcasted_iota(jnp.int32, (KC, TQ), 1)

    def col_reduce(x, op):
        return op(_fold_rows(x, op), axis=0, keepdims=True)

    iw = iwT_ref[0]

    def score_chunk(c, carry):
        rmin, rmax = carry
        k0 = pl.multiple_of(c * KC, KC)
        ik = ik_ref[pl.ds(k0, KC), :]
        acc = jnp.zeros((KC, TQ), f32)
        for h in range(IDX_HEADS):
            sh = jnp.dot(ik, iqT_ref[0, h * IDX_DIM:(h + 1) * IDX_DIM, :], preferred_element_type=f32)
            acc = acc + jnp.maximum(sh, 0.0) * iw[h:h + 1, :]
        valid = (k0 + key_id) <= (q0 + qry_id)
        sc_ref[pl.ds(k0, KC), :] = jnp.where(valid, acc, MASK_NEG)
        rmin = jnp.minimum(rmin, _fold_rows(jnp.where(valid, acc, -MASK_NEG), jnp.min))
        rmax = jnp.maximum(rmax, _fold_rows(jnp.where(valid, acc, MASK_NEG), jnp.max))
        return rmin, rmax

    rmin8, rmax8 = lax.fori_loop(
        0, nch // 2, lambda j, cr: score_chunk(2 * j + 1, score_chunk(2 * j, cr)),
        (jnp.full((V7X_SUBLANES, TQ), -MASK_NEG, f32), jnp.full((V7X_SUBLANES, TQ), MASK_NEG, f32)))
    rmin8, rmax8 = lax.cond(nch % 2 == 1, lambda cr: score_chunk(nch - 1, cr), lambda cr: cr, (rmin8, rmax8))
    rmin = jnp.min(rmin8, axis=0, keepdims=True)
    rmax = jnp.max(rmax8, axis=0, keepdims=True)

    def count_where(pred_fn):
        def one(c, acc):
            k0 = pl.multiple_of(c * KC, KC)
            blk = sc_ref[pl.ds(k0, KC), :]
            return acc + _fold_rows(jnp.where(pred_fn(blk), 1.0, 0.0), jnp.sum)
        acc = lax.fori_loop(0, nch // 2, lambda j, acc: one(2 * j + 1, one(2 * j, acc)),
                            jnp.zeros((V7X_SUBLANES, TQ), f32))
        acc = lax.cond(nch % 2 == 1, lambda a: one(nch - 1, a), lambda a: a, acc)
        return jnp.sum(acc, axis=0, keepdims=True)

    def band_min_max(lo, hi):
        def body(c, carry):
            bmin, bmax = carry
            k0 = pl.multiple_of(c * KC, KC)
            blk = sc_ref[pl.ds(k0, KC), :]
            bmin = jnp.minimum(bmin, _fold_rows(jnp.where(blk >= lo, blk, -MASK_NEG), jnp.min))
            bmax = jnp.maximum(bmax, _fold_rows(jnp.where(blk < hi, blk, MASK_NEG), jnp.max))
            return bmin, bmax
        bmin8, bmax8 = lax.fori_loop(
            0, nch, body,
            (jnp.full((V7X_SUBLANES, TQ), -MASK_NEG, f32), jnp.full((V7X_SUBLANES, TQ), MASK_NEG, f32)))
        return jnp.min(bmin8, axis=0, keepdims=True), jnp.max(bmax8, axis=0, keepdims=True)

    kf = float(topk)
    n_valid = (q0 + 1 + lax.broadcasted_iota(jnp.int32, (1, TQ), 1)).astype(f32)
    lo0 = rmin
    cnt0 = n_valid
    hi0 = rmax + jnp.maximum(jnp.abs(rmax) * 2.0 ** -20, 1e-30)
    done0 = jnp.where(cnt0 <= kf, 1.0, 0.0)

    def probe(st, lo_s, mid, tie):
        it, lo, hi, cnt, done = st
        active = done < 0.5
        lo_s = jnp.where(active, lo_s, lo)
        c = count_where(lambda blk: blk >= mid)
        feas = c >= kf
        move = active & jnp.logical_not(tie)
        lo_n = jnp.where(move & feas, mid, lo_s)
        cnt_n = jnp.where(move & feas, c, cnt)
        hi_n = jnp.where(move & jnp.logical_not(feas), mid, hi)
        done_n = jnp.where((active & tie) | (cnt_n <= kf), 1.0, done)
        return it + 1, lo_n, hi_n, cnt_n, done_n

    def halve(st):
        _, lo, hi, _, _ = st
        half = lo + 0.5 * (hi - lo)
        stuck = (half <= lo) | (half >= hi)
        return probe(st, lo, half, stuck)

    def snap(st):
        _, lo, hi, _, _ = st
        bmin, bmax = band_min_max(lo, hi)
        mid = bmin + 0.5 * (bmax - bmin)
        return probe(st, bmin, jnp.where(mid <= bmin, bmax, mid), bmax <= bmin)

    st = lax.while_loop(lambda st: (jnp.min(st[-1]) < 0.5) & (st[0] < BISECT_FAST_ITERS),
                        lambda st: halve(halve(st)), (jnp.int32(0), lo0, hi0, cnt0, done0))
    _, thr, _, cnt_thr, _ = lax.while_loop(lambda st: jnp.min(st[-1]) < 0.5, snap, st)

    tie_overflow = jnp.max(cnt_thr) > kf

    @pl.when(jnp.logical_not(tie_overflow))
    def _():
        def mask_chunk(c, _):
            k0 = pl.multiple_of(c * KC, KC)
            sc_ref[pl.ds(k0, KC), :] = jnp.where(sc_ref[pl.ds(k0, KC), :] >= thr, 0.0, MASK_NEG)
            return 0
        lax.fori_loop(0, nch, mask_chunk, 0)

    @pl.when(tie_overflow)
    def _():
        need = kf - count_where(lambda blk: blk > thr)
        tril = jnp.where(lax.broadcasted_iota(jnp.int32, (KC, KC), 1)
                         <= lax.broadcasted_iota(jnp.int32, (KC, KC), 0), 1.0, 0.0).astype(bf16)

        def mask_chunk(c, run):
            k0 = pl.multiple_of(c * KC, KC)
            blk = sc_ref[pl.ds(k0, KC), :]
            eq = jnp.where(blk == thr, 1.0, 0.0)
            pref = jnp.dot(tril, eq.astype(bf16), preferred_element_type=f32)
            sel = (blk > thr) | ((eq > 0.5) & (run + pref <= need))
            sc_ref[pl.ds(k0, KC), :] = jnp.where(sel, 0.0, MASK_NEG)
            return run + pref[KC - 1:KC, :]

        lax.fori_loop(0, nch, mask_chunk, jnp.zeros((1, TQ), f32))

    AK = min(ATT_KC, TQ)
    per = TQ // AK
    head0_q = (lax.broadcasted_iota(jnp.int32, (V7X_LANES, TQ), 0) // A_HEAD_DIM) == 0
    n_pairs = A_HEADS // 2

    m_scr[...] = jnp.full(m_scr.shape, MASK_NEG, f32)
    acc_scr[...] = jnp.zeros(acc_scr.shape, f32)
    v_row = lax.broadcasted_iota(jnp.int32, (V7X_LANES, AK), 0)
    denom_row = [A_HEAD_DIM * (1 - sub) for sub in range(2)]
    for p in range(n_pairs):
        q_pair = qT_ref[0, p * V7X_LANES:(p + 1) * V7X_LANES, :]
        zq = jnp.zeros_like(q_pair)
        qh_scr[2 * p] = jnp.where(head0_q, q_pair, zq)
        qh_scr[2 * p + 1] = jnp.where(head0_q, zq, q_pair)

    def step(c, bias_rows):
        k0 = pl.multiple_of(c * AK, AK)
        msk = sc_ref[pl.ds(k0, AK), :]
        for p in range(n_pairs):
            kp = k_ref[pl.ds(k0, AK), p * V7X_LANES:(p + 1) * V7X_LANES]
            vp = vT_ref[0, p * V7X_LANES:(p + 1) * V7X_LANES, pl.ds(k0, AK)]
            for sub in range(2):
                h = 2 * p + sub
                s = jnp.dot(kp, qh_scr[h], preferred_element_type=f32) + msk
                if bias_rows is not None:
                    s = s + enear_ref[h, bias_rows, :]
                m = m_scr[h:h + 1, :]
                m_new = jnp.maximum(m, col_reduce(s, jnp.max))
                alpha = jnp.exp2(m - m_new)
                pr = jnp.exp2(s - m_new)
                m_scr[h:h + 1, :] = m_new
                v_aug = jnp.where(v_row == denom_row[sub], jnp.ones_like(vp), vp)
                acc_scr[h] = alpha * acc_scr[h] + jnp.dot(v_aug, pr.astype(bf16), preferred_element_type=f32)

    def far(blk, _):
        for jj in range(per):
            step(blk * per + jj, None)
        return 0

    def near(block, first_chunk):
        for jj in range(per):
            step(first_chunk + jj, slice(block * TQ + jj * AK, block * TQ + (jj + 1) * AK))

    n_far = jnp.maximum(i - 1, 0)
    lax.fori_loop(0, n_far // 2, lambda j, z: far(2 * j + 1, far(2 * j, z)), 0)

    @pl.when(n_far % 2 == 1)
    def _():
        far(n_far - 1, 0)

    @pl.when(i >= 1)
    def _():
        near(0, (i - 1) * per)

    near(1, i * per)
    for p in range(n_pairs):
        outs = [acc_scr[2 * p + sub] / acc_scr[2 * p + sub, denom_row[sub]:denom_row[sub] + 1, :]
                for sub in range(2)]
        o_pair = jnp.where(head0_q, outs[0], outs[1])
        o_ref[:, p * V7X_LANES:(p + 1) * V7X_LANES] = o_pair.T.astype(o_ref.dtype)


def _dsa(k, ik, qT, vT, iqT, iwT, rel_bias, B, S):
    T = k.shape[0]
    TQ = min(ATT_Q, S)
    nQ = S // TQ
    topk = min(TOPK_MAX, S // 4)
    buckets = _t5_bucket_table(2 * TQ + 1)
    assert np.all(_t5_bucket_table(S + 1)[TQ + 1:] == REL_BUCKETS - 1)
    j = np.arange(2 * TQ)[:, None]
    r = np.arange(TQ)[None, :]
    dist = np.maximum(r + TQ - j, 0)
    onehot = (jnp.asarray(buckets[dist], jnp.int32)[None]
              == jnp.arange(REL_BUCKETS, dtype=jnp.int32)[:, None, None]).astype(jnp.float32)
    rel = (rel_bias.astype(jnp.float32) - rel_bias[REL_BUCKETS - 1].astype(jnp.float32)[None, :]) * LOG2_E
    enear = jnp.einsum('nh,njr->hjr', rel, onehot, precision=lax.Precision.HIGHEST)

    return pl.pallas_call(
        functools.partial(_dsa_kernel, topk=topk),
        grid=(B, nQ),
        in_specs=[
            pl.BlockSpec((1, A_WIDTH, TQ), lambda b, i: (b, 0, i)),
            pl.BlockSpec((S, A_WIDTH), lambda b, i: (b, 0)),
            pl.BlockSpec((1, A_WIDTH, S), lambda b, i: (b, 0, 0)),
            pl.BlockSpec((1, IDX_WIDTH, TQ), lambda b, i: (b, 0, i)),
            pl.BlockSpec((1, IDX_HEADS, TQ), lambda b, i: (b, 0, i)),
            pl.BlockSpec((S, IDX_DIM), lambda b, i: (b, 0)),
            pl.BlockSpec((A_HEADS, 2 * TQ, TQ), lambda b, i: (0, 0, 0)),
        ],
        out_specs=pl.BlockSpec((TQ, A_WIDTH), lambda b, i: (b * nQ + i, 0)),
        scratch_shapes=[pltpu.VMEM((S, TQ), jnp.float32),
                        pltpu.VMEM((A_HEADS, V7X_LANES, TQ), jnp.bfloat16),
                        pltpu.VMEM((A_HEADS, TQ), jnp.float32),
                        pltpu.VMEM((A_HEADS, V7X_LANES, TQ), jnp.float32)],
        out_shape=jax.ShapeDtypeStruct((T, A_WIDTH), jnp.bfloat16),
        compiler_params=_cparams(("parallel", "arbitrary")),
        name="dsa",
    )(qT, k, vT, iqT, iwT, ik, enear)


def _hgrn_kernel(bq_ref, bf_ref, bi_ref, bg_ref, lb_ref, gain_ref, o_ref,
                 st_ref, b_scr, q_scr, k_scr, v_scr, oi_scr, qd_s, kl_s, vv_s, dec_s, oi_s, upd_s, st_s,
                 kdT_s, vvT_s):
    R = bq_ref.shape[0]
    C = HGRN_CHUNK
    nC = R // C
    f32 = jnp.float32
    bf16 = jnp.bfloat16
    h = pl.program_id(1)

    @pl.when(pl.program_id(2) == 0)
    def _():
        st_ref[...] = jnp.zeros_like(st_ref)

    lb = lb_ref[pl.ds(h, 1), :]
    gain = gain_ref[pl.ds(h, 1), :]
    tril_incl = jnp.where(lax.broadcasted_iota(jnp.int32, (C, C), 1)
                          <= lax.broadcasted_iota(jnp.int32, (C, C), 0), 1.0, 0.0)
    srow = lax.broadcasted_iota(jnp.int32, (C, B_KEY_DIM), 0)

    def gates(r0):
        f = lb + (1.0 - lb) * _sigmoid(bf_ref[pl.ds(r0, C), :])
        qr = bq_ref[pl.ds(r0, C), :]
        return jnp.log(f), 1.0 - f, qr * _sigmoid(qr) * (B_KEY_DIM ** -0.5), bi_ref[pl.ds(r0, C), :]

    def cumdecay(g):
        tri = tril_incl.astype(bf16)
        g_hi = g.astype(bf16)
        rest = g - g_hi.astype(f32)
        g_mid = rest.astype(bf16)
        g_lo = (rest - g_mid.astype(f32)).astype(bf16)
        return (jnp.dot(tri, g_hi, preferred_element_type=f32) + jnp.dot(tri, g_mid, preferred_element_type=f32)
                + jnp.dot(tri, g_lo, preferred_element_type=f32))

    def advance(r0, st, qd, o_intra, upd, decay_row):
        o_inter = lax.dot_general(qd, st.astype(bf16), (((1,), (1,)), ((), ())), preferred_element_type=f32)
        og = bg_ref[pl.ds(r0, C), :]
        y = _rms(o_inter + o_intra, gain) * (og * _sigmoid(og))
        o_ref[pl.ds(r0, C), :] = y.astype(o_ref.dtype)
        return st * decay_row + upd

    f_all = lb + (1.0 - lb) * _sigmoid(bf_ref[...])
    g_all = jnp.log(f_all)
    decay = jnp.sum(g_all.reshape(nC, C, B_KEY_DIM), axis=1)
    safe = jnp.min(decay) >= -HGRN_SAFE_DECAY

    @pl.when(safe)
    def _():
        qr = bq_ref[...]
        qq = qr * _sigmoid(qr) * (B_KEY_DIM ** -0.5)
        kk = 1.0 - f_all
        b = jnp.concatenate([cumdecay(g_all[c * C:(c + 1) * C]) for c in range(nC)], axis=0)
        b_end = jnp.concatenate([jnp.broadcast_to(b[(c + 1) * C - 1:(c + 1) * C], (C, B_KEY_DIM))
                                 for c in range(nC)], axis=0)
        qd_s[...] = (qq * jnp.exp(b)).astype(bf16)
        kd = kk * jnp.exp(-b)
        kl_s[...] = (kk * jnp.exp(b_end - b)).astype(bf16)
        vv = bi_ref[...]
        vv_s[...] = vv.astype(bf16)
        dec_s[...] = jnp.exp(b_end)
        for c in range(nC):
            rows = slice(c * C, (c + 1) * C)
            kdT_s[c] = kd[rows].T.astype(bf16)
            vvT_s[c] = vv[rows].T.astype(bf16)
        for c in range(nC):
            rows = slice(c * C, (c + 1) * C)
            att = jnp.dot(qd_s[rows], kdT_s[c], preferred_element_type=f32) * tril_incl
            oi_s[rows] = jnp.dot(att.astype(bf16), vv_s[rows], preferred_element_type=f32)
            upd_s[c] = jnp.dot(vvT_s[c], kl_s[rows], preferred_element_type=f32)
        st = st_ref[...]
        for c in range(nC):
            st_s[c] = st.T.astype(bf16)
            st = st * dec_s[c * C:c * C + 1] + upd_s[c]
        st_ref[...] = st
        for c in range(nC):
            rows = slice(c * C, (c + 1) * C)
            oi_s[rows] = oi_s[rows] + jnp.dot(qd_s[rows], st_s[c], preferred_element_type=f32)
        og = bg_ref[...]
        o_ref[...] = (_rms(oi_s[...], gain) * (og * _sigmoid(og))).astype(o_ref.dtype)

    @pl.when(jnp.logical_not(safe))
    def _():
        def body(c, st):
            r0 = pl.multiple_of(c * C, C)
            g, kk, qq, vv = gates(r0)
            b = cumdecay(g)
            b_last = b[C - 1:C, :]
            b_scr[...] = b
            q_scr[...] = qq
            k_scr[...] = kk
            v_scr[...] = vv

            def row(t, _):
                bt = b_scr[pl.ds(t, 1), :]
                qt = q_scr[pl.ds(t, 1), :]
                ex = jnp.where(srow <= t, bt - b_scr[...], -jnp.inf)
                a = jnp.sum(qt * k_scr[...] * jnp.exp(ex), axis=1, keepdims=True)
                oi_scr[pl.ds(t, 1), :] = jnp.sum(a * v_scr[...], axis=0, keepdims=True)
                return 0
            lax.fori_loop(0, C, row, 0)
            kd_last = (kk * jnp.exp(b_last - b)).astype(bf16)
            upd = lax.dot_general(vv.astype(bf16), kd_last, (((0,), (0,)), ((), ())),
                                  preferred_element_type=f32)
            return advance(r0, st, (qq * jnp.exp(b)).astype(bf16), oi_scr[...], upd, jnp.exp(b_last))
        st_ref[...] = lax.fori_loop(0, nC, body, st_ref[...])


def _hgrn(bq, bf, bi, bg, lb, gain, B, S):
    T = bq.shape[0]
    R = min(HGRN_ROWS, S)
    nR = S // R
    C = HGRN_CHUNK
    blk = pl.BlockSpec((R, B_KEY_DIM), lambda b, h, c: (b * nR + c, h))
    small = pl.BlockSpec((B_HEADS, B_KEY_DIM), lambda b, h, c: (0, 0))
    f32 = jnp.float32
    return pl.pallas_call(
        _hgrn_kernel,
        grid=(B, B_HEADS, nR),
        in_specs=[blk, blk, blk, blk, small, small],
        out_specs=blk,
        out_shape=jax.ShapeDtypeStruct((T, B_WIDTH), jnp.bfloat16),
        scratch_shapes=[pltpu.VMEM((B_VAL_DIM, B_KEY_DIM), f32)] +
                       [pltpu.VMEM((C, B_KEY_DIM), f32) for _ in range(5)] +
                       [pltpu.VMEM((R, B_KEY_DIM), jnp.bfloat16) for _ in range(3)] +
                       [pltpu.VMEM((R, B_KEY_DIM), f32) for _ in range(2)] +
                       [pltpu.VMEM((R // C, B_VAL_DIM, B_KEY_DIM), f32),
                        pltpu.VMEM((R // C, B_KEY_DIM, B_VAL_DIM), jnp.bfloat16),
                        pltpu.VMEM((R // C, B_KEY_DIM, C), jnp.bfloat16),
                        pltpu.VMEM((R // C, B_VAL_DIM, C), jnp.bfloat16)],
        compiler_params=_cparams(("parallel", "parallel", "arbitrary")),
        name="hgrn",
    )(bq, bf, bi, bg, lb, gain)


def _merge_kernel(x_ref, ya_ref, yb_ref, ga_ref, gb_ref, wa_ref, wb_ref, wo_ref, g_ref, wrh_ref, wrl_ref, br_ref,
                  x1_ref, xn_ref, lg_ref):
    f32 = jnp.float32
    ma = jnp.dot(ya_ref[...], wa_ref[...], preferred_element_type=f32)
    mb = jnp.dot(yb_ref[...], wb_ref[...], preferred_element_type=f32)
    merged = ga_ref[...].astype(f32) * ma + gb_ref[...].astype(f32) * mb
    x1 = x_ref[...] + jnp.dot(merged.astype(jnp.bfloat16), wo_ref[...], preferred_element_type=f32)
    x1_ref[...] = x1
    hn = _rms(x1, g_ref[...])
    xn_ref[...] = _pack_bf16_pairs(hn)
    hn_hi = hn.astype(jnp.bfloat16)
    hn_lo = (hn - hn_hi.astype(f32)).astype(jnp.bfloat16)
    lg_ref[...] = (jnp.dot(hn_hi, wrh_ref[...], preferred_element_type=f32)
                   + jnp.dot(hn_lo, wrh_ref[...], preferred_element_type=f32)
                   + jnp.dot(hn_hi, wrl_ref[...], preferred_element_type=f32) + br_ref[...])


def _merge(x2, ya, yb, ga, gb, w_up_a, w_up_b, w_out, gain, w_router, b_router):
    T, D = x2.shape
    R = min(PROJ_ROWS, T)
    bf = jnp.bfloat16
    wr_hi = w_router.astype(bf)
    wr_lo = (w_router - wr_hi.astype(jnp.float32)).astype(bf)
    ins = [x2, ya, yb, ga, gb, w_up_a.astype(bf), w_up_b.astype(bf), w_out.astype(bf),
           gain.reshape(1, D), wr_hi, wr_lo, b_router.reshape(1, N_EXPERTS)]
    row = lambda n: pl.BlockSpec((R, n), lambda i: (i, 0))
    full = lambda a: pl.BlockSpec(a.shape, lambda i: (0,) * a.ndim)
    in_specs = [row(D), row(A_WIDTH), row(B_WIDTH), row(D), row(D)] + [full(a) for a in ins[5:]]
    return pl.pallas_call(
        _merge_kernel,
        grid=(T // R,),
        in_specs=in_specs,
        out_specs=[row(D), row(D // 2), row(N_EXPERTS)],
        out_shape=[jax.ShapeDtypeStruct((T, D), jnp.float32), jax.ShapeDtypeStruct((T, D // 2), jnp.uint32),
                   jax.ShapeDtypeStruct((T, N_EXPERTS), jnp.float32)],
        compiler_params=_cparams(("parallel",)),
        name="merge",
    )(*ins)


def _route_kernel(lg_ref, eidx_ref, gate_ref, rank_ref, cnt_ref, run_ref):
    R = lg_ref.shape[0]
    f32 = jnp.float32

    @pl.when(pl.program_id(0) == 0)
    def _():
        run_ref[...] = jnp.zeros_like(run_ref)

    lg = lg_ref[...].T
    expert = lax.broadcasted_iota(jnp.int32, (N_EXPERTS, R), 0)
    work = lg
    onehots, vals, idxs = [], [], []
    for _ in range(TOP_K):
        m = jnp.max(work, axis=0, keepdims=True)
        idx = jnp.min(jnp.where(work == m, expert, N_EXPERTS), axis=0, keepdims=True)
        oh = expert == idx
        onehots.append(oh)
        vals.append(m)
        idxs.append(idx)
        work = jnp.where(oh, -jnp.inf, work)
    ex = [jnp.exp(v - vals[0]) for v in vals]
    den = ex[0] + ex[1] + ex[2] + ex[3]
    chosen = jnp.where(onehots[0] | onehots[1] | onehots[2] | onehots[3], 1.0, 0.0)
    earlier = jnp.where(lax.broadcasted_iota(jnp.int32, (R, R), 0)
                        < lax.broadcasted_iota(jnp.int32, (R, R), 1), 1.0, 0.0).astype(jnp.bfloat16)
    before = jnp.dot(chosen.astype(jnp.bfloat16), earlier, preferred_element_type=f32) + run_ref[...]
    slot = lax.broadcasted_iota(jnp.int32, (TOP_K, R), 0)
    eidx = jnp.zeros((TOP_K, R), jnp.int32)
    gate = jnp.zeros((TOP_K, R), f32)
    rank = jnp.zeros((TOP_K, R), f32)
    for k in range(TOP_K):
        eidx = jnp.where(slot == k, idxs[k], eidx)
        gate = jnp.where(slot == k, ex[k] / den, gate)
        rk = jnp.sum(jnp.where(onehots[k], before, 0.0), axis=0, keepdims=True)
        rank = jnp.where(slot == k, rk, rank)
    eidx_ref[...] = eidx
    gate_ref[...] = gate
    rank_ref[...] = rank.astype(jnp.int32)
    run_ref[...] = run_ref[...] + jnp.sum(chosen, axis=1, keepdims=True)
    cnt_ref[...] = run_ref[...].astype(jnp.int32)


def _route(logits):
    T = logits.shape[0]
    R = min(ROUTE_ROWS, T)
    col = pl.BlockSpec((TOP_K, R), lambda i: (0, i))
    return pl.pallas_call(
        _route_kernel,
        grid=(T // R,),
        in_specs=[pl.BlockSpec((R, N_EXPERTS), lambda i: (i, 0))],
        out_specs=[col, col, col, pl.BlockSpec((N_EXPERTS, 1), lambda i: (0, 0))],
        out_shape=[jax.ShapeDtypeStruct((TOP_K, T), jnp.int32), jax.ShapeDtypeStruct((TOP_K, T), jnp.float32),
                   jax.ShapeDtypeStruct((TOP_K, T), jnp.int32), jax.ShapeDtypeStruct((N_EXPERTS, 1), jnp.int32)],
        scratch_shapes=[pltpu.VMEM((N_EXPERTS, 1), jnp.float32)],
        compiler_params=_cparams(("arbitrary",)),
        name="route",
    )(logits)


def _sc_gather_rows(table, idx):
    N, W = table.shape
    M = idx.shape[0]
    workers = V7X_SC_CORES * V7X_SC_SUBCORES
    per_worker = M // workers
    pieces = per_worker // SC_GATHER_ROWS
    assert per_worker * workers == M and pieces * SC_GATHER_ROWS == per_worker and pieces % 2 == 0
    mesh = plsc.VectorSubcoreMesh(core_axis_name="core", subcore_axis_name="subcore",
                                  num_cores=V7X_SC_CORES, num_subcores=V7X_SC_SUBCORES)

    @functools.partial(
        pl.kernel, mesh=mesh,
        out_type=jax.ShapeDtypeStruct((M, W), table.dtype),
        scratch_types=[pltpu.VMEM((per_worker,), jnp.int32),
                       pltpu.VMEM((SC_GATHER_ROWS, W), table.dtype),
                       pltpu.VMEM((SC_GATHER_ROWS, W), table.dtype),
                       pltpu.SemaphoreType.DMA, pltpu.SemaphoreType.DMA],
    )
    def gather(table_hbm, idx_hbm, out_hbm, idx_v, rows_a, rows_b, sem_a, sem_b):
        worker = lax.axis_index("subcore") * V7X_SC_CORES + lax.axis_index("core")
        base = pl.multiple_of(worker * per_worker, SC_GATHER_ROWS)
        pltpu.sync_copy(idx_hbm.at[pl.ds(base, per_worker)], idx_v)
        bufs = ((rows_a, sem_a), (rows_b, sem_b))

        def fetch(g, buf, sem):
            off = pl.multiple_of(g * SC_GATHER_ROWS, SC_GATHER_ROWS)
            return pltpu.make_async_copy(table_hbm.at[idx_v.at[pl.ds(off, SC_GATHER_ROWS)]], buf, sem)

        fetch(0, *bufs[0]).start()

        @pl.loop(0, pieces, step=2)
        def _(g0):
            for half in range(2):
                g = g0 + half
                buf, sem = bufs[half]
                fetch(g, buf, sem).wait()

                @pl.when(g + 1 < pieces)
                def _():
                    fetch(g + 1, *bufs[1 - half]).start()

                off = pl.multiple_of(g * SC_GATHER_ROWS, SC_GATHER_ROWS)
                pltpu.sync_copy(buf, out_hbm.at[pl.ds(base + off, SC_GATHER_ROWS)])

    return gather(table, idx)


def _sc_scatter_rows(rows, dest, n_out):
    T, W = rows.shape
    slots = dest.shape[0]
    workers = V7X_SC_CORES * V7X_SC_SUBCORES
    per_worker = T // workers
    pieces = per_worker // SC_GATHER_ROWS
    assert per_worker * workers == T and pieces * SC_GATHER_ROWS == per_worker and pieces % 2 == 0
    idx = dest.reshape(slots, workers, pieces, SC_GATHER_ROWS).transpose(1, 2, 0, 3)
    mesh = plsc.VectorSubcoreMesh(core_axis_name="core", subcore_axis_name="subcore",
                                  num_cores=V7X_SC_CORES, num_subcores=V7X_SC_SUBCORES)

    @functools.partial(
        pl.kernel, mesh=mesh,
        out_type=jax.ShapeDtypeStruct((n_out, W), rows.dtype),
        scratch_types=[pltpu.VMEM((pieces, slots, SC_GATHER_ROWS), jnp.int32),
                       pltpu.VMEM((SC_GATHER_ROWS, W), rows.dtype),
                       pltpu.VMEM((SC_GATHER_ROWS, W), rows.dtype),
                       pltpu.SemaphoreType.DMA, pltpu.SemaphoreType.DMA, pltpu.SemaphoreType.DMA],
    )
    def scatter(rows_hbm, idx_hbm, out_hbm, idx_v, rows_a, rows_b, sem_a, sem_b, sem_out):
        worker = lax.axis_index("subcore") * V7X_SC_CORES + lax.axis_index("core")
        base = pl.multiple_of(worker * per_worker, SC_GATHER_ROWS)
        pltpu.sync_copy(idx_hbm.at[worker], idx_v)
        bufs = ((rows_a, sem_a), (rows_b, sem_b))

        def fetch(g, buf, sem):
            off = pl.multiple_of(g * SC_GATHER_ROWS, SC_GATHER_ROWS)
            return pltpu.make_async_copy(rows_hbm.at[pl.ds(base + off, SC_GATHER_ROWS)], buf, sem)

        fetch(0, *bufs[0]).start()

        @pl.loop(0, pieces, step=2)
        def _(g0):
            for half in range(2):
                g = g0 + half
                buf, sem = bufs[half]
                fetch(g, buf, sem).wait()

                @pl.when(g + 1 < pieces)
                def _():
                    fetch(g + 1, *bufs[1 - half]).start()

                puts = [pltpu.make_async_copy(buf, out_hbm.at[idx_v.at[g, k]], sem_out) for k in range(slots)]
                for put in puts:
                    put.start()
                for put in puts:
                    put.wait()

    return scatter(rows, idx)


def _experts_kernel(be_ref, nb_ref, first_ref, slot_ref, next_ref,
                    x_ref, wgu_hbm, bgu_ref, wd_hbm, bd_ref, o_ref,
                    wgu_f32, wd_f32, wgu_bf, wd_bf, sems):
    f32 = jnp.float32
    d_ff = wd_bf.shape[0]
    i = pl.program_id(0)
    live = i < nb_ref[0]

    def weight_copies(e, s):
        return (pltpu.make_async_copy(wgu_hbm.at[e], wgu_f32.at[s], sems.at[s, 0]),
                pltpu.make_async_copy(wd_hbm.at[e], wd_f32.at[s], sems.at[s, 1]))

    @pl.when(live & (first_ref[i] == 1))
    def _():
        e = be_ref[i]
        s = slot_ref[i]

        @pl.when(i == 0)
        def _():
            for cp in weight_copies(e, s):
                cp.start()

        for cp in weight_copies(e, s):
            cp.wait()

        @pl.when(next_ref[i] >= 0)
        def _():
            for cp in weight_copies(next_ref[i], 1 - s):
                cp.start()

        wgu_bf[...] = wgu_f32[s].astype(jnp.bfloat16)
        wd_bf[...] = wd_f32[s].astype(jnp.bfloat16)

    @pl.when(live)
    def _():
        x_hi, x_lo = _unpack_bf16_pairs(x_ref[...])
        half = x_hi.shape[1]
        gu = (jnp.dot(x_hi, wgu_bf[:half, :], preferred_element_type=f32)
              + jnp.dot(x_lo, wgu_bf[half:, :], preferred_element_type=f32) + bgu_ref[0])
        gate = jnp.minimum(gu[:, :d_ff], SWIGLU_LIMIT)
        lin = jnp.clip(gu[:, d_ff:], -SWIGLU_LIMIT, SWIGLU_LIMIT)
        act = (lin + 1.0) * gate * _sigmoid(SWIGLU_ALPHA * gate)
        y = jnp.dot(act.astype(jnp.bfloat16), wd_bf[...], preferred_element_type=f32) + bd_ref[0]
        o_ref[...] = _pack_bf16_pairs(y)

    @pl.when(pl.program_id(0) >= nb_ref[0])
    def _():
        o_ref[...] = jnp.zeros_like(o_ref)


def _experts(xs, plan, w_gu, b_gu, w_down, b_down):
    P, W = xs.shape
    E, D, F2 = w_gu.shape
    nb = P // EXPERT_ROWS
    by_expert = lambda i, be, *_: (be[i], 0, 0)
    grid_spec = pltpu.PrefetchScalarGridSpec(
        num_scalar_prefetch=5,
        grid=(nb,),
        in_specs=[
            pl.BlockSpec((EXPERT_ROWS, W), lambda i, *_: (i, 0)),
            pl.BlockSpec(memory_space=pl.ANY),
            pl.BlockSpec((1, 1, F2), by_expert),
            pl.BlockSpec(memory_space=pl.ANY),
            pl.BlockSpec((1, 1, D), by_expert),
        ],
        out_specs=pl.BlockSpec((EXPERT_ROWS, W), lambda i, *_: (i, 0)),
        scratch_shapes=[pltpu.VMEM((2, D, F2), jnp.float32), pltpu.VMEM((2, F2 // 2, D), jnp.float32),
                        pltpu.VMEM((D, F2), jnp.bfloat16), pltpu.VMEM((F2 // 2, D), jnp.bfloat16),
                        pltpu.SemaphoreType.DMA((2, 2))],
    )
    return pl.pallas_call(
        _experts_kernel,
        grid_spec=grid_spec,
        out_shape=jax.ShapeDtypeStruct((P, W), jnp.uint32),
        compiler_params=_cparams(("arbitrary",)),
        name="experts",
    )(*plan, xs, w_gu, b_gu.reshape(E, 1, F2), w_down, b_down.reshape(E, 1, D))


def _combine_kernel(ya_ref, x1_ref, gate_ref, g_ref, o_ref):
    half = x1_ref.shape[1] // 2
    f32 = jnp.float32
    gate = gate_ref[...].T
    x1 = x1_ref[...]
    y_hi = x1[:, :half]
    y_lo = x1[:, half:]
    for k in range(TOP_K):
        hi, lo = _unpack_bf16_pairs(ya_ref[k])
        y_hi = y_hi + gate[:, k:k + 1] * hi.astype(f32)
        y_lo = y_lo + gate[:, k:k + 1] * lo.astype(f32)
    o_ref[...] = _rms(jnp.concatenate([y_hi, y_lo], axis=1), g_ref[...])


def _combine(ya, x1, gates, gain):
    T, D = x1.shape
    R = min(COMBINE_ROWS, T)
    row = lambda w: pl.BlockSpec((R, w), lambda i: (i, 0))
    return pl.pallas_call(
        _combine_kernel,
        grid=(T // R,),
        in_specs=[pl.BlockSpec((TOP_K, R, D // 2), lambda i: (0, i, 0)), row(D),
                  pl.BlockSpec((TOP_K, R), lambda i: (0, i)), pl.BlockSpec((1, D), lambda i: (0, 0))],
        out_specs=row(D),
        out_shape=jax.ShapeDtypeStruct((T, D), jnp.float32),
        compiler_params=_cparams(("parallel",)),
        name="combine",
    )(ya, x1, gates, gain.reshape(1, D))


def _moe_plan(eidx, rank, counts, A):
    counts = counts.reshape(N_EXPERTS)
    padded = (counts + EXPERT_ROWS - 1) // EXPERT_ROWS * EXPERT_ROWS
    pad_ends = jnp.cumsum(padded)
    pad_starts = pad_ends - padded
    n_blocks = -(-A // EXPERT_ROWS) + N_EXPERTS
    ids = jnp.arange(N_EXPERTS, dtype=jnp.int32)
    dest = rank + jnp.sum(jnp.where(eidx[None] == ids[:, None, None], pad_starts[:, None, None], 0), axis=0)
    block_start = jnp.arange(n_blocks, dtype=pad_ends.dtype) * EXPERT_ROWS
    block_expert = jnp.minimum(jnp.sum(pad_ends[None, :] <= block_start[:, None], axis=1),
                               N_EXPERTS - 1).astype(jnp.int32)
    n_used = (pad_ends[-1] // EXPERT_ROWS).astype(jnp.int32).reshape(1)
    has_rows = counts > 0
    ordinal = jnp.cumsum(has_rows.astype(jnp.int32)) - 1
    later = has_rows[None, :] & (ids[None, :] > ids[:, None])
    next_expert = jnp.where(jnp.any(later, axis=1), jnp.argmax(later, axis=1), -1).astype(jnp.int32)
    block_first = ((block_start == pad_starts[block_expert]) & (block_start < pad_ends[-1])).astype(jnp.int32)
    block_slot = (ordinal[block_expert] % 2).astype(jnp.int32)
    block_next = next_expert[block_expert]
    plan = (block_expert, n_used, block_first, block_slot, block_next)
    return dest.astype(jnp.int32), plan, n_blocks


def kernel(x, w_in, w_up_a, w_up_b, w_out, norm_mix, norm_ffn, norm_final, hgrn_norm,
           lb_logits, rel_bias, w_router, b_router, w_gu, b_gu, w_down, b_down):
    B, S, D = x.shape
    T = B * S
    assert w_in.shape[0] == 1, "the final rmsnorm is fused into the single layer's combine stage"
    lb_all = jnp.cumsum(jax.nn.softmax(lb_logits.astype(jnp.float32), axis=0), axis=0)
    x2 = x.reshape(T, D)
    (k, ik, qT, vT, iqT, iwT, bq, bf, bi, bg, ga, gb) = _inproj(x2, norm_mix[0], w_in[0], B, S)
    ya = _dsa(k, ik, qT, vT, iqT, iwT, rel_bias, B, S)
    yb = _hgrn(bq, bf, bi, bg, lb_all[0].reshape(B_HEADS, B_KEY_DIM), hgrn_norm[0], B, S)
    x1, xn, logits = _merge(x2, ya, yb, ga, gb, w_up_a[0], w_up_b[0], w_out[0], norm_ffn[0],
                            w_router[0], b_router[0])
    eidx, gates, rank, counts = _route(logits)
    dest, plan, n_blocks = _moe_plan(eidx, rank, counts, T * TOP_K)
    P = n_blocks * EXPERT_ROWS
    A = T * TOP_K
    xs = _sc_scatter_rows(xn, dest, P)
    y_buf = _experts(xs, plan, w_gu[0], b_gu[0], w_down[0], b_down[0])
    ya = _sc_gather_rows(y_buf, dest.reshape(A)).reshape(TOP_K, T, D // 2)
    out = _combine(ya, x1, gates, norm_final)
    return out.reshape(B, S, D)
```

```python
import functools
import math

import numpy as np
import jax
import jax.numpy as jnp
from jax import lax
from jax.experimental import pallas as pl
from jax.experimental.pallas import tpu as pltpu
from jax.experimental.pallas import tpu_sc as plsc

A_HEADS = 8
A_HEAD_DIM = 64
IDX_HEADS = 8
IDX_DIM = 32
TOPK_MAX = 256
REL_BUCKETS = 32
REL_MAX_DIST = 128
B_HEADS = 4
B_KEY_DIM = 128
B_VAL_DIM = 128
N_EXPERTS = 32
TOP_K = 4
SWIGLU_LIMIT = 7.0
SWIGLU_ALPHA = 1.702
EPS = 1e-6
LOG2_E = math.log2(math.e)

A_WIDTH = A_HEADS * A_HEAD_DIM
B_WIDTH = B_HEADS * B_VAL_DIM
IDX_WIDTH = IDX_HEADS * IDX_DIM

V7X_LANES = 128
V7X_SUBLANES = 8
V7X_VMEM_LIMIT_BYTES = 56 * 1024 * 1024
V7X_SC_CORES = 2
V7X_SC_SUBCORES = 16

PROJ_ROWS = 512
ATT_Q = 256
ATT_KC = 128
SCORE_UNROLL = 4
COUNT_UNROLL = 2
FAR_UNROLL = 4
HGRN_ROWS = 1024
HGRN_CHUNK = 64
HGRN_SAFE_DECAY = 70.0
ROUTE_ROWS = 512
EXPERT_ROWS = 512
COMBINE_ROWS = 512
SC_GATHER_ROWS = 64
MASK_NEG = -1e30
BISECT_FAST_ITERS = 26


def _cparams(dims):
    return pltpu.CompilerParams(dimension_semantics=dims, vmem_limit_bytes=V7X_VMEM_LIMIT_BYTES)


def _rms(x, gain):
    return x * lax.rsqrt(jnp.mean(x * x, axis=-1, keepdims=True) + EPS) * gain


def _sigmoid(x):
    return 1.0 / (1.0 + jnp.exp(-x))


def _pack_bf16_pairs(x):
    n = x.shape[1] // 2
    as_bits = lambda v: lax.bitcast_convert_type(v.astype(jnp.bfloat16).astype(jnp.float32), jnp.uint32)
    return (as_bits(x[:, :n]) & jnp.uint32(0xFFFF0000)) | (as_bits(x[:, n:]) >> 16)


def _unpack_bf16_pairs(w):
    hi = lax.bitcast_convert_type(w & jnp.uint32(0xFFFF0000), jnp.float32).astype(jnp.bfloat16)
    lo = lax.bitcast_convert_type(w << 16, jnp.float32).astype(jnp.bfloat16)
    return hi, lo


def _blocked_loop(n, step, carry, unroll):
    def run(first, count, cr):
        for t in range(count):
            cr = step(first + t, cr)
        return cr

    carry = lax.fori_loop(0, n // unroll, lambda j, cr: run(j * unroll, unroll, cr), carry)
    base = (n // unroll) * unroll
    piece = unroll // 2
    while piece >= 1:
        take = (n & piece) != 0
        carry = lax.cond(take, functools.partial(run, base, piece), lambda cr: cr, carry)
        base = base + jnp.where(take, piece, 0)
        piece //= 2
    return carry


def _fold_rows(x, op):
    return op(x.reshape(x.shape[0] // V7X_SUBLANES, V7X_SUBLANES, x.shape[1]), axis=0)


def _inproj_kernel(x_ref, g_ref, wk_ref, wik_ref, wqT_ref, wvT_ref, wiqT_ref, wiwT_ref, wb_ref, wg_ref,
                   k_ref, ik_ref, qT_ref, vT_ref, iqT_ref, iwT_ref, bq_ref, bf_ref, bi_ref, bg_ref,
                   ga_ref, gb_ref):
    x = x_ref[...]
    hn = _rms(x, g_ref[...]).astype(jnp.bfloat16)

    def mm(w_ref):
        return jnp.dot(hn, w_ref[...], preferred_element_type=jnp.float32)

    def mm_t(w_ref):
        return lax.dot_general(w_ref[...], hn, (((1,), (1,)), ((), ())),
                               preferred_element_type=jnp.float32)

    k_ref[...] = mm(wk_ref).astype(jnp.bfloat16)
    ik_ref[...] = mm(wik_ref).astype(jnp.bfloat16)
    qT_ref[0] = (mm_t(wqT_ref) * (A_HEAD_DIM ** -0.5 * LOG2_E)).astype(jnp.bfloat16)
    vT_ref[0] = mm_t(wvT_ref).astype(jnp.bfloat16)
    iqT_ref[0] = mm_t(wiqT_ref).astype(jnp.bfloat16)
    iwT_ref[0] = mm_t(wiwT_ref) * ((IDX_HEADS * IDX_DIM) ** -0.5)
    hb = mm(wb_ref)
    bq_ref[...] = hb[:, 0 * B_WIDTH:1 * B_WIDTH]
    bf_ref[...] = hb[:, 1 * B_WIDTH:2 * B_WIDTH]
    bi_ref[...] = hb[:, 2 * B_WIDTH:3 * B_WIDTH]
    bg_ref[...] = hb[:, 3 * B_WIDTH:4 * B_WIDTH]
    d = ga_ref.shape[-1]
    hg = mm(wg_ref)
    ga_ref[...] = _sigmoid(hg[:, :d]).astype(jnp.bfloat16)
    gb_ref[...] = _sigmoid(hg[:, d:]).astype(jnp.bfloat16)


def _inproj(x2, gain, w_in, B, S):
    T, D = x2.shape
    R = min(PROJ_ROWS, S)
    nS = S // R
    o = np.cumsum((0, A_WIDTH, A_WIDTH, A_WIDTH, IDX_WIDTH, IDX_HEADS, IDX_DIM,
                   B_WIDTH, B_WIDTH, B_WIDTH, B_WIDTH, D, D))
    bf = jnp.bfloat16
    wqT = w_in[:, o[0]:o[1]].T.astype(bf)
    wk = w_in[:, o[1]:o[2]].astype(bf)
    wvT = w_in[:, o[2]:o[3]].T.astype(bf)
    wiqT = w_in[:, o[3]:o[4]].T.astype(bf)
    wiwT = w_in[:, o[4]:o[5]].T.astype(bf)
    wik = w_in[:, o[5]:o[6]].astype(bf)
    wb = w_in[:, o[6]:o[10]].astype(bf)
    wg = w_in[:, o[10]:o[12]].astype(bf)

    def full(a):
        return pl.BlockSpec(a.shape, lambda b, i: (0,) * a.ndim)

    row = lambda n: pl.BlockSpec((R, n), lambda b, i: (b * nS + i, 0))
    colT = lambda n: pl.BlockSpec((1, n, R), lambda b, i: (b, 0, i))
    f32 = jnp.float32
    outs = [
        (jax.ShapeDtypeStruct((T, A_WIDTH), bf), row(A_WIDTH)),
        (jax.ShapeDtypeStruct((T, IDX_DIM), bf), row(IDX_DIM)),
        (jax.ShapeDtypeStruct((B, A_WIDTH, S), bf), colT(A_WIDTH)),
        (jax.ShapeDtypeStruct((B, A_WIDTH, S), bf), colT(A_WIDTH)),
        (jax.ShapeDtypeStruct((B, IDX_WIDTH, S), bf), colT(IDX_WIDTH)),
        (jax.ShapeDtypeStruct((B, IDX_HEADS, S), f32), colT(IDX_HEADS)),
        (jax.ShapeDtypeStruct((T, B_WIDTH), f32), row(B_WIDTH)),
        (jax.ShapeDtypeStruct((T, B_WIDTH), f32), row(B_WIDTH)),
        (jax.ShapeDtypeStruct((T, B_WIDTH), f32), row(B_WIDTH)),
        (jax.ShapeDtypeStruct((T, B_WIDTH), f32), row(B_WIDTH)),
        (jax.ShapeDtypeStruct((T, D), bf), row(D)),
        (jax.ShapeDtypeStruct((T, D), bf), row(D)),
    ]
    ins = [x2, gain.reshape(1, D), wk, wik, wqT, wvT, wiqT, wiwT, wb, wg]
    in_specs = [row(D)] + [full(a) for a in ins[1:]]
    return pl.pallas_call(
        _inproj_kernel,
        grid=(B, nS),
        in_specs=in_specs,
        out_specs=[s for _, s in outs],
        out_shape=[s for s, _ in outs],
        compiler_params=_cparams(("parallel", "parallel")),
        name="inproj",
    )(*ins)


def _t5_bucket_table(n):
    d = np.arange(n)
    max_exact = REL_BUCKETS // 2
    nf = np.maximum(d, 1).astype(np.float64)
    large = max_exact + (np.log(nf / max_exact) / math.log(REL_MAX_DIST / max_exact)
                         * (REL_BUCKETS - max_exact)).astype(np.int32)
    large = np.minimum(large, REL_BUCKETS - 1)
    return np.where(d < max_exact, d, large)


def _dsa_kernel(qT_ref, k_ref, vT_ref, iqT_ref, iwT_ref, ik_ref, enear_ref,
                o_ref, sc_ref, qh_scr, m_scr, acc_scr, *, topk):
    TQ = qT_ref.shape[2]
    KC = TQ
    i = pl.program_id(1)
    nch = i + 1
    q0 = i * TQ
    f32 = jnp.float32
    bf16 = jnp.bfloat16
    key_id = lax.broadcasted_iota(jnp.int32, (KC, TQ), 0)
    qry_id = lax.broadcasted_iota(jnp.int32, (KC, TQ), 1)

    def col_reduce(x, op):
        return op(_fold_rows(x, op), axis=0, keepdims=True)

    iw = iwT_ref[0]

    def score_chunk(c, carry):
        rmin, rmax = carry
        k0 = pl.multiple_of(c * KC, KC)
        ik = ik_ref[pl.ds(k0, KC), :]
        acc = jnp.zeros((KC, TQ), f32)
        for h in range(IDX_HEADS):
            sh = jnp.dot(ik, iqT_ref[0, h * IDX_DIM:(h + 1) * IDX_DIM, :], preferred_element_type=f32)
            acc = acc + jnp.maximum(sh, 0.0) * iw[h:h + 1, :]
        valid = (k0 + key_id) <= (q0 + qry_id)
        sc_ref[pl.ds(k0, KC), :] = jnp.where(valid, acc, MASK_NEG)
        rmin = jnp.minimum(rmin, _fold_rows(jnp.where(valid, acc, -MASK_NEG), jnp.min))
        rmax = jnp.maximum(rmax, _fold_rows(jnp.where(valid, acc, MASK_NEG), jnp.max))
        return rmin, rmax

    rmin8, rmax8 = _blocked_loop(
        nch, score_chunk,
        (jnp.full((V7X_SUBLANES, TQ), -MASK_NEG, f32), jnp.full((V7X_SUBLANES, TQ), MASK_NEG, f32)),
        SCORE_UNROLL)
    rmin = jnp.min(rmin8, axis=0, keepdims=True)
    rmax = jnp.max(rmax8, axis=0, keepdims=True)

    def count_where(pred_fn):
        def one(c, acc):
            k0 = pl.multiple_of(c * KC, KC)
            blk = sc_ref[pl.ds(k0, KC), :]
            return acc + _fold_rows(jnp.where(pred_fn(blk), 1.0, 0.0), jnp.sum)
        acc = _blocked_loop(nch, one, jnp.zeros((V7X_SUBLANES, TQ), f32), COUNT_UNROLL)
        return jnp.sum(acc, axis=0, keepdims=True)

    def band_min_max(lo, hi):
        def body(c, carry):
            bmin, bmax = carry
            k0 = pl.multiple_of(c * KC, KC)
            blk = sc_ref[pl.ds(k0, KC), :]
            bmin = jnp.minimum(bmin, _fold_rows(jnp.where(blk >= lo, blk, -MASK_NEG), jnp.min))
            bmax = jnp.maximum(bmax, _fold_rows(jnp.where(blk < hi, blk, MASK_NEG), jnp.max))
            return bmin, bmax
        bmin8, bmax8 = lax.fori_loop(
            0, nch, body,
            (jnp.full((V7X_SUBLANES, TQ), -MASK_NEG, f32), jnp.full((V7X_SUBLANES, TQ), MASK_NEG, f32)))
        return jnp.min(bmin8, axis=0, keepdims=True), jnp.max(bmax8, axis=0, keepdims=True)

    kf = float(topk)
    n_valid = (q0 + 1 + lax.broadcasted_iota(jnp.int32, (1, TQ), 1)).astype(f32)
    lo0 = rmin
    cnt0 = n_valid
    hi0 = rmax + jnp.maximum(jnp.abs(rmax) * 2.0 ** -20, 1e-30)
    done0 = jnp.where(cnt0 <= kf, 1.0, 0.0)

    def probe(st, lo_s, mid, tie):
        it, lo, hi, cnt, done = st
        active = done < 0.5
        lo_s = jnp.where(active, lo_s, lo)
        c = count_where(lambda blk: blk >= mid)
        feas = c >= kf
        move = active & jnp.logical_not(tie)
        lo_n = jnp.where(move & feas, mid, lo_s)
        cnt_n = jnp.where(move & feas, c, cnt)
        hi_n = jnp.where(move & jnp.logical_not(feas), mid, hi)
        done_n = jnp.where((active & tie) | (cnt_n <= kf), 1.0, done)
        return it + 1, lo_n, hi_n, cnt_n, done_n

    def halve(st):
        _, lo, hi, _, _ = st
        half = lo + 0.5 * (hi - lo)
        stuck = (half <= lo) | (half >= hi)
        return probe(st, lo, half, stuck)

    def snap(st):
        _, lo, hi, _, _ = st
        bmin, bmax = band_min_max(lo, hi)
        mid = bmin + 0.5 * (bmax - bmin)
        return probe(st, bmin, jnp.where(mid <= bmin, bmax, mid), bmax <= bmin)

    st = lax.while_loop(lambda st: (jnp.min(st[-1]) < 0.5) & (st[0] < BISECT_FAST_ITERS),
                        lambda st: halve(halve(st)), (jnp.int32(0), lo0, hi0, cnt0, done0))
    _, thr, _, cnt_thr, _ = lax.while_loop(lambda st: jnp.min(st[-1]) < 0.5, snap, st)

    tie_overflow = jnp.max(cnt_thr) > kf

    @pl.when(jnp.logical_not(tie_overflow))
    def _():
        def mask_chunk(c, _):
            k0 = pl.multiple_of(c * KC, KC)
            sc_ref[pl.ds(k0, KC), :] = jnp.where(sc_ref[pl.ds(k0, KC), :] >= thr, 0.0, MASK_NEG)
            return 0
        lax.fori_loop(0, nch, mask_chunk, 0)

    @pl.when(tie_overflow)
    def _():
        need = kf - count_where(lambda blk: blk > thr)
        tril = jnp.where(lax.broadcasted_iota(jnp.int32, (KC, KC), 1)
                         <= lax.broadcasted_iota(jnp.int32, (KC, KC), 0), 1.0, 0.0).astype(bf16)

        def mask_chunk(c, run):
            k0 = pl.multiple_of(c * KC, KC)
            blk = sc_ref[pl.ds(k0, KC), :]
            eq = jnp.where(blk == thr, 1.0, 0.0)
            pref = jnp.dot(tril, eq.astype(bf16), preferred_element_type=f32)
            sel = (blk > thr) | ((eq > 0.5) & (run + pref <= need))
            sc_ref[pl.ds(k0, KC), :] = jnp.where(sel, 0.0, MASK_NEG)
            return run + pref[KC - 1:KC, :]

        lax.fori_loop(0, nch, mask_chunk, jnp.zeros((1, TQ), f32))

    AK = min(ATT_KC, TQ)
    per = TQ // AK
    head0_q = (lax.broadcasted_iota(jnp.int32, (V7X_LANES, TQ), 0) // A_HEAD_DIM) == 0
    n_pairs = A_HEADS // 2

    m_scr[...] = jnp.full(m_scr.shape, MASK_NEG, f32)
    acc_scr[...] = jnp.zeros(acc_scr.shape, f32)
    v_row = lax.broadcasted_iota(jnp.int32, (V7X_LANES, AK), 0)
    denom_row = [A_HEAD_DIM * (1 - sub) for sub in range(2)]
    for p in range(n_pairs):
        q_pair = qT_ref[0, p * V7X_LANES:(p + 1) * V7X_LANES, :]
        zq = jnp.zeros_like(q_pair)
        qh_scr[2 * p] = jnp.where(head0_q, q_pair, zq)
        qh_scr[2 * p + 1] = jnp.where(head0_q, zq, q_pair)

    def step(c, bias_rows):
        k0 = pl.multiple_of(c * AK, AK)
        msk = sc_ref[pl.ds(k0, AK), :]
        for p in range(n_pairs):
            kp = k_ref[pl.ds(k0, AK), p * V7X_LANES:(p + 1) * V7X_LANES]
            vp = vT_ref[0, p * V7X_LANES:(p + 1) * V7X_LANES, pl.ds(k0, AK)]
            for sub in range(2):
                h = 2 * p + sub
                s = jnp.dot(kp, qh_scr[h], preferred_element_type=f32) + msk
                if bias_rows is not None:
                    s = s + enear_ref[h, bias_rows, :]
                m = m_scr[h:h + 1, :]
                m_new = jnp.maximum(m, col_reduce(s, jnp.max))
                alpha = jnp.exp2(m - m_new)
                pr = jnp.exp2(s - m_new)
                m_scr[h:h + 1, :] = m_new
                v_aug = jnp.where(v_row == denom_row[sub], jnp.ones_like(vp), vp)
                acc_scr[h] = alpha * acc_scr[h] + jnp.dot(v_aug, pr.astype(bf16), preferred_element_type=f32)

    def far(blk, _):
        for jj in range(per):
            step(blk * per + jj, None)
        return 0

    def near(block, first_chunk):
        for jj in range(per):
            step(first_chunk + jj, slice(block * TQ + jj * AK, block * TQ + (jj + 1) * AK))

    _blocked_loop(jnp.maximum(i - 1, 0), far, 0, FAR_UNROLL)

    @pl.when(i >= 1)
    def _():
        near(0, (i - 1) * per)

    near(1, i * per)
    for p in range(n_pairs):
        outs = [acc_scr[2 * p + sub] / acc_scr[2 * p + sub, denom_row[sub]:denom_row[sub] + 1, :]
                for sub in range(2)]
        o_pair = jnp.where(head0_q, outs[0], outs[1])
        o_ref[:, p * V7X_LANES:(p + 1) * V7X_LANES] = o_pair.T.astype(o_ref.dtype)


def _dsa(k, ik, qT, vT, iqT, iwT, rel_bias, B, S):
    T = k.shape[0]
    TQ = min(ATT_Q, S)
    nQ = S // TQ
    topk = min(TOPK_MAX, S // 4)
    buckets = _t5_bucket_table(2 * TQ + 1)
    assert np.all(_t5_bucket_table(S + 1)[TQ + 1:] == REL_BUCKETS - 1)
    j = np.arange(2 * TQ)[:, None]
    r = np.arange(TQ)[None, :]
    dist = np.maximum(r + TQ - j, 0)
    onehot = (jnp.asarray(buckets[dist], jnp.int32)[None]
              == jnp.arange(REL_BUCKETS, dtype=jnp.int32)[:, None, None]).astype(jnp.float32)
    rel = (rel_bias.astype(jnp.float32) - rel_bias[REL_BUCKETS - 1].astype(jnp.float32)[None, :]) * LOG2_E
    enear = jnp.einsum('nh,njr->hjr', rel, onehot, precision=lax.Precision.HIGHEST)

    return pl.pallas_call(
        functools.partial(_dsa_kernel, topk=topk),
        grid=(B, nQ),
        in_specs=[
            pl.BlockSpec((1, A_WIDTH, TQ), lambda b, i: (b, 0, i)),
            pl.BlockSpec((S, A_WIDTH), lambda b, i: (b, 0)),
            pl.BlockSpec((1, A_WIDTH, S), lambda b, i: (b, 0, 0)),
            pl.BlockSpec((1, IDX_WIDTH, TQ), lambda b, i: (b, 0, i)),
            pl.BlockSpec((1, IDX_HEADS, TQ), lambda b, i: (b, 0, i)),
            pl.BlockSpec((S, IDX_DIM), lambda b, i: (b, 0)),
            pl.BlockSpec((A_HEADS, 2 * TQ, TQ), lambda b, i: (0, 0, 0)),
        ],
        out_specs=pl.BlockSpec((TQ, A_WIDTH), lambda b, i: (b * nQ + i, 0)),
        scratch_shapes=[pltpu.VMEM((S, TQ), jnp.float32),
                        pltpu.VMEM((A_HEADS, V7X_LANES, TQ), jnp.bfloat16),
                        pltpu.VMEM((A_HEADS, TQ), jnp.float32),
                        pltpu.VMEM((A_HEADS, V7X_LANES, TQ), jnp.float32)],
        out_shape=jax.ShapeDtypeStruct((T, A_WIDTH), jnp.bfloat16),
        compiler_params=_cparams(("parallel", "arbitrary")),
        name="dsa",
    )(qT, k, vT, iqT, iwT, ik, enear)


def _hgrn_kernel(bq_ref, bf_ref, bi_ref, bg_ref, lb_ref, gain_ref, o_ref,
                 st_ref, b_scr, q_scr, k_scr, v_scr, oi_scr, qd_s, kl_s, vv_s, dec_s, oi_s, upd_s, st_s,
                 kdT_s, vvT_s):
    R = bq_ref.shape[0]
    C = HGRN_CHUNK
    nC = R // C
    f32 = jnp.float32
    bf16 = jnp.bfloat16
    h = pl.program_id(1)

    @pl.when(pl.program_id(2) == 0)
    def _():
        st_ref[...] = jnp.zeros_like(st_ref)

    lb = lb_ref[pl.ds(h, 1), :]
    gain = gain_ref[pl.ds(h, 1), :]
    tril_incl = jnp.where(lax.broadcasted_iota(jnp.int32, (C, C), 1)
                          <= lax.broadcasted_iota(jnp.int32, (C, C), 0), 1.0, 0.0)
    srow = lax.broadcasted_iota(jnp.int32, (C, B_KEY_DIM), 0)

    def gates(r0):
        f = lb + (1.0 - lb) * _sigmoid(bf_ref[pl.ds(r0, C), :])
        qr = bq_ref[pl.ds(r0, C), :]
        return jnp.log(f), 1.0 - f, qr * _sigmoid(qr) * (B_KEY_DIM ** -0.5), bi_ref[pl.ds(r0, C), :]

    def cumdecay(g):
        tri = tril_incl.astype(bf16)
        g_hi = g.astype(bf16)
        rest = g - g_hi.astype(f32)
        g_mid = rest.astype(bf16)
        g_lo = (rest - g_mid.astype(f32)).astype(bf16)
        return (jnp.dot(tri, g_hi, preferred_element_type=f32) + jnp.dot(tri, g_mid, preferred_element_type=f32)
                + jnp.dot(tri, g_lo, preferred_element_type=f32))

    def advance(r0, st, qd, o_intra, upd, decay_row):
        o_inter = lax.dot_general(qd, st.astype(bf16), (((1,), (1,)), ((), ())), preferred_element_type=f32)
        og = bg_ref[pl.ds(r0, C), :]
        y = _rms(o_inter + o_intra, gain) * (og * _sigmoid(og))
        o_ref[pl.ds(r0, C), :] = y.astype(o_ref.dtype)
        return st * decay_row + upd

    f_all = lb + (1.0 - lb) * _sigmoid(bf_ref[...])
    g_all = jnp.log(f_all)
    decay = jnp.sum(g_all.reshape(nC, C, B_KEY_DIM), axis=1)
    safe = jnp.min(decay) >= -HGRN_SAFE_DECAY

    @pl.when(safe)
    def _():
        qr = bq_ref[...]
        qq = qr * _sigmoid(qr) * (B_KEY_DIM ** -0.5)
        kk = 1.0 - f_all
        b = jnp.concatenate([cumdecay(g_all[c * C:(c + 1) * C]) for c in range(nC)], axis=0)
        b_end = jnp.concatenate([jnp.broadcast_to(b[(c + 1) * C - 1:(c + 1) * C], (C, B_KEY_DIM))
                                 for c in range(nC)], axis=0)
        qd_s[...] = (qq * jnp.exp(b)).astype(bf16)
        kd = kk * jnp.exp(-b)
        kl_s[...] = (kk * jnp.exp(b_end - b)).astype(bf16)
        vv = bi_ref[...]
        vv_s[...] = vv.astype(bf16)
        dec_s[...] = jnp.exp(b_end)
        for c in range(nC):
            rows = slice(c * C, (c + 1) * C)
            kdT_s[c] = kd[rows].T.astype(bf16)
            vvT_s[c] = vv[rows].T.astype(bf16)
        for c in range(nC):
            rows = slice(c * C, (c + 1) * C)
            att = jnp.dot(qd_s[rows], kdT_s[c], preferred_element_type=f32) * tril_incl
            oi_s[rows] = jnp.dot(att.astype(bf16), vv_s[rows], preferred_element_type=f32)
            upd_s[c] = jnp.dot(vvT_s[c], kl_s[rows], preferred_element_type=f32)
        st = st_ref[...]
        for c in range(nC):
            st_s[c] = st.T.astype(bf16)
            st = st * dec_s[c * C:c * C + 1] + upd_s[c]
        st_ref[...] = st
        for c in range(nC):
            rows = slice(c * C, (c + 1) * C)
            oi_s[rows] = oi_s[rows] + jnp.dot(qd_s[rows], st_s[c], preferred_element_type=f32)
        og = bg_ref[...]
        o_ref[...] = (_rms(oi_s[...], gain) * (og * _sigmoid(og))).astype(o_ref.dtype)

    @pl.when(jnp.logical_not(safe))
    def _():
        def body(c, st):
            r0 = pl.multiple_of(c * C, C)
            g, kk, qq, vv = gates(r0)
            b = cumdecay(g)
            b_last = b[C - 1:C, :]
            b_scr[...] = b
            q_scr[...] = qq
            k_scr[...] = kk
            v_scr[...] = vv

            def row(t, _):
                bt = b_scr[pl.ds(t, 1), :]
                qt = q_scr[pl.ds(t, 1), :]
                ex = jnp.where(srow <= t, bt - b_scr[...], -jnp.inf)
                a = jnp.sum(qt * k_scr[...] * jnp.exp(ex), axis=1, keepdims=True)
                oi_scr[pl.ds(t, 1), :] = jnp.sum(a * v_scr[...], axis=0, keepdims=True)
                return 0
            lax.fori_loop(0, C, row, 0)
            kd_last = (kk * jnp.exp(b_last - b)).astype(bf16)
            upd = lax.dot_general(vv.astype(bf16), kd_last, (((0,), (0,)), ((), ())),
                                  preferred_element_type=f32)
            return advance(r0, st, (qq * jnp.exp(b)).astype(bf16), oi_scr[...], upd, jnp.exp(b_last))
        st_ref[...] = lax.fori_loop(0, nC, body, st_ref[...])


def _hgrn(bq, bf, bi, bg, lb, gain, B, S):
    T = bq.shape[0]
    R = min(HGRN_ROWS, S)
    nR = S // R
    C = HGRN_CHUNK
    blk = pl.BlockSpec((R, B_KEY_DIM), lambda b, h, c: (b * nR + c, h))
    small = pl.BlockSpec((B_HEADS, B_KEY_DIM), lambda b, h, c: (0, 0))
    f32 = jnp.float32
    return pl.pallas_call(
        _hgrn_kernel,
        grid=(B, B_HEADS, nR),
        in_specs=[blk, blk, blk, blk, small, small],
        out_specs=blk,
        out_shape=jax.ShapeDtypeStruct((T, B_WIDTH), jnp.bfloat16),
        scratch_shapes=[pltpu.VMEM((B_VAL_DIM, B_KEY_DIM), f32)] +
                       [pltpu.VMEM((C, B_KEY_DIM), f32) for _ in range(5)] +
                       [pltpu.VMEM((R, B_KEY_DIM), jnp.bfloat16) for _ in range(3)] +
                       [pltpu.VMEM((R, B_KEY_DIM), f32) for _ in range(2)] +
                       [pltpu.VMEM((R // C, B_VAL_DIM, B_KEY_DIM), f32),
                        pltpu.VMEM((R // C, B_KEY_DIM, B_VAL_DIM), jnp.bfloat16),
                        pltpu.VMEM((R // C, B_KEY_DIM, C), jnp.bfloat16),
                        pltpu.VMEM((R // C, B_VAL_DIM, C), jnp.bfloat16)],
        compiler_params=_cparams(("parallel", "parallel", "arbitrary")),
        name="hgrn",
    )(bq, bf, bi, bg, lb, gain)


def _merge_kernel(x_ref, ya_ref, yb_ref, ga_ref, gb_ref, wa_ref, wb_ref, wo_ref, g_ref, wrh_ref, wrl_ref, br_ref,
                  x1_ref, xn_ref, lg_ref):
    f32 = jnp.float32
    ma = jnp.dot(ya_ref[...], wa_ref[...], preferred_element_type=f32)
    mb = jnp.dot(yb_ref[...], wb_ref[...], preferred_element_type=f32)
    merged = ga_ref[...].astype(f32) * ma + gb_ref[...].astype(f32) * mb
    x1 = x_ref[...] + jnp.dot(merged.astype(jnp.bfloat16), wo_ref[...], preferred_element_type=f32)
    x1_ref[...] = x1
    hn = _rms(x1, g_ref[...])
    xn_ref[...] = _pack_bf16_pairs(hn)
    hn_hi = hn.astype(jnp.bfloat16)
    hn_lo = (hn - hn_hi.astype(f32)).astype(jnp.bfloat16)
    lg_ref[...] = (jnp.dot(hn_hi, wrh_ref[...], preferred_element_type=f32)
                   + jnp.dot(hn_lo, wrh_ref[...], preferred_element_type=f32)
                   + jnp.dot(hn_hi, wrl_ref[...], preferred_element_type=f32) + br_ref[...])


def _merge(x2, ya, yb, ga, gb, w_up_a, w_up_b, w_out, gain, w_router, b_router):
    T, D = x2.shape
    R = min(PROJ_ROWS, T)
    bf = jnp.bfloat16
    wr_hi = w_router.astype(bf)
    wr_lo = (w_router - wr_hi.astype(jnp.float32)).astype(bf)
    ins = [x2, ya, yb, ga, gb, w_up_a.astype(bf), w_up_b.astype(bf), w_out.astype(bf),
           gain.reshape(1, D), wr_hi, wr_lo, b_router.reshape(1, N_EXPERTS)]
    row = lambda n: pl.BlockSpec((R, n), lambda i: (i, 0))
    full = lambda a: pl.BlockSpec(a.shape, lambda i: (0,) * a.ndim)
    in_specs = [row(D), row(A_WIDTH), row(B_WIDTH), row(D), row(D)] + [full(a) for a in ins[5:]]
    return pl.pallas_call(
        _merge_kernel,
        grid=(T // R,),
        in_specs=in_specs,
        out_specs=[row(D), row(D // 2), row(N_EXPERTS)],
        out_shape=[jax.ShapeDtypeStruct((T, D), jnp.float32), jax.ShapeDtypeStruct((T, D // 2), jnp.uint32),
                   jax.ShapeDtypeStruct((T, N_EXPERTS), jnp.float32)],
        compiler_params=_cparams(("parallel",)),
        name="merge",
    )(*ins)


def _route_kernel(lg_ref, eidx_ref, gate_ref, rank_ref, cnt_ref, run_ref):
    R = lg_ref.shape[0]
    f32 = jnp.float32

    @pl.when(pl.program_id(0) == 0)
    def _():
        run_ref[...] = jnp.zeros_like(run_ref)

    lg = lg_ref[...].T
    expert = lax.broadcasted_iota(jnp.int32, (N_EXPERTS, R), 0)
    work = lg
    onehots, vals, idxs = [], [], []
    for _ in range(TOP_K):
        m = jnp.max(work, axis=0, keepdims=True)
        idx = jnp.min(jnp.where(work == m, expert, N_EXPERTS), axis=0, keepdims=True)
        oh = expert == idx
        onehots.append(oh)
        vals.append(m)
        idxs.append(idx)
        work = jnp.where(oh, -jnp.inf, work)
    ex = [jnp.exp(v - vals[0]) for v in vals]
    den = ex[0] + ex[1] + ex[2] + ex[3]
    chosen = jnp.where(onehots[0] | onehots[1] | onehots[2] | onehots[3], 1.0, 0.0)
    earlier = jnp.where(lax.broadcasted_iota(jnp.int32, (R, R), 0)
                        < lax.broadcasted_iota(jnp.int32, (R, R), 1), 1.0, 0.0).astype(jnp.bfloat16)
    before = jnp.dot(chosen.astype(jnp.bfloat16), earlier, preferred_element_type=f32) + run_ref[...]
    slot = lax.broadcasted_iota(jnp.int32, (TOP_K, R), 0)
    eidx = jnp.zeros((TOP_K, R), jnp.int32)
    gate = jnp.zeros((TOP_K, R), f32)
    rank = jnp.zeros((TOP_K, R), f32)
    for k in range(TOP_K):
        eidx = jnp.where(slot == k, idxs[k], eidx)
        gate = jnp.where(slot == k, ex[k] / den, gate)
        rk = jnp.sum(jnp.where(onehots[k], before, 0.0), axis=0, keepdims=True)
        rank = jnp.where(slot == k, rk, rank)
    eidx_ref[...] = eidx
    gate_ref[...] = gate
    rank_ref[...] = rank.astype(jnp.int32)
    run_ref[...] = run_ref[...] + jnp.sum(chosen, axis=1, keepdims=True)
    cnt_ref[...] = run_ref[...].astype(jnp.int32)


def _route(logits):
    T = logits.shape[0]
    R = min(ROUTE_ROWS, T)
    col = pl.BlockSpec((TOP_K, R), lambda i: (0, i))
    return pl.pallas_call(
        _route_kernel,
        grid=(T // R,),
        in_specs=[pl.BlockSpec((R, N_EXPERTS), lambda i: (i, 0))],
        out_specs=[col, col, col, pl.BlockSpec((N_EXPERTS, 1), lambda i: (0, 0))],
        out_shape=[jax.ShapeDtypeStruct((TOP_K, T), jnp.int32), jax.ShapeDtypeStruct((TOP_K, T), jnp.float32),
                   jax.ShapeDtypeStruct((TOP_K, T), jnp.int32), jax.ShapeDtypeStruct((N_EXPERTS, 1), jnp.int32)],
        scratch_shapes=[pltpu.VMEM((N_EXPERTS, 1), jnp.float32)],
        compiler_params=_cparams(("arbitrary",)),
        name="route",
    )(logits)


def _sc_gather_rows(table, idx):
    N, W = table.shape
    M = idx.shape[0]
    workers = V7X_SC_CORES * V7X_SC_SUBCORES
    per_worker = M // workers
    pieces = per_worker // SC_GATHER_ROWS
    assert per_worker * workers == M and pieces * SC_GATHER_ROWS == per_worker and pieces % 2 == 0
    mesh = plsc.VectorSubcoreMesh(core_axis_name="core", subcore_axis_name="subcore",
                                  num_cores=V7X_SC_CORES, num_subcores=V7X_SC_SUBCORES)

    @functools.partial(
        pl.kernel, mesh=mesh,
        out_type=jax.ShapeDtypeStruct((M, W), table.dtype),
        scratch_types=[pltpu.VMEM((per_worker,), jnp.int32),
                       pltpu.VMEM((SC_GATHER_ROWS, W), table.dtype),
                       pltpu.VMEM((SC_GATHER_ROWS, W), table.dtype),
                       pltpu.SemaphoreType.DMA, pltpu.SemaphoreType.DMA],
    )
    def gather(table_hbm, idx_hbm, out_hbm, idx_v, rows_a, rows_b, sem_a, sem_b):
        worker = lax.axis_index("subcore") * V7X_SC_CORES + lax.axis_index("core")
        base = pl.multiple_of(worker * per_worker, SC_GATHER_ROWS)
        pltpu.sync_copy(idx_hbm.at[pl.ds(base, per_worker)], idx_v)
        bufs = ((rows_a, sem_a), (rows_b, sem_b))

        def fetch(g, buf, sem):
            off = pl.multiple_of(g * SC_GATHER_ROWS, SC_GATHER_ROWS)
            return pltpu.make_async_copy(table_hbm.at[idx_v.at[pl.ds(off, SC_GATHER_ROWS)]], buf, sem)

        fetch(0, *bufs[0]).start()

        @pl.loop(0, pieces, step=2)
        def _(g0):
            for half in range(2):
                g = g0 + half
                buf, sem = bufs[half]
                fetch(g, buf, sem).wait()

                @pl.when(g + 1 < pieces)
                def _():
                    fetch(g + 1, *bufs[1 - half]).start()

                off = pl.multiple_of(g * SC_GATHER_ROWS, SC_GATHER_ROWS)
                pltpu.sync_copy(buf, out_hbm.at[pl.ds(base + off, SC_GATHER_ROWS)])

    return gather(table, idx)


def _sc_scatter_rows(rows, dest, n_out):
    T, W = rows.shape
    slots = dest.shape[0]
    workers = V7X_SC_CORES * V7X_SC_SUBCORES
    per_worker = T // workers
    pieces = per_worker // SC_GATHER_ROWS
    assert per_worker * workers == T and pieces * SC_GATHER_ROWS == per_worker and pieces % 2 == 0
    idx = dest.reshape(slots, workers, pieces, SC_GATHER_ROWS).transpose(1, 2, 0, 3)
    mesh = plsc.VectorSubcoreMesh(core_axis_name="core", subcore_axis_name="subcore",
                                  num_cores=V7X_SC_CORES, num_subcores=V7X_SC_SUBCORES)

    @functools.partial(
        pl.kernel, mesh=mesh,
        out_type=jax.ShapeDtypeStruct((n_out, W), rows.dtype),
        scratch_types=[pltpu.VMEM((pieces, slots, SC_GATHER_ROWS), jnp.int32),
                       pltpu.VMEM((SC_GATHER_ROWS, W), rows.dtype),
                       pltpu.VMEM((SC_GATHER_ROWS, W), rows.dtype),
                       pltpu.SemaphoreType.DMA, pltpu.SemaphoreType.DMA, pltpu.SemaphoreType.DMA],
    )
    def scatter(rows_hbm, idx_hbm, out_hbm, idx_v, rows_a, rows_b, sem_a, sem_b, sem_out):
        worker = lax.axis_index("subcore") * V7X_SC_CORES + lax.axis_index("core")
        base = pl.multiple_of(worker * per_worker, SC_GATHER_ROWS)
        pltpu.sync_copy(idx_hbm.at[worker], idx_v)
        bufs = ((rows_a, sem_a), (rows_b, sem_b))

        def fetch(g, buf, sem):
            off = pl.multiple_of(g * SC_GATHER_ROWS, SC_GATHER_ROWS)
            return pltpu.make_async_copy(rows_hbm.at[pl.ds(base + off, SC_GATHER_ROWS)], buf, sem)

        fetch(0, *bufs[0]).start()

        @pl.loop(0, pieces, step=2)
        def _(g0):
            for half in range(2):
                g = g0 + half
                buf, sem = bufs[half]
                fetch(g, buf, sem).wait()

                @pl.when(g + 1 < pieces)
                def _():
                    fetch(g + 1, *bufs[1 - half]).start()

                puts = [pltpu.make_async_copy(buf, out_hbm.at[idx_v.at[g, k]], sem_out) for k in range(slots)]
                for put in puts:
                    put.start()
                for put in puts:
                    put.wait()

    return scatter(rows, idx)


def _experts_kernel(be_ref, nb_ref, first_ref, slot_ref, next_ref,
                    x_ref, wgu_hbm, bgu_ref, wd_hbm, bd_ref, o_ref,
                    wgu_f32, wd_f32, wgu_bf, wd_bf, sems):
    f32 = jnp.float32
    d_ff = wd_bf.shape[0]
    i = pl.program_id(0)
    live = i < nb_ref[0]

    def weight_copies(e, s):
        return (pltpu.make_async_copy(wgu_hbm.at[e], wgu_f32.at[s], sems.at[s, 0]),
                pltpu.make_async_copy(wd_hbm.at[e], wd_f32.at[s], sems.at[s, 1]))

    @pl.when(live & (first_ref[i] == 1))
    def _():
        e = be_ref[i]
        s = slot_ref[i]

        @pl.when(i == 0)
        def _():
            for cp in weight_copies(e, s):
                cp.start()

        for cp in weight_copies(e, s):
            cp.wait()

        @pl.when(next_ref[i] >= 0)
        def _():
            for cp in weight_copies(next_ref[i], 1 - s):
                cp.start()

        wgu_bf[...] = wgu_f32[s].astype(jnp.bfloat16)
        wd_bf[...] = wd_f32[s].astype(jnp.bfloat16)

    @pl.when(live)
    def _():
        x_hi, x_lo = _unpack_bf16_pairs(x_ref[...])
        half = x_hi.shape[1]
        gu = (jnp.dot(x_hi, wgu_bf[:half, :], preferred_element_type=f32)
              + jnp.dot(x_lo, wgu_bf[half:, :], preferred_element_type=f32) + bgu_ref[0])
        gate = jnp.minimum(gu[:, :d_ff], SWIGLU_LIMIT)
        lin = jnp.clip(gu[:, d_ff:], -SWIGLU_LIMIT, SWIGLU_LIMIT)
        act = (lin + 1.0) * gate * _sigmoid(SWIGLU_ALPHA * gate)
        y = jnp.dot(act.astype(jnp.bfloat16), wd_bf[...], preferred_element_type=f32) + bd_ref[0]
        o_ref[...] = _pack_bf16_pairs(y)

    @pl.when(pl.program_id(0) >= nb_ref[0])
    def _():
        o_ref[...] = jnp.zeros_like(o_ref)


def _experts(xs, plan, w_gu, b_gu, w_down, b_down):
    P, W = xs.shape
    E, D, F2 = w_gu.shape
    nb = P // EXPERT_ROWS
    by_expert = lambda i, be, *_: (be[i], 0, 0)
    grid_spec = pltpu.PrefetchScalarGridSpec(
        num_scalar_prefetch=5,
        grid=(nb,),
        in_specs=[
            pl.BlockSpec((EXPERT_ROWS, W), lambda i, *_: (i, 0)),
            pl.BlockSpec(memory_space=pl.ANY),
            pl.BlockSpec((1, 1, F2), by_expert),
            pl.BlockSpec(memory_space=pl.ANY),
            pl.BlockSpec((1, 1, D), by_expert),
        ],
        out_specs=pl.BlockSpec((EXPERT_ROWS, W), lambda i, *_: (i, 0)),
        scratch_shapes=[pltpu.VMEM((2, D, F2), jnp.float32), pltpu.VMEM((2, F2 // 2, D), jnp.float32),
                        pltpu.VMEM((D, F2), jnp.bfloat16), pltpu.VMEM((F2 // 2, D), jnp.bfloat16),
                        pltpu.SemaphoreType.DMA((2, 2))],
    )
    return pl.pallas_call(
        _experts_kernel,
        grid_spec=grid_spec,
        out_shape=jax.ShapeDtypeStruct((P, W), jnp.uint32),
        compiler_params=_cparams(("arbitrary",)),
        name="experts",
    )(*plan, xs, w_gu, b_gu.reshape(E, 1, F2), w_down, b_down.reshape(E, 1, D))


def _combine_kernel(ya_ref, x1_ref, gate_ref, g_ref, o_ref):
    half = x1_ref.shape[1] // 2
    f32 = jnp.float32
    gate = gate_ref[...].T
    x1 = x1_ref[...]
    y_hi = x1[:, :half]
    y_lo = x1[:, half:]
    for k in range(TOP_K):
        hi, lo = _unpack_bf16_pairs(ya_ref[k])
        y_hi = y_hi + gate[:, k:k + 1] * hi.astype(f32)
        y_lo = y_lo + gate[:, k:k + 1] * lo.astype(f32)
    o_ref[...] = _rms(jnp.concatenate([y_hi, y_lo], axis=1), g_ref[...])


def _combine(ya, x1, gates, gain):
    T, D = x1.shape
    R = min(COMBINE_ROWS, T)
    row = lambda w: pl.BlockSpec((R, w), lambda i: (i, 0))
    return pl.pallas_call(
        _combine_kernel,
        grid=(T // R,),
        in_specs=[pl.BlockSpec((TOP_K, R, D // 2), lambda i: (0, i, 0)), row(D),
                  pl.BlockSpec((TOP_K, R), lambda i: (0, i)), pl.BlockSpec((1, D), lambda i: (0, 0))],
        out_specs=row(D),
        out_shape=jax.ShapeDtypeStruct((T, D), jnp.float32),
        compiler_params=_cparams(("parallel",)),
        name="combine",
    )(ya, x1, gates, gain.reshape(1, D))


def _moe_plan(eidx, rank, counts, A):
    counts = counts.reshape(N_EXPERTS)
    padded = (counts + EXPERT_ROWS - 1) // EXPERT_ROWS * EXPERT_ROWS
    pad_ends = jnp.cumsum(padded)
    pad_starts = pad_ends - padded
    n_blocks = -(-A // EXPERT_ROWS) + N_EXPERTS
    ids = jnp.arange(N_EXPERTS, dtype=jnp.int32)
    dest = rank + jnp.sum(jnp.where(eidx[None] == ids[:, None, None], pad_starts[:, None, None], 0), axis=0)
    block_start = jnp.arange(n_blocks, dtype=pad_ends.dtype) * EXPERT_ROWS
    block_expert = jnp.minimum(jnp.sum(pad_ends[None, :] <= block_start[:, None], axis=1),
                               N_EXPERTS - 1).astype(jnp.int32)
    n_used = (pad_ends[-1] // EXPERT_ROWS).astype(jnp.int32).reshape(1)
    has_rows = counts > 0
    ordinal = jnp.cumsum(has_rows.astype(jnp.int32)) - 1
    later = has_rows[None, :] & (ids[None, :] > ids[:, None])
    next_expert = jnp.where(jnp.any(later, axis=1), jnp.argmax(later, axis=1), -1).astype(jnp.int32)
    block_first = ((block_start == pad_starts[block_expert]) & (block_start < pad_ends[-1])).astype(jnp.int32)
    block_slot = (ordinal[block_expert] % 2).astype(jnp.int32)
    block_next = next_expert[block_expert]
    plan = (block_expert, n_used, block_first, block_slot, block_next)
    return dest.astype(jnp.int32), plan, n_blocks


def kernel(x, w_in, w_up_a, w_up_b, w_out, norm_mix, norm_ffn, norm_final, hgrn_norm,
           lb_logits, rel_bias, w_router, b_router, w_gu, b_gu, w_down, b_down):
    B, S, D = x.shape
    T = B * S
    assert w_in.shape[0] == 1, "the final rmsnorm is fused into the single layer's combine stage"
    lb_all = jnp.cumsum(jax.nn.softmax(lb_logits.astype(jnp.float32), axis=0), axis=0)
    x2 = x.reshape(T, D)
    (k, ik, qT, vT, iqT, iwT, bq, bf, bi, bg, ga, gb) = _inproj(x2, norm_mix[0], w_in[0], B, S)
    ya = _dsa(k, ik, qT, vT, iqT, iwT, rel_bias, B, S)
    yb = _hgrn(bq, bf, bi, bg, lb_all[0].reshape(B_HEADS, B_KEY_DIM), hgrn_norm[0], B, S)
    x1, xn, logits = _merge(x2, ya, yb, ga, gb, w_up_a[0], w_up_b[0], w_out[0], norm_ffn[0],
                            w_router[0], b_router[0])
    eidx, gates, rank, counts = _route(logits)
    dest, plan, n_blocks = _moe_plan(eidx, rank, counts, T * TOP_K)
    P = n_blocks * EXPERT_ROWS
    A = T * TOP_K
    xs = _sc_scatter_rows(xn, dest, P)
    y_buf = _experts(xs, plan, w_gu[0], b_gu[0], w_down[0], b_down[0])
    ya = _sc_gather_rows(y_buf, dest.reshape(A)).reshape(TOP_K, T, D // 2)
    out = _combine(ya, x1, gates, norm_final)
    return out.reshape(B, S, D)
```

```python
import functools
import math

import numpy as np
import jax
import jax.numpy as jnp
from jax import lax
from jax.experimental import pallas as pl
from jax.experimental.pallas import tpu as pltpu
from jax.experimental.pallas import tpu_sc as plsc

A_HEADS = 8
A_HEAD_DIM = 64
IDX_HEADS = 8
IDX_DIM = 32
TOPK_MAX = 256
REL_BUCKETS = 32
REL_MAX_DIST = 128
B_HEADS = 4
B_KEY_DIM = 128
B_VAL_DIM = 128
N_EXPERTS = 32
TOP_K = 4
SWIGLU_LIMIT = 7.0
SWIGLU_ALPHA = 1.702
EPS = 1e-6
LOG2_E = math.log2(math.e)

A_WIDTH = A_HEADS * A_HEAD_DIM
B_WIDTH = B_HEADS * B_VAL_DIM
IDX_WIDTH = IDX_HEADS * IDX_DIM

V7X_LANES = 128
V7X_SUBLANES = 8
V7X_VMEM_LIMIT_BYTES = 56 * 1024 * 1024
V7X_SC_CORES = 2
V7X_SC_SUBCORES = 16

PROJ_ROWS = 512
ATT_Q = 256
ATT_KC = 128
SCORE_UNROLL = 4
COUNT_UNROLL = 2
FAR_UNROLL = 4
HGRN_ROWS = 1024
HGRN_CHUNK = 64
HGRN_GROUP = 4
HGRN_SAFE_DECAY = 70.0
ROUTE_ROWS = 512
EXPERT_ROWS = 512
COMBINE_ROWS = 512
SC_GATHER_ROWS = 64
MASK_NEG = -1e30
BISECT_FAST_ITERS = 26


def _cparams(dims):
    return pltpu.CompilerParams(dimension_semantics=dims, vmem_limit_bytes=V7X_VMEM_LIMIT_BYTES)


def _rms(x, gain):
    return x * lax.rsqrt(jnp.mean(x * x, axis=-1, keepdims=True) + EPS) * gain


def _sigmoid(x):
    return 1.0 / (1.0 + jnp.exp(-x))


def _pack_bf16_pairs(x):
    n = x.shape[1] // 2
    as_bits = lambda v: lax.bitcast_convert_type(v.astype(jnp.bfloat16).astype(jnp.float32), jnp.uint32)
    return (as_bits(x[:, :n]) & jnp.uint32(0xFFFF0000)) | (as_bits(x[:, n:]) >> 16)


def _unpack_bf16_pairs(w):
    hi = lax.bitcast_convert_type(w & jnp.uint32(0xFFFF0000), jnp.float32).astype(jnp.bfloat16)
    lo = lax.bitcast_convert_type(w << 16, jnp.float32).astype(jnp.bfloat16)
    return hi, lo


def _blocked_loop(n, step, carry, unroll):
    def run(first, count, cr):
        for t in range(count):
            cr = step(first + t, cr)
        return cr

    carry = lax.fori_loop(0, n // unroll, lambda j, cr: run(j * unroll, unroll, cr), carry)
    base = (n // unroll) * unroll
    piece = unroll // 2
    while piece >= 1:
        take = (n & piece) != 0
        carry = lax.cond(take, functools.partial(run, base, piece), lambda cr: cr, carry)
        base = base + jnp.where(take, piece, 0)
        piece //= 2
    return carry


def _fold_rows(x, op):
    return op(x.reshape(x.shape[0] // V7X_SUBLANES, V7X_SUBLANES, x.shape[1]), axis=0)


def _inproj_kernel(x_ref, g_ref, wk_ref, wik_ref, wqT_ref, wvT_ref, wiqT_ref, wiwT_ref, wb_ref, wg_ref,
                   k_ref, ik_ref, qT_ref, vT_ref, iqT_ref, iwT_ref, bq_ref, bf_ref, bi_ref, bg_ref,
                   ga_ref, gb_ref):
    x = x_ref[...]
    hn = _rms(x, g_ref[...]).astype(jnp.bfloat16)

    def mm(w_ref):
        return jnp.dot(hn, w_ref[...], preferred_element_type=jnp.float32)

    def mm_t(w_ref):
        return lax.dot_general(w_ref[...], hn, (((1,), (1,)), ((), ())),
                               preferred_element_type=jnp.float32)

    k_ref[...] = mm(wk_ref).astype(jnp.bfloat16)
    ik_ref[...] = mm(wik_ref).astype(jnp.bfloat16)
    qT_ref[0] = (mm_t(wqT_ref) * (A_HEAD_DIM ** -0.5 * LOG2_E)).astype(jnp.bfloat16)
    vT_ref[0] = mm_t(wvT_ref).astype(jnp.bfloat16)
    iqT_ref[0] = mm_t(wiqT_ref).astype(jnp.bfloat16)
    iwT_ref[0] = mm_t(wiwT_ref) * ((IDX_HEADS * IDX_DIM) ** -0.5)
    hb = mm(wb_ref)
    bq_ref[...] = hb[:, 0 * B_WIDTH:1 * B_WIDTH]
    bf_ref[...] = hb[:, 1 * B_WIDTH:2 * B_WIDTH]
    bi_ref[...] = hb[:, 2 * B_WIDTH:3 * B_WIDTH]
    bg_ref[...] = hb[:, 3 * B_WIDTH:4 * B_WIDTH]
    d = ga_ref.shape[-1]
    hg = mm(wg_ref)
    ga_ref[...] = _sigmoid(hg[:, :d]).astype(jnp.bfloat16)
    gb_ref[...] = _sigmoid(hg[:, d:]).astype(jnp.bfloat16)


def _inproj(x2, gain, w_in, B, S):
    T, D = x2.shape
    R = min(PROJ_ROWS, S)
    nS = S // R
    o = np.cumsum((0, A_WIDTH, A_WIDTH, A_WIDTH, IDX_WIDTH, IDX_HEADS, IDX_DIM,
                   B_WIDTH, B_WIDTH, B_WIDTH, B_WIDTH, D, D))
    bf = jnp.bfloat16
    wqT = w_in[:, o[0]:o[1]].T.astype(bf)
    wk = w_in[:, o[1]:o[2]].astype(bf)
    wvT = w_in[:, o[2]:o[3]].T.astype(bf)
    wiqT = w_in[:, o[3]:o[4]].T.astype(bf)
    wiwT = w_in[:, o[4]:o[5]].T.astype(bf)
    wik = w_in[:, o[5]:o[6]].astype(bf)
    wb = w_in[:, o[6]:o[10]].astype(bf)
    wg = w_in[:, o[10]:o[12]].astype(bf)

    def full(a):
        return pl.BlockSpec(a.shape, lambda b, i: (0,) * a.ndim)

    row = lambda n: pl.BlockSpec((R, n), lambda b, i: (b * nS + i, 0))
    colT = lambda n: pl.BlockSpec((1, n, R), lambda b, i: (b, 0, i))
    f32 = jnp.float32
    outs = [
        (jax.ShapeDtypeStruct((T, A_WIDTH), bf), row(A_WIDTH)),
        (jax.ShapeDtypeStruct((T, IDX_DIM), bf), row(IDX_DIM)),
        (jax.ShapeDtypeStruct((B, A_WIDTH, S), bf), colT(A_WIDTH)),
        (jax.ShapeDtypeStruct((B, A_WIDTH, S), bf), colT(A_WIDTH)),
        (jax.ShapeDtypeStruct((B, IDX_WIDTH, S), bf), colT(IDX_WIDTH)),
        (jax.ShapeDtypeStruct((B, IDX_HEADS, S), f32), colT(IDX_HEADS)),
        (jax.ShapeDtypeStruct((T, B_WIDTH), f32), row(B_WIDTH)),
        (jax.ShapeDtypeStruct((T, B_WIDTH), f32), row(B_WIDTH)),
        (jax.ShapeDtypeStruct((T, B_WIDTH), f32), row(B_WIDTH)),
        (jax.ShapeDtypeStruct((T, B_WIDTH), f32), row(B_WIDTH)),
        (jax.ShapeDtypeStruct((T, D), bf), row(D)),
        (jax.ShapeDtypeStruct((T, D), bf), row(D)),
    ]
    ins = [x2, gain.reshape(1, D), wk, wik, wqT, wvT, wiqT, wiwT, wb, wg]
    in_specs = [row(D)] + [full(a) for a in ins[1:]]
    return pl.pallas_call(
        _inproj_kernel,
        grid=(B, nS),
        in_specs=in_specs,
        out_specs=[s for _, s in outs],
        out_shape=[s for s, _ in outs],
        compiler_params=_cparams(("parallel", "parallel")),
        name="inproj",
    )(*ins)


def _t5_bucket_table(n):
    d = np.arange(n)
    max_exact = REL_BUCKETS // 2
    nf = np.maximum(d, 1).astype(np.float64)
    large = max_exact + (np.log(nf / max_exact) / math.log(REL_MAX_DIST / max_exact)
                         * (REL_BUCKETS - max_exact)).astype(np.int32)
    large = np.minimum(large, REL_BUCKETS - 1)
    return np.where(d < max_exact, d, large)


def _dsa_kernel(qT_ref, k_ref, vT_ref, iqT_ref, iwT_ref, ik_ref, enear_ref,
                o_ref, sc_ref, qh_scr, m_scr, acc_scr, *, topk):
    TQ = qT_ref.shape[2]
    KC = TQ
    i = pl.program_id(1)
    nch = i + 1
    q0 = i * TQ
    f32 = jnp.float32
    bf16 = jnp.bfloat16
    key_id = lax.broadcasted_iota(jnp.int32, (KC, TQ), 0)
    qry_id = lax.broadcasted_iota(jnp.int32, (KC, TQ), 1)

    def col_reduce(x, op):
        return op(_fold_rows(x, op), axis=0, keepdims=True)

    iw = iwT_ref[0]

    def score_chunk(c, carry):
        rmin, rmax = carry
        k0 = pl.multiple_of(c * KC, KC)
        ik = ik_ref[pl.ds(k0, KC), :]
        acc = jnp.zeros((KC, TQ), f32)
        for h in range(IDX_HEADS):
            sh = jnp.dot(ik, iqT_ref[0, h * IDX_DIM:(h + 1) * IDX_DIM, :], preferred_element_type=f32)
            acc = acc + jnp.maximum(sh, 0.0) * iw[h:h + 1, :]
        valid = (k0 + key_id) <= (q0 + qry_id)
        sc_ref[pl.ds(k0, KC), :] = jnp.where(valid, acc, MASK_NEG)
        rmin = jnp.minimum(rmin, _fold_rows(jnp.where(valid, acc, -MASK_NEG), jnp.min))
        rmax = jnp.maximum(rmax, _fold_rows(jnp.where(valid, acc, MASK_NEG), jnp.max))
        return rmin, rmax

    rmin8, rmax8 = _blocked_loop(
        nch, score_chunk,
        (jnp.full((V7X_SUBLANES, TQ), -MASK_NEG, f32), jnp.full((V7X_SUBLANES, TQ), MASK_NEG, f32)),
        SCORE_UNROLL)
    rmin = jnp.min(rmin8, axis=0, keepdims=True)
    rmax = jnp.max(rmax8, axis=0, keepdims=True)

    def count_where(pred_fn):
        def one(c, acc):
            k0 = pl.multiple_of(c * KC, KC)
            blk = sc_ref[pl.ds(k0, KC), :]
            return acc + _fold_rows(jnp.where(pred_fn(blk), 1.0, 0.0), jnp.sum)
        acc = _blocked_loop(nch, one, jnp.zeros((V7X_SUBLANES, TQ), f32), COUNT_UNROLL)
        return jnp.sum(acc, axis=0, keepdims=True)

    def band_min_max(lo, hi):
        def body(c, carry):
            bmin, bmax = carry
            k0 = pl.multiple_of(c * KC, KC)
            blk = sc_ref[pl.ds(k0, KC), :]
            bmin = jnp.minimum(bmin, _fold_rows(jnp.where(blk >= lo, blk, -MASK_NEG), jnp.min))
            bmax = jnp.maximum(bmax, _fold_rows(jnp.where(blk < hi, blk, MASK_NEG), jnp.max))
            return bmin, bmax
        bmin8, bmax8 = lax.fori_loop(
            0, nch, body,
            (jnp.full((V7X_SUBLANES, TQ), -MASK_NEG, f32), jnp.full((V7X_SUBLANES, TQ), MASK_NEG, f32)))
        return jnp.min(bmin8, axis=0, keepdims=True), jnp.max(bmax8, axis=0, keepdims=True)

    kf = float(topk)
    n_valid = (q0 + 1 + lax.broadcasted_iota(jnp.int32, (1, TQ), 1)).astype(f32)
    lo0 = rmin
    cnt0 = n_valid
    hi0 = rmax + jnp.maximum(jnp.abs(rmax) * 2.0 ** -20, 1e-30)
    done0 = jnp.where(cnt0 <= kf, 1.0, 0.0)

    def probe(st, lo_s, mid, tie):
        it, lo, hi, cnt, done = st
        active = done < 0.5
        lo_s = jnp.where(active, lo_s, lo)
        c = count_where(lambda blk: blk >= mid)
        feas = c >= kf
        move = active & jnp.logical_not(tie)
        lo_n = jnp.where(move & feas, mid, lo_s)
        cnt_n = jnp.where(move & feas, c, cnt)
        hi_n = jnp.where(move & jnp.logical_not(feas), mid, hi)
        done_n = jnp.where((active & tie) | (cnt_n <= kf), 1.0, done)
        return it + 1, lo_n, hi_n, cnt_n, done_n

    def halve(st):
        _, lo, hi, _, _ = st
        half = lo + 0.5 * (hi - lo)
        stuck = (half <= lo) | (half >= hi)
        return probe(st, lo, half, stuck)

    def snap(st):
        _, lo, hi, _, _ = st
        bmin, bmax = band_min_max(lo, hi)
        mid = bmin + 0.5 * (bmax - bmin)
        return probe(st, bmin, jnp.where(mid <= bmin, bmax, mid), bmax <= bmin)

    st = lax.while_loop(lambda st: (jnp.min(st[-1]) < 0.5) & (st[0] < BISECT_FAST_ITERS),
                        lambda st: halve(halve(st)), (jnp.int32(0), lo0, hi0, cnt0, done0))
    _, thr, _, cnt_thr, _ = lax.while_loop(lambda st: jnp.min(st[-1]) < 0.5, snap, st)

    tie_overflow = jnp.max(cnt_thr) > kf

    @pl.when(jnp.logical_not(tie_overflow))
    def _():
        def mask_chunk(c, _):
            k0 = pl.multiple_of(c * KC, KC)
            sc_ref[pl.ds(k0, KC), :] = jnp.where(sc_ref[pl.ds(k0, KC), :] >= thr, 0.0, MASK_NEG)
            return 0
        lax.fori_loop(0, nch, mask_chunk, 0)

    @pl.when(tie_overflow)
    def _():
        need = kf - count_where(lambda blk: blk > thr)
        tril = jnp.where(lax.broadcasted_iota(jnp.int32, (KC, KC), 1)
                         <= lax.broadcasted_iota(jnp.int32, (KC, KC), 0), 1.0, 0.0).astype(bf16)

        def mask_chunk(c, run):
            k0 = pl.multiple_of(c * KC, KC)
            blk = sc_ref[pl.ds(k0, KC), :]
            eq = jnp.where(blk == thr, 1.0, 0.0)
            pref = jnp.dot(tril, eq.astype(bf16), preferred_element_type=f32)
            sel = (blk > thr) | ((eq > 0.5) & (run + pref <= need))
            sc_ref[pl.ds(k0, KC), :] = jnp.where(sel, 0.0, MASK_NEG)
            return run + pref[KC - 1:KC, :]

        lax.fori_loop(0, nch, mask_chunk, jnp.zeros((1, TQ), f32))

    AK = min(ATT_KC, TQ)
    per = TQ // AK
    head0_q = (lax.broadcasted_iota(jnp.int32, (V7X_LANES, TQ), 0) // A_HEAD_DIM) == 0
    n_pairs = A_HEADS // 2

    m_scr[...] = jnp.full(m_scr.shape, MASK_NEG, f32)
    acc_scr[...] = jnp.zeros(acc_scr.shape, f32)
    v_row = lax.broadcasted_iota(jnp.int32, (V7X_LANES, AK), 0)
    denom_row = [A_HEAD_DIM * (1 - sub) for sub in range(2)]
    for p in range(n_pairs):
        q_pair = qT_ref[0, p * V7X_LANES:(p + 1) * V7X_LANES, :]
        zq = jnp.zeros_like(q_pair)
        qh_scr[2 * p] = jnp.where(head0_q, q_pair, zq)
        qh_scr[2 * p + 1] = jnp.where(head0_q, zq, q_pair)

    def step(c, bias_rows):
        k0 = pl.multiple_of(c * AK, AK)
        msk = sc_ref[pl.ds(k0, AK), :]
        for p in range(n_pairs):
            kp = k_ref[pl.ds(k0, AK), p * V7X_LANES:(p + 1) * V7X_LANES]
            vp = vT_ref[0, p * V7X_LANES:(p + 1) * V7X_LANES, pl.ds(k0, AK)]
            for sub in range(2):
                h = 2 * p + sub
                s = jnp.dot(kp, qh_scr[h], preferred_element_type=f32) + msk
                if bias_rows is not None:
                    s = s + enear_ref[h, bias_rows, :]
                m = m_scr[h:h + 1, :]
                m_new = jnp.maximum(m, col_reduce(s, jnp.max))
                alpha = jnp.exp2(m - m_new)
                pr = jnp.exp2(s - m_new)
                m_scr[h:h + 1, :] = m_new
                v_aug = jnp.where(v_row == denom_row[sub], jnp.ones_like(vp), vp)
                acc_scr[h] = alpha * acc_scr[h] + jnp.dot(v_aug, pr.astype(bf16), preferred_element_type=f32)

    def far(blk, _):
        for jj in range(per):
            step(blk * per + jj, None)
        return 0

    def near(block, first_chunk):
        for jj in range(per):
            step(first_chunk + jj, slice(block * TQ + jj * AK, block * TQ + (jj + 1) * AK))

    _blocked_loop(jnp.maximum(i - 1, 0), far, 0, FAR_UNROLL)

    @pl.when(i >= 1)
    def _():
        near(0, (i - 1) * per)

    near(1, i * per)
    for p in range(n_pairs):
        outs = [acc_scr[2 * p + sub] / acc_scr[2 * p + sub, denom_row[sub]:denom_row[sub] + 1, :]
                for sub in range(2)]
        o_pair = jnp.where(head0_q, outs[0], outs[1])
        o_ref[:, p * V7X_LANES:(p + 1) * V7X_LANES] = o_pair.T.astype(o_ref.dtype)


def _dsa(k, ik, qT, vT, iqT, iwT, rel_bias, B, S):
    T = k.shape[0]
    TQ = min(ATT_Q, S)
    nQ = S // TQ
    topk = min(TOPK_MAX, S // 4)
    buckets = _t5_bucket_table(2 * TQ + 1)
    assert np.all(_t5_bucket_table(S + 1)[TQ + 1:] == REL_BUCKETS - 1)
    j = np.arange(2 * TQ)[:, None]
    r = np.arange(TQ)[None, :]
    dist = np.maximum(r + TQ - j, 0)
    onehot = (jnp.asarray(buckets[dist], jnp.int32)[None]
              == jnp.arange(REL_BUCKETS, dtype=jnp.int32)[:, None, None]).astype(jnp.float32)
    rel = (rel_bias.astype(jnp.float32) - rel_bias[REL_BUCKETS - 1].astype(jnp.float32)[None, :]) * LOG2_E
    enear = jnp.einsum('nh,njr->hjr', rel, onehot, precision=lax.Precision.HIGHEST)

    return pl.pallas_call(
        functools.partial(_dsa_kernel, topk=topk),
        grid=(B, nQ),
        in_specs=[
            pl.BlockSpec((1, A_WIDTH, TQ), lambda b, i: (b, 0, i)),
            pl.BlockSpec((S, A_WIDTH), lambda b, i: (b, 0)),
            pl.BlockSpec((1, A_WIDTH, S), lambda b, i: (b, 0, 0)),
            pl.BlockSpec((1, IDX_WIDTH, TQ), lambda b, i: (b, 0, i)),
            pl.BlockSpec((1, IDX_HEADS, TQ), lambda b, i: (b, 0, i)),
            pl.BlockSpec((S, IDX_DIM), lambda b, i: (b, 0)),
            pl.BlockSpec((A_HEADS, 2 * TQ, TQ), lambda b, i: (0, 0, 0)),
        ],
        out_specs=pl.BlockSpec((TQ, A_WIDTH), lambda b, i: (b * nQ + i, 0)),
        scratch_shapes=[pltpu.VMEM((S, TQ), jnp.float32),
                        pltpu.VMEM((A_HEADS, V7X_LANES, TQ), jnp.bfloat16),
                        pltpu.VMEM((A_HEADS, TQ), jnp.float32),
                        pltpu.VMEM((A_HEADS, V7X_LANES, TQ), jnp.float32)],
        out_shape=jax.ShapeDtypeStruct((T, A_WIDTH), jnp.bfloat16),
        compiler_params=_cparams(("parallel", "arbitrary")),
        name="dsa",
    )(qT, k, vT, iqT, iwT, ik, enear)


def _hgrn_kernel(bq_ref, bf_ref, bi_ref, bg_ref, lb_ref, gain_ref, o_ref,
                 st_ref, b_scr, q_scr, k_scr, v_scr, oi_scr, qd_s, kl_s, vv_s, dec_s, oi_s, upd_s, st_s,
                 kdT_s, vvT_s):
    R = bq_ref.shape[0]
    C = HGRN_CHUNK
    nC = R // C
    f32 = jnp.float32
    bf16 = jnp.bfloat16
    h = pl.program_id(1)

    @pl.when(pl.program_id(2) == 0)
    def _():
        st_ref[...] = jnp.zeros_like(st_ref)

    lb = lb_ref[pl.ds(h, 1), :]
    gain = gain_ref[pl.ds(h, 1), :]
    tril_incl = jnp.where(lax.broadcasted_iota(jnp.int32, (C, C), 1)
                          <= lax.broadcasted_iota(jnp.int32, (C, C), 0), 1.0, 0.0)
    srow = lax.broadcasted_iota(jnp.int32, (C, B_KEY_DIM), 0)

    def gates(r0):
        f = lb + (1.0 - lb) * _sigmoid(bf_ref[pl.ds(r0, C), :])
        qr = bq_ref[pl.ds(r0, C), :]
        return jnp.log(f), 1.0 - f, qr * _sigmoid(qr) * (B_KEY_DIM ** -0.5), bi_ref[pl.ds(r0, C), :]

    def cumdecay(g):
        tri = tril_incl.astype(bf16)
        g_hi = g.astype(bf16)
        rest = g - g_hi.astype(f32)
        g_mid = rest.astype(bf16)
        g_lo = (rest - g_mid.astype(f32)).astype(bf16)
        return (jnp.dot(tri, g_hi, preferred_element_type=f32) + jnp.dot(tri, g_mid, preferred_element_type=f32)
                + jnp.dot(tri, g_lo, preferred_element_type=f32))

    def advance(r0, st, qd, o_intra, upd, decay_row):
        o_inter = lax.dot_general(qd, st.astype(bf16), (((1,), (1,)), ((), ())), preferred_element_type=f32)
        og = bg_ref[pl.ds(r0, C), :]
        y = _rms(o_inter + o_intra, gain) * (og * _sigmoid(og))
        o_ref[pl.ds(r0, C), :] = y.astype(o_ref.dtype)
        return st * decay_row + upd

    f_all = lb + (1.0 - lb) * _sigmoid(bf_ref[...])
    g_all = jnp.log(f_all)
    decay = jnp.sum(g_all.reshape(nC, C, B_KEY_DIM), axis=1)
    safe = jnp.min(decay) >= -HGRN_SAFE_DECAY

    @pl.when(safe)
    def _():
        qr = bq_ref[...]
        qq = qr * _sigmoid(qr) * (B_KEY_DIM ** -0.5)
        kk = 1.0 - f_all
        GR = HGRN_GROUP * C
        r_id = lax.broadcasted_iota(jnp.int32, (GR, GR), 0)
        c_id = lax.broadcasted_iota(jnp.int32, (GR, GR), 1)
        tri_group = jnp.where((r_id // C == c_id // C) & (c_id <= r_id), 1.0, 0.0)
        tri_group_bf = tri_group.astype(bf16)
        g_hi = g_all.astype(bf16)
        rest = g_all - g_hi.astype(f32)
        g_mid = rest.astype(bf16)
        g_lo = (rest - g_mid.astype(f32)).astype(bf16)
        g_cat = jnp.concatenate([g_hi, g_mid, g_lo], axis=1)
        b_parts = []
        for gi in range(R // GR):
            bc = jnp.dot(tri_group_bf, g_cat[gi * GR:(gi + 1) * GR], preferred_element_type=f32)
            b_parts.append(bc[:, :B_KEY_DIM] + bc[:, B_KEY_DIM:2 * B_KEY_DIM] + bc[:, 2 * B_KEY_DIM:])
        b = jnp.concatenate(b_parts, axis=0)
        b_end = jnp.concatenate([jnp.broadcast_to(b[(c + 1) * C - 1:(c + 1) * C], (C, B_KEY_DIM))
                                 for c in range(nC)], axis=0)
        qd_s[...] = (qq * jnp.exp(b)).astype(bf16)
        kd = kk * jnp.exp(-b)
        kl_s[...] = (kk * jnp.exp(b_end - b)).astype(bf16)
        vv = bi_ref[...]
        vv_s[...] = vv.astype(bf16)
        dec_s[...] = jnp.exp(b_end)
        for gi in range(R // GR):
            kdT_s[gi] = kd[gi * GR:(gi + 1) * GR].T.astype(bf16)
        for c in range(nC):
            vvT_s[c] = vv[c * C:(c + 1) * C].T.astype(bf16)
        for gi in range(R // GR):
            rows = slice(gi * GR, (gi + 1) * GR)
            att = jnp.dot(qd_s[rows], kdT_s[gi], preferred_element_type=f32) * tri_group
            oi_s[rows] = jnp.dot(att.astype(bf16), vv_s[rows], preferred_element_type=f32)
        for c in range(nC):
            rows = slice(c * C, (c + 1) * C)
            upd_s[c] = jnp.dot(vvT_s[c], kl_s[rows], preferred_element_type=f32)
        st = st_ref[...]
        for c in range(nC):
            st_s[c] = st.T.astype(bf16)
            st = st * dec_s[c * C:c * C + 1] + upd_s[c]
        st_ref[...] = st
        for c in range(nC):
            rows = slice(c * C, (c + 1) * C)
            oi_s[rows] = oi_s[rows] + jnp.dot(qd_s[rows], st_s[c], preferred_element_type=f32)
        og = bg_ref[...]
        o_ref[...] = (_rms(oi_s[...], gain) * (og * _sigmoid(og))).astype(o_ref.dtype)

    @pl.when(jnp.logical_not(safe))
    def _():
        def body(c, st):
            r0 = pl.multiple_of(c * C, C)
            g, kk, qq, vv = gates(r0)
            b = cumdecay(g)
            b_last = b[C - 1:C, :]
            b_scr[...] = b
            q_scr[...] = qq
            k_scr[...] = kk
            v_scr[...] = vv

            def row(t, _):
                bt = b_scr[pl.ds(t, 1), :]
                qt = q_scr[pl.ds(t, 1), :]
                ex = jnp.where(srow <= t, bt - b_scr[...], -jnp.inf)
                a = jnp.sum(qt * k_scr[...] * jnp.exp(ex), axis=1, keepdims=True)
                oi_scr[pl.ds(t, 1), :] = jnp.sum(a * v_scr[...], axis=0, keepdims=True)
                return 0
            lax.fori_loop(0, C, row, 0)
            kd_last = (kk * jnp.exp(b_last - b)).astype(bf16)
            upd = lax.dot_general(vv.astype(bf16), kd_last, (((0,), (0,)), ((), ())),
                                  preferred_element_type=f32)
            return advance(r0, st, (qq * jnp.exp(b)).astype(bf16), oi_scr[...], upd, jnp.exp(b_last))
        st_ref[...] = lax.fori_loop(0, nC, body, st_ref[...])


def _hgrn(bq, bf, bi, bg, lb, gain, B, S):
    T = bq.shape[0]
    R = min(HGRN_ROWS, S)
    nR = S // R
    C = HGRN_CHUNK
    blk = pl.BlockSpec((R, B_KEY_DIM), lambda b, h, c: (b * nR + c, h))
    small = pl.BlockSpec((B_HEADS, B_KEY_DIM), lambda b, h, c: (0, 0))
    f32 = jnp.float32
    return pl.pallas_call(
        _hgrn_kernel,
        grid=(B, B_HEADS, nR),
        in_specs=[blk, blk, blk, blk, small, small],
        out_specs=blk,
        out_shape=jax.ShapeDtypeStruct((T, B_WIDTH), jnp.bfloat16),
        scratch_shapes=[pltpu.VMEM((B_VAL_DIM, B_KEY_DIM), f32)] +
                       [pltpu.VMEM((C, B_KEY_DIM), f32) for _ in range(5)] +
                       [pltpu.VMEM((R, B_KEY_DIM), jnp.bfloat16) for _ in range(3)] +
                       [pltpu.VMEM((R, B_KEY_DIM), f32) for _ in range(2)] +
                       [pltpu.VMEM((R // C, B_VAL_DIM, B_KEY_DIM), f32),
                        pltpu.VMEM((R // C, B_KEY_DIM, B_VAL_DIM), jnp.bfloat16),
                        pltpu.VMEM((R // (HGRN_GROUP * C), B_KEY_DIM, HGRN_GROUP * C), jnp.bfloat16),
                        pltpu.VMEM((R // C, B_VAL_DIM, C), jnp.bfloat16)],
        compiler_params=_cparams(("parallel", "parallel", "arbitrary")),
        name="hgrn",
    )(bq, bf, bi, bg, lb, gain)


def _merge_kernel(x_ref, ya_ref, yb_ref, ga_ref, gb_ref, wa_ref, wb_ref, wo_ref, g_ref, wrh_ref, wrl_ref, br_ref,
                  x1_ref, xn_ref, lg_ref):
    f32 = jnp.float32
    ma = jnp.dot(ya_ref[...], wa_ref[...], preferred_element_type=f32)
    mb = jnp.dot(yb_ref[...], wb_ref[...], preferred_element_type=f32)
    merged = ga_ref[...].astype(f32) * ma + gb_ref[...].astype(f32) * mb
    x1 = x_ref[...] + jnp.dot(merged.astype(jnp.bfloat16), wo_ref[...], preferred_element_type=f32)
    x1_ref[...] = x1
    hn = _rms(x1, g_ref[...])
    xn_ref[...] = _pack_bf16_pairs(hn)
    hn_hi = hn.astype(jnp.bfloat16)
    hn_lo = (hn - hn_hi.astype(f32)).astype(jnp.bfloat16)
    lg_ref[...] = (jnp.dot(hn_hi, wrh_ref[...], preferred_element_type=f32)
                   + jnp.dot(hn_lo, wrh_ref[...], preferred_element_type=f32)
                   + jnp.dot(hn_hi, wrl_ref[...], preferred_element_type=f32) + br_ref[...])


def _merge(x2, ya, yb, ga, gb, w_up_a, w_up_b, w_out, gain, w_router, b_router):
    T, D = x2.shape
    R = min(PROJ_ROWS, T)
    bf = jnp.bfloat16
    wr_hi = w_router.astype(bf)
    wr_lo = (w_router - wr_hi.astype(jnp.float32)).astype(bf)
    ins = [x2, ya, yb, ga, gb, w_up_a.astype(bf), w_up_b.astype(bf), w_out.astype(bf),
           gain.reshape(1, D), wr_hi, wr_lo, b_router.reshape(1, N_EXPERTS)]
    row = lambda n: pl.BlockSpec((R, n), lambda i: (i, 0))
    full = lambda a: pl.BlockSpec(a.shape, lambda i: (0,) * a.ndim)
    in_specs = [row(D), row(A_WIDTH), row(B_WIDTH), row(D), row(D)] + [full(a) for a in ins[5:]]
    return pl.pallas_call(
        _merge_kernel,
        grid=(T // R,),
        in_specs=in_specs,
        out_specs=[row(D), row(D // 2), row(N_EXPERTS)],
        out_shape=[jax.ShapeDtypeStruct((T, D), jnp.float32), jax.ShapeDtypeStruct((T, D // 2), jnp.uint32),
                   jax.ShapeDtypeStruct((T, N_EXPERTS), jnp.float32)],
        compiler_params=_cparams(("parallel",)),
        name="merge",
    )(*ins)


def _route_kernel(lg_ref, eidx_ref, gate_ref, rank_ref, cnt_ref, run_ref):
    R = lg_ref.shape[0]
    f32 = jnp.float32

    @pl.when(pl.program_id(0) == 0)
    def _():
        run_ref[...] = jnp.zeros_like(run_ref)

    lg = lg_ref[...].T
    expert = lax.broadcasted_iota(jnp.int32, (N_EXPERTS, R), 0)
    work = lg
    onehots, vals, idxs = [], [], []
    for _ in range(TOP_K):
        m = jnp.max(work, axis=0, keepdims=True)
        idx = jnp.min(jnp.where(work == m, expert, N_EXPERTS), axis=0, keepdims=True)
        oh = expert == idx
        onehots.append(oh)
        vals.append(m)
        idxs.append(idx)
        work = jnp.where(oh, -jnp.inf, work)
    ex = [jnp.exp(v - vals[0]) for v in vals]
    den = ex[0] + ex[1] + ex[2] + ex[3]
    chosen = jnp.where(onehots[0] | onehots[1] | onehots[2] | onehots[3], 1.0, 0.0)
    earlier = jnp.where(lax.broadcasted_iota(jnp.int32, (R, R), 0)
                        < lax.broadcasted_iota(jnp.int32, (R, R), 1), 1.0, 0.0).astype(jnp.bfloat16)
    before = jnp.dot(chosen.astype(jnp.bfloat16), earlier, preferred_element_type=f32) + run_ref[...]
    slot = lax.broadcasted_iota(jnp.int32, (TOP_K, R), 0)
    eidx = jnp.zeros((TOP_K, R), jnp.int32)
    gate = jnp.zeros((TOP_K, R), f32)
    rank = jnp.zeros((TOP_K, R), f32)
    for k in range(TOP_K):
        eidx = jnp.where(slot == k, idxs[k], eidx)
        gate = jnp.where(slot == k, ex[k] / den, gate)
        rk = jnp.sum(jnp.where(onehots[k], before, 0.0), axis=0, keepdims=True)
        rank = jnp.where(slot == k, rk, rank)
    eidx_ref[...] = eidx
    gate_ref[...] = gate
    rank_ref[...] = rank.astype(jnp.int32)
    run_ref[...] = run_ref[...] + jnp.sum(chosen, axis=1, keepdims=True)
    cnt_ref[...] = run_ref[...].astype(jnp.int32)


def _route(logits):
    T = logits.shape[0]
    R = min(ROUTE_ROWS, T)
    col = pl.BlockSpec((TOP_K, R), lambda i: (0, i))
    return pl.pallas_call(
        _route_kernel,
        grid=(T // R,),
        in_specs=[pl.BlockSpec((R, N_EXPERTS), lambda i: (i, 0))],
        out_specs=[col, col, col, pl.BlockSpec((N_EXPERTS, 1), lambda i: (0, 0))],
        out_shape=[jax.ShapeDtypeStruct((TOP_K, T), jnp.int32), jax.ShapeDtypeStruct((TOP_K, T), jnp.float32),
                   jax.ShapeDtypeStruct((TOP_K, T), jnp.int32), jax.ShapeDtypeStruct((N_EXPERTS, 1), jnp.int32)],
        scratch_shapes=[pltpu.VMEM((N_EXPERTS, 1), jnp.float32)],
        compiler_params=_cparams(("arbitrary",)),
        name="route",
    )(logits)


def _sc_gather_rows(table, idx):
    N, W = table.shape
    M = idx.shape[0]
    workers = V7X_SC_CORES * V7X_SC_SUBCORES
    per_worker = M // workers
    pieces = per_worker // SC_GATHER_ROWS
    assert per_worker * workers == M and pieces * SC_GATHER_ROWS == per_worker and pieces % 2 == 0
    mesh = plsc.VectorSubcoreMesh(core_axis_name="core", subcore_axis_name="subcore",
                                  num_cores=V7X_SC_CORES, num_subcores=V7X_SC_SUBCORES)

    @functools.partial(
        pl.kernel, mesh=mesh,
        out_type=jax.ShapeDtypeStruct((M, W), table.dtype),
        scratch_types=[pltpu.VMEM((per_worker,), jnp.int32),
                       pltpu.VMEM((SC_GATHER_ROWS, W), table.dtype),
                       pltpu.VMEM((SC_GATHER_ROWS, W), table.dtype),
                       pltpu.SemaphoreType.DMA, pltpu.SemaphoreType.DMA],
    )
    def gather(table_hbm, idx_hbm, out_hbm, idx_v, rows_a, rows_b, sem_a, sem_b):
        worker = lax.axis_index("subcore") * V7X_SC_CORES + lax.axis_index("core")
        base = pl.multiple_of(worker * per_worker, SC_GATHER_ROWS)
        pltpu.sync_copy(idx_hbm.at[pl.ds(base, per_worker)], idx_v)
        bufs = ((rows_a, sem_a), (rows_b, sem_b))

        def fetch(g, buf, sem):
            off = pl.multiple_of(g * SC_GATHER_ROWS, SC_GATHER_ROWS)
            return pltpu.make_async_copy(table_hbm.at[idx_v.at[pl.ds(off, SC_GATHER_ROWS)]], buf, sem)

        fetch(0, *bufs[0]).start()

        @pl.loop(0, pieces, step=2)
        def _(g0):
            for half in range(2):
                g = g0 + half
                buf, sem = bufs[half]
                fetch(g, buf, sem).wait()

                @pl.when(g + 1 < pieces)
                def _():
                    fetch(g + 1, *bufs[1 - half]).start()

                off = pl.multiple_of(g * SC_GATHER_ROWS, SC_GATHER_ROWS)
                pltpu.sync_copy(buf, out_hbm.at[pl.ds(base + off, SC_GATHER_ROWS)])

    return gather(table, idx)


def _sc_scatter_rows(rows, dest, n_out):
    T, W = rows.shape
    slots = dest.shape[0]
    workers = V7X_SC_CORES * V7X_SC_SUBCORES
    per_worker = T // workers
    pieces = per_worker // SC_GATHER_ROWS
    assert per_worker * workers == T and pieces * SC_GATHER_ROWS == per_worker and pieces % 2 == 0
    idx = dest.reshape(slots, workers, pieces, SC_GATHER_ROWS).transpose(1, 2, 0, 3)
    mesh = plsc.VectorSubcoreMesh(core_axis_name="core", subcore_axis_name="subcore",
                                  num_cores=V7X_SC_CORES, num_subcores=V7X_SC_SUBCORES)

    @functools.partial(
        pl.kernel, mesh=mesh,
        out_type=jax.ShapeDtypeStruct((n_out, W), rows.dtype),
        scratch_types=[pltpu.VMEM((pieces, slots, SC_GATHER_ROWS), jnp.int32),
                       pltpu.VMEM((SC_GATHER_ROWS, W), rows.dtype),
                       pltpu.VMEM((SC_GATHER_ROWS, W), rows.dtype),
                       pltpu.SemaphoreType.DMA, pltpu.SemaphoreType.DMA, pltpu.SemaphoreType.DMA],
    )
    def scatter(rows_hbm, idx_hbm, out_hbm, idx_v, rows_a, rows_b, sem_a, sem_b, sem_out):
        worker = lax.axis_index("subcore") * V7X_SC_CORES + lax.axis_index("core")
        base = pl.multiple_of(worker * per_worker, SC_GATHER_ROWS)
        pltpu.sync_copy(idx_hbm.at[worker], idx_v)
        bufs = ((rows_a, sem_a), (rows_b, sem_b))

        def fetch(g, buf, sem):
            off = pl.multiple_of(g * SC_GATHER_ROWS, SC_GATHER_ROWS)
            return pltpu.make_async_copy(rows_hbm.at[pl.ds(base + off, SC_GATHER_ROWS)], buf, sem)

        fetch(0, *bufs[0]).start()

        @pl.loop(0, pieces, step=2)
        def _(g0):
            for half in range(2):
                g = g0 + half
                buf, sem = bufs[half]
                fetch(g, buf, sem).wait()

                @pl.when(g + 1 < pieces)
                def _():
                    fetch(g + 1, *bufs[1 - half]).start()

                puts = [pltpu.make_async_copy(buf, out_hbm.at[idx_v.at[g, k]], sem_out) for k in range(slots)]
                for put in puts:
                    put.start()
                for put in puts:
                    put.wait()

    return scatter(rows, idx)


def _experts_kernel(be_ref, nb_ref, first_ref, slot_ref, next_ref,
                    x_ref, wgu_hbm, bgu_ref, wd_hbm, bd_ref, o_ref,
                    wgu_f32, wd_f32, wgu_bf, wd_bf, sems):
    f32 = jnp.float32
    d_ff = wd_bf.shape[0]
    i = pl.program_id(0)
    live = i < nb_ref[0]

    def weight_copies(e, s):
        return (pltpu.make_async_copy(wgu_hbm.at[e], wgu_f32.at[s], sems.at[s, 0]),
                pltpu.make_async_copy(wd_hbm.at[e], wd_f32.at[s], sems.at[s, 1]))

    @pl.when(live & (first_ref[i] == 1))
    def _():
        e = be_ref[i]
        s = slot_ref[i]

        @pl.when(i == 0)
        def _():
            for cp in weight_copies(e, s):
                cp.start()

        for cp in weight_copies(e, s):
            cp.wait()

        @pl.when(next_ref[i] >= 0)
        def _():
            for cp in weight_copies(next_ref[i], 1 - s):
                cp.start()

        wgu_bf[...] = wgu_f32[s].astype(jnp.bfloat16)
        wd_bf[...] = wd_f32[s].astype(jnp.bfloat16)

    @pl.when(live)
    def _():
        x_hi, x_lo = _unpack_bf16_pairs(x_ref[...])
        half = x_hi.shape[1]
        gu = (jnp.dot(x_hi, wgu_bf[:half, :], preferred_element_type=f32)
              + jnp.dot(x_lo, wgu_bf[half:, :], preferred_element_type=f32) + bgu_ref[0])
        gate = jnp.minimum(gu[:, :d_ff], SWIGLU_LIMIT)
        lin = jnp.clip(gu[:, d_ff:], -SWIGLU_LIMIT, SWIGLU_LIMIT)
        act = (lin + 1.0) * gate * _sigmoid(SWIGLU_ALPHA * gate)
        y = jnp.dot(act.astype(jnp.bfloat16), wd_bf[...], preferred_element_type=f32) + bd_ref[0]
        o_ref[...] = _pack_bf16_pairs(y)

    @pl.when(pl.program_id(0) >= nb_ref[0])
    def _():
        o_ref[...] = jnp.zeros_like(o_ref)


def _experts(xs, plan, w_gu, b_gu, w_down, b_down):
    P, W = xs.shape
    E, D, F2 = w_gu.shape
    nb = P // EXPERT_ROWS
    by_expert = lambda i, be, *_: (be[i], 0, 0)
    grid_spec = pltpu.PrefetchScalarGridSpec(
        num_scalar_prefetch=5,
        grid=(nb,),
        in_specs=[
            pl.BlockSpec((EXPERT_ROWS, W), lambda i, *_: (i, 0)),
            pl.BlockSpec(memory_space=pl.ANY),
            pl.BlockSpec((1, 1, F2), by_expert),
            pl.BlockSpec(memory_space=pl.ANY),
            pl.BlockSpec((1, 1, D), by_expert),
        ],
        out_specs=pl.BlockSpec((EXPERT_ROWS, W), lambda i, *_: (i, 0)),
        scratch_shapes=[pltpu.VMEM((2, D, F2), jnp.float32), pltpu.VMEM((2, F2 // 2, D), jnp.float32),
                        pltpu.VMEM((D, F2), jnp.bfloat16), pltpu.VMEM((F2 // 2, D), jnp.bfloat16),
                        pltpu.SemaphoreType.DMA((2, 2))],
    )
    return pl.pallas_call(
        _experts_kernel,
        grid_spec=grid_spec,
        out_shape=jax.ShapeDtypeStruct((P, W), jnp.uint32),
        compiler_params=_cparams(("arbitrary",)),
        name="experts",
    )(*plan, xs, w_gu, b_gu.reshape(E, 1, F2), w_down, b_down.reshape(E, 1, D))


def _combine_kernel(ya_ref, x1_ref, gate_ref, g_ref, o_ref):
    half = x1_ref.shape[1] // 2
    f32 = jnp.float32
    gate = gate_ref[...].T
    x1 = x1_ref[...]
    y_hi = x1[:, :half]
    y_lo = x1[:, half:]
    for k in range(TOP_K):
        hi, lo = _unpack_bf16_pairs(ya_ref[k])
        y_hi = y_hi + gate[:, k:k + 1] * hi.astype(f32)
        y_lo = y_lo + gate[:, k:k + 1] * lo.astype(f32)
    o_ref[...] = _rms(jnp.concatenate([y_hi, y_lo], axis=1), g_ref[...])


def _combine(ya, x1, gates, gain):
    T, D = x1.shape
    R = min(COMBINE_ROWS, T)
    row = lambda w: pl.BlockSpec((R, w), lambda i: (i, 0))
    return pl.pallas_call(
        _combine_kernel,
        grid=(T // R,),
        in_specs=[pl.BlockSpec((TOP_K, R, D // 2), lambda i: (0, i, 0)), row(D),
                  pl.BlockSpec((TOP_K, R), lambda i: (0, i)), pl.BlockSpec((1, D), lambda i: (0, 0))],
        out_specs=row(D),
        out_shape=jax.ShapeDtypeStruct((T, D), jnp.float32),
        compiler_params=_cparams(("parallel",)),
        name="combine",
    )(ya, x1, gates, gain.reshape(1, D))


def _moe_plan(eidx, rank, counts, A):
    counts = counts.reshape(N_EXPERTS)
    padded = (counts + EXPERT_ROWS - 1) // EXPERT_ROWS * EXPERT_ROWS
    pad_ends = jnp.cumsum(padded)
    pad_starts = pad_ends - padded
    n_blocks = -(-A // EXPERT_ROWS) + N_EXPERTS
    ids = jnp.arange(N_EXPERTS, dtype=jnp.int32)
    dest = rank + jnp.sum(jnp.where(eidx[None] == ids[:, None, None], pad_starts[:, None, None], 0), axis=0)
    block_start = jnp.arange(n_blocks, dtype=pad_ends.dtype) * EXPERT_ROWS
    block_expert = jnp.minimum(jnp.sum(pad_ends[None, :] <= block_start[:, None], axis=1),
                               N_EXPERTS - 1).astype(jnp.int32)
    n_used = (pad_ends[-1] // EXPERT_ROWS).astype(jnp.int32).reshape(1)
    has_rows = counts > 0
    ordinal = jnp.cumsum(has_rows.astype(jnp.int32)) - 1
    later = has_rows[None, :] & (ids[None, :] > ids[:, None])
    next_expert = jnp.where(jnp.any(later, axis=1), jnp.argmax(later, axis=1), -1).astype(jnp.int32)
    block_first = ((block_start == pad_starts[block_expert]) & (block_start < pad_ends[-1])).astype(jnp.int32)
    block_slot = (ordinal[block_expert] % 2).astype(jnp.int32)
    block_next = next_expert[block_expert]
    plan = (block_expert, n_used, block_first, block_slot, block_next)
    return dest.astype(jnp.int32), plan, n_blocks


def kernel(x, w_in, w_up_a, w_up_b, w_out, norm_mix, norm_ffn, norm_final, hgrn_norm,
           lb_logits, rel_bias, w_router, b_router, w_gu, b_gu, w_down, b_down):
    B, S, D = x.shape
    T = B * S
    assert w_in.shape[0] == 1, "the final rmsnorm is fused into the single layer's combine stage"
    lb_all = jnp.cumsum(jax.nn.softmax(lb_logits.astype(jnp.float32), axis=0), axis=0)
    x2 = x.reshape(T, D)
    (k, ik, qT, vT, iqT, iwT, bq, bf, bi, bg, ga, gb) = _inproj(x2, norm_mix[0], w_in[0], B, S)
    ya = _dsa(k, ik, qT, vT, iqT, iwT, rel_bias, B, S)
    yb = _hgrn(bq, bf, bi, bg, lb_all[0].reshape(B_HEADS, B_KEY_DIM), hgrn_norm[0], B, S)
    x1, xn, logits = _merge(x2, ya, yb, ga, gb, w_up_a[0], w_up_b[0], w_out[0], norm_ffn[0],
                            w_router[0], b_router[0])
    eidx, gates, rank, counts = _route(logits)
    dest, plan, n_blocks = _moe_plan(eidx, rank, counts, T * TOP_K)
    P = n_blocks * EXPERT_ROWS
    A = T * TOP_K
    xs = _sc_scatter_rows(xn, dest, P)
    y_buf = _experts(xs, plan, w_gu[0], b_gu[0], w_down[0], b_down[0])
    ya = _sc_gather_rows(y_buf, dest.reshape(A)).reshape(TOP_K, T, D // 2)
    out = _combine(ya, x1, gates, norm_final)
    return out.reshape(B, S, D)
```

```python
import functools
import math

import numpy as np
import jax
import jax.numpy as jnp
from jax import lax
from jax.experimental import pallas as pl
from jax.experimental.pallas import tpu as pltpu
from jax.experimental.pallas import tpu_sc as plsc

A_HEADS = 8
A_HEAD_DIM = 64
IDX_HEADS = 8
IDX_DIM = 32
TOPK_MAX = 256
REL_BUCKETS = 32
REL_MAX_DIST = 128
B_HEADS = 4
B_KEY_DIM = 128
B_VAL_DIM = 128
N_EXPERTS = 32
TOP_K = 4
SWIGLU_LIMIT = 7.0
SWIGLU_ALPHA = 1.702
EPS = 1e-6
LOG2_E = math.log2(math.e)

A_WIDTH = A_HEADS * A_HEAD_DIM
B_WIDTH = B_HEADS * B_VAL_DIM
IDX_WIDTH = IDX_HEADS * IDX_DIM

V7X_LANES = 128
V7X_SUBLANES = 8
V7X_VMEM_LIMIT_BYTES = 56 * 1024 * 1024
V7X_SC_CORES = 2
V7X_SC_SUBCORES = 16

PROJ_ROWS = 512
ATT_Q = 256
ATT_KC = 128
SCORE_UNROLL = 4
COUNT_UNROLL = 2
FAR_UNROLL = 4
HGRN_ROWS = 1024
HGRN_CHUNK = 64
HGRN_GROUP = 4
HGRN_SAFE_DECAY = 70.0
ROUTE_ROWS = 512
EXPERT_ROWS = 512
COMBINE_ROWS = 512
SC_GATHER_ROWS = 64
MASK_NEG = -1e30
BISECT_FAST_ITERS = 26


def _cparams(dims):
    return pltpu.CompilerParams(dimension_semantics=dims, vmem_limit_bytes=V7X_VMEM_LIMIT_BYTES)


def _rms(x, gain):
    return x * lax.rsqrt(jnp.mean(x * x, axis=-1, keepdims=True) + EPS) * gain


def _sigmoid(x):
    return 1.0 / (1.0 + jnp.exp(-x))


def _pack_bf16_pairs(x):
    n = x.shape[1] // 2
    as_bits = lambda v: lax.bitcast_convert_type(v.astype(jnp.bfloat16).astype(jnp.float32), jnp.uint32)
    return (as_bits(x[:, :n]) & jnp.uint32(0xFFFF0000)) | (as_bits(x[:, n:]) >> 16)


def _unpack_bf16_pairs(w):
    hi = lax.bitcast_convert_type(w & jnp.uint32(0xFFFF0000), jnp.float32).astype(jnp.bfloat16)
    lo = lax.bitcast_convert_type(w << 16, jnp.float32).astype(jnp.bfloat16)
    return hi, lo


def _blocked_loop(n, step, carry, unroll):
    def run(first, count, cr):
        for t in range(count):
            cr = step(first + t, cr)
        return cr

    carry = lax.fori_loop(0, n // unroll, lambda j, cr: run(j * unroll, unroll, cr), carry)
    base = (n // unroll) * unroll
    piece = unroll // 2
    while piece >= 1:
        take = (n & piece) != 0
        carry = lax.cond(take, functools.partial(run, base, piece), lambda cr: cr, carry)
        base = base + jnp.where(take, piece, 0)
        piece //= 2
    return carry


def _fold_rows(x, op):
    return op(x.reshape(x.shape[0] // V7X_SUBLANES, V7X_SUBLANES, x.shape[1]), axis=0)


def _inproj_kernel(x_ref, g_ref, wk_ref, wik_ref, wqT_ref, wvT_ref, wiqT_ref, wiwT_ref, wb_ref, wg_ref,
                   k_ref, ik_ref, qT_ref, vT_ref, iqT_ref, iwT_ref, bq_ref, bf_ref, bi_ref, bg_ref,
                   ga_ref, gb_ref):
    x = x_ref[...]
    hn = _rms(x, g_ref[...]).astype(jnp.bfloat16)

    def mm(w_ref):
        return jnp.dot(hn, w_ref[...], preferred_element_type=jnp.float32)

    def mm_t(w_ref):
        return lax.dot_general(w_ref[...], hn, (((1,), (1,)), ((), ())),
                               preferred_element_type=jnp.float32)

    k_ref[...] = mm(wk_ref).astype(jnp.bfloat16)
    ik_ref[...] = mm(wik_ref).astype(jnp.bfloat16)
    qT_ref[0] = (mm_t(wqT_ref) * (A_HEAD_DIM ** -0.5 * LOG2_E)).astype(jnp.bfloat16)
    vT_ref[0] = mm_t(wvT_ref).astype(jnp.bfloat16)
    iqT_ref[0] = mm_t(wiqT_ref).astype(jnp.bfloat16)
    iwT_ref[0] = mm_t(wiwT_ref) * ((IDX_HEADS * IDX_DIM) ** -0.5)
    hb = mm(wb_ref)
    bq_ref[...] = hb[:, 0 * B_WIDTH:1 * B_WIDTH]
    bf_ref[...] = hb[:, 1 * B_WIDTH:2 * B_WIDTH]
    bi_ref[...] = hb[:, 2 * B_WIDTH:3 * B_WIDTH]
    bg_ref[...] = hb[:, 3 * B_WIDTH:4 * B_WIDTH]
    d = ga_ref.shape[-1]
    hg = mm(wg_ref)
    ga_ref[...] = _sigmoid(hg[:, :d]).astype(jnp.bfloat16)
    gb_ref[...] = _sigmoid(hg[:, d:]).astype(jnp.bfloat16)


def _inproj(x2, gain, w_in, B, S):
    T, D = x2.shape
    R = min(PROJ_ROWS, S)
    nS = S // R
    o = np.cumsum((0, A_WIDTH, A_WIDTH, A_WIDTH, IDX_WIDTH, IDX_HEADS, IDX_DIM,
                   B_WIDTH, B_WIDTH, B_WIDTH, B_WIDTH, D, D))
    bf = jnp.bfloat16
    wqT = w_in[:, o[0]:o[1]].T.astype(bf)
    wk = w_in[:, o[1]:o[2]].astype(bf)
    wvT = w_in[:, o[2]:o[3]].T.astype(bf)
    wiqT = w_in[:, o[3]:o[4]].T.astype(bf)
    wiwT = w_in[:, o[4]:o[5]].T.astype(bf)
    wik = w_in[:, o[5]:o[6]].astype(bf)
    wb = w_in[:, o[6]:o[10]].astype(bf)
    wg = w_in[:, o[10]:o[12]].astype(bf)

    def full(a):
        return pl.BlockSpec(a.shape, lambda b, i: (0,) * a.ndim)

    row = lambda n: pl.BlockSpec((R, n), lambda b, i: (b * nS + i, 0))
    colT = lambda n: pl.BlockSpec((1, n, R), lambda b, i: (b, 0, i))
    f32 = jnp.float32
    outs = [
        (jax.ShapeDtypeStruct((T, A_WIDTH), bf), row(A_WIDTH)),
        (jax.ShapeDtypeStruct((T, IDX_DIM), bf), row(IDX_DIM)),
        (jax.ShapeDtypeStruct((B, A_WIDTH, S), bf), colT(A_WIDTH)),
        (jax.ShapeDtypeStruct((B, A_WIDTH, S), bf), colT(A_WIDTH)),
        (jax.ShapeDtypeStruct((B, IDX_WIDTH, S), bf), colT(IDX_WIDTH)),
        (jax.ShapeDtypeStruct((B, IDX_HEADS, S), f32), colT(IDX_HEADS)),
        (jax.ShapeDtypeStruct((T, B_WIDTH), f32), row(B_WIDTH)),
        (jax.ShapeDtypeStruct((T, B_WIDTH), f32), row(B_WIDTH)),
        (jax.ShapeDtypeStruct((T, B_WIDTH), f32), row(B_WIDTH)),
        (jax.ShapeDtypeStruct((T, B_WIDTH), f32), row(B_WIDTH)),
        (jax.ShapeDtypeStruct((T, D), bf), row(D)),
        (jax.ShapeDtypeStruct((T, D), bf), row(D)),
    ]
    ins = [x2, gain.reshape(1, D), wk, wik, wqT, wvT, wiqT, wiwT, wb, wg]
    in_specs = [row(D)] + [full(a) for a in ins[1:]]
    return pl.pallas_call(
        _inproj_kernel,
        grid=(B, nS),
        in_specs=in_specs,
        out_specs=[s for _, s in outs],
        out_shape=[s for s, _ in outs],
        compiler_params=_cparams(("parallel", "parallel")),
        name="inproj",
    )(*ins)


def _t5_bucket_table(n):
    d = np.arange(n)
    max_exact = REL_BUCKETS // 2
    nf = np.maximum(d, 1).astype(np.float64)
    large = max_exact + (np.log(nf / max_exact) / math.log(REL_MAX_DIST / max_exact)
                         * (REL_BUCKETS - max_exact)).astype(np.int32)
    large = np.minimum(large, REL_BUCKETS - 1)
    return np.where(d < max_exact, d, large)


def _dsa_kernel(qT_ref, k_ref, vT_ref, iqT_ref, iwT_ref, ik_ref, enear_ref,
                o_ref, sc_ref, qh_scr, m_scr, acc_scr, *, topk):
    TQ = qT_ref.shape[2]
    KC = TQ
    i = pl.program_id(1)
    nch = i + 1
    q0 = i * TQ
    f32 = jnp.float32
    bf16 = jnp.bfloat16
    key_id = lax.broadcasted_iota(jnp.int32, (KC, TQ), 0)
    qry_id = lax.broadcasted_iota(jnp.int32, (KC, TQ), 1)

    def col_reduce(x, op):
        return op(_fold_rows(x, op), axis=0, keepdims=True)

    iw = iwT_ref[0]

    def score_chunk(c, carry):
        rmin, rmax = carry
        k0 = pl.multiple_of(c * KC, KC)
        ik = ik_ref[pl.ds(k0, KC), :]
        acc = jnp.zeros((KC, TQ), f32)
        for h in range(IDX_HEADS):
            sh = jnp.dot(ik, iqT_ref[0, h * IDX_DIM:(h + 1) * IDX_DIM, :], preferred_element_type=f32)
            acc = acc + jnp.maximum(sh, 0.0) * iw[h:h + 1, :]
        valid = (k0 + key_id) <= (q0 + qry_id)
        sc_ref[pl.ds(k0, KC), :] = jnp.where(valid, acc, MASK_NEG)
        rmin = jnp.minimum(rmin, _fold_rows(jnp.where(valid, acc, -MASK_NEG), jnp.min))
        rmax = jnp.maximum(rmax, _fold_rows(jnp.where(valid, acc, MASK_NEG), jnp.max))
        return rmin, rmax

    rmin8, rmax8 = _blocked_loop(
        nch, score_chunk,
        (jnp.full((V7X_SUBLANES, TQ), -MASK_NEG, f32), jnp.full((V7X_SUBLANES, TQ), MASK_NEG, f32)),
        SCORE_UNROLL)
    rmin = jnp.min(rmin8, axis=0, keepdims=True)
    rmax = jnp.max(rmax8, axis=0, keepdims=True)

    def count_where(pred_fn):
        def one(c, acc):
            k0 = pl.multiple_of(c * KC, KC)
            blk = sc_ref[pl.ds(k0, KC), :]
            return acc + _fold_rows(jnp.where(pred_fn(blk), 1.0, 0.0), jnp.sum)
        acc = _blocked_loop(nch, one, jnp.zeros((V7X_SUBLANES, TQ), f32), COUNT_UNROLL)
        return jnp.sum(acc, axis=0, keepdims=True)

    def band_min_max(lo, hi):
        def body(c, carry):
            bmin, bmax = carry
            k0 = pl.multiple_of(c * KC, KC)
            blk = sc_ref[pl.ds(k0, KC), :]
            bmin = jnp.minimum(bmin, _fold_rows(jnp.where(blk >= lo, blk, -MASK_NEG), jnp.min))
            bmax = jnp.maximum(bmax, _fold_rows(jnp.where(blk < hi, blk, MASK_NEG), jnp.max))
            return bmin, bmax
        bmin8, bmax8 = lax.fori_loop(
            0, nch, body,
            (jnp.full((V7X_SUBLANES, TQ), -MASK_NEG, f32), jnp.full((V7X_SUBLANES, TQ), MASK_NEG, f32)))
        return jnp.min(bmin8, axis=0, keepdims=True), jnp.max(bmax8, axis=0, keepdims=True)

    kf = float(topk)
    n_valid = (q0 + 1 + lax.broadcasted_iota(jnp.int32, (1, TQ), 1)).astype(f32)
    lo0 = rmin
    cnt0 = n_valid
    hi0 = rmax + jnp.maximum(jnp.abs(rmax) * 2.0 ** -20, 1e-30)
    done0 = jnp.where(cnt0 <= kf, 1.0, 0.0)

    def probe(st, lo_s, mid, tie):
        it, lo, hi, cnt, done = st
        active = done < 0.5
        lo_s = jnp.where(active, lo_s, lo)
        c = count_where(lambda blk: blk >= mid)
        feas = c >= kf
        move = active & jnp.logical_not(tie)
        lo_n = jnp.where(move & feas, mid, lo_s)
        cnt_n = jnp.where(move & feas, c, cnt)
        hi_n = jnp.where(move & jnp.logical_not(feas), mid, hi)
        done_n = jnp.where((active & tie) | (cnt_n <= kf), 1.0, done)
        return it + 1, lo_n, hi_n, cnt_n, done_n

    def halve(st):
        _, lo, hi, _, _ = st
        half = lo + 0.5 * (hi - lo)
        stuck = (half <= lo) | (half >= hi)
        return probe(st, lo, half, stuck)

    def snap(st):
        _, lo, hi, _, _ = st
        bmin, bmax = band_min_max(lo, hi)
        mid = bmin + 0.5 * (bmax - bmin)
        return probe(st, bmin, jnp.where(mid <= bmin, bmax, mid), bmax <= bmin)

    st = lax.while_loop(lambda st: (jnp.min(st[-1]) < 0.5) & (st[0] < BISECT_FAST_ITERS),
                        lambda st: halve(halve(st)), (jnp.int32(0), lo0, hi0, cnt0, done0))
    _, thr, _, cnt_thr, _ = lax.while_loop(lambda st: jnp.min(st[-1]) < 0.5, snap, st)

    tie_overflow = jnp.max(cnt_thr) > kf

    @pl.when(jnp.logical_not(tie_overflow))
    def _():
        def mask_chunk(c, _):
            k0 = pl.multiple_of(c * KC, KC)
            sc_ref[pl.ds(k0, KC), :] = jnp.where(sc_ref[pl.ds(k0, KC), :] >= thr, 0.0, MASK_NEG)
            return 0
        lax.fori_loop(0, nch, mask_chunk, 0)

    @pl.when(tie_overflow)
    def _():
        need = kf - count_where(lambda blk: blk > thr)
        tril = jnp.where(lax.broadcasted_iota(jnp.int32, (KC, KC), 1)
                         <= lax.broadcasted_iota(jnp.int32, (KC, KC), 0), 1.0, 0.0).astype(bf16)

        def mask_chunk(c, run):
            k0 = pl.multiple_of(c * KC, KC)
            blk = sc_ref[pl.ds(k0, KC), :]
            eq = jnp.where(blk == thr, 1.0, 0.0)
            pref = jnp.dot(tril, eq.astype(bf16), preferred_element_type=f32)
            sel = (blk > thr) | ((eq > 0.5) & (run + pref <= need))
            sc_ref[pl.ds(k0, KC), :] = jnp.where(sel, 0.0, MASK_NEG)
            return run + pref[KC - 1:KC, :]

        lax.fori_loop(0, nch, mask_chunk, jnp.zeros((1, TQ), f32))

    AK = min(ATT_KC, TQ)
    per = TQ // AK
    head0_q = (lax.broadcasted_iota(jnp.int32, (V7X_LANES, TQ), 0) // A_HEAD_DIM) == 0
    n_pairs = A_HEADS // 2

    m_scr[...] = jnp.full(m_scr.shape, MASK_NEG, f32)
    acc_scr[...] = jnp.zeros(acc_scr.shape, f32)
    v_row = lax.broadcasted_iota(jnp.int32, (V7X_LANES, AK), 0)
    denom_row = [A_HEAD_DIM * (1 - sub) for sub in range(2)]
    for p in range(n_pairs):
        q_pair = qT_ref[0, p * V7X_LANES:(p + 1) * V7X_LANES, :]
        zq = jnp.zeros_like(q_pair)
        qh_scr[2 * p] = jnp.where(head0_q, q_pair, zq)
        qh_scr[2 * p + 1] = jnp.where(head0_q, zq, q_pair)

    def step(c, bias_rows):
        k0 = pl.multiple_of(c * AK, AK)
        msk = sc_ref[pl.ds(k0, AK), :]
        for p in range(n_pairs):
            kp = k_ref[pl.ds(k0, AK), p * V7X_LANES:(p + 1) * V7X_LANES]
            vp = vT_ref[0, p * V7X_LANES:(p + 1) * V7X_LANES, pl.ds(k0, AK)]
            for sub in range(2):
                h = 2 * p + sub
                s = jnp.dot(kp, qh_scr[h], preferred_element_type=f32) + msk
                if bias_rows is not None:
                    s = s + enear_ref[h, bias_rows, :]
                m = m_scr[h:h + 1, :]
                m_new = jnp.maximum(m, col_reduce(s, jnp.max))
                alpha = jnp.exp2(m - m_new)
                pr = jnp.exp2(s - m_new)
                m_scr[h:h + 1, :] = m_new
                v_aug = jnp.where(v_row == denom_row[sub], jnp.ones_like(vp), vp)
                acc_scr[h] = alpha * acc_scr[h] + jnp.dot(v_aug, pr.astype(bf16), preferred_element_type=f32)

    def far(blk, _):
        for jj in range(per):
            step(blk * per + jj, None)
        return 0

    def near(block, first_chunk):
        for jj in range(per):
            step(first_chunk + jj, slice(block * TQ + jj * AK, block * TQ + (jj + 1) * AK))

    _blocked_loop(jnp.maximum(i - 1, 0), far, 0, FAR_UNROLL)

    @pl.when(i >= 1)
    def _():
        near(0, (i - 1) * per)
        near(1, i * per)

    @pl.when(i == 0)
    def _():
        near(1, i * per)
    for p in range(n_pairs):
        outs = [acc_scr[2 * p + sub] / acc_scr[2 * p + sub, denom_row[sub]:denom_row[sub] + 1, :]
                for sub in range(2)]
        o_pair = jnp.where(head0_q, outs[0], outs[1])
        o_ref[:, p * V7X_LANES:(p + 1) * V7X_LANES] = o_pair.T.astype(o_ref.dtype)


def _dsa(k, ik, qT, vT, iqT, iwT, rel_bias, B, S):
    T = k.shape[0]
    TQ = min(ATT_Q, S)
    nQ = S // TQ
    topk = min(TOPK_MAX, S // 4)
    buckets = _t5_bucket_table(2 * TQ + 1)
    assert np.all(_t5_bucket_table(S + 1)[TQ + 1:] == REL_BUCKETS - 1)
    j = np.arange(2 * TQ)[:, None]
    r = np.arange(TQ)[None, :]
    dist = np.maximum(r + TQ - j, 0)
    onehot = (jnp.asarray(buckets[dist], jnp.int32)[None]
              == jnp.arange(REL_BUCKETS, dtype=jnp.int32)[:, None, None]).astype(jnp.float32)
    rel = (rel_bias.astype(jnp.float32) - rel_bias[REL_BUCKETS - 1].astype(jnp.float32)[None, :]) * LOG2_E
    enear = jnp.einsum('nh,njr->hjr', rel, onehot, precision=lax.Precision.HIGHEST)

    return pl.pallas_call(
        functools.partial(_dsa_kernel, topk=topk),
        grid=(B, nQ),
        in_specs=[
            pl.BlockSpec((1, A_WIDTH, TQ), lambda b, i: (b, 0, i)),
            pl.BlockSpec((S, A_WIDTH), lambda b, i: (b, 0)),
            pl.BlockSpec((1, A_WIDTH, S), lambda b, i: (b, 0, 0)),
            pl.BlockSpec((1, IDX_WIDTH, TQ), lambda b, i: (b, 0, i)),
            pl.BlockSpec((1, IDX_HEADS, TQ), lambda b, i: (b, 0, i)),
            pl.BlockSpec((S, IDX_DIM), lambda b, i: (b, 0)),
            pl.BlockSpec((A_HEADS, 2 * TQ, TQ), lambda b, i: (0, 0, 0)),
        ],
        out_specs=pl.BlockSpec((TQ, A_WIDTH), lambda b, i: (b * nQ + i, 0)),
        scratch_shapes=[pltpu.VMEM((S, TQ), jnp.float32),
                        pltpu.VMEM((A_HEADS, V7X_LANES, TQ), jnp.bfloat16),
                        pltpu.VMEM((A_HEADS, TQ), jnp.float32),
                        pltpu.VMEM((A_HEADS, V7X_LANES, TQ), jnp.float32)],
        out_shape=jax.ShapeDtypeStruct((T, A_WIDTH), jnp.bfloat16),
        compiler_params=_cparams(("parallel", "arbitrary")),
        name="dsa",
    )(qT, k, vT, iqT, iwT, ik, enear)


def _hgrn_kernel(bq_ref, bf_ref, bi_ref, bg_ref, lb_ref, gain_ref, o_ref,
                 st_ref, b_scr, q_scr, k_scr, v_scr, oi_scr, qd_s, kl_s, vv_s, dec_s, oi_s, upd_s, st_s,
                 kdT_s, vvT_s):
    R = bq_ref.shape[0]
    C = HGRN_CHUNK
    nC = R // C
    f32 = jnp.float32
    bf16 = jnp.bfloat16
    h = pl.program_id(1)

    @pl.when(pl.program_id(2) == 0)
    def _():
        st_ref[...] = jnp.zeros_like(st_ref)

    lb = lb_ref[pl.ds(h, 1), :]
    gain = gain_ref[pl.ds(h, 1), :]
    tril_incl = jnp.where(lax.broadcasted_iota(jnp.int32, (C, C), 1)
                          <= lax.broadcasted_iota(jnp.int32, (C, C), 0), 1.0, 0.0)
    srow = lax.broadcasted_iota(jnp.int32, (C, B_KEY_DIM), 0)

    def gates(r0):
        f = lb + (1.0 - lb) * _sigmoid(bf_ref[pl.ds(r0, C), :])
        qr = bq_ref[pl.ds(r0, C), :]
        return jnp.log(f), 1.0 - f, qr * _sigmoid(qr) * (B_KEY_DIM ** -0.5), bi_ref[pl.ds(r0, C), :]

    def cumdecay(g):
        tri = tril_incl.astype(bf16)
        g_hi = g.astype(bf16)
        rest = g - g_hi.astype(f32)
        g_mid = rest.astype(bf16)
        g_lo = (rest - g_mid.astype(f32)).astype(bf16)
        return (jnp.dot(tri, g_hi, preferred_element_type=f32) + jnp.dot(tri, g_mid, preferred_element_type=f32)
                + jnp.dot(tri, g_lo, preferred_element_type=f32))

    def advance(r0, st, qd, o_intra, upd, decay_row):
        o_inter = lax.dot_general(qd, st.astype(bf16), (((1,), (1,)), ((), ())), preferred_element_type=f32)
        og = bg_ref[pl.ds(r0, C), :]
        y = _rms(o_inter + o_intra, gain) * (og * _sigmoid(og))
        o_ref[pl.ds(r0, C), :] = y.astype(o_ref.dtype)
        return st * decay_row + upd

    f_all = lb + (1.0 - lb) * _sigmoid(bf_ref[...])
    g_all = jnp.log(f_all)
    decay = jnp.sum(g_all.reshape(nC, C, B_KEY_DIM), axis=1)
    safe = jnp.min(decay) >= -HGRN_SAFE_DECAY

    @pl.when(safe)
    def _():
        qr = bq_ref[...]
        qq = qr * _sigmoid(qr) * (B_KEY_DIM ** -0.5)
        kk = 1.0 - f_all
        GR = HGRN_GROUP * C
        r_id = lax.broadcasted_iota(jnp.int32, (GR, GR), 0)
        c_id = lax.broadcasted_iota(jnp.int32, (GR, GR), 1)
        tri_group = jnp.where((r_id // C == c_id // C) & (c_id <= r_id), 1.0, 0.0)
        tri_group_bf = tri_group.astype(bf16)
        g_hi = g_all.astype(bf16)
        rest = g_all - g_hi.astype(f32)
        g_mid = rest.astype(bf16)
        g_lo = (rest - g_mid.astype(f32)).astype(bf16)
        g_cat = jnp.concatenate([g_hi, g_mid, g_lo], axis=1)
        b_parts = []
        for gi in range(R // GR):
            bc = jnp.dot(tri_group_bf, g_cat[gi * GR:(gi + 1) * GR], preferred_element_type=f32)
            b_parts.append(bc[:, :B_KEY_DIM] + bc[:, B_KEY_DIM:2 * B_KEY_DIM] + bc[:, 2 * B_KEY_DIM:])
        b = jnp.concatenate(b_parts, axis=0)
        b_end = jnp.concatenate([jnp.broadcast_to(b[(c + 1) * C - 1:(c + 1) * C], (C, B_KEY_DIM))
                                 for c in range(nC)], axis=0)
        qd_s[...] = (qq * jnp.exp(b)).astype(bf16)
        kd = kk * jnp.exp(-b)
        kl_s[...] = (kk * jnp.exp(b_end - b)).astype(bf16)
        vv = bi_ref[...]
        vv_s[...] = vv.astype(bf16)
        dec_s[...] = jnp.exp(b_end)
        for gi in range(R // GR):
            kdT_s[gi] = kd[gi * GR:(gi + 1) * GR].T.astype(bf16)
        for c in range(nC):
            vvT_s[c] = vv[c * C:(c + 1) * C].T.astype(bf16)
        for gi in range(R // GR):
            rows = slice(gi * GR, (gi + 1) * GR)
            att = jnp.dot(qd_s[rows], kdT_s[gi], preferred_element_type=f32) * tri_group
            oi_s[rows] = jnp.dot(att.astype(bf16), vv_s[rows], preferred_element_type=f32)
        for c in range(nC):
            rows = slice(c * C, (c + 1) * C)
            upd_s[c] = jnp.dot(vvT_s[c], kl_s[rows], preferred_element_type=f32)
        st = st_ref[...]
        for c in range(nC):
            st_s[c] = st.T.astype(bf16)
            st = st * dec_s[c * C:c * C + 1] + upd_s[c]
        st_ref[...] = st
        for c in range(nC):
            rows = slice(c * C, (c + 1) * C)
            oi_s[rows] = oi_s[rows] + jnp.dot(qd_s[rows], st_s[c], preferred_element_type=f32)
        og = bg_ref[...]
        o_ref[...] = (_rms(oi_s[...], gain) * (og * _sigmoid(og))).astype(o_ref.dtype)

    @pl.when(jnp.logical_not(safe))
    def _():
        def body(c, st):
            r0 = pl.multiple_of(c * C, C)
            g, kk, qq, vv = gates(r0)
            b = cumdecay(g)
            b_last = b[C - 1:C, :]
            b_scr[...] = b
            q_scr[...] = qq
            k_scr[...] = kk
            v_scr[...] = vv

            def row(t, _):
                bt = b_scr[pl.ds(t, 1), :]
                qt = q_scr[pl.ds(t, 1), :]
                ex = jnp.where(srow <= t, bt - b_scr[...], -jnp.inf)
                a = jnp.sum(qt * k_scr[...] * jnp.exp(ex), axis=1, keepdims=True)
                oi_scr[pl.ds(t, 1), :] = jnp.sum(a * v_scr[...], axis=0, keepdims=True)
                return 0
            lax.fori_loop(0, C, row, 0)
            kd_last = (kk * jnp.exp(b_last - b)).astype(bf16)
            upd = lax.dot_general(vv.astype(bf16), kd_last, (((0,), (0,)), ((), ())),
                                  preferred_element_type=f32)
            return advance(r0, st, (qq * jnp.exp(b)).astype(bf16), oi_scr[...], upd, jnp.exp(b_last))
        st_ref[...] = lax.fori_loop(0, nC, body, st_ref[...])


def _hgrn(bq, bf, bi, bg, lb, gain, B, S):
    T = bq.shape[0]
    R = min(HGRN_ROWS, S)
    nR = S // R
    C = HGRN_CHUNK
    blk = pl.BlockSpec((R, B_KEY_DIM), lambda b, h, c: (b * nR + c, h))
    small = pl.BlockSpec((B_HEADS, B_KEY_DIM), lambda b, h, c: (0, 0))
    f32 = jnp.float32
    return pl.pallas_call(
        _hgrn_kernel,
        grid=(B, B_HEADS, nR),
        in_specs=[blk, blk, blk, blk, small, small],
        out_specs=blk,
        out_shape=jax.ShapeDtypeStruct((T, B_WIDTH), jnp.bfloat16),
        scratch_shapes=[pltpu.VMEM((B_VAL_DIM, B_KEY_DIM), f32)] +
                       [pltpu.VMEM((C, B_KEY_DIM), f32) for _ in range(5)] +
                       [pltpu.VMEM((R, B_KEY_DIM), jnp.bfloat16) for _ in range(3)] +
                       [pltpu.VMEM((R, B_KEY_DIM), f32) for _ in range(2)] +
                       [pltpu.VMEM((R // C, B_VAL_DIM, B_KEY_DIM), f32),
                        pltpu.VMEM((R // C, B_KEY_DIM, B_VAL_DIM), jnp.bfloat16),
                        pltpu.VMEM((R // (HGRN_GROUP * C), B_KEY_DIM, HGRN_GROUP * C), jnp.bfloat16),
                        pltpu.VMEM((R // C, B_VAL_DIM, C), jnp.bfloat16)],
        compiler_params=_cparams(("parallel", "parallel", "arbitrary")),
        name="hgrn",
    )(bq, bf, bi, bg, lb, gain)


def _merge_kernel(x_ref, ya_ref, yb_ref, ga_ref, gb_ref, wa_ref, wb_ref, wo_ref, g_ref, wrh_ref, wrl_ref, br_ref,
                  x1_ref, xn_ref, lg_ref):
    f32 = jnp.float32
    ma = jnp.dot(ya_ref[...], wa_ref[...], preferred_element_type=f32)
    mb = jnp.dot(yb_ref[...], wb_ref[...], preferred_element_type=f32)
    merged = ga_ref[...].astype(f32) * ma + gb_ref[...].astype(f32) * mb
    x1 = x_ref[...] + jnp.dot(merged.astype(jnp.bfloat16), wo_ref[...], preferred_element_type=f32)
    x1_ref[...] = x1
    hn = _rms(x1, g_ref[...])
    xn_ref[...] = _pack_bf16_pairs(hn)
    hn_hi = hn.astype(jnp.bfloat16)
    hn_lo = (hn - hn_hi.astype(f32)).astype(jnp.bfloat16)
    lg_ref[...] = (jnp.dot(hn_hi, wrh_ref[...], preferred_element_type=f32)
                   + jnp.dot(hn_lo, wrh_ref[...], preferred_element_type=f32)
                   + jnp.dot(hn_hi, wrl_ref[...], preferred_element_type=f32) + br_ref[...])


def _merge(x2, ya, yb, ga, gb, w_up_a, w_up_b, w_out, gain, w_router, b_router):
    T, D = x2.shape
    R = min(PROJ_ROWS, T)
    bf = jnp.bfloat16
    wr_hi = w_router.astype(bf)
    wr_lo = (w_router - wr_hi.astype(jnp.float32)).astype(bf)
    ins = [x2, ya, yb, ga, gb, w_up_a.astype(bf), w_up_b.astype(bf), w_out.astype(bf),
           gain.reshape(1, D), wr_hi, wr_lo, b_router.reshape(1, N_EXPERTS)]
    row = lambda n: pl.BlockSpec((R, n), lambda i: (i, 0))
    full = lambda a: pl.BlockSpec(a.shape, lambda i: (0,) * a.ndim)
    in_specs = [row(D), row(A_WIDTH), row(B_WIDTH), row(D), row(D)] + [full(a) for a in ins[5:]]
    return pl.pallas_call(
        _merge_kernel,
        grid=(T // R,),
        in_specs=in_specs,
        out_specs=[row(D), row(D // 2), row(N_EXPERTS)],
        out_shape=[jax.ShapeDtypeStruct((T, D), jnp.float32), jax.ShapeDtypeStruct((T, D // 2), jnp.uint32),
                   jax.ShapeDtypeStruct((T, N_EXPERTS), jnp.float32)],
        compiler_params=_cparams(("parallel",)),
        name="merge",
    )(*ins)


def _route_kernel(lg_ref, eidx_ref, gate_ref, rank_ref, cnt_ref, run_ref):
    R = lg_ref.shape[0]
    f32 = jnp.float32

    @pl.when(pl.program_id(0) == 0)
    def _():
        run_ref[...] = jnp.zeros_like(run_ref)

    lg = lg_ref[...].T
    expert = lax.broadcasted_iota(jnp.int32, (N_EXPERTS, R), 0)
    work = lg
    onehots, vals, idxs = [], [], []
    for _ in range(TOP_K):
        m = jnp.max(work, axis=0, keepdims=True)
        idx = jnp.min(jnp.where(work == m, expert, N_EXPERTS), axis=0, keepdims=True)
        oh = expert == idx
        onehots.append(oh)
        vals.append(m)
        idxs.append(idx)
        work = jnp.where(oh, -jnp.inf, work)
    ex = [jnp.exp(v - vals[0]) for v in vals]
    den = ex[0] + ex[1] + ex[2] + ex[3]
    chosen = jnp.where(onehots[0] | onehots[1] | onehots[2] | onehots[3], 1.0, 0.0)
    earlier = jnp.where(lax.broadcasted_iota(jnp.int32, (R, R), 0)
                        < lax.broadcasted_iota(jnp.int32, (R, R), 1), 1.0, 0.0).astype(jnp.bfloat16)
    before = jnp.dot(chosen.astype(jnp.bfloat16), earlier, preferred_element_type=f32) + run_ref[...]
    slot = lax.broadcasted_iota(jnp.int32, (TOP_K, R), 0)
    eidx = jnp.zeros((TOP_K, R), jnp.int32)
    gate = jnp.zeros((TOP_K, R), f32)
    rank = jnp.zeros((TOP_K, R), f32)
    for k in range(TOP_K):
        eidx = jnp.where(slot == k, idxs[k], eidx)
        gate = jnp.where(slot == k, ex[k] / den, gate)
        rk = jnp.sum(jnp.where(onehots[k], before, 0.0), axis=0, keepdims=True)
        rank = jnp.where(slot == k, rk, rank)
    eidx_ref[...] = eidx
    gate_ref[...] = gate
    rank_ref[...] = rank.astype(jnp.int32)
    run_ref[...] = run_ref[...] + jnp.sum(chosen, axis=1, keepdims=True)
    cnt_ref[...] = run_ref[...].astype(jnp.int32)


def _route(logits):
    T = logits.shape[0]
    R = min(ROUTE_ROWS, T)
    col = pl.BlockSpec((TOP_K, R), lambda i: (0, i))
    return pl.pallas_call(
        _route_kernel,
        grid=(T // R,),
        in_specs=[pl.BlockSpec((R, N_EXPERTS), lambda i: (i, 0))],
        out_specs=[col, col, col, pl.BlockSpec((N_EXPERTS, 1), lambda i: (0, 0))],
        out_shape=[jax.ShapeDtypeStruct((TOP_K, T), jnp.int32), jax.ShapeDtypeStruct((TOP_K, T), jnp.float32),
                   jax.ShapeDtypeStruct((TOP_K, T), jnp.int32), jax.ShapeDtypeStruct((N_EXPERTS, 1), jnp.int32)],
        scratch_shapes=[pltpu.VMEM((N_EXPERTS, 1), jnp.float32)],
        compiler_params=_cparams(("arbitrary",)),
        name="route",
    )(logits)


def _sc_gather_rows(table, idx):
    N, W = table.shape
    M = idx.shape[0]
    workers = V7X_SC_CORES * V7X_SC_SUBCORES
    per_worker = M // workers
    pieces = per_worker // SC_GATHER_ROWS
    assert per_worker * workers == M and pieces * SC_GATHER_ROWS == per_worker and pieces % 2 == 0
    mesh = plsc.VectorSubcoreMesh(core_axis_name="core", subcore_axis_name="subcore",
                                  num_cores=V7X_SC_CORES, num_subcores=V7X_SC_SUBCORES)

    @functools.partial(
        pl.kernel, mesh=mesh,
        out_type=jax.ShapeDtypeStruct((M, W), table.dtype),
        scratch_types=[pltpu.VMEM((per_worker,), jnp.int32),
                       pltpu.VMEM((SC_GATHER_ROWS, W), table.dtype),
                       pltpu.VMEM((SC_GATHER_ROWS, W), table.dtype),
                       pltpu.SemaphoreType.DMA, pltpu.SemaphoreType.DMA],
    )
    def gather(table_hbm, idx_hbm, out_hbm, idx_v, rows_a, rows_b, sem_a, sem_b):
        worker = lax.axis_index("subcore") * V7X_SC_CORES + lax.axis_index("core")
        base = pl.multiple_of(worker * per_worker, SC_GATHER_ROWS)
        pltpu.sync_copy(idx_hbm.at[pl.ds(base, per_worker)], idx_v)
        bufs = ((rows_a, sem_a), (rows_b, sem_b))

        def fetch(g, buf, sem):
            off = pl.multiple_of(g * SC_GATHER_ROWS, SC_GATHER_ROWS)
            return pltpu.make_async_copy(table_hbm.at[idx_v.at[pl.ds(off, SC_GATHER_ROWS)]], buf, sem)

        fetch(0, *bufs[0]).start()

        @pl.loop(0, pieces, step=2)
        def _(g0):
            for half in range(2):
                g = g0 + half
                buf, sem = bufs[half]
                fetch(g, buf, sem).wait()

                @pl.when(g + 1 < pieces)
                def _():
                    fetch(g + 1, *bufs[1 - half]).start()

                off = pl.multiple_of(g * SC_GATHER_ROWS, SC_GATHER_ROWS)
                pltpu.sync_copy(buf, out_hbm.at[pl.ds(base + off, SC_GATHER_ROWS)])

    return gather(table, idx)


def _sc_scatter_rows(rows, dest, n_out):
    T, W = rows.shape
    slots = dest.shape[0]
    workers = V7X_SC_CORES * V7X_SC_SUBCORES
    per_worker = T // workers
    pieces = per_worker // SC_GATHER_ROWS
    assert per_worker * workers == T and pieces * SC_GATHER_ROWS == per_worker and pieces % 2 == 0
    idx = dest.reshape(slots, workers, pieces, SC_GATHER_ROWS).transpose(1, 2, 0, 3)
    mesh = plsc.VectorSubcoreMesh(core_axis_name="core", subcore_axis_name="subcore",
                                  num_cores=V7X_SC_CORES, num_subcores=V7X_SC_SUBCORES)

    @functools.partial(
        pl.kernel, mesh=mesh,
        out_type=jax.ShapeDtypeStruct((n_out, W), rows.dtype),
        scratch_types=[pltpu.VMEM((pieces, slots, SC_GATHER_ROWS), jnp.int32),
                       pltpu.VMEM((SC_GATHER_ROWS, W), rows.dtype),
                       pltpu.VMEM((SC_GATHER_ROWS, W), rows.dtype),
                       pltpu.SemaphoreType.DMA, pltpu.SemaphoreType.DMA, pltpu.SemaphoreType.DMA],
    )
    def scatter(rows_hbm, idx_hbm, out_hbm, idx_v, rows_a, rows_b, sem_a, sem_b, sem_out):
        worker = lax.axis_index("subcore") * V7X_SC_CORES + lax.axis_index("core")
        base = pl.multiple_of(worker * per_worker, SC_GATHER_ROWS)
        pltpu.sync_copy(idx_hbm.at[worker], idx_v)
        bufs = ((rows_a, sem_a), (rows_b, sem_b))

        def fetch(g, buf, sem):
            off = pl.multiple_of(g * SC_GATHER_ROWS, SC_GATHER_ROWS)
            return pltpu.make_async_copy(rows_hbm.at[pl.ds(base + off, SC_GATHER_ROWS)], buf, sem)

        fetch(0, *bufs[0]).start()

        @pl.loop(0, pieces, step=2)
        def _(g0):
            for half in range(2):
                g = g0 + half
                buf, sem = bufs[half]
                fetch(g, buf, sem).wait()

                @pl.when(g + 1 < pieces)
                def _():
                    fetch(g + 1, *bufs[1 - half]).start()

                puts = [pltpu.make_async_copy(buf, out_hbm.at[idx_v.at[g, k]], sem_out) for k in range(slots)]
                for put in puts:
                    put.start()
                for put in puts:
                    put.wait()

    return scatter(rows, idx)


def _experts_kernel(be_ref, nb_ref, first_ref, slot_ref, next_ref,
                    x_ref, wgu_hbm, bgu_ref, wd_hbm, bd_ref, o_ref,
                    wgu_f32, wd_f32, wgu_bf, wd_bf, sems):
    f32 = jnp.float32
    d_ff = wd_bf.shape[0]
    i = pl.program_id(0)
    live = i < nb_ref[0]

    def weight_copies(e, s):
        return (pltpu.make_async_copy(wgu_hbm.at[e], wgu_f32.at[s], sems.at[s, 0]),
                pltpu.make_async_copy(wd_hbm.at[e], wd_f32.at[s], sems.at[s, 1]))

    @pl.when(live & (first_ref[i] == 1))
    def _():
        e = be_ref[i]
        s = slot_ref[i]

        @pl.when(i == 0)
        def _():
            for cp in weight_copies(e, s):
                cp.start()

        for cp in weight_copies(e, s):
            cp.wait()

        @pl.when(next_ref[i] >= 0)
        def _():
            for cp in weight_copies(next_ref[i], 1 - s):
                cp.start()

        wgu_bf[...] = wgu_f32[s].astype(jnp.bfloat16)
        wd_bf[...] = wd_f32[s].astype(jnp.bfloat16)

    @pl.when(live)
    def _():
        x_hi, x_lo = _unpack_bf16_pairs(x_ref[...])
        half = x_hi.shape[1]
        gu = (jnp.dot(x_hi, wgu_bf[:half, :], preferred_element_type=f32)
              + jnp.dot(x_lo, wgu_bf[half:, :], preferred_element_type=f32) + bgu_ref[0])
        gate = jnp.minimum(gu[:, :d_ff], SWIGLU_LIMIT)
        lin = jnp.clip(gu[:, d_ff:], -SWIGLU_LIMIT, SWIGLU_LIMIT)
        act = (lin + 1.0) * gate * _sigmoid(SWIGLU_ALPHA * gate)
        y = jnp.dot(act.astype(jnp.bfloat16), wd_bf[...], preferred_element_type=f32) + bd_ref[0]
        o_ref[...] = _pack_bf16_pairs(y)

    @pl.when(pl.program_id(0) >= nb_ref[0])
    def _():
        o_ref[...] = jnp.zeros_like(o_ref)


def _experts(xs, plan, w_gu, b_gu, w_down, b_down):
    P, W = xs.shape
    E, D, F2 = w_gu.shape
    nb = P // EXPERT_ROWS
    by_expert = lambda i, be, *_: (be[i], 0, 0)
    grid_spec = pltpu.PrefetchScalarGridSpec(
        num_scalar_prefetch=5,
        grid=(nb,),
        in_specs=[
            pl.BlockSpec((EXPERT_ROWS, W), lambda i, *_: (i, 0)),
            pl.BlockSpec(memory_space=pl.ANY),
            pl.BlockSpec((1, 1, F2), by_expert),
            pl.BlockSpec(memory_space=pl.ANY),
            pl.BlockSpec((1, 1, D), by_expert),
        ],
        out_specs=pl.BlockSpec((EXPERT_ROWS, W), lambda i, *_: (i, 0)),
        scratch_shapes=[pltpu.VMEM((2, D, F2), jnp.float32), pltpu.VMEM((2, F2 // 2, D), jnp.float32),
                        pltpu.VMEM((D, F2), jnp.bfloat16), pltpu.VMEM((F2 // 2, D), jnp.bfloat16),
                        pltpu.SemaphoreType.DMA((2, 2))],
    )
    return pl.pallas_call(
        _experts_kernel,
        grid_spec=grid_spec,
        out_shape=jax.ShapeDtypeStruct((P, W), jnp.uint32),
        compiler_params=_cparams(("arbitrary",)),
        name="experts",
    )(*plan, xs, w_gu, b_gu.reshape(E, 1, F2), w_down, b_down.reshape(E, 1, D))


def _combine_kernel(ya_ref, x1_ref, gate_ref, g_ref, o_ref):
    half = x1_ref.shape[1] // 2
    f32 = jnp.float32
    gate = gate_ref[...].T
    x1 = x1_ref[...]
    y_hi = x1[:, :half]
    y_lo = x1[:, half:]
    for k in range(TOP_K):
        hi, lo = _unpack_bf16_pairs(ya_ref[k])
        y_hi = y_hi + gate[:, k:k + 1] * hi.astype(f32)
        y_lo = y_lo + gate[:, k:k + 1] * lo.astype(f32)
    o_ref[...] = _rms(jnp.concatenate([y_hi, y_lo], axis=1), g_ref[...])


def _combine(ya, x1, gates, gain):
    T, D = x1.shape
    R = min(COMBINE_ROWS, T)
    row = lambda w: pl.BlockSpec((R, w), lambda i: (i, 0))
    return pl.pallas_call(
        _combine_kernel,
        grid=(T // R,),
        in_specs=[pl.BlockSpec((TOP_K, R, D // 2), lambda i: (0, i, 0)), row(D),
                  pl.BlockSpec((TOP_K, R), lambda i: (0, i)), pl.BlockSpec((1, D), lambda i: (0, 0))],
        out_specs=row(D),
        out_shape=jax.ShapeDtypeStruct((T, D), jnp.float32),
        compiler_params=_cparams(("parallel",)),
        name="combine",
    )(ya, x1, gates, gain.reshape(1, D))


def _moe_plan(eidx, rank, counts, A):
    counts = counts.reshape(N_EXPERTS)
    padded = (counts + EXPERT_ROWS - 1) // EXPERT_ROWS * EXPERT_ROWS
    pad_ends = jnp.cumsum(padded)
    pad_starts = pad_ends - padded
    n_blocks = -(-A // EXPERT_ROWS) + N_EXPERTS
    ids = jnp.arange(N_EXPERTS, dtype=jnp.int32)
    dest = rank + jnp.sum(jnp.where(eidx[None] == ids[:, None, None], pad_starts[:, None, None], 0), axis=0)
    block_start = jnp.arange(n_blocks, dtype=pad_ends.dtype) * EXPERT_ROWS
    block_expert = jnp.minimum(jnp.sum(pad_ends[None, :] <= block_start[:, None], axis=1),
                               N_EXPERTS - 1).astype(jnp.int32)
    n_used = (pad_ends[-1] // EXPERT_ROWS).astype(jnp.int32).reshape(1)
    has_rows = counts > 0
    ordinal = jnp.cumsum(has_rows.astype(jnp.int32)) - 1
    later = has_rows[None, :] & (ids[None, :] > ids[:, None])
    next_expert = jnp.where(jnp.any(later, axis=1), jnp.argmax(later, axis=1), -1).astype(jnp.int32)
    block_first = ((block_start == pad_starts[block_expert]) & (block_start < pad_ends[-1])).astype(jnp.int32)
    block_slot = (ordinal[block_expert] % 2).astype(jnp.int32)
    block_next = next_expert[block_expert]
    plan = (block_expert, n_used, block_first, block_slot, block_next)
    return dest.astype(jnp.int32), plan, n_blocks


def kernel(x, w_in, w_up_a, w_up_b, w_out, norm_mix, norm_ffn, norm_final, hgrn_norm,
           lb_logits, rel_bias, w_router, b_router, w_gu, b_gu, w_down, b_down):
    B, S, D = x.shape
    T = B * S
    assert w_in.shape[0] == 1, "the final rmsnorm is fused into the single layer's combine stage"
    lb_all = jnp.cumsum(jax.nn.softmax(lb_logits.astype(jnp.float32), axis=0), axis=0)
    x2 = x.reshape(T, D)
    (k, ik, qT, vT, iqT, iwT, bq, bf, bi, bg, ga, gb) = _inproj(x2, norm_mix[0], w_in[0], B, S)
    ya = _dsa(k, ik, qT, vT, iqT, iwT, rel_bias, B, S)
    yb = _hgrn(bq, bf, bi, bg, lb_all[0].reshape(B_HEADS, B_KEY_DIM), hgrn_norm[0], B, S)
    x1, xn, logits = _merge(x2, ya, yb, ga, gb, w_up_a[0], w_up_b[0], w_out[0], norm_ffn[0],
                            w_router[0], b_router[0])
    eidx, gates, rank, counts = _route(logits)
    dest, plan, n_blocks = _moe_plan(eidx, rank, counts, T * TOP_K)
    P = n_blocks * EXPERT_ROWS
    A = T * TOP_K
    xs = _sc_scatter_rows(xn, dest, P)
    y_buf = _experts(xs, plan, w_gu[0], b_gu[0], w_down[0], b_down[0])
    ya = _sc_gather_rows(y_buf, dest.reshape(A)).reshape(TOP_K, T, D // 2)
    out = _combine(ya, x1, gates, norm_final)
    return out.reshape(B, S, D)
```

```python
import functools
import math

import numpy as np
import jax
import jax.numpy as jnp
from jax import lax
from jax.experimental import pallas as pl
from jax.experimental.pallas import tpu as pltpu
from jax.experimental.pallas import tpu_sc as plsc

A_HEADS = 8
A_HEAD_DIM = 64
IDX_HEADS = 8
IDX_DIM = 32
TOPK_MAX = 256
REL_BUCKETS = 32
REL_MAX_DIST = 128
B_HEADS = 4
B_KEY_DIM = 128
B_VAL_DIM = 128
N_EXPERTS = 32
TOP_K = 4
SWIGLU_LIMIT = 7.0
SWIGLU_ALPHA = 1.702
EPS = 1e-6
LOG2_E = math.log2(math.e)

A_WIDTH = A_HEADS * A_HEAD_DIM
B_WIDTH = B_HEADS * B_VAL_DIM
IDX_WIDTH = IDX_HEADS * IDX_DIM

V7X_LANES = 128
V7X_SUBLANES = 8
V7X_VMEM_LIMIT_BYTES = 56 * 1024 * 1024
V7X_SC_CORES = 2
V7X_SC_SUBCORES = 16

PROJ_ROWS = 512
ATT_Q = 256
ATT_KC = 128
SCORE_UNROLL = 4
COUNT_UNROLL = 2
FAR_UNROLL = 4
HGRN_ROWS = 1024
HGRN_CHUNK = 64
HGRN_GROUP = 4
HGRN_SAFE_DECAY = 70.0
ROUTE_ROWS = 512
EXPERT_ROWS = 512
COMBINE_ROWS = 512
SC_GATHER_ROWS = 64
MASK_NEG = -1e30
BISECT_FAST_ITERS = 26


def _cparams(dims):
    return pltpu.CompilerParams(dimension_semantics=dims, vmem_limit_bytes=V7X_VMEM_LIMIT_BYTES)


def _rms(x, gain):
    return x * lax.rsqrt(jnp.mean(x * x, axis=-1, keepdims=True) + EPS) * gain


def _sigmoid(x):
    return 1.0 / (1.0 + jnp.exp(-x))


def _pack_bf16_pairs(x):
    n = x.shape[1] // 2
    as_bits = lambda v: lax.bitcast_convert_type(v.astype(jnp.bfloat16).astype(jnp.float32), jnp.uint32)
    return (as_bits(x[:, :n]) & jnp.uint32(0xFFFF0000)) | (as_bits(x[:, n:]) >> 16)


def _unpack_bf16_pairs(w):
    hi = lax.bitcast_convert_type(w & jnp.uint32(0xFFFF0000), jnp.float32).astype(jnp.bfloat16)
    lo = lax.bitcast_convert_type(w << 16, jnp.float32).astype(jnp.bfloat16)
    return hi, lo


def _blocked_loop(n, step, carry, unroll):
    def run(first, count, cr):
        for t in range(count):
            cr = step(first + t, cr)
        return cr

    carry = lax.fori_loop(0, n // unroll, lambda j, cr: run(j * unroll, unroll, cr), carry)
    base = (n // unroll) * unroll
    piece = unroll // 2
    while piece >= 1:
        take = (n & piece) != 0
        carry = lax.cond(take, functools.partial(run, base, piece), lambda cr: cr, carry)
        base = base + jnp.where(take, piece, 0)
        piece //= 2
    return carry


def _fold_rows(x, op):
    return op(x.reshape(x.shape[0] // V7X_SUBLANES, V7X_SUBLANES, x.shape[1]), axis=0)


def _inproj_kernel(x_ref, g_ref, wk_ref, wik_ref, wqT_ref, wvT_ref, wiqT_ref, wiwT_ref, wb_ref, wg_ref,
                   k_ref, ik_ref, qT_ref, vT_ref, iqT_ref, iwT_ref, bq_ref, bf_ref, bi_ref, bg_ref,
                   ga_ref, gb_ref):
    x = x_ref[...]
    hn = _rms(x, g_ref[...]).astype(jnp.bfloat16)

    def mm(w_ref):
        return jnp.dot(hn, w_ref[...], preferred_element_type=jnp.float32)

    def mm_t(w_ref):
        return lax.dot_general(w_ref[...], hn, (((1,), (1,)), ((), ())),
                               preferred_element_type=jnp.float32)

    k_ref[...] = mm(wk_ref).astype(jnp.bfloat16)
    ik_ref[...] = mm(wik_ref).astype(jnp.bfloat16)
    qT_ref[0] = (mm_t(wqT_ref) * (A_HEAD_DIM ** -0.5 * LOG2_E)).astype(jnp.bfloat16)
    vT_ref[0] = mm_t(wvT_ref).astype(jnp.bfloat16)
    iqT_ref[0] = mm_t(wiqT_ref).astype(jnp.bfloat16)
    iwT_ref[0] = mm_t(wiwT_ref) * ((IDX_HEADS * IDX_DIM) ** -0.5)
    hb = mm(wb_ref)
    bq_ref[...] = hb[:, 0 * B_WIDTH:1 * B_WIDTH]
    bf_ref[...] = hb[:, 1 * B_WIDTH:2 * B_WIDTH]
    bi_ref[...] = hb[:, 2 * B_WIDTH:3 * B_WIDTH]
    bg_ref[...] = hb[:, 3 * B_WIDTH:4 * B_WIDTH]
    d = ga_ref.shape[-1]
    hg = mm(wg_ref)
    ga_ref[...] = _sigmoid(hg[:, :d]).astype(jnp.bfloat16)
    gb_ref[...] = _sigmoid(hg[:, d:]).astype(jnp.bfloat16)


def _inproj(x2, gain, w_in, B, S):
    T, D = x2.shape
    R = min(PROJ_ROWS, S)
    nS = S // R
    o = np.cumsum((0, A_WIDTH, A_WIDTH, A_WIDTH, IDX_WIDTH, IDX_HEADS, IDX_DIM,
                   B_WIDTH, B_WIDTH, B_WIDTH, B_WIDTH, D, D))
    bf = jnp.bfloat16
    wqT = w_in[:, o[0]:o[1]].T.astype(bf)
    wk = w_in[:, o[1]:o[2]].astype(bf)
    wvT = w_in[:, o[2]:o[3]].T.astype(bf)
    wiqT = w_in[:, o[3]:o[4]].T.astype(bf)
    wiwT = w_in[:, o[4]:o[5]].T.astype(bf)
    wik = w_in[:, o[5]:o[6]].astype(bf)
    wb = w_in[:, o[6]:o[10]].astype(bf)
    wg = w_in[:, o[10]:o[12]].astype(bf)

    def full(a):
        return pl.BlockSpec(a.shape, lambda b, i: (0,) * a.ndim)

    row = lambda n: pl.BlockSpec((R, n), lambda b, i: (b * nS + i, 0))
    colT = lambda n: pl.BlockSpec((1, n, R), lambda b, i: (b, 0, i))
    f32 = jnp.float32
    outs = [
        (jax.ShapeDtypeStruct((T, A_WIDTH), bf), row(A_WIDTH)),
        (jax.ShapeDtypeStruct((T, IDX_DIM), bf), row(IDX_DIM)),
        (jax.ShapeDtypeStruct((B, A_WIDTH, S), bf), colT(A_WIDTH)),
        (jax.ShapeDtypeStruct((B, A_WIDTH, S), bf), colT(A_WIDTH)),
        (jax.ShapeDtypeStruct((B, IDX_WIDTH, S), bf), colT(IDX_WIDTH)),
        (jax.ShapeDtypeStruct((B, IDX_HEADS, S), f32), colT(IDX_HEADS)),
        (jax.ShapeDtypeStruct((T, B_WIDTH), f32), row(B_WIDTH)),
        (jax.ShapeDtypeStruct((T, B_WIDTH), f32), row(B_WIDTH)),
        (jax.ShapeDtypeStruct((T, B_WIDTH), f32), row(B_WIDTH)),
        (jax.ShapeDtypeStruct((T, B_WIDTH), f32), row(B_WIDTH)),
        (jax.ShapeDtypeStruct((T, D), bf), row(D)),
        (jax.ShapeDtypeStruct((T, D), bf), row(D)),
    ]
    ins = [x2, gain.reshape(1, D), wk, wik, wqT, wvT, wiqT, wiwT, wb, wg]
    in_specs = [row(D)] + [full(a) for a in ins[1:]]
    return pl.pallas_call(
        _inproj_kernel,
        grid=(B, nS),
        in_specs=in_specs,
        out_specs=[s for _, s in outs],
        out_shape=[s for s, _ in outs],
        compiler_params=_cparams(("parallel", "parallel")),
        name="inproj",
    )(*ins)


def _t5_bucket_table(n):
    d = np.arange(n)
    max_exact = REL_BUCKETS // 2
    nf = np.maximum(d, 1).astype(np.float64)
    large = max_exact + (np.log(nf / max_exact) / math.log(REL_MAX_DIST / max_exact)
                         * (REL_BUCKETS - max_exact)).astype(np.int32)
    large = np.minimum(large, REL_BUCKETS - 1)
    return np.where(d < max_exact, d, large)


def _dsa_kernel(qT_ref, k_ref, vT_ref, iqT_ref, iwT_ref, ik_ref, enear_ref,
                o_ref, sc_ref, qh_scr, m_scr, acc_scr, *, topk):
    TQ = qT_ref.shape[2]
    KC = TQ
    i = pl.program_id(1)
    nch = i + 1
    q0 = i * TQ
    f32 = jnp.float32
    bf16 = jnp.bfloat16
    key_id = lax.broadcasted_iota(jnp.int32, (KC, TQ), 0)
    qry_id = lax.broadcasted_iota(jnp.int32, (KC, TQ), 1)

    def col_reduce(x, op):
        return op(_fold_rows(x, op), axis=0, keepdims=True)

    iw = iwT_ref[0]

    def score_chunk(c, carry):
        rmin, rmax = carry
        k0 = pl.multiple_of(c * KC, KC)
        ik = ik_ref[pl.ds(k0, KC), :]
        acc = jnp.zeros((KC, TQ), f32)
        for h in range(IDX_HEADS):
            sh = jnp.dot(ik, iqT_ref[0, h * IDX_DIM:(h + 1) * IDX_DIM, :], preferred_element_type=f32)
            acc = acc + jnp.maximum(sh, 0.0) * iw[h:h + 1, :]
        valid = (k0 + key_id) <= (q0 + qry_id)
        sc_ref[pl.ds(k0, KC), :] = jnp.where(valid, acc, MASK_NEG)
        rmin = jnp.minimum(rmin, _fold_rows(jnp.where(valid, acc, -MASK_NEG), jnp.min))
        rmax = jnp.maximum(rmax, _fold_rows(jnp.where(valid, acc, MASK_NEG), jnp.max))
        return rmin, rmax

    rmin8, rmax8 = _blocked_loop(
        nch, score_chunk,
        (jnp.full((V7X_SUBLANES, TQ), -MASK_NEG, f32), jnp.full((V7X_SUBLANES, TQ), MASK_NEG, f32)),
        SCORE_UNROLL)
    rmin = jnp.min(rmin8, axis=0, keepdims=True)
    rmax = jnp.max(rmax8, axis=0, keepdims=True)

    def count_where(pred_fn):
        def one(c, acc):
            k0 = pl.multiple_of(c * KC, KC)
            blk = sc_ref[pl.ds(k0, KC), :]
            return acc + _fold_rows(jnp.where(pred_fn(blk), 1.0, 0.0), jnp.sum)
        acc = _blocked_loop(nch, one, jnp.zeros((V7X_SUBLANES, TQ), f32), COUNT_UNROLL)
        return jnp.sum(acc, axis=0, keepdims=True)

    def band_min_max(lo, hi):
        def body(c, carry):
            bmin, bmax = carry
            k0 = pl.multiple_of(c * KC, KC)
            blk = sc_ref[pl.ds(k0, KC), :]
            bmin = jnp.minimum(bmin, _fold_rows(jnp.where(blk >= lo, blk, -MASK_NEG), jnp.min))
            bmax = jnp.maximum(bmax, _fold_rows(jnp.where(blk < hi, blk, MASK_NEG), jnp.max))
            return bmin, bmax
        bmin8, bmax8 = lax.fori_loop(
            0, nch, body,
            (jnp.full((V7X_SUBLANES, TQ), -MASK_NEG, f32), jnp.full((V7X_SUBLANES, TQ), MASK_NEG, f32)))
        return jnp.min(bmin8, axis=0, keepdims=True), jnp.max(bmax8, axis=0, keepdims=True)

    kf = float(topk)
    n_valid = (q0 + 1 + lax.broadcasted_iota(jnp.int32, (1, TQ), 1)).astype(f32)
    lo0 = rmin
    cnt0 = n_valid
    hi0 = rmax + jnp.maximum(jnp.abs(rmax) * 2.0 ** -20, 1e-30)
    done0 = jnp.where(cnt0 <= kf, 1.0, 0.0)

    def probe(st, lo_s, mid, tie):
        it, lo, hi, cnt, done = st
        active = done < 0.5
        lo_s = jnp.where(active, lo_s, lo)
        c = count_where(lambda blk: blk >= mid)
        feas = c >= kf
        move = active & jnp.logical_not(tie)
        lo_n = jnp.where(move & feas, mid, lo_s)
        cnt_n = jnp.where(move & feas, c, cnt)
        hi_n = jnp.where(move & jnp.logical_not(feas), mid, hi)
        done_n = jnp.where((active & tie) | (cnt_n <= kf), 1.0, done)
        return it + 1, lo_n, hi_n, cnt_n, done_n

    def halve(st):
        _, lo, hi, _, _ = st
        half = lo + 0.5 * (hi - lo)
        stuck = (half <= lo) | (half >= hi)
        return probe(st, lo, half, stuck)

    def snap(st):
        _, lo, hi, _, _ = st
        bmin, bmax = band_min_max(lo, hi)
        mid = bmin + 0.5 * (bmax - bmin)
        return probe(st, bmin, jnp.where(mid <= bmin, bmax, mid), bmax <= bmin)

    st = lax.while_loop(lambda st: (jnp.min(st[-1]) < 0.5) & (st[0] < BISECT_FAST_ITERS),
                        lambda st: halve(halve(st)), (jnp.int32(0), lo0, hi0, cnt0, done0))
    _, thr, _, cnt_thr, _ = lax.while_loop(lambda st: jnp.min(st[-1]) < 0.5, snap, st)

    tie_overflow = jnp.max(cnt_thr) > kf

    @pl.when(jnp.logical_not(tie_overflow))
    def _():
        def mask_chunk(c, _):
            k0 = pl.multiple_of(c * KC, KC)
            sc_ref[pl.ds(k0, KC), :] = jnp.where(sc_ref[pl.ds(k0, KC), :] >= thr, 0.0, MASK_NEG)
            return 0
        lax.fori_loop(0, nch, mask_chunk, 0)

    @pl.when(tie_overflow)
    def _():
        need = kf - count_where(lambda blk: blk > thr)
        tril = jnp.where(lax.broadcasted_iota(jnp.int32, (KC, KC), 1)
                         <= lax.broadcasted_iota(jnp.int32, (KC, KC), 0), 1.0, 0.0).astype(bf16)

        def mask_chunk(c, run):
            k0 = pl.multiple_of(c * KC, KC)
            blk = sc_ref[pl.ds(k0, KC), :]
            eq = jnp.where(blk == thr, 1.0, 0.0)
            pref = jnp.dot(tril, eq.astype(bf16), preferred_element_type=f32)
            sel = (blk > thr) | ((eq > 0.5) & (run + pref <= need))
            sc_ref[pl.ds(k0, KC), :] = jnp.where(sel, 0.0, MASK_NEG)
            return run + pref[KC - 1:KC, :]

        lax.fori_loop(0, nch, mask_chunk, jnp.zeros((1, TQ), f32))

    AK = min(ATT_KC, TQ)
    per = TQ // AK
    head0_q = (lax.broadcasted_iota(jnp.int32, (V7X_LANES, TQ), 0) // A_HEAD_DIM) == 0
    n_pairs = A_HEADS // 2

    m_scr[...] = jnp.full(m_scr.shape, MASK_NEG, f32)
    acc_scr[...] = jnp.zeros(acc_scr.shape, f32)
    v_row = lax.broadcasted_iota(jnp.int32, (V7X_LANES, AK), 0)
    denom_row = [A_HEAD_DIM * (1 - sub) for sub in range(2)]
    for p in range(n_pairs):
        q_pair = qT_ref[0, p * V7X_LANES:(p + 1) * V7X_LANES, :]
        zq = jnp.zeros_like(q_pair)
        qh_scr[2 * p] = jnp.where(head0_q, q_pair, zq)
        qh_scr[2 * p + 1] = jnp.where(head0_q, zq, q_pair)

    def step(c, bias_rows):
        k0 = pl.multiple_of(c * AK, AK)
        msk = sc_ref[pl.ds(k0, AK), :]
        for p in range(n_pairs):
            kp = k_ref[pl.ds(k0, AK), p * V7X_LANES:(p + 1) * V7X_LANES]
            vp = vT_ref[0, p * V7X_LANES:(p + 1) * V7X_LANES, pl.ds(k0, AK)]
            for sub in range(2):
                h = 2 * p + sub
                s = jnp.dot(kp, qh_scr[h], preferred_element_type=f32) + msk
                if bias_rows is not None:
                    s = s + enear_ref[h, bias_rows, :]
                m = m_scr[h:h + 1, :]
                m_new = jnp.maximum(m, col_reduce(s, jnp.max))
                alpha = jnp.exp2(m - m_new)
                pr = jnp.exp2(s - m_new)
                m_scr[h:h + 1, :] = m_new
                v_aug = jnp.where(v_row == denom_row[sub], jnp.ones_like(vp), vp)
                acc_scr[h] = alpha * acc_scr[h] + jnp.dot(v_aug, pr.astype(bf16), preferred_element_type=f32)

    def far(blk, _):
        for jj in range(per):
            step(blk * per + jj, None)
        return 0

    def near(block, first_chunk):
        for jj in range(per):
            step(first_chunk + jj, slice(block * TQ + jj * AK, block * TQ + (jj + 1) * AK))

    _blocked_loop(jnp.maximum(i - 1, 0), far, 0, FAR_UNROLL)

    @pl.when(i >= 1)
    def _():
        near(0, (i - 1) * per)
        near(1, i * per)

    @pl.when(i == 0)
    def _():
        near(1, i * per)
    for p in range(n_pairs):
        outs = [acc_scr[2 * p + sub] / acc_scr[2 * p + sub, denom_row[sub]:denom_row[sub] + 1, :]
                for sub in range(2)]
        o_pair = jnp.where(head0_q, outs[0], outs[1])
        o_ref[:, p * V7X_LANES:(p + 1) * V7X_LANES] = o_pair.T.astype(o_ref.dtype)


def _dsa(k, ik, qT, vT, iqT, iwT, rel_bias, B, S):
    T = k.shape[0]
    TQ = min(ATT_Q, S)
    nQ = S // TQ
    topk = min(TOPK_MAX, S // 4)
    buckets = _t5_bucket_table(2 * TQ + 1)
    assert np.all(_t5_bucket_table(S + 1)[TQ + 1:] == REL_BUCKETS - 1)
    j = np.arange(2 * TQ)[:, None]
    r = np.arange(TQ)[None, :]
    dist = np.maximum(r + TQ - j, 0)
    onehot = (jnp.asarray(buckets[dist], jnp.int32)[None]
              == jnp.arange(REL_BUCKETS, dtype=jnp.int32)[:, None, None]).astype(jnp.float32)
    rel = (rel_bias.astype(jnp.float32) - rel_bias[REL_BUCKETS - 1].astype(jnp.float32)[None, :]) * LOG2_E
    enear = jnp.einsum('nh,njr->hjr', rel, onehot, precision=lax.Precision.HIGHEST)

    return pl.pallas_call(
        functools.partial(_dsa_kernel, topk=topk),
        grid=(B, nQ),
        in_specs=[
            pl.BlockSpec((1, A_WIDTH, TQ), lambda b, i: (b, 0, i)),
            pl.BlockSpec((S, A_WIDTH), lambda b, i: (b, 0)),
            pl.BlockSpec((1, A_WIDTH, S), lambda b, i: (b, 0, 0)),
            pl.BlockSpec((1, IDX_WIDTH, TQ), lambda b, i: (b, 0, i)),
            pl.BlockSpec((1, IDX_HEADS, TQ), lambda b, i: (b, 0, i)),
            pl.BlockSpec((S, IDX_DIM), lambda b, i: (b, 0)),
            pl.BlockSpec((A_HEADS, 2 * TQ, TQ), lambda b, i: (0, 0, 0)),
        ],
        out_specs=pl.BlockSpec((TQ, A_WIDTH), lambda b, i: (b * nQ + i, 0)),
        scratch_shapes=[pltpu.VMEM((S, TQ), jnp.float32),
                        pltpu.VMEM((A_HEADS, V7X_LANES, TQ), jnp.bfloat16),
                        pltpu.VMEM((A_HEADS, TQ), jnp.float32),
                        pltpu.VMEM((A_HEADS, V7X_LANES, TQ), jnp.float32)],
        out_shape=jax.ShapeDtypeStruct((T, A_WIDTH), jnp.bfloat16),
        compiler_params=_cparams(("parallel", "arbitrary")),
        name="dsa",
    )(qT, k, vT, iqT, iwT, ik, enear)


def _hgrn_kernel(bq_ref, bf_ref, bi_ref, bg_ref, lb_ref, gain_ref, o_ref,
                 st_ref, b_scr, q_scr, k_scr, v_scr, oi_scr, qd_s, kl_s, vv_s, dec_s, oi_s, upd_s, st_s,
                 kdT_s, vvT_s):
    R = bq_ref.shape[0]
    C = HGRN_CHUNK
    nC = R // C
    f32 = jnp.float32
    bf16 = jnp.bfloat16
    h = pl.program_id(1)

    @pl.when(pl.program_id(2) == 0)
    def _():
        st_ref[...] = jnp.zeros_like(st_ref)

    lb = lb_ref[pl.ds(h, 1), :]
    gain = gain_ref[pl.ds(h, 1), :]
    tril_incl = jnp.where(lax.broadcasted_iota(jnp.int32, (C, C), 1)
                          <= lax.broadcasted_iota(jnp.int32, (C, C), 0), 1.0, 0.0)
    srow = lax.broadcasted_iota(jnp.int32, (C, B_KEY_DIM), 0)

    def gates(r0):
        f = lb + (1.0 - lb) * _sigmoid(bf_ref[pl.ds(r0, C), :])
        qr = bq_ref[pl.ds(r0, C), :]
        return jnp.log(f), 1.0 - f, qr * _sigmoid(qr) * (B_KEY_DIM ** -0.5), bi_ref[pl.ds(r0, C), :]

    def cumdecay(g):
        tri = tril_incl.astype(bf16)
        g_hi = g.astype(bf16)
        rest = g - g_hi.astype(f32)
        g_mid = rest.astype(bf16)
        g_lo = (rest - g_mid.astype(f32)).astype(bf16)
        return (jnp.dot(tri, g_hi, preferred_element_type=f32) + jnp.dot(tri, g_mid, preferred_element_type=f32)
                + jnp.dot(tri, g_lo, preferred_element_type=f32))

    def advance(r0, st, qd, o_intra, upd, decay_row):
        o_inter = lax.dot_general(qd, st.astype(bf16), (((1,), (1,)), ((), ())), preferred_element_type=f32)
        og = bg_ref[pl.ds(r0, C), :]
        y = _rms(o_inter + o_intra, gain) * (og * _sigmoid(og))
        o_ref[pl.ds(r0, C), :] = y.astype(o_ref.dtype)
        return st * decay_row + upd

    f_all = lb + (1.0 - lb) * _sigmoid(bf_ref[...])
    g_all = jnp.log(f_all)
    decay = jnp.sum(g_all.reshape(nC, C, B_KEY_DIM), axis=1)
    safe = jnp.min(decay) >= -HGRN_SAFE_DECAY

    @pl.when(safe)
    def _():
        qr = bq_ref[...]
        qq = qr * _sigmoid(qr) * (B_KEY_DIM ** -0.5)
        kk = 1.0 - f_all
        GR = HGRN_GROUP * C
        r_id = lax.broadcasted_iota(jnp.int32, (GR, GR), 0)
        c_id = lax.broadcasted_iota(jnp.int32, (GR, GR), 1)
        tri_group = jnp.where((r_id // C == c_id // C) & (c_id <= r_id), 1.0, 0.0)
        tri_group_bf = tri_group.astype(bf16)
        g_hi = g_all.astype(bf16)
        rest = g_all - g_hi.astype(f32)
        g_mid = rest.astype(bf16)
        g_lo = (rest - g_mid.astype(f32)).astype(bf16)
        g_cat = jnp.concatenate([g_hi, g_mid, g_lo], axis=1)
        b_parts = []
        for gi in range(R // GR):
            bc = jnp.dot(tri_group_bf, g_cat[gi * GR:(gi + 1) * GR], preferred_element_type=f32)
            b_parts.append(bc[:, :B_KEY_DIM] + bc[:, B_KEY_DIM:2 * B_KEY_DIM] + bc[:, 2 * B_KEY_DIM:])
        b = jnp.concatenate(b_parts, axis=0)
        b_end = jnp.concatenate([jnp.broadcast_to(b[(c + 1) * C - 1:(c + 1) * C], (C, B_KEY_DIM))
                                 for c in range(nC)], axis=0)
        qd_s[...] = (qq * jnp.exp(b)).astype(bf16)
        kd = kk * jnp.exp(-b)
        kl_s[...] = (kk * jnp.exp(b_end - b)).astype(bf16)
        vv = bi_ref[...]
        vv_s[...] = vv.astype(bf16)
        dec_s[...] = jnp.exp(b_end)
        for gi in range(R // GR):
            kdT_s[gi] = kd[gi * GR:(gi + 1) * GR].T.astype(bf16)
        for c in range(nC):
            vvT_s[c] = vv[c * C:(c + 1) * C].T.astype(bf16)
        for gi in range(R // GR):
            rows = slice(gi * GR, (gi + 1) * GR)
            att = jnp.dot(qd_s[rows], kdT_s[gi], preferred_element_type=f32) * tri_group
            oi_s[rows] = jnp.dot(att.astype(bf16), vv_s[rows], preferred_element_type=f32)
        for c in range(nC):
            rows = slice(c * C, (c + 1) * C)
            upd_s[c] = jnp.dot(vvT_s[c], kl_s[rows], preferred_element_type=f32)
        st = st_ref[...]
        for c in range(nC):
            st_s[c] = st.T.astype(bf16)
            st = st * dec_s[c * C:c * C + 1] + upd_s[c]
        st_ref[...] = st
        for c in range(nC):
            rows = slice(c * C, (c + 1) * C)
            oi_s[rows] = oi_s[rows] + jnp.dot(qd_s[rows], st_s[c], preferred_element_type=f32)
        og = bg_ref[...]
        o_ref[...] = (_rms(oi_s[...], gain) * (og * _sigmoid(og))).astype(o_ref.dtype)

    @pl.when(jnp.logical_not(safe))
    def _():
        def body(c, st):
            r0 = pl.multiple_of(c * C, C)
            g, kk, qq, vv = gates(r0)
            b = cumdecay(g)
            b_last = b[C - 1:C, :]
            b_scr[...] = b
            q_scr[...] = qq
            k_scr[...] = kk
            v_scr[...] = vv

            def row(t, _):
                bt = b_scr[pl.ds(t, 1), :]
                qt = q_scr[pl.ds(t, 1), :]
                ex = jnp.where(srow <= t, bt - b_scr[...], -jnp.inf)
                a = jnp.sum(qt * k_scr[...] * jnp.exp(ex), axis=1, keepdims=True)
                oi_scr[pl.ds(t, 1), :] = jnp.sum(a * v_scr[...], axis=0, keepdims=True)
                return 0
            lax.fori_loop(0, C, row, 0)
            kd_last = (kk * jnp.exp(b_last - b)).astype(bf16)
            upd = lax.dot_general(vv.astype(bf16), kd_last, (((0,), (0,)), ((), ())),
                                  preferred_element_type=f32)
            return advance(r0, st, (qq * jnp.exp(b)).astype(bf16), oi_scr[...], upd, jnp.exp(b_last))
        st_ref[...] = lax.fori_loop(0, nC, body, st_ref[...])


def _hgrn(bq, bf, bi, bg, lb, gain, B, S):
    T = bq.shape[0]
    R = min(HGRN_ROWS, S)
    nR = S // R
    C = HGRN_CHUNK
    blk = pl.BlockSpec((R, B_KEY_DIM), lambda b, h, c: (b * nR + c, h))
    small = pl.BlockSpec((B_HEADS, B_KEY_DIM), lambda b, h, c: (0, 0))
    f32 = jnp.float32
    return pl.pallas_call(
        _hgrn_kernel,
        grid=(B, B_HEADS, nR),
        in_specs=[blk, blk, blk, blk, small, small],
        out_specs=blk,
        out_shape=jax.ShapeDtypeStruct((T, B_WIDTH), jnp.bfloat16),
        scratch_shapes=[pltpu.VMEM((B_VAL_DIM, B_KEY_DIM), f32)] +
                       [pltpu.VMEM((C, B_KEY_DIM), f32) for _ in range(5)] +
                       [pltpu.VMEM((R, B_KEY_DIM), jnp.bfloat16) for _ in range(3)] +
                       [pltpu.VMEM((R, B_KEY_DIM), f32) for _ in range(2)] +
                       [pltpu.VMEM((R // C, B_VAL_DIM, B_KEY_DIM), f32),
                        pltpu.VMEM((R // C, B_KEY_DIM, B_VAL_DIM), jnp.bfloat16),
                        pltpu.VMEM((R // (HGRN_GROUP * C), B_KEY_DIM, HGRN_GROUP * C), jnp.bfloat16),
                        pltpu.VMEM((R // C, B_VAL_DIM, C), jnp.bfloat16)],
        compiler_params=_cparams(("parallel", "parallel", "arbitrary")),
        name="hgrn",
    )(bq, bf, bi, bg, lb, gain)


def _merge_kernel(x_ref, ya_ref, yb_ref, ga_ref, gb_ref, wa_ref, wb_ref, wo_ref, g_ref, wrh_ref, wrl_ref, br_ref,
                  x1_ref, xn_ref, lg_ref):
    f32 = jnp.float32
    ma = jnp.dot(ya_ref[...], wa_ref[...], preferred_element_type=f32)
    mb = jnp.dot(yb_ref[...], wb_ref[...], preferred_element_type=f32)
    merged = ga_ref[...].astype(f32) * ma + gb_ref[...].astype(f32) * mb
    x1 = x_ref[...] + jnp.dot(merged.astype(jnp.bfloat16), wo_ref[...], preferred_element_type=f32)
    x1_ref[...] = x1
    hn = _rms(x1, g_ref[...])
    xn_ref[...] = _pack_bf16_pairs(hn)
    hn_hi = hn.astype(jnp.bfloat16)
    hn_lo = (hn - hn_hi.astype(f32)).astype(jnp.bfloat16)
    lg_ref[...] = (jnp.dot(hn_hi, wrh_ref[...], preferred_element_type=f32)
                   + jnp.dot(hn_lo, wrh_ref[...], preferred_element_type=f32)
                   + jnp.dot(hn_hi, wrl_ref[...], preferred_element_type=f32) + br_ref[...])


def _merge(x2, ya, yb, ga, gb, w_up_a, w_up_b, w_out, gain, w_router, b_router):
    T, D = x2.shape
    R = min(PROJ_ROWS, T)
    bf = jnp.bfloat16
    wr_hi = w_router.astype(bf)
    wr_lo = (w_router - wr_hi.astype(jnp.float32)).astype(bf)
    ins = [x2, ya, yb, ga, gb, w_up_a.astype(bf), w_up_b.astype(bf), w_out.astype(bf),
           gain.reshape(1, D), wr_hi, wr_lo, b_router.reshape(1, N_EXPERTS)]
    row = lambda n: pl.BlockSpec((R, n), lambda i: (i, 0))
    full = lambda a: pl.BlockSpec(a.shape, lambda i: (0,) * a.ndim)
    in_specs = [row(D), row(A_WIDTH), row(B_WIDTH), row(D), row(D)] + [full(a) for a in ins[5:]]
    return pl.pallas_call(
        _merge_kernel,
        grid=(T // R,),
        in_specs=in_specs,
        out_specs=[row(D), row(D // 2), row(N_EXPERTS)],
        out_shape=[jax.ShapeDtypeStruct((T, D), jnp.float32), jax.ShapeDtypeStruct((T, D // 2), jnp.uint32),
                   jax.ShapeDtypeStruct((T, N_EXPERTS), jnp.float32)],
        compiler_params=_cparams(("parallel",)),
        name="merge",
    )(*ins)


def _route_kernel(lg_ref, eidx_ref, gate_ref, rank_ref, cnt_ref, run_ref):
    R = lg_ref.shape[0]
    f32 = jnp.float32

    @pl.when(pl.program_id(0) == 0)
    def _():
        run_ref[...] = jnp.zeros_like(run_ref)

    lg = lg_ref[...].T
    expert = lax.broadcasted_iota(jnp.int32, (N_EXPERTS, R), 0)
    work = lg
    onehots, vals, idxs = [], [], []
    for _ in range(TOP_K):
        m = jnp.max(work, axis=0, keepdims=True)
        idx = jnp.min(jnp.where(work == m, expert, N_EXPERTS), axis=0, keepdims=True)
        oh = expert == idx
        onehots.append(oh)
        vals.append(m)
        idxs.append(idx)
        work = jnp.where(oh, -jnp.inf, work)
    ex = [jnp.exp(v - vals[0]) for v in vals]
    den = ex[0] + ex[1] + ex[2] + ex[3]
    chosen = jnp.where(onehots[0] | onehots[1] | onehots[2] | onehots[3], 1.0, 0.0)
    earlier = jnp.where(lax.broadcasted_iota(jnp.int32, (R, R), 0)
                        < lax.broadcasted_iota(jnp.int32, (R, R), 1), 1.0, 0.0).astype(jnp.bfloat16)
    before = jnp.dot(chosen.astype(jnp.bfloat16), earlier, preferred_element_type=f32) + run_ref[...]
    slot = lax.broadcasted_iota(jnp.int32, (TOP_K, R), 0)
    eidx = jnp.zeros((TOP_K, R), jnp.int32)
    gate = jnp.zeros((TOP_K, R), f32)
    rank = jnp.zeros((TOP_K, R), f32)
    for k in range(TOP_K):
        eidx = jnp.where(slot == k, idxs[k], eidx)
        gate = jnp.where(slot == k, ex[k] / den, gate)
        rk = jnp.sum(jnp.where(onehots[k], before, 0.0), axis=0, keepdims=True)
        rank = jnp.where(slot == k, rk, rank)
    eidx_ref[...] = eidx
    gate_ref[...] = gate
    rank_ref[...] = rank.astype(jnp.int32)
    run_ref[...] = run_ref[...] + jnp.sum(chosen, axis=1, keepdims=True)
    cnt_ref[...] = run_ref[...].astype(jnp.int32)


def _route(logits):
    T = logits.shape[0]
    R = min(ROUTE_ROWS, T)
    col = pl.BlockSpec((TOP_K, R), lambda i: (0, i))
    return pl.pallas_call(
        _route_kernel,
        grid=(T // R,),
        in_specs=[pl.BlockSpec((R, N_EXPERTS), lambda i: (i, 0))],
        out_specs=[col, col, col, pl.BlockSpec((N_EXPERTS, 1), lambda i: (0, 0))],
        out_shape=[jax.ShapeDtypeStruct((TOP_K, T), jnp.int32), jax.ShapeDtypeStruct((TOP_K, T), jnp.float32),
                   jax.ShapeDtypeStruct((TOP_K, T), jnp.int32), jax.ShapeDtypeStruct((N_EXPERTS, 1), jnp.int32)],
        scratch_shapes=[pltpu.VMEM((N_EXPERTS, 1), jnp.float32)],
        compiler_params=_cparams(("arbitrary",)),
        name="route",
    )(logits)


def _sc_gather_rows(table, idx):
    N, W = table.shape
    M = idx.shape[0]
    workers = V7X_SC_CORES * V7X_SC_SUBCORES
    per_worker = M // workers
    pieces = per_worker // SC_GATHER_ROWS
    assert per_worker * workers == M and pieces * SC_GATHER_ROWS == per_worker and pieces % 2 == 0
    mesh = plsc.VectorSubcoreMesh(core_axis_name="core", subcore_axis_name="subcore",
                                  num_cores=V7X_SC_CORES, num_subcores=V7X_SC_SUBCORES)

    @functools.partial(
        pl.kernel, mesh=mesh,
        out_type=jax.ShapeDtypeStruct((M, W), table.dtype),
        scratch_types=[pltpu.VMEM((per_worker,), jnp.int32),
                       pltpu.VMEM((SC_GATHER_ROWS, W), table.dtype),
                       pltpu.VMEM((SC_GATHER_ROWS, W), table.dtype),
                       pltpu.SemaphoreType.DMA, pltpu.SemaphoreType.DMA],
    )
    def gather(table_hbm, idx_hbm, out_hbm, idx_v, rows_a, rows_b, sem_a, sem_b):
        worker = lax.axis_index("subcore") * V7X_SC_CORES + lax.axis_index("core")
        base = pl.multiple_of(worker * per_worker, SC_GATHER_ROWS)
        pltpu.sync_copy(idx_hbm.at[pl.ds(base, per_worker)], idx_v)
        bufs = ((rows_a, sem_a), (rows_b, sem_b))

        def fetch(g, buf, sem):
            off = pl.multiple_of(g * SC_GATHER_ROWS, SC_GATHER_ROWS)
            return pltpu.make_async_copy(table_hbm.at[idx_v.at[pl.ds(off, SC_GATHER_ROWS)]], buf, sem)

        fetch(0, *bufs[0]).start()

        @pl.loop(0, pieces, step=2)
        def _(g0):
            for half in range(2):
                g = g0 + half
                buf, sem = bufs[half]
                fetch(g, buf, sem).wait()

                @pl.when(g + 1 < pieces)
                def _():
                    fetch(g + 1, *bufs[1 - half]).start()

                off = pl.multiple_of(g * SC_GATHER_ROWS, SC_GATHER_ROWS)
                pltpu.sync_copy(buf, out_hbm.at[pl.ds(base + off, SC_GATHER_ROWS)])

    return gather(table, idx)


def _sc_scatter_rows(rows, dest, n_out):
    T, W = rows.shape
    slots = dest.shape[0]
    workers = V7X_SC_CORES * V7X_SC_SUBCORES
    per_worker = T // workers
    pieces = per_worker // SC_GATHER_ROWS
    assert per_worker * workers == T and pieces * SC_GATHER_ROWS == per_worker and pieces % 2 == 0
    idx = dest.reshape(slots, workers, pieces, SC_GATHER_ROWS).transpose(1, 2, 0, 3)
    mesh = plsc.VectorSubcoreMesh(core_axis_name="core", subcore_axis_name="subcore",
                                  num_cores=V7X_SC_CORES, num_subcores=V7X_SC_SUBCORES)

    @functools.partial(
        pl.kernel, mesh=mesh,
        out_type=jax.ShapeDtypeStruct((n_out, W), rows.dtype),
        scratch_types=[pltpu.VMEM((pieces, slots, SC_GATHER_ROWS), jnp.int32),
                       pltpu.VMEM((SC_GATHER_ROWS, W), rows.dtype),
                       pltpu.VMEM((SC_GATHER_ROWS, W), rows.dtype),
                       pltpu.SemaphoreType.DMA, pltpu.SemaphoreType.DMA, pltpu.SemaphoreType.DMA],
    )
    def scatter(rows_hbm, idx_hbm, out_hbm, idx_v, rows_a, rows_b, sem_a, sem_b, sem_out):
        worker = lax.axis_index("subcore") * V7X_SC_CORES + lax.axis_index("core")
        base = pl.multiple_of(worker * per_worker, SC_GATHER_ROWS)
        pltpu.sync_copy(idx_hbm.at[worker], idx_v)
        bufs = ((rows_a, sem_a), (rows_b, sem_b))

        def fetch(g, buf, sem):
            off = pl.multiple_of(g * SC_GATHER_ROWS, SC_GATHER_ROWS)
            return pltpu.make_async_copy(rows_hbm.at[pl.ds(base + off, SC_GATHER_ROWS)], buf, sem)

        fetch(0, *bufs[0]).start()

        @pl.loop(0, pieces, step=2)
        def _(g0):
            for half in range(2):
                g = g0 + half
                buf, sem = bufs[half]
                fetch(g, buf, sem).wait()

                @pl.when(g + 1 < pieces)
                def _():
                    fetch(g + 1, *bufs[1 - half]).start()

                puts = [pltpu.make_async_copy(buf, out_hbm.at[idx_v.at[g, k]], sem_out) for k in range(slots)]
                for put in puts:
                    put.start()
                for put in puts:
                    put.wait()

    return scatter(rows, idx)


def _experts_kernel(be_ref, nb_ref, first_ref, slot_ref, next_ref,
                    x_ref, wgu_hbm, bgu_ref, wd_hbm, bd_ref, o_ref,
                    wgu_f32, wd_f32, wgu_bf, wd_bf, sems):
    f32 = jnp.float32
    d_ff = wd_bf.shape[0]
    i = pl.program_id(0)
    live = i < nb_ref[0]

    def weight_copies(e, s):
        return (pltpu.make_async_copy(wgu_hbm.at[e], wgu_f32.at[s], sems.at[s, 0]),
                pltpu.make_async_copy(wd_hbm.at[e], wd_f32.at[s], sems.at[s, 1]))

    @pl.when(live & (first_ref[i] == 1))
    def _():
        e = be_ref[i]
        s = slot_ref[i]

        @pl.when(i == 0)
        def _():
            for cp in weight_copies(e, s):
                cp.start()

        for cp in weight_copies(e, s):
            cp.wait()

        @pl.when(next_ref[i] >= 0)
        def _():
            for cp in weight_copies(next_ref[i], 1 - s):
                cp.start()

        wgu_bf[...] = wgu_f32[s].astype(jnp.bfloat16)
        wd_bf[...] = wd_f32[s].astype(jnp.bfloat16)

    @pl.when(live)
    def _():
        x_hi, x_lo = _unpack_bf16_pairs(x_ref[...])
        half = x_hi.shape[1]
        gu = (jnp.dot(x_hi, wgu_bf[:half, :], preferred_element_type=f32)
              + jnp.dot(x_lo, wgu_bf[half:, :], preferred_element_type=f32) + bgu_ref[0])
        gate = jnp.minimum(gu[:, :d_ff], SWIGLU_LIMIT)
        lin = jnp.clip(gu[:, d_ff:], -SWIGLU_LIMIT, SWIGLU_LIMIT)
        act = (lin + 1.0) * gate * _sigmoid(SWIGLU_ALPHA * gate)
        y = jnp.dot(act.astype(jnp.bfloat16), wd_bf[...], preferred_element_type=f32) + bd_ref[0]
        o_ref[...] = _pack_bf16_pairs(y)

    @pl.when(pl.program_id(0) >= nb_ref[0])
    def _():
        o_ref[...] = jnp.zeros_like(o_ref)


def _experts(xs, plan, w_gu, b_gu, w_down, b_down):
    P, W = xs.shape
    E, D, F2 = w_gu.shape
    nb = P // EXPERT_ROWS
    by_expert = lambda i, be, *_: (be[i], 0, 0)
    grid_spec = pltpu.PrefetchScalarGridSpec(
        num_scalar_prefetch=5,
        grid=(nb,),
        in_specs=[
            pl.BlockSpec((EXPERT_ROWS, W), lambda i, *_: (i, 0)),
            pl.BlockSpec(memory_space=pl.ANY),
            pl.BlockSpec((1, 1, F2), by_expert),
            pl.BlockSpec(memory_space=pl.ANY),
            pl.BlockSpec((1, 1, D), by_expert),
        ],
        out_specs=pl.BlockSpec((EXPERT_ROWS, W), lambda i, *_: (i, 0)),
        scratch_shapes=[pltpu.VMEM((2, D, F2), jnp.float32), pltpu.VMEM((2, F2 // 2, D), jnp.float32),
                        pltpu.VMEM((D, F2), jnp.bfloat16), pltpu.VMEM((F2 // 2, D), jnp.bfloat16),
                        pltpu.SemaphoreType.DMA((2, 2))],
    )
    return pl.pallas_call(
        _experts_kernel,
        grid_spec=grid_spec,
        out_shape=jax.ShapeDtypeStruct((P, W), jnp.uint32),
        compiler_params=_cparams(("arbitrary",)),
        name="experts",
    )(*plan, xs, w_gu, b_gu.reshape(E, 1, F2), w_down, b_down.reshape(E, 1, D))


def _combine_kernel(ya_ref, x1_ref, gate_ref, g_ref, o_ref):
    half = x1_ref.shape[1] // 2
    f32 = jnp.float32
    gate = gate_ref[...].T
    x1 = x1_ref[...]
    y_hi = x1[:, :half]
    y_lo = x1[:, half:]
    for k in range(TOP_K):
        hi, lo = _unpack_bf16_pairs(ya_ref[k])
        y_hi = y_hi + gate[:, k:k + 1] * hi.astype(f32)
        y_lo = y_lo + gate[:, k:k + 1] * lo.astype(f32)
    o_ref[...] = _rms(jnp.concatenate([y_hi, y_lo], axis=1), g_ref[...])


def _combine(ya, x1, gates, gain):
    T, D = x1.shape
    R = min(COMBINE_ROWS, T)
    row = lambda w: pl.BlockSpec((R, w), lambda i: (i, 0))
    return pl.pallas_call(
        _combine_kernel,
        grid=(T // R,),
        in_specs=[pl.BlockSpec((TOP_K, R, D // 2), lambda i: (0, i, 0)), row(D),
                  pl.BlockSpec((TOP_K, R), lambda i: (0, i)), pl.BlockSpec((1, D), lambda i: (0, 0))],
        out_specs=row(D),
        out_shape=jax.ShapeDtypeStruct((T, D), jnp.float32),
        compiler_params=_cparams(("parallel",)),
        name="combine",
    )(ya, x1, gates, gain.reshape(1, D))


def _moe_plan(eidx, rank, counts, A):
    counts = counts.reshape(N_EXPERTS)
    padded = (counts + EXPERT_ROWS - 1) // EXPERT_ROWS * EXPERT_ROWS
    pad_ends = jnp.cumsum(padded)
    pad_starts = pad_ends - padded
    n_blocks = -(-A // EXPERT_ROWS) + N_EXPERTS
    ids = jnp.arange(N_EXPERTS, dtype=jnp.int32)
    dest = rank + jnp.sum(jnp.where(eidx[None] == ids[:, None, None], pad_starts[:, None, None], 0), axis=0)
    block_start = jnp.arange(n_blocks, dtype=pad_ends.dtype) * EXPERT_ROWS
    block_expert = jnp.minimum(jnp.sum(pad_ends[None, :] <= block_start[:, None], axis=1),
                               N_EXPERTS - 1).astype(jnp.int32)
    n_used = (pad_ends[-1] // EXPERT_ROWS).astype(jnp.int32).reshape(1)
    has_rows = counts > 0
    ordinal = jnp.cumsum(has_rows.astype(jnp.int32)) - 1
    later = has_rows[None, :] & (ids[None, :] > ids[:, None])
    next_expert = jnp.where(jnp.any(later, axis=1), jnp.argmax(later, axis=1), -1).astype(jnp.int32)
    tables = jnp.stack([pad_starts, ordinal % 2, next_expert], axis=1).astype(jnp.float32)
    onehot = (block_expert[:, None] == ids[None, :]).astype(jnp.float32)
    looked = jnp.dot(onehot, tables, precision=lax.Precision.HIGHEST).astype(jnp.int32)
    block_first = ((block_start == looked[:, 0]) & (block_start < pad_ends[-1])).astype(jnp.int32)
    plan = (block_expert, n_used, block_first, looked[:, 1], looked[:, 2])
    return dest.astype(jnp.int32), plan, n_blocks


def kernel(x, w_in, w_up_a, w_up_b, w_out, norm_mix, norm_ffn, norm_final, hgrn_norm,
           lb_logits, rel_bias, w_router, b_router, w_gu, b_gu, w_down, b_down):
    B, S, D = x.shape
    T = B * S
    assert w_in.shape[0] == 1, "the final rmsnorm is fused into the single layer's combine stage"
    lb_all = jnp.cumsum(jax.nn.softmax(lb_logits.astype(jnp.float32), axis=0), axis=0)
    x2 = x.reshape(T, D)
    (k, ik, qT, vT, iqT, iwT, bq, bf, bi, bg, ga, gb) = _inproj(x2, norm_mix[0], w_in[0], B, S)
    ya = _dsa(k, ik, qT, vT, iqT, iwT, rel_bias, B, S)
    yb = _hgrn(bq, bf, bi, bg, lb_all[0].reshape(B_HEADS, B_KEY_DIM), hgrn_norm[0], B, S)
    x1, xn, logits = _merge(x2, ya, yb, ga, gb, w_up_a[0], w_up_b[0], w_out[0], norm_ffn[0],
                            w_router[0], b_router[0])
    eidx, gates, rank, counts = _route(logits)
    dest, plan, n_blocks = _moe_plan(eidx, rank, counts, T * TOP_K)
    P = n_blocks * EXPERT_ROWS
    A = T * TOP_K
    xs = _sc_scatter_rows(xn, dest, P)
    y_buf = _experts(xs, plan, w_gu[0], b_gu[0], w_down[0], b_down[0])
    ya = _sc_gather_rows(y_buf, dest.reshape(A)).reshape(TOP_K, T, D // 2)
    out = _combine(ya, x1, gates, norm_final)
    return out.reshape(B, S, D)
```

```python
import functools
import math

import numpy as np
import jax
import jax.numpy as jnp
from jax import lax
from jax.experimental import pallas as pl
from jax.experimental.pallas import tpu as pltpu
from jax.experimental.pallas import tpu_sc as plsc

A_HEADS = 8
A_HEAD_DIM = 64
IDX_HEADS = 8
IDX_DIM = 32
TOPK_MAX = 256
REL_BUCKETS = 32
REL_MAX_DIST = 128
B_HEADS = 4
B_KEY_DIM = 128
B_VAL_DIM = 128
N_EXPERTS = 32
TOP_K = 4
SWIGLU_LIMIT = 7.0
SWIGLU_ALPHA = 1.702
EPS = 1e-6
LOG2_E = math.log2(math.e)

A_WIDTH = A_HEADS * A_HEAD_DIM
B_WIDTH = B_HEADS * B_VAL_DIM
IDX_WIDTH = IDX_HEADS * IDX_DIM

V7X_LANES = 128
V7X_SUBLANES = 8
V7X_VMEM_LIMIT_BYTES = 56 * 1024 * 1024
V7X_SC_CORES = 2
V7X_SC_SUBCORES = 16

PROJ_ROWS = 512
ATT_Q = 256
ATT_KC = 128
SCORE_UNROLL = 4
COUNT_UNROLL = 2
FAR_UNROLL = 4
HGRN_ROWS = 1024
HGRN_CHUNK = 64
HGRN_GROUP = 4
HGRN_SAFE_DECAY = 70.0
ROUTE_ROWS = 512
EXPERT_ROWS = 512
COMBINE_ROWS = 512
SC_GATHER_ROWS = 64
MASK_NEG = -1e30
BISECT_FAST_ITERS = 26
BISECT_BLIND_ITERS = 12


def _cparams(dims):
    return pltpu.CompilerParams(dimension_semantics=dims, vmem_limit_bytes=V7X_VMEM_LIMIT_BYTES)


def _rms(x, gain):
    return x * lax.rsqrt(jnp.mean(x * x, axis=-1, keepdims=True) + EPS) * gain


def _sigmoid(x):
    return 1.0 / (1.0 + jnp.exp(-x))


def _pack_bf16_pairs(x):
    n = x.shape[1] // 2
    as_bits = lambda v: lax.bitcast_convert_type(v.astype(jnp.bfloat16).astype(jnp.float32), jnp.uint32)
    return (as_bits(x[:, :n]) & jnp.uint32(0xFFFF0000)) | (as_bits(x[:, n:]) >> 16)


def _unpack_bf16_pairs(w):
    hi = lax.bitcast_convert_type(w & jnp.uint32(0xFFFF0000), jnp.float32).astype(jnp.bfloat16)
    lo = lax.bitcast_convert_type(w << 16, jnp.float32).astype(jnp.bfloat16)
    return hi, lo


def _blocked_loop(n, step, carry, unroll):
    def run(first, count, cr):
        for t in range(count):
            cr = step(first + t, cr)
        return cr

    carry = lax.fori_loop(0, n // unroll, lambda j, cr: run(j * unroll, unroll, cr), carry)
    base = (n // unroll) * unroll
    piece = unroll // 2
    while piece >= 1:
        take = (n & piece) != 0
        carry = lax.cond(take, functools.partial(run, base, piece), lambda cr: cr, carry)
        base = base + jnp.where(take, piece, 0)
        piece //= 2
    return carry


def _fold_rows(x, op):
    return op(x.reshape(x.shape[0] // V7X_SUBLANES, V7X_SUBLANES, x.shape[1]), axis=0)


def _inproj_kernel(x_ref, g_ref, wk_ref, wik_ref, wqT_ref, wvT_ref, wiqT_ref, wiwT_ref, wb_ref, wg_ref,
                   k_ref, ik_ref, qT_ref, vT_ref, iqT_ref, iwT_ref, bq_ref, bf_ref, bi_ref, bg_ref,
                   ga_ref, gb_ref):
    x = x_ref[...]
    hn = _rms(x, g_ref[...]).astype(jnp.bfloat16)

    def mm(w_ref):
        return jnp.dot(hn, w_ref[...], preferred_element_type=jnp.float32)

    def mm_t(w_ref):
        return lax.dot_general(w_ref[...], hn, (((1,), (1,)), ((), ())),
                               preferred_element_type=jnp.float32)

    k_ref[...] = mm(wk_ref).astype(jnp.bfloat16)
    ik_ref[...] = mm(wik_ref).astype(jnp.bfloat16)
    qT_ref[0] = (mm_t(wqT_ref) * (A_HEAD_DIM ** -0.5 * LOG2_E)).astype(jnp.bfloat16)
    vT_ref[0] = mm_t(wvT_ref).astype(jnp.bfloat16)
    iqT_ref[0] = mm_t(wiqT_ref).astype(jnp.bfloat16)
    iwT_ref[0] = mm_t(wiwT_ref) * ((IDX_HEADS * IDX_DIM) ** -0.5)
    hb = mm(wb_ref)
    bq_ref[...] = hb[:, 0 * B_WIDTH:1 * B_WIDTH]
    bf_ref[...] = hb[:, 1 * B_WIDTH:2 * B_WIDTH]
    bi_ref[...] = hb[:, 2 * B_WIDTH:3 * B_WIDTH]
    bg_ref[...] = hb[:, 3 * B_WIDTH:4 * B_WIDTH]
    d = ga_ref.shape[-1]
    hg = mm(wg_ref)
    ga_ref[...] = _sigmoid(hg[:, :d]).astype(jnp.bfloat16)
    gb_ref[...] = _sigmoid(hg[:, d:]).astype(jnp.bfloat16)


def _inproj(x2, gain, w_in, B, S):
    T, D = x2.shape
    R = min(PROJ_ROWS, S)
    nS = S // R
    o = np.cumsum((0, A_WIDTH, A_WIDTH, A_WIDTH, IDX_WIDTH, IDX_HEADS, IDX_DIM,
                   B_WIDTH, B_WIDTH, B_WIDTH, B_WIDTH, D, D))
    bf = jnp.bfloat16
    wqT = w_in[:, o[0]:o[1]].T.astype(bf)
    wk = w_in[:, o[1]:o[2]].astype(bf)
    wvT = w_in[:, o[2]:o[3]].T.astype(bf)
    wiqT = w_in[:, o[3]:o[4]].T.astype(bf)
    wiwT = w_in[:, o[4]:o[5]].T.astype(bf)
    wik = w_in[:, o[5]:o[6]].astype(bf)
    wb = w_in[:, o[6]:o[10]].astype(bf)
    wg = w_in[:, o[10]:o[12]].astype(bf)

    def full(a):
        return pl.BlockSpec(a.shape, lambda b, i: (0,) * a.ndim)

    row = lambda n: pl.BlockSpec((R, n), lambda b, i: (b * nS + i, 0))
    colT = lambda n: pl.BlockSpec((1, n, R), lambda b, i: (b, 0, i))
    f32 = jnp.float32
    outs = [
        (jax.ShapeDtypeStruct((T, A_WIDTH), bf), row(A_WIDTH)),
        (jax.ShapeDtypeStruct((T, IDX_DIM), bf), row(IDX_DIM)),
        (jax.ShapeDtypeStruct((B, A_WIDTH, S), bf), colT(A_WIDTH)),
        (jax.ShapeDtypeStruct((B, A_WIDTH, S), bf), colT(A_WIDTH)),
        (jax.ShapeDtypeStruct((B, IDX_WIDTH, S), bf), colT(IDX_WIDTH)),
        (jax.ShapeDtypeStruct((B, IDX_HEADS, S), f32), colT(IDX_HEADS)),
        (jax.ShapeDtypeStruct((T, B_WIDTH), f32), row(B_WIDTH)),
        (jax.ShapeDtypeStruct((T, B_WIDTH), f32), row(B_WIDTH)),
        (jax.ShapeDtypeStruct((T, B_WIDTH), f32), row(B_WIDTH)),
        (jax.ShapeDtypeStruct((T, B_WIDTH), f32), row(B_WIDTH)),
        (jax.ShapeDtypeStruct((T, D), bf), row(D)),
        (jax.ShapeDtypeStruct((T, D), bf), row(D)),
    ]
    ins = [x2, gain.reshape(1, D), wk, wik, wqT, wvT, wiqT, wiwT, wb, wg]
    in_specs = [row(D)] + [full(a) for a in ins[1:]]
    return pl.pallas_call(
        _inproj_kernel,
        grid=(B, nS),
        in_specs=in_specs,
        out_specs=[s for _, s in outs],
        out_shape=[s for s, _ in outs],
        compiler_params=_cparams(("parallel", "parallel")),
        name="inproj",
    )(*ins)


def _t5_bucket_table(n):
    d = np.arange(n)
    max_exact = REL_BUCKETS // 2
    nf = np.maximum(d, 1).astype(np.float64)
    large = max_exact + (np.log(nf / max_exact) / math.log(REL_MAX_DIST / max_exact)
                         * (REL_BUCKETS - max_exact)).astype(np.int32)
    large = np.minimum(large, REL_BUCKETS - 1)
    return np.where(d < max_exact, d, large)


def _dsa_kernel(qT_ref, k_ref, vT_ref, iqT_ref, iwT_ref, ik_ref, enear_ref,
                o_ref, sc_ref, qh_scr, m_scr, acc_scr, *, topk):
    TQ = qT_ref.shape[2]
    KC = TQ
    i = pl.program_id(1)
    nch = i + 1
    q0 = i * TQ
    f32 = jnp.float32
    bf16 = jnp.bfloat16
    key_id = lax.broadcasted_iota(jnp.int32, (KC, TQ), 0)
    qry_id = lax.broadcasted_iota(jnp.int32, (KC, TQ), 1)

    def col_reduce(x, op):
        return op(_fold_rows(x, op), axis=0, keepdims=True)

    iw = iwT_ref[0]

    def score_chunk(c, carry):
        rmin, rmax = carry
        k0 = pl.multiple_of(c * KC, KC)
        ik = ik_ref[pl.ds(k0, KC), :]
        acc = jnp.zeros((KC, TQ), f32)
        for h in range(IDX_HEADS):
            sh = jnp.dot(ik, iqT_ref[0, h * IDX_DIM:(h + 1) * IDX_DIM, :], preferred_element_type=f32)
            acc = acc + jnp.maximum(sh, 0.0) * iw[h:h + 1, :]
        valid = (k0 + key_id) <= (q0 + qry_id)
        sc_ref[pl.ds(k0, KC), :] = jnp.where(valid, acc, MASK_NEG)
        rmin = jnp.minimum(rmin, _fold_rows(jnp.where(valid, acc, -MASK_NEG), jnp.min))
        rmax = jnp.maximum(rmax, _fold_rows(jnp.where(valid, acc, MASK_NEG), jnp.max))
        return rmin, rmax

    rmin8, rmax8 = _blocked_loop(
        nch, score_chunk,
        (jnp.full((V7X_SUBLANES, TQ), -MASK_NEG, f32), jnp.full((V7X_SUBLANES, TQ), MASK_NEG, f32)),
        SCORE_UNROLL)
    rmin = jnp.min(rmin8, axis=0, keepdims=True)
    rmax = jnp.max(rmax8, axis=0, keepdims=True)

    def count_where(pred_fn):
        def one(c, acc):
            k0 = pl.multiple_of(c * KC, KC)
            blk = sc_ref[pl.ds(k0, KC), :]
            return acc + _fold_rows(jnp.where(pred_fn(blk), 1.0, 0.0), jnp.sum)
        acc = _blocked_loop(nch, one, jnp.zeros((V7X_SUBLANES, TQ), f32), COUNT_UNROLL)
        return jnp.sum(acc, axis=0, keepdims=True)

    def band_min_max(lo, hi):
        def body(c, carry):
            bmin, bmax = carry
            k0 = pl.multiple_of(c * KC, KC)
            blk = sc_ref[pl.ds(k0, KC), :]
            bmin = jnp.minimum(bmin, _fold_rows(jnp.where(blk >= lo, blk, -MASK_NEG), jnp.min))
            bmax = jnp.maximum(bmax, _fold_rows(jnp.where(blk < hi, blk, MASK_NEG), jnp.max))
            return bmin, bmax
        bmin8, bmax8 = lax.fori_loop(
            0, nch, body,
            (jnp.full((V7X_SUBLANES, TQ), -MASK_NEG, f32), jnp.full((V7X_SUBLANES, TQ), MASK_NEG, f32)))
        return jnp.min(bmin8, axis=0, keepdims=True), jnp.max(bmax8, axis=0, keepdims=True)

    kf = float(topk)
    n_valid = (q0 + 1 + lax.broadcasted_iota(jnp.int32, (1, TQ), 1)).astype(f32)
    lo0 = rmin
    cnt0 = n_valid
    hi0 = rmax + jnp.maximum(jnp.abs(rmax) * 2.0 ** -20, 1e-30)
    done0 = jnp.where(cnt0 <= kf, 1.0, 0.0)

    def probe(st, lo_s, mid, tie):
        it, lo, hi, cnt, done = st
        active = done < 0.5
        lo_s = jnp.where(active, lo_s, lo)
        c = count_where(lambda blk: blk >= mid)
        feas = c >= kf
        move = active & jnp.logical_not(tie)
        lo_n = jnp.where(move & feas, mid, lo_s)
        cnt_n = jnp.where(move & feas, c, cnt)
        hi_n = jnp.where(move & jnp.logical_not(feas), mid, hi)
        done_n = jnp.where((active & tie) | (cnt_n <= kf), 1.0, done)
        return it + 1, lo_n, hi_n, cnt_n, done_n

    def halve(st):
        _, lo, hi, _, _ = st
        half = lo + 0.5 * (hi - lo)
        stuck = (half <= lo) | (half >= hi)
        return probe(st, lo, half, stuck)

    def snap(st):
        _, lo, hi, _, _ = st
        bmin, bmax = band_min_max(lo, hi)
        mid = bmin + 0.5 * (bmax - bmin)
        return probe(st, bmin, jnp.where(mid <= bmin, bmax, mid), bmax <= bmin)

    st = lax.fori_loop(0, BISECT_BLIND_ITERS // 2, lambda _, st: halve(halve(st)),
                       (jnp.int32(0), lo0, hi0, cnt0, done0))
    st = lax.while_loop(lambda st: (jnp.min(st[-1]) < 0.5) & (st[0] < BISECT_FAST_ITERS),
                        lambda st: halve(halve(st)), st)
    _, thr, _, cnt_thr, _ = lax.while_loop(lambda st: jnp.min(st[-1]) < 0.5, snap, st)

    tie_overflow = jnp.max(cnt_thr) > kf

    @pl.when(jnp.logical_not(tie_overflow))
    def _():
        def mask_chunk(c, _):
            k0 = pl.multiple_of(c * KC, KC)
            sc_ref[pl.ds(k0, KC), :] = jnp.where(sc_ref[pl.ds(k0, KC), :] >= thr, 0.0, MASK_NEG)
            return 0
        lax.fori_loop(0, nch, mask_chunk, 0)

    @pl.when(tie_overflow)
    def _():
        need = kf - count_where(lambda blk: blk > thr)
        tril = jnp.where(lax.broadcasted_iota(jnp.int32, (KC, KC), 1)
                         <= lax.broadcasted_iota(jnp.int32, (KC, KC), 0), 1.0, 0.0).astype(bf16)

        def mask_chunk(c, run):
            k0 = pl.multiple_of(c * KC, KC)
            blk = sc_ref[pl.ds(k0, KC), :]
            eq = jnp.where(blk == thr, 1.0, 0.0)
            pref = jnp.dot(tril, eq.astype(bf16), preferred_element_type=f32)
            sel = (blk > thr) | ((eq > 0.5) & (run + pref <= need))
            sc_ref[pl.ds(k0, KC), :] = jnp.where(sel, 0.0, MASK_NEG)
            return run + pref[KC - 1:KC, :]

        lax.fori_loop(0, nch, mask_chunk, jnp.zeros((1, TQ), f32))

    AK = min(ATT_KC, TQ)
    per = TQ // AK
    head0_q = (lax.broadcasted_iota(jnp.int32, (V7X_LANES, TQ), 0) // A_HEAD_DIM) == 0
    n_pairs = A_HEADS // 2

    m_scr[...] = jnp.full(m_scr.shape, MASK_NEG, f32)
    acc_scr[...] = jnp.zeros(acc_scr.shape, f32)
    v_row = lax.broadcasted_iota(jnp.int32, (V7X_LANES, AK), 0)
    denom_row = [A_HEAD_DIM * (1 - sub) for sub in range(2)]
    for p in range(n_pairs):
        q_pair = qT_ref[0, p * V7X_LANES:(p + 1) * V7X_LANES, :]
        zq = jnp.zeros_like(q_pair)
        qh_scr[2 * p] = jnp.where(head0_q, q_pair, zq)
        qh_scr[2 * p + 1] = jnp.where(head0_q, zq, q_pair)

    def step(c, bias_rows):
        k0 = pl.multiple_of(c * AK, AK)
        msk = sc_ref[pl.ds(k0, AK), :]
        for p in range(n_pairs):
            kp = k_ref[pl.ds(k0, AK), p * V7X_LANES:(p + 1) * V7X_LANES]
            vp = vT_ref[0, p * V7X_LANES:(p + 1) * V7X_LANES, pl.ds(k0, AK)]
            for sub in range(2):
                h = 2 * p + sub
                s = jnp.dot(kp, qh_scr[h], preferred_element_type=f32) + msk
                if bias_rows is not None:
                    s = s + enear_ref[h, bias_rows, :]
                m = m_scr[h:h + 1, :]
                m_new = jnp.maximum(m, col_reduce(s, jnp.max))
                alpha = jnp.exp2(m - m_new)
                pr = jnp.exp2(s - m_new)
                m_scr[h:h + 1, :] = m_new
                v_aug = jnp.where(v_row == denom_row[sub], jnp.ones_like(vp), vp)
                acc_scr[h] = alpha * acc_scr[h] + jnp.dot(v_aug, pr.astype(bf16), preferred_element_type=f32)

    def far(blk, _):
        for jj in range(per):
            step(blk * per + jj, None)
        return 0

    def near(block, first_chunk):
        for jj in range(per):
            step(first_chunk + jj, slice(block * TQ + jj * AK, block * TQ + (jj + 1) * AK))

    _blocked_loop(jnp.maximum(i - 1, 0), far, 0, FAR_UNROLL)

    @pl.when(i >= 1)
    def _():
        near(0, (i - 1) * per)
        near(1, i * per)

    @pl.when(i == 0)
    def _():
        near(1, i * per)
    for p in range(n_pairs):
        outs = [acc_scr[2 * p + sub] / acc_scr[2 * p + sub, denom_row[sub]:denom_row[sub] + 1, :]
                for sub in range(2)]
        o_pair = jnp.where(head0_q, outs[0], outs[1])
        o_ref[:, p * V7X_LANES:(p + 1) * V7X_LANES] = o_pair.T.astype(o_ref.dtype)


def _dsa(k, ik, qT, vT, iqT, iwT, rel_bias, B, S):
    T = k.shape[0]
    TQ = min(ATT_Q, S)
    nQ = S // TQ
    topk = min(TOPK_MAX, S // 4)
    buckets = _t5_bucket_table(2 * TQ + 1)
    assert np.all(_t5_bucket_table(S + 1)[TQ + 1:] == REL_BUCKETS - 1)
    j = np.arange(2 * TQ)[:, None]
    r = np.arange(TQ)[None, :]
    dist = np.maximum(r + TQ - j, 0)
    onehot = (jnp.asarray(buckets[dist], jnp.int32)[None]
              == jnp.arange(REL_BUCKETS, dtype=jnp.int32)[:, None, None]).astype(jnp.float32)
    rel = (rel_bias.astype(jnp.float32) - rel_bias[REL_BUCKETS - 1].astype(jnp.float32)[None, :]) * LOG2_E
    enear = jnp.einsum('nh,njr->hjr', rel, onehot, precision=lax.Precision.HIGHEST)

    return pl.pallas_call(
        functools.partial(_dsa_kernel, topk=topk),
        grid=(B, nQ),
        in_specs=[
            pl.BlockSpec((1, A_WIDTH, TQ), lambda b, i: (b, 0, i)),
            pl.BlockSpec((S, A_WIDTH), lambda b, i: (b, 0)),
            pl.BlockSpec((1, A_WIDTH, S), lambda b, i: (b, 0, 0)),
            pl.BlockSpec((1, IDX_WIDTH, TQ), lambda b, i: (b, 0, i)),
            pl.BlockSpec((1, IDX_HEADS, TQ), lambda b, i: (b, 0, i)),
            pl.BlockSpec((S, IDX_DIM), lambda b, i: (b, 0)),
            pl.BlockSpec((A_HEADS, 2 * TQ, TQ), lambda b, i: (0, 0, 0)),
        ],
        out_specs=pl.BlockSpec((TQ, A_WIDTH), lambda b, i: (b * nQ + i, 0)),
        scratch_shapes=[pltpu.VMEM((S, TQ), jnp.float32),
                        pltpu.VMEM((A_HEADS, V7X_LANES, TQ), jnp.bfloat16),
                        pltpu.VMEM((A_HEADS, TQ), jnp.float32),
                        pltpu.VMEM((A_HEADS, V7X_LANES, TQ), jnp.float32)],
        out_shape=jax.ShapeDtypeStruct((T, A_WIDTH), jnp.bfloat16),
        compiler_params=_cparams(("parallel", "arbitrary")),
        name="dsa",
    )(qT, k, vT, iqT, iwT, ik, enear)


def _hgrn_kernel(bq_ref, bf_ref, bi_ref, bg_ref, lb_ref, gain_ref, o_ref,
                 st_ref, b_scr, q_scr, k_scr, v_scr, oi_scr, qd_s, kl_s, vv_s, dec_s, oi_s, upd_s, st_s,
                 kdT_s, vvT_s):
    R = bq_ref.shape[0]
    C = HGRN_CHUNK
    nC = R // C
    f32 = jnp.float32
    bf16 = jnp.bfloat16
    h = pl.program_id(1)

    @pl.when(pl.program_id(2) == 0)
    def _():
        st_ref[...] = jnp.zeros_like(st_ref)

    lb = lb_ref[pl.ds(h, 1), :]
    gain = gain_ref[pl.ds(h, 1), :]
    tril_incl = jnp.where(lax.broadcasted_iota(jnp.int32, (C, C), 1)
                          <= lax.broadcasted_iota(jnp.int32, (C, C), 0), 1.0, 0.0)
    srow = lax.broadcasted_iota(jnp.int32, (C, B_KEY_DIM), 0)

    def gates(r0):
        f = lb + (1.0 - lb) * _sigmoid(bf_ref[pl.ds(r0, C), :])
        qr = bq_ref[pl.ds(r0, C), :]
        return jnp.log(f), 1.0 - f, qr * _sigmoid(qr) * (B_KEY_DIM ** -0.5), bi_ref[pl.ds(r0, C), :]

    def cumdecay(g):
        tri = tril_incl.astype(bf16)
        g_hi = g.astype(bf16)
        rest = g - g_hi.astype(f32)
        g_mid = rest.astype(bf16)
        g_lo = (rest - g_mid.astype(f32)).astype(bf16)
        return (jnp.dot(tri, g_hi, preferred_element_type=f32) + jnp.dot(tri, g_mid, preferred_element_type=f32)
                + jnp.dot(tri, g_lo, preferred_element_type=f32))

    def advance(r0, st, qd, o_intra, upd, decay_row):
        o_inter = lax.dot_general(qd, st.astype(bf16), (((1,), (1,)), ((), ())), preferred_element_type=f32)
        og = bg_ref[pl.ds(r0, C), :]
        y = _rms(o_inter + o_intra, gain) * (og * _sigmoid(og))
        o_ref[pl.ds(r0, C), :] = y.astype(o_ref.dtype)
        return st * decay_row + upd

    f_all = lb + (1.0 - lb) * _sigmoid(bf_ref[...])
    g_all = jnp.log(f_all)
    decay = jnp.sum(g_all.reshape(nC, C, B_KEY_DIM), axis=1)
    safe = jnp.min(decay) >= -HGRN_SAFE_DECAY

    @pl.when(safe)
    def _():
        qr = bq_ref[...]
        qq = qr * _sigmoid(qr) * (B_KEY_DIM ** -0.5)
        kk = 1.0 - f_all
        GR = HGRN_GROUP * C
        r_id = lax.broadcasted_iota(jnp.int32, (GR, GR), 0)
        c_id = lax.broadcasted_iota(jnp.int32, (GR, GR), 1)
        tri_group = jnp.where((r_id // C == c_id // C) & (c_id <= r_id), 1.0, 0.0)
        tri_group_bf = tri_group.astype(bf16)
        g_hi = g_all.astype(bf16)
        rest = g_all - g_hi.astype(f32)
        g_mid = rest.astype(bf16)
        g_lo = (rest - g_mid.astype(f32)).astype(bf16)
        g_cat = jnp.concatenate([g_hi, g_mid, g_lo], axis=1)
        b_parts = []
        for gi in range(R // GR):
            bc = jnp.dot(tri_group_bf, g_cat[gi * GR:(gi + 1) * GR], preferred_element_type=f32)
            b_parts.append(bc[:, :B_KEY_DIM] + bc[:, B_KEY_DIM:2 * B_KEY_DIM] + bc[:, 2 * B_KEY_DIM:])
        b = jnp.concatenate(b_parts, axis=0)
        b_end = jnp.concatenate([jnp.broadcast_to(b[(c + 1) * C - 1:(c + 1) * C], (C, B_KEY_DIM))
                                 for c in range(nC)], axis=0)
        qd_s[...] = (qq * jnp.exp(b)).astype(bf16)
        kd = kk * jnp.exp(-b)
        kl_s[...] = (kk * jnp.exp(b_end - b)).astype(bf16)
        vv = bi_ref[...]
        vv_s[...] = vv.astype(bf16)
        dec_s[...] = jnp.exp(b_end)
        for gi in range(R // GR):
            kdT_s[gi] = kd[gi * GR:(gi + 1) * GR].T.astype(bf16)
        for c in range(nC):
            vvT_s[c] = vv[c * C:(c + 1) * C].T.astype(bf16)
        for gi in range(R // GR):
            rows = slice(gi * GR, (gi + 1) * GR)
            att = jnp.dot(qd_s[rows], kdT_s[gi], preferred_element_type=f32) * tri_group
            oi_s[rows] = jnp.dot(att.astype(bf16), vv_s[rows], preferred_element_type=f32)
        for c in range(nC):
            rows = slice(c * C, (c + 1) * C)
            upd_s[c] = jnp.dot(vvT_s[c], kl_s[rows], preferred_element_type=f32)
        st = st_ref[...]
        for c in range(nC):
            st_s[c] = st.T.astype(bf16)
            st = st * dec_s[c * C:c * C + 1] + upd_s[c]
        st_ref[...] = st
        for c in range(nC):
            rows = slice(c * C, (c + 1) * C)
            oi_s[rows] = oi_s[rows] + jnp.dot(qd_s[rows], st_s[c], preferred_element_type=f32)
        og = bg_ref[...]
        o_ref[...] = (_rms(oi_s[...], gain) * (og * _sigmoid(og))).astype(o_ref.dtype)

    @pl.when(jnp.logical_not(safe))
    def _():
        def body(c, st):
            r0 = pl.multiple_of(c * C, C)
            g, kk, qq, vv = gates(r0)
            b = cumdecay(g)
            b_last = b[C - 1:C, :]
            b_scr[...] = b
            q_scr[...] = qq
            k_scr[...] = kk
            v_scr[...] = vv

            def row(t, _):
                bt = b_scr[pl.ds(t, 1), :]
                qt = q_scr[pl.ds(t, 1), :]
                ex = jnp.where(srow <= t, bt - b_scr[...], -jnp.inf)
                a = jnp.sum(qt * k_scr[...] * jnp.exp(ex), axis=1, keepdims=True)
                oi_scr[pl.ds(t, 1), :] = jnp.sum(a * v_scr[...], axis=0, keepdims=True)
                return 0
            lax.fori_loop(0, C, row, 0)
            kd_last = (kk * jnp.exp(b_last - b)).astype(bf16)
            upd = lax.dot_general(vv.astype(bf16), kd_last, (((0,), (0,)), ((), ())),
                                  preferred_element_type=f32)
            return advance(r0, st, (qq * jnp.exp(b)).astype(bf16), oi_scr[...], upd, jnp.exp(b_last))
        st_ref[...] = lax.fori_loop(0, nC, body, st_ref[...])


def _hgrn(bq, bf, bi, bg, lb, gain, B, S):
    T = bq.shape[0]
    R = min(HGRN_ROWS, S)
    nR = S // R
    C = HGRN_CHUNK
    blk = pl.BlockSpec((R, B_KEY_DIM), lambda b, h, c: (b * nR + c, h))
    small = pl.BlockSpec((B_HEADS, B_KEY_DIM), lambda b, h, c: (0, 0))
    f32 = jnp.float32
    return pl.pallas_call(
        _hgrn_kernel,
        grid=(B, B_HEADS, nR),
        in_specs=[blk, blk, blk, blk, small, small],
        out_specs=blk,
        out_shape=jax.ShapeDtypeStruct((T, B_WIDTH), jnp.bfloat16),
        scratch_shapes=[pltpu.VMEM((B_VAL_DIM, B_KEY_DIM), f32)] +
                       [pltpu.VMEM((C, B_KEY_DIM), f32) for _ in range(5)] +
                       [pltpu.VMEM((R, B_KEY_DIM), jnp.bfloat16) for _ in range(3)] +
                       [pltpu.VMEM((R, B_KEY_DIM), f32) for _ in range(2)] +
                       [pltpu.VMEM((R // C, B_VAL_DIM, B_KEY_DIM), f32),
                        pltpu.VMEM((R // C, B_KEY_DIM, B_VAL_DIM), jnp.bfloat16),
                        pltpu.VMEM((R // (HGRN_GROUP * C), B_KEY_DIM, HGRN_GROUP * C), jnp.bfloat16),
                        pltpu.VMEM((R // C, B_VAL_DIM, C), jnp.bfloat16)],
        compiler_params=_cparams(("parallel", "parallel", "arbitrary")),
        name="hgrn",
    )(bq, bf, bi, bg, lb, gain)


def _merge_kernel(x_ref, ya_ref, yb_ref, ga_ref, gb_ref, wa_ref, wb_ref, wo_ref, g_ref, wrh_ref, wrl_ref, br_ref,
                  x1_ref, xn_ref, lg_ref):
    f32 = jnp.float32
    ma = jnp.dot(ya_ref[...], wa_ref[...], preferred_element_type=f32)
    mb = jnp.dot(yb_ref[...], wb_ref[...], preferred_element_type=f32)
    merged = ga_ref[...].astype(f32) * ma + gb_ref[...].astype(f32) * mb
    x1 = x_ref[...] + jnp.dot(merged.astype(jnp.bfloat16), wo_ref[...], preferred_element_type=f32)
    x1_ref[...] = x1
    hn = _rms(x1, g_ref[...])
    xn_ref[...] = _pack_bf16_pairs(hn)
    hn_hi = hn.astype(jnp.bfloat16)
    hn_lo = (hn - hn_hi.astype(f32)).astype(jnp.bfloat16)
    lg_ref[...] = (jnp.dot(hn_hi, wrh_ref[...], preferred_element_type=f32)
                   + jnp.dot(hn_lo, wrh_ref[...], preferred_element_type=f32)
                   + jnp.dot(hn_hi, wrl_ref[...], preferred_element_type=f32) + br_ref[...])


def _merge(x2, ya, yb, ga, gb, w_up_a, w_up_b, w_out, gain, w_router, b_router):
    T, D = x2.shape
    R = min(PROJ_ROWS, T)
    bf = jnp.bfloat16
    wr_hi = w_router.astype(bf)
    wr_lo = (w_router - wr_hi.astype(jnp.float32)).astype(bf)
    ins = [x2, ya, yb, ga, gb, w_up_a.astype(bf), w_up_b.astype(bf), w_out.astype(bf),
           gain.reshape(1, D), wr_hi, wr_lo, b_router.reshape(1, N_EXPERTS)]
    row = lambda n: pl.BlockSpec((R, n), lambda i: (i, 0))
    full = lambda a: pl.BlockSpec(a.shape, lambda i: (0,) * a.ndim)
    in_specs = [row(D), row(A_WIDTH), row(B_WIDTH), row(D), row(D)] + [full(a) for a in ins[5:]]
    return pl.pallas_call(
        _merge_kernel,
        grid=(T // R,),
        in_specs=in_specs,
        out_specs=[row(D), row(D // 2), row(N_EXPERTS)],
        out_shape=[jax.ShapeDtypeStruct((T, D), jnp.float32), jax.ShapeDtypeStruct((T, D // 2), jnp.uint32),
                   jax.ShapeDtypeStruct((T, N_EXPERTS), jnp.float32)],
        compiler_params=_cparams(("parallel",)),
        name="merge",
    )(*ins)


def _route_kernel(lg_ref, eidx_ref, gate_ref, rank_ref, cnt_ref, run_ref):
    R = lg_ref.shape[0]
    f32 = jnp.float32

    @pl.when(pl.program_id(0) == 0)
    def _():
        run_ref[...] = jnp.zeros_like(run_ref)

    lg = lg_ref[...].T
    expert = lax.broadcasted_iota(jnp.int32, (N_EXPERTS, R), 0)
    work = lg
    onehots, vals, idxs = [], [], []
    for _ in range(TOP_K):
        m = jnp.max(work, axis=0, keepdims=True)
        idx = jnp.min(jnp.where(work == m, expert, N_EXPERTS), axis=0, keepdims=True)
        oh = expert == idx
        onehots.append(oh)
        vals.append(m)
        idxs.append(idx)
        work = jnp.where(oh, -jnp.inf, work)
    ex = [jnp.exp(v - vals[0]) for v in vals]
    den = ex[0] + ex[1] + ex[2] + ex[3]
    chosen = jnp.where(onehots[0] | onehots[1] | onehots[2] | onehots[3], 1.0, 0.0)
    earlier = jnp.where(lax.broadcasted_iota(jnp.int32, (R, R), 0)
                        < lax.broadcasted_iota(jnp.int32, (R, R), 1), 1.0, 0.0).astype(jnp.bfloat16)
    before = jnp.dot(chosen.astype(jnp.bfloat16), earlier, preferred_element_type=f32) + run_ref[...]
    slot = lax.broadcasted_iota(jnp.int32, (TOP_K, R), 0)
    eidx = jnp.zeros((TOP_K, R), jnp.int32)
    gate = jnp.zeros((TOP_K, R), f32)
    rank = jnp.zeros((TOP_K, R), f32)
    for k in range(TOP_K):
        eidx = jnp.where(slot == k, idxs[k], eidx)
        gate = jnp.where(slot == k, ex[k] / den, gate)
        rk = jnp.sum(jnp.where(onehots[k], before, 0.0), axis=0, keepdims=True)
        rank = jnp.where(slot == k, rk, rank)
    eidx_ref[...] = eidx
    gate_ref[...] = gate
    rank_ref[...] = rank.astype(jnp.int32)
    run_ref[...] = run_ref[...] + jnp.sum(chosen, axis=1, keepdims=True)
    cnt_ref[...] = run_ref[...].astype(jnp.int32)


def _route(logits):
    T = logits.shape[0]
    R = min(ROUTE_ROWS, T)
    col = pl.BlockSpec((TOP_K, R), lambda i: (0, i))
    return pl.pallas_call(
        _route_kernel,
        grid=(T // R,),
        in_specs=[pl.BlockSpec((R, N_EXPERTS), lambda i: (i, 0))],
        out_specs=[col, col, col, pl.BlockSpec((N_EXPERTS, 1), lambda i: (0, 0))],
        out_shape=[jax.ShapeDtypeStruct((TOP_K, T), jnp.int32), jax.ShapeDtypeStruct((TOP_K, T), jnp.float32),
                   jax.ShapeDtypeStruct((TOP_K, T), jnp.int32), jax.ShapeDtypeStruct((N_EXPERTS, 1), jnp.int32)],
        scratch_shapes=[pltpu.VMEM((N_EXPERTS, 1), jnp.float32)],
        compiler_params=_cparams(("arbitrary",)),
        name="route",
    )(logits)


def _sc_gather_rows(table, idx):
    N, W = table.shape
    M = idx.shape[0]
    workers = V7X_SC_CORES * V7X_SC_SUBCORES
    per_worker = M // workers
    pieces = per_worker // SC_GATHER_ROWS
    assert per_worker * workers == M and pieces * SC_GATHER_ROWS == per_worker and pieces % 2 == 0
    mesh = plsc.VectorSubcoreMesh(core_axis_name="core", subcore_axis_name="subcore",
                                  num_cores=V7X_SC_CORES, num_subcores=V7X_SC_SUBCORES)

    @functools.partial(
        pl.kernel, mesh=mesh,
        out_type=jax.ShapeDtypeStruct((M, W), table.dtype),
        scratch_types=[pltpu.VMEM((per_worker,), jnp.int32),
                       pltpu.VMEM((SC_GATHER_ROWS, W), table.dtype),
                       pltpu.VMEM((SC_GATHER_ROWS, W), table.dtype),
                       pltpu.SemaphoreType.DMA, pltpu.SemaphoreType.DMA],
    )
    def gather(table_hbm, idx_hbm, out_hbm, idx_v, rows_a, rows_b, sem_a, sem_b):
        worker = lax.axis_index("subcore") * V7X_SC_CORES + lax.axis_index("core")
        base = pl.multiple_of(worker * per_worker, SC_GATHER_ROWS)
        pltpu.sync_copy(idx_hbm.at[pl.ds(base, per_worker)], idx_v)
        bufs = ((rows_a, sem_a), (rows_b, sem_b))

        def fetch(g, buf, sem):
            off = pl.multiple_of(g * SC_GATHER_ROWS, SC_GATHER_ROWS)
            return pltpu.make_async_copy(table_hbm.at[idx_v.at[pl.ds(off, SC_GATHER_ROWS)]], buf, sem)

        fetch(0, *bufs[0]).start()

        @pl.loop(0, pieces, step=2)
        def _(g0):
            for half in range(2):
                g = g0 + half
                buf, sem = bufs[half]
                fetch(g, buf, sem).wait()

                @pl.when(g + 1 < pieces)
                def _():
                    fetch(g + 1, *bufs[1 - half]).start()

                off = pl.multiple_of(g * SC_GATHER_ROWS, SC_GATHER_ROWS)
                pltpu.sync_copy(buf, out_hbm.at[pl.ds(base + off, SC_GATHER_ROWS)])

    return gather(table, idx)


def _sc_scatter_rows(rows, dest, n_out):
    T, W = rows.shape
    slots = dest.shape[0]
    workers = V7X_SC_CORES * V7X_SC_SUBCORES
    per_worker = T // workers
    pieces = per_worker // SC_GATHER_ROWS
    assert per_worker * workers == T and pieces * SC_GATHER_ROWS == per_worker and pieces % 2 == 0
    idx = dest.reshape(slots, workers, pieces, SC_GATHER_ROWS).transpose(1, 2, 0, 3)
    mesh = plsc.VectorSubcoreMesh(core_axis_name="core", subcore_axis_name="subcore",
                                  num_cores=V7X_SC_CORES, num_subcores=V7X_SC_SUBCORES)

    @functools.partial(
        pl.kernel, mesh=mesh,
        out_type=jax.ShapeDtypeStruct((n_out, W), rows.dtype),
        scratch_types=[pltpu.VMEM((pieces, slots, SC_GATHER_ROWS), jnp.int32),
                       pltpu.VMEM((SC_GATHER_ROWS, W), rows.dtype),
                       pltpu.VMEM((SC_GATHER_ROWS, W), rows.dtype),
                       pltpu.SemaphoreType.DMA, pltpu.SemaphoreType.DMA, pltpu.SemaphoreType.DMA],
    )
    def scatter(rows_hbm, idx_hbm, out_hbm, idx_v, rows_a, rows_b, sem_a, sem_b, sem_out):
        worker = lax.axis_index("subcore") * V7X_SC_CORES + lax.axis_index("core")
        base = pl.multiple_of(worker * per_worker, SC_GATHER_ROWS)
        pltpu.sync_copy(idx_hbm.at[worker], idx_v)
        bufs = ((rows_a, sem_a), (rows_b, sem_b))

        def fetch(g, buf, sem):
            off = pl.multiple_of(g * SC_GATHER_ROWS, SC_GATHER_ROWS)
            return pltpu.make_async_copy(rows_hbm.at[pl.ds(base + off, SC_GATHER_ROWS)], buf, sem)

        fetch(0, *bufs[0]).start()

        @pl.loop(0, pieces, step=2)
        def _(g0):
            for half in range(2):
                g = g0 + half
                buf, sem = bufs[half]
                fetch(g, buf, sem).wait()

                @pl.when(g + 1 < pieces)
                def _():
                    fetch(g + 1, *bufs[1 - half]).start()

                puts = [pltpu.make_async_copy(buf, out_hbm.at[idx_v.at[g, k]], sem_out) for k in range(slots)]
                for put in puts:
                    put.start()
                for put in puts:
                    put.wait()

    return scatter(rows, idx)


def _experts_kernel(be_ref, nb_ref, first_ref, slot_ref, next_ref,
                    x_ref, wgu_hbm, bgu_ref, wd_hbm, bd_ref, o_ref,
                    wgu_f32, wd_f32, wgu_bf, wd_bf, sems):
    f32 = jnp.float32
    d_ff = wd_bf.shape[0]
    i = pl.program_id(0)
    live = i < nb_ref[0]

    def weight_copies(e, s):
        return (pltpu.make_async_copy(wgu_hbm.at[e], wgu_f32.at[s], sems.at[s, 0]),
                pltpu.make_async_copy(wd_hbm.at[e], wd_f32.at[s], sems.at[s, 1]))

    @pl.when(live & (first_ref[i] == 1))
    def _():
        e = be_ref[i]
        s = slot_ref[i]

        @pl.when(i == 0)
        def _():
            for cp in weight_copies(e, s):
                cp.start()

        for cp in weight_copies(e, s):
            cp.wait()

        @pl.when(next_ref[i] >= 0)
        def _():
            for cp in weight_copies(next_ref[i], 1 - s):
                cp.start()

        wgu_bf[...] = wgu_f32[s].astype(jnp.bfloat16)
        wd_bf[...] = wd_f32[s].astype(jnp.bfloat16)

    @pl.when(live)
    def _():
        x_hi, x_lo = _unpack_bf16_pairs(x_ref[...])
        half = x_hi.shape[1]
        gu = (jnp.dot(x_hi, wgu_bf[:half, :], preferred_element_type=f32)
              + jnp.dot(x_lo, wgu_bf[half:, :], preferred_element_type=f32) + bgu_ref[0])
        gate = jnp.minimum(gu[:, :d_ff], SWIGLU_LIMIT)
        lin = jnp.clip(gu[:, d_ff:], -SWIGLU_LIMIT, SWIGLU_LIMIT)
        act = (lin + 1.0) * gate * _sigmoid(SWIGLU_ALPHA * gate)
        y = jnp.dot(act.astype(jnp.bfloat16), wd_bf[...], preferred_element_type=f32) + bd_ref[0]
        o_ref[...] = _pack_bf16_pairs(y)

    @pl.when(pl.program_id(0) >= nb_ref[0])
    def _():
        o_ref[...] = jnp.zeros_like(o_ref)


def _experts(xs, plan, w_gu, b_gu, w_down, b_down):
    P, W = xs.shape
    E, D, F2 = w_gu.shape
    nb = P // EXPERT_ROWS
    by_expert = lambda i, be, *_: (be[i], 0, 0)
    grid_spec = pltpu.PrefetchScalarGridSpec(
        num_scalar_prefetch=5,
        grid=(nb,),
        in_specs=[
            pl.BlockSpec((EXPERT_ROWS, W), lambda i, *_: (i, 0)),
            pl.BlockSpec(memory_space=pl.ANY),
            pl.BlockSpec((1, 1, F2), by_expert),
            pl.BlockSpec(memory_space=pl.ANY),
            pl.BlockSpec((1, 1, D), by_expert),
        ],
        out_specs=pl.BlockSpec((EXPERT_ROWS, W), lambda i, *_: (i, 0)),
        scratch_shapes=[pltpu.VMEM((2, D, F2), jnp.float32), pltpu.VMEM((2, F2 // 2, D), jnp.float32),
                        pltpu.VMEM((D, F2), jnp.bfloat16), pltpu.VMEM((F2 // 2, D), jnp.bfloat16),
                        pltpu.SemaphoreType.DMA((2, 2))],
    )
    return pl.pallas_call(
        _experts_kernel,
        grid_spec=grid_spec,
        out_shape=jax.ShapeDtypeStruct((P, W), jnp.uint32),
        compiler_params=_cparams(("arbitrary",)),
        name="experts",
    )(*plan, xs, w_gu, b_gu.reshape(E, 1, F2), w_down, b_down.reshape(E, 1, D))


def _combine_kernel(ya_ref, x1_ref, gate_ref, g_ref, o_ref):
    half = x1_ref.shape[1] // 2
    f32 = jnp.float32
    gate = gate_ref[...].T
    x1 = x1_ref[...]
    y_hi = x1[:, :half]
    y_lo = x1[:, half:]
    for k in range(TOP_K):
        hi, lo = _unpack_bf16_pairs(ya_ref[k])
        y_hi = y_hi + gate[:, k:k + 1] * hi.astype(f32)
        y_lo = y_lo + gate[:, k:k + 1] * lo.astype(f32)
    o_ref[...] = _rms(jnp.concatenate([y_hi, y_lo], axis=1), g_ref[...])


def _combine(ya, x1, gates, gain):
    T, D = x1.shape
    R = min(COMBINE_ROWS, T)
    row = lambda w: pl.BlockSpec((R, w), lambda i: (i, 0))
    return pl.pallas_call(
        _combine_kernel,
        grid=(T // R,),
        in_specs=[pl.BlockSpec((TOP_K, R, D // 2), lambda i: (0, i, 0)), row(D),
                  pl.BlockSpec((TOP_K, R), lambda i: (0, i)), pl.BlockSpec((1, D), lambda i: (0, 0))],
        out_specs=row(D),
        out_shape=jax.ShapeDtypeStruct((T, D), jnp.float32),
        compiler_params=_cparams(("parallel",)),
        name="combine",
    )(ya, x1, gates, gain.reshape(1, D))


def _moe_plan(eidx, rank, counts, A):
    counts = counts.reshape(N_EXPERTS)
    padded = (counts + EXPERT_ROWS - 1) // EXPERT_ROWS * EXPERT_ROWS
    pad_ends = jnp.cumsum(padded)
    pad_starts = pad_ends - padded
    n_blocks = -(-A // EXPERT_ROWS) + N_EXPERTS
    ids = jnp.arange(N_EXPERTS, dtype=jnp.int32)
    dest = rank + jnp.sum(jnp.where(eidx[None] == ids[:, None, None], pad_starts[:, None, None], 0), axis=0)
    block_start = jnp.arange(n_blocks, dtype=pad_ends.dtype) * EXPERT_ROWS
    block_expert = jnp.minimum(jnp.sum(pad_ends[None, :] <= block_start[:, None], axis=1),
                               N_EXPERTS - 1).astype(jnp.int32)
    n_used = (pad_ends[-1] // EXPERT_ROWS).astype(jnp.int32).reshape(1)
    has_rows = counts > 0
    ordinal = jnp.cumsum(has_rows.astype(jnp.int32)) - 1
    later = has_rows[None, :] & (ids[None, :] > ids[:, None])
    next_expert = jnp.where(jnp.any(later, axis=1), jnp.argmax(later, axis=1), -1).astype(jnp.int32)
    tables = jnp.stack([pad_starts, ordinal % 2, next_expert], axis=1).astype(jnp.float32)
    onehot = (block_expert[:, None] == ids[None, :]).astype(jnp.float32)
    looked = jnp.dot(onehot, tables, precision=lax.Precision.HIGHEST).astype(jnp.int32)
    block_first = ((block_start == looked[:, 0]) & (block_start < pad_ends[-1])).astype(jnp.int32)
    plan = (block_expert, n_used, block_first, looked[:, 1], looked[:, 2])
    return dest.astype(jnp.int32), plan, n_blocks


def kernel(x, w_in, w_up_a, w_up_b, w_out, norm_mix, norm_ffn, norm_final, hgrn_norm,
           lb_logits, rel_bias, w_router, b_router, w_gu, b_gu, w_down, b_down):
    B, S, D = x.shape
    T = B * S
    assert w_in.shape[0] == 1, "the final rmsnorm is fused into the single layer's combine stage"
    lb_all = jnp.cumsum(jax.nn.softmax(lb_logits.astype(jnp.float32), axis=0), axis=0)
    x2 = x.reshape(T, D)
    (k, ik, qT, vT, iqT, iwT, bq, bf, bi, bg, ga, gb) = _inproj(x2, norm_mix[0], w_in[0], B, S)
    ya = _dsa(k, ik, qT, vT, iqT, iwT, rel_bias, B, S)
    yb = _hgrn(bq, bf, bi, bg, lb_all[0].reshape(B_HEADS, B_KEY_DIM), hgrn_norm[0], B, S)
    x1, xn, logits = _merge(x2, ya, yb, ga, gb, w_up_a[0], w_up_b[0], w_out[0], norm_ffn[0],
                            w_router[0], b_router[0])
    eidx, gates, rank, counts = _route(logits)
    dest, plan, n_blocks = _moe_plan(eidx, rank, counts, T * TOP_K)
    P = n_blocks * EXPERT_ROWS
    A = T * TOP_K
    xs = _sc_scatter_rows(xn, dest, P)
    y_buf = _experts(xs, plan, w_gu[0], b_gu[0], w_down[0], b_down[0])
    ya = _sc_gather_rows(y_buf, dest.reshape(A)).reshape(TOP_K, T, D // 2)
    out = _combine(ya, x1, gates, norm_final)
    return out.reshape(B, S, D)
```

```python
import functools
import math

import numpy as np
import jax
import jax.numpy as jnp
from jax import lax
from jax.experimental import pallas as pl
from jax.experimental.pallas import tpu as pltpu
from jax.experimental.pallas import tpu_sc as plsc

A_HEADS = 8
A_HEAD_DIM = 64
IDX_HEADS = 8
IDX_DIM = 32
TOPK_MAX = 256
REL_BUCKETS = 32
REL_MAX_DIST = 128
B_HEADS = 4
B_KEY_DIM = 128
B_VAL_DIM = 128
N_EXPERTS = 32
TOP_K = 4
SWIGLU_LIMIT = 7.0
SWIGLU_ALPHA = 1.702
EPS = 1e-6
LOG2_E = math.log2(math.e)

A_WIDTH = A_HEADS * A_HEAD_DIM
B_WIDTH = B_HEADS * B_VAL_DIM
IDX_WIDTH = IDX_HEADS * IDX_DIM

V7X_LANES = 128
V7X_SUBLANES = 8
V7X_VMEM_LIMIT_BYTES = 56 * 1024 * 1024
V7X_SC_CORES = 2
V7X_SC_SUBCORES = 16

PROJ_ROWS = 512
ATT_Q = 256
ATT_KC = 128
SCORE_UNROLL = 4
COUNT_UNROLL = 2
FAR_UNROLL = 4
HGRN_ROWS = 1024
HGRN_CHUNK = 64
HGRN_GROUP = 4
HGRN_SAFE_DECAY = 70.0
ROUTE_ROWS = 512
EXPERT_ROWS = 512
COMBINE_ROWS = 512
SC_GATHER_ROWS = 64
MASK_NEG = -1e30
BISECT_FAST_ITERS = 26
BISECT_BLIND_ITERS = 12


def _cparams(dims):
    return pltpu.CompilerParams(dimension_semantics=dims, vmem_limit_bytes=V7X_VMEM_LIMIT_BYTES)


def _rms(x, gain):
    return x * lax.rsqrt(jnp.mean(x * x, axis=-1, keepdims=True) + EPS) * gain


def _sigmoid(x):
    return 1.0 / (1.0 + jnp.exp(-x))


def _pack_bf16_pairs(x):
    n = x.shape[1] // 2
    as_bits = lambda v: lax.bitcast_convert_type(v.astype(jnp.bfloat16).astype(jnp.float32), jnp.uint32)
    return (as_bits(x[:, :n]) & jnp.uint32(0xFFFF0000)) | (as_bits(x[:, n:]) >> 16)


def _unpack_bf16_pairs(w):
    hi = lax.bitcast_convert_type(w & jnp.uint32(0xFFFF0000), jnp.float32).astype(jnp.bfloat16)
    lo = lax.bitcast_convert_type(w << 16, jnp.float32).astype(jnp.bfloat16)
    return hi, lo


def _blocked_loop(n, step, carry, unroll):
    def run(first, count, cr):
        for t in range(count):
            cr = step(first + t, cr)
        return cr

    carry = lax.fori_loop(0, n // unroll, lambda j, cr: run(j * unroll, unroll, cr), carry)
    base = (n // unroll) * unroll
    piece = unroll // 2
    while piece >= 1:
        take = (n & piece) != 0
        carry = lax.cond(take, functools.partial(run, base, piece), lambda cr: cr, carry)
        base = base + jnp.where(take, piece, 0)
        piece //= 2
    return carry


def _fold_rows(x, op):
    return op(x.reshape(x.shape[0] // V7X_SUBLANES, V7X_SUBLANES, x.shape[1]), axis=0)


def _inproj_kernel(x_ref, g_ref, wk_ref, wik_ref, wqT_ref, wvT_ref, wiqT_ref, wiwT_ref, wb_ref, wg_ref,
                   k_ref, ik_ref, qT_ref, vT_ref, iqT_ref, iwT_ref, bq_ref, bf_ref, bi_ref, bg_ref,
                   ga_ref, gb_ref):
    x = x_ref[...]
    hn = _rms(x, g_ref[...]).astype(jnp.bfloat16)

    def mm(w_ref):
        return jnp.dot(hn, w_ref[...], preferred_element_type=jnp.float32)

    def mm_t(w_ref):
        return lax.dot_general(w_ref[...], hn, (((1,), (1,)), ((), ())),
                               preferred_element_type=jnp.float32)

    k_ref[...] = mm(wk_ref).astype(jnp.bfloat16)
    ik_ref[...] = mm(wik_ref).astype(jnp.bfloat16)
    qT_ref[0] = (mm_t(wqT_ref) * (A_HEAD_DIM ** -0.5 * LOG2_E)).astype(jnp.bfloat16)
    vT_ref[0] = mm_t(wvT_ref).astype(jnp.bfloat16)
    iqT_ref[0] = mm_t(wiqT_ref).astype(jnp.bfloat16)
    iwT_ref[0] = mm_t(wiwT_ref) * ((IDX_HEADS * IDX_DIM) ** -0.5)
    hb = mm(wb_ref)
    bq_ref[...] = hb[:, 0 * B_WIDTH:1 * B_WIDTH]
    bf_ref[...] = hb[:, 1 * B_WIDTH:2 * B_WIDTH]
    bi_ref[...] = hb[:, 2 * B_WIDTH:3 * B_WIDTH]
    bg_ref[...] = hb[:, 3 * B_WIDTH:4 * B_WIDTH]
    d = ga_ref.shape[-1]
    hg = mm(wg_ref)
    ga_ref[...] = _sigmoid(hg[:, :d]).astype(jnp.bfloat16)
    gb_ref[...] = _sigmoid(hg[:, d:]).astype(jnp.bfloat16)


def _inproj(x2, gain, w_in, B, S):
    T, D = x2.shape
    R = min(PROJ_ROWS, S)
    nS = S // R
    o = np.cumsum((0, A_WIDTH, A_WIDTH, A_WIDTH, IDX_WIDTH, IDX_HEADS, IDX_DIM,
                   B_WIDTH, B_WIDTH, B_WIDTH, B_WIDTH, D, D))
    bf = jnp.bfloat16
    wqT = w_in[:, o[0]:o[1]].T.astype(bf)
    wk = w_in[:, o[1]:o[2]].astype(bf)
    wvT = w_in[:, o[2]:o[3]].T.astype(bf)
    wiqT = w_in[:, o[3]:o[4]].T.astype(bf)
    wiwT = w_in[:, o[4]:o[5]].T.astype(bf)
    wik = w_in[:, o[5]:o[6]].astype(bf)
    wb = w_in[:, o[6]:o[10]].astype(bf)
    wg = w_in[:, o[10]:o[12]].astype(bf)

    def full(a):
        return pl.BlockSpec(a.shape, lambda b, i: (0,) * a.ndim)

    row = lambda n: pl.BlockSpec((R, n), lambda b, i: (b * nS + i, 0))
    colT = lambda n: pl.BlockSpec((1, n, R), lambda b, i: (b, 0, i))
    f32 = jnp.float32
    outs = [
        (jax.ShapeDtypeStruct((T, A_WIDTH), bf), row(A_WIDTH)),
        (jax.ShapeDtypeStruct((T, IDX_DIM), bf), row(IDX_DIM)),
        (jax.ShapeDtypeStruct((B, A_WIDTH, S), bf), colT(A_WIDTH)),
        (jax.ShapeDtypeStruct((B, A_WIDTH, S), bf), colT(A_WIDTH)),
        (jax.ShapeDtypeStruct((B, IDX_WIDTH, S), bf), colT(IDX_WIDTH)),
        (jax.ShapeDtypeStruct((B, IDX_HEADS, S), f32), colT(IDX_HEADS)),
        (jax.ShapeDtypeStruct((T, B_WIDTH), f32), row(B_WIDTH)),
        (jax.ShapeDtypeStruct((T, B_WIDTH), f32), row(B_WIDTH)),
        (jax.ShapeDtypeStruct((T, B_WIDTH), f32), row(B_WIDTH)),
        (jax.ShapeDtypeStruct((T, B_WIDTH), f32), row(B_WIDTH)),
        (jax.ShapeDtypeStruct((T, D), bf), row(D)),
        (jax.ShapeDtypeStruct((T, D), bf), row(D)),
    ]
    ins = [x2, gain.reshape(1, D), wk, wik, wqT, wvT, wiqT, wiwT, wb, wg]
    in_specs = [row(D)] + [full(a) for a in ins[1:]]
    return pl.pallas_call(
        _inproj_kernel,
        grid=(B, nS),
        in_specs=in_specs,
        out_specs=[s for _, s in outs],
        out_shape=[s for s, _ in outs],
        compiler_params=_cparams(("parallel", "parallel")),
        name="inproj",
    )(*ins)


def _t5_bucket_table(n):
    d = np.arange(n)
    max_exact = REL_BUCKETS // 2
    nf = np.maximum(d, 1).astype(np.float64)
    large = max_exact + (np.log(nf / max_exact) / math.log(REL_MAX_DIST / max_exact)
                         * (REL_BUCKETS - max_exact)).astype(np.int32)
    large = np.minimum(large, REL_BUCKETS - 1)
    return np.where(d < max_exact, d, large)


def _dsa_kernel(qT_ref, k_ref, vT_ref, iqT_ref, iwT_ref, ik_ref, enear_ref,
                o_ref, sc_ref, qh_scr, m_scr, acc_scr, *, topk):
    TQ = qT_ref.shape[2]
    KC = TQ
    i = pl.program_id(1)
    nch = i + 1
    q0 = i * TQ
    f32 = jnp.float32
    bf16 = jnp.bfloat16
    key_id = lax.broadcasted_iota(jnp.int32, (KC, TQ), 0)
    qry_id = lax.broadcasted_iota(jnp.int32, (KC, TQ), 1)

    def col_reduce(x, op):
        return op(_fold_rows(x, op), axis=0, keepdims=True)

    iw = iwT_ref[0]

    def score_chunk(c, carry):
        rmin, rmax = carry
        k0 = pl.multiple_of(c * KC, KC)
        ik = ik_ref[pl.ds(k0, KC), :]
        acc = jnp.zeros((KC, TQ), f32)
        for h in range(IDX_HEADS):
            sh = jnp.dot(ik, iqT_ref[0, h * IDX_DIM:(h + 1) * IDX_DIM, :], preferred_element_type=f32)
            acc = acc + jnp.maximum(sh, 0.0) * iw[h:h + 1, :]
        valid = (k0 + key_id) <= (q0 + qry_id)
        sc_ref[pl.ds(k0, KC), :] = jnp.where(valid, acc, MASK_NEG)
        rmin = jnp.minimum(rmin, _fold_rows(jnp.where(valid, acc, -MASK_NEG), jnp.min))
        rmax = jnp.maximum(rmax, _fold_rows(jnp.where(valid, acc, MASK_NEG), jnp.max))
        return rmin, rmax

    rmin8, rmax8 = _blocked_loop(
        nch, score_chunk,
        (jnp.full((V7X_SUBLANES, TQ), -MASK_NEG, f32), jnp.full((V7X_SUBLANES, TQ), MASK_NEG, f32)),
        SCORE_UNROLL)
    rmin = jnp.min(rmin8, axis=0, keepdims=True)
    rmax = jnp.max(rmax8, axis=0, keepdims=True)

    def count_where(pred_fn):
        def one(c, acc):
            k0 = pl.multiple_of(c * KC, KC)
            blk = sc_ref[pl.ds(k0, KC), :]
            return acc + _fold_rows(jnp.where(pred_fn(blk), 1.0, 0.0), jnp.sum)
        acc = _blocked_loop(nch, one, jnp.zeros((V7X_SUBLANES, TQ), f32), COUNT_UNROLL)
        return jnp.sum(acc, axis=0, keepdims=True)

    def band_min_max(lo, hi):
        def body(c, carry):
            bmin, bmax = carry
            k0 = pl.multiple_of(c * KC, KC)
            blk = sc_ref[pl.ds(k0, KC), :]
            bmin = jnp.minimum(bmin, _fold_rows(jnp.where(blk >= lo, blk, -MASK_NEG), jnp.min))
            bmax = jnp.maximum(bmax, _fold_rows(jnp.where(blk < hi, blk, MASK_NEG), jnp.max))
            return bmin, bmax
        bmin8, bmax8 = lax.fori_loop(
            0, nch, body,
            (jnp.full((V7X_SUBLANES, TQ), -MASK_NEG, f32), jnp.full((V7X_SUBLANES, TQ), MASK_NEG, f32)))
        return jnp.min(bmin8, axis=0, keepdims=True), jnp.max(bmax8, axis=0, keepdims=True)

    kf = float(topk)
    n_valid = (q0 + 1 + lax.broadcasted_iota(jnp.int32, (1, TQ), 1)).astype(f32)
    lo0 = rmin
    cnt0 = n_valid
    hi0 = rmax + jnp.maximum(jnp.abs(rmax) * 2.0 ** -20, 1e-30)
    done0 = jnp.where(cnt0 <= kf, 1.0, 0.0)

    def probe(st, lo_s, mid, tie):
        it, lo, hi, cnt, done = st
        active = done < 0.5
        lo_s = jnp.where(active, lo_s, lo)
        c = count_where(lambda blk: blk >= mid)
        feas = c >= kf
        move = active & jnp.logical_not(tie)
        lo_n = jnp.where(move & feas, mid, lo_s)
        cnt_n = jnp.where(move & feas, c, cnt)
        hi_n = jnp.where(move & jnp.logical_not(feas), mid, hi)
        done_n = jnp.where((active & tie) | (cnt_n <= kf), 1.0, done)
        return it + 1, lo_n, hi_n, cnt_n, done_n

    def halve(st):
        _, lo, hi, _, _ = st
        half = lo + 0.5 * (hi - lo)
        stuck = (half <= lo) | (half >= hi)
        return probe(st, lo, half, stuck)

    def snap(st):
        _, lo, hi, _, _ = st
        bmin, bmax = band_min_max(lo, hi)
        mid = bmin + 0.5 * (bmax - bmin)
        return probe(st, bmin, jnp.where(mid <= bmin, bmax, mid), bmax <= bmin)

    st = lax.fori_loop(0, BISECT_BLIND_ITERS // 2, lambda _, st: halve(halve(st)),
                       (jnp.int32(0), lo0, hi0, cnt0, done0))
    st = lax.while_loop(lambda st: (jnp.min(st[-1]) < 0.5) & (st[0] < BISECT_FAST_ITERS),
                        lambda st: halve(halve(st)), st)
    _, thr, _, cnt_thr, _ = lax.cond(st[0] >= BISECT_FAST_ITERS,
                                     lambda st: lax.while_loop(lambda s: jnp.min(s[-1]) < 0.5, snap, st),
                                     lambda st: st, st)

    tie_overflow = jnp.max(cnt_thr) > kf

    @pl.when(jnp.logical_not(tie_overflow))
    def _():
        def mask_chunk(c, _):
            k0 = pl.multiple_of(c * KC, KC)
            sc_ref[pl.ds(k0, KC), :] = jnp.where(sc_ref[pl.ds(k0, KC), :] >= thr, 0.0, MASK_NEG)
            return 0
        lax.fori_loop(0, nch, mask_chunk, 0)

    @pl.when(tie_overflow)
    def _():
        need = kf - count_where(lambda blk: blk > thr)
        tril = jnp.where(lax.broadcasted_iota(jnp.int32, (KC, KC), 1)
                         <= lax.broadcasted_iota(jnp.int32, (KC, KC), 0), 1.0, 0.0).astype(bf16)

        def mask_chunk(c, run):
            k0 = pl.multiple_of(c * KC, KC)
            blk = sc_ref[pl.ds(k0, KC), :]
            eq = jnp.where(blk == thr, 1.0, 0.0)
            pref = jnp.dot(tril, eq.astype(bf16), preferred_element_type=f32)
            sel = (blk > thr) | ((eq > 0.5) & (run + pref <= need))
            sc_ref[pl.ds(k0, KC), :] = jnp.where(sel, 0.0, MASK_NEG)
            return run + pref[KC - 1:KC, :]

        lax.fori_loop(0, nch, mask_chunk, jnp.zeros((1, TQ), f32))

    AK = min(ATT_KC, TQ)
    per = TQ // AK
    head0_q = (lax.broadcasted_iota(jnp.int32, (V7X_LANES, TQ), 0) // A_HEAD_DIM) == 0
    n_pairs = A_HEADS // 2

    m_scr[...] = jnp.full(m_scr.shape, MASK_NEG, f32)
    acc_scr[...] = jnp.zeros(acc_scr.shape, f32)
    v_row = lax.broadcasted_iota(jnp.int32, (V7X_LANES, AK), 0)
    denom_row = [A_HEAD_DIM * (1 - sub) for sub in range(2)]
    for p in range(n_pairs):
        q_pair = qT_ref[0, p * V7X_LANES:(p + 1) * V7X_LANES, :]
        zq = jnp.zeros_like(q_pair)
        qh_scr[2 * p] = jnp.where(head0_q, q_pair, zq)
        qh_scr[2 * p + 1] = jnp.where(head0_q, zq, q_pair)

    def step(c, bias_rows):
        k0 = pl.multiple_of(c * AK, AK)
        msk = sc_ref[pl.ds(k0, AK), :]
        for p in range(n_pairs):
            kp = k_ref[pl.ds(k0, AK), p * V7X_LANES:(p + 1) * V7X_LANES]
            vp = vT_ref[0, p * V7X_LANES:(p + 1) * V7X_LANES, pl.ds(k0, AK)]
            for sub in range(2):
                h = 2 * p + sub
                s = jnp.dot(kp, qh_scr[h], preferred_element_type=f32) + msk
                if bias_rows is not None:
                    s = s + enear_ref[h, bias_rows, :]
                m = m_scr[h:h + 1, :]
                m_new = jnp.maximum(m, col_reduce(s, jnp.max))
                alpha = jnp.exp2(m - m_new)
                pr = jnp.exp2(s - m_new)
                m_scr[h:h + 1, :] = m_new
                v_aug = jnp.where(v_row == denom_row[sub], jnp.ones_like(vp), vp)
                acc_scr[h] = alpha * acc_scr[h] + jnp.dot(v_aug, pr.astype(bf16), preferred_element_type=f32)

    def far(blk, _):
        for jj in range(per):
            step(blk * per + jj, None)
        return 0

    def near(block, first_chunk):
        for jj in range(per):
            step(first_chunk + jj, slice(block * TQ + jj * AK, block * TQ + (jj + 1) * AK))

    _blocked_loop(jnp.maximum(i - 1, 0), far, 0, FAR_UNROLL)

    @pl.when(i >= 1)
    def _():
        near(0, (i - 1) * per)
        near(1, i * per)

    @pl.when(i == 0)
    def _():
        near(1, i * per)
    for p in range(n_pairs):
        outs = [acc_scr[2 * p + sub] / acc_scr[2 * p + sub, denom_row[sub]:denom_row[sub] + 1, :]
                for sub in range(2)]
        o_pair = jnp.where(head0_q, outs[0], outs[1])
        o_ref[:, p * V7X_LANES:(p + 1) * V7X_LANES] = o_pair.T.astype(o_ref.dtype)


def _dsa(k, ik, qT, vT, iqT, iwT, rel_bias, B, S):
    T = k.shape[0]
    TQ = min(ATT_Q, S)
    nQ = S // TQ
    topk = min(TOPK_MAX, S // 4)
    buckets = _t5_bucket_table(2 * TQ + 1)
    assert np.all(_t5_bucket_table(S + 1)[TQ + 1:] == REL_BUCKETS - 1)
    j = np.arange(2 * TQ)[:, None]
    r = np.arange(TQ)[None, :]
    dist = np.maximum(r + TQ - j, 0)
    onehot = (jnp.asarray(buckets[dist], jnp.int32)[None]
              == jnp.arange(REL_BUCKETS, dtype=jnp.int32)[:, None, None]).astype(jnp.float32)
    rel = (rel_bias.astype(jnp.float32) - rel_bias[REL_BUCKETS - 1].astype(jnp.float32)[None, :]) * LOG2_E
    enear = jnp.einsum('nh,njr->hjr', rel, onehot, precision=lax.Precision.HIGHEST)

    return pl.pallas_call(
        functools.partial(_dsa_kernel, topk=topk),
        grid=(B, nQ),
        in_specs=[
            pl.BlockSpec((1, A_WIDTH, TQ), lambda b, i: (b, 0, i)),
            pl.BlockSpec((S, A_WIDTH), lambda b, i: (b, 0)),
            pl.BlockSpec((1, A_WIDTH, S), lambda b, i: (b, 0, 0)),
            pl.BlockSpec((1, IDX_WIDTH, TQ), lambda b, i: (b, 0, i)),
            pl.BlockSpec((1, IDX_HEADS, TQ), lambda b, i: (b, 0, i)),
            pl.BlockSpec((S, IDX_DIM), lambda b, i: (b, 0)),
            pl.BlockSpec((A_HEADS, 2 * TQ, TQ), lambda b, i: (0, 0, 0)),
        ],
        out_specs=pl.BlockSpec((TQ, A_WIDTH), lambda b, i: (b * nQ + i, 0)),
        scratch_shapes=[pltpu.VMEM((S, TQ), jnp.float32),
                        pltpu.VMEM((A_HEADS, V7X_LANES, TQ), jnp.bfloat16),
                        pltpu.VMEM((A_HEADS, TQ), jnp.float32),
                        pltpu.VMEM((A_HEADS, V7X_LANES, TQ), jnp.float32)],
        out_shape=jax.ShapeDtypeStruct((T, A_WIDTH), jnp.bfloat16),
        compiler_params=_cparams(("parallel", "arbitrary")),
        name="dsa",
    )(qT, k, vT, iqT, iwT, ik, enear)


def _hgrn_kernel(bq_ref, bf_ref, bi_ref, bg_ref, lb_ref, gain_ref, o_ref,
                 st_ref, b_scr, q_scr, k_scr, v_scr, oi_scr, qd_s, kl_s, vv_s, dec_s, oi_s, upd_s, st_s,
                 kdT_s, vvT_s):
    R = bq_ref.shape[0]
    C = HGRN_CHUNK
    nC = R // C
    f32 = jnp.float32
    bf16 = jnp.bfloat16
    h = pl.program_id(1)

    @pl.when(pl.program_id(2) == 0)
    def _():
        st_ref[...] = jnp.zeros_like(st_ref)

    lb = lb_ref[pl.ds(h, 1), :]
    gain = gain_ref[pl.ds(h, 1), :]
    tril_incl = jnp.where(lax.broadcasted_iota(jnp.int32, (C, C), 1)
                          <= lax.broadcasted_iota(jnp.int32, (C, C), 0), 1.0, 0.0)
    srow = lax.broadcasted_iota(jnp.int32, (C, B_KEY_DIM), 0)

    def gates(r0):
        f = lb + (1.0 - lb) * _sigmoid(bf_ref[pl.ds(r0, C), :])
        qr = bq_ref[pl.ds(r0, C), :]
        return jnp.log(f), 1.0 - f, qr * _sigmoid(qr) * (B_KEY_DIM ** -0.5), bi_ref[pl.ds(r0, C), :]

    def cumdecay(g):
        tri = tril_incl.astype(bf16)
        g_hi = g.astype(bf16)
        rest = g - g_hi.astype(f32)
        g_mid = rest.astype(bf16)
        g_lo = (rest - g_mid.astype(f32)).astype(bf16)
        return (jnp.dot(tri, g_hi, preferred_element_type=f32) + jnp.dot(tri, g_mid, preferred_element_type=f32)
                + jnp.dot(tri, g_lo, preferred_element_type=f32))

    def advance(r0, st, qd, o_intra, upd, decay_row):
        o_inter = lax.dot_general(qd, st.astype(bf16), (((1,), (1,)), ((), ())), preferred_element_type=f32)
        og = bg_ref[pl.ds(r0, C), :]
        y = _rms(o_inter + o_intra, gain) * (og * _sigmoid(og))
        o_ref[pl.ds(r0, C), :] = y.astype(o_ref.dtype)
        return st * decay_row + upd

    f_all = lb + (1.0 - lb) * _sigmoid(bf_ref[...])
    g_all = jnp.log(f_all)
    decay = jnp.sum(g_all.reshape(nC, C, B_KEY_DIM), axis=1)
    safe = jnp.min(decay) >= -HGRN_SAFE_DECAY

    @pl.when(safe)
    def _():
        qr = bq_ref[...]
        qq = qr * _sigmoid(qr) * (B_KEY_DIM ** -0.5)
        kk = 1.0 - f_all
        GR = HGRN_GROUP * C
        r_id = lax.broadcasted_iota(jnp.int32, (GR, GR), 0)
        c_id = lax.broadcasted_iota(jnp.int32, (GR, GR), 1)
        tri_group = jnp.where((r_id // C == c_id // C) & (c_id <= r_id), 1.0, 0.0)
        tri_group_bf = tri_group.astype(bf16)
        g_hi = g_all.astype(bf16)
        rest = g_all - g_hi.astype(f32)
        g_mid = rest.astype(bf16)
        g_lo = (rest - g_mid.astype(f32)).astype(bf16)
        g_cat = jnp.concatenate([g_hi, g_mid, g_lo], axis=1)
        b_parts = []
        for gi in range(R // GR):
            bc = jnp.dot(tri_group_bf, g_cat[gi * GR:(gi + 1) * GR], preferred_element_type=f32)
            b_parts.append(bc[:, :B_KEY_DIM] + bc[:, B_KEY_DIM:2 * B_KEY_DIM] + bc[:, 2 * B_KEY_DIM:])
        b = jnp.concatenate(b_parts, axis=0)
        b_end = jnp.concatenate([jnp.broadcast_to(b[(c + 1) * C - 1:(c + 1) * C], (C, B_KEY_DIM))
                                 for c in range(nC)], axis=0)
        qd_s[...] = (qq * jnp.exp(b)).astype(bf16)
        kd = kk * jnp.exp(-b)
        kl_s[...] = (kk * jnp.exp(b_end - b)).astype(bf16)
        vv = bi_ref[...]
        vv_s[...] = vv.astype(bf16)
        dec_s[...] = jnp.exp(b_end)
        for gi in range(R // GR):
            kdT_s[gi] = kd[gi * GR:(gi + 1) * GR].T.astype(bf16)
        for c in range(nC):
            vvT_s[c] = vv[c * C:(c + 1) * C].T.astype(bf16)
        for gi in range(R // GR):
            rows = slice(gi * GR, (gi + 1) * GR)
            att = jnp.dot(qd_s[rows], kdT_s[gi], preferred_element_type=f32) * tri_group
            oi_s[rows] = jnp.dot(att.astype(bf16), vv_s[rows], preferred_element_type=f32)
        for c in range(nC):
            rows = slice(c * C, (c + 1) * C)
            upd_s[c] = jnp.dot(vvT_s[c], kl_s[rows], preferred_element_type=f32)
        st = st_ref[...]
        for c in range(nC):
            st_s[c] = st.T.astype(bf16)
            st = st * dec_s[c * C:c * C + 1] + upd_s[c]
        st_ref[...] = st
        for c in range(nC):
            rows = slice(c * C, (c + 1) * C)
            oi_s[rows] = oi_s[rows] + jnp.dot(qd_s[rows], st_s[c], preferred_element_type=f32)
        og = bg_ref[...]
        o_ref[...] = (_rms(oi_s[...], gain) * (og * _sigmoid(og))).astype(o_ref.dtype)

    @pl.when(jnp.logical_not(safe))
    def _():
        def body(c, st):
            r0 = pl.multiple_of(c * C, C)
            g, kk, qq, vv = gates(r0)
            b = cumdecay(g)
            b_last = b[C - 1:C, :]
            b_scr[...] = b
            q_scr[...] = qq
            k_scr[...] = kk
            v_scr[...] = vv

            def row(t, _):
                bt = b_scr[pl.ds(t, 1), :]
                qt = q_scr[pl.ds(t, 1), :]
                ex = jnp.where(srow <= t, bt - b_scr[...], -jnp.inf)
                a = jnp.sum(qt * k_scr[...] * jnp.exp(ex), axis=1, keepdims=True)
                oi_scr[pl.ds(t, 1), :] = jnp.sum(a * v_scr[...], axis=0, keepdims=True)
                return 0
            lax.fori_loop(0, C, row, 0)
            kd_last = (kk * jnp.exp(b_last - b)).astype(bf16)
            upd = lax.dot_general(vv.astype(bf16), kd_last, (((0,), (0,)), ((), ())),
                                  preferred_element_type=f32)
            return advance(r0, st, (qq * jnp.exp(b)).astype(bf16), oi_scr[...], upd, jnp.exp(b_last))
        st_ref[...] = lax.fori_loop(0, nC, body, st_ref[...])


def _hgrn(bq, bf, bi, bg, lb, gain, B, S):
    T = bq.shape[0]
    R = min(HGRN_ROWS, S)
    nR = S // R
    C = HGRN_CHUNK
    blk = pl.BlockSpec((R, B_KEY_DIM), lambda b, h, c: (b * nR + c, h))
    small = pl.BlockSpec((B_HEADS, B_KEY_DIM), lambda b, h, c: (0, 0))
    f32 = jnp.float32
    return pl.pallas_call(
        _hgrn_kernel,
        grid=(B, B_HEADS, nR),
        in_specs=[blk, blk, blk, blk, small, small],
        out_specs=blk,
        out_shape=jax.ShapeDtypeStruct((T, B_WIDTH), jnp.bfloat16),
        scratch_shapes=[pltpu.VMEM((B_VAL_DIM, B_KEY_DIM), f32)] +
                       [pltpu.VMEM((C, B_KEY_DIM), f32) for _ in range(5)] +
                       [pltpu.VMEM((R, B_KEY_DIM), jnp.bfloat16) for _ in range(3)] +
                       [pltpu.VMEM((R, B_KEY_DIM), f32) for _ in range(2)] +
                       [pltpu.VMEM((R // C, B_VAL_DIM, B_KEY_DIM), f32),
                        pltpu.VMEM((R // C, B_KEY_DIM, B_VAL_DIM), jnp.bfloat16),
                        pltpu.VMEM((R // (HGRN_GROUP * C), B_KEY_DIM, HGRN_GROUP * C), jnp.bfloat16),
                        pltpu.VMEM((R // C, B_VAL_DIM, C), jnp.bfloat16)],
        compiler_params=_cparams(("parallel", "parallel", "arbitrary")),
        name="hgrn",
    )(bq, bf, bi, bg, lb, gain)


def _merge_kernel(x_ref, ya_ref, yb_ref, ga_ref, gb_ref, wa_ref, wb_ref, wo_ref, g_ref, wrh_ref, wrl_ref, br_ref,
                  x1_ref, xn_ref, lg_ref):
    f32 = jnp.float32
    ma = jnp.dot(ya_ref[...], wa_ref[...], preferred_element_type=f32)
    mb = jnp.dot(yb_ref[...], wb_ref[...], preferred_element_type=f32)
    merged = ga_ref[...].astype(f32) * ma + gb_ref[...].astype(f32) * mb
    x1 = x_ref[...] + jnp.dot(merged.astype(jnp.bfloat16), wo_ref[...], preferred_element_type=f32)
    x1_ref[...] = x1
    hn = _rms(x1, g_ref[...])
    xn_ref[...] = _pack_bf16_pairs(hn)
    hn_hi = hn.astype(jnp.bfloat16)
    hn_lo = (hn - hn_hi.astype(f32)).astype(jnp.bfloat16)
    lg_ref[...] = (jnp.dot(hn_hi, wrh_ref[...], preferred_element_type=f32)
                   + jnp.dot(hn_lo, wrh_ref[...], preferred_element_type=f32)
                   + jnp.dot(hn_hi, wrl_ref[...], preferred_element_type=f32) + br_ref[...])


def _merge(x2, ya, yb, ga, gb, w_up_a, w_up_b, w_out, gain, w_router, b_router):
    T, D = x2.shape
    R = min(PROJ_ROWS, T)
    bf = jnp.bfloat16
    wr_hi = w_router.astype(bf)
    wr_lo = (w_router - wr_hi.astype(jnp.float32)).astype(bf)
    ins = [x2, ya, yb, ga, gb, w_up_a.astype(bf), w_up_b.astype(bf), w_out.astype(bf),
           gain.reshape(1, D), wr_hi, wr_lo, b_router.reshape(1, N_EXPERTS)]
    row = lambda n: pl.BlockSpec((R, n), lambda i: (i, 0))
    full = lambda a: pl.BlockSpec(a.shape, lambda i: (0,) * a.ndim)
    in_specs = [row(D), row(A_WIDTH), row(B_WIDTH), row(D), row(D)] + [full(a) for a in ins[5:]]
    return pl.pallas_call(
        _merge_kernel,
        grid=(T // R,),
        in_specs=in_specs,
        out_specs=[row(D), row(D // 2), row(N_EXPERTS)],
        out_shape=[jax.ShapeDtypeStruct((T, D), jnp.float32), jax.ShapeDtypeStruct((T, D // 2), jnp.uint32),
                   jax.ShapeDtypeStruct((T, N_EXPERTS), jnp.float32)],
        compiler_params=_cparams(("parallel",)),
        name="merge",
    )(*ins)


def _route_kernel(lg_ref, eidx_ref, gate_ref, rank_ref, cnt_ref, run_ref):
    R = lg_ref.shape[0]
    f32 = jnp.float32

    @pl.when(pl.program_id(0) == 0)
    def _():
        run_ref[...] = jnp.zeros_like(run_ref)

    lg = lg_ref[...].T
    expert = lax.broadcasted_iota(jnp.int32, (N_EXPERTS, R), 0)
    work = lg
    onehots, vals, idxs = [], [], []
    for _ in range(TOP_K):
        m = jnp.max(work, axis=0, keepdims=True)
        idx = jnp.min(jnp.where(work == m, expert, N_EXPERTS), axis=0, keepdims=True)
        oh = expert == idx
        onehots.append(oh)
        vals.append(m)
        idxs.append(idx)
        work = jnp.where(oh, -jnp.inf, work)
    ex = [jnp.exp(v - vals[0]) for v in vals]
    den = ex[0] + ex[1] + ex[2] + ex[3]
    chosen = jnp.where(onehots[0] | onehots[1] | onehots[2] | onehots[3], 1.0, 0.0)
    earlier = jnp.where(lax.broadcasted_iota(jnp.int32, (R, R), 0)
                        < lax.broadcasted_iota(jnp.int32, (R, R), 1), 1.0, 0.0).astype(jnp.bfloat16)
    before = jnp.dot(chosen.astype(jnp.bfloat16), earlier, preferred_element_type=f32) + run_ref[...]
    slot = lax.broadcasted_iota(jnp.int32, (TOP_K, R), 0)
    eidx = jnp.zeros((TOP_K, R), jnp.int32)
    gate = jnp.zeros((TOP_K, R), f32)
    rank = jnp.zeros((TOP_K, R), f32)
    for k in range(TOP_K):
        eidx = jnp.where(slot == k, idxs[k], eidx)
        gate = jnp.where(slot == k, ex[k] / den, gate)
        rk = jnp.sum(jnp.where(onehots[k], before, 0.0), axis=0, keepdims=True)
        rank = jnp.where(slot == k, rk, rank)
    eidx_ref[...] = eidx
    gate_ref[...] = gate
    rank_ref[...] = rank.astype(jnp.int32)
    run_ref[...] = run_ref[...] + jnp.sum(chosen, axis=1, keepdims=True)
    cnt_ref[...] = run_ref[...].astype(jnp.int32)


def _route(logits):
    T = logits.shape[0]
    R = min(ROUTE_ROWS, T)
    col = pl.BlockSpec((TOP_K, R), lambda i: (0, i))
    return pl.pallas_call(
        _route_kernel,
        grid=(T // R,),
        in_specs=[pl.BlockSpec((R, N_EXPERTS), lambda i: (i, 0))],
        out_specs=[col, col, col, pl.BlockSpec((N_EXPERTS, 1), lambda i: (0, 0))],
        out_shape=[jax.ShapeDtypeStruct((TOP_K, T), jnp.int32), jax.ShapeDtypeStruct((TOP_K, T), jnp.float32),
                   jax.ShapeDtypeStruct((TOP_K, T), jnp.int32), jax.ShapeDtypeStruct((N_EXPERTS, 1), jnp.int32)],
        scratch_shapes=[pltpu.VMEM((N_EXPERTS, 1), jnp.float32)],
        compiler_params=_cparams(("arbitrary",)),
        name="route",
    )(logits)


def _sc_gather_rows(table, idx):
    N, W = table.shape
    M = idx.shape[0]
    workers = V7X_SC_CORES * V7X_SC_SUBCORES
    per_worker = M // workers
    pieces = per_worker // SC_GATHER_ROWS
    assert per_worker * workers == M and pieces * SC_GATHER_ROWS == per_worker and pieces % 2 == 0
    mesh = plsc.VectorSubcoreMesh(core_axis_name="core", subcore_axis_name="subcore",
                                  num_cores=V7X_SC_CORES, num_subcores=V7X_SC_SUBCORES)

    @functools.partial(
        pl.kernel, mesh=mesh,
        out_type=jax.ShapeDtypeStruct((M, W), table.dtype),
        scratch_types=[pltpu.VMEM((per_worker,), jnp.int32),
                       pltpu.VMEM((SC_GATHER_ROWS, W), table.dtype),
                       pltpu.VMEM((SC_GATHER_ROWS, W), table.dtype),
                       pltpu.SemaphoreType.DMA, pltpu.SemaphoreType.DMA],
    )
    def gather(table_hbm, idx_hbm, out_hbm, idx_v, rows_a, rows_b, sem_a, sem_b):
        worker = lax.axis_index("subcore") * V7X_SC_CORES + lax.axis_index("core")
        base = pl.multiple_of(worker * per_worker, SC_GATHER_ROWS)
        pltpu.sync_copy(idx_hbm.at[pl.ds(base, per_worker)], idx_v)
        bufs = ((rows_a, sem_a), (rows_b, sem_b))

        def fetch(g, buf, sem):
            off = pl.multiple_of(g * SC_GATHER_ROWS, SC_GATHER_ROWS)
            return pltpu.make_async_copy(table_hbm.at[idx_v.at[pl.ds(off, SC_GATHER_ROWS)]], buf, sem)

        fetch(0, *bufs[0]).start()

        @pl.loop(0, pieces, step=2)
        def _(g0):
            for half in range(2):
                g = g0 + half
                buf, sem = bufs[half]
                fetch(g, buf, sem).wait()

                @pl.when(g + 1 < pieces)
                def _():
                    fetch(g + 1, *bufs[1 - half]).start()

                off = pl.multiple_of(g * SC_GATHER_ROWS, SC_GATHER_ROWS)
                pltpu.sync_copy(buf, out_hbm.at[pl.ds(base + off, SC_GATHER_ROWS)])

    return gather(table, idx)


def _sc_scatter_rows(rows, dest, n_out):
    T, W = rows.shape
    slots = dest.shape[0]
    workers = V7X_SC_CORES * V7X_SC_SUBCORES
    per_worker = T // workers
    pieces = per_worker // SC_GATHER_ROWS
    assert per_worker * workers == T and pieces * SC_GATHER_ROWS == per_worker and pieces % 2 == 0
    idx = dest.reshape(slots, workers, pieces, SC_GATHER_ROWS).transpose(1, 2, 0, 3)
    mesh = plsc.VectorSubcoreMesh(core_axis_name="core", subcore_axis_name="subcore",
                                  num_cores=V7X_SC_CORES, num_subcores=V7X_SC_SUBCORES)

    @functools.partial(
        pl.kernel, mesh=mesh,
        out_type=jax.ShapeDtypeStruct((n_out, W), rows.dtype),
        scratch_types=[pltpu.VMEM((pieces, slots, SC_GATHER_ROWS), jnp.int32),
                       pltpu.VMEM((SC_GATHER_ROWS, W), rows.dtype),
                       pltpu.VMEM((SC_GATHER_ROWS, W), rows.dtype),
                       pltpu.SemaphoreType.DMA, pltpu.SemaphoreType.DMA, pltpu.SemaphoreType.DMA],
    )
    def scatter(rows_hbm, idx_hbm, out_hbm, idx_v, rows_a, rows_b, sem_a, sem_b, sem_out):
        worker = lax.axis_index("subcore") * V7X_SC_CORES + lax.axis_index("core")
        base = pl.multiple_of(worker * per_worker, SC_GATHER_ROWS)
        pltpu.sync_copy(idx_hbm.at[worker], idx_v)
        bufs = ((rows_a, sem_a), (rows_b, sem_b))

        def fetch(g, buf, sem):
            off = pl.multiple_of(g * SC_GATHER_ROWS, SC_GATHER_ROWS)
            return pltpu.make_async_copy(rows_hbm.at[pl.ds(base + off, SC_GATHER_ROWS)], buf, sem)

        fetch(0, *bufs[0]).start()

        @pl.loop(0, pieces, step=2)
        def _(g0):
            for half in range(2):
                g = g0 + half
                buf, sem = bufs[half]
                fetch(g, buf, sem).wait()

                @pl.when(g + 1 < pieces)
                def _():
                    fetch(g + 1, *bufs[1 - half]).start()

                puts = [pltpu.make_async_copy(buf, out_hbm.at[idx_v.at[g, k]], sem_out) for k in range(slots)]
                for put in puts:
                    put.start()
                for put in puts:
                    put.wait()

    return scatter(rows, idx)


def _experts_kernel(be_ref, nb_ref, first_ref, slot_ref, next_ref,
                    x_ref, wgu_hbm, bgu_ref, wd_hbm, bd_ref, o_ref,
                    wgu_f32, wd_f32, wgu_bf, wd_bf, sems):
    f32 = jnp.float32
    d_ff = wd_bf.shape[0]
    i = pl.program_id(0)
    live = i < nb_ref[0]

    def weight_copies(e, s):
        return (pltpu.make_async_copy(wgu_hbm.at[e], wgu_f32.at[s], sems.at[s, 0]),
                pltpu.make_async_copy(wd_hbm.at[e], wd_f32.at[s], sems.at[s, 1]))

    @pl.when(live & (first_ref[i] == 1))
    def _():
        e = be_ref[i]
        s = slot_ref[i]

        @pl.when(i == 0)
        def _():
            for cp in weight_copies(e, s):
                cp.start()

        for cp in weight_copies(e, s):
            cp.wait()

        @pl.when(next_ref[i] >= 0)
        def _():
            for cp in weight_copies(next_ref[i], 1 - s):
                cp.start()

        wgu_bf[...] = wgu_f32[s].astype(jnp.bfloat16)
        wd_bf[...] = wd_f32[s].astype(jnp.bfloat16)

    @pl.when(live)
    def _():
        x_hi, x_lo = _unpack_bf16_pairs(x_ref[...])
        half = x_hi.shape[1]
        gu = (jnp.dot(x_hi, wgu_bf[:half, :], preferred_element_type=f32)
              + jnp.dot(x_lo, wgu_bf[half:, :], preferred_element_type=f32) + bgu_ref[0])
        gate = jnp.minimum(gu[:, :d_ff], SWIGLU_LIMIT)
        lin = jnp.clip(gu[:, d_ff:], -SWIGLU_LIMIT, SWIGLU_LIMIT)
        act = (lin + 1.0) * gate * _sigmoid(SWIGLU_ALPHA * gate)
        y = jnp.dot(act.astype(jnp.bfloat16), wd_bf[...], preferred_element_type=f32) + bd_ref[0]
        o_ref[...] = _pack_bf16_pairs(y)

    @pl.when(pl.program_id(0) >= nb_ref[0])
    def _():
        o_ref[...] = jnp.zeros_like(o_ref)


def _experts(xs, plan, w_gu, b_gu, w_down, b_down):
    P, W = xs.shape
    E, D, F2 = w_gu.shape
    nb = P // EXPERT_ROWS
    by_expert = lambda i, be, *_: (be[i], 0, 0)
    grid_spec = pltpu.PrefetchScalarGridSpec(
        num_scalar_prefetch=5,
        grid=(nb,),
        in_specs=[
            pl.BlockSpec((EXPERT_ROWS, W), lambda i, *_: (i, 0)),
            pl.BlockSpec(memory_space=pl.ANY),
            pl.BlockSpec((1, 1, F2), by_expert),
            pl.BlockSpec(memory_space=pl.ANY),
            pl.BlockSpec((1, 1, D), by_expert),
        ],
        out_specs=pl.BlockSpec((EXPERT_ROWS, W), lambda i, *_: (i, 0)),
        scratch_shapes=[pltpu.VMEM((2, D, F2), jnp.float32), pltpu.VMEM((2, F2 // 2, D), jnp.float32),
                        pltpu.VMEM((D, F2), jnp.bfloat16), pltpu.VMEM((F2 // 2, D), jnp.bfloat16),
                        pltpu.SemaphoreType.DMA((2, 2))],
    )
    return pl.pallas_call(
        _experts_kernel,
        grid_spec=grid_spec,
        out_shape=jax.ShapeDtypeStruct((P, W), jnp.uint32),
        compiler_params=_cparams(("arbitrary",)),
        name="experts",
    )(*plan, xs, w_gu, b_gu.reshape(E, 1, F2), w_down, b_down.reshape(E, 1, D))


def _combine_kernel(ya_ref, x1_ref, gate_ref, g_ref, o_ref):
    half = x1_ref.shape[1] // 2
    f32 = jnp.float32
    gate = gate_ref[...].T
    x1 = x1_ref[...]
    y_hi = x1[:, :half]
    y_lo = x1[:, half:]
    for k in range(TOP_K):
        hi, lo = _unpack_bf16_pairs(ya_ref[k])
        y_hi = y_hi + gate[:, k:k + 1] * hi.astype(f32)
        y_lo = y_lo + gate[:, k:k + 1] * lo.astype(f32)
    o_ref[...] = _rms(jnp.concatenate([y_hi, y_lo], axis=1), g_ref[...])


def _combine(ya, x1, gates, gain):
    T, D = x1.shape
    R = min(COMBINE_ROWS, T)
    row = lambda w: pl.BlockSpec((R, w), lambda i: (i, 0))
    return pl.pallas_call(
        _combine_kernel,
        grid=(T // R,),
        in_specs=[pl.BlockSpec((TOP_K, R, D // 2), lambda i: (0, i, 0)), row(D),
                  pl.BlockSpec((TOP_K, R), lambda i: (0, i)), pl.BlockSpec((1, D), lambda i: (0, 0))],
        out_specs=row(D),
        out_shape=jax.ShapeDtypeStruct((T, D), jnp.float32),
        compiler_params=_cparams(("parallel",)),
        name="combine",
    )(ya, x1, gates, gain.reshape(1, D))


def _moe_plan(eidx, rank, counts, A):
    counts = counts.reshape(N_EXPERTS)
    padded = (counts + EXPERT_ROWS - 1) // EXPERT_ROWS * EXPERT_ROWS
    pad_ends = jnp.cumsum(padded)
    pad_starts = pad_ends - padded
    n_blocks = -(-A // EXPERT_ROWS) + N_EXPERTS
    ids = jnp.arange(N_EXPERTS, dtype=jnp.int32)
    dest = rank + jnp.sum(jnp.where(eidx[None] == ids[:, None, None], pad_starts[:, None, None], 0), axis=0)
    block_start = jnp.arange(n_blocks, dtype=pad_ends.dtype) * EXPERT_ROWS
    block_expert = jnp.minimum(jnp.sum(pad_ends[None, :] <= block_start[:, None], axis=1),
                               N_EXPERTS - 1).astype(jnp.int32)
    n_used = (pad_ends[-1] // EXPERT_ROWS).astype(jnp.int32).reshape(1)
    has_rows = counts > 0
    ordinal = jnp.cumsum(has_rows.astype(jnp.int32)) - 1
    later = has_rows[None, :] & (ids[None, :] > ids[:, None])
    next_expert = jnp.where(jnp.any(later, axis=1), jnp.argmax(later, axis=1), -1).astype(jnp.int32)
    tables = jnp.stack([pad_starts, ordinal % 2, next_expert], axis=1).astype(jnp.float32)
    onehot = (block_expert[:, None] == ids[None, :]).astype(jnp.float32)
    looked = jnp.dot(onehot, tables, precision=lax.Precision.HIGHEST).astype(jnp.int32)
    block_first = ((block_start == looked[:, 0]) & (block_start < pad_ends[-1])).astype(jnp.int32)
    plan = (block_expert, n_used, block_first, looked[:, 1], looked[:, 2])
    return dest.astype(jnp.int32), plan, n_blocks


def kernel(x, w_in, w_up_a, w_up_b, w_out, norm_mix, norm_ffn, norm_final, hgrn_norm,
           lb_logits, rel_bias, w_router, b_router, w_gu, b_gu, w_down, b_down):
    B, S, D = x.shape
    T = B * S
    assert w_in.shape[0] == 1, "the final rmsnorm is fused into the single layer's combine stage"
    lb_all = jnp.cumsum(jax.nn.softmax(lb_logits.astype(jnp.float32), axis=0), axis=0)
    x2 = x.reshape(T, D)
    (k, ik, qT, vT, iqT, iwT, bq, bf, bi, bg, ga, gb) = _inproj(x2, norm_mix[0], w_in[0], B, S)
    ya = _dsa(k, ik, qT, vT, iqT, iwT, rel_bias, B, S)
    yb = _hgrn(bq, bf, bi, bg, lb_all[0].reshape(B_HEADS, B_KEY_DIM), hgrn_norm[0], B, S)
    x1, xn, logits = _merge(x2, ya, yb, ga, gb, w_up_a[0], w_up_b[0], w_out[0], norm_ffn[0],
                            w_router[0], b_router[0])
    eidx, gates, rank, counts = _route(logits)
    dest, plan, n_blocks = _moe_plan(eidx, rank, counts, T * TOP_K)
    P = n_blocks * EXPERT_ROWS
    A = T * TOP_K
    xs = _sc_scatter_rows(xn, dest, P)
    y_buf = _experts(xs, plan, w_gu[0], b_gu[0], w_down[0], b_down[0])
    ya = _sc_gather_rows(y_buf, dest.reshape(A)).reshape(TOP_K, T, D // 2)
    out = _combine(ya, x1, gates, norm_final)
    return out.reshape(B, S, D)
```

```python
import functools
import math

import numpy as np
import jax
import jax.numpy as jnp
from jax import lax
from jax.experimental import pallas as pl
from jax.experimental.pallas import tpu as pltpu
from jax.experimental.pallas import tpu_sc as plsc

A_HEADS = 8
A_HEAD_DIM = 64
IDX_HEADS = 8
IDX_DIM = 32
TOPK_MAX = 256
REL_BUCKETS = 32
REL_MAX_DIST = 128
B_HEADS = 4
B_KEY_DIM = 128
B_VAL_DIM = 128
N_EXPERTS = 32
TOP_K = 4
SWIGLU_LIMIT = 7.0
SWIGLU_ALPHA = 1.702
EPS = 1e-6
LOG2_E = math.log2(math.e)

A_WIDTH = A_HEADS * A_HEAD_DIM
B_WIDTH = B_HEADS * B_VAL_DIM
IDX_WIDTH = IDX_HEADS * IDX_DIM

V7X_LANES = 128
V7X_SUBLANES = 8
V7X_VMEM_LIMIT_BYTES = 56 * 1024 * 1024
V7X_SC_CORES = 2
V7X_SC_SUBCORES = 16

PROJ_ROWS = 512
ATT_Q = 256
ATT_KC = 128
SCORE_UNROLL = 4
COUNT_UNROLL = 2
FAR_UNROLL = 4
HGRN_ROWS = 1024
HGRN_CHUNK = 64
HGRN_GROUP = 4
HGRN_SAFE_DECAY = 70.0
ROUTE_ROWS = 512
EXPERT_ROWS = 512
COMBINE_ROWS = 512
SC_GATHER_ROWS = 64
MASK_NEG = -1e30
BISECT_FAST_ITERS = 26
BISECT_BLIND_ITERS = 12


def _cparams(dims):
    return pltpu.CompilerParams(dimension_semantics=dims, vmem_limit_bytes=V7X_VMEM_LIMIT_BYTES)


def _rms(x, gain):
    return x * lax.rsqrt(jnp.mean(x * x, axis=-1, keepdims=True) + EPS) * gain


def _sigmoid(x):
    return 1.0 / (1.0 + jnp.exp(-x))


def _pack_bf16_pairs(x):
    n = x.shape[1] // 2
    as_bits = lambda v: lax.bitcast_convert_type(v.astype(jnp.bfloat16).astype(jnp.float32), jnp.uint32)
    return (as_bits(x[:, :n]) & jnp.uint32(0xFFFF0000)) | (as_bits(x[:, n:]) >> 16)


def _unpack_bf16_pairs(w):
    hi = lax.bitcast_convert_type(w & jnp.uint32(0xFFFF0000), jnp.float32).astype(jnp.bfloat16)
    lo = lax.bitcast_convert_type(w << 16, jnp.float32).astype(jnp.bfloat16)
    return hi, lo


def _blocked_loop(n, step, carry, unroll):
    def run(first, count, cr):
        for t in range(count):
            cr = step(first + t, cr)
        return cr

    carry = lax.fori_loop(0, n // unroll, lambda j, cr: run(j * unroll, unroll, cr), carry)
    base = (n // unroll) * unroll
    piece = unroll // 2
    while piece >= 1:
        take = (n & piece) != 0
        carry = lax.cond(take, functools.partial(run, base, piece), lambda cr: cr, carry)
        base = base + jnp.where(take, piece, 0)
        piece //= 2
    return carry


def _fold_rows(x, op):
    return op(x.reshape(x.shape[0] // V7X_SUBLANES, V7X_SUBLANES, x.shape[1]), axis=0)


def _inproj_kernel(x_ref, g_ref, wk_ref, wik_ref, wqT_ref, wvT_ref, wiqT_ref, wiwT_ref, wb_ref, wg_ref,
                   k_ref, ik_ref, qT_ref, vT_ref, iqT_ref, iwT_ref, bq_ref, bf_ref, bi_ref, bg_ref,
                   ga_ref, gb_ref):
    x = x_ref[...]
    hn = _rms(x, g_ref[...]).astype(jnp.bfloat16)

    def mm(w_ref):
        return jnp.dot(hn, w_ref[...], preferred_element_type=jnp.float32)

    def mm_t(w_ref):
        return lax.dot_general(w_ref[...], hn, (((1,), (1,)), ((), ())),
                               preferred_element_type=jnp.float32)

    k_ref[...] = mm(wk_ref).astype(jnp.bfloat16)
    ik_ref[...] = mm(wik_ref).astype(jnp.bfloat16)
    qT_ref[0] = (mm_t(wqT_ref) * (A_HEAD_DIM ** -0.5 * LOG2_E)).astype(jnp.bfloat16)
    vT_ref[0] = mm_t(wvT_ref).astype(jnp.bfloat16)
    iqT_ref[0] = mm_t(wiqT_ref).astype(jnp.bfloat16)
    iwT_ref[0] = mm_t(wiwT_ref) * ((IDX_HEADS * IDX_DIM) ** -0.5)
    hb = mm(wb_ref)
    bq_ref[...] = hb[:, 0 * B_WIDTH:1 * B_WIDTH]
    bf_ref[...] = hb[:, 1 * B_WIDTH:2 * B_WIDTH]
    bi_ref[...] = hb[:, 2 * B_WIDTH:3 * B_WIDTH]
    bg_ref[...] = hb[:, 3 * B_WIDTH:4 * B_WIDTH]
    d = ga_ref.shape[-1]
    hg = mm(wg_ref)
    ga_ref[...] = _sigmoid(hg[:, :d]).astype(jnp.bfloat16)
    gb_ref[...] = _sigmoid(hg[:, d:]).astype(jnp.bfloat16)


def _inproj(x2, gain, w_in, B, S):
    T, D = x2.shape
    R = min(PROJ_ROWS, S)
    nS = S // R
    o = np.cumsum((0, A_WIDTH, A_WIDTH, A_WIDTH, IDX_WIDTH, IDX_HEADS, IDX_DIM,
                   B_WIDTH, B_WIDTH, B_WIDTH, B_WIDTH, D, D))
    bf = jnp.bfloat16
    wqT = w_in[:, o[0]:o[1]].T.astype(bf)
    wk = w_in[:, o[1]:o[2]].astype(bf)
    wvT = w_in[:, o[2]:o[3]].T.astype(bf)
    wiqT = w_in[:, o[3]:o[4]].T.astype(bf)
    wiwT = w_in[:, o[4]:o[5]].T.astype(bf)
    wik = w_in[:, o[5]:o[6]].astype(bf)
    wb = w_in[:, o[6]:o[10]].astype(bf)
    wg = w_in[:, o[10]:o[12]].astype(bf)

    def full(a):
        return pl.BlockSpec(a.shape, lambda b, i: (0,) * a.ndim)

    row = lambda n: pl.BlockSpec((R, n), lambda b, i: (b * nS + i, 0))
    colT = lambda n: pl.BlockSpec((1, n, R), lambda b, i: (b, 0, i))
    f32 = jnp.float32
    outs = [
        (jax.ShapeDtypeStruct((T, A_WIDTH), bf), row(A_WIDTH)),
        (jax.ShapeDtypeStruct((T, IDX_DIM), bf), row(IDX_DIM)),
        (jax.ShapeDtypeStruct((B, A_WIDTH, S), bf), colT(A_WIDTH)),
        (jax.ShapeDtypeStruct((B, A_WIDTH, S), bf), colT(A_WIDTH)),
        (jax.ShapeDtypeStruct((B, IDX_WIDTH, S), bf), colT(IDX_WIDTH)),
        (jax.ShapeDtypeStruct((B, IDX_HEADS, S), f32), colT(IDX_HEADS)),
        (jax.ShapeDtypeStruct((T, B_WIDTH), f32), row(B_WIDTH)),
        (jax.ShapeDtypeStruct((T, B_WIDTH), f32), row(B_WIDTH)),
        (jax.ShapeDtypeStruct((T, B_WIDTH), f32), row(B_WIDTH)),
        (jax.ShapeDtypeStruct((T, B_WIDTH), f32), row(B_WIDTH)),
        (jax.ShapeDtypeStruct((T, D), bf), row(D)),
        (jax.ShapeDtypeStruct((T, D), bf), row(D)),
    ]
    ins = [x2, gain.reshape(1, D), wk, wik, wqT, wvT, wiqT, wiwT, wb, wg]
    in_specs = [row(D)] + [full(a) for a in ins[1:]]
    return pl.pallas_call(
        _inproj_kernel,
        grid=(B, nS),
        in_specs=in_specs,
        out_specs=[s for _, s in outs],
        out_shape=[s for s, _ in outs],
        compiler_params=_cparams(("parallel", "parallel")),
        name="inproj",
    )(*ins)


def _t5_bucket_table(n):
    d = np.arange(n)
    max_exact = REL_BUCKETS // 2
    nf = np.maximum(d, 1).astype(np.float64)
    large = max_exact + (np.log(nf / max_exact) / math.log(REL_MAX_DIST / max_exact)
                         * (REL_BUCKETS - max_exact)).astype(np.int32)
    large = np.minimum(large, REL_BUCKETS - 1)
    return np.where(d < max_exact, d, large)


def _dsa_kernel(qT_ref, k_ref, vT_ref, iqT_ref, iwT_ref, ik_ref, enear_ref,
                o_ref, sc_ref, qh_scr, m_scr, acc_scr, *, topk):
    TQ = qT_ref.shape[2]
    KC = TQ
    i = pl.program_id(1)
    nch = i + 1
    q0 = i * TQ
    f32 = jnp.float32
    bf16 = jnp.bfloat16
    key_id = lax.broadcasted_iota(jnp.int32, (KC, TQ), 0)
    qry_id = lax.broadcasted_iota(jnp.int32, (KC, TQ), 1)

    def col_reduce(x, op):
        return op(_fold_rows(x, op), axis=0, keepdims=True)

    iw = iwT_ref[0]

    def score_chunk(c, carry):
        rmin, rmax = carry
        k0 = pl.multiple_of(c * KC, KC)
        ik = ik_ref[pl.ds(k0, KC), :]
        acc = jnp.zeros((KC, TQ), f32)
        for h in range(IDX_HEADS):
            sh = jnp.dot(ik, iqT_ref[0, h * IDX_DIM:(h + 1) * IDX_DIM, :], preferred_element_type=f32)
            acc = acc + jnp.maximum(sh, 0.0) * iw[h:h + 1, :]
        valid = (k0 + key_id) <= (q0 + qry_id)
        sc_ref[pl.ds(k0, KC), :] = jnp.where(valid, acc, MASK_NEG)
        rmin = jnp.minimum(rmin, _fold_rows(jnp.where(valid, acc, -MASK_NEG), jnp.min))
        rmax = jnp.maximum(rmax, _fold_rows(jnp.where(valid, acc, MASK_NEG), jnp.max))
        return rmin, rmax

    rmin8, rmax8 = _blocked_loop(
        nch, score_chunk,
        (jnp.full((V7X_SUBLANES, TQ), -MASK_NEG, f32), jnp.full((V7X_SUBLANES, TQ), MASK_NEG, f32)),
        SCORE_UNROLL)
    rmin = jnp.min(rmin8, axis=0, keepdims=True)
    rmax = jnp.max(rmax8, axis=0, keepdims=True)

    def count_where(pred_fn):
        def one(c, acc):
            k0 = pl.multiple_of(c * KC, KC)
            blk = sc_ref[pl.ds(k0, KC), :]
            return acc + _fold_rows(jnp.where(pred_fn(blk), 1.0, 0.0), jnp.sum)
        acc = _blocked_loop(nch, one, jnp.zeros((V7X_SUBLANES, TQ), f32), COUNT_UNROLL)
        return jnp.sum(acc, axis=0, keepdims=True)

    def band_min_max(lo, hi):
        def body(c, carry):
            bmin, bmax = carry
            k0 = pl.multiple_of(c * KC, KC)
            blk = sc_ref[pl.ds(k0, KC), :]
            bmin = jnp.minimum(bmin, _fold_rows(jnp.where(blk >= lo, blk, -MASK_NEG), jnp.min))
            bmax = jnp.maximum(bmax, _fold_rows(jnp.where(blk < hi, blk, MASK_NEG), jnp.max))
            return bmin, bmax
        bmin8, bmax8 = lax.fori_loop(
            0, nch, body,
            (jnp.full((V7X_SUBLANES, TQ), -MASK_NEG, f32), jnp.full((V7X_SUBLANES, TQ), MASK_NEG, f32)))
        return jnp.min(bmin8, axis=0, keepdims=True), jnp.max(bmax8, axis=0, keepdims=True)

    kf = float(topk)
    n_valid = (q0 + 1 + lax.broadcasted_iota(jnp.int32, (1, TQ), 1)).astype(f32)
    lo0 = rmin
    cnt0 = n_valid
    hi0 = rmax + jnp.maximum(jnp.abs(rmax) * 2.0 ** -20, 1e-30)
    done0 = jnp.where(cnt0 <= kf, 1.0, 0.0)

    def probe(st, lo_s, mid, tie):
        it, lo, hi, cnt, done = st
        active = done < 0.5
        lo_s = jnp.where(active, lo_s, lo)
        c = count_where(lambda blk: blk >= mid)
        feas = c >= kf
        move = active & jnp.logical_not(tie)
        lo_n = jnp.where(move & feas, mid, lo_s)
        cnt_n = jnp.where(move & feas, c, cnt)
        hi_n = jnp.where(move & jnp.logical_not(feas), mid, hi)
        done_n = jnp.where((active & tie) | (cnt_n <= kf), 1.0, done)
        return it + 1, lo_n, hi_n, cnt_n, done_n

    def halve(st):
        _, lo, hi, _, _ = st
        half = lo + 0.5 * (hi - lo)
        stuck = (half <= lo) | (half >= hi)
        return probe(st, lo, half, stuck)

    def snap(st):
        _, lo, hi, _, _ = st
        bmin, bmax = band_min_max(lo, hi)
        mid = bmin + 0.5 * (bmax - bmin)
        return probe(st, bmin, jnp.where(mid <= bmin, bmax, mid), bmax <= bmin)

    st = lax.fori_loop(0, BISECT_BLIND_ITERS // 2, lambda _, st: halve(halve(st)),
                       (jnp.int32(0), lo0, hi0, cnt0, done0))
    st = lax.while_loop(lambda st: (jnp.min(st[-1]) < 0.5) & (st[0] < BISECT_FAST_ITERS),
                        lambda st: halve(halve(st)), st)
    _, thr, _, cnt_thr, _ = lax.while_loop(lambda st: jnp.min(st[-1]) < 0.5, snap, st)

    tie_overflow = jnp.max(cnt_thr) > kf

    @pl.when(jnp.logical_not(tie_overflow))
    def _():
        def mask_chunk(c, _):
            k0 = pl.multiple_of(c * KC, KC)
            sc_ref[pl.ds(k0, KC), :] = jnp.where(sc_ref[pl.ds(k0, KC), :] >= thr, 0.0, MASK_NEG)
            return 0
        lax.fori_loop(0, nch, mask_chunk, 0)

    @pl.when(tie_overflow)
    def _():
        need = kf - count_where(lambda blk: blk > thr)
        tril = jnp.where(lax.broadcasted_iota(jnp.int32, (KC, KC), 1)
                         <= lax.broadcasted_iota(jnp.int32, (KC, KC), 0), 1.0, 0.0).astype(bf16)

        def mask_chunk(c, run):
            k0 = pl.multiple_of(c * KC, KC)
            blk = sc_ref[pl.ds(k0, KC), :]
            eq = jnp.where(blk == thr, 1.0, 0.0)
            pref = jnp.dot(tril, eq.astype(bf16), preferred_element_type=f32)
            sel = (blk > thr) | ((eq > 0.5) & (run + pref <= need))
            sc_ref[pl.ds(k0, KC), :] = jnp.where(sel, 0.0, MASK_NEG)
            return run + pref[KC - 1:KC, :]

        lax.fori_loop(0, nch, mask_chunk, jnp.zeros((1, TQ), f32))

    AK = min(ATT_KC, TQ)
    per = TQ // AK
    head0_q = (lax.broadcasted_iota(jnp.int32, (V7X_LANES, TQ), 0) // A_HEAD_DIM) == 0
    n_pairs = A_HEADS // 2

    m_scr[...] = jnp.full(m_scr.shape, MASK_NEG, f32)
    acc_scr[...] = jnp.zeros(acc_scr.shape, f32)
    v_row = lax.broadcasted_iota(jnp.int32, (V7X_LANES, AK), 0)
    denom_row = [A_HEAD_DIM * (1 - sub) for sub in range(2)]
    for p in range(n_pairs):
        q_pair = qT_ref[0, p * V7X_LANES:(p + 1) * V7X_LANES, :]
        zq = jnp.zeros_like(q_pair)
        qh_scr[2 * p] = jnp.where(head0_q, q_pair, zq)
        qh_scr[2 * p + 1] = jnp.where(head0_q, zq, q_pair)

    def step(c, bias_rows):
        k0 = pl.multiple_of(c * AK, AK)
        msk = sc_ref[pl.ds(k0, AK), :]
        for p in range(n_pairs):
            kp = k_ref[pl.ds(k0, AK), p * V7X_LANES:(p + 1) * V7X_LANES]
            vp = vT_ref[0, p * V7X_LANES:(p + 1) * V7X_LANES, pl.ds(k0, AK)]
            for sub in range(2):
                h = 2 * p + sub
                s = jnp.dot(kp, qh_scr[h], preferred_element_type=f32) + msk
                if bias_rows is not None:
                    s = s + enear_ref[h, bias_rows, :]
                m = m_scr[h:h + 1, :]
                m_new = jnp.maximum(m, col_reduce(s, jnp.max))
                alpha = jnp.exp2(m - m_new)
                pr = jnp.exp2(s - m_new)
                m_scr[h:h + 1, :] = m_new
                v_aug = jnp.where(v_row == denom_row[sub], jnp.ones_like(vp), vp)
                acc_scr[h] = alpha * acc_scr[h] + jnp.dot(v_aug, pr.astype(bf16), preferred_element_type=f32)

    def far(blk, _):
        for jj in range(per):
            step(blk * per + jj, None)
        return 0

    def near(block, first_chunk):
        for jj in range(per):
            step(first_chunk + jj, slice(block * TQ + jj * AK, block * TQ + (jj + 1) * AK))

    _blocked_loop(jnp.maximum(i - 1, 0), far, 0, FAR_UNROLL)

    @pl.when(i >= 1)
    def _():
        near(0, (i - 1) * per)
        near(1, i * per)

    @pl.when(i == 0)
    def _():
        near(1, i * per)
    for p in range(n_pairs):
        outs = [acc_scr[2 * p + sub] / acc_scr[2 * p + sub, denom_row[sub]:denom_row[sub] + 1, :]
                for sub in range(2)]
        o_pair = jnp.where(head0_q, outs[0], outs[1])
        o_ref[:, p * V7X_LANES:(p + 1) * V7X_LANES] = o_pair.T.astype(o_ref.dtype)


def _dsa(k, ik, qT, vT, iqT, iwT, rel_bias, B, S):
    T = k.shape[0]
    TQ = min(ATT_Q, S)
    nQ = S // TQ
    topk = min(TOPK_MAX, S // 4)
    buckets = _t5_bucket_table(2 * TQ + 1)
    assert np.all(_t5_bucket_table(S + 1)[TQ + 1:] == REL_BUCKETS - 1)
    j = np.arange(2 * TQ)[:, None]
    r = np.arange(TQ)[None, :]
    dist = np.maximum(r + TQ - j, 0)
    onehot = (jnp.asarray(buckets[dist], jnp.int32)[None]
              == jnp.arange(REL_BUCKETS, dtype=jnp.int32)[:, None, None]).astype(jnp.float32)
    rel = (rel_bias.astype(jnp.float32) - rel_bias[REL_BUCKETS - 1].astype(jnp.float32)[None, :]) * LOG2_E
    enear = jnp.einsum('nh,njr->hjr', rel, onehot, precision=lax.Precision.HIGHEST)

    return pl.pallas_call(
        functools.partial(_dsa_kernel, topk=topk),
        grid=(B, nQ),
        in_specs=[
            pl.BlockSpec((1, A_WIDTH, TQ), lambda b, i: (b, 0, i)),
            pl.BlockSpec((S, A_WIDTH), lambda b, i: (b, 0)),
            pl.BlockSpec((1, A_WIDTH, S), lambda b, i: (b, 0, 0)),
            pl.BlockSpec((1, IDX_WIDTH, TQ), lambda b, i: (b, 0, i)),
            pl.BlockSpec((1, IDX_HEADS, TQ), lambda b, i: (b, 0, i)),
            pl.BlockSpec((S, IDX_DIM), lambda b, i: (b, 0)),
            pl.BlockSpec((A_HEADS, 2 * TQ, TQ), lambda b, i: (0, 0, 0)),
        ],
        out_specs=pl.BlockSpec((TQ, A_WIDTH), lambda b, i: (b * nQ + i, 0)),
        scratch_shapes=[pltpu.VMEM((S, TQ), jnp.float32),
                        pltpu.VMEM((A_HEADS, V7X_LANES, TQ), jnp.bfloat16),
                        pltpu.VMEM((A_HEADS, TQ), jnp.float32),
                        pltpu.VMEM((A_HEADS, V7X_LANES, TQ), jnp.float32)],
        out_shape=jax.ShapeDtypeStruct((T, A_WIDTH), jnp.bfloat16),
        compiler_params=_cparams(("parallel", "arbitrary")),
        name="dsa",
    )(qT, k, vT, iqT, iwT, ik, enear)


def _hgrn_kernel(bq_ref, bf_ref, bi_ref, bg_ref, lb_ref, gain_ref, o_ref,
                 st_ref, b_scr, q_scr, k_scr, v_scr, oi_scr, qd_s, kl_s, vv_s, dec_s, oi_s, upd_s, st_s,
                 kdT_s, vvT_s):
    R = bq_ref.shape[0]
    C = HGRN_CHUNK
    nC = R // C
    f32 = jnp.float32
    bf16 = jnp.bfloat16
    h = pl.program_id(1)

    @pl.when(pl.program_id(2) == 0)
    def _():
        st_ref[...] = jnp.zeros_like(st_ref)

    lb = lb_ref[pl.ds(h, 1), :]
    gain = gain_ref[pl.ds(h, 1), :]
    tril_incl = jnp.where(lax.broadcasted_iota(jnp.int32, (C, C), 1)
                          <= lax.broadcasted_iota(jnp.int32, (C, C), 0), 1.0, 0.0)
    srow = lax.broadcasted_iota(jnp.int32, (C, B_KEY_DIM), 0)

    def gates(r0):
        f = lb + (1.0 - lb) * _sigmoid(bf_ref[pl.ds(r0, C), :])
        qr = bq_ref[pl.ds(r0, C), :]
        return jnp.log(f), 1.0 - f, qr * _sigmoid(qr) * (B_KEY_DIM ** -0.5), bi_ref[pl.ds(r0, C), :]

    def cumdecay(g):
        tri = tril_incl.astype(bf16)
        g_hi = g.astype(bf16)
        rest = g - g_hi.astype(f32)
        g_mid = rest.astype(bf16)
        g_lo = (rest - g_mid.astype(f32)).astype(bf16)
        return (jnp.dot(tri, g_hi, preferred_element_type=f32) + jnp.dot(tri, g_mid, preferred_element_type=f32)
                + jnp.dot(tri, g_lo, preferred_element_type=f32))

    def advance(r0, st, qd, o_intra, upd, decay_row):
        o_inter = lax.dot_general(qd, st.astype(bf16), (((1,), (1,)), ((), ())), preferred_element_type=f32)
        og = bg_ref[pl.ds(r0, C), :]
        y = _rms(o_inter + o_intra, gain) * (og * _sigmoid(og))
        o_ref[pl.ds(r0, C), :] = y.astype(o_ref.dtype)
        return st * decay_row + upd

    f_all = lb + (1.0 - lb) * _sigmoid(bf_ref[...])
    g_all = jnp.log(f_all)
    decay = jnp.sum(g_all.reshape(nC, C, B_KEY_DIM), axis=1)
    safe = jnp.min(decay) >= -HGRN_SAFE_DECAY

    @pl.when(safe)
    def _():
        qr = bq_ref[...]
        qq = qr * _sigmoid(qr) * (B_KEY_DIM ** -0.5)
        kk = 1.0 - f_all
        GR = HGRN_GROUP * C
        r_id = lax.broadcasted_iota(jnp.int32, (GR, GR), 0)
        c_id = lax.broadcasted_iota(jnp.int32, (GR, GR), 1)
        tri_group = jnp.where((r_id // C == c_id // C) & (c_id <= r_id), 1.0, 0.0)
        tri_group_bf = tri_group.astype(bf16)
        g_hi = g_all.astype(bf16)
        rest = g_all - g_hi.astype(f32)
        g_mid = rest.astype(bf16)
        g_lo = (rest - g_mid.astype(f32)).astype(bf16)
        g_cat = jnp.concatenate([g_hi, g_mid, g_lo], axis=1)
        b_parts = []
        for gi in range(R // GR):
            bc = jnp.dot(tri_group_bf, g_cat[gi * GR:(gi + 1) * GR], preferred_element_type=f32)
            b_parts.append(bc[:, :B_KEY_DIM] + bc[:, B_KEY_DIM:2 * B_KEY_DIM] + bc[:, 2 * B_KEY_DIM:])
        b = jnp.concatenate(b_parts, axis=0)
        b_end = jnp.concatenate([jnp.broadcast_to(b[(c + 1) * C - 1:(c + 1) * C], (C, B_KEY_DIM))
                                 for c in range(nC)], axis=0)
        qd_s[...] = (qq * jnp.exp(b)).astype(bf16)
        kd = kk * jnp.exp(-b)
        kl_s[...] = (kk * jnp.exp(b_end - b)).astype(bf16)
        vv = bi_ref[...]
        vv_s[...] = vv.astype(bf16)
        dec_s[...] = jnp.exp(b_end)
        for gi in range(R // GR):
            kdT_s[gi] = kd[gi * GR:(gi + 1) * GR].T.astype(bf16)
        for c in range(nC):
            vvT_s[c] = vv[c * C:(c + 1) * C].T.astype(bf16)
        for gi in range(R // GR):
            rows = slice(gi * GR, (gi + 1) * GR)
            att = jnp.dot(qd_s[rows], kdT_s[gi], preferred_element_type=f32) * tri_group
            oi_s[rows] = jnp.dot(att.astype(bf16), vv_s[rows], preferred_element_type=f32)
        for c in range(nC):
            rows = slice(c * C, (c + 1) * C)
            upd_s[c] = jnp.dot(vvT_s[c], kl_s[rows], preferred_element_type=f32)
        st = st_ref[...]
        for c in range(nC):
            st_s[c] = st.T.astype(bf16)
            st = st * dec_s[c * C:c * C + 1] + upd_s[c]
        st_ref[...] = st
        for c in range(nC):
            rows = slice(c * C, (c + 1) * C)
            oi_s[rows] = oi_s[rows] + jnp.dot(qd_s[rows], st_s[c], preferred_element_type=f32)
        og = bg_ref[...]
        o_ref[...] = (_rms(oi_s[...], gain) * (og * _sigmoid(og))).astype(o_ref.dtype)

    @pl.when(jnp.logical_not(safe))
    def _():
        def body(c, st):
            r0 = pl.multiple_of(c * C, C)
            g, kk, qq, vv = gates(r0)
            b = cumdecay(g)
            b_last = b[C - 1:C, :]
            b_scr[...] = b
            q_scr[...] = qq
            k_scr[...] = kk
            v_scr[...] = vv

            def row(t, _):
                bt = b_scr[pl.ds(t, 1), :]
                qt = q_scr[pl.ds(t, 1), :]
                ex = jnp.where(srow <= t, bt - b_scr[...], -jnp.inf)
                a = jnp.sum(qt * k_scr[...] * jnp.exp(ex), axis=1, keepdims=True)
                oi_scr[pl.ds(t, 1), :] = jnp.sum(a * v_scr[...], axis=0, keepdims=True)
                return 0
            lax.fori_loop(0, C, row, 0)
            kd_last = (kk * jnp.exp(b_last - b)).astype(bf16)
            upd = lax.dot_general(vv.astype(bf16), kd_last, (((0,), (0,)), ((), ())),
                                  preferred_element_type=f32)
            return advance(r0, st, (qq * jnp.exp(b)).astype(bf16), oi_scr[...], upd, jnp.exp(b_last))
        st_ref[...] = lax.fori_loop(0, nC, body, st_ref[...])


def _hgrn(bq, bf, bi, bg, lb, gain, B, S):
    T = bq.shape[0]
    R = min(HGRN_ROWS, S)
    nR = S // R
    C = HGRN_CHUNK
    blk = pl.BlockSpec((R, B_KEY_DIM), lambda b, h, c: (b * nR + c, h))
    small = pl.BlockSpec((B_HEADS, B_KEY_DIM), lambda b, h, c: (0, 0))
    f32 = jnp.float32
    return pl.pallas_call(
        _hgrn_kernel,
        grid=(B, B_HEADS, nR),
        in_specs=[blk, blk, blk, blk, small, small],
        out_specs=blk,
        out_shape=jax.ShapeDtypeStruct((T, B_WIDTH), jnp.bfloat16),
        scratch_shapes=[pltpu.VMEM((B_VAL_DIM, B_KEY_DIM), f32)] +
                       [pltpu.VMEM((C, B_KEY_DIM), f32) for _ in range(5)] +
                       [pltpu.VMEM((R, B_KEY_DIM), jnp.bfloat16) for _ in range(3)] +
                       [pltpu.VMEM((R, B_KEY_DIM), f32) for _ in range(2)] +
                       [pltpu.VMEM((R // C, B_VAL_DIM, B_KEY_DIM), f32),
                        pltpu.VMEM((R // C, B_KEY_DIM, B_VAL_DIM), jnp.bfloat16),
                        pltpu.VMEM((R // (HGRN_GROUP * C), B_KEY_DIM, HGRN_GROUP * C), jnp.bfloat16),
                        pltpu.VMEM((R // C, B_VAL_DIM, C), jnp.bfloat16)],
        compiler_params=_cparams(("parallel", "parallel", "arbitrary")),
        name="hgrn",
    )(bq, bf, bi, bg, lb, gain)


def _merge_kernel(x_ref, ya_ref, yb_ref, ga_ref, gb_ref, wa_ref, wb_ref, wo_ref, g_ref, wrh_ref, wrl_ref, br_ref,
                  x1_ref, xn_ref, lg_ref):
    f32 = jnp.float32
    ma = jnp.dot(ya_ref[...], wa_ref[...], preferred_element_type=f32)
    mb = jnp.dot(yb_ref[...], wb_ref[...], preferred_element_type=f32)
    merged = ga_ref[...].astype(f32) * ma + gb_ref[...].astype(f32) * mb
    x1 = x_ref[...] + jnp.dot(merged.astype(jnp.bfloat16), wo_ref[...], preferred_element_type=f32)
    x1_ref[...] = x1
    hn = _rms(x1, g_ref[...])
    xn_ref[...] = _pack_bf16_pairs(hn)
    hn_hi = hn.astype(jnp.bfloat16)
    hn_lo = (hn - hn_hi.astype(f32)).astype(jnp.bfloat16)
    lg_ref[...] = (jnp.dot(hn_hi, wrh_ref[...], preferred_element_type=f32)
                   + jnp.dot(hn_lo, wrh_ref[...], preferred_element_type=f32)
                   + jnp.dot(hn_hi, wrl_ref[...], preferred_element_type=f32) + br_ref[...])


def _merge(x2, ya, yb, ga, gb, w_up_a, w_up_b, w_out, gain, w_router, b_router):
    T, D = x2.shape
    R = min(PROJ_ROWS, T)
    bf = jnp.bfloat16
    wr_hi = w_router.astype(bf)
    wr_lo = (w_router - wr_hi.astype(jnp.float32)).astype(bf)
    ins = [x2, ya, yb, ga, gb, w_up_a.astype(bf), w_up_b.astype(bf), w_out.astype(bf),
           gain.reshape(1, D), wr_hi, wr_lo, b_router.reshape(1, N_EXPERTS)]
    row = lambda n: pl.BlockSpec((R, n), lambda i: (i, 0))
    full = lambda a: pl.BlockSpec(a.shape, lambda i: (0,) * a.ndim)
    in_specs = [row(D), row(A_WIDTH), row(B_WIDTH), row(D), row(D)] + [full(a) for a in ins[5:]]
    return pl.pallas_call(
        _merge_kernel,
        grid=(T // R,),
        in_specs=in_specs,
        out_specs=[row(D), row(D // 2), row(N_EXPERTS)],
        out_shape=[jax.ShapeDtypeStruct((T, D), jnp.float32), jax.ShapeDtypeStruct((T, D // 2), jnp.uint32),
                   jax.ShapeDtypeStruct((T, N_EXPERTS), jnp.float32)],
        compiler_params=_cparams(("parallel",)),
        name="merge",
    )(*ins)


def _route_kernel(lg_ref, eidx_ref, gate_ref, rank_ref, cnt_ref, run_ref):
    R = lg_ref.shape[0]
    f32 = jnp.float32

    @pl.when(pl.program_id(0) == 0)
    def _():
        run_ref[...] = jnp.zeros_like(run_ref)

    lg = lg_ref[...].T
    expert = lax.broadcasted_iota(jnp.int32, (N_EXPERTS, R), 0)
    work = lg
    onehots, vals, idxs = [], [], []
    for _ in range(TOP_K):
        m = jnp.max(work, axis=0, keepdims=True)
        idx = jnp.min(jnp.where(work == m, expert, N_EXPERTS), axis=0, keepdims=True)
        oh = expert == idx
        onehots.append(oh)
        vals.append(m)
        idxs.append(idx)
        work = jnp.where(oh, -jnp.inf, work)
    ex = [jnp.exp(v - vals[0]) for v in vals]
    den = ex[0] + ex[1] + ex[2] + ex[3]
    chosen = jnp.where(onehots[0] | onehots[1] | onehots[2] | onehots[3], 1.0, 0.0)
    earlier = jnp.where(lax.broadcasted_iota(jnp.int32, (R, R), 0)
                        < lax.broadcasted_iota(jnp.int32, (R, R), 1), 1.0, 0.0).astype(jnp.bfloat16)
    before = jnp.dot(chosen.astype(jnp.bfloat16), earlier, preferred_element_type=f32) + run_ref[...]
    slot = lax.broadcasted_iota(jnp.int32, (TOP_K, R), 0)
    eidx = jnp.zeros((TOP_K, R), jnp.int32)
    gate = jnp.zeros((TOP_K, R), f32)
    rank = jnp.zeros((TOP_K, R), f32)
    for k in range(TOP_K):
        eidx = jnp.where(slot == k, idxs[k], eidx)
        gate = jnp.where(slot == k, ex[k] / den, gate)
        rk = jnp.sum(jnp.where(onehots[k], before, 0.0), axis=0, keepdims=True)
        rank = jnp.where(slot == k, rk, rank)
    eidx_ref[...] = eidx
    gate_ref[...] = gate
    rank_ref[...] = rank.astype(jnp.int32)
    run_ref[...] = run_ref[...] + jnp.sum(chosen, axis=1, keepdims=True)
    cnt_ref[...] = run_ref[...].astype(jnp.int32)


def _route(logits):
    T = logits.shape[0]
    R = min(ROUTE_ROWS, T)
    col = pl.BlockSpec((TOP_K, R), lambda i: (0, i))
    return pl.pallas_call(
        _route_kernel,
        grid=(T // R,),
        in_specs=[pl.BlockSpec((R, N_EXPERTS), lambda i: (i, 0))],
        out_specs=[col, col, col, pl.BlockSpec((N_EXPERTS, 1), lambda i: (0, 0))],
        out_shape=[jax.ShapeDtypeStruct((TOP_K, T), jnp.int32), jax.ShapeDtypeStruct((TOP_K, T), jnp.float32),
                   jax.ShapeDtypeStruct((TOP_K, T), jnp.int32), jax.ShapeDtypeStruct((N_EXPERTS, 1), jnp.int32)],
        scratch_shapes=[pltpu.VMEM((N_EXPERTS, 1), jnp.float32)],
        compiler_params=_cparams(("arbitrary",)),
        name="route",
    )(logits)


def _sc_gather_rows(table, idx):
    W = table.shape[1]
    M = idx.shape[0]
    workers = V7X_SC_CORES * V7X_SC_SUBCORES
    per_worker = M // workers
    pieces = per_worker // SC_GATHER_ROWS
    assert per_worker * workers == M and pieces * SC_GATHER_ROWS == per_worker and pieces % 2 == 0
    mesh = plsc.VectorSubcoreMesh(core_axis_name="core", subcore_axis_name="subcore",
                                  num_cores=V7X_SC_CORES, num_subcores=V7X_SC_SUBCORES)

    @functools.partial(
        pl.kernel, mesh=mesh,
        out_type=jax.ShapeDtypeStruct((M, W), table.dtype),
        scratch_types=[pltpu.VMEM((per_worker,), jnp.int32),
                       pltpu.VMEM((SC_GATHER_ROWS, W), table.dtype),
                       pltpu.VMEM((SC_GATHER_ROWS, W), table.dtype),
                       pltpu.SemaphoreType.DMA, pltpu.SemaphoreType.DMA],
    )
    def gather(table_hbm, idx_hbm, out_hbm, idx_v, rows_a, rows_b, sem_a, sem_b):
        worker = lax.axis_index("subcore") * V7X_SC_CORES + lax.axis_index("core")
        base = pl.multiple_of(worker * per_worker, SC_GATHER_ROWS)
        pltpu.sync_copy(idx_hbm.at[pl.ds(base, per_worker)], idx_v)
        bufs = ((rows_a, sem_a), (rows_b, sem_b))

        def fetch(g, buf, sem):
            off = pl.multiple_of(g * SC_GATHER_ROWS, SC_GATHER_ROWS)
            return pltpu.make_async_copy(table_hbm.at[idx_v.at[pl.ds(off, SC_GATHER_ROWS)]], buf, sem)

        fetch(0, *bufs[0]).start()

        @pl.loop(0, pieces, step=2)
        def _(g0):
            for half in range(2):
                g = g0 + half
                buf, sem = bufs[half]
                fetch(g, buf, sem).wait()

                @pl.when(g + 1 < pieces)
                def _():
                    fetch(g + 1, *bufs[1 - half]).start()

                off = pl.multiple_of(g * SC_GATHER_ROWS, SC_GATHER_ROWS)
                pltpu.sync_copy(buf, out_hbm.at[pl.ds(base + off, SC_GATHER_ROWS)])

    return gather(table, idx)


def _sc_scatter_rows(rows, dest, n_out):
    T, W = rows.shape
    slots = dest.shape[0]
    workers = V7X_SC_CORES * V7X_SC_SUBCORES
    per_worker = T // workers
    pieces = per_worker // SC_GATHER_ROWS
    assert per_worker * workers == T and pieces * SC_GATHER_ROWS == per_worker and pieces % 2 == 0
    idx = dest.reshape(slots, workers, pieces, SC_GATHER_ROWS).transpose(1, 2, 0, 3)
    mesh = plsc.VectorSubcoreMesh(core_axis_name="core", subcore_axis_name="subcore",
                                  num_cores=V7X_SC_CORES, num_subcores=V7X_SC_SUBCORES)

    @functools.partial(
        pl.kernel, mesh=mesh,
        out_type=jax.ShapeDtypeStruct((n_out, W), rows.dtype),
        scratch_types=[pltpu.VMEM((pieces, slots, SC_GATHER_ROWS), jnp.int32),
                       pltpu.VMEM((SC_GATHER_ROWS, W), rows.dtype),
                       pltpu.VMEM((SC_GATHER_ROWS, W), rows.dtype),
                       pltpu.SemaphoreType.DMA, pltpu.SemaphoreType.DMA, pltpu.SemaphoreType.DMA],
    )
    def scatter(rows_hbm, idx_hbm, out_hbm, idx_v, rows_a, rows_b, sem_a, sem_b, sem_out):
        worker = lax.axis_index("subcore") * V7X_SC_CORES + lax.axis_index("core")
        base = pl.multiple_of(worker * per_worker, SC_GATHER_ROWS)
        pltpu.sync_copy(idx_hbm.at[worker], idx_v)
        bufs = ((rows_a, sem_a), (rows_b, sem_b))

        def fetch(g, buf, sem):
            off = pl.multiple_of(g * SC_GATHER_ROWS, SC_GATHER_ROWS)
            return pltpu.make_async_copy(rows_hbm.at[pl.ds(base + off, SC_GATHER_ROWS)], buf, sem)

        fetch(0, *bufs[0]).start()

        @pl.loop(0, pieces, step=2)
        def _(g0):
            for half in range(2):
                g = g0 + half
                buf, sem = bufs[half]
                fetch(g, buf, sem).wait()

                @pl.when(g + 1 < pieces)
                def _():
                    fetch(g + 1, *bufs[1 - half]).start()

                puts = [pltpu.make_async_copy(buf, out_hbm.at[idx_v.at[g, k]], sem_out) for k in range(slots)]
                for put in puts:
                    put.start()
                for put in puts:
                    put.wait()

    return scatter(rows, idx)


def _experts_kernel(be_ref, nb_ref, first_ref, slot_ref, next_ref,
                    x_ref, wgu_hbm, bgu_ref, wd_hbm, bd_ref, o_ref,
                    wgu_f32, wd_f32, wgu_bf, wd_bf, sems):
    f32 = jnp.float32
    d_ff = wd_bf.shape[0]
    i = pl.program_id(0)
    live = i < nb_ref[0]

    def weight_copies(e, s):
        return (pltpu.make_async_copy(wgu_hbm.at[e], wgu_f32.at[s], sems.at[s, 0]),
                pltpu.make_async_copy(wd_hbm.at[e], wd_f32.at[s], sems.at[s, 1]))

    @pl.when(live & (first_ref[i] == 1))
    def _():
        e = be_ref[i]
        s = slot_ref[i]

        @pl.when(i == 0)
        def _():
            for cp in weight_copies(e, s):
                cp.start()

        for cp in weight_copies(e, s):
            cp.wait()

        @pl.when(next_ref[i] >= 0)
        def _():
            for cp in weight_copies(next_ref[i], 1 - s):
                cp.start()

        wgu_bf[...] = wgu_f32[s].astype(jnp.bfloat16)
        wd_bf[...] = wd_f32[s].astype(jnp.bfloat16)

    @pl.when(live)
    def _():
        x_hi, x_lo = _unpack_bf16_pairs(x_ref[...])
        half = x_hi.shape[1]
        gu = (jnp.dot(x_hi, wgu_bf[:half, :], preferred_element_type=f32)
              + jnp.dot(x_lo, wgu_bf[half:, :], preferred_element_type=f32) + bgu_ref[0])
        gate = jnp.minimum(gu[:, :d_ff], SWIGLU_LIMIT)
        lin = jnp.clip(gu[:, d_ff:], -SWIGLU_LIMIT, SWIGLU_LIMIT)
        act = (lin + 1.0) * gate * _sigmoid(SWIGLU_ALPHA * gate)
        y = jnp.dot(act.astype(jnp.bfloat16), wd_bf[...], preferred_element_type=f32) + bd_ref[0]
        o_ref[...] = _pack_bf16_pairs(y)

    @pl.when(pl.program_id(0) >= nb_ref[0])
    def _():
        o_ref[...] = jnp.zeros_like(o_ref)


def _experts(xs, plan, w_gu, b_gu, w_down, b_down):
    P, W = xs.shape
    E, D, F2 = w_gu.shape
    nb = P // EXPERT_ROWS
    by_expert = lambda i, be, *_: (be[i], 0, 0)
    grid_spec = pltpu.PrefetchScalarGridSpec(
        num_scalar_prefetch=5,
        grid=(nb,),
        in_specs=[
            pl.BlockSpec((EXPERT_ROWS, W), lambda i, *_: (i, 0)),
            pl.BlockSpec(memory_space=pl.ANY),
            pl.BlockSpec((1, 1, F2), by_expert),
            pl.BlockSpec(memory_space=pl.ANY),
            pl.BlockSpec((1, 1, D), by_expert),
        ],
        out_specs=pl.BlockSpec((EXPERT_ROWS, W), lambda i, *_: (i, 0)),
        scratch_shapes=[pltpu.VMEM((2, D, F2), jnp.float32), pltpu.VMEM((2, F2 // 2, D), jnp.float32),
                        pltpu.VMEM((D, F2), jnp.bfloat16), pltpu.VMEM((F2 // 2, D), jnp.bfloat16),
                        pltpu.SemaphoreType.DMA((2, 2))],
    )
    return pl.pallas_call(
        _experts_kernel,
        grid_spec=grid_spec,
        out_shape=jax.ShapeDtypeStruct((P, W), jnp.uint32),
        compiler_params=_cparams(("arbitrary",)),
        name="experts",
    )(*plan, xs, w_gu, b_gu.reshape(E, 1, F2), w_down, b_down.reshape(E, 1, D))


def _combine_kernel(ya_ref, x1_ref, gate_ref, g_ref, o_ref):
    half = x1_ref.shape[1] // 2
    f32 = jnp.float32
    gate = gate_ref[...].T
    x1 = x1_ref[...]
    y_hi = x1[:, :half]
    y_lo = x1[:, half:]
    for k in range(TOP_K):
        hi, lo = _unpack_bf16_pairs(ya_ref[k])
        y_hi = y_hi + gate[:, k:k + 1] * hi.astype(f32)
        y_lo = y_lo + gate[:, k:k + 1] * lo.astype(f32)
    o_ref[...] = _rms(jnp.concatenate([y_hi, y_lo], axis=1), g_ref[...])


def _combine(ya, x1, gates, gain):
    T, D = x1.shape
    R = min(COMBINE_ROWS, T)
    row = lambda w: pl.BlockSpec((R, w), lambda i: (i, 0))
    return pl.pallas_call(
        _combine_kernel,
        grid=(T // R,),
        in_specs=[pl.BlockSpec((TOP_K, R, D // 2), lambda i: (0, i, 0)), row(D),
                  pl.BlockSpec((TOP_K, R), lambda i: (0, i)), pl.BlockSpec((1, D), lambda i: (0, 0))],
        out_specs=row(D),
        out_shape=jax.ShapeDtypeStruct((T, D), jnp.float32),
        compiler_params=_cparams(("parallel",)),
        name="combine",
    )(ya, x1, gates, gain.reshape(1, D))


def _moe_plan(eidx, rank, counts, A):
    counts = counts.reshape(N_EXPERTS)
    padded = (counts + EXPERT_ROWS - 1) // EXPERT_ROWS * EXPERT_ROWS
    pad_ends = jnp.cumsum(padded)
    pad_starts = pad_ends - padded
    n_blocks = -(-A // EXPERT_ROWS) + N_EXPERTS
    ids = jnp.arange(N_EXPERTS, dtype=jnp.int32)
    dest = rank + jnp.sum(jnp.where(eidx[None] == ids[:, None, None], pad_starts[:, None, None], 0), axis=0)
    block_start = jnp.arange(n_blocks, dtype=pad_ends.dtype) * EXPERT_ROWS
    block_expert = jnp.minimum(jnp.sum(pad_ends[None, :] <= block_start[:, None], axis=1),
                               N_EXPERTS - 1).astype(jnp.int32)
    n_used = (pad_ends[-1] // EXPERT_ROWS).astype(jnp.int32).reshape(1)
    has_rows = counts > 0
    ordinal = jnp.cumsum(has_rows.astype(jnp.int32)) - 1
    later = has_rows[None, :] & (ids[None, :] > ids[:, None])
    next_expert = jnp.where(jnp.any(later, axis=1), jnp.argmax(later, axis=1), -1).astype(jnp.int32)
    tables = jnp.stack([pad_starts, ordinal % 2, next_expert], axis=1).astype(jnp.float32)
    onehot = (block_expert[:, None] == ids[None, :]).astype(jnp.float32)
    looked = jnp.dot(onehot, tables, precision=lax.Precision.HIGHEST).astype(jnp.int32)
    block_first = ((block_start == looked[:, 0]) & (block_start < pad_ends[-1])).astype(jnp.int32)
    plan = (block_expert, n_used, block_first, looked[:, 1], looked[:, 2])
    return dest.astype(jnp.int32), plan, n_blocks


def kernel(x, w_in, w_up_a, w_up_b, w_out, norm_mix, norm_ffn, norm_final, hgrn_norm,
           lb_logits, rel_bias, w_router, b_router, w_gu, b_gu, w_down, b_down):
    B, S, D = x.shape
    T = B * S
    assert w_in.shape[0] == 1, "the final rmsnorm is fused into the single layer's combine stage"
    lb_all = jnp.cumsum(jax.nn.softmax(lb_logits.astype(jnp.float32), axis=0), axis=0)
    x2 = x.reshape(T, D)
    (k, ik, qT, vT, iqT, iwT, bq, bf, bi, bg, ga, gb) = _inproj(x2, norm_mix[0], w_in[0], B, S)
    ya = _dsa(k, ik, qT, vT, iqT, iwT, rel_bias, B, S)
    yb = _hgrn(bq, bf, bi, bg, lb_all[0].reshape(B_HEADS, B_KEY_DIM), hgrn_norm[0], B, S)
    x1, xn, logits = _merge(x2, ya, yb, ga, gb, w_up_a[0], w_up_b[0], w_out[0], norm_ffn[0],
                            w_router[0], b_router[0])
    eidx, gates, rank, counts = _route(logits)
    dest, plan, n_blocks = _moe_plan(eidx, rank, counts, T * TOP_K)
    P = n_blocks * EXPERT_ROWS
    A = T * TOP_K
    xs = _sc_scatter_rows(xn, dest, P)
    y_buf = _experts(xs, plan, w_gu[0], b_gu[0], w_down[0], b_down[0])
    ya = _sc_gather_rows(y_buf, dest.reshape(A)).reshape(TOP_K, T, D // 2)
    out = _combine(ya, x1, gates, norm_final)
    return out.reshape(B, S, D)
```

```python
import functools
import math

import numpy as np
import jax
import jax.numpy as jnp
from jax import lax
from jax.experimental import pallas as pl
from jax.experimental.pallas import tpu as pltpu
from jax.experimental.pallas import tpu_sc as plsc

A_HEADS = 8
A_HEAD_DIM = 64
IDX_HEADS = 8
IDX_DIM = 32
TOPK_MAX = 256
REL_BUCKETS = 32
REL_MAX_DIST = 128
B_HEADS = 4
B_KEY_DIM = 128
B_VAL_DIM = 128
N_EXPERTS = 32
TOP_K = 4
SWIGLU_LIMIT = 7.0
SWIGLU_ALPHA = 1.702
EPS = 1e-6
LOG2_E = math.log2(math.e)

A_WIDTH = A_HEADS * A_HEAD_DIM
B_WIDTH = B_HEADS * B_VAL_DIM
IDX_WIDTH = IDX_HEADS * IDX_DIM

V7X_LANES = 128
V7X_SUBLANES = 8
V7X_VMEM_LIMIT_BYTES = 56 * 1024 * 1024
V7X_SC_CORES = 2
V7X_SC_SUBCORES = 16

PROJ_ROWS = 512
ATT_Q = 256
ATT_KC = 128
SCORE_UNROLL = 4
COUNT_UNROLL = 2
FAR_UNROLL = 4
HGRN_ROWS = 1024
HGRN_CHUNK = 64
HGRN_GROUP = 4
HGRN_SAFE_DECAY = 70.0
ROUTE_ROWS = 512
EXPERT_ROWS = 512
COMBINE_ROWS = 512
SC_GATHER_ROWS = 64
MASK_NEG = -1e30
BISECT_FAST_ITERS = 26
BISECT_BLIND_ITERS = 16


def _cparams(dims):
    return pltpu.CompilerParams(dimension_semantics=dims, vmem_limit_bytes=V7X_VMEM_LIMIT_BYTES)


def _rms(x, gain):
    return x * lax.rsqrt(jnp.mean(x * x, axis=-1, keepdims=True) + EPS) * gain


def _sigmoid(x):
    return 1.0 / (1.0 + jnp.exp(-x))


def _pack_bf16_pairs(x):
    n = x.shape[1] // 2
    as_bits = lambda v: lax.bitcast_convert_type(v.astype(jnp.bfloat16).astype(jnp.float32), jnp.uint32)
    return (as_bits(x[:, :n]) & jnp.uint32(0xFFFF0000)) | (as_bits(x[:, n:]) >> 16)


def _unpack_bf16_pairs(w):
    hi = lax.bitcast_convert_type(w & jnp.uint32(0xFFFF0000), jnp.float32).astype(jnp.bfloat16)
    lo = lax.bitcast_convert_type(w << 16, jnp.float32).astype(jnp.bfloat16)
    return hi, lo


def _blocked_loop(n, step, carry, unroll):
    def run(first, count, cr):
        for t in range(count):
            cr = step(first + t, cr)
        return cr

    carry = lax.fori_loop(0, n // unroll, lambda j, cr: run(j * unroll, unroll, cr), carry)
    base = (n // unroll) * unroll
    piece = unroll // 2
    while piece >= 1:
        take = (n & piece) != 0
        carry = lax.cond(take, functools.partial(run, base, piece), lambda cr: cr, carry)
        base = base + jnp.where(take, piece, 0)
        piece //= 2
    return carry


def _fold_rows(x, op):
    return op(x.reshape(x.shape[0] // V7X_SUBLANES, V7X_SUBLANES, x.shape[1]), axis=0)


def _inproj_kernel(x_ref, g_ref, wk_ref, wik_ref, wqT_ref, wvT_ref, wiqT_ref, wiwT_ref, wb_ref, wg_ref,
                   k_ref, ik_ref, qT_ref, vT_ref, iqT_ref, iwT_ref, bq_ref, bf_ref, bi_ref, bg_ref,
                   ga_ref, gb_ref):
    x = x_ref[...]
    hn = _rms(x, g_ref[...]).astype(jnp.bfloat16)

    def mm(w_ref):
        return jnp.dot(hn, w_ref[...], preferred_element_type=jnp.float32)

    def mm_t(w_ref):
        return lax.dot_general(w_ref[...], hn, (((1,), (1,)), ((), ())),
                               preferred_element_type=jnp.float32)

    k_ref[...] = mm(wk_ref).astype(jnp.bfloat16)
    ik_ref[...] = mm(wik_ref).astype(jnp.bfloat16)
    qT_ref[0] = (mm_t(wqT_ref) * (A_HEAD_DIM ** -0.5 * LOG2_E)).astype(jnp.bfloat16)
    vT_ref[0] = mm_t(wvT_ref).astype(jnp.bfloat16)
    iqT_ref[0] = mm_t(wiqT_ref).astype(jnp.bfloat16)
    iwT_ref[0] = mm_t(wiwT_ref) * ((IDX_HEADS * IDX_DIM) ** -0.5)
    hb = mm(wb_ref)
    bq_ref[...] = hb[:, 0 * B_WIDTH:1 * B_WIDTH]
    bf_ref[...] = hb[:, 1 * B_WIDTH:2 * B_WIDTH]
    bi_ref[...] = hb[:, 2 * B_WIDTH:3 * B_WIDTH]
    bg_ref[...] = hb[:, 3 * B_WIDTH:4 * B_WIDTH]
    d = ga_ref.shape[-1]
    hg = mm(wg_ref)
    ga_ref[...] = _sigmoid(hg[:, :d]).astype(jnp.bfloat16)
    gb_ref[...] = _sigmoid(hg[:, d:]).astype(jnp.bfloat16)


def _inproj(x2, gain, w_in, B, S):
    T, D = x2.shape
    R = min(PROJ_ROWS, S)
    nS = S // R
    o = np.cumsum((0, A_WIDTH, A_WIDTH, A_WIDTH, IDX_WIDTH, IDX_HEADS, IDX_DIM,
                   B_WIDTH, B_WIDTH, B_WIDTH, B_WIDTH, D, D))
    bf = jnp.bfloat16
    wqT = w_in[:, o[0]:o[1]].T.astype(bf)
    wk = w_in[:, o[1]:o[2]].astype(bf)
    wvT = w_in[:, o[2]:o[3]].T.astype(bf)
    wiqT = w_in[:, o[3]:o[4]].T.astype(bf)
    wiwT = w_in[:, o[4]:o[5]].T.astype(bf)
    wik = w_in[:, o[5]:o[6]].astype(bf)
    wb = w_in[:, o[6]:o[10]].astype(bf)
    wg = w_in[:, o[10]:o[12]].astype(bf)

    def full(a):
        return pl.BlockSpec(a.shape, lambda b, i: (0,) * a.ndim)

    row = lambda n: pl.BlockSpec((R, n), lambda b, i: (b * nS + i, 0))
    colT = lambda n: pl.BlockSpec((1, n, R), lambda b, i: (b, 0, i))
    f32 = jnp.float32
    outs = [
        (jax.ShapeDtypeStruct((T, A_WIDTH), bf), row(A_WIDTH)),
        (jax.ShapeDtypeStruct((T, IDX_DIM), bf), row(IDX_DIM)),
        (jax.ShapeDtypeStruct((B, A_WIDTH, S), bf), colT(A_WIDTH)),
        (jax.ShapeDtypeStruct((B, A_WIDTH, S), bf), colT(A_WIDTH)),
        (jax.ShapeDtypeStruct((B, IDX_WIDTH, S), bf), colT(IDX_WIDTH)),
        (jax.ShapeDtypeStruct((B, IDX_HEADS, S), f32), colT(IDX_HEADS)),
        (jax.ShapeDtypeStruct((T, B_WIDTH), f32), row(B_WIDTH)),
        (jax.ShapeDtypeStruct((T, B_WIDTH), f32), row(B_WIDTH)),
        (jax.ShapeDtypeStruct((T, B_WIDTH), f32), row(B_WIDTH)),
        (jax.ShapeDtypeStruct((T, B_WIDTH), f32), row(B_WIDTH)),
        (jax.ShapeDtypeStruct((T, D), bf), row(D)),
        (jax.ShapeDtypeStruct((T, D), bf), row(D)),
    ]
    ins = [x2, gain.reshape(1, D), wk, wik, wqT, wvT, wiqT, wiwT, wb, wg]
    in_specs = [row(D)] + [full(a) for a in ins[1:]]
    return pl.pallas_call(
        _inproj_kernel,
        grid=(B, nS),
        in_specs=in_specs,
        out_specs=[s for _, s in outs],
        out_shape=[s for s, _ in outs],
        compiler_params=_cparams(("parallel", "parallel")),
        name="inproj",
    )(*ins)


def _t5_bucket_table(n):
    d = np.arange(n)
    max_exact = REL_BUCKETS // 2
    nf = np.maximum(d, 1).astype(np.float64)
    large = max_exact + (np.log(nf / max_exact) / math.log(REL_MAX_DIST / max_exact)
                         * (REL_BUCKETS - max_exact)).astype(np.int32)
    large = np.minimum(large, REL_BUCKETS - 1)
    return np.where(d < max_exact, d, large)


def _dsa_kernel(qT_ref, k_ref, vT_ref, iqT_ref, iwT_ref, ik_ref, enear_ref,
                o_ref, sc_ref, qh_scr, m_scr, acc_scr, *, topk):
    TQ = qT_ref.shape[2]
    KC = TQ
    i = pl.program_id(1)
    nch = i + 1
    q0 = i * TQ
    f32 = jnp.float32
    bf16 = jnp.bfloat16
    key_id = lax.broadcasted_iota(jnp.int32, (KC, TQ), 0)
    qry_id = lax.broadcasted_iota(jnp.int32, (KC, TQ), 1)

    def col_reduce(x, op):
        return op(_fold_rows(x, op), axis=0, keepdims=True)

    iw = iwT_ref[0]

    def score_chunk(c, carry):
        rmin, rmax = carry
        k0 = pl.multiple_of(c * KC, KC)
        ik = ik_ref[pl.ds(k0, KC), :]
        acc = jnp.zeros((KC, TQ), f32)
        for h in range(IDX_HEADS):
            sh = jnp.dot(ik, iqT_ref[0, h * IDX_DIM:(h + 1) * IDX_DIM, :], preferred_element_type=f32)
            acc = acc + jnp.maximum(sh, 0.0) * iw[h:h + 1, :]
        valid = (k0 + key_id) <= (q0 + qry_id)
        sc_ref[pl.ds(k0, KC), :] = jnp.where(valid, acc, MASK_NEG)
        rmin = jnp.minimum(rmin, _fold_rows(jnp.where(valid, acc, -MASK_NEG), jnp.min))
        rmax = jnp.maximum(rmax, _fold_rows(jnp.where(valid, acc, MASK_NEG), jnp.max))
        return rmin, rmax

    rmin8, rmax8 = _blocked_loop(
        nch, score_chunk,
        (jnp.full((V7X_SUBLANES, TQ), -MASK_NEG, f32), jnp.full((V7X_SUBLANES, TQ), MASK_NEG, f32)),
        SCORE_UNROLL)
    rmin = jnp.min(rmin8, axis=0, keepdims=True)
    rmax = jnp.max(rmax8, axis=0, keepdims=True)

    def count_where(pred_fn):
        def one(c, acc):
            k0 = pl.multiple_of(c * KC, KC)
            blk = sc_ref[pl.ds(k0, KC), :]
            return acc + _fold_rows(jnp.where(pred_fn(blk), 1.0, 0.0), jnp.sum)
        acc = _blocked_loop(nch, one, jnp.zeros((V7X_SUBLANES, TQ), f32), COUNT_UNROLL)
        return jnp.sum(acc, axis=0, keepdims=True)

    def band_min_max(lo, hi):
        def body(c, carry):
            bmin, bmax = carry
            k0 = pl.multiple_of(c * KC, KC)
            blk = sc_ref[pl.ds(k0, KC), :]
            bmin = jnp.minimum(bmin, _fold_rows(jnp.where(blk >= lo, blk, -MASK_NEG), jnp.min))
            bmax = jnp.maximum(bmax, _fold_rows(jnp.where(blk < hi, blk, MASK_NEG), jnp.max))
            return bmin, bmax
        bmin8, bmax8 = lax.fori_loop(
            0, nch, body,
            (jnp.full((V7X_SUBLANES, TQ), -MASK_NEG, f32), jnp.full((V7X_SUBLANES, TQ), MASK_NEG, f32)))
        return jnp.min(bmin8, axis=0, keepdims=True), jnp.max(bmax8, axis=0, keepdims=True)

    kf = float(topk)
    n_valid = (q0 + 1 + lax.broadcasted_iota(jnp.int32, (1, TQ), 1)).astype(f32)
    lo0 = rmin
    cnt0 = n_valid
    hi0 = rmax + jnp.maximum(jnp.abs(rmax) * 2.0 ** -20, 1e-30)
    done0 = jnp.where(cnt0 <= kf, 1.0, 0.0)

    def probe(st, lo_s, mid, tie):
        it, lo, hi, cnt, done = st
        active = done < 0.5
        lo_s = jnp.where(active, lo_s, lo)
        c = count_where(lambda blk: blk >= mid)
        feas = c >= kf
        move = active & jnp.logical_not(tie)
        lo_n = jnp.where(move & feas, mid, lo_s)
        cnt_n = jnp.where(move & feas, c, cnt)
        hi_n = jnp.where(move & jnp.logical_not(feas), mid, hi)
        done_n = jnp.where((active & tie) | (cnt_n <= kf), 1.0, done)
        return it + 1, lo_n, hi_n, cnt_n, done_n

    def halve(st):
        _, lo, hi, _, _ = st
        half = lo + 0.5 * (hi - lo)
        stuck = (half <= lo) | (half >= hi)
        return probe(st, lo, half, stuck)

    def snap(st):
        _, lo, hi, _, _ = st
        bmin, bmax = band_min_max(lo, hi)
        mid = bmin + 0.5 * (bmax - bmin)
        return probe(st, bmin, jnp.where(mid <= bmin, bmax, mid), bmax <= bmin)

    st = lax.fori_loop(0, BISECT_BLIND_ITERS // 2, lambda _, st: halve(halve(st)),
                       (jnp.int32(0), lo0, hi0, cnt0, done0))
    st = lax.while_loop(lambda st: (jnp.min(st[-1]) < 0.5) & (st[0] < BISECT_FAST_ITERS),
                        lambda st: halve(halve(st)), st)
    _, thr, _, cnt_thr, _ = lax.while_loop(lambda st: jnp.min(st[-1]) < 0.5, snap, st)

    tie_overflow = jnp.max(cnt_thr) > kf

    @pl.when(jnp.logical_not(tie_overflow))
    def _():
        def mask_chunk(c, _):
            k0 = pl.multiple_of(c * KC, KC)
            sc_ref[pl.ds(k0, KC), :] = jnp.where(sc_ref[pl.ds(k0, KC), :] >= thr, 0.0, MASK_NEG)
            return 0
        lax.fori_loop(0, nch, mask_chunk, 0)

    @pl.when(tie_overflow)
    def _():
        need = kf - count_where(lambda blk: blk > thr)
        tril = jnp.where(lax.broadcasted_iota(jnp.int32, (KC, KC), 1)
                         <= lax.broadcasted_iota(jnp.int32, (KC, KC), 0), 1.0, 0.0).astype(bf16)

        def mask_chunk(c, run):
            k0 = pl.multiple_of(c * KC, KC)
            blk = sc_ref[pl.ds(k0, KC), :]
            eq = jnp.where(blk == thr, 1.0, 0.0)
            pref = jnp.dot(tril, eq.astype(bf16), preferred_element_type=f32)
            sel = (blk > thr) | ((eq > 0.5) & (run + pref <= need))
            sc_ref[pl.ds(k0, KC), :] = jnp.where(sel, 0.0, MASK_NEG)
            return run + pref[KC - 1:KC, :]

        lax.fori_loop(0, nch, mask_chunk, jnp.zeros((1, TQ), f32))

    AK = min(ATT_KC, TQ)
    per = TQ // AK
    head0_q = (lax.broadcasted_iota(jnp.int32, (V7X_LANES, TQ), 0) // A_HEAD_DIM) == 0
    n_pairs = A_HEADS // 2

    m_scr[...] = jnp.full(m_scr.shape, MASK_NEG, f32)
    acc_scr[...] = jnp.zeros(acc_scr.shape, f32)
    v_row = lax.broadcasted_iota(jnp.int32, (V7X_LANES, AK), 0)
    denom_row = [A_HEAD_DIM * (1 - sub) for sub in range(2)]
    for p in range(n_pairs):
        q_pair = qT_ref[0, p * V7X_LANES:(p + 1) * V7X_LANES, :]
        zq = jnp.zeros_like(q_pair)
        qh_scr[2 * p] = jnp.where(head0_q, q_pair, zq)
        qh_scr[2 * p + 1] = jnp.where(head0_q, zq, q_pair)

    def step(c, bias_rows):
        k0 = pl.multiple_of(c * AK, AK)
        msk = sc_ref[pl.ds(k0, AK), :]
        for p in range(n_pairs):
            kp = k_ref[pl.ds(k0, AK), p * V7X_LANES:(p + 1) * V7X_LANES]
            vp = vT_ref[0, p * V7X_LANES:(p + 1) * V7X_LANES, pl.ds(k0, AK)]
            for sub in range(2):
                h = 2 * p + sub
                s = jnp.dot(kp, qh_scr[h], preferred_element_type=f32) + msk
                if bias_rows is not None:
                    s = s + enear_ref[h, bias_rows, :]
                m = m_scr[h:h + 1, :]
                m_new = jnp.maximum(m, col_reduce(s, jnp.max))
                alpha = jnp.exp2(m - m_new)
                pr = jnp.exp2(s - m_new)
                m_scr[h:h + 1, :] = m_new
                v_aug = jnp.where(v_row == denom_row[sub], jnp.ones_like(vp), vp)
                acc_scr[h] = alpha * acc_scr[h] + jnp.dot(v_aug, pr.astype(bf16), preferred_element_type=f32)

    def far(blk, _):
        for jj in range(per):
            step(blk * per + jj, None)
        return 0

    def near(block, first_chunk):
        for jj in range(per):
            step(first_chunk + jj, slice(block * TQ + jj * AK, block * TQ + (jj + 1) * AK))

    _blocked_loop(jnp.maximum(i - 1, 0), far, 0, FAR_UNROLL)

    @pl.when(i >= 1)
    def _():
        near(0, (i - 1) * per)
        near(1, i * per)

    @pl.when(i == 0)
    def _():
        near(1, i * per)
    for p in range(n_pairs):
        outs = [acc_scr[2 * p + sub] / acc_scr[2 * p + sub, denom_row[sub]:denom_row[sub] + 1, :]
                for sub in range(2)]
        o_pair = jnp.where(head0_q, outs[0], outs[1])
        o_ref[:, p * V7X_LANES:(p + 1) * V7X_LANES] = o_pair.T.astype(o_ref.dtype)


def _dsa(k, ik, qT, vT, iqT, iwT, rel_bias, B, S):
    T = k.shape[0]
    TQ = min(ATT_Q, S)
    nQ = S // TQ
    topk = min(TOPK_MAX, S // 4)
    buckets = _t5_bucket_table(2 * TQ + 1)
    assert np.all(_t5_bucket_table(S + 1)[TQ + 1:] == REL_BUCKETS - 1)
    j = np.arange(2 * TQ)[:, None]
    r = np.arange(TQ)[None, :]
    dist = np.maximum(r + TQ - j, 0)
    onehot = (jnp.asarray(buckets[dist], jnp.int32)[None]
              == jnp.arange(REL_BUCKETS, dtype=jnp.int32)[:, None, None]).astype(jnp.float32)
    rel = (rel_bias.astype(jnp.float32) - rel_bias[REL_BUCKETS - 1].astype(jnp.float32)[None, :]) * LOG2_E
    enear = jnp.einsum('nh,njr->hjr', rel, onehot, precision=lax.Precision.HIGHEST)

    return pl.pallas_call(
        functools.partial(_dsa_kernel, topk=topk),
        grid=(B, nQ),
        in_specs=[
            pl.BlockSpec((1, A_WIDTH, TQ), lambda b, i: (b, 0, i)),
            pl.BlockSpec((S, A_WIDTH), lambda b, i: (b, 0)),
            pl.BlockSpec((1, A_WIDTH, S), lambda b, i: (b, 0, 0)),
            pl.BlockSpec((1, IDX_WIDTH, TQ), lambda b, i: (b, 0, i)),
            pl.BlockSpec((1, IDX_HEADS, TQ), lambda b, i: (b, 0, i)),
            pl.BlockSpec((S, IDX_DIM), lambda b, i: (b, 0)),
            pl.BlockSpec((A_HEADS, 2 * TQ, TQ), lambda b, i: (0, 0, 0)),
        ],
        out_specs=pl.BlockSpec((TQ, A_WIDTH), lambda b, i: (b * nQ + i, 0)),
        scratch_shapes=[pltpu.VMEM((S, TQ), jnp.float32),
                        pltpu.VMEM((A_HEADS, V7X_LANES, TQ), jnp.bfloat16),
                        pltpu.VMEM((A_HEADS, TQ), jnp.float32),
                        pltpu.VMEM((A_HEADS, V7X_LANES, TQ), jnp.float32)],
        out_shape=jax.ShapeDtypeStruct((T, A_WIDTH), jnp.bfloat16),
        compiler_params=_cparams(("parallel", "arbitrary")),
        name="dsa",
    )(qT, k, vT, iqT, iwT, ik, enear)


def _hgrn_kernel(bq_ref, bf_ref, bi_ref, bg_ref, lb_ref, gain_ref, o_ref,
                 st_ref, b_scr, q_scr, k_scr, v_scr, oi_scr, qd_s, kl_s, vv_s, dec_s, oi_s, upd_s, st_s,
                 kdT_s, vvT_s):
    R = bq_ref.shape[0]
    C = HGRN_CHUNK
    nC = R // C
    f32 = jnp.float32
    bf16 = jnp.bfloat16
    h = pl.program_id(1)

    @pl.when(pl.program_id(2) == 0)
    def _():
        st_ref[...] = jnp.zeros_like(st_ref)

    lb = lb_ref[pl.ds(h, 1), :]
    gain = gain_ref[pl.ds(h, 1), :]
    tril_incl = jnp.where(lax.broadcasted_iota(jnp.int32, (C, C), 1)
                          <= lax.broadcasted_iota(jnp.int32, (C, C), 0), 1.0, 0.0)
    srow = lax.broadcasted_iota(jnp.int32, (C, B_KEY_DIM), 0)

    def gates(r0):
        f = lb + (1.0 - lb) * _sigmoid(bf_ref[pl.ds(r0, C), :])
        qr = bq_ref[pl.ds(r0, C), :]
        return jnp.log(f), 1.0 - f, qr * _sigmoid(qr) * (B_KEY_DIM ** -0.5), bi_ref[pl.ds(r0, C), :]

    def cumdecay(g):
        tri = tril_incl.astype(bf16)
        g_hi = g.astype(bf16)
        rest = g - g_hi.astype(f32)
        g_mid = rest.astype(bf16)
        g_lo = (rest - g_mid.astype(f32)).astype(bf16)
        return (jnp.dot(tri, g_hi, preferred_element_type=f32) + jnp.dot(tri, g_mid, preferred_element_type=f32)
                + jnp.dot(tri, g_lo, preferred_element_type=f32))

    def advance(r0, st, qd, o_intra, upd, decay_row):
        o_inter = lax.dot_general(qd, st.astype(bf16), (((1,), (1,)), ((), ())), preferred_element_type=f32)
        og = bg_ref[pl.ds(r0, C), :]
        y = _rms(o_inter + o_intra, gain) * (og * _sigmoid(og))
        o_ref[pl.ds(r0, C), :] = y.astype(o_ref.dtype)
        return st * decay_row + upd

    f_all = lb + (1.0 - lb) * _sigmoid(bf_ref[...])
    g_all = jnp.log(f_all)
    decay = jnp.sum(g_all.reshape(nC, C, B_KEY_DIM), axis=1)
    safe = jnp.min(decay) >= -HGRN_SAFE_DECAY

    @pl.when(safe)
    def _():
        qr = bq_ref[...]
        qq = qr * _sigmoid(qr) * (B_KEY_DIM ** -0.5)
        kk = 1.0 - f_all
        GR = HGRN_GROUP * C
        r_id = lax.broadcasted_iota(jnp.int32, (GR, GR), 0)
        c_id = lax.broadcasted_iota(jnp.int32, (GR, GR), 1)
        tri_group = jnp.where((r_id // C == c_id // C) & (c_id <= r_id), 1.0, 0.0)
        tri_group_bf = tri_group.astype(bf16)
        g_hi = g_all.astype(bf16)
        rest = g_all - g_hi.astype(f32)
        g_mid = rest.astype(bf16)
        g_lo = (rest - g_mid.astype(f32)).astype(bf16)
        g_cat = jnp.concatenate([g_hi, g_mid, g_lo], axis=1)
        b_parts = []
        for gi in range(R // GR):
            bc = jnp.dot(tri_group_bf, g_cat[gi * GR:(gi + 1) * GR], preferred_element_type=f32)
            b_parts.append(bc[:, :B_KEY_DIM] + bc[:, B_KEY_DIM:2 * B_KEY_DIM] + bc[:, 2 * B_KEY_DIM:])
        b = jnp.concatenate(b_parts, axis=0)
        b_end = jnp.concatenate([jnp.broadcast_to(b[(c + 1) * C - 1:(c + 1) * C], (C, B_KEY_DIM))
                                 for c in range(nC)], axis=0)
        qd_s[...] = (qq * jnp.exp(b)).astype(bf16)
        kd = kk * jnp.exp(-b)
        kl_s[...] = (kk * jnp.exp(b_end - b)).astype(bf16)
        vv = bi_ref[...]
        vv_s[...] = vv.astype(bf16)
        dec_s[...] = jnp.exp(b_end)
        for gi in range(R // GR):
            kdT_s[gi] = kd[gi * GR:(gi + 1) * GR].T.astype(bf16)
        for c in range(nC):
            vvT_s[c] = vv[c * C:(c + 1) * C].T.astype(bf16)
        for gi in range(R // GR):
            rows = slice(gi * GR, (gi + 1) * GR)
            att = jnp.dot(qd_s[rows], kdT_s[gi], preferred_element_type=f32) * tri_group
            oi_s[rows] = jnp.dot(att.astype(bf16), vv_s[rows], preferred_element_type=f32)
        for c in range(nC):
            rows = slice(c * C, (c + 1) * C)
            upd_s[c] = jnp.dot(vvT_s[c], kl_s[rows], preferred_element_type=f32)
        st = st_ref[...]
        for c in range(nC):
            st_s[c] = st.T.astype(bf16)
            st = st * dec_s[c * C:c * C + 1] + upd_s[c]
        st_ref[...] = st
        for c in range(nC):
            rows = slice(c * C, (c + 1) * C)
            oi_s[rows] = oi_s[rows] + jnp.dot(qd_s[rows], st_s[c], preferred_element_type=f32)
        og = bg_ref[...]
        o_ref[...] = (_rms(oi_s[...], gain) * (og * _sigmoid(og))).astype(o_ref.dtype)

    @pl.when(jnp.logical_not(safe))
    def _():
        def body(c, st):
            r0 = pl.multiple_of(c * C, C)
            g, kk, qq, vv = gates(r0)
            b = cumdecay(g)
            b_last = b[C - 1:C, :]
            b_scr[...] = b
            q_scr[...] = qq
            k_scr[...] = kk
            v_scr[...] = vv

            def row(t, _):
                bt = b_scr[pl.ds(t, 1), :]
                qt = q_scr[pl.ds(t, 1), :]
                ex = jnp.where(srow <= t, bt - b_scr[...], -jnp.inf)
                a = jnp.sum(qt * k_scr[...] * jnp.exp(ex), axis=1, keepdims=True)
                oi_scr[pl.ds(t, 1), :] = jnp.sum(a * v_scr[...], axis=0, keepdims=True)
                return 0
            lax.fori_loop(0, C, row, 0)
            kd_last = (kk * jnp.exp(b_last - b)).astype(bf16)
            upd = lax.dot_general(vv.astype(bf16), kd_last, (((0,), (0,)), ((), ())),
                                  preferred_element_type=f32)
            return advance(r0, st, (qq * jnp.exp(b)).astype(bf16), oi_scr[...], upd, jnp.exp(b_last))
        st_ref[...] = lax.fori_loop(0, nC, body, st_ref[...])


def _hgrn(bq, bf, bi, bg, lb, gain, B, S):
    T = bq.shape[0]
    R = min(HGRN_ROWS, S)
    nR = S // R
    C = HGRN_CHUNK
    blk = pl.BlockSpec((R, B_KEY_DIM), lambda b, h, c: (b * nR + c, h))
    small = pl.BlockSpec((B_HEADS, B_KEY_DIM), lambda b, h, c: (0, 0))
    f32 = jnp.float32
    return pl.pallas_call(
        _hgrn_kernel,
        grid=(B, B_HEADS, nR),
        in_specs=[blk, blk, blk, blk, small, small],
        out_specs=blk,
        out_shape=jax.ShapeDtypeStruct((T, B_WIDTH), jnp.bfloat16),
        scratch_shapes=[pltpu.VMEM((B_VAL_DIM, B_KEY_DIM), f32)] +
                       [pltpu.VMEM((C, B_KEY_DIM), f32) for _ in range(5)] +
                       [pltpu.VMEM((R, B_KEY_DIM), jnp.bfloat16) for _ in range(3)] +
                       [pltpu.VMEM((R, B_KEY_DIM), f32) for _ in range(2)] +
                       [pltpu.VMEM((R // C, B_VAL_DIM, B_KEY_DIM), f32),
                        pltpu.VMEM((R // C, B_KEY_DIM, B_VAL_DIM), jnp.bfloat16),
                        pltpu.VMEM((R // (HGRN_GROUP * C), B_KEY_DIM, HGRN_GROUP * C), jnp.bfloat16),
                        pltpu.VMEM((R // C, B_VAL_DIM, C), jnp.bfloat16)],
        compiler_params=_cparams(("parallel", "parallel", "arbitrary")),
        name="hgrn",
    )(bq, bf, bi, bg, lb, gain)


def _merge_kernel(x_ref, ya_ref, yb_ref, ga_ref, gb_ref, wa_ref, wb_ref, wo_ref, g_ref, wrh_ref, wrl_ref, br_ref,
                  x1_ref, xn_ref, lg_ref):
    f32 = jnp.float32
    ma = jnp.dot(ya_ref[...], wa_ref[...], preferred_element_type=f32)
    mb = jnp.dot(yb_ref[...], wb_ref[...], preferred_element_type=f32)
    merged = ga_ref[...].astype(f32) * ma + gb_ref[...].astype(f32) * mb
    x1 = x_ref[...] + jnp.dot(merged.astype(jnp.bfloat16), wo_ref[...], preferred_element_type=f32)
    x1_ref[...] = x1
    hn = _rms(x1, g_ref[...])
    xn_ref[...] = _pack_bf16_pairs(hn)
    hn_hi = hn.astype(jnp.bfloat16)
    hn_lo = (hn - hn_hi.astype(f32)).astype(jnp.bfloat16)
    lg_ref[...] = (jnp.dot(hn_hi, wrh_ref[...], preferred_element_type=f32)
                   + jnp.dot(hn_lo, wrh_ref[...], preferred_element_type=f32)
                   + jnp.dot(hn_hi, wrl_ref[...], preferred_element_type=f32) + br_ref[...])


def _merge(x2, ya, yb, ga, gb, w_up_a, w_up_b, w_out, gain, w_router, b_router):
    T, D = x2.shape
    R = min(PROJ_ROWS, T)
    bf = jnp.bfloat16
    wr_hi = w_router.astype(bf)
    wr_lo = (w_router - wr_hi.astype(jnp.float32)).astype(bf)
    ins = [x2, ya, yb, ga, gb, w_up_a.astype(bf), w_up_b.astype(bf), w_out.astype(bf),
           gain.reshape(1, D), wr_hi, wr_lo, b_router.reshape(1, N_EXPERTS)]
    row = lambda n: pl.BlockSpec((R, n), lambda i: (i, 0))
    full = lambda a: pl.BlockSpec(a.shape, lambda i: (0,) * a.ndim)
    in_specs = [row(D), row(A_WIDTH), row(B_WIDTH), row(D), row(D)] + [full(a) for a in ins[5:]]
    return pl.pallas_call(
        _merge_kernel,
        grid=(T // R,),
        in_specs=in_specs,
        out_specs=[row(D), row(D // 2), row(N_EXPERTS)],
        out_shape=[jax.ShapeDtypeStruct((T, D), jnp.float32), jax.ShapeDtypeStruct((T, D // 2), jnp.uint32),
                   jax.ShapeDtypeStruct((T, N_EXPERTS), jnp.float32)],
        compiler_params=_cparams(("parallel",)),
        name="merge",
    )(*ins)


def _route_kernel(lg_ref, eidx_ref, gate_ref, rank_ref, cnt_ref, run_ref):
    R = lg_ref.shape[0]
    f32 = jnp.float32

    @pl.when(pl.program_id(0) == 0)
    def _():
        run_ref[...] = jnp.zeros_like(run_ref)

    lg = lg_ref[...].T
    expert = lax.broadcasted_iota(jnp.int32, (N_EXPERTS, R), 0)
    work = lg
    onehots, vals, idxs = [], [], []
    for _ in range(TOP_K):
        m = jnp.max(work, axis=0, keepdims=True)
        idx = jnp.min(jnp.where(work == m, expert, N_EXPERTS), axis=0, keepdims=True)
        oh = expert == idx
        onehots.append(oh)
        vals.append(m)
        idxs.append(idx)
        work = jnp.where(oh, -jnp.inf, work)
    ex = [jnp.exp(v - vals[0]) for v in vals]
    den = ex[0] + ex[1] + ex[2] + ex[3]
    chosen = jnp.where(onehots[0] | onehots[1] | onehots[2] | onehots[3], 1.0, 0.0)
    earlier = jnp.where(lax.broadcasted_iota(jnp.int32, (R, R), 0)
                        < lax.broadcasted_iota(jnp.int32, (R, R), 1), 1.0, 0.0).astype(jnp.bfloat16)
    before = jnp.dot(chosen.astype(jnp.bfloat16), earlier, preferred_element_type=f32) + run_ref[...]
    slot = lax.broadcasted_iota(jnp.int32, (TOP_K, R), 0)
    eidx = jnp.zeros((TOP_K, R), jnp.int32)
    gate = jnp.zeros((TOP_K, R), f32)
    rank = jnp.zeros((TOP_K, R), f32)
    for k in range(TOP_K):
        eidx = jnp.where(slot == k, idxs[k], eidx)
        gate = jnp.where(slot == k, ex[k] / den, gate)
        rk = jnp.sum(jnp.where(onehots[k], before, 0.0), axis=0, keepdims=True)
        rank = jnp.where(slot == k, rk, rank)
    eidx_ref[...] = eidx
    gate_ref[...] = gate
    rank_ref[...] = rank.astype(jnp.int32)
    run_ref[...] = run_ref[...] + jnp.sum(chosen, axis=1, keepdims=True)
    cnt_ref[...] = run_ref[...].astype(jnp.int32)


def _route(logits):
    T = logits.shape[0]
    R = min(ROUTE_ROWS, T)
    col = pl.BlockSpec((TOP_K, R), lambda i: (0, i))
    return pl.pallas_call(
        _route_kernel,
        grid=(T // R,),
        in_specs=[pl.BlockSpec((R, N_EXPERTS), lambda i: (i, 0))],
        out_specs=[col, col, col, pl.BlockSpec((N_EXPERTS, 1), lambda i: (0, 0))],
        out_shape=[jax.ShapeDtypeStruct((TOP_K, T), jnp.int32), jax.ShapeDtypeStruct((TOP_K, T), jnp.float32),
                   jax.ShapeDtypeStruct((TOP_K, T), jnp.int32), jax.ShapeDtypeStruct((N_EXPERTS, 1), jnp.int32)],
        scratch_shapes=[pltpu.VMEM((N_EXPERTS, 1), jnp.float32)],
        compiler_params=_cparams(("arbitrary",)),
        name="route",
    )(logits)


def _sc_gather_rows(table, idx):
    W = table.shape[1]
    M = idx.shape[0]
    workers = V7X_SC_CORES * V7X_SC_SUBCORES
    per_worker = M // workers
    pieces = per_worker // SC_GATHER_ROWS
    assert per_worker * workers == M and pieces * SC_GATHER_ROWS == per_worker and pieces % 2 == 0
    mesh = plsc.VectorSubcoreMesh(core_axis_name="core", subcore_axis_name="subcore",
                                  num_cores=V7X_SC_CORES, num_subcores=V7X_SC_SUBCORES)

    @functools.partial(
        pl.kernel, mesh=mesh,
        out_type=jax.ShapeDtypeStruct((M, W), table.dtype),
        scratch_types=[pltpu.VMEM((per_worker,), jnp.int32),
                       pltpu.VMEM((SC_GATHER_ROWS, W), table.dtype),
                       pltpu.VMEM((SC_GATHER_ROWS, W), table.dtype),
                       pltpu.SemaphoreType.DMA, pltpu.SemaphoreType.DMA],
    )
    def gather(table_hbm, idx_hbm, out_hbm, idx_v, rows_a, rows_b, sem_a, sem_b):
        worker = lax.axis_index("subcore") * V7X_SC_CORES + lax.axis_index("core")
        base = pl.multiple_of(worker * per_worker, SC_GATHER_ROWS)
        pltpu.sync_copy(idx_hbm.at[pl.ds(base, per_worker)], idx_v)
        bufs = ((rows_a, sem_a), (rows_b, sem_b))

        def fetch(g, buf, sem):
            off = pl.multiple_of(g * SC_GATHER_ROWS, SC_GATHER_ROWS)
            return pltpu.make_async_copy(table_hbm.at[idx_v.at[pl.ds(off, SC_GATHER_ROWS)]], buf, sem)

        fetch(0, *bufs[0]).start()

        @pl.loop(0, pieces, step=2)
        def _(g0):
            for half in range(2):
                g = g0 + half
                buf, sem = bufs[half]
                fetch(g, buf, sem).wait()

                @pl.when(g + 1 < pieces)
                def _():
                    fetch(g + 1, *bufs[1 - half]).start()

                off = pl.multiple_of(g * SC_GATHER_ROWS, SC_GATHER_ROWS)
                pltpu.sync_copy(buf, out_hbm.at[pl.ds(base + off, SC_GATHER_ROWS)])

    return gather(table, idx)


def _sc_scatter_rows(rows, dest, n_out):
    T, W = rows.shape
    slots = dest.shape[0]
    workers = V7X_SC_CORES * V7X_SC_SUBCORES
    per_worker = T // workers
    pieces = per_worker // SC_GATHER_ROWS
    assert per_worker * workers == T and pieces * SC_GATHER_ROWS == per_worker and pieces % 2 == 0
    idx = dest.reshape(slots, workers, pieces, SC_GATHER_ROWS).transpose(1, 2, 0, 3)
    mesh = plsc.VectorSubcoreMesh(core_axis_name="core", subcore_axis_name="subcore",
                                  num_cores=V7X_SC_CORES, num_subcores=V7X_SC_SUBCORES)

    @functools.partial(
        pl.kernel, mesh=mesh,
        out_type=jax.ShapeDtypeStruct((n_out, W), rows.dtype),
        scratch_types=[pltpu.VMEM((pieces, slots, SC_GATHER_ROWS), jnp.int32),
                       pltpu.VMEM((SC_GATHER_ROWS, W), rows.dtype),
                       pltpu.VMEM((SC_GATHER_ROWS, W), rows.dtype),
                       pltpu.SemaphoreType.DMA, pltpu.SemaphoreType.DMA, pltpu.SemaphoreType.DMA],
    )
    def scatter(rows_hbm, idx_hbm, out_hbm, idx_v, rows_a, rows_b, sem_a, sem_b, sem_out):
        worker = lax.axis_index("subcore") * V7X_SC_CORES + lax.axis_index("core")
        base = pl.multiple_of(worker * per_worker, SC_GATHER_ROWS)
        pltpu.sync_copy(idx_hbm.at[worker], idx_v)
        bufs = ((rows_a, sem_a), (rows_b, sem_b))

        def fetch(g, buf, sem):
            off = pl.multiple_of(g * SC_GATHER_ROWS, SC_GATHER_ROWS)
            return pltpu.make_async_copy(rows_hbm.at[pl.ds(base + off, SC_GATHER_ROWS)], buf, sem)

        fetch(0, *bufs[0]).start()

        @pl.loop(0, pieces, step=2)
        def _(g0):
            for half in range(2):
                g = g0 + half
                buf, sem = bufs[half]
                fetch(g, buf, sem).wait()

                @pl.when(g + 1 < pieces)
                def _():
                    fetch(g + 1, *bufs[1 - half]).start()

                puts = [pltpu.make_async_copy(buf, out_hbm.at[idx_v.at[g, k]], sem_out) for k in range(slots)]
                for put in puts:
                    put.start()
                for put in puts:
                    put.wait()

    return scatter(rows, idx)


def _experts_kernel(be_ref, nb_ref, first_ref, slot_ref, next_ref,
                    x_ref, wgu_hbm, bgu_ref, wd_hbm, bd_ref, o_ref,
                    wgu_f32, wd_f32, wgu_bf, wd_bf, sems):
    f32 = jnp.float32
    d_ff = wd_bf.shape[0]
    i = pl.program_id(0)
    live = i < nb_ref[0]

    def weight_copies(e, s):
        return (pltpu.make_async_copy(wgu_hbm.at[e], wgu_f32.at[s], sems.at[s, 0]),
                pltpu.make_async_copy(wd_hbm.at[e], wd_f32.at[s], sems.at[s, 1]))

    @pl.when(live & (first_ref[i] == 1))
    def _():
        e = be_ref[i]
        s = slot_ref[i]

        @pl.when(i == 0)
        def _():
            for cp in weight_copies(e, s):
                cp.start()

        for cp in weight_copies(e, s):
            cp.wait()

        @pl.when(next_ref[i] >= 0)
        def _():
            for cp in weight_copies(next_ref[i], 1 - s):
                cp.start()

        wgu_bf[...] = wgu_f32[s].astype(jnp.bfloat16)
        wd_bf[...] = wd_f32[s].astype(jnp.bfloat16)

    @pl.when(live)
    def _():
        x_hi, x_lo = _unpack_bf16_pairs(x_ref[...])
        half = x_hi.shape[1]
        gu = (jnp.dot(x_hi, wgu_bf[:half, :], preferred_element_type=f32)
              + jnp.dot(x_lo, wgu_bf[half:, :], preferred_element_type=f32) + bgu_ref[0])
        gate = jnp.minimum(gu[:, :d_ff], SWIGLU_LIMIT)
        lin = jnp.clip(gu[:, d_ff:], -SWIGLU_LIMIT, SWIGLU_LIMIT)
        act = (lin + 1.0) * gate * _sigmoid(SWIGLU_ALPHA * gate)
        y = jnp.dot(act.astype(jnp.bfloat16), wd_bf[...], preferred_element_type=f32) + bd_ref[0]
        o_ref[...] = _pack_bf16_pairs(y)

    @pl.when(pl.program_id(0) >= nb_ref[0])
    def _():
        o_ref[...] = jnp.zeros_like(o_ref)


def _experts(xs, plan, w_gu, b_gu, w_down, b_down):
    P, W = xs.shape
    E, D, F2 = w_gu.shape
    nb = P // EXPERT_ROWS
    by_expert = lambda i, be, *_: (be[i], 0, 0)
    grid_spec = pltpu.PrefetchScalarGridSpec(
        num_scalar_prefetch=5,
        grid=(nb,),
        in_specs=[
            pl.BlockSpec((EXPERT_ROWS, W), lambda i, *_: (i, 0)),
            pl.BlockSpec(memory_space=pl.ANY),
            pl.BlockSpec((1, 1, F2), by_expert),
            pl.BlockSpec(memory_space=pl.ANY),
            pl.BlockSpec((1, 1, D), by_expert),
        ],
        out_specs=pl.BlockSpec((EXPERT_ROWS, W), lambda i, *_: (i, 0)),
        scratch_shapes=[pltpu.VMEM((2, D, F2), jnp.float32), pltpu.VMEM((2, F2 // 2, D), jnp.float32),
                        pltpu.VMEM((D, F2), jnp.bfloat16), pltpu.VMEM((F2 // 2, D), jnp.bfloat16),
                        pltpu.SemaphoreType.DMA((2, 2))],
    )
    return pl.pallas_call(
        _experts_kernel,
        grid_spec=grid_spec,
        out_shape=jax.ShapeDtypeStruct((P, W), jnp.uint32),
        compiler_params=_cparams(("arbitrary",)),
        name="experts",
    )(*plan, xs, w_gu, b_gu.reshape(E, 1, F2), w_down, b_down.reshape(E, 1, D))


def _combine_kernel(ya_ref, x1_ref, gate_ref, g_ref, o_ref):
    half = x1_ref.shape[1] // 2
    f32 = jnp.float32
    gate = gate_ref[...].T
    x1 = x1_ref[...]
    y_hi = x1[:, :half]
    y_lo = x1[:, half:]
    for k in range(TOP_K):
        hi, lo = _unpack_bf16_pairs(ya_ref[k])
        y_hi = y_hi + gate[:, k:k + 1] * hi.astype(f32)
        y_lo = y_lo + gate[:, k:k + 1] * lo.astype(f32)
    o_ref[...] = _rms(jnp.concatenate([y_hi, y_lo], axis=1), g_ref[...])


def _combine(ya, x1, gates, gain):
    T, D = x1.shape
    R = min(COMBINE_ROWS, T)
    row = lambda w: pl.BlockSpec((R, w), lambda i: (i, 0))
    return pl.pallas_call(
        _combine_kernel,
        grid=(T // R,),
        in_specs=[pl.BlockSpec((TOP_K, R, D // 2), lambda i: (0, i, 0)), row(D),
                  pl.BlockSpec((TOP_K, R), lambda i: (0, i)), pl.BlockSpec((1, D), lambda i: (0, 0))],
        out_specs=row(D),
        out_shape=jax.ShapeDtypeStruct((T, D), jnp.float32),
        compiler_params=_cparams(("parallel",)),
        name="combine",
    )(ya, x1, gates, gain.reshape(1, D))


def _moe_plan(eidx, rank, counts, A):
    counts = counts.reshape(N_EXPERTS)
    padded = (counts + EXPERT_ROWS - 1) // EXPERT_ROWS * EXPERT_ROWS
    pad_ends = jnp.cumsum(padded)
    pad_starts = pad_ends - padded
    n_blocks = -(-A // EXPERT_ROWS) + N_EXPERTS
    ids = jnp.arange(N_EXPERTS, dtype=jnp.int32)
    dest = rank + jnp.sum(jnp.where(eidx[None] == ids[:, None, None], pad_starts[:, None, None], 0), axis=0)
    block_start = jnp.arange(n_blocks, dtype=pad_ends.dtype) * EXPERT_ROWS
    block_expert = jnp.minimum(jnp.sum(pad_ends[None, :] <= block_start[:, None], axis=1),
                               N_EXPERTS - 1).astype(jnp.int32)
    n_used = (pad_ends[-1] // EXPERT_ROWS).astype(jnp.int32).reshape(1)
    has_rows = counts > 0
    ordinal = jnp.cumsum(has_rows.astype(jnp.int32)) - 1
    later = has_rows[None, :] & (ids[None, :] > ids[:, None])
    next_expert = jnp.where(jnp.any(later, axis=1), jnp.argmax(later, axis=1), -1).astype(jnp.int32)
    tables = jnp.stack([pad_starts, ordinal % 2, next_expert], axis=1).astype(jnp.float32)
    onehot = (block_expert[:, None] == ids[None, :]).astype(jnp.float32)
    looked = jnp.dot(onehot, tables, precision=lax.Precision.HIGHEST).astype(jnp.int32)
    block_first = ((block_start == looked[:, 0]) & (block_start < pad_ends[-1])).astype(jnp.int32)
    plan = (block_expert, n_used, block_first, looked[:, 1], looked[:, 2])
    return dest.astype(jnp.int32), plan, n_blocks


def kernel(x, w_in, w_up_a, w_up_b, w_out, norm_mix, norm_ffn, norm_final, hgrn_norm,
           lb_logits, rel_bias, w_router, b_router, w_gu, b_gu, w_down, b_down):
    B, S, D = x.shape
    T = B * S
    assert w_in.shape[0] == 1, "the final rmsnorm is fused into the single layer's combine stage"
    lb_all = jnp.cumsum(jax.nn.softmax(lb_logits.astype(jnp.float32), axis=0), axis=0)
    x2 = x.reshape(T, D)
    (k, ik, qT, vT, iqT, iwT, bq, bf, bi, bg, ga, gb) = _inproj(x2, norm_mix[0], w_in[0], B, S)
    ya = _dsa(k, ik, qT, vT, iqT, iwT, rel_bias, B, S)
    yb = _hgrn(bq, bf, bi, bg, lb_all[0].reshape(B_HEADS, B_KEY_DIM), hgrn_norm[0], B, S)
    x1, xn, logits = _merge(x2, ya, yb, ga, gb, w_up_a[0], w_up_b[0], w_out[0], norm_ffn[0],
                            w_router[0], b_router[0])
    eidx, gates, rank, counts = _route(logits)
    dest, plan, n_blocks = _moe_plan(eidx, rank, counts, T * TOP_K)
    P = n_blocks * EXPERT_ROWS
    A = T * TOP_K
    xs = _sc_scatter_rows(xn, dest, P)
    y_buf = _experts(xs, plan, w_gu[0], b_gu[0], w_down[0], b_down[0])
    ya = _sc_gather_rows(y_buf, dest.reshape(A)).reshape(TOP_K, T, D // 2)
    out = _combine(ya, x1, gates, norm_final)
    return out.reshape(B, S, D)
```

```python
import functools
import math

import numpy as np
import jax
import jax.numpy as jnp
from jax import lax
from jax.experimental import pallas as pl
from jax.experimental.pallas import tpu as pltpu
from jax.experimental.pallas import tpu_sc as plsc

A_HEADS = 8
A_HEAD_DIM = 64
IDX_HEADS = 8
IDX_DIM = 32
TOPK_MAX = 256
REL_BUCKETS = 32
REL_MAX_DIST = 128
B_HEADS = 4
B_KEY_DIM = 128
B_VAL_DIM = 128
N_EXPERTS = 32
TOP_K = 4
SWIGLU_LIMIT = 7.0
SWIGLU_ALPHA = 1.702
EPS = 1e-6
LOG2_E = math.log2(math.e)

A_WIDTH = A_HEADS * A_HEAD_DIM
B_WIDTH = B_HEADS * B_VAL_DIM
IDX_WIDTH = IDX_HEADS * IDX_DIM

V7X_LANES = 128
V7X_SUBLANES = 8
V7X_VMEM_LIMIT_BYTES = 56 * 1024 * 1024
V7X_SC_CORES = 2
V7X_SC_SUBCORES = 16

PROJ_ROWS = 512
ATT_Q = 256
ATT_KC = 128
SCORE_UNROLL = 4
COUNT_UNROLL = 2
FAR_UNROLL = 4
HGRN_ROWS = 1024
HGRN_CHUNK = 64
HGRN_GROUP = 4
HGRN_SAFE_DECAY = 70.0
ROUTE_ROWS = 512
EXPERT_ROWS = 512
COMBINE_ROWS = 512
SC_GATHER_ROWS = 32
MASK_NEG = -1e30
BISECT_FAST_ITERS = 26
BISECT_BLIND_ITERS = 16


def _cparams(dims):
    return pltpu.CompilerParams(dimension_semantics=dims, vmem_limit_bytes=V7X_VMEM_LIMIT_BYTES)


def _rms(x, gain):
    return x * lax.rsqrt(jnp.mean(x * x, axis=-1, keepdims=True) + EPS) * gain


def _sigmoid(x):
    return 1.0 / (1.0 + jnp.exp(-x))


def _pack_bf16_pairs(x):
    n = x.shape[1] // 2
    as_bits = lambda v: lax.bitcast_convert_type(v.astype(jnp.bfloat16).astype(jnp.float32), jnp.uint32)
    return (as_bits(x[:, :n]) & jnp.uint32(0xFFFF0000)) | (as_bits(x[:, n:]) >> 16)


def _unpack_bf16_pairs(w):
    hi = lax.bitcast_convert_type(w & jnp.uint32(0xFFFF0000), jnp.float32).astype(jnp.bfloat16)
    lo = lax.bitcast_convert_type(w << 16, jnp.float32).astype(jnp.bfloat16)
    return hi, lo


def _blocked_loop(n, step, carry, unroll):
    def run(first, count, cr):
        for t in range(count):
            cr = step(first + t, cr)
        return cr

    carry = lax.fori_loop(0, n // unroll, lambda j, cr: run(j * unroll, unroll, cr), carry)
    base = (n // unroll) * unroll
    piece = unroll // 2
    while piece >= 1:
        take = (n & piece) != 0
        carry = lax.cond(take, functools.partial(run, base, piece), lambda cr: cr, carry)
        base = base + jnp.where(take, piece, 0)
        piece //= 2
    return carry


def _fold_rows(x, op):
    return op(x.reshape(x.shape[0] // V7X_SUBLANES, V7X_SUBLANES, x.shape[1]), axis=0)


def _inproj_kernel(x_ref, g_ref, wk_ref, wik_ref, wqT_ref, wvT_ref, wiqT_ref, wiwT_ref, wb_ref, wg_ref,
                   k_ref, ik_ref, qT_ref, vT_ref, iqT_ref, iwT_ref, bq_ref, bf_ref, bi_ref, bg_ref,
                   ga_ref, gb_ref):
    x = x_ref[...]
    hn = _rms(x, g_ref[...]).astype(jnp.bfloat16)

    def mm(w_ref):
        return jnp.dot(hn, w_ref[...], preferred_element_type=jnp.float32)

    def mm_t(w_ref):
        return lax.dot_general(w_ref[...], hn, (((1,), (1,)), ((), ())),
                               preferred_element_type=jnp.float32)

    k_ref[...] = mm(wk_ref).astype(jnp.bfloat16)
    ik_ref[...] = mm(wik_ref).astype(jnp.bfloat16)
    qT_ref[0] = (mm_t(wqT_ref) * (A_HEAD_DIM ** -0.5 * LOG2_E)).astype(jnp.bfloat16)
    vT_ref[0] = mm_t(wvT_ref).astype(jnp.bfloat16)
    iqT_ref[0] = mm_t(wiqT_ref).astype(jnp.bfloat16)
    iwT_ref[0] = mm_t(wiwT_ref) * ((IDX_HEADS * IDX_DIM) ** -0.5)
    hb = mm(wb_ref)
    bq_ref[...] = hb[:, 0 * B_WIDTH:1 * B_WIDTH]
    bf_ref[...] = hb[:, 1 * B_WIDTH:2 * B_WIDTH]
    bi_ref[...] = hb[:, 2 * B_WIDTH:3 * B_WIDTH]
    bg_ref[...] = hb[:, 3 * B_WIDTH:4 * B_WIDTH]
    d = ga_ref.shape[-1]
    hg = mm(wg_ref)
    ga_ref[...] = _sigmoid(hg[:, :d]).astype(jnp.bfloat16)
    gb_ref[...] = _sigmoid(hg[:, d:]).astype(jnp.bfloat16)


def _inproj(x2, gain, w_in, B, S):
    T, D = x2.shape
    R = min(PROJ_ROWS, S)
    nS = S // R
    o = np.cumsum((0, A_WIDTH, A_WIDTH, A_WIDTH, IDX_WIDTH, IDX_HEADS, IDX_DIM,
                   B_WIDTH, B_WIDTH, B_WIDTH, B_WIDTH, D, D))
    bf = jnp.bfloat16
    wqT = w_in[:, o[0]:o[1]].T.astype(bf)
    wk = w_in[:, o[1]:o[2]].astype(bf)
    wvT = w_in[:, o[2]:o[3]].T.astype(bf)
    wiqT = w_in[:, o[3]:o[4]].T.astype(bf)
    wiwT = w_in[:, o[4]:o[5]].T.astype(bf)
    wik = w_in[:, o[5]:o[6]].astype(bf)
    wb = w_in[:, o[6]:o[10]].astype(bf)
    wg = w_in[:, o[10]:o[12]].astype(bf)

    def full(a):
        return pl.BlockSpec(a.shape, lambda b, i: (0,) * a.ndim)

    row = lambda n: pl.BlockSpec((R, n), lambda b, i: (b * nS + i, 0))
    colT = lambda n: pl.BlockSpec((1, n, R), lambda b, i: (b, 0, i))
    f32 = jnp.float32
    outs = [
        (jax.ShapeDtypeStruct((T, A_WIDTH), bf), row(A_WIDTH)),
        (jax.ShapeDtypeStruct((T, IDX_DIM), bf), row(IDX_DIM)),
        (jax.ShapeDtypeStruct((B, A_WIDTH, S), bf), colT(A_WIDTH)),
        (jax.ShapeDtypeStruct((B, A_WIDTH, S), bf), colT(A_WIDTH)),
        (jax.ShapeDtypeStruct((B, IDX_WIDTH, S), bf), colT(IDX_WIDTH)),
        (jax.ShapeDtypeStruct((B, IDX_HEADS, S), f32), colT(IDX_HEADS)),
        (jax.ShapeDtypeStruct((T, B_WIDTH), f32), row(B_WIDTH)),
        (jax.ShapeDtypeStruct((T, B_WIDTH), f32), row(B_WIDTH)),
        (jax.ShapeDtypeStruct((T, B_WIDTH), f32), row(B_WIDTH)),
        (jax.ShapeDtypeStruct((T, B_WIDTH), f32), row(B_WIDTH)),
        (jax.ShapeDtypeStruct((T, D), bf), row(D)),
        (jax.ShapeDtypeStruct((T, D), bf), row(D)),
    ]
    ins = [x2, gain.reshape(1, D), wk, wik, wqT, wvT, wiqT, wiwT, wb, wg]
    in_specs = [row(D)] + [full(a) for a in ins[1:]]
    return pl.pallas_call(
        _inproj_kernel,
        grid=(B, nS),
        in_specs=in_specs,
        out_specs=[s for _, s in outs],
        out_shape=[s for s, _ in outs],
        compiler_params=_cparams(("parallel", "parallel")),
        name="inproj",
    )(*ins)


def _t5_bucket_table(n):
    d = np.arange(n)
    max_exact = REL_BUCKETS // 2
    nf = np.maximum(d, 1).astype(np.float64)
    large = max_exact + (np.log(nf / max_exact) / math.log(REL_MAX_DIST / max_exact)
                         * (REL_BUCKETS - max_exact)).astype(np.int32)
    large = np.minimum(large, REL_BUCKETS - 1)
    return np.where(d < max_exact, d, large)


def _dsa_kernel(qT_ref, k_ref, vT_ref, iqT_ref, iwT_ref, ik_ref, enear_ref,
                o_ref, sc_ref, qh_scr, m_scr, acc_scr, *, topk):
    TQ = qT_ref.shape[2]
    KC = TQ
    i = pl.program_id(1)
    nch = i + 1
    q0 = i * TQ
    f32 = jnp.float32
    bf16 = jnp.bfloat16
    key_id = lax.broadcasted_iota(jnp.int32, (KC, TQ), 0)
    qry_id = lax.broadcasted_iota(jnp.int32, (KC, TQ), 1)

    def col_reduce(x, op):
        return op(_fold_rows(x, op), axis=0, keepdims=True)

    iw = iwT_ref[0]

    def score_chunk(c, carry):
        rmin, rmax = carry
        k0 = pl.multiple_of(c * KC, KC)
        ik = ik_ref[pl.ds(k0, KC), :]
        acc = jnp.zeros((KC, TQ), f32)
        for h in range(IDX_HEADS):
            sh = jnp.dot(ik, iqT_ref[0, h * IDX_DIM:(h + 1) * IDX_DIM, :], preferred_element_type=f32)
            acc = acc + jnp.maximum(sh, 0.0) * iw[h:h + 1, :]
        valid = (k0 + key_id) <= (q0 + qry_id)
        sc_ref[pl.ds(k0, KC), :] = jnp.where(valid, acc, MASK_NEG)
        rmin = jnp.minimum(rmin, _fold_rows(jnp.where(valid, acc, -MASK_NEG), jnp.min))
        rmax = jnp.maximum(rmax, _fold_rows(jnp.where(valid, acc, MASK_NEG), jnp.max))
        return rmin, rmax

    rmin8, rmax8 = _blocked_loop(
        nch, score_chunk,
        (jnp.full((V7X_SUBLANES, TQ), -MASK_NEG, f32), jnp.full((V7X_SUBLANES, TQ), MASK_NEG, f32)),
        SCORE_UNROLL)
    rmin = jnp.min(rmin8, axis=0, keepdims=True)
    rmax = jnp.max(rmax8, axis=0, keepdims=True)

    def count_where(pred_fn):
        def one(c, acc):
            k0 = pl.multiple_of(c * KC, KC)
            blk = sc_ref[pl.ds(k0, KC), :]
            return acc + _fold_rows(jnp.where(pred_fn(blk), 1.0, 0.0), jnp.sum)
        acc = _blocked_loop(nch, one, jnp.zeros((V7X_SUBLANES, TQ), f32), COUNT_UNROLL)
        return jnp.sum(acc, axis=0, keepdims=True)

    def band_min_max(lo, hi):
        def body(c, carry):
            bmin, bmax = carry
            k0 = pl.multiple_of(c * KC, KC)
            blk = sc_ref[pl.ds(k0, KC), :]
            bmin = jnp.minimum(bmin, _fold_rows(jnp.where(blk >= lo, blk, -MASK_NEG), jnp.min))
            bmax = jnp.maximum(bmax, _fold_rows(jnp.where(blk < hi, blk, MASK_NEG), jnp.max))
            return bmin, bmax
        bmin8, bmax8 = lax.fori_loop(
            0, nch, body,
            (jnp.full((V7X_SUBLANES, TQ), -MASK_NEG, f32), jnp.full((V7X_SUBLANES, TQ), MASK_NEG, f32)))
        return jnp.min(bmin8, axis=0, keepdims=True), jnp.max(bmax8, axis=0, keepdims=True)

    kf = float(topk)
    n_valid = (q0 + 1 + lax.broadcasted_iota(jnp.int32, (1, TQ), 1)).astype(f32)
    lo0 = rmin
    cnt0 = n_valid
    hi0 = rmax + jnp.maximum(jnp.abs(rmax) * 2.0 ** -20, 1e-30)
    done0 = jnp.where(cnt0 <= kf, 1.0, 0.0)

    def probe(st, lo_s, mid, tie):
        it, lo, hi, cnt, done = st
        active = done < 0.5
        lo_s = jnp.where(active, lo_s, lo)
        c = count_where(lambda blk: blk >= mid)
        feas = c >= kf
        move = active & jnp.logical_not(tie)
        lo_n = jnp.where(move & feas, mid, lo_s)
        cnt_n = jnp.where(move & feas, c, cnt)
        hi_n = jnp.where(move & jnp.logical_not(feas), mid, hi)
        done_n = jnp.where((active & tie) | (cnt_n <= kf), 1.0, done)
        return it + 1, lo_n, hi_n, cnt_n, done_n

    def halve(st):
        _, lo, hi, _, _ = st
        half = lo + 0.5 * (hi - lo)
        stuck = (half <= lo) | (half >= hi)
        return probe(st, lo, half, stuck)

    def snap(st):
        _, lo, hi, _, _ = st
        bmin, bmax = band_min_max(lo, hi)
        mid = bmin + 0.5 * (bmax - bmin)
        return probe(st, bmin, jnp.where(mid <= bmin, bmax, mid), bmax <= bmin)

    st = lax.fori_loop(0, BISECT_BLIND_ITERS // 2, lambda _, st: halve(halve(st)),
                       (jnp.int32(0), lo0, hi0, cnt0, done0))
    st = lax.while_loop(lambda st: (jnp.min(st[-1]) < 0.5) & (st[0] < BISECT_FAST_ITERS),
                        lambda st: halve(halve(st)), st)
    _, thr, _, cnt_thr, _ = lax.while_loop(lambda st: jnp.min(st[-1]) < 0.5, snap, st)

    tie_overflow = jnp.max(cnt_thr) > kf

    @pl.when(jnp.logical_not(tie_overflow))
    def _():
        def mask_chunk(c, _):
            k0 = pl.multiple_of(c * KC, KC)
            sc_ref[pl.ds(k0, KC), :] = jnp.where(sc_ref[pl.ds(k0, KC), :] >= thr, 0.0, MASK_NEG)
            return 0
        lax.fori_loop(0, nch, mask_chunk, 0)

    @pl.when(tie_overflow)
    def _():
        need = kf - count_where(lambda blk: blk > thr)
        tril = jnp.where(lax.broadcasted_iota(jnp.int32, (KC, KC), 1)
                         <= lax.broadcasted_iota(jnp.int32, (KC, KC), 0), 1.0, 0.0).astype(bf16)

        def mask_chunk(c, run):
            k0 = pl.multiple_of(c * KC, KC)
            blk = sc_ref[pl.ds(k0, KC), :]
            eq = jnp.where(blk == thr, 1.0, 0.0)
            pref = jnp.dot(tril, eq.astype(bf16), preferred_element_type=f32)
            sel = (blk > thr) | ((eq > 0.5) & (run + pref <= need))
            sc_ref[pl.ds(k0, KC), :] = jnp.where(sel, 0.0, MASK_NEG)
            return run + pref[KC - 1:KC, :]

        lax.fori_loop(0, nch, mask_chunk, jnp.zeros((1, TQ), f32))

    AK = min(ATT_KC, TQ)
    per = TQ // AK
    head0_q = (lax.broadcasted_iota(jnp.int32, (V7X_LANES, TQ), 0) // A_HEAD_DIM) == 0
    n_pairs = A_HEADS // 2

    m_scr[...] = jnp.full(m_scr.shape, MASK_NEG, f32)
    acc_scr[...] = jnp.zeros(acc_scr.shape, f32)
    v_row = lax.broadcasted_iota(jnp.int32, (V7X_LANES, AK), 0)
    denom_row = [A_HEAD_DIM * (1 - sub) for sub in range(2)]
    for p in range(n_pairs):
        q_pair = qT_ref[0, p * V7X_LANES:(p + 1) * V7X_LANES, :]
        zq = jnp.zeros_like(q_pair)
        qh_scr[2 * p] = jnp.where(head0_q, q_pair, zq)
        qh_scr[2 * p + 1] = jnp.where(head0_q, zq, q_pair)

    def step(c, bias_rows):
        k0 = pl.multiple_of(c * AK, AK)
        msk = sc_ref[pl.ds(k0, AK), :]
        for p in range(n_pairs):
            kp = k_ref[pl.ds(k0, AK), p * V7X_LANES:(p + 1) * V7X_LANES]
            vp = vT_ref[0, p * V7X_LANES:(p + 1) * V7X_LANES, pl.ds(k0, AK)]
            for sub in range(2):
                h = 2 * p + sub
                s = jnp.dot(kp, qh_scr[h], preferred_element_type=f32) + msk
                if bias_rows is not None:
                    s = s + enear_ref[h, bias_rows, :]
                m = m_scr[h:h + 1, :]
                m_new = jnp.maximum(m, col_reduce(s, jnp.max))
                alpha = jnp.exp2(m - m_new)
                pr = jnp.exp2(s - m_new)
                m_scr[h:h + 1, :] = m_new
                v_aug = jnp.where(v_row == denom_row[sub], jnp.ones_like(vp), vp)
                acc_scr[h] = alpha * acc_scr[h] + jnp.dot(v_aug, pr.astype(bf16), preferred_element_type=f32)

    def far(blk, _):
        for jj in range(per):
            step(blk * per + jj, None)
        return 0

    def near(block, first_chunk):
        for jj in range(per):
            step(first_chunk + jj, slice(block * TQ + jj * AK, block * TQ + (jj + 1) * AK))

    _blocked_loop(jnp.maximum(i - 1, 0), far, 0, FAR_UNROLL)

    @pl.when(i >= 1)
    def _():
        near(0, (i - 1) * per)
        near(1, i * per)

    @pl.when(i == 0)
    def _():
        near(1, i * per)
    for p in range(n_pairs):
        outs = [acc_scr[2 * p + sub] / acc_scr[2 * p + sub, denom_row[sub]:denom_row[sub] + 1, :]
                for sub in range(2)]
        o_pair = jnp.where(head0_q, outs[0], outs[1])
        o_ref[:, p * V7X_LANES:(p + 1) * V7X_LANES] = o_pair.T.astype(o_ref.dtype)


def _dsa(k, ik, qT, vT, iqT, iwT, rel_bias, B, S):
    T = k.shape[0]
    TQ = min(ATT_Q, S)
    nQ = S // TQ
    topk = min(TOPK_MAX, S // 4)
    buckets = _t5_bucket_table(2 * TQ + 1)
    assert np.all(_t5_bucket_table(S + 1)[TQ + 1:] == REL_BUCKETS - 1)
    j = np.arange(2 * TQ)[:, None]
    r = np.arange(TQ)[None, :]
    dist = np.maximum(r + TQ - j, 0)
    onehot = (jnp.asarray(buckets[dist], jnp.int32)[None]
              == jnp.arange(REL_BUCKETS, dtype=jnp.int32)[:, None, None]).astype(jnp.float32)
    rel = (rel_bias.astype(jnp.float32) - rel_bias[REL_BUCKETS - 1].astype(jnp.float32)[None, :]) * LOG2_E
    enear = jnp.einsum('nh,njr->hjr', rel, onehot, precision=lax.Precision.HIGHEST)

    return pl.pallas_call(
        functools.partial(_dsa_kernel, topk=topk),
        grid=(B, nQ),
        in_specs=[
            pl.BlockSpec((1, A_WIDTH, TQ), lambda b, i: (b, 0, i)),
            pl.BlockSpec((S, A_WIDTH), lambda b, i: (b, 0)),
            pl.BlockSpec((1, A_WIDTH, S), lambda b, i: (b, 0, 0)),
            pl.BlockSpec((1, IDX_WIDTH, TQ), lambda b, i: (b, 0, i)),
            pl.BlockSpec((1, IDX_HEADS, TQ), lambda b, i: (b, 0, i)),
            pl.BlockSpec((S, IDX_DIM), lambda b, i: (b, 0)),
            pl.BlockSpec((A_HEADS, 2 * TQ, TQ), lambda b, i: (0, 0, 0)),
        ],
        out_specs=pl.BlockSpec((TQ, A_WIDTH), lambda b, i: (b * nQ + i, 0)),
        scratch_shapes=[pltpu.VMEM((S, TQ), jnp.float32),
                        pltpu.VMEM((A_HEADS, V7X_LANES, TQ), jnp.bfloat16),
                        pltpu.VMEM((A_HEADS, TQ), jnp.float32),
                        pltpu.VMEM((A_HEADS, V7X_LANES, TQ), jnp.float32)],
        out_shape=jax.ShapeDtypeStruct((T, A_WIDTH), jnp.bfloat16),
        compiler_params=_cparams(("parallel", "arbitrary")),
        name="dsa",
    )(qT, k, vT, iqT, iwT, ik, enear)


def _hgrn_kernel(bq_ref, bf_ref, bi_ref, bg_ref, lb_ref, gain_ref, o_ref,
                 st_ref, b_scr, q_scr, k_scr, v_scr, oi_scr, qd_s, kl_s, vv_s, dec_s, oi_s, upd_s, st_s,
                 kdT_s, vvT_s):
    R = bq_ref.shape[0]
    C = HGRN_CHUNK
    nC = R // C
    f32 = jnp.float32
    bf16 = jnp.bfloat16
    h = pl.program_id(1)

    @pl.when(pl.program_id(2) == 0)
    def _():
        st_ref[...] = jnp.zeros_like(st_ref)

    lb = lb_ref[pl.ds(h, 1), :]
    gain = gain_ref[pl.ds(h, 1), :]
    tril_incl = jnp.where(lax.broadcasted_iota(jnp.int32, (C, C), 1)
                          <= lax.broadcasted_iota(jnp.int32, (C, C), 0), 1.0, 0.0)
    srow = lax.broadcasted_iota(jnp.int32, (C, B_KEY_DIM), 0)

    def gates(r0):
        f = lb + (1.0 - lb) * _sigmoid(bf_ref[pl.ds(r0, C), :])
        qr = bq_ref[pl.ds(r0, C), :]
        return jnp.log(f), 1.0 - f, qr * _sigmoid(qr) * (B_KEY_DIM ** -0.5), bi_ref[pl.ds(r0, C), :]

    def cumdecay(g):
        tri = tril_incl.astype(bf16)
        g_hi = g.astype(bf16)
        rest = g - g_hi.astype(f32)
        g_mid = rest.astype(bf16)
        g_lo = (rest - g_mid.astype(f32)).astype(bf16)
        return (jnp.dot(tri, g_hi, preferred_element_type=f32) + jnp.dot(tri, g_mid, preferred_element_type=f32)
                + jnp.dot(tri, g_lo, preferred_element_type=f32))

    def advance(r0, st, qd, o_intra, upd, decay_row):
        o_inter = lax.dot_general(qd, st.astype(bf16), (((1,), (1,)), ((), ())), preferred_element_type=f32)
        og = bg_ref[pl.ds(r0, C), :]
        y = _rms(o_inter + o_intra, gain) * (og * _sigmoid(og))
        o_ref[pl.ds(r0, C), :] = y.astype(o_ref.dtype)
        return st * decay_row + upd

    f_all = lb + (1.0 - lb) * _sigmoid(bf_ref[...])
    g_all = jnp.log(f_all)
    decay = jnp.sum(g_all.reshape(nC, C, B_KEY_DIM), axis=1)
    safe = jnp.min(decay) >= -HGRN_SAFE_DECAY

    @pl.when(safe)
    def _():
        qr = bq_ref[...]
        qq = qr * _sigmoid(qr) * (B_KEY_DIM ** -0.5)
        kk = 1.0 - f_all
        GR = HGRN_GROUP * C
        r_id = lax.broadcasted_iota(jnp.int32, (GR, GR), 0)
        c_id = lax.broadcasted_iota(jnp.int32, (GR, GR), 1)
        tri_group = jnp.where((r_id // C == c_id // C) & (c_id <= r_id), 1.0, 0.0)
        tri_group_bf = tri_group.astype(bf16)
        g_hi = g_all.astype(bf16)
        rest = g_all - g_hi.astype(f32)
        g_mid = rest.astype(bf16)
        g_lo = (rest - g_mid.astype(f32)).astype(bf16)
        g_cat = jnp.concatenate([g_hi, g_mid, g_lo], axis=1)
        b_parts = []
        for gi in range(R // GR):
            bc = jnp.dot(tri_group_bf, g_cat[gi * GR:(gi + 1) * GR], preferred_element_type=f32)
            b_parts.append(bc[:, :B_KEY_DIM] + bc[:, B_KEY_DIM:2 * B_KEY_DIM] + bc[:, 2 * B_KEY_DIM:])
        b = jnp.concatenate(b_parts, axis=0)
        b_end = jnp.concatenate([jnp.broadcast_to(b[(c + 1) * C - 1:(c + 1) * C], (C, B_KEY_DIM))
                                 for c in range(nC)], axis=0)
        qd_s[...] = (qq * jnp.exp(b)).astype(bf16)
        kd = kk * jnp.exp(-b)
        kl_s[...] = (kk * jnp.exp(b_end - b)).astype(bf16)
        vv = bi_ref[...]
        vv_s[...] = vv.astype(bf16)
        dec_s[...] = jnp.exp(b_end)
        for gi in range(R // GR):
            kdT_s[gi] = kd[gi * GR:(gi + 1) * GR].T.astype(bf16)
        for c in range(nC):
            vvT_s[c] = vv[c * C:(c + 1) * C].T.astype(bf16)
        for gi in range(R // GR):
            rows = slice(gi * GR, (gi + 1) * GR)
            att = jnp.dot(qd_s[rows], kdT_s[gi], preferred_element_type=f32) * tri_group
            oi_s[rows] = jnp.dot(att.astype(bf16), vv_s[rows], preferred_element_type=f32)
        for c in range(nC):
            rows = slice(c * C, (c + 1) * C)
            upd_s[c] = jnp.dot(vvT_s[c], kl_s[rows], preferred_element_type=f32)
        st = st_ref[...]
        for c in range(nC):
            st_s[c] = st.T.astype(bf16)
            st = st * dec_s[c * C:c * C + 1] + upd_s[c]
        st_ref[...] = st
        for c in range(nC):
            rows = slice(c * C, (c + 1) * C)
            oi_s[rows] = oi_s[rows] + jnp.dot(qd_s[rows], st_s[c], preferred_element_type=f32)
        og = bg_ref[...]
        o_ref[...] = (_rms(oi_s[...], gain) * (og * _sigmoid(og))).astype(o_ref.dtype)

    @pl.when(jnp.logical_not(safe))
    def _():
        def body(c, st):
            r0 = pl.multiple_of(c * C, C)
            g, kk, qq, vv = gates(r0)
            b = cumdecay(g)
            b_last = b[C - 1:C, :]
            b_scr[...] = b
            q_scr[...] = qq
            k_scr[...] = kk
            v_scr[...] = vv

            def row(t, _):
                bt = b_scr[pl.ds(t, 1), :]
                qt = q_scr[pl.ds(t, 1), :]
                ex = jnp.where(srow <= t, bt - b_scr[...], -jnp.inf)
                a = jnp.sum(qt * k_scr[...] * jnp.exp(ex), axis=1, keepdims=True)
                oi_scr[pl.ds(t, 1), :] = jnp.sum(a * v_scr[...], axis=0, keepdims=True)
                return 0
            lax.fori_loop(0, C, row, 0)
            kd_last = (kk * jnp.exp(b_last - b)).astype(bf16)
            upd = lax.dot_general(vv.astype(bf16), kd_last, (((0,), (0,)), ((), ())),
                                  preferred_element_type=f32)
            return advance(r0, st, (qq * jnp.exp(b)).astype(bf16), oi_scr[...], upd, jnp.exp(b_last))
        st_ref[...] = lax.fori_loop(0, nC, body, st_ref[...])


def _hgrn(bq, bf, bi, bg, lb, gain, B, S):
    T = bq.shape[0]
    R = min(HGRN_ROWS, S)
    nR = S // R
    C = HGRN_CHUNK
    blk = pl.BlockSpec((R, B_KEY_DIM), lambda b, h, c: (b * nR + c, h))
    small = pl.BlockSpec((B_HEADS, B_KEY_DIM), lambda b, h, c: (0, 0))
    f32 = jnp.float32
    return pl.pallas_call(
        _hgrn_kernel,
        grid=(B, B_HEADS, nR),
        in_specs=[blk, blk, blk, blk, small, small],
        out_specs=blk,
        out_shape=jax.ShapeDtypeStruct((T, B_WIDTH), jnp.bfloat16),
        scratch_shapes=[pltpu.VMEM((B_VAL_DIM, B_KEY_DIM), f32)] +
                       [pltpu.VMEM((C, B_KEY_DIM), f32) for _ in range(5)] +
                       [pltpu.VMEM((R, B_KEY_DIM), jnp.bfloat16) for _ in range(3)] +
                       [pltpu.VMEM((R, B_KEY_DIM), f32) for _ in range(2)] +
                       [pltpu.VMEM((R // C, B_VAL_DIM, B_KEY_DIM), f32),
                        pltpu.VMEM((R // C, B_KEY_DIM, B_VAL_DIM), jnp.bfloat16),
                        pltpu.VMEM((R // (HGRN_GROUP * C), B_KEY_DIM, HGRN_GROUP * C), jnp.bfloat16),
                        pltpu.VMEM((R // C, B_VAL_DIM, C), jnp.bfloat16)],
        compiler_params=_cparams(("parallel", "parallel", "arbitrary")),
        name="hgrn",
    )(bq, bf, bi, bg, lb, gain)


def _merge_kernel(x_ref, ya_ref, yb_ref, ga_ref, gb_ref, wa_ref, wb_ref, wo_ref, g_ref, wrh_ref, wrl_ref, br_ref,
                  x1_ref, xn_ref, lg_ref):
    f32 = jnp.float32
    ma = jnp.dot(ya_ref[...], wa_ref[...], preferred_element_type=f32)
    mb = jnp.dot(yb_ref[...], wb_ref[...], preferred_element_type=f32)
    merged = ga_ref[...].astype(f32) * ma + gb_ref[...].astype(f32) * mb
    x1 = x_ref[...] + jnp.dot(merged.astype(jnp.bfloat16), wo_ref[...], preferred_element_type=f32)
    x1_ref[...] = x1
    hn = _rms(x1, g_ref[...])
    xn_ref[...] = _pack_bf16_pairs(hn)
    hn_hi = hn.astype(jnp.bfloat16)
    hn_lo = (hn - hn_hi.astype(f32)).astype(jnp.bfloat16)
    lg_ref[...] = (jnp.dot(hn_hi, wrh_ref[...], preferred_element_type=f32)
                   + jnp.dot(hn_lo, wrh_ref[...], preferred_element_type=f32)
                   + jnp.dot(hn_hi, wrl_ref[...], preferred_element_type=f32) + br_ref[...])


def _merge(x2, ya, yb, ga, gb, w_up_a, w_up_b, w_out, gain, w_router, b_router):
    T, D = x2.shape
    R = min(PROJ_ROWS, T)
    bf = jnp.bfloat16
    wr_hi = w_router.astype(bf)
    wr_lo = (w_router - wr_hi.astype(jnp.float32)).astype(bf)
    ins = [x2, ya, yb, ga, gb, w_up_a.astype(bf), w_up_b.astype(bf), w_out.astype(bf),
           gain.reshape(1, D), wr_hi, wr_lo, b_router.reshape(1, N_EXPERTS)]
    row = lambda n: pl.BlockSpec((R, n), lambda i: (i, 0))
    full = lambda a: pl.BlockSpec(a.shape, lambda i: (0,) * a.ndim)
    in_specs = [row(D), row(A_WIDTH), row(B_WIDTH), row(D), row(D)] + [full(a) for a in ins[5:]]
    return pl.pallas_call(
        _merge_kernel,
        grid=(T // R,),
        in_specs=in_specs,
        out_specs=[row(D), row(D // 2), row(N_EXPERTS)],
        out_shape=[jax.ShapeDtypeStruct((T, D), jnp.float32), jax.ShapeDtypeStruct((T, D // 2), jnp.uint32),
                   jax.ShapeDtypeStruct((T, N_EXPERTS), jnp.float32)],
        compiler_params=_cparams(("parallel",)),
        name="merge",
    )(*ins)


def _route_kernel(lg_ref, eidx_ref, gate_ref, rank_ref, cnt_ref, run_ref):
    R = lg_ref.shape[0]
    f32 = jnp.float32

    @pl.when(pl.program_id(0) == 0)
    def _():
        run_ref[...] = jnp.zeros_like(run_ref)

    lg = lg_ref[...].T
    expert = lax.broadcasted_iota(jnp.int32, (N_EXPERTS, R), 0)
    work = lg
    onehots, vals, idxs = [], [], []
    for _ in range(TOP_K):
        m = jnp.max(work, axis=0, keepdims=True)
        idx = jnp.min(jnp.where(work == m, expert, N_EXPERTS), axis=0, keepdims=True)
        oh = expert == idx
        onehots.append(oh)
        vals.append(m)
        idxs.append(idx)
        work = jnp.where(oh, -jnp.inf, work)
    ex = [jnp.exp(v - vals[0]) for v in vals]
    den = ex[0] + ex[1] + ex[2] + ex[3]
    chosen = jnp.where(onehots[0] | onehots[1] | onehots[2] | onehots[3], 1.0, 0.0)
    earlier = jnp.where(lax.broadcasted_iota(jnp.int32, (R, R), 0)
                        < lax.broadcasted_iota(jnp.int32, (R, R), 1), 1.0, 0.0).astype(jnp.bfloat16)
    before = jnp.dot(chosen.astype(jnp.bfloat16), earlier, preferred_element_type=f32) + run_ref[...]
    slot = lax.broadcasted_iota(jnp.int32, (TOP_K, R), 0)
    eidx = jnp.zeros((TOP_K, R), jnp.int32)
    gate = jnp.zeros((TOP_K, R), f32)
    rank = jnp.zeros((TOP_K, R), f32)
    for k in range(TOP_K):
        eidx = jnp.where(slot == k, idxs[k], eidx)
        gate = jnp.where(slot == k, ex[k] / den, gate)
        rk = jnp.sum(jnp.where(onehots[k], before, 0.0), axis=0, keepdims=True)
        rank = jnp.where(slot == k, rk, rank)
    eidx_ref[...] = eidx
    gate_ref[...] = gate
    rank_ref[...] = rank.astype(jnp.int32)
    run_ref[...] = run_ref[...] + jnp.sum(chosen, axis=1, keepdims=True)
    cnt_ref[...] = run_ref[...].astype(jnp.int32)


def _route(logits):
    T = logits.shape[0]
    R = min(ROUTE_ROWS, T)
    col = pl.BlockSpec((TOP_K, R), lambda i: (0, i))
    return pl.pallas_call(
        _route_kernel,
        grid=(T // R,),
        in_specs=[pl.BlockSpec((R, N_EXPERTS), lambda i: (i, 0))],
        out_specs=[col, col, col, pl.BlockSpec((N_EXPERTS, 1), lambda i: (0, 0))],
        out_shape=[jax.ShapeDtypeStruct((TOP_K, T), jnp.int32), jax.ShapeDtypeStruct((TOP_K, T), jnp.float32),
                   jax.ShapeDtypeStruct((TOP_K, T), jnp.int32), jax.ShapeDtypeStruct((N_EXPERTS, 1), jnp.int32)],
        scratch_shapes=[pltpu.VMEM((N_EXPERTS, 1), jnp.float32)],
        compiler_params=_cparams(("arbitrary",)),
        name="route",
    )(logits)


def _sc_gather_rows(table, idx):
    W = table.shape[1]
    M = idx.shape[0]
    workers = V7X_SC_CORES * V7X_SC_SUBCORES
    per_worker = M // workers
    pieces = per_worker // SC_GATHER_ROWS
    assert per_worker * workers == M and pieces * SC_GATHER_ROWS == per_worker and pieces % 2 == 0
    mesh = plsc.VectorSubcoreMesh(core_axis_name="core", subcore_axis_name="subcore",
                                  num_cores=V7X_SC_CORES, num_subcores=V7X_SC_SUBCORES)

    @functools.partial(
        pl.kernel, mesh=mesh,
        out_type=jax.ShapeDtypeStruct((M, W), table.dtype),
        scratch_types=[pltpu.VMEM((per_worker,), jnp.int32),
                       pltpu.VMEM((SC_GATHER_ROWS, W), table.dtype),
                       pltpu.VMEM((SC_GATHER_ROWS, W), table.dtype),
                       pltpu.SemaphoreType.DMA, pltpu.SemaphoreType.DMA],
    )
    def gather(table_hbm, idx_hbm, out_hbm, idx_v, rows_a, rows_b, sem_a, sem_b):
        worker = lax.axis_index("subcore") * V7X_SC_CORES + lax.axis_index("core")
        base = pl.multiple_of(worker * per_worker, SC_GATHER_ROWS)
        pltpu.sync_copy(idx_hbm.at[pl.ds(base, per_worker)], idx_v)
        bufs = ((rows_a, sem_a), (rows_b, sem_b))

        def fetch(g, buf, sem):
            off = pl.multiple_of(g * SC_GATHER_ROWS, SC_GATHER_ROWS)
            return pltpu.make_async_copy(table_hbm.at[idx_v.at[pl.ds(off, SC_GATHER_ROWS)]], buf, sem)

        fetch(0, *bufs[0]).start()

        @pl.loop(0, pieces, step=2)
        def _(g0):
            for half in range(2):
                g = g0 + half
                buf, sem = bufs[half]
                fetch(g, buf, sem).wait()

                @pl.when(g + 1 < pieces)
                def _():
                    fetch(g + 1, *bufs[1 - half]).start()

                off = pl.multiple_of(g * SC_GATHER_ROWS, SC_GATHER_ROWS)
                pltpu.sync_copy(buf, out_hbm.at[pl.ds(base + off, SC_GATHER_ROWS)])

    return gather(table, idx)


def _sc_scatter_rows(rows, dest, n_out):
    T, W = rows.shape
    slots = dest.shape[0]
    workers = V7X_SC_CORES * V7X_SC_SUBCORES
    per_worker = T // workers
    pieces = per_worker // SC_GATHER_ROWS
    assert per_worker * workers == T and pieces * SC_GATHER_ROWS == per_worker and pieces % 2 == 0
    idx = dest.reshape(slots, workers, pieces, SC_GATHER_ROWS).transpose(1, 2, 0, 3)
    mesh = plsc.VectorSubcoreMesh(core_axis_name="core", subcore_axis_name="subcore",
                                  num_cores=V7X_SC_CORES, num_subcores=V7X_SC_SUBCORES)

    @functools.partial(
        pl.kernel, mesh=mesh,
        out_type=jax.ShapeDtypeStruct((n_out, W), rows.dtype),
        scratch_types=[pltpu.VMEM((pieces, slots, SC_GATHER_ROWS), jnp.int32),
                       pltpu.VMEM((SC_GATHER_ROWS, W), rows.dtype),
                       pltpu.VMEM((SC_GATHER_ROWS, W), rows.dtype),
                       pltpu.SemaphoreType.DMA, pltpu.SemaphoreType.DMA, pltpu.SemaphoreType.DMA],
    )
    def scatter(rows_hbm, idx_hbm, out_hbm, idx_v, rows_a, rows_b, sem_a, sem_b, sem_out):
        worker = lax.axis_index("subcore") * V7X_SC_CORES + lax.axis_index("core")
        base = pl.multiple_of(worker * per_worker, SC_GATHER_ROWS)
        pltpu.sync_copy(idx_hbm.at[worker], idx_v)
        bufs = ((rows_a, sem_a), (rows_b, sem_b))

        def fetch(g, buf, sem):
            off = pl.multiple_of(g * SC_GATHER_ROWS, SC_GATHER_ROWS)
            return pltpu.make_async_copy(rows_hbm.at[pl.ds(base + off, SC_GATHER_ROWS)], buf, sem)

        fetch(0, *bufs[0]).start()

        @pl.loop(0, pieces, step=2)
        def _(g0):
            for half in range(2):
                g = g0 + half
                buf, sem = bufs[half]
                fetch(g, buf, sem).wait()

                @pl.when(g + 1 < pieces)
                def _():
                    fetch(g + 1, *bufs[1 - half]).start()

                puts = [pltpu.make_async_copy(buf, out_hbm.at[idx_v.at[g, k]], sem_out) for k in range(slots)]
                for put in puts:
                    put.start()
                for put in puts:
                    put.wait()

    return scatter(rows, idx)


def _experts_kernel(be_ref, nb_ref, first_ref, slot_ref, next_ref,
                    x_ref, wgu_hbm, bgu_ref, wd_hbm, bd_ref, o_ref,
                    wgu_f32, wd_f32, wgu_bf, wd_bf, sems):
    f32 = jnp.float32
    d_ff = wd_bf.shape[0]
    i = pl.program_id(0)
    live = i < nb_ref[0]

    def weight_copies(e, s):
        return (pltpu.make_async_copy(wgu_hbm.at[e], wgu_f32.at[s], sems.at[s, 0]),
                pltpu.make_async_copy(wd_hbm.at[e], wd_f32.at[s], sems.at[s, 1]))

    @pl.when(live & (first_ref[i] == 1))
    def _():
        e = be_ref[i]
        s = slot_ref[i]

        @pl.when(i == 0)
        def _():
            for cp in weight_copies(e, s):
                cp.start()

        for cp in weight_copies(e, s):
            cp.wait()

        @pl.when(next_ref[i] >= 0)
        def _():
            for cp in weight_copies(next_ref[i], 1 - s):
                cp.start()

        wgu_bf[...] = wgu_f32[s].astype(jnp.bfloat16)
        wd_bf[...] = wd_f32[s].astype(jnp.bfloat16)

    @pl.when(live)
    def _():
        x_hi, x_lo = _unpack_bf16_pairs(x_ref[...])
        half = x_hi.shape[1]
        gu = (jnp.dot(x_hi, wgu_bf[:half, :], preferred_element_type=f32)
              + jnp.dot(x_lo, wgu_bf[half:, :], preferred_element_type=f32) + bgu_ref[0])
        gate = jnp.minimum(gu[:, :d_ff], SWIGLU_LIMIT)
        lin = jnp.clip(gu[:, d_ff:], -SWIGLU_LIMIT, SWIGLU_LIMIT)
        act = (lin + 1.0) * gate * _sigmoid(SWIGLU_ALPHA * gate)
        y = jnp.dot(act.astype(jnp.bfloat16), wd_bf[...], preferred_element_type=f32) + bd_ref[0]
        o_ref[...] = _pack_bf16_pairs(y)

    @pl.when(pl.program_id(0) >= nb_ref[0])
    def _():
        o_ref[...] = jnp.zeros_like(o_ref)


def _experts(xs, plan, w_gu, b_gu, w_down, b_down):
    P, W = xs.shape
    E, D, F2 = w_gu.shape
    nb = P // EXPERT_ROWS
    by_expert = lambda i, be, *_: (be[i], 0, 0)
    grid_spec = pltpu.PrefetchScalarGridSpec(
        num_scalar_prefetch=5,
        grid=(nb,),
        in_specs=[
            pl.BlockSpec((EXPERT_ROWS, W), lambda i, *_: (i, 0)),
            pl.BlockSpec(memory_space=pl.ANY),
            pl.BlockSpec((1, 1, F2), by_expert),
            pl.BlockSpec(memory_space=pl.ANY),
            pl.BlockSpec((1, 1, D), by_expert),
        ],
        out_specs=pl.BlockSpec((EXPERT_ROWS, W), lambda i, *_: (i, 0)),
        scratch_shapes=[pltpu.VMEM((2, D, F2), jnp.float32), pltpu.VMEM((2, F2 // 2, D), jnp.float32),
                        pltpu.VMEM((D, F2), jnp.bfloat16), pltpu.VMEM((F2 // 2, D), jnp.bfloat16),
                        pltpu.SemaphoreType.DMA((2, 2))],
    )
    return pl.pallas_call(
        _experts_kernel,
        grid_spec=grid_spec,
        out_shape=jax.ShapeDtypeStruct((P, W), jnp.uint32),
        compiler_params=_cparams(("arbitrary",)),
        name="experts",
    )(*plan, xs, w_gu, b_gu.reshape(E, 1, F2), w_down, b_down.reshape(E, 1, D))


def _combine_kernel(ya_ref, x1_ref, gate_ref, g_ref, o_ref):
    half = x1_ref.shape[1] // 2
    f32 = jnp.float32
    gate = gate_ref[...].T
    x1 = x1_ref[...]
    y_hi = x1[:, :half]
    y_lo = x1[:, half:]
    for k in range(TOP_K):
        hi, lo = _unpack_bf16_pairs(ya_ref[k])
        y_hi = y_hi + gate[:, k:k + 1] * hi.astype(f32)
        y_lo = y_lo + gate[:, k:k + 1] * lo.astype(f32)
    o_ref[...] = _rms(jnp.concatenate([y_hi, y_lo], axis=1), g_ref[...])


def _combine(ya, x1, gates, gain):
    T, D = x1.shape
    R = min(COMBINE_ROWS, T)
    row = lambda w: pl.BlockSpec((R, w), lambda i: (i, 0))
    return pl.pallas_call(
        _combine_kernel,
        grid=(T // R,),
        in_specs=[pl.BlockSpec((TOP_K, R, D // 2), lambda i: (0, i, 0)), row(D),
                  pl.BlockSpec((TOP_K, R), lambda i: (0, i)), pl.BlockSpec((1, D), lambda i: (0, 0))],
        out_specs=row(D),
        out_shape=jax.ShapeDtypeStruct((T, D), jnp.float32),
        compiler_params=_cparams(("parallel",)),
        name="combine",
    )(ya, x1, gates, gain.reshape(1, D))


def _moe_plan(eidx, rank, counts, A):
    counts = counts.reshape(N_EXPERTS)
    padded = (counts + EXPERT_ROWS - 1) // EXPERT_ROWS * EXPERT_ROWS
    pad_ends = jnp.cumsum(padded)
    pad_starts = pad_ends - padded
    n_blocks = -(-A // EXPERT_ROWS) + N_EXPERTS
    ids = jnp.arange(N_EXPERTS, dtype=jnp.int32)
    dest = rank + jnp.sum(jnp.where(eidx[None] == ids[:, None, None], pad_starts[:, None, None], 0), axis=0)
    block_start = jnp.arange(n_blocks, dtype=pad_ends.dtype) * EXPERT_ROWS
    block_expert = jnp.minimum(jnp.sum(pad_ends[None, :] <= block_start[:, None], axis=1),
                               N_EXPERTS - 1).astype(jnp.int32)
    n_used = (pad_ends[-1] // EXPERT_ROWS).astype(jnp.int32).reshape(1)
    has_rows = counts > 0
    ordinal = jnp.cumsum(has_rows.astype(jnp.int32)) - 1
    later = has_rows[None, :] & (ids[None, :] > ids[:, None])
    next_expert = jnp.where(jnp.any(later, axis=1), jnp.argmax(later, axis=1), -1).astype(jnp.int32)
    tables = jnp.stack([pad_starts, ordinal % 2, next_expert], axis=1).astype(jnp.float32)
    onehot = (block_expert[:, None] == ids[None, :]).astype(jnp.float32)
    looked = jnp.dot(onehot, tables, precision=lax.Precision.HIGHEST).astype(jnp.int32)
    block_first = ((block_start == looked[:, 0]) & (block_start < pad_ends[-1])).astype(jnp.int32)
    plan = (block_expert, n_used, block_first, looked[:, 1], looked[:, 2])
    return dest.astype(jnp.int32), plan, n_blocks


def kernel(x, w_in, w_up_a, w_up_b, w_out, norm_mix, norm_ffn, norm_final, hgrn_norm,
           lb_logits, rel_bias, w_router, b_router, w_gu, b_gu, w_down, b_down):
    B, S, D = x.shape
    T = B * S
    assert w_in.shape[0] == 1, "the final rmsnorm is fused into the single layer's combine stage"
    lb_all = jnp.cumsum(jax.nn.softmax(lb_logits.astype(jnp.float32), axis=0), axis=0)
    x2 = x.reshape(T, D)
    (k, ik, qT, vT, iqT, iwT, bq, bf, bi, bg, ga, gb) = _inproj(x2, norm_mix[0], w_in[0], B, S)
    ya = _dsa(k, ik, qT, vT, iqT, iwT, rel_bias, B, S)
    yb = _hgrn(bq, bf, bi, bg, lb_all[0].reshape(B_HEADS, B_KEY_DIM), hgrn_norm[0], B, S)
    x1, xn, logits = _merge(x2, ya, yb, ga, gb, w_up_a[0], w_up_b[0], w_out[0], norm_ffn[0],
                            w_router[0], b_router[0])
    eidx, gates, rank, counts = _route(logits)
    dest, plan, n_blocks = _moe_plan(eidx, rank, counts, T * TOP_K)
    P = n_blocks * EXPERT_ROWS
    A = T * TOP_K
    xs = _sc_scatter_rows(xn, dest, P)
    y_buf = _experts(xs, plan, w_gu[0], b_gu[0], w_down[0], b_down[0])
    ya = _sc_gather_rows(y_buf, dest.reshape(A)).reshape(TOP_K, T, D // 2)
    out = _combine(ya, x1, gates, norm_final)
    return out.reshape(B, S, D)
```
